```python
import jax
import jax.numpy as jnp
from jax import lax
import numpy as np

D_MODEL = 2048
BATCH = 32
SEQ = 256
DEPTH = 2
DEC_BATCH = 2
DEC_SEQ = 1024
PAST_LEN = 256

GRID_W = 64
QBLK = 128
EPS = 1e-6
ROPE_BASE = 10000.0
CONV_W = 5
SSD_HEADS = 16
SSD_HEAD_DIM = 64
SSD_D_INNER = SSD_HEADS * SSD_HEAD_DIM
SSD_GROUPS = 2
SSD_STATE = 128
SSD_CHUNK = 64
SSD_CONV_DIM = SSD_D_INNER + 2 * SSD_GROUPS * SSD_STATE
MLA_HEADS = 8
MLA_Q_RANK = 512
MLA_KV_RANK = 512
MLA_NOPE = 128
MLA_ROPE = 64
MLA_V = 128
GQA_HEADS = 16
GQA_KV_HEADS = 4
GQA_GROUP = GQA_HEADS // GQA_KV_HEADS
GQA_HEAD_DIM = 64
WINDOW = 128
ML_HEADS = 8
ML_HEAD_DIM = 128
ML_D = ML_HEADS * ML_HEAD_DIM
ML_CHUNK = 64
PEER_HEADS = 8
PEER_NKEYS = 128
PEER_N = PEER_NKEYS * PEER_NKEYS
PEER_QDIM = 256
PEER_TOPK = 16
PEER_TBLK = 128
L0_SIZES = (SSD_D_INNER, SSD_CONV_DIM, 2 * SSD_HEADS, MLA_Q_RANK, MLA_KV_RANK, MLA_ROPE)
L0_IN = sum(L0_SIZES)
L0_MIX = SSD_D_INNER + MLA_HEADS * MLA_V
L1_SIZES = (GQA_HEADS * GQA_HEAD_DIM, GQA_KV_HEADS * GQA_HEAD_DIM, GQA_KV_HEADS * GQA_HEAD_DIM,
            2 * ML_D, ML_D, ML_D, 2 * ML_HEADS, 2 * ML_HEADS)
L1_IN = sum(L1_SIZES)
L1_MIX = GQA_HEADS * GQA_HEAD_DIM + ML_D

kernel_name = 'hybrid_ssd_mla_swa_mlstm_peer_diffusion_step'


def rmsnorm(x, g):
    xf = x.astype(jnp.float32)
    y = xf * lax.rsqrt(jnp.mean(xf * xf, axis=-1, keepdims=True) + EPS)
    return (y * g.astype(jnp.float32)).astype(x.dtype)


def split_cols(x, sizes):
    return jnp.split(x, [int(i) for i in np.cumsum(sizes)[:-1]], axis=-1)


def flip(t):
    return jnp.flip(t, axis=1)


def grid_rope(L, rot_dim):
    rows = L // GRID_W
    r, col = jnp.meshgrid(jnp.arange(rows), jnp.arange(GRID_W), indexing='ij')
    r = r.reshape(-1).astype(jnp.float32)
    col = col.reshape(-1).astype(jnp.float32)
    nf = rot_dim // 4
    inv = ROPE_BASE ** (-jnp.arange(nf, dtype=jnp.float32) / nf)
    ang = jnp.concatenate([r[:, None] * inv, col[:, None] * inv], axis=-1)
    ang = jnp.concatenate([ang, ang], axis=-1)
    return jnp.cos(ang), jnp.sin(ang)


def apply_rope(x, cos, sin):
    half = x.shape[-1] // 2
    rot = jnp.concatenate([-x[..., half:], x[..., :half]], axis=-1)
    shape = (1, x.shape[1]) + (1,) * (x.ndim - 3) + (x.shape[-1],)
    return (x * cos.reshape(shape) + rot * sin.reshape(shape)).astype(x.dtype)


def dwconv(x, w, b):
    y = lax.conv_general_dilated(x, w[:, None, :].astype(x.dtype), window_strides=(1,),
                                 padding=[(CONV_W // 2, CONV_W // 2)],
                                 dimension_numbers=('NWC', 'WIO', 'NWC'),
                                 feature_group_count=x.shape[-1])
    return y + b.astype(x.dtype)


def attend(q, k, v, sink):
    b, Lq, KH, G, dq = q.shape
    Lk = k.shape[1]
    nb = Lq // QBLK
    scale = dq ** -0.5
    qb = jnp.moveaxis(q.reshape(b, nb, QBLK, KH, G, dq), 1, 0)

    def block(qi):
        s = jnp.einsum('bqhgd,bkhd->bhgqk', qi, k, preferred_element_type=jnp.float32) * scale
        if sink is not None:
            sk = jnp.broadcast_to(sink.astype(jnp.float32).reshape(1, KH, G, 1, 1), s.shape[:-1] + (1,))
            s = jnp.concatenate([s, sk], axis=-1)
        p = jax.nn.softmax(s, axis=-1)[..., :Lk].astype(v.dtype)
        return jnp.einsum('bhgqk,bkhd->bqhgd', p, v)

    o = lax.map(block, qb)
    return jnp.moveaxis(o, 0, 1).reshape(b, Lq, -1)


def window_attention(q, k, v, k_ctx, v_ctx, sink):
    b, L, KH, G, d = q.shape
    nb = L // QBLK
    span = QBLK + 2 * WINDOW
    kp = jnp.pad(k, ((0, 0), (WINDOW, WINDOW), (0, 0), (0, 0)))
    vp = jnp.pad(v, ((0, 0), (WINDOW, WINDOW), (0, 0), (0, 0)))
    idx = jnp.arange(nb)[:, None] * QBLK + jnp.arange(span)[None, :]
    kw = kp[:, idx]
    vw = vp[:, idx]
    qb = q.reshape(b, nb, QBLK, KH, G, d)
    qpos = jnp.arange(nb)[:, None] * QBLK + jnp.arange(QBLK)[None, :]
    kpos = idx - WINDOW
    valid = ((kpos[:, None, :] >= 0) & (kpos[:, None, :] < L)
             & (jnp.abs(qpos[:, :, None] - kpos[:, None, :]) <= WINDOW))
    scale = d ** -0.5
    s_loc = jnp.einsum('bnqhgd,bnkhd->bnhgqk', qb, kw, preferred_element_type=jnp.float32) * scale
    s_loc = jnp.where(valid[None, :, None, None, :, :], s_loc, -jnp.inf)
    s_ctx = jnp.einsum('bnqhgd,bkhd->bnhgqk', qb, k_ctx, preferred_element_type=jnp.float32) * scale
    s_sink = jnp.broadcast_to(sink.astype(jnp.float32).reshape(1, 1, KH, G, 1, 1), s_loc.shape[:-1] + (1,))
    p = jax.nn.softmax(jnp.concatenate([s_loc, s_ctx, s_sink], axis=-1), axis=-1).astype(v.dtype)
    Lc = k_ctx.shape[1]
    out = (jnp.einsum('bnhgqk,bnkhd->bnqhgd', p[..., :span], vw)
           + jnp.einsum('bnhgqk,bkhd->bnqhgd', p[..., span:span + Lc], v_ctx))
    return out.reshape(b, L, KH * G * d)


def ssd_scan(x, dt, A, Bm, Cm, h0):
    b, L, H, P = x.shape
    nc = L // SSD_CHUNK
    f32 = jnp.float32
    rep = H // SSD_GROUPS
    Bh = jnp.repeat(Bm.astype(f32), rep, axis=2)
    Ch = jnp.repeat(Cm.astype(f32), rep, axis=2)
    a = dt * A
    xd = x.astype(f32) * dt[..., None]

    def chunk(t):
        return jnp.moveaxis(t.reshape((b, nc, SSD_CHUNK) + t.shape[2:]), 1, 0)

    causal = jnp.tril(jnp.ones((SSD_CHUNK, SSD_CHUNK), dtype=bool))[None, :, :, None]

    def step(h, inp):
        ak, xk, Bk, Ck = inp
        cum = jnp.cumsum(ak, axis=1)
        seg = jnp.where(causal, cum[:, :, None, :] - cum[:, None, :, :], -jnp.inf)
        w = jnp.einsum('bqhn,bkhn->bqkh', Ck, Bk) * jnp.exp(seg)
        y = (jnp.einsum('bqkh,bkhp->bqhp', w, xk)
             + jnp.einsum('bqhn,bhpn->bqhp', Ck, h) * jnp.exp(cum)[..., None])
        h_new = (h * jnp.exp(cum[:, -1])[:, :, None, None]
                 + jnp.einsum('bkhn,bkh,bkhp->bhpn', Bk, jnp.exp(cum[:, -1:] - cum), xk))
        return h_new, y

    hT, y = lax.scan(step, h0.astype(f32), (chunk(a), chunk(xd), chunk(Bh), chunk(Ch)))
    y = jnp.moveaxis(y, 0, 1).reshape(b, L, H, P)
    return y.astype(x.dtype), hT.astype(h0.dtype)


def ssd_mixer(z, xbc, dt_raw, conv_w, conv_b, A_log, dt_bias, D_skip, norm_g, state0):
    b, L, _ = z.shape
    xbc = jax.nn.silu(dwconv(xbc, conv_w, conv_b))
    xs, Bm, Cm = split_cols(xbc, (SSD_D_INNER, SSD_GROUPS * SSD_STATE, SSD_GROUPS * SSD_STATE))
    xs = xs.reshape(b, L, SSD_HEADS, SSD_HEAD_DIM)
    Bm = Bm.reshape(b, L, SSD_GROUPS, SSD_STATE)
    Cm = Cm.reshape(b, L, SSD_GROUPS, SSD_STATE)
    dt = jax.nn.softplus(dt_raw.reshape(b, L, 2, SSD_HEADS).astype(jnp.float32) + dt_bias.astype(jnp.float32))
    A = -jnp.exp(A_log.astype(jnp.float32))
    y_f, s_f = ssd_scan(xs, dt[:, :, 0], A[0], Bm, Cm, state0[:, 0])
    y_b, s_b = ssd_scan(flip(xs), flip(dt[:, :, 1]), A[1], flip(Bm), flip(Cm), state0[:, 1])
    y = y_f + flip(y_b) + D_skip[:, None] * xs
    y = rmsnorm(y.reshape(b, L, SSD_D_INNER) * jax.nn.silu(z), norm_g)
    return y, jnp.stack([s_f, s_b], axis=1)


def mlstm_scan(q, k, v, li, lf, C0, n0, m0):
    b, L, H, d = q.shape
    nc = L // ML_CHUNK
    f32 = jnp.float32

    def chunk(t):
        return jnp.moveaxis(t.astype(f32).reshape((b, nc, ML_CHUNK) + t.shape[2:]), 1, 0)

    causal = jnp.tril(jnp.ones((ML_CHUNK, ML_CHUNK), dtype=bool))[None, :, :, None]

    def step(carry, inp):
        C, n, m = carry
        qk, kk, vk, ik, fk = inp
        bc = jnp.cumsum(fk, axis=1)
        logD = jnp.where(causal, bc[:, :, None, :] - bc[:, None, :, :] + ik[:, None, :, :], -jnp.inf)
        log_inter = bc + m[:, None, :]
        m_out = jnp.maximum(log_inter, jnp.max(logD, axis=2))
        s = jnp.einsum('bqhd,bkhd->bqkh', qk, kk) * jnp.exp(logD - m_out[:, :, None, :])
        w_inter = jnp.exp(log_inter - m_out)
        num = (jnp.einsum('bqkh,bkhe->bqhe', s, vk)
               + w_inter[..., None] * jnp.einsum('bhed,bqhd->bqhe', C, qk))
        den = jnp.sum(s, axis=2) + w_inter * jnp.einsum('bhd,bqhd->bqh', n, qk)
        h = num / jnp.maximum(jnp.abs(den), jnp.exp(-m_out))[..., None]
        log_end_inter = bc[:, -1] + m
        log_end_intra = bc[:, -1:] - bc + ik
        m_new = jnp.maximum(log_end_inter, jnp.max(log_end_intra, axis=1))
        w_c = jnp.exp(log_end_inter - m_new)
        w_k = jnp.exp(log_end_intra - m_new[:, None])
        C_new = w_c[..., None, None] * C + jnp.einsum('bkh,bkhe,bkhd->bhed', w_k, vk, kk)
        n_new = w_c[..., None] * n + jnp.einsum('bkh,bkhd->bhd', w_k, kk)
        return (C_new, n_new, m_new), h

    (CT, nT, mT), h = lax.scan(step, (C0.astype(f32), n0.astype(f32), m0.astype(f32)),
                               (chunk(q), chunk(k), chunk(v), chunk(li), chunk(lf)))
    h = jnp.moveaxis(h, 0, 1).reshape(b, L, H, d)
    return h.astype(q.dtype), CT.astype(C0.dtype), nT.astype(n0.dtype), mT.astype(m0.dtype)


def mlstm_mixer(x_qk, v, o, ig, fg, conv_w, conv_b, ig_b, fg_b, norm_g, C0, n0, m0):
    b, L, _ = v.shape
    qk = jax.nn.silu(dwconv(x_qk, conv_w, conv_b))
    q = qk[..., :ML_D].reshape(b, L, ML_HEADS, ML_HEAD_DIM)
    k = qk[..., ML_D:].reshape(b, L, ML_HEADS, ML_HEAD_DIM) * (ML_HEAD_DIM ** -0.5)
    v = v.reshape(b, L, ML_HEADS, ML_HEAD_DIM)
    li = ig.reshape(b, L, 2, ML_HEADS).astype(jnp.float32) + ig_b.astype(jnp.float32)
    lf = jax.nn.log_sigmoid(fg.reshape(b, L, 2, ML_HEADS).astype(jnp.float32) + fg_b.astype(jnp.float32))
    h_f, C_f, n_f, m_f = mlstm_scan(q, k, v, li[:, :, 0], lf[:, :, 0], C0[:, 0], n0[:, 0], m0[:, 0])
    h_b, C_b, n_b, m_b = mlstm_scan(flip(q), flip(k), flip(v), flip(li[:, :, 1]), flip(lf[:, :, 1]),
                                    C0[:, 1], n0[:, 1], m0[:, 1])
    h = rmsnorm(h_f + flip(h_b), norm_g.reshape(ML_HEADS, ML_HEAD_DIM))
    y = h.reshape(b, L, ML_D) * jax.nn.sigmoid(o)
    return (y, jnp.stack([C_f, C_b], axis=1), jnp.stack([n_f, n_b], axis=1), jnp.stack([m_f, m_b], axis=1))


def mla_expand_kv(ckv, k_pe, w_kvb):
    b, Lk, _ = ckv.shape
    kv = (ckv @ w_kvb).reshape(b, Lk, MLA_HEADS, MLA_NOPE + MLA_V)
    k = jnp.concatenate([kv[..., :MLA_NOPE],
                         jnp.broadcast_to(k_pe[:, :, None, :], (b, Lk, MLA_HEADS, MLA_ROPE))], axis=-1)
    return k, kv[..., MLA_NOPE:]


def mixer_even(h, p, ctx, rope):
    b, L, _ = h.shape
    z, xbc, dt_raw, q_a, kv_a, k_pe = split_cols(h @ p['w_in'], L0_SIZES)
    ssd0 = jnp.zeros((b, 2, SSD_HEADS, SSD_HEAD_DIM, SSD_STATE), h.dtype) if ctx is None else ctx[0]
    y_ssd, ssd_state = ssd_mixer(z, xbc, dt_raw, p['ssd_conv_w'], p['ssd_conv_b'], p['ssd_A_log'],
                                 p['ssd_dt_bias'], p['ssd_D'], p['ssd_norm_g'], ssd0)
    q = (rmsnorm(q_a, p['mla_q_norm_g']) @ p['mla_w_qb']).reshape(b, L, MLA_HEADS, MLA_NOPE + MLA_ROPE)
    ckv = rmsnorm(kv_a, p['mla_kv_norm_g'])
    if ctx is None:
        k, v = mla_expand_kv(ckv, k_pe, p['mla_w_kvb'])
        new_ctx = (ssd_state, ckv, k_pe)
    else:
        q = jnp.concatenate([q[..., :MLA_NOPE], apply_rope(q[..., MLA_NOPE:], *rope)], axis=-1)
        k, v = mla_expand_kv(jnp.concatenate([ctx[1], ckv], axis=1),
                             jnp.concatenate([ctx[2], apply_rope(k_pe, *rope)], axis=1), p['mla_w_kvb'])
        new_ctx = None
    y_mla = attend(q[:, :, :, None, :], k, v, None)
    return jnp.concatenate([y_ssd, y_mla], axis=-1) @ p['w_out'], new_ctx


def mixer_odd(h, p, ctx, rope):
    b, L, _ = h.shape
    gq, gk, gv, m_qk, m_v, m_o, m_i, m_f = split_cols(h @ p['w_in'], L1_SIZES)
    gq = gq.reshape(b, L, GQA_KV_HEADS, GQA_GROUP, GQA_HEAD_DIM)
    gk = gk.reshape(b, L, GQA_KV_HEADS, GQA_HEAD_DIM)
    gv = gv.reshape(b, L, GQA_KV_HEADS, GQA_HEAD_DIM)
    if ctx is None:
        y_gqa = attend(gq, gk, gv, p['gqa_sink'])
        C0 = jnp.zeros((b, 2, ML_HEADS, ML_HEAD_DIM, ML_HEAD_DIM), h.dtype)
        n0 = jnp.zeros((b, 2, ML_HEADS, ML_HEAD_DIM), h.dtype)
        m0 = jnp.zeros((b, 2, ML_HEADS), h.dtype)
    else:
        y_gqa = window_attention(apply_rope(gq, *rope), apply_rope(gk, *rope), gv, ctx[0], ctx[1], p['gqa_sink'])
        C0, n0, m0 = ctx[2], ctx[3], ctx[4]
    y_ml, C, n, m = mlstm_mixer(m_qk, m_v, m_o, m_i, m_f, p['ml_conv_w'], p['ml_conv_b'], p['ml_ig_b'],
                                p['ml_fg_b'], p['ml_norm_g'], C0, n0, m0)
    new_ctx = (gk, gv, C, n, m) if ctx is None else None
    return jnp.concatenate([y_gqa, y_ml], axis=-1) @ p['w_out'], new_ctx


def peer(x, wq, keys, U, V):
    b, L, D = x.shape
    nblk = (b * L) // PEER_TBLK
    half = PEER_QDIM // 2

    def block(xb):
        q = (xb @ wq).reshape(PEER_TBLK, PEER_HEADS, 2, half)
        s = jnp.einsum('thcd,hckd->thck', q, keys, preferred_element_type=jnp.float32)
        s1, i1 = lax.top_k(s[:, :, 0], PEER_TOPK)
        s2, i2 = lax.top_k(s[:, :, 1], PEER_TOPK)
        cand = (s1[..., :, None] + s2[..., None, :]).reshape(PEER_TBLK, PEER_HEADS, PEER_TOPK * PEER_TOPK)
        cidx = (i1[..., :, None] * PEER_NKEYS + i2[..., None, :]).reshape(PEER_TBLK, PEER_HEADS, PEER_TOPK * PEER_TOPK)
        top_s, pos = lax.top_k(cand, PEER_TOPK)
        eidx = jnp.take_along_axis(cidx, pos, axis=-1)
        g = jax.nn.softmax(top_s, axis=-1)
        a = jax.nn.gelu(jnp.einsum('thkd,td->thk', U[eidx], xb, preferred_element_type=jnp.float32), approximate=False)
        w = (g * a).astype(x.dtype)
        return jnp.einsum('thk,thkd->td', w, V[eidx])

    y = lax.map(block, x.reshape(nblk, PEER_TBLK, D))
    return y.reshape(b, L, D)


def layer_fwd(layer, x, cond, p, ctx, rope):
    mod = jax.nn.silu(cond) @ p['ada_w'] + p['ada_b']
    sh1, sc1, g1, sh2, sc2, g2 = jnp.split(mod[:, None, :], 6, axis=-1)
    h = rmsnorm(x, p['norm1_g']) * (1 + sc1) + sh1
    mixer = mixer_even if layer % 2 == 0 else mixer_odd
    y, new_ctx = mixer(h, p, ctx, rope)
    x = x + g1 * y
    h = rmsnorm(x, p['norm2_g']) * (1 + sc2) + sh2
    x = x + g2 * peer(h, p['peer_wq'], p['peer_keys'], p['peer_u'], p['peer_v'])
    return x, new_ctx


def setup_inputs(seed: int = 0) -> dict:
    key = jax.random.key(seed)
    ks = iter(jax.random.split(key, 80))

    def nrm(shape, scale):
        return jax.random.normal(next(ks), shape, jnp.float32) * scale

    def gain(n):
        return 1.0 + nrm((n,), 0.02)

    def unif(shape, lo, hi):
        return jax.random.uniform(next(ks), shape, jnp.float32, lo, hi)

    inp = {}
    inp['x_prompt'] = nrm((BATCH, SEQ, D_MODEL), 1.0)
    inp['x_sample'] = nrm((DEC_BATCH, DEC_SEQ, D_MODEL), 1.0)
    inp['state_l0_ssd'] = nrm((DEC_BATCH, 2, SSD_HEADS, SSD_HEAD_DIM, SSD_STATE), 0.1)
    inp['cache_l0_mla_ckv'] = nrm((DEC_BATCH, PAST_LEN, MLA_KV_RANK), 1.0)
    inp['cache_l0_mla_kpe'] = nrm((DEC_BATCH, PAST_LEN, MLA_ROPE), 1.0)
    inp['cache_l1_gqa_k'] = nrm((DEC_BATCH, PAST_LEN, GQA_KV_HEADS, GQA_HEAD_DIM), 1.0)
    inp['cache_l1_gqa_v'] = nrm((DEC_BATCH, PAST_LEN, GQA_KV_HEADS, GQA_HEAD_DIM), 1.0)
    inp['state_l1_mlstm_C'] = nrm((DEC_BATCH, 2, ML_HEADS, ML_HEAD_DIM, ML_HEAD_DIM), 0.1)
    inp['state_l1_mlstm_n'] = nrm((DEC_BATCH, 2, ML_HEADS, ML_HEAD_DIM), 0.1)
    inp['state_l1_mlstm_m'] = nrm((DEC_BATCH, 2, ML_HEADS), 0.5)
    inp['c'] = nrm((DEC_BATCH, D_MODEL), 1.0)
    inp['c_ctx'] = nrm((D_MODEL,), 1.0)
    inp['final_norm_g'] = gain(D_MODEL)
    inp['l0_ada_w'] = nrm((D_MODEL, 6 * D_MODEL), 0.5 * D_MODEL ** -0.5)
    inp['l0_ada_b'] = nrm((6 * D_MODEL,), 0.02)
    inp['l0_norm1_g'] = gain(D_MODEL)
    inp['l0_norm2_g'] = gain(D_MODEL)
    inp['l0_w_in'] = nrm((D_MODEL, L0_IN), D_MODEL ** -0.5)
    inp['l0_ssd_conv_w'] = nrm((CONV_W, SSD_CONV_DIM), CONV_W ** -0.5)
    inp['l0_ssd_conv_b'] = nrm((SSD_CONV_DIM,), 0.02)
    inp['l0_ssd_A_log'] = jnp.log(unif((2, SSD_HEADS), 1.0, 16.0))
    dt0 = jnp.exp(unif((2, SSD_HEADS), float(np.log(1e-3)), float(np.log(1e-1))))
    inp['l0_ssd_dt_bias'] = dt0 + jnp.log(-jnp.expm1(-dt0))
    inp['l0_ssd_D'] = 1.0 + nrm((SSD_HEADS,), 0.1)
    inp['l0_ssd_norm_g'] = gain(SSD_D_INNER)
    inp['l0_mla_q_norm_g'] = gain(MLA_Q_RANK)
    inp['l0_mla_w_qb'] = nrm((MLA_Q_RANK, MLA_HEADS * (MLA_NOPE + MLA_ROPE)), MLA_Q_RANK ** -0.5)
    inp['l0_mla_kv_norm_g'] = gain(MLA_KV_RANK)
    inp['l0_mla_w_kvb'] = nrm((MLA_KV_RANK, MLA_HEADS * (MLA_NOPE + MLA_V)), MLA_KV_RANK ** -0.5)
    inp['l0_w_out'] = nrm((L0_MIX, D_MODEL), L0_MIX ** -0.5)
    inp['l0_peer_wq'] = nrm((D_MODEL, PEER_HEADS * PEER_QDIM), D_MODEL ** -0.5)
    inp['l0_peer_keys'] = nrm((PEER_HEADS, 2, PEER_NKEYS, PEER_QDIM // 2), (PEER_QDIM // 2) ** -0.5)
    inp['l0_peer_u'] = nrm((PEER_N, D_MODEL), D_MODEL ** -0.5)
    inp['l0_peer_v'] = nrm((PEER_N, D_MODEL), 0.5)
    inp['l1_ada_w'] = nrm((D_MODEL, 6 * D_MODEL), 0.5 * D_MODEL ** -0.5)
    inp['l1_ada_b'] = nrm((6 * D_MODEL,), 0.02)
    inp['l1_norm1_g'] = gain(D_MODEL)
    inp['l1_norm2_g'] = gain(D_MODEL)
    inp['l1_w_in'] = nrm((D_MODEL, L1_IN), D_MODEL ** -0.5)
    inp['l1_gqa_sink'] = nrm((GQA_HEADS,), 0.5)
    inp['l1_ml_conv_w'] = nrm((CONV_W, 2 * ML_D), CONV_W ** -0.5)
    inp['l1_ml_conv_b'] = nrm((2 * ML_D,), 0.02)
    inp['l1_ml_ig_b'] = nrm((2, ML_HEADS), 0.1)
    inp['l1_ml_fg_b'] = unif((2, ML_HEADS), 3.0, 6.0)
    inp['l1_ml_norm_g'] = gain(ML_D)
    inp['l1_w_out'] = nrm((L1_MIX, D_MODEL), L1_MIX ** -0.5)
    inp['l1_peer_wq'] = nrm((D_MODEL, PEER_HEADS * PEER_QDIM), D_MODEL ** -0.5)
    inp['l1_peer_keys'] = nrm((PEER_HEADS, 2, PEER_NKEYS, PEER_QDIM // 2), (PEER_QDIM // 2) ** -0.5)
    inp['l1_peer_u'] = nrm((PEER_N, D_MODEL), D_MODEL ** -0.5)
    inp['l1_peer_v'] = nrm((PEER_N, D_MODEL), 0.5)
    return inp


def reference(x_prompt, x_sample, state_l0_ssd, cache_l0_mla_ckv, cache_l0_mla_kpe, cache_l1_gqa_k,
              cache_l1_gqa_v, state_l1_mlstm_C, state_l1_mlstm_n, state_l1_mlstm_m, c, c_ctx, final_norm_g,
              l0_ada_w, l0_ada_b, l0_norm1_g, l0_norm2_g, l0_w_in, l0_ssd_conv_w, l0_ssd_conv_b, l0_ssd_A_log,
              l0_ssd_dt_bias, l0_ssd_D, l0_ssd_norm_g, l0_mla_q_norm_g, l0_mla_w_qb, l0_mla_kv_norm_g,
              l0_mla_w_kvb, l0_w_out, l0_peer_wq, l0_peer_keys, l0_peer_u, l0_peer_v,
              l1_ada_w, l1_ada_b, l1_norm1_g, l1_norm2_g, l1_w_in, l1_gqa_sink, l1_ml_conv_w, l1_ml_conv_b,
              l1_ml_ig_b, l1_ml_fg_b, l1_ml_norm_g, l1_w_out, l1_peer_wq, l1_peer_keys, l1_peer_u, l1_peer_v):
    layers = [
        dict(ada_w=l0_ada_w, ada_b=l0_ada_b, norm1_g=l0_norm1_g, norm2_g=l0_norm2_g, w_in=l0_w_in,
             ssd_conv_w=l0_ssd_conv_w, ssd_conv_b=l0_ssd_conv_b, ssd_A_log=l0_ssd_A_log,
             ssd_dt_bias=l0_ssd_dt_bias, ssd_D=l0_ssd_D, ssd_norm_g=l0_ssd_norm_g,
             mla_q_norm_g=l0_mla_q_norm_g, mla_w_qb=l0_mla_w_qb, mla_kv_norm_g=l0_mla_kv_norm_g,
             mla_w_kvb=l0_mla_w_kvb, w_out=l0_w_out, peer_wq=l0_peer_wq, peer_keys=l0_peer_keys,
             peer_u=l0_peer_u, peer_v=l0_peer_v),
        dict(ada_w=l1_ada_w, ada_b=l1_ada_b, norm1_g=l1_norm1_g, norm2_g=l1_norm2_g, w_in=l1_w_in,
             gqa_sink=l1_gqa_sink, ml_conv_w=l1_ml_conv_w, ml_conv_b=l1_ml_conv_b, ml_ig_b=l1_ml_ig_b,
             ml_fg_b=l1_ml_fg_b, ml_norm_g=l1_ml_norm_g, w_out=l1_w_out, peer_wq=l1_peer_wq,
             peer_keys=l1_peer_keys, peer_u=l1_peer_u, peer_v=l1_peer_v),
    ]
    caches = [
        (state_l0_ssd, cache_l0_mla_ckv, cache_l0_mla_kpe),
        (cache_l1_gqa_k, cache_l1_gqa_v, state_l1_mlstm_C, state_l1_mlstm_n, state_l1_mlstm_m),
    ]
    xp = x_prompt
    cond_ctx = c_ctx[None, :]
    new_state = []
    for layer in range(DEPTH):
        xp, ctx_t = layer_fwd(layer, xp, cond_ctx, layers[layer], None, None)
        new_state.append(ctx_t)
    y_prompt = rmsnorm(xp, final_norm_g)
    L = x_sample.shape[1]
    ropes = (grid_rope(L, MLA_ROPE), grid_rope(L, GQA_HEAD_DIM))
    xs = x_sample
    for layer in range(DEPTH):
        xs, _ = layer_fwd(layer, xs, c, layers[layer], caches[layer], ropes[layer % 2])
    y_sample = rmsnorm(xs, final_norm_g)
    (s_ssd, ckv, kpe), (gk, gv, mC, mn, mm) = new_state
    return (y_prompt, y_sample, s_ssd, ckv, kpe, gk, gv, mC, mn, mm)
```

```python
import functools

import jax
import jax.numpy as jnp
import numpy as np
from jax import lax
from jax.experimental import pallas as pl
from jax.experimental.pallas import tpu as pltpu

F32 = jnp.float32
BF16 = jnp.bfloat16
NEG_INF = float("-inf")

D_MODEL = 2048
EPS = 1e-6
ROPE_BASE = 10000.0
GRID_W = 64
CONV_W = 5
CHUNK = 64
SSD_HEADS = 16
SSD_HEAD_DIM = 64
SSD_D_INNER = 1024
SSD_STATE = 128
MLA_HEADS = 8
MLA_NOPE = 128
MLA_ROPE = 64
MLA_V = 128
MLA_RANK = 512
GQA_HEADS = 16
GQA_KV_HEADS = 4
GQA_HEAD_DIM = 64
WINDOW = 128
ML_HEADS = 8
ML_HEAD_DIM = 128
ML_D = 1024
PEER_HEADS = 8
PEER_NKEYS = 128
PEER_N = PEER_NKEYS * PEER_NKEYS
PEER_TOPK = 16

LANES = 128
SUBLANES = 8
VMEM_LIMIT = 56 * 2**20

L0_COLS = 3840
L1_COLS = 5760


def _cp(n_grid):
    return pltpu.CompilerParams(dimension_semantics=("arbitrary",) * n_grid, vmem_limit_bytes=VMEM_LIMIT)


def _silu(x):
    return x * jax.nn.sigmoid(x)


def _softplus(x):
    return jnp.maximum(x, 0.0) + jnp.log1p(jnp.exp(-jnp.abs(x)))


def _rms(x):
    return x * lax.rsqrt(jnp.mean(x * x, axis=-1, keepdims=True) + EPS)


def _dot(a, b):
    return jnp.dot(a, b, preferred_element_type=F32)


def _dot_nt(a, b):
    return lax.dot_general(a, b, (((1,), (1,)), ((), ())), preferred_element_type=F32)


def _dot_tn(a, b):
    return lax.dot_general(a, b, (((0,), (0,)), ((), ())), preferred_element_type=F32)


def _dot_f32(a, b):
    return jnp.dot(a, b, preferred_element_type=F32, precision=lax.Precision.HIGHEST)


def _full(shape):
    nd = len(shape)
    return pl.BlockSpec(shape, lambda *_: (0,) * nd)


def _mod_spec(chunk, tb, mod_base, rows_per_cond):
    def index(i, *_):
        return (mod_base + (i * tb) // rows_per_cond, 0, chunk)
    return pl.BlockSpec((1, 1, D_MODEL), index)


def _ada_kernel(c_ref, w_ref, b_ref, o_ref):
    s = _silu(c_ref[...])
    o_ref[...] = _dot(s.astype(BF16), w_ref[...].astype(BF16)) + b_ref[...]


def _ada(cond8, w, b):
    n = w.shape[1]
    tn = 1024
    return pl.pallas_call(
        _ada_kernel,
        grid=(n // tn,),
        in_specs=[_full((8, D_MODEL)), pl.BlockSpec((D_MODEL, tn), lambda j: (0, j)),
                  pl.BlockSpec((1, tn), lambda j: (0, j))],
        out_specs=pl.BlockSpec((8, tn), lambda j: (0, j)),
        out_shape=jax.ShapeDtypeStruct((8, n), F32),
        compiler_params=_cp(1), name="ada_table",
    )(cond8, w, b.reshape(1, n))


def _proj_kernel(x_ref, g_ref, sc_ref, sh_ref, w_ref, o_ref, h_ref):
    @pl.when(pl.program_id(1) == 0)
    def _():
        h = _rms(x_ref[...]) * g_ref[...] * (1.0 + sc_ref[0]) + sh_ref[0]
        h_ref[...] = h.astype(BF16)
    o_ref[...] = _dot(h_ref[...], w_ref[...])


def _proj(x, g, mod, w, *, tn, mod_base, rows_per_cond):
    n, ncol = x.shape[0], w.shape[1]
    tb = 512
    return pl.pallas_call(
        _proj_kernel,
        grid=(n // tb, ncol // tn),
        in_specs=[pl.BlockSpec((tb, D_MODEL), lambda i, j: (i, 0)), _full((1, D_MODEL)),
                  _mod_spec(1, tb, mod_base, rows_per_cond), _mod_spec(0, tb, mod_base, rows_per_cond),
                  pl.BlockSpec((D_MODEL, tn), lambda i, j: (0, j))],
        out_specs=pl.BlockSpec((tb, tn), lambda i, j: (i, j)),
        out_shape=jax.ShapeDtypeStruct((n, ncol), F32),
        scratch_shapes=[pltpu.VMEM((tb, D_MODEL), BF16)],
        compiler_params=_cp(2), name="norm_mod_proj",
    )(x, g.reshape(1, D_MODEL), mod, mod, w)


def _outproj_kernel(ya_ref, yb_ref, w_ref, x_ref, g1_ref, o_ref):
    half = ya_ref.shape[1]
    y = _dot(ya_ref[...].astype(BF16), w_ref[0:half, :]) + _dot(yb_ref[...].astype(BF16), w_ref[half:, :])
    o_ref[...] = x_ref[...] + g1_ref[0] * y


def _outproj(ya, yb, w, x, mod, *, mod_base, rows_per_cond):
    n = x.shape[0]
    tb = 512
    return pl.pallas_call(
        _outproj_kernel,
        grid=(n // tb,),
        in_specs=[pl.BlockSpec((tb, ya.shape[1]), lambda i: (i, 0)), pl.BlockSpec((tb, yb.shape[1]), lambda i: (i, 0)),
                  _full(w.shape), pl.BlockSpec((tb, D_MODEL), lambda i: (i, 0)),
                  _mod_spec(2, tb, mod_base, rows_per_cond)],
        out_specs=pl.BlockSpec((tb, D_MODEL), lambda i: (i, 0)),
        out_shape=jax.ShapeDtypeStruct((n, D_MODEL), F32),
        compiler_params=_cp(1), name="out_proj_residual",
    )(ya, yb, w, x, mod)


ROUTE_TB = 256
_CAND_KEEP = (16, 8, 5, 4, 3, 2, 2, 2)


def _extract_max(s, iota, nrows):
    m = jnp.max(s, axis=0, keepdims=True)
    first = jnp.min(jnp.where(s == m, iota, nrows), axis=0, keepdims=True)
    return m, jnp.where(iota == first, NEG_INF, s)


def _route_kernel(x_ref, g_ref, sc_ref, sh_ref, wqt_ref, keys_ref,
                  hb_ref, s1_ref, s2_ref, e2_ref, c1_ref, tau_ref, qt_s, top_s):
    tb = x_ref.shape[0]
    h = _rms(x_ref[...]) * g_ref[...] * (1.0 + sc_ref[0]) + sh_ref[0]
    hb = h.astype(BF16)
    hb_ref[...] = hb
    qt_s[...] = _dot_nt(wqt_ref[...], hb).astype(BF16)

    iota_k = lax.broadcasted_iota(jnp.int32, (PEER_NKEYS, tb), 0)
    iota_c = lax.broadcasted_iota(jnp.int32, (80, tb), 0)
    iota_8 = lax.broadcasted_iota(jnp.int32, (SUBLANES, tb), 0)

    def head(hd, _):
        scores = []
        for c in range(2):
            r0 = pl.multiple_of((hd * 2 + c) * PEER_NKEYS, PEER_NKEYS)
            s = _dot(keys_ref[hd * 2 + c], qt_s[pl.ds(r0, PEER_NKEYS), :])
            scores.append(s)

            def take(it, cur, c=c):
                m, cur = _extract_max(cur, iota_k, PEER_NKEYS)
                top_s[c, pl.ds(it, 1), :] = m
                return cur
            lax.fori_loop(0, PEER_TOPK, take, s)
        a1 = top_s[0]
        a2 = top_s[1]
        pieces = [a1[0:1] + a2, a1[1:2] + a2[0:8]]
        for i in range(2, 8):
            pieces.append(jnp.where(iota_8 < _CAND_KEEP[i], a1[i:i + 1] + a2[0:8], NEG_INF))
        pieces.append(a1[8:16] + a2[0:1])
        cand = jnp.concatenate(pieces, axis=0)

        def take16(it, carry):
            cur, _ = carry
            m, cur = _extract_max(cur, iota_c, 80)
            return cur, m
        _, tau = lax.fori_loop(0, PEER_TOPK, take16, (cand, jnp.zeros((1, tb), F32)))
        top = a1[0:1] + a2[0:1]
        z = jnp.sum(jnp.where(cand >= tau, jnp.exp(cand - top), 0.0), axis=0, keepdims=True)
        s1_ref[hd] = scores[0]
        s2_ref[hd] = scores[1]
        e2_ref[hd] = jnp.exp(scores[1] - a2[0:1])
        c1_ref[hd] = jnp.exp(scores[0] - a1[0:1]) / z
        tau_ref[pl.ds(hd, 1), :] = tau
        return 0

    lax.fori_loop(0, PEER_HEADS, head, 0)


def _route(x, g, mod, wqt, keys, *, mod_base, rows_per_cond):
    n = x.shape[0]
    tb = ROUTE_TB
    hk = (PEER_HEADS, PEER_NKEYS, n)
    rspec = pl.BlockSpec((PEER_HEADS, PEER_NKEYS, tb), lambda i: (0, 0, i))
    return pl.pallas_call(
        _route_kernel,
        grid=(n // tb,),
        in_specs=[pl.BlockSpec((tb, D_MODEL), lambda i: (i, 0)), _full((1, D_MODEL)),
                  _mod_spec(4, tb, mod_base, rows_per_cond), _mod_spec(3, tb, mod_base, rows_per_cond),
                  _full(wqt.shape), _full(keys.shape)],
        out_specs=[pl.BlockSpec((tb, D_MODEL), lambda i: (i, 0)), rspec, rspec, rspec, rspec,
                   pl.BlockSpec((PEER_HEADS, tb), lambda i: (0, i))],
        out_shape=[jax.ShapeDtypeStruct((n, D_MODEL), BF16)] + [jax.ShapeDtypeStruct(hk, F32)] * 4
        + [jax.ShapeDtypeStruct((PEER_HEADS, n), F32)],
        scratch_shapes=[pltpu.VMEM((PEER_HEADS * 2 * PEER_NKEYS, tb), BF16), pltpu.VMEM((2, PEER_TOPK, tb), F32)],
        compiler_params=_cp(1), name="peer_route",
    )(x, g.reshape(1, D_MODEL), mod, mod, wqt, keys)


PEER_TB = 512
PEER_EC = 512


def _experts_kernel(hb_ref, s1_ref, s2_ref, e2_ref, c1_ref, tau_ref, u_ref, vt_ref, x_ref, g2_ref, fg_ref,
                    o_ref, acc_s, st_s, a_s, *, final_norm):
    e = pl.program_id(1)
    tb = hb_ref.shape[0]
    ntg = tb // LANES
    rows_per_step = PEER_EC // PEER_NKEYS

    @pl.when(e == 0)
    def _():
        acc_s[...] = jnp.zeros_like(acc_s)

    st_s[...] = _dot_nt(u_ref[...], hb_ref[...])

    groups = SUBLANES // rows_per_step
    g8 = pl.multiple_of((e // groups) * SUBLANES, SUBLANES)
    part = e % groups

    def pick(rows8, k):
        row = rows8[k:k + 1]
        for m in range(1, groups):
            row = jnp.where(part == m, rows8[m * rows_per_step + k:m * rows_per_step + k + 1], row)
        return row

    def tile(tg, _):
        c0 = pl.multiple_of(tg * LANES, LANES)
        s1g = [s1_ref[hd, pl.ds(g8, SUBLANES), pl.ds(c0, LANES)] for hd in range(PEER_HEADS)]
        c1g = [c1_ref[hd, pl.ds(g8, SUBLANES), pl.ds(c0, LANES)] for hd in range(PEER_HEADS)]
        taus = tau_ref[:, pl.ds(c0, LANES)]
        for k in range(rows_per_step):
            w = jnp.zeros((PEER_NKEYS, LANES), F32)
            for hd in range(PEER_HEADS):
                val = pick(s1g[hd], k) + s2_ref[hd, :, pl.ds(c0, LANES)]
                sel = jnp.where(val >= taus[hd:hd + 1], e2_ref[hd, :, pl.ds(c0, LANES)], 0.0)
                w = w + sel * pick(c1g[hd], k)
            s = st_s[k * PEER_NKEYS:(k + 1) * PEER_NKEYS, pl.ds(c0, LANES)]
            act = 0.5 * s * (1.0 + lax.erf(s * np.float32(2.0 ** -0.5)))
            a_s[k * PEER_NKEYS:(k + 1) * PEER_NKEYS, pl.ds(c0, LANES)] = (act * w).astype(BF16)
        return 0

    lax.fori_loop(0, ntg, tile, 0)
    acc_s[...] += _dot(vt_ref[...], a_s[...])

    @pl.when(e == pl.num_programs(1) - 1)
    def _():
        y = x_ref[...] + g2_ref[0] * acc_s[...].T
        if final_norm:
            y = _rms(y) * fg_ref[...]
        o_ref[...] = y


def _experts(hb, s1, s2, e2, c1, tau, u, vt, x, mod, fg, *, mod_base, rows_per_cond, final_norm):
    n = x.shape[0]
    tb = PEER_TB
    rspec = pl.BlockSpec((PEER_HEADS, PEER_NKEYS, tb), lambda i, e: (0, 0, i))
    return pl.pallas_call(
        functools.partial(_experts_kernel, final_norm=final_norm),
        grid=(n // tb, PEER_N // PEER_EC),
        in_specs=[pl.BlockSpec((tb, D_MODEL), lambda i, e: (i, 0)), rspec, rspec, rspec, rspec,
                  pl.BlockSpec((PEER_HEADS, tb), lambda i, e: (0, i)),
                  pl.BlockSpec((PEER_EC, D_MODEL), lambda i, e: (e, 0)),
                  pl.BlockSpec((D_MODEL, PEER_EC), lambda i, e: (0, e)),
                  pl.BlockSpec((tb, D_MODEL), lambda i, e: (i, 0)),
                  _mod_spec(5, tb, mod_base, rows_per_cond), _full((1, D_MODEL))],
        out_specs=pl.BlockSpec((tb, D_MODEL), lambda i, e: (i, 0)),
        out_shape=jax.ShapeDtypeStruct((n, D_MODEL), F32),
        scratch_shapes=[pltpu.VMEM((D_MODEL, tb), F32), pltpu.VMEM((PEER_EC, tb), F32),
                        pltpu.VMEM((PEER_EC, tb), BF16)],
        compiler_params=_cp(2), name="peer_experts",
    )(hb, s1, s2, e2, c1, tau, u, vt, x, mod, fg.reshape(1, D_MODEL))


def _peer(x, g, mod, wqt, keys, u, vt, fg, *, mod_base, rows_per_cond, final_norm):
    hb, s1, s2, e2, c1, tau = _route(x, g, mod, wqt, keys, mod_base=mod_base, rows_per_cond=rows_per_cond)
    return _experts(hb, s1, s2, e2, c1, tau, u, vt, x, mod, fg, mod_base=mod_base, rows_per_cond=rows_per_cond,
                    final_norm=final_norm)


def _chunk_index(s, nc):
    return jnp.where(s < nc, s, 2 * nc - 1 - s)


def _scan_specs(nc, nrow8):
    def cur(col):
        return lambda b, s: (b * nc + _chunk_index(s, nc), col)

    def prev(col):
        return lambda b, s: (jnp.maximum((b * nc + _chunk_index(s, nc)) * 8 - 1, 0), col)

    def nxt(col):
        return lambda b, s: (jnp.minimum((b * nc + _chunk_index(s, nc)) * 8 + 8, nrow8 - 1), col)

    def out(col):
        return lambda b, s: (b * nc + jnp.where(s < nc, nc - 1, 2 * nc - 1 - s), col)
    return cur, prev, nxt, out


def _conv_silu(prev, cur, nxt, w_ref, b_ref, first, last):
    win = jnp.concatenate([jnp.where(first, 0.0, prev), cur, jnp.where(last, 0.0, nxt)], axis=0)
    acc = b_ref[...] + win[6:70, :] * w_ref[0:1, :]
    for k in range(1, CONV_W):
        acc = acc + win[6 + k:70 + k, :] * w_ref[k:k + 1, :]
    return _silu(acc)


def _tri(bwd):
    r = lax.broadcasted_iota(jnp.int32, (CHUNK, CHUNK), 0)
    c = lax.broadcasted_iota(jnp.int32, (CHUNK, CHUNK), 1)
    lower = (c <= r).astype(F32)
    upper = (c >= r).astype(F32)
    return jnp.where(bwd, upper, lower), jnp.where(bwd, lower, upper)


def _ssd_kernel(z_ref, xp_ref, xc_ref, xn_ref, bp_ref, bc_ref, bn_ref, sm_ref, dtt_ref,
                cwx_ref, cbx_ref, cwb_ref, cbb_ref, arow_ref, acol_ref, brow_ref, bcol_ref, drow_ref, ng_ref,
                st0_ref, y_ref, st_ref, yf_s, h_s, *, nc):
    s = pl.program_id(1)
    bwd = s >= nc
    cidx = _chunk_index(s, nc)
    first = cidx == 0
    last = cidx == nc - 1
    nh = SSD_HEADS

    @pl.when(s == 0)
    def _():
        h_s[...] = st0_ref[0, 0]

    @pl.when(s == nc)
    def _():
        st_ref[0, 0] = h_s[...]
        h_s[...] = st0_ref[0, 1]

    xs = _conv_silu(xp_ref[...], xc_ref[...], xn_ref[...], cwx_ref, cbx_ref, first, last)
    bcv = _conv_silu(bp_ref[...], bc_ref[...], bn_ref[...], cwb_ref, cbb_ref, first, last)

    dt_all = _softplus(sm_ref[:, 64:96] + brow_ref[...])
    dtt_all = _softplus(dtt_ref[0] + bcol_ref[...])
    a_all = dt_all * (-jnp.exp(arow_ref[...]))
    at_all = dtt_all * (-jnp.exp(acol_ref[...]))
    dt = jnp.where(bwd, dt_all[:, nh:2 * nh], dt_all[:, 0:nh])
    a = jnp.where(bwd, a_all[:, nh:2 * nh], a_all[:, 0:nh])
    at = jnp.where(bwd, at_all[nh:2 * nh, :], at_all[0:nh, :])
    tsel, tsel_t = _tri(bwd)
    cum = _dot_f32(tsel, a)
    cum_t = _dot_f32(at, tsel_t)
    tot = jnp.where(bwd, cum[0:1, :], cum[CHUNK - 1:CHUNK, :])
    mask = tsel > 0.0

    ys = []
    for g in range(2):
        bg = bcv[:, g * SSD_STATE:(g + 1) * SSD_STATE]
        cg = bcv[:, 256 + g * SSD_STATE:256 + (g + 1) * SSD_STATE]
        cg16 = cg.astype(BF16)
        cb = _dot_nt(cg16, bg.astype(BF16))
        for hh in range(nh // 2):
            h = g * (nh // 2) + hh
            cq = cum[:, h:h + 1]
            decay = jnp.exp(jnp.where(mask, cq - cum_t[h:h + 1, :], NEG_INF))
            xd = xs[:, h * SSD_HEAD_DIM:(h + 1) * SSD_HEAD_DIM] * dt[:, h:h + 1]
            xd16 = xd.astype(BF16)
            hprev = h_s[h]
            y = _dot((cb * decay).astype(BF16), xd16) + _dot_nt(cg16, hprev.astype(BF16)) * jnp.exp(cq)
            ys.append(y)
            th = tot[:, h:h + 1]
            bw = bg * jnp.exp(th - cq)
            h_s[h] = hprev * jnp.exp(th) + _dot_tn(xd16, bw.astype(BF16))
    y = jnp.concatenate(ys, axis=1)
    r0 = pl.multiple_of(cidx * CHUNK, CHUNK)

    @pl.when(jnp.logical_not(bwd))
    def _():
        yf_s[pl.ds(r0, CHUNK), :] = y

    @pl.when(bwd)
    def _():
        yt = (yf_s[pl.ds(r0, CHUNK), :] + y + drow_ref[...] * xs) * _silu(z_ref[...])
        y_ref[...] = _rms(yt) * ng_ref[...]

    @pl.when(s == 2 * nc - 1)
    def _():
        st_ref[0, 1] = h_s[...]


def _ssd(p0, dtt, cwx, cbx, cwb, cbb, a_log, dt_bias, d_skip, norm_g, st0, *, L):
    n = p0.shape[0]
    nseq, nc = n // L, L // CHUNK
    cur, prev, nxt, out = _scan_specs(nc, n // 8)
    per_seq = st0.shape[0] > 1
    st_shape = (1, 2, SSD_HEADS, SSD_HEAD_DIM, SSD_STATE)
    row = lambda v: v.reshape(1, -1)
    col = lambda v: v.reshape(-1, 1)
    small = [cwx, row(cbx), cwb, row(cbb), row(a_log), col(a_log), row(dt_bias), col(dt_bias),
             row(jnp.repeat(d_skip, SSD_HEAD_DIM)), row(norm_g)]
    return pl.pallas_call(
        functools.partial(_ssd_kernel, nc=nc),
        grid=(nseq, 2 * nc),
        in_specs=[pl.BlockSpec((CHUNK, 1024), cur(0)),
                  pl.BlockSpec((8, 1024), prev(1)), pl.BlockSpec((CHUNK, 1024), cur(1)), pl.BlockSpec((8, 1024), nxt(1)),
                  pl.BlockSpec((8, 512), prev(4)), pl.BlockSpec((CHUNK, 512), cur(4)), pl.BlockSpec((8, 512), nxt(4)),
                  pl.BlockSpec((CHUNK, LANES), cur(28)),
                  pl.BlockSpec((1, 32, CHUNK), lambda b, s: (b * nc + _chunk_index(s, nc), 0, 0))]
        + [_full(v.shape) for v in small]
        + [pl.BlockSpec(st_shape, lambda b, s: (b if per_seq else 0, 0, 0, 0, 0))],
        out_specs=[pl.BlockSpec((CHUNK, SSD_D_INNER), out(0)),
                   pl.BlockSpec(st_shape, lambda b, s: (b, 0, 0, 0, 0))],
        out_shape=[jax.ShapeDtypeStruct((n, SSD_D_INNER), F32),
                   jax.ShapeDtypeStruct((nseq,) + st_shape[1:], F32)],
        scratch_shapes=[pltpu.VMEM((L, SSD_D_INNER), F32), pltpu.VMEM(st_shape[2:], F32)],
        compiler_params=_cp(2), name="ssd_mixer",
    )(p0, p0, p0, p0, p0, p0, p0, p0, dtt, *small, st0)


def _rope_tables(L):
    t = jnp.arange(L)
    r = (t // GRID_W).astype(F32)
    c = (t % GRID_W).astype(F32)
    nf = 16
    inv = ROPE_BASE ** (-jnp.arange(nf, dtype=F32) / nf)
    ang = jnp.concatenate([r[:, None] * inv, c[:, None] * inv], axis=-1)
    ang = jnp.concatenate([ang, ang], axis=-1)
    return jnp.cos(ang), jnp.sin(ang)


def _rope128(x, cos, sin):
    lane = lax.broadcasted_iota(jnp.int32, x.shape, 1)
    rot = jnp.where(lane % 64 < 32, -pltpu.roll(x, 96, 1), pltpu.roll(x, 32, 1))
    return x * cos + rot * sin


def _pad_lanes(x):
    return jnp.concatenate([x, jnp.zeros_like(x)], axis=1)


MLA_QB = 256


def _mla_kernel(*refs, L, Lc, rope, emit):
    refs = list(refs)
    qa_ref, kva_ref, sm_ref, qg_ref, kvg_ref, wqn_ref, wqr_ref, wkk_ref, wkv_ref = refs[:9]
    pos = 9
    if Lc:
        cckv_ref, ckpe_ref = refs[pos:pos + 2]
        pos += 2
    if rope:
        cos_ref, sin_ref = refs[pos:pos + 2]
        pos += 2
    y_ref = refs[pos]
    pos += 1
    if emit:
        ckv_ref = refs[pos]
        pos += 1
    qn_s, qr_s, kn_s, v_s, kpe_s = refs[pos:]
    qb = pl.program_id(1)
    scale = np.float32((MLA_NOPE + MLA_ROPE) ** -0.5)

    @pl.when(qb == 0)
    def _():
        ckv = _rms(kva_ref[...]) * kvg_ref[...]
        if emit:
            ckv_ref[...] = ckv
        kpe = _pad_lanes(sm_ref[:, 0:MLA_ROPE])
        if rope:
            kpe = _rope128(kpe, cos_ref[...], sin_ref[...])
        if Lc:
            ckv = jnp.concatenate([cckv_ref[0], ckv], axis=0)
            kpe = jnp.concatenate([_pad_lanes(ckpe_ref[0]), kpe], axis=0)
        c16 = ckv.astype(BF16)
        kn_s[...] = _dot(c16, wkk_ref[...]).astype(BF16)
        v_s[...] = _dot(c16, wkv_ref[...]).astype(BF16)
        kpe_s[...] = kpe.astype(BF16)

    qn = (_rms(qa_ref[...]) * qg_ref[...]).astype(BF16)
    qn_s[...] = _dot(qn, wqn_ref[...]).astype(BF16)
    qr = _dot(qn, wqr_ref[...])
    if rope:
        q0 = pl.multiple_of(qb * MLA_QB, MLA_QB)
        cos = cos_ref[pl.ds(q0, MLA_QB), :]
        sin = sin_ref[pl.ds(q0, MLA_QB), :]
        for h in range(MLA_HEADS):
            qr_s[:, h * LANES:(h + 1) * LANES] = _rope128(qr[:, h * LANES:(h + 1) * LANES], cos, sin).astype(BF16)
    else:
        qr_s[...] = qr.astype(BF16)

    def head(h, _):
        c0 = pl.multiple_of(h * LANES, LANES)
        s = (_dot_nt(qn_s[:, pl.ds(c0, LANES)], kn_s[:, pl.ds(c0, LANES)])
             + _dot_nt(qr_s[:, pl.ds(c0, LANES)], kpe_s[...])) * scale
        p = jnp.exp(s - jnp.max(s, axis=-1, keepdims=True))
        p = p / jnp.sum(p, axis=-1, keepdims=True)
        y_ref[:, pl.ds(c0, LANES)] = _dot(p.astype(BF16), v_s[:, pl.ds(c0, LANES)])
        return 0

    lax.fori_loop(0, MLA_HEADS, head, 0)


def _mla(p0, qg, kvg, wqn, wqr, wkk, wkv, cache_ckv, cache_kpe, rope_tabs, *, L):
    n = p0.shape[0]
    nseq, nqb = n // L, L // MLA_QB
    Lc = 0 if cache_ckv is None else cache_ckv.shape[1]
    rope = rope_tabs is not None
    emit = cache_ckv is None
    Lk = L + Lc
    args = [p0, p0, p0, qg.reshape(1, -1), kvg.reshape(1, -1), wqn, wqr, wkk, wkv]
    specs = [pl.BlockSpec((MLA_QB, MLA_RANK), lambda b, q: (b * nqb + q, 5)),
             pl.BlockSpec((L, MLA_RANK), lambda b, q: (b, 6)),
             pl.BlockSpec((L, LANES), lambda b, q: (b, 28)),
             _full((1, MLA_RANK)), _full((1, MLA_RANK)), _full(wqn.shape), _full(wqr.shape), _full(wkk.shape),
             _full(wkv.shape)]
    if Lc:
        args += [cache_ckv, cache_kpe]
        specs += [pl.BlockSpec((1, Lc, MLA_RANK), lambda b, q: (b, 0, 0)),
                  pl.BlockSpec((1, Lc, MLA_ROPE), lambda b, q: (b, 0, 0))]
    if rope:
        args += list(rope_tabs)
        specs += [_full((L, LANES)), _full((L, LANES))]
    out_specs = [pl.BlockSpec((MLA_QB, 1024), lambda b, q: (b * nqb + q, 0))]
    out_shape = [jax.ShapeDtypeStruct((n, 1024), F32)]
    if emit:
        out_specs.append(pl.BlockSpec((L, MLA_RANK), lambda b, q: (b, 0)))
        out_shape.append(jax.ShapeDtypeStruct((n, MLA_RANK), F32))
    return pl.pallas_call(
        functools.partial(_mla_kernel, L=L, Lc=Lc, rope=rope, emit=emit),
        grid=(nseq, nqb), in_specs=specs, out_specs=out_specs, out_shape=out_shape,
        scratch_shapes=[pltpu.VMEM((MLA_QB, 1024), BF16), pltpu.VMEM((MLA_QB, 1024), BF16),
                        pltpu.VMEM((Lk, 1024), BF16), pltpu.VMEM((Lk, 1024), BF16), pltpu.VMEM((Lk, LANES), BF16)],
        compiler_params=_cp(2), name="mla_attention",
    )(*args)


def _gqa_ctx_kernel(q_ref, k_ref, v_ref, sink_ref, y_ref):
    q = q_ref[...]
    k = k_ref[...].astype(BF16)
    v = v_ref[...].astype(BF16)
    scale = np.float32(GQA_HEAD_DIM ** -0.5)
    group = GQA_HEADS // GQA_KV_HEADS
    outs = []
    for h in range(GQA_HEADS):
        kh = h // group
        ks = slice(kh * GQA_HEAD_DIM, (kh + 1) * GQA_HEAD_DIM)
        s = _dot_nt(q[:, h * GQA_HEAD_DIM:(h + 1) * GQA_HEAD_DIM].astype(BF16), k[:, ks]) * scale
        m = jnp.maximum(jnp.max(s, axis=-1, keepdims=True), sink_ref[h])
        p = jnp.exp(s - m)
        p = p / (jnp.sum(p, axis=-1, keepdims=True) + jnp.exp(sink_ref[h] - m))
        outs.append(_dot(p.astype(BF16), v[:, ks]))
    y_ref[...] = jnp.concatenate(outs, axis=1)


def _gqa_ctx(p1, sink, *, L):
    n = p1.shape[0]
    return pl.pallas_call(
        _gqa_ctx_kernel,
        grid=(n // L,),
        in_specs=[pl.BlockSpec((L, 1024), lambda b: (b, 2)), pl.BlockSpec((L, 256), lambda b: (b, 20)),
                  pl.BlockSpec((L, 256), lambda b: (b, 21)), pl.BlockSpec(memory_space=pltpu.SMEM)],
        out_specs=pl.BlockSpec((L, 1024), lambda b: (b, 0)),
        out_shape=jax.ShapeDtypeStruct((n, 1024), F32),
        compiler_params=_cp(1), name="gqa_context",
    )(p1, p1, p1, sink)


GQA_QB = 128
GQA_SPAN = GQA_QB + 2 * WINDOW


def _gqa_win_kernel(q_ref, k_ref, v_ref, ck_ref, cv_ref, cos_ref, sin_ref, sink_ref, y_ref, kp_s, vp_s, *, L):
    qb = pl.program_id(1)
    scale = np.float32(GQA_HEAD_DIM ** -0.5)
    group = GQA_HEADS // GQA_KV_HEADS

    @pl.when(qb == 0)
    def _():
        zeros = jnp.zeros((WINDOW, 256), BF16)
        kp_s[0:WINDOW, :] = zeros
        kp_s[WINDOW + L:, :] = zeros
        vp_s[0:WINDOW, :] = zeros
        vp_s[WINDOW + L:, :] = zeros
        for j in range(2):
            cs = slice(j * LANES, (j + 1) * LANES)
            kp_s[WINDOW:WINDOW + L, cs] = _rope128(k_ref[:, cs], cos_ref[...], sin_ref[...]).astype(BF16)
        vp_s[WINDOW:WINDOW + L, :] = v_ref[...].astype(BF16)

    q0 = pl.multiple_of(qb * GQA_QB, GQA_QB)
    cos = cos_ref[pl.ds(q0, GQA_QB), :]
    sin = sin_ref[pl.ds(q0, GQA_QB), :]
    kw = kp_s[pl.ds(q0, GQA_SPAN), :]
    vw = vp_s[pl.ds(q0, GQA_SPAN), :]
    kc = ck_ref[0].astype(BF16)
    vc = cv_ref[0].astype(BF16)
    qpos = q0 + lax.broadcasted_iota(jnp.int32, (GQA_QB, GQA_SPAN), 0)
    kpos = q0 - WINDOW + lax.broadcasted_iota(jnp.int32, (GQA_QB, GQA_SPAN), 1)
    bias = jnp.where(kpos < 0, NEG_INF, jnp.where(kpos >= L, NEG_INF, jnp.where(jnp.abs(qpos - kpos) <= WINDOW, 0.0, NEG_INF)))
    outs = []
    for j in range(GQA_HEADS * GQA_HEAD_DIM // LANES):
        qj = _rope128(q_ref[:, j * LANES:(j + 1) * LANES], cos, sin).astype(BF16)
        for half in range(2):
            h = 2 * j + half
            kh = h // group
            ks = slice(kh * GQA_HEAD_DIM, (kh + 1) * GQA_HEAD_DIM)
            qh = qj[:, half * GQA_HEAD_DIM:(half + 1) * GQA_HEAD_DIM]
            s_loc = _dot_nt(qh, kw[:, ks]) * scale + bias
            s_ctx = _dot_nt(qh, kc[:, ks]) * scale
            m = jnp.maximum(jnp.maximum(jnp.max(s_loc, axis=-1, keepdims=True), jnp.max(s_ctx, axis=-1, keepdims=True)),
                            sink_ref[h])
            p_loc = jnp.exp(s_loc - m)
            p_ctx = jnp.exp(s_ctx - m)
            inv = 1.0 / (jnp.sum(p_loc, axis=-1, keepdims=True) + jnp.sum(p_ctx, axis=-1, keepdims=True)
                         + jnp.exp(sink_ref[h] - m))
            outs.append(_dot((p_loc * inv).astype(BF16), vw[:, ks]) + _dot((p_ctx * inv).astype(BF16), vc[:, ks]))
    y_ref[...] = jnp.concatenate(outs, axis=1)


def _gqa_win(p1, cache_k, cache_v, cos, sin, sink, *, L):
    n = p1.shape[0]
    nseq, nqb = n // L, L // GQA_QB
    Lc = cache_k.shape[1]
    return pl.pallas_call(
        functools.partial(_gqa_win_kernel, L=L),
        grid=(nseq, nqb),
        in_specs=[pl.BlockSpec((GQA_QB, 1024), lambda b, q: (b * nqb + q, 2)),
                  pl.BlockSpec((L, 256), lambda b, q: (b, 20)), pl.BlockSpec((L, 256), lambda b, q: (b, 21)),
                  pl.BlockSpec((1, Lc, 256), lambda b, q: (b, 0, 0)), pl.BlockSpec((1, Lc, 256), lambda b, q: (b, 0, 0)),
                  _full((L, LANES)), _full((L, LANES)), pl.BlockSpec(memory_space=pltpu.SMEM)],
        out_specs=pl.BlockSpec((GQA_QB, 1024), lambda b, q: (b * nqb + q, 0)),
        out_shape=jax.ShapeDtypeStruct((n, 1024), F32),
        scratch_shapes=[pltpu.VMEM((L + 2 * WINDOW, 256), BF16), pltpu.VMEM((L + 2 * WINDOW, 256), BF16)],
        compiler_params=_cp(2), name="gqa_window",
    )(p1, p1, p1, cache_k, cache_v, cos, sin, sink)


def _mlstm_kernel(qp_ref, qc_ref, qn_ref, v_ref, o_ref, sm_ref, gt_ref, cw_ref, cb_ref, brow_ref, bcol_ref, ng_ref,
                  c0_ref, n0_ref, m0_ref, y_ref, cst_ref, nst_ref, mst_ref, hf_s, c_s, n_s, m_s, *, nc):
    s = pl.program_id(1)
    bwd = s >= nc
    cidx = _chunk_index(s, nc)
    nh = ML_HEADS
    hd = ML_HEAD_DIM

    @pl.when(s == 0)
    def _():
        c_s[...] = c0_ref[0, 0]
        n_s[...] = n0_ref[0, 0]
        m_s[...] = m0_ref[0, 0]

    @pl.when(s == nc)
    def _():
        cst_ref[0, 0] = c_s[...]
        nst_ref[0, 0] = n_s[...]
        mst_ref[0, 0] = m_s[...]
        c_s[...] = c0_ref[0, 1]
        n_s[...] = n0_ref[0, 1]
        m_s[...] = m0_ref[0, 1]

    qk = _conv_silu(qp_ref[...], qc_ref[...], qn_ref[...], cw_ref, cb_ref, cidx == 0, cidx == nc - 1)
    gates = sm_ref[:, 0:4 * nh] + brow_ref[...]
    gates_t = gt_ref[0] + bcol_ref[...]
    li = jnp.where(bwd, gates[:, nh:2 * nh], gates[:, 0:nh])
    li_t = jnp.where(bwd, gates_t[nh:2 * nh, :], gates_t[0:nh, :])
    lf = -_softplus(-jnp.where(bwd, gates[:, 3 * nh:4 * nh], gates[:, 2 * nh:3 * nh]))
    lf_t = -_softplus(-jnp.where(bwd, gates_t[3 * nh:4 * nh, :], gates_t[2 * nh:3 * nh, :]))
    tsel, tsel_t = _tri(bwd)
    bc = _dot_f32(tsel, lf)
    bc_t = _dot_f32(lf_t, tsel_t)
    tot = jnp.where(bwd, bc[0:1, :], bc[CHUNK - 1:CHUNK, :])
    mask = tsel > 0.0
    kscale = np.float32(hd ** -0.5)

    hs = []
    for h in range(nh):
        cs = slice(h * hd, (h + 1) * hd)
        q = qk[:, cs]
        k = qk[:, ML_D + h * hd:ML_D + (h + 1) * hd] * kscale
        q16, k16 = q.astype(BF16), k.astype(BF16)
        v = v_ref[:, cs]
        bq = bc[:, h:h + 1]
        m_prev = m_s[h:h + 1, 0:1]
        log_d = jnp.where(mask, bq - bc_t[h:h + 1, :] + li_t[h:h + 1, :], NEG_INF)
        log_inter = bq + m_prev
        m_out = jnp.maximum(log_inter, jnp.max(log_d, axis=1, keepdims=True))
        sd = _dot_nt(q16, k16) * jnp.exp(log_d - m_out)
        w_inter = jnp.exp(log_inter - m_out)
        c_prev = c_s[h]
        n_prev = n_s[h:h + 1, :]
        num = _dot(sd.astype(BF16), v.astype(BF16)) + w_inter * _dot_nt(q16, c_prev.astype(BF16))
        den = jnp.sum(sd, axis=1, keepdims=True) + w_inter * jnp.sum(q * n_prev, axis=1, keepdims=True)
        hs.append(num / jnp.maximum(jnp.abs(den), jnp.exp(-m_out)))
        th = tot[:, h:h + 1]
        end_inter = th + m_prev
        end_intra = th - bq + li[:, h:h + 1]
        m_new = jnp.maximum(end_inter, jnp.max(end_intra, axis=0, keepdims=True))
        w_c = jnp.exp(end_inter - m_new)
        w_k = jnp.exp(end_intra - m_new)
        c_s[h] = w_c * c_prev + _dot_tn((v * w_k).astype(BF16), k16)
        n_s[h:h + 1, :] = w_c * n_prev + jnp.sum(k * w_k, axis=0, keepdims=True)
        m_s[h:h + 1, :] = jnp.broadcast_to(m_new, (1, hd))
    hcat = jnp.concatenate(hs, axis=1)
    r0 = pl.multiple_of(cidx * CHUNK, CHUNK)

    @pl.when(jnp.logical_not(bwd))
    def _():
        hf_s[pl.ds(r0, CHUNK), :] = hcat

    @pl.when(bwd)
    def _():
        tot_h = hf_s[pl.ds(r0, CHUNK), :] + hcat
        normed = jnp.concatenate([_rms(tot_h[:, h * hd:(h + 1) * hd]) for h in range(nh)], axis=1) * ng_ref[...]
        y_ref[...] = normed * jax.nn.sigmoid(o_ref[...])

    @pl.when(s == 2 * nc - 1)
    def _():
        cst_ref[0, 1] = c_s[...]
        nst_ref[0, 1] = n_s[...]
        mst_ref[0, 1] = m_s[...]


def _mlstm(p1, gt, conv_w, conv_b, ig_b, fg_b, norm_g, c0, n0, m0, *, L):
    n = p1.shape[0]
    nseq, nc = n // L, L // CHUNK
    cur, prev, nxt, out = _scan_specs(nc, n // 8)
    per_seq = c0.shape[0] > 1
    gb = jnp.concatenate([ig_b.reshape(-1), fg_b.reshape(-1)])
    small = [conv_w, conv_b.reshape(1, -1), gb.reshape(1, -1), gb.reshape(-1, 1), norm_g.reshape(1, -1)]
    cshape = (1, 2, ML_HEADS, ML_HEAD_DIM, ML_HEAD_DIM)
    nshape = (1, 2, ML_HEADS, ML_HEAD_DIM)
    st_in = lambda shape: pl.BlockSpec(shape, lambda b, s: (b if per_seq else 0,) + (0,) * (len(shape) - 1))
    st_out = lambda shape: pl.BlockSpec(shape, lambda b, s: (b,) + (0,) * (len(shape) - 1))
    return pl.pallas_call(
        functools.partial(_mlstm_kernel, nc=nc),
        grid=(nseq, 2 * nc),
        in_specs=[pl.BlockSpec((8, 2048), prev(0)), pl.BlockSpec((CHUNK, 2048), cur(0)), pl.BlockSpec((8, 2048), nxt(0)),
                  pl.BlockSpec((CHUNK, 1024), cur(3)), pl.BlockSpec((CHUNK, 1024), cur(4)),
                  pl.BlockSpec((CHUNK, LANES), cur(44)),
                  pl.BlockSpec((1, 32, CHUNK), lambda b, s: (b * nc + _chunk_index(s, nc), 0, 0))]
        + [_full(v.shape) for v in small] + [st_in(cshape), st_in(nshape), st_in(nshape)],
        out_specs=[pl.BlockSpec((CHUNK, ML_D), out(0)), st_out(cshape), st_out(nshape), st_out(nshape)],
        out_shape=[jax.ShapeDtypeStruct((n, ML_D), F32), jax.ShapeDtypeStruct((nseq,) + cshape[1:], F32),
                   jax.ShapeDtypeStruct((nseq,) + nshape[1:], F32), jax.ShapeDtypeStruct((nseq,) + nshape[1:], F32)],
        scratch_shapes=[pltpu.VMEM((L, ML_D), F32), pltpu.VMEM(cshape[2:], F32), pltpu.VMEM(nshape[2:], F32),
                        pltpu.VMEM(nshape[2:], F32)],
        compiler_params=_cp(2), name="mlstm_mixer",
    )(p1, p1, p1, p1, p1, p1, gt, *small, c0, n0, m0)


def _prep_layer0(w_in, w_qb, w_kvb, w_out):
    pad = jnp.zeros((D_MODEL, L0_COLS - 3680), F32)
    w = jnp.concatenate([w_in[:, :2560], w_in[:, 2592:3680], w_in[:, 2560:2592], pad], axis=1).astype(BF16)
    q3 = w_qb.reshape(MLA_RANK, MLA_HEADS, MLA_NOPE + MLA_ROPE)
    wqn = q3[:, :, :MLA_NOPE].reshape(MLA_RANK, -1).astype(BF16)
    wqr = jnp.concatenate([q3[:, :, MLA_NOPE:], jnp.zeros((MLA_RANK, MLA_HEADS, LANES - MLA_ROPE), F32)], axis=-1)
    wqr = wqr.reshape(MLA_RANK, -1).astype(BF16)
    k3 = w_kvb.reshape(MLA_RANK, MLA_HEADS, MLA_NOPE + MLA_V)
    wkk = k3[:, :, :MLA_NOPE].reshape(MLA_RANK, -1).astype(BF16)
    wkv = k3[:, :, MLA_NOPE:].reshape(MLA_RANK, -1).astype(BF16)
    return w, wqn, wqr, wkk, wkv, w_out.astype(BF16)


def _prep_layer1(w_in, w_out):
    pad = jnp.zeros((D_MODEL, L1_COLS - 5664), F32)
    w = jnp.concatenate([w_in[:, 1536:3584], w_in[:, 0:1024], w_in[:, 3584:5632], w_in[:, 1024:1536],
                         w_in[:, 5632:5664], pad], axis=1).astype(BF16)
    return w, w_out.astype(BF16)


def _prep_peer(wq, keys, u, v):
    return (wq.T.astype(BF16), keys.reshape(PEER_HEADS * 2, PEER_NKEYS, -1).astype(BF16), u.astype(BF16),
            v.T.astype(BF16))


def _chunk_transposed(cols):
    return cols.reshape(-1, CHUNK, cols.shape[1]).transpose(0, 2, 1)


def _trunk(x, L, mod0, mod1, w0, w1, peer0, peer1, p, ctx, rope_tabs, mod_base, rows_per_cond, fg):
    kw = dict(mod_base=mod_base, rows_per_cond=rows_per_cond)
    w_in0, wqn, wqr, wkk, wkv, w_out0 = w0
    w_in1, w_out1 = w1
    nseq = x.shape[0] // L

    p0 = _proj(x, p['l0_norm1_g'], mod0, w_in0, tn=1280, **kw)
    st0 = jnp.zeros((1, 2, SSD_HEADS, SSD_HEAD_DIM, SSD_STATE), F32) if ctx is None else ctx[0]
    cw, cb = p['l0_ssd_conv_w'], p['l0_ssd_conv_b']
    y_ssd, ssd_state = _ssd(p0, _chunk_transposed(p0[:, 3648:3680]), cw[:, :1024], cb[:1024], cw[:, 1024:], cb[1024:],
                            p['l0_ssd_A_log'], p['l0_ssd_dt_bias'], p['l0_ssd_D'], p['l0_ssd_norm_g'], st0, L=L)
    mla_tabs = None
    if rope_tabs is not None:
        z64 = jnp.zeros_like(rope_tabs[0])
        mla_tabs = (jnp.concatenate([rope_tabs[0], z64], axis=1), jnp.concatenate([rope_tabs[1], z64], axis=1))
    mla_out = _mla(p0, p['l0_mla_q_norm_g'], p['l0_mla_kv_norm_g'], wqn, wqr, wkk, wkv,
                   None if ctx is None else ctx[1], None if ctx is None else ctx[2], mla_tabs, L=L)
    x = _outproj(y_ssd, mla_out[0], w_out0, x, mod0, **kw)
    x = _peer(x, p['l0_norm2_g'], mod0, *peer0, fg, final_norm=False, **kw)

    p1 = _proj(x, p['l1_norm1_g'], mod1, w_in1, tn=1920, **kw)
    if ctx is None:
        y_gqa = _gqa_ctx(p1, p['l1_gqa_sink'], L=L)
        c0 = jnp.zeros((1, 2, ML_HEADS, ML_HEAD_DIM, ML_HEAD_DIM), F32)
        n0 = jnp.zeros((1, 2, ML_HEADS, ML_HEAD_DIM), F32)
        m0 = n0
    else:
        gqa_tabs = tuple(jnp.concatenate([t, t], axis=1) for t in rope_tabs)
        y_gqa = _gqa_win(p1, ctx[3].reshape(nseq, -1, 256), ctx[4].reshape(nseq, -1, 256), *gqa_tabs,
                         p['l1_gqa_sink'], L=L)
        c0, n0 = ctx[5], ctx[6]
        m0 = jnp.broadcast_to(ctx[7][..., None], ctx[7].shape + (ML_HEAD_DIM,))
    y_ml, mc, mn, mm = _mlstm(p1, _chunk_transposed(p1[:, 5632:5664]), p['l1_ml_conv_w'], p['l1_ml_conv_b'],
                              p['l1_ml_ig_b'], p['l1_ml_fg_b'], p['l1_ml_norm_g'], c0, n0, m0, L=L)
    x = _outproj(y_gqa, y_ml, w_out1, x, mod1, **kw)
    y = _peer(x, p['l1_norm2_g'], mod1, *peer1, fg, final_norm=True, **kw)
    new = None
    if ctx is None:
        new = (ssd_state, mla_out[1].reshape(nseq, L, MLA_RANK), p0[:, 3584:3648].reshape(nseq, L, MLA_ROPE),
               p1[:, 5120:5376].reshape(nseq, L, GQA_KV_HEADS, GQA_HEAD_DIM),
               p1[:, 5376:5632].reshape(nseq, L, GQA_KV_HEADS, GQA_HEAD_DIM), mc, mn, mm[..., 0])
    return y, new


def kernel(x_prompt, x_sample, state_l0_ssd, cache_l0_mla_ckv, cache_l0_mla_kpe, cache_l1_gqa_k, cache_l1_gqa_v, state_l1_mlstm_C, state_l1_mlstm_n, state_l1_mlstm_m, c, c_ctx, final_norm_g, l0_ada_w, l0_ada_b, l0_norm1_g, l0_norm2_g, l0_w_in, l0_ssd_conv_w, l0_ssd_conv_b, l0_ssd_A_log, l0_ssd_dt_bias, l0_ssd_D, l0_ssd_norm_g, l0_mla_q_norm_g, l0_mla_w_qb, l0_mla_kv_norm_g, l0_mla_w_kvb, l0_w_out, l0_peer_wq, l0_peer_keys, l0_peer_u, l0_peer_v, l1_ada_w, l1_ada_b, l1_norm1_g, l1_norm2_g, l1_w_in, l1_gqa_sink, l1_ml_conv_w, l1_ml_conv_b, l1_ml_ig_b, l1_ml_fg_b, l1_ml_norm_g, l1_w_out, l1_peer_wq, l1_peer_keys, l1_peer_u, l1_peer_v):
    p = dict(l0_norm1_g=l0_norm1_g, l0_norm2_g=l0_norm2_g, l0_ssd_conv_w=l0_ssd_conv_w, l0_ssd_conv_b=l0_ssd_conv_b,
             l0_ssd_A_log=l0_ssd_A_log, l0_ssd_dt_bias=l0_ssd_dt_bias, l0_ssd_D=l0_ssd_D, l0_ssd_norm_g=l0_ssd_norm_g,
             l0_mla_q_norm_g=l0_mla_q_norm_g, l0_mla_kv_norm_g=l0_mla_kv_norm_g, l1_norm1_g=l1_norm1_g,
             l1_norm2_g=l1_norm2_g, l1_gqa_sink=l1_gqa_sink, l1_ml_conv_w=l1_ml_conv_w, l1_ml_conv_b=l1_ml_conv_b,
             l1_ml_ig_b=l1_ml_ig_b, l1_ml_fg_b=l1_ml_fg_b, l1_ml_norm_g=l1_ml_norm_g)
    nb, seq = x_prompt.shape[:2]
    db, dseq = x_sample.shape[:2]
    assert db <= 7 and seq % MLA_QB == 0 and dseq % MLA_QB == 0

    cond8 = jnp.zeros((8, D_MODEL), F32).at[0].set(c_ctx).at[1:1 + db].set(c)
    mod0 = _ada(cond8, l0_ada_w, l0_ada_b).reshape(8, 1, 6 * D_MODEL)
    mod1 = _ada(cond8, l1_ada_w, l1_ada_b).reshape(8, 1, 6 * D_MODEL)
    w0 = _prep_layer0(l0_w_in, l0_mla_w_qb, l0_mla_w_kvb, l0_w_out)
    w1 = _prep_layer1(l1_w_in, l1_w_out)
    peer0 = _prep_peer(l0_peer_wq, l0_peer_keys, l0_peer_u, l0_peer_v)
    peer1 = _prep_peer(l1_peer_wq, l1_peer_keys, l1_peer_u, l1_peer_v)

    y_prompt, new = _trunk(x_prompt.reshape(nb * seq, D_MODEL), seq, mod0, mod1, w0, w1, peer0, peer1, p, None, None,
                           0, nb * seq, final_norm_g)
    ctx = (state_l0_ssd, cache_l0_mla_ckv, cache_l0_mla_kpe, cache_l1_gqa_k, cache_l1_gqa_v, state_l1_mlstm_C,
           state_l1_mlstm_n, state_l1_mlstm_m)
    y_sample, _ = _trunk(x_sample.reshape(db * dseq, D_MODEL), dseq, mod0, mod1, w0, w1, peer0, peer1, p, ctx,
                         _rope_tables(dseq), 1, dseq, final_norm_g)
    return (y_prompt.reshape(nb, seq, D_MODEL), y_sample.reshape(db, dseq, D_MODEL)) + new
```

```python
import functools

import jax
import jax.numpy as jnp
import numpy as np
from jax import lax
from jax.experimental import pallas as pl
from jax.experimental.pallas import tpu as pltpu

F32 = jnp.float32
BF16 = jnp.bfloat16
NEG_INF = float("-inf")

D_MODEL = 2048
EPS = 1e-6
ROPE_BASE = 10000.0
GRID_W = 64
CONV_W = 5
CHUNK = 64
SSD_HEADS = 16
SSD_HEAD_DIM = 64
SSD_D_INNER = 1024
SSD_STATE = 128
MLA_HEADS = 8
MLA_NOPE = 128
MLA_ROPE = 64
MLA_V = 128
MLA_RANK = 512
GQA_HEADS = 16
GQA_KV_HEADS = 4
GQA_HEAD_DIM = 64
WINDOW = 128
ML_HEADS = 8
ML_HEAD_DIM = 128
ML_D = 1024
PEER_HEADS = 8
PEER_NKEYS = 128
PEER_N = PEER_NKEYS * PEER_NKEYS
PEER_TOPK = 16

LANES = 128
SUBLANES = 8
VMEM_LIMIT = 56 * 2**20

L0_COLS = 3840
L1_COLS = 5760


def _cp(n_grid, flags=None):
    return pltpu.CompilerParams(dimension_semantics=("arbitrary",) * n_grid, vmem_limit_bytes=VMEM_LIMIT, flags=flags)


def _silu(x):
    return x * jax.nn.sigmoid(x)


def _softplus(x):
    return jnp.maximum(x, 0.0) + jnp.log1p(jnp.exp(-jnp.abs(x)))


def _rms(x):
    return x * lax.rsqrt(jnp.mean(x * x, axis=-1, keepdims=True) + EPS)


def _dot(a, b):
    return jnp.dot(a, b, preferred_element_type=F32)


def _dot_nt(a, b):
    return lax.dot_general(a, b, (((1,), (1,)), ((), ())), preferred_element_type=F32)


def _dot_tn(a, b):
    return lax.dot_general(a, b, (((0,), (0,)), ((), ())), preferred_element_type=F32)


def _dot_f32(a, b):
    return jnp.dot(a, b, preferred_element_type=F32, precision=lax.Precision.HIGHEST)


def _full(shape):
    nd = len(shape)
    return pl.BlockSpec(shape, lambda *_: (0,) * nd)


def _mod_spec(chunk, tb, mod_base, rows_per_cond):
    def index(i, *_):
        return (mod_base + (i * tb) // rows_per_cond, 0, chunk)
    return pl.BlockSpec((1, 1, D_MODEL), index)


def _ada_kernel(c_ref, w_ref, b_ref, o_ref):
    s = _silu(c_ref[...])
    o_ref[...] = _dot(s.astype(BF16), w_ref[...].astype(BF16)) + b_ref[...]


def _ada(cond8, w, b):
    n = w.shape[1]
    tn = 1024
    return pl.pallas_call(
        _ada_kernel,
        grid=(n // tn,),
        in_specs=[_full((8, D_MODEL)), pl.BlockSpec((D_MODEL, tn), lambda j: (0, j)),
                  pl.BlockSpec((1, tn), lambda j: (0, j))],
        out_specs=pl.BlockSpec((8, tn), lambda j: (0, j)),
        out_shape=jax.ShapeDtypeStruct((8, n), F32),
        compiler_params=_cp(1), name="ada_table",
    )(cond8, w, b.reshape(1, n))


def _proj_kernel(x_ref, g_ref, sc_ref, sh_ref, w_ref, o_ref, h_ref):
    @pl.when(pl.program_id(1) == 0)
    def _():
        h = _rms(x_ref[...]) * g_ref[...] * (1.0 + sc_ref[0]) + sh_ref[0]
        h_ref[...] = h.astype(BF16)
    o_ref[...] = _dot(h_ref[...], w_ref[...])


def _proj(x, g, mod, w, *, tn, mod_base, rows_per_cond):
    n, ncol = x.shape[0], w.shape[1]
    tb = 512
    return pl.pallas_call(
        _proj_kernel,
        grid=(n // tb, ncol // tn),
        in_specs=[pl.BlockSpec((tb, D_MODEL), lambda i, j: (i, 0)), _full((1, D_MODEL)),
                  _mod_spec(1, tb, mod_base, rows_per_cond), _mod_spec(0, tb, mod_base, rows_per_cond),
                  pl.BlockSpec((D_MODEL, tn), lambda i, j: (0, j))],
        out_specs=pl.BlockSpec((tb, tn), lambda i, j: (i, j)),
        out_shape=jax.ShapeDtypeStruct((n, ncol), F32),
        scratch_shapes=[pltpu.VMEM((tb, D_MODEL), BF16)],
        compiler_params=_cp(2), name="norm_mod_proj",
    )(x, g.reshape(1, D_MODEL), mod, mod, w)


def _outproj_kernel(ya_ref, yb_ref, w_ref, x_ref, g1_ref, o_ref):
    half = ya_ref.shape[1]
    y = _dot(ya_ref[...].astype(BF16), w_ref[0:half, :]) + _dot(yb_ref[...].astype(BF16), w_ref[half:, :])
    o_ref[...] = x_ref[...] + g1_ref[0] * y


def _outproj(ya, yb, w, x, mod, *, mod_base, rows_per_cond):
    n = x.shape[0]
    tb = 512
    return pl.pallas_call(
        _outproj_kernel,
        grid=(n // tb,),
        in_specs=[pl.BlockSpec((tb, ya.shape[1]), lambda i: (i, 0)), pl.BlockSpec((tb, yb.shape[1]), lambda i: (i, 0)),
                  _full(w.shape), pl.BlockSpec((tb, D_MODEL), lambda i: (i, 0)),
                  _mod_spec(2, tb, mod_base, rows_per_cond)],
        out_specs=pl.BlockSpec((tb, D_MODEL), lambda i: (i, 0)),
        out_shape=jax.ShapeDtypeStruct((n, D_MODEL), F32),
        compiler_params=_cp(1), name="out_proj_residual",
    )(ya, yb, w, x, mod)


ROUTE_TB = 256
_CAND_KEEP = (16, 8, 5, 4, 3, 2, 2, 2)


def _sort_pairs(n):
    pairs = []

    def merge(lo, hi, r):
        step = r * 2
        if step < hi - lo:
            merge(lo, hi, step)
            merge(lo + r, hi, step)
            pairs.extend((i, i + r) for i in range(lo + r, hi - r, step))
        else:
            pairs.append((lo, lo + r))

    def sort(lo, hi):
        if hi - lo >= 1:
            mid = lo + (hi - lo) // 2
            sort(lo, mid)
            sort(mid + 1, hi)
            merge(lo, hi, 1)

    sort(0, 15)
    return [(i, j) for i, j in pairs if j < n]


def _top_values(x, count):
    groups = [x[r:r + SUBLANES] for r in range(0, x.shape[0], SUBLANES)]
    n = len(groups)
    for i, j in _sort_pairs(n):
        groups[i], groups[j] = jnp.maximum(groups[i], groups[j]), jnp.minimum(groups[i], groups[j])
    sub = lax.broadcasted_iota(jnp.int32, groups[0].shape, 0)
    tops = []
    for t in range(count):
        head = groups[0]
        m = jnp.max(head, axis=0, keepdims=True)
        tops.append(m)
        if t == count - 1:
            break
        first = jnp.min(jnp.where(head == m, sub, SUBLANES), axis=0, keepdims=True)
        popped = sub == first
        depth = min(count - 1 - t, n)
        for r in range(depth):
            below = groups[r + 1] if r + 1 < n else NEG_INF
            groups[r] = jnp.where(popped, below, groups[r])
    return tops


def _route_kernel(x_ref, g_ref, sc_ref, sh_ref, wqt_ref, keys_ref,
                  hb_ref, s1_ref, s2_ref, e2_ref, c1_ref, tau_ref, qt_s, top_s):
    tb = x_ref.shape[0]
    h = _rms(x_ref[...]) * g_ref[...] * (1.0 + sc_ref[0]) + sh_ref[0]
    hb = h.astype(BF16)
    hb_ref[...] = hb
    qt_s[...] = _dot_nt(wqt_ref[...], hb).astype(BF16)

    iota_8 = lax.broadcasted_iota(jnp.int32, (SUBLANES, tb), 0)

    def head(hd, _):
        scores = []
        for c in range(2):
            r0 = pl.multiple_of((hd * 2 + c) * PEER_NKEYS, PEER_NKEYS)
            s = _dot(keys_ref[hd * 2 + c], qt_s[pl.ds(r0, PEER_NKEYS), :])
            scores.append(s)
            for it, m in enumerate(_top_values(s, PEER_TOPK)):
                top_s[c, it:it + 1, :] = m
        a1 = top_s[0]
        a2 = top_s[1]
        pieces = [a1[0:1] + a2, a1[1:2] + a2[0:8]]
        for i in range(2, 8):
            pieces.append(jnp.where(iota_8 < _CAND_KEEP[i], a1[i:i + 1] + a2[0:8], NEG_INF))
        pieces.append(a1[8:16] + a2[0:1])
        cand = jnp.concatenate(pieces, axis=0)
        tau = _top_values(cand, PEER_TOPK)[-1]
        top = a1[0:1] + a2[0:1]
        z = jnp.sum(jnp.where(cand >= tau, jnp.exp(cand - top), 0.0), axis=0, keepdims=True)
        e2 = jnp.exp(scores[1] - a2[0:1])
        c1 = jnp.exp(scores[0] - a1[0:1]) / z
        for t in range(tb // LANES):
            ts = slice(t * LANES, (t + 1) * LANES)
            s1_ref[hd, t] = scores[0][:, ts]
            s2_ref[hd, t] = scores[1][:, ts]
            e2_ref[hd, t] = e2[:, ts]
            c1_ref[hd, t] = c1[:, ts]
            tau_ref[t, pl.ds(hd, 1), :] = tau[:, ts]
        return 0

    lax.fori_loop(0, PEER_HEADS, head, 0)


def _route(x, g, mod, wqt, keys, *, mod_base, rows_per_cond):
    n = x.shape[0]
    tb = ROUTE_TB
    ntg = tb // LANES
    hk = (PEER_HEADS, n // LANES, PEER_NKEYS, LANES)
    rspec = pl.BlockSpec((PEER_HEADS, ntg, PEER_NKEYS, LANES), lambda i: (0, i, 0, 0))
    return pl.pallas_call(
        _route_kernel,
        grid=(n // tb,),
        in_specs=[pl.BlockSpec((tb, D_MODEL), lambda i: (i, 0)), _full((1, D_MODEL)),
                  _mod_spec(4, tb, mod_base, rows_per_cond), _mod_spec(3, tb, mod_base, rows_per_cond),
                  _full(wqt.shape), _full(keys.shape)],
        out_specs=[pl.BlockSpec((tb, D_MODEL), lambda i: (i, 0)), rspec, rspec, rspec, rspec,
                   pl.BlockSpec((ntg, PEER_HEADS, LANES), lambda i: (i, 0, 0))],
        out_shape=[jax.ShapeDtypeStruct((n, D_MODEL), BF16)] + [jax.ShapeDtypeStruct(hk, F32)] * 4
        + [jax.ShapeDtypeStruct((n // LANES, PEER_HEADS, LANES), F32)],
        scratch_shapes=[pltpu.VMEM((PEER_HEADS * 2 * PEER_NKEYS, tb), BF16), pltpu.VMEM((2, PEER_TOPK, tb), F32)],
        compiler_params=_cp(1), name="peer_route",
    )(x, g.reshape(1, D_MODEL), mod, mod, wqt, keys)


PEER_TB = 512
PEER_EC = 512


def _experts_kernel(hb_ref, s1_ref, s2_ref, e2_ref, c1_ref, tau_ref, u_ref, vt_ref, x_ref, g2_ref, fg_ref,
                    o_ref, acc_s, st_s, a_s, *, final_norm):
    e = pl.program_id(1)
    tb = hb_ref.shape[0]
    ntg = tb // LANES
    rows_per_step = PEER_EC // PEER_NKEYS

    @pl.when(e == 0)
    def _():
        acc_s[...] = jnp.zeros_like(acc_s)

    st = _dot_nt(u_ref[...], hb_ref[...])
    for tg in range(ntg):
        st_s[tg] = st[:, tg * LANES:(tg + 1) * LANES]

    groups = SUBLANES // rows_per_step
    g8 = pl.multiple_of((e // groups) * SUBLANES, SUBLANES)
    part = e % groups

    def pick(rows8, k):
        row = rows8[k:k + 1]
        for m in range(1, groups):
            row = jnp.where(part == m, rows8[m * rows_per_step + k:m * rows_per_step + k + 1], row)
        return row

    def tile(tg, _):
        s1g = [s1_ref[hd, tg, pl.ds(g8, SUBLANES), :] for hd in range(PEER_HEADS)]
        c1g = [c1_ref[hd, tg, pl.ds(g8, SUBLANES), :] for hd in range(PEER_HEADS)]
        taus = tau_ref[tg]
        for k in range(rows_per_step):
            ks = slice(k * PEER_NKEYS, (k + 1) * PEER_NKEYS)
            w = jnp.zeros((PEER_NKEYS, LANES), F32)
            for hd in range(PEER_HEADS):
                val = pick(s1g[hd], k) + s2_ref[hd, tg]
                sel = jnp.where(val >= taus[hd:hd + 1], e2_ref[hd, tg], 0.0)
                w = w + sel * pick(c1g[hd], k)
            s = st_s[tg, ks, :]
            act = 0.5 * s * (1.0 + lax.erf(s * np.float32(2.0 ** -0.5)))
            a_s[tg, ks, :] = (act * w).astype(BF16)
        return 0

    lax.fori_loop(0, ntg, tile, 0)
    act_all = jnp.concatenate([a_s[tg] for tg in range(ntg)], axis=1)
    acc_s[...] += _dot(vt_ref[...], act_all)

    @pl.when(e == pl.num_programs(1) - 1)
    def _():
        y = x_ref[...] + g2_ref[0] * acc_s[...].T
        if final_norm:
            y = _rms(y) * fg_ref[...]
        o_ref[...] = y


def _experts(hb, s1, s2, e2, c1, tau, u, vt, x, mod, fg, *, mod_base, rows_per_cond, final_norm):
    n = x.shape[0]
    tb = PEER_TB
    ntg = tb // LANES
    rspec = pl.BlockSpec((PEER_HEADS, ntg, PEER_NKEYS, LANES), lambda i, e: (0, i, 0, 0))
    return pl.pallas_call(
        functools.partial(_experts_kernel, final_norm=final_norm),
        grid=(n // tb, PEER_N // PEER_EC),
        in_specs=[pl.BlockSpec((tb, D_MODEL), lambda i, e: (i, 0)), rspec, rspec, rspec, rspec,
                  pl.BlockSpec((ntg, PEER_HEADS, LANES), lambda i, e: (i, 0, 0)),
                  pl.BlockSpec((PEER_EC, D_MODEL), lambda i, e: (e, 0)),
                  pl.BlockSpec((D_MODEL, PEER_EC), lambda i, e: (0, e)),
                  pl.BlockSpec((tb, D_MODEL), lambda i, e: (i, 0)),
                  _mod_spec(5, tb, mod_base, rows_per_cond), _full((1, D_MODEL))],
        out_specs=pl.BlockSpec((tb, D_MODEL), lambda i, e: (i, 0)),
        out_shape=jax.ShapeDtypeStruct((n, D_MODEL), F32),
        scratch_shapes=[pltpu.VMEM((D_MODEL, tb), F32), pltpu.VMEM((ntg, PEER_EC, LANES), F32),
                        pltpu.VMEM((ntg, PEER_EC, LANES), BF16)],
        compiler_params=_cp(2), name="peer_experts",
    )(hb, s1, s2, e2, c1, tau, u, vt, x, mod, fg.reshape(1, D_MODEL))


def _peer(x, g, mod, wqt, keys, u, vt, fg, *, mod_base, rows_per_cond, final_norm):
    hb, s1, s2, e2, c1, tau = _route(x, g, mod, wqt, keys, mod_base=mod_base, rows_per_cond=rows_per_cond)
    return _experts(hb, s1, s2, e2, c1, tau, u, vt, x, mod, fg, mod_base=mod_base, rows_per_cond=rows_per_cond,
                    final_norm=final_norm)


def _chunk_index(s, nc):
    return jnp.where(s < nc, s, 2 * nc - 1 - s)


def _scan_specs(nc, nrow8):
    def cur(col):
        return lambda b, s: (b * nc + _chunk_index(s, nc), col)

    def prev(col):
        return lambda b, s: (jnp.maximum((b * nc + _chunk_index(s, nc)) * 8 - 1, 0), col)

    def nxt(col):
        return lambda b, s: (jnp.minimum((b * nc + _chunk_index(s, nc)) * 8 + 8, nrow8 - 1), col)

    def out(col):
        return lambda b, s: (b * nc + jnp.where(s < nc, nc - 1, 2 * nc - 1 - s), col)
    return cur, prev, nxt, out


def _conv_silu(prev, cur, nxt, w_ref, b_ref, first, last):
    win = jnp.concatenate([jnp.where(first, 0.0, prev), cur, jnp.where(last, 0.0, nxt)], axis=0)
    acc = b_ref[...] + win[6:70, :] * w_ref[0:1, :]
    for k in range(1, CONV_W):
        acc = acc + win[6 + k:70 + k, :] * w_ref[k:k + 1, :]
    return _silu(acc)


def _tri(bwd):
    r = lax.broadcasted_iota(jnp.int32, (CHUNK, CHUNK), 0)
    c = lax.broadcasted_iota(jnp.int32, (CHUNK, CHUNK), 1)
    lower = (c <= r).astype(F32)
    upper = (c >= r).astype(F32)
    return jnp.where(bwd, upper, lower), jnp.where(bwd, lower, upper)


def _ssd_kernel(z_ref, xp_ref, xc_ref, xn_ref, bp_ref, bc_ref, bn_ref, sm_ref, dtt_ref,
                cwx_ref, cbx_ref, cwb_ref, cbb_ref, arow_ref, acol_ref, brow_ref, bcol_ref, drow_ref, ng_ref,
                st0_ref, y_ref, st_ref, yf_s, h_s, *, nc):
    s = pl.program_id(1)
    bwd = s >= nc
    cidx = _chunk_index(s, nc)
    first = cidx == 0
    last = cidx == nc - 1
    nh = SSD_HEADS

    @pl.when(s == 0)
    def _():
        h_s[...] = st0_ref[0, 0]

    @pl.when(s == nc)
    def _():
        st_ref[0, 0] = h_s[...]
        h_s[...] = st0_ref[0, 1]

    xs = _conv_silu(xp_ref[...], xc_ref[...], xn_ref[...], cwx_ref, cbx_ref, first, last)
    bcv = _conv_silu(bp_ref[...], bc_ref[...], bn_ref[...], cwb_ref, cbb_ref, first, last)

    dt_all = _softplus(sm_ref[:, 64:96] + brow_ref[...])
    dtt_all = _softplus(dtt_ref[0] + bcol_ref[...])
    a_all = dt_all * (-jnp.exp(arow_ref[...]))
    at_all = dtt_all * (-jnp.exp(acol_ref[...]))
    dt = jnp.where(bwd, dt_all[:, nh:2 * nh], dt_all[:, 0:nh])
    a = jnp.where(bwd, a_all[:, nh:2 * nh], a_all[:, 0:nh])
    at = jnp.where(bwd, at_all[nh:2 * nh, :], at_all[0:nh, :])
    tsel, tsel_t = _tri(bwd)
    cum = _dot_f32(tsel, a)
    cum_t = _dot_f32(at, tsel_t)
    tot = jnp.where(bwd, cum[0:1, :], cum[CHUNK - 1:CHUNK, :])
    mask = tsel > 0.0

    ys = []
    for g in range(2):
        bg = bcv[:, g * SSD_STATE:(g + 1) * SSD_STATE]
        cg = bcv[:, 256 + g * SSD_STATE:256 + (g + 1) * SSD_STATE]
        cg16 = cg.astype(BF16)
        cb = _dot_nt(cg16, bg.astype(BF16))
        for hh in range(nh // 2):
            h = g * (nh // 2) + hh
            cq = cum[:, h:h + 1]
            decay = jnp.exp(jnp.where(mask, cq - cum_t[h:h + 1, :], NEG_INF))
            xd = xs[:, h * SSD_HEAD_DIM:(h + 1) * SSD_HEAD_DIM] * dt[:, h:h + 1]
            xd16 = xd.astype(BF16)
            hprev = h_s[h]
            y = _dot((cb * decay).astype(BF16), xd16) + _dot_nt(cg16, hprev.astype(BF16)) * jnp.exp(cq)
            ys.append(y)
            th = tot[:, h:h + 1]
            bw = bg * jnp.exp(th - cq)
            h_s[h] = hprev * jnp.exp(th) + _dot_tn(xd16, bw.astype(BF16))
    y = jnp.concatenate(ys, axis=1)
    r0 = pl.multiple_of(cidx * CHUNK, CHUNK)

    @pl.when(jnp.logical_not(bwd))
    def _():
        yf_s[pl.ds(r0, CHUNK), :] = y

    @pl.when(bwd)
    def _():
        yt = (yf_s[pl.ds(r0, CHUNK), :] + y + drow_ref[...] * xs) * _silu(z_ref[...])
        y_ref[...] = _rms(yt) * ng_ref[...]

    @pl.when(s == 2 * nc - 1)
    def _():
        st_ref[0, 1] = h_s[...]


def _ssd(p0, dtt, cwx, cbx, cwb, cbb, a_log, dt_bias, d_skip, norm_g, st0, *, L):
    n = p0.shape[0]
    nseq, nc = n // L, L // CHUNK
    cur, prev, nxt, out = _scan_specs(nc, n // 8)
    per_seq = st0.shape[0] > 1
    st_shape = (1, 2, SSD_HEADS, SSD_HEAD_DIM, SSD_STATE)
    row = lambda v: v.reshape(1, -1)
    col = lambda v: v.reshape(-1, 1)
    small = [cwx, row(cbx), cwb, row(cbb), row(a_log), col(a_log), row(dt_bias), col(dt_bias),
             row(jnp.repeat(d_skip, SSD_HEAD_DIM)), row(norm_g)]
    return pl.pallas_call(
        functools.partial(_ssd_kernel, nc=nc),
        grid=(nseq, 2 * nc),
        in_specs=[pl.BlockSpec((CHUNK, 1024), cur(0)),
                  pl.BlockSpec((8, 1024), prev(1)), pl.BlockSpec((CHUNK, 1024), cur(1)), pl.BlockSpec((8, 1024), nxt(1)),
                  pl.BlockSpec((8, 512), prev(4)), pl.BlockSpec((CHUNK, 512), cur(4)), pl.BlockSpec((8, 512), nxt(4)),
                  pl.BlockSpec((CHUNK, LANES), cur(28)),
                  pl.BlockSpec((1, 32, CHUNK), lambda b, s: (b * nc + _chunk_index(s, nc), 0, 0))]
        + [_full(v.shape) for v in small]
        + [pl.BlockSpec(st_shape, lambda b, s: (b if per_seq else 0, 0, 0, 0, 0))],
        out_specs=[pl.BlockSpec((CHUNK, SSD_D_INNER), out(0)),
                   pl.BlockSpec(st_shape, lambda b, s: (b, 0, 0, 0, 0))],
        out_shape=[jax.ShapeDtypeStruct((n, SSD_D_INNER), F32),
                   jax.ShapeDtypeStruct((nseq,) + st_shape[1:], F32)],
        scratch_shapes=[pltpu.VMEM((L, SSD_D_INNER), F32), pltpu.VMEM(st_shape[2:], F32)],
        compiler_params=_cp(2), name="ssd_mixer",
    )(p0, p0, p0, p0, p0, p0, p0, p0, dtt, *small, st0)


def _rope_tables(L):
    t = jnp.arange(L)
    r = (t // GRID_W).astype(F32)
    c = (t % GRID_W).astype(F32)
    nf = 16
    inv = ROPE_BASE ** (-jnp.arange(nf, dtype=F32) / nf)
    ang = jnp.concatenate([r[:, None] * inv, c[:, None] * inv], axis=-1)
    ang = jnp.concatenate([ang, ang], axis=-1)
    return jnp.cos(ang), jnp.sin(ang)


def _rope128(x, cos, sin):
    lane = lax.broadcasted_iota(jnp.int32, x.shape, 1)
    rot = jnp.where(lane % 64 < 32, -pltpu.roll(x, 96, 1), pltpu.roll(x, 32, 1))
    return x * cos + rot * sin


def _pad_lanes(x):
    return jnp.concatenate([x, jnp.zeros_like(x)], axis=1)


MLA_QB = 256


def _mla_kernel(*refs, L, Lc, rope, emit):
    refs = list(refs)
    qa_ref, kva_ref, sm_ref, qg_ref, kvg_ref, wqn_ref, wqr_ref, wkk_ref, wkv_ref = refs[:9]
    pos = 9
    if Lc:
        cckv_ref, ckpe_ref = refs[pos:pos + 2]
        pos += 2
    if rope:
        cos_ref, sin_ref = refs[pos:pos + 2]
        pos += 2
    y_ref = refs[pos]
    pos += 1
    if emit:
        ckv_ref = refs[pos]
        pos += 1
    qn_s, qr_s, kn_s, v_s, kpe_s = refs[pos:]
    qb = pl.program_id(1)
    scale = np.float32((MLA_NOPE + MLA_ROPE) ** -0.5)

    @pl.when(qb == 0)
    def _():
        ckv = _rms(kva_ref[...]) * kvg_ref[...]
        if emit:
            ckv_ref[...] = ckv
        kpe = _pad_lanes(sm_ref[:, 0:MLA_ROPE])
        if rope:
            kpe = _rope128(kpe, cos_ref[...], sin_ref[...])
        if Lc:
            ckv = jnp.concatenate([cckv_ref[0], ckv], axis=0)
            kpe = jnp.concatenate([_pad_lanes(ckpe_ref[0]), kpe], axis=0)
        c16 = ckv.astype(BF16)
        kn_s[...] = _dot(c16, wkk_ref[...]).astype(BF16)
        v_s[...] = _dot(c16, wkv_ref[...]).astype(BF16)
        kpe_s[...] = kpe.astype(BF16)

    qn = (_rms(qa_ref[...]) * qg_ref[...]).astype(BF16)
    qn_s[...] = _dot(qn, wqn_ref[...]).astype(BF16)
    qr = _dot(qn, wqr_ref[...])
    if rope:
        q0 = pl.multiple_of(qb * MLA_QB, MLA_QB)
        cos = cos_ref[pl.ds(q0, MLA_QB), :]
        sin = sin_ref[pl.ds(q0, MLA_QB), :]
        for h in range(MLA_HEADS):
            qr_s[:, h * LANES:(h + 1) * LANES] = _rope128(qr[:, h * LANES:(h + 1) * LANES], cos, sin).astype(BF16)
    else:
        qr_s[...] = qr.astype(BF16)

    def head(h, _):
        c0 = pl.multiple_of(h * LANES, LANES)
        s = (_dot_nt(qn_s[:, pl.ds(c0, LANES)], kn_s[:, pl.ds(c0, LANES)])
             + _dot_nt(qr_s[:, pl.ds(c0, LANES)], kpe_s[...])) * scale
        p = jnp.exp(s - jnp.max(s, axis=-1, keepdims=True))
        p = p / jnp.sum(p, axis=-1, keepdims=True)
        y_ref[:, pl.ds(c0, LANES)] = _dot(p.astype(BF16), v_s[:, pl.ds(c0, LANES)])
        return 0

    lax.fori_loop(0, MLA_HEADS, head, 0)


def _mla(p0, qg, kvg, wqn, wqr, wkk, wkv, cache_ckv, cache_kpe, rope_tabs, *, L):
    n = p0.shape[0]
    nseq, nqb = n // L, L // MLA_QB
    Lc = 0 if cache_ckv is None else cache_ckv.shape[1]
    rope = rope_tabs is not None
    emit = cache_ckv is None
    Lk = L + Lc
    args = [p0, p0, p0, qg.reshape(1, -1), kvg.reshape(1, -1), wqn, wqr, wkk, wkv]
    specs = [pl.BlockSpec((MLA_QB, MLA_RANK), lambda b, q: (b * nqb + q, 5)),
             pl.BlockSpec((L, MLA_RANK), lambda b, q: (b, 6)),
             pl.BlockSpec((L, LANES), lambda b, q: (b, 28)),
             _full((1, MLA_RANK)), _full((1, MLA_RANK)), _full(wqn.shape), _full(wqr.shape), _full(wkk.shape),
             _full(wkv.shape)]
    if Lc:
        args += [cache_ckv, cache_kpe]
        specs += [pl.BlockSpec((1, Lc, MLA_RANK), lambda b, q: (b, 0, 0)),
                  pl.BlockSpec((1, Lc, MLA_ROPE), lambda b, q: (b, 0, 0))]
    if rope:
        args += list(rope_tabs)
        specs += [_full((L, LANES)), _full((L, LANES))]
    out_specs = [pl.BlockSpec((MLA_QB, 1024), lambda b, q: (b * nqb + q, 0))]
    out_shape = [jax.ShapeDtypeStruct((n, 1024), F32)]
    if emit:
        out_specs.append(pl.BlockSpec((L, MLA_RANK), lambda b, q: (b, 0)))
        out_shape.append(jax.ShapeDtypeStruct((n, MLA_RANK), F32))
    return pl.pallas_call(
        functools.partial(_mla_kernel, L=L, Lc=Lc, rope=rope, emit=emit),
        grid=(nseq, nqb), in_specs=specs, out_specs=out_specs, out_shape=out_shape,
        scratch_shapes=[pltpu.VMEM((MLA_QB, 1024), BF16), pltpu.VMEM((MLA_QB, 1024), BF16),
                        pltpu.VMEM((Lk, 1024), BF16), pltpu.VMEM((Lk, 1024), BF16), pltpu.VMEM((Lk, LANES), BF16)],
        compiler_params=_cp(2), name="mla_attention",
    )(*args)


def _gqa_ctx_kernel(q_ref, k_ref, v_ref, sink_ref, y_ref):
    q = q_ref[...]
    k = k_ref[...].astype(BF16)
    v = v_ref[...].astype(BF16)
    scale = np.float32(GQA_HEAD_DIM ** -0.5)
    group = GQA_HEADS // GQA_KV_HEADS
    outs = []
    for h in range(GQA_HEADS):
        kh = h // group
        ks = slice(kh * GQA_HEAD_DIM, (kh + 1) * GQA_HEAD_DIM)
        s = _dot_nt(q[:, h * GQA_HEAD_DIM:(h + 1) * GQA_HEAD_DIM].astype(BF16), k[:, ks]) * scale
        m = jnp.maximum(jnp.max(s, axis=-1, keepdims=True), sink_ref[h])
        p = jnp.exp(s - m)
        p = p / (jnp.sum(p, axis=-1, keepdims=True) + jnp.exp(sink_ref[h] - m))
        outs.append(_dot(p.astype(BF16), v[:, ks]))
    y_ref[...] = jnp.concatenate(outs, axis=1)


def _gqa_ctx(p1, sink, *, L):
    n = p1.shape[0]
    return pl.pallas_call(
        _gqa_ctx_kernel,
        grid=(n // L,),
        in_specs=[pl.BlockSpec((L, 1024), lambda b: (b, 2)), pl.BlockSpec((L, 256), lambda b: (b, 20)),
                  pl.BlockSpec((L, 256), lambda b: (b, 21)), pl.BlockSpec(memory_space=pltpu.SMEM)],
        out_specs=pl.BlockSpec((L, 1024), lambda b: (b, 0)),
        out_shape=jax.ShapeDtypeStruct((n, 1024), F32),
        compiler_params=_cp(1), name="gqa_context",
    )(p1, p1, p1, sink)


GQA_QB = 128
GQA_SPAN = GQA_QB + 2 * WINDOW


def _gqa_win_kernel(q_ref, k_ref, v_ref, ck_ref, cv_ref, cos_ref, sin_ref, sink_ref, y_ref, kp_s, vp_s, *, L):
    qb = pl.program_id(1)
    scale = np.float32(GQA_HEAD_DIM ** -0.5)
    group = GQA_HEADS // GQA_KV_HEADS

    @pl.when(qb == 0)
    def _():
        zeros = jnp.zeros((WINDOW, 256), BF16)
        kp_s[0:WINDOW, :] = zeros
        kp_s[WINDOW + L:, :] = zeros
        vp_s[0:WINDOW, :] = zeros
        vp_s[WINDOW + L:, :] = zeros
        for j in range(2):
            cs = slice(j * LANES, (j + 1) * LANES)
            kp_s[WINDOW:WINDOW + L, cs] = _rope128(k_ref[:, cs], cos_ref[...], sin_ref[...]).astype(BF16)
        vp_s[WINDOW:WINDOW + L, :] = v_ref[...].astype(BF16)

    q0 = pl.multiple_of(qb * GQA_QB, GQA_QB)
    cos = cos_ref[pl.ds(q0, GQA_QB), :]
    sin = sin_ref[pl.ds(q0, GQA_QB), :]
    kw = kp_s[pl.ds(q0, GQA_SPAN), :]
    vw = vp_s[pl.ds(q0, GQA_SPAN), :]
    kc = ck_ref[0].astype(BF16)
    vc = cv_ref[0].astype(BF16)
    qpos = q0 + lax.broadcasted_iota(jnp.int32, (GQA_QB, GQA_SPAN), 0)
    kpos = q0 - WINDOW + lax.broadcasted_iota(jnp.int32, (GQA_QB, GQA_SPAN), 1)
    bias = jnp.where(kpos < 0, NEG_INF, jnp.where(kpos >= L, NEG_INF, jnp.where(jnp.abs(qpos - kpos) <= WINDOW, 0.0, NEG_INF)))
    outs = []
    for j in range(GQA_HEADS * GQA_HEAD_DIM // LANES):
        qj = _rope128(q_ref[:, j * LANES:(j + 1) * LANES], cos, sin).astype(BF16)
        for half in range(2):
            h = 2 * j + half
            kh = h // group
            ks = slice(kh * GQA_HEAD_DIM, (kh + 1) * GQA_HEAD_DIM)
            qh = qj[:, half * GQA_HEAD_DIM:(half + 1) * GQA_HEAD_DIM]
            s_loc = _dot_nt(qh, kw[:, ks]) * scale + bias
            s_ctx = _dot_nt(qh, kc[:, ks]) * scale
            m = jnp.maximum(jnp.maximum(jnp.max(s_loc, axis=-1, keepdims=True), jnp.max(s_ctx, axis=-1, keepdims=True)),
                            sink_ref[h])
            p_loc = jnp.exp(s_loc - m)
            p_ctx = jnp.exp(s_ctx - m)
            inv = 1.0 / (jnp.sum(p_loc, axis=-1, keepdims=True) + jnp.sum(p_ctx, axis=-1, keepdims=True)
                         + jnp.exp(sink_ref[h] - m))
            outs.append(_dot((p_loc * inv).astype(BF16), vw[:, ks]) + _dot((p_ctx * inv).astype(BF16), vc[:, ks]))
    y_ref[...] = jnp.concatenate(outs, axis=1)


def _gqa_win(p1, cache_k, cache_v, cos, sin, sink, *, L):
    n = p1.shape[0]
    nseq, nqb = n // L, L // GQA_QB
    Lc = cache_k.shape[1]
    return pl.pallas_call(
        functools.partial(_gqa_win_kernel, L=L),
        grid=(nseq, nqb),
        in_specs=[pl.BlockSpec((GQA_QB, 1024), lambda b, q: (b * nqb + q, 2)),
                  pl.BlockSpec((L, 256), lambda b, q: (b, 20)), pl.BlockSpec((L, 256), lambda b, q: (b, 21)),
                  pl.BlockSpec((1, Lc, 256), lambda b, q: (b, 0, 0)), pl.BlockSpec((1, Lc, 256), lambda b, q: (b, 0, 0)),
                  _full((L, LANES)), _full((L, LANES)), pl.BlockSpec(memory_space=pltpu.SMEM)],
        out_specs=pl.BlockSpec((GQA_QB, 1024), lambda b, q: (b * nqb + q, 0)),
        out_shape=jax.ShapeDtypeStruct((n, 1024), F32),
        scratch_shapes=[pltpu.VMEM((L + 2 * WINDOW, 256), BF16), pltpu.VMEM((L + 2 * WINDOW, 256), BF16)],
        compiler_params=_cp(2), name="gqa_window",
    )(p1, p1, p1, cache_k, cache_v, cos, sin, sink)


def _mlstm_kernel(qp_ref, qc_ref, qn_ref, v_ref, o_ref, sm_ref, gt_ref, cw_ref, cb_ref, brow_ref, bcol_ref, ng_ref,
                  c0_ref, n0_ref, m0_ref, y_ref, cst_ref, nst_ref, mst_ref, hf_s, c_s, n_s, m_s, *, nc):
    s = pl.program_id(1)
    bwd = s >= nc
    cidx = _chunk_index(s, nc)
    nh = ML_HEADS
    hd = ML_HEAD_DIM

    @pl.when(s == 0)
    def _():
        c_s[...] = c0_ref[0, 0]
        n_s[...] = n0_ref[0, 0]
        m_s[...] = m0_ref[0, 0]

    @pl.when(s == nc)
    def _():
        cst_ref[0, 0] = c_s[...]
        nst_ref[0, 0] = n_s[...]
        mst_ref[0, 0] = m_s[...]
        c_s[...] = c0_ref[0, 1]
        n_s[...] = n0_ref[0, 1]
        m_s[...] = m0_ref[0, 1]

    qk = _conv_silu(qp_ref[...], qc_ref[...], qn_ref[...], cw_ref, cb_ref, cidx == 0, cidx == nc - 1)
    gates = sm_ref[:, 0:4 * nh] + brow_ref[...]
    gates_t = gt_ref[0] + bcol_ref[...]
    li = jnp.where(bwd, gates[:, nh:2 * nh], gates[:, 0:nh])
    li_t = jnp.where(bwd, gates_t[nh:2 * nh, :], gates_t[0:nh, :])
    lf = -_softplus(-jnp.where(bwd, gates[:, 3 * nh:4 * nh], gates[:, 2 * nh:3 * nh]))
    lf_t = -_softplus(-jnp.where(bwd, gates_t[3 * nh:4 * nh, :], gates_t[2 * nh:3 * nh, :]))
    tsel, tsel_t = _tri(bwd)
    bc = _dot_f32(tsel, lf)
    bc_t = _dot_f32(lf_t, tsel_t)
    tot = jnp.where(bwd, bc[0:1, :], bc[CHUNK - 1:CHUNK, :])
    mask = tsel > 0.0
    kscale = np.float32(hd ** -0.5)

    hs = []
    for h in range(nh):
        cs = slice(h * hd, (h + 1) * hd)
        q = qk[:, cs]
        k = qk[:, ML_D + h * hd:ML_D + (h + 1) * hd] * kscale
        q16, k16 = q.astype(BF16), k.astype(BF16)
        v = v_ref[:, cs]
        bq = bc[:, h:h + 1]
        m_prev = m_s[h:h + 1, 0:1]
        log_d = jnp.where(mask, bq - bc_t[h:h + 1, :] + li_t[h:h + 1, :], NEG_INF)
        log_inter = bq + m_prev
        m_out = jnp.maximum(log_inter, jnp.max(log_d, axis=1, keepdims=True))
        sd = _dot_nt(q16, k16) * jnp.exp(log_d - m_out)
        w_inter = jnp.exp(log_inter - m_out)
        c_prev = c_s[h]
        n_prev = n_s[h:h + 1, :]
        num = _dot(sd.astype(BF16), v.astype(BF16)) + w_inter * _dot_nt(q16, c_prev.astype(BF16))
        den = jnp.sum(sd, axis=1, keepdims=True) + w_inter * jnp.sum(q * n_prev, axis=1, keepdims=True)
        hs.append(num / jnp.maximum(jnp.abs(den), jnp.exp(-m_out)))
        th = tot[:, h:h + 1]
        end_inter = th + m_prev
        end_intra = th - bq + li[:, h:h + 1]
        m_new = jnp.maximum(end_inter, jnp.max(end_intra, axis=0, keepdims=True))
        w_c = jnp.exp(end_inter - m_new)
        w_k = jnp.exp(end_intra - m_new)
        c_s[h] = w_c * c_prev + _dot_tn((v * w_k).astype(BF16), k16)
        n_s[h:h + 1, :] = w_c * n_prev + jnp.sum(k * w_k, axis=0, keepdims=True)
        m_s[h:h + 1, :] = jnp.broadcast_to(m_new, (1, hd))
    hcat = jnp.concatenate(hs, axis=1)
    r0 = pl.multiple_of(cidx * CHUNK, CHUNK)

    @pl.when(jnp.logical_not(bwd))
    def _():
        hf_s[pl.ds(r0, CHUNK), :] = hcat

    @pl.when(bwd)
    def _():
        tot_h = hf_s[pl.ds(r0, CHUNK), :] + hcat
        normed = jnp.concatenate([_rms(tot_h[:, h * hd:(h + 1) * hd]) for h in range(nh)], axis=1) * ng_ref[...]
        y_ref[...] = normed * jax.nn.sigmoid(o_ref[...])

    @pl.when(s == 2 * nc - 1)
    def _():
        cst_ref[0, 1] = c_s[...]
        nst_ref[0, 1] = n_s[...]
        mst_ref[0, 1] = m_s[...]


def _mlstm(p1, gt, conv_w, conv_b, ig_b, fg_b, norm_g, c0, n0, m0, *, L):
    n = p1.shape[0]
    nseq, nc = n // L, L // CHUNK
    cur, prev, nxt, out = _scan_specs(nc, n // 8)
    per_seq = c0.shape[0] > 1
    gb = jnp.concatenate([ig_b.reshape(-1), fg_b.reshape(-1)])
    small = [conv_w, conv_b.reshape(1, -1), gb.reshape(1, -1), gb.reshape(-1, 1), norm_g.reshape(1, -1)]
    cshape = (1, 2, ML_HEADS, ML_HEAD_DIM, ML_HEAD_DIM)
    nshape = (1, 2, ML_HEADS, ML_HEAD_DIM)
    st_in = lambda shape: pl.BlockSpec(shape, lambda b, s: (b if per_seq else 0,) + (0,) * (len(shape) - 1))
    st_out = lambda shape: pl.BlockSpec(shape, lambda b, s: (b,) + (0,) * (len(shape) - 1))
    return pl.pallas_call(
        functools.partial(_mlstm_kernel, nc=nc),
        grid=(nseq, 2 * nc),
        in_specs=[pl.BlockSpec((8, 2048), prev(0)), pl.BlockSpec((CHUNK, 2048), cur(0)), pl.BlockSpec((8, 2048), nxt(0)),
                  pl.BlockSpec((CHUNK, 1024), cur(3)), pl.BlockSpec((CHUNK, 1024), cur(4)),
                  pl.BlockSpec((CHUNK, LANES), cur(44)),
                  pl.BlockSpec((1, 32, CHUNK), lambda b, s: (b * nc + _chunk_index(s, nc), 0, 0))]
        + [_full(v.shape) for v in small] + [st_in(cshape), st_in(nshape), st_in(nshape)],
        out_specs=[pl.BlockSpec((CHUNK, ML_D), out(0)), st_out(cshape), st_out(nshape), st_out(nshape)],
        out_shape=[jax.ShapeDtypeStruct((n, ML_D), F32), jax.ShapeDtypeStruct((nseq,) + cshape[1:], F32),
                   jax.ShapeDtypeStruct((nseq,) + nshape[1:], F32), jax.ShapeDtypeStruct((nseq,) + nshape[1:], F32)],
        scratch_shapes=[pltpu.VMEM((L, ML_D), F32), pltpu.VMEM(cshape[2:], F32), pltpu.VMEM(nshape[2:], F32),
                        pltpu.VMEM(nshape[2:], F32)],
        compiler_params=_cp(2), name="mlstm_mixer",
    )(p1, p1, p1, p1, p1, p1, gt, *small, c0, n0, m0)


def _prep_layer0(w_in, w_qb, w_kvb, w_out):
    pad = jnp.zeros((D_MODEL, L0_COLS - 3680), F32)
    w = jnp.concatenate([w_in[:, :2560], w_in[:, 2592:3680], w_in[:, 2560:2592], pad], axis=1).astype(BF16)
    q3 = w_qb.reshape(MLA_RANK, MLA_HEADS, MLA_NOPE + MLA_ROPE)
    wqn = q3[:, :, :MLA_NOPE].reshape(MLA_RANK, -1).astype(BF16)
    wqr = jnp.concatenate([q3[:, :, MLA_NOPE:], jnp.zeros((MLA_RANK, MLA_HEADS, LANES - MLA_ROPE), F32)], axis=-1)
    wqr = wqr.reshape(MLA_RANK, -1).astype(BF16)
    k3 = w_kvb.reshape(MLA_RANK, MLA_HEADS, MLA_NOPE + MLA_V)
    wkk = k3[:, :, :MLA_NOPE].reshape(MLA_RANK, -1).astype(BF16)
    wkv = k3[:, :, MLA_NOPE:].reshape(MLA_RANK, -1).astype(BF16)
    return w, wqn, wqr, wkk, wkv, w_out.astype(BF16)


def _prep_layer1(w_in, w_out):
    pad = jnp.zeros((D_MODEL, L1_COLS - 5664), F32)
    w = jnp.concatenate([w_in[:, 1536:3584], w_in[:, 0:1024], w_in[:, 3584:5632], w_in[:, 1024:1536],
                         w_in[:, 5632:5664], pad], axis=1).astype(BF16)
    return w, w_out.astype(BF16)


def _prep_peer(wq, keys, u, v):
    return (wq.T.astype(BF16), keys.reshape(PEER_HEADS * 2, PEER_NKEYS, -1).astype(BF16), u.astype(BF16),
            v.T.astype(BF16))


def _chunk_transposed(cols):
    return cols.reshape(-1, CHUNK, cols.shape[1]).transpose(0, 2, 1)


def _trunk(x, L, mod0, mod1, w0, w1, peer0, peer1, p, ctx, rope_tabs, mod_base, rows_per_cond, fg):
    kw = dict(mod_base=mod_base, rows_per_cond=rows_per_cond)
    w_in0, wqn, wqr, wkk, wkv, w_out0 = w0
    w_in1, w_out1 = w1
    nseq = x.shape[0] // L

    p0 = _proj(x, p['l0_norm1_g'], mod0, w_in0, tn=1280, **kw)
    st0 = jnp.zeros((1, 2, SSD_HEADS, SSD_HEAD_DIM, SSD_STATE), F32) if ctx is None else ctx[0]
    cw, cb = p['l0_ssd_conv_w'], p['l0_ssd_conv_b']
    y_ssd, ssd_state = _ssd(p0, _chunk_transposed(p0[:, 3648:3680]), cw[:, :1024], cb[:1024], cw[:, 1024:], cb[1024:],
                            p['l0_ssd_A_log'], p['l0_ssd_dt_bias'], p['l0_ssd_D'], p['l0_ssd_norm_g'], st0, L=L)
    mla_tabs = None
    if rope_tabs is not None:
        z64 = jnp.zeros_like(rope_tabs[0])
        mla_tabs = (jnp.concatenate([rope_tabs[0], z64], axis=1), jnp.concatenate([rope_tabs[1], z64], axis=1))
    mla_out = _mla(p0, p['l0_mla_q_norm_g'], p['l0_mla_kv_norm_g'], wqn, wqr, wkk, wkv,
                   None if ctx is None else ctx[1], None if ctx is None else ctx[2], mla_tabs, L=L)
    x = _outproj(y_ssd, mla_out[0], w_out0, x, mod0, **kw)
    x = _peer(x, p['l0_norm2_g'], mod0, *peer0, fg, final_norm=False, **kw)

    p1 = _proj(x, p['l1_norm1_g'], mod1, w_in1, tn=1920, **kw)
    if ctx is None:
        y_gqa = _gqa_ctx(p1, p['l1_gqa_sink'], L=L)
        c0 = jnp.zeros((1, 2, ML_HEADS, ML_HEAD_DIM, ML_HEAD_DIM), F32)
        n0 = jnp.zeros((1, 2, ML_HEADS, ML_HEAD_DIM), F32)
        m0 = n0
    else:
        gqa_tabs = tuple(jnp.concatenate([t, t], axis=1) for t in rope_tabs)
        y_gqa = _gqa_win(p1, ctx[3].reshape(nseq, -1, 256), ctx[4].reshape(nseq, -1, 256), *gqa_tabs,
                         p['l1_gqa_sink'], L=L)
        c0, n0 = ctx[5], ctx[6]
        m0 = jnp.broadcast_to(ctx[7][..., None], ctx[7].shape + (ML_HEAD_DIM,))
    y_ml, mc, mn, mm = _mlstm(p1, _chunk_transposed(p1[:, 5632:5664]), p['l1_ml_conv_w'], p['l1_ml_conv_b'],
                              p['l1_ml_ig_b'], p['l1_ml_fg_b'], p['l1_ml_norm_g'], c0, n0, m0, L=L)
    x = _outproj(y_gqa, y_ml, w_out1, x, mod1, **kw)
    y = _peer(x, p['l1_norm2_g'], mod1, *peer1, fg, final_norm=True, **kw)
    new = None
    if ctx is None:
        new = (ssd_state, mla_out[1].reshape(nseq, L, MLA_RANK), p0[:, 3584:3648].reshape(nseq, L, MLA_ROPE),
               p1[:, 5120:5376].reshape(nseq, L, GQA_KV_HEADS, GQA_HEAD_DIM),
               p1[:, 5376:5632].reshape(nseq, L, GQA_KV_HEADS, GQA_HEAD_DIM), mc, mn, mm[..., 0])
    return y, new


def kernel(x_prompt, x_sample, state_l0_ssd, cache_l0_mla_ckv, cache_l0_mla_kpe, cache_l1_gqa_k, cache_l1_gqa_v, state_l1_mlstm_C, state_l1_mlstm_n, state_l1_mlstm_m, c, c_ctx, final_norm_g, l0_ada_w, l0_ada_b, l0_norm1_g, l0_norm2_g, l0_w_in, l0_ssd_conv_w, l0_ssd_conv_b, l0_ssd_A_log, l0_ssd_dt_bias, l0_ssd_D, l0_ssd_norm_g, l0_mla_q_norm_g, l0_mla_w_qb, l0_mla_kv_norm_g, l0_mla_w_kvb, l0_w_out, l0_peer_wq, l0_peer_keys, l0_peer_u, l0_peer_v, l1_ada_w, l1_ada_b, l1_norm1_g, l1_norm2_g, l1_w_in, l1_gqa_sink, l1_ml_conv_w, l1_ml_conv_b, l1_ml_ig_b, l1_ml_fg_b, l1_ml_norm_g, l1_w_out, l1_peer_wq, l1_peer_keys, l1_peer_u, l1_peer_v):
    p = dict(l0_norm1_g=l0_norm1_g, l0_norm2_g=l0_norm2_g, l0_ssd_conv_w=l0_ssd_conv_w, l0_ssd_conv_b=l0_ssd_conv_b,
             l0_ssd_A_log=l0_ssd_A_log, l0_ssd_dt_bias=l0_ssd_dt_bias, l0_ssd_D=l0_ssd_D, l0_ssd_norm_g=l0_ssd_norm_g,
             l0_mla_q_norm_g=l0_mla_q_norm_g, l0_mla_kv_norm_g=l0_mla_kv_norm_g, l1_norm1_g=l1_norm1_g,
             l1_norm2_g=l1_norm2_g, l1_gqa_sink=l1_gqa_sink, l1_ml_conv_w=l1_ml_conv_w, l1_ml_conv_b=l1_ml_conv_b,
             l1_ml_ig_b=l1_ml_ig_b, l1_ml_fg_b=l1_ml_fg_b, l1_ml_norm_g=l1_ml_norm_g)
    nb, seq = x_prompt.shape[:2]
    db, dseq = x_sample.shape[:2]
    assert db <= 7 and seq % MLA_QB == 0 and dseq % MLA_QB == 0

    cond8 = jnp.zeros((8, D_MODEL), F32).at[0].set(c_ctx).at[1:1 + db].set(c)
    mod0 = _ada(cond8, l0_ada_w, l0_ada_b).reshape(8, 1, 6 * D_MODEL)
    mod1 = _ada(cond8, l1_ada_w, l1_ada_b).reshape(8, 1, 6 * D_MODEL)
    w0 = _prep_layer0(l0_w_in, l0_mla_w_qb, l0_mla_w_kvb, l0_w_out)
    w1 = _prep_layer1(l1_w_in, l1_w_out)
    peer0 = _prep_peer(l0_peer_wq, l0_peer_keys, l0_peer_u, l0_peer_v)
    peer1 = _prep_peer(l1_peer_wq, l1_peer_keys, l1_peer_u, l1_peer_v)

    y_prompt, new = _trunk(x_prompt.reshape(nb * seq, D_MODEL), seq, mod0, mod1, w0, w1, peer0, peer1, p, None, None,
                           0, nb * seq, final_norm_g)
    ctx = (state_l0_ssd, cache_l0_mla_ckv, cache_l0_mla_kpe, cache_l1_gqa_k, cache_l1_gqa_v, state_l1_mlstm_C,
           state_l1_mlstm_n, state_l1_mlstm_m)
    y_sample, _ = _trunk(x_sample.reshape(db * dseq, D_MODEL), dseq, mod0, mod1, w0, w1, peer0, peer1, p, ctx,
                         _rope_tables(dseq), 1, dseq, final_norm_g)
    return (y_prompt.reshape(nb, seq, D_MODEL), y_sample.reshape(db, dseq, D_MODEL)) + new
```

```python
import functools

import jax
import jax.numpy as jnp
import numpy as np
from jax import lax
from jax.experimental import pallas as pl
from jax.experimental.pallas import tpu as pltpu

F32 = jnp.float32
BF16 = jnp.bfloat16
NEG_INF = float("-inf")

D_MODEL = 2048
EPS = 1e-6
ROPE_BASE = 10000.0
GRID_W = 64
CONV_W = 5
CHUNK = 64
SSD_HEADS = 16
SSD_HEAD_DIM = 64
SSD_D_INNER = 1024
SSD_STATE = 128
MLA_HEADS = 8
MLA_NOPE = 128
MLA_ROPE = 64
MLA_V = 128
MLA_RANK = 512
GQA_HEADS = 16
GQA_KV_HEADS = 4
GQA_HEAD_DIM = 64
WINDOW = 128
ML_HEADS = 8
ML_HEAD_DIM = 128
ML_D = 1024
PEER_HEADS = 8
PEER_NKEYS = 128
PEER_N = PEER_NKEYS * PEER_NKEYS
PEER_TOPK = 16

LANES = 128
SUBLANES = 8
VMEM_LIMIT = 56 * 2**20

L0_COLS = 3840
L1_COLS = 5760


def _cp(n_grid, flags=None):
    return pltpu.CompilerParams(dimension_semantics=("arbitrary",) * n_grid, vmem_limit_bytes=VMEM_LIMIT, flags=flags)


def _silu(x):
    return x * jax.nn.sigmoid(x)


def _softplus(x):
    return jnp.maximum(x, 0.0) + jnp.log1p(jnp.exp(-jnp.abs(x)))


def _rms(x):
    return x * lax.rsqrt(jnp.mean(x * x, axis=-1, keepdims=True) + EPS)


def _dot(a, b):
    return jnp.dot(a, b, preferred_element_type=F32)


def _dot_nt(a, b):
    return lax.dot_general(a, b, (((1,), (1,)), ((), ())), preferred_element_type=F32)


def _dot_tn(a, b):
    return lax.dot_general(a, b, (((0,), (0,)), ((), ())), preferred_element_type=F32)


def _dot_f32(a, b):
    return jnp.dot(a, b, preferred_element_type=F32, precision=lax.Precision.HIGHEST)


def _full(shape):
    nd = len(shape)
    return pl.BlockSpec(shape, lambda *_: (0,) * nd)


def _mod_spec(chunk, tb, mod_base, rows_per_cond):
    def index(i, *_):
        return (mod_base + (i * tb) // rows_per_cond, 0, chunk)
    return pl.BlockSpec((1, 1, D_MODEL), index)


def _ada_kernel(c_ref, w_ref, b_ref, o_ref):
    s = _silu(c_ref[...])
    o_ref[...] = _dot(s.astype(BF16), w_ref[...].astype(BF16)) + b_ref[...]


def _ada(cond8, w, b):
    n = w.shape[1]
    tn = 1024
    return pl.pallas_call(
        _ada_kernel,
        grid=(n // tn,),
        in_specs=[_full((8, D_MODEL)), pl.BlockSpec((D_MODEL, tn), lambda j: (0, j)),
                  pl.BlockSpec((1, tn), lambda j: (0, j))],
        out_specs=pl.BlockSpec((8, tn), lambda j: (0, j)),
        out_shape=jax.ShapeDtypeStruct((8, n), F32),
        compiler_params=_cp(1), name="ada_table",
    )(cond8, w, b.reshape(1, n))


def _proj_kernel(x_ref, g_ref, sc_ref, sh_ref, w_ref, o_ref, h_ref):
    @pl.when(pl.program_id(1) == 0)
    def _():
        h = _rms(x_ref[...]) * g_ref[...] * (1.0 + sc_ref[0]) + sh_ref[0]
        h_ref[...] = h.astype(BF16)
    o_ref[...] = _dot(h_ref[...], w_ref[...])


def _proj(x, g, mod, w, *, tn, mod_base, rows_per_cond):
    n, ncol = x.shape[0], w.shape[1]
    tb = 512
    return pl.pallas_call(
        _proj_kernel,
        grid=(n // tb, ncol // tn),
        in_specs=[pl.BlockSpec((tb, D_MODEL), lambda i, j: (i, 0)), _full((1, D_MODEL)),
                  _mod_spec(1, tb, mod_base, rows_per_cond), _mod_spec(0, tb, mod_base, rows_per_cond),
                  pl.BlockSpec((D_MODEL, tn), lambda i, j: (0, j))],
        out_specs=pl.BlockSpec((tb, tn), lambda i, j: (i, j)),
        out_shape=jax.ShapeDtypeStruct((n, ncol), F32),
        scratch_shapes=[pltpu.VMEM((tb, D_MODEL), BF16)],
        compiler_params=_cp(2), name="norm_mod_proj",
    )(x, g.reshape(1, D_MODEL), mod, mod, w)


def _outproj_kernel(ya_ref, yb_ref, w_ref, x_ref, g1_ref, o_ref):
    half = ya_ref.shape[1]
    y = _dot(ya_ref[...].astype(BF16), w_ref[0:half, :]) + _dot(yb_ref[...].astype(BF16), w_ref[half:, :])
    o_ref[...] = x_ref[...] + g1_ref[0] * y


def _outproj(ya, yb, w, x, mod, *, mod_base, rows_per_cond):
    n = x.shape[0]
    tb = 512
    return pl.pallas_call(
        _outproj_kernel,
        grid=(n // tb,),
        in_specs=[pl.BlockSpec((tb, ya.shape[1]), lambda i: (i, 0)), pl.BlockSpec((tb, yb.shape[1]), lambda i: (i, 0)),
                  _full(w.shape), pl.BlockSpec((tb, D_MODEL), lambda i: (i, 0)),
                  _mod_spec(2, tb, mod_base, rows_per_cond)],
        out_specs=pl.BlockSpec((tb, D_MODEL), lambda i: (i, 0)),
        out_shape=jax.ShapeDtypeStruct((n, D_MODEL), F32),
        compiler_params=_cp(1), name="out_proj_residual",
    )(ya, yb, w, x, mod)


ROUTE_TB = 256
_CAND_KEEP = (16, 8, 5, 4, 3, 2, 2, 2)


def _sort_pairs(n):
    pairs = []

    def merge(lo, hi, r):
        step = r * 2
        if step < hi - lo:
            merge(lo, hi, step)
            merge(lo + r, hi, step)
            pairs.extend((i, i + r) for i in range(lo + r, hi - r, step))
        else:
            pairs.append((lo, lo + r))

    def sort(lo, hi):
        if hi - lo >= 1:
            mid = lo + (hi - lo) // 2
            sort(lo, mid)
            sort(mid + 1, hi)
            merge(lo, hi, 1)

    sort(0, 15)
    return [(i, j) for i, j in pairs if j < n]


def _top_values(x, count):
    groups = [x[r:r + SUBLANES] for r in range(0, x.shape[0], SUBLANES)]
    n = len(groups)
    for i, j in _sort_pairs(n):
        groups[i], groups[j] = jnp.maximum(groups[i], groups[j]), jnp.minimum(groups[i], groups[j])
    sub = lax.broadcasted_iota(jnp.int32, groups[0].shape, 0)
    tops = []
    for t in range(count):
        head = groups[0]
        m = jnp.max(head, axis=0, keepdims=True)
        tops.append(m)
        if t == count - 1:
            break
        first = jnp.min(jnp.where(head == m, sub, SUBLANES), axis=0, keepdims=True)
        popped = sub == first
        depth = min(count - 1 - t, n)
        for r in range(depth):
            below = groups[r + 1] if r + 1 < n else NEG_INF
            groups[r] = jnp.where(popped, below, groups[r])
    return tops


def _route_kernel(x_ref, g_ref, sc_ref, sh_ref, wqt_ref, keys_ref,
                  hb_ref, s1_ref, s2_ref, e2_ref, c1_ref, tau_ref, qt_s, top_s):
    tb = x_ref.shape[0]
    h = _rms(x_ref[...]) * g_ref[...] * (1.0 + sc_ref[0]) + sh_ref[0]
    hb = h.astype(BF16)
    hb_ref[...] = hb
    qt_s[...] = _dot_nt(wqt_ref[...], hb).astype(BF16)

    iota_8 = lax.broadcasted_iota(jnp.int32, (SUBLANES, tb), 0)

    def head(hd, _):
        scores = []
        for c in range(2):
            r0 = pl.multiple_of((hd * 2 + c) * PEER_NKEYS, PEER_NKEYS)
            s = _dot(keys_ref[hd * 2 + c], qt_s[pl.ds(r0, PEER_NKEYS), :])
            scores.append(s)
            for it, m in enumerate(_top_values(s, PEER_TOPK)):
                top_s[c, it:it + 1, :] = m
        a1 = top_s[0]
        a2 = top_s[1]
        pieces = [a1[0:1] + a2, a1[1:2] + a2[0:8]]
        for i in range(2, 8):
            pieces.append(jnp.where(iota_8 < _CAND_KEEP[i], a1[i:i + 1] + a2[0:8], NEG_INF))
        pieces.append(a1[8:16] + a2[0:1])
        cand = jnp.concatenate(pieces, axis=0)
        tau = _top_values(cand, PEER_TOPK)[-1]
        top = a1[0:1] + a2[0:1]
        z = jnp.sum(jnp.where(cand >= tau, jnp.exp(cand - top), 0.0), axis=0, keepdims=True)
        e2 = jnp.exp(scores[1] - a2[0:1])
        c1 = jnp.exp(scores[0] - a1[0:1]) / z
        for t in range(tb // LANES):
            ts = slice(t * LANES, (t + 1) * LANES)
            s1_ref[hd, t] = scores[0][:, ts]
            s2_ref[hd, t] = scores[1][:, ts]
            e2_ref[hd, t] = e2[:, ts]
            c1_ref[hd, t] = c1[:, ts]
            tau_ref[t, pl.ds(hd, 1), :] = tau[:, ts]
        return 0

    lax.fori_loop(0, PEER_HEADS, head, 0)


def _route(x, g, mod, wqt, keys, *, mod_base, rows_per_cond):
    n = x.shape[0]
    tb = ROUTE_TB
    ntg = tb // LANES
    hk = (PEER_HEADS, n // LANES, PEER_NKEYS, LANES)
    rspec = pl.BlockSpec((PEER_HEADS, ntg, PEER_NKEYS, LANES), lambda i: (0, i, 0, 0))
    return pl.pallas_call(
        _route_kernel,
        grid=(n // tb,),
        in_specs=[pl.BlockSpec((tb, D_MODEL), lambda i: (i, 0)), _full((1, D_MODEL)),
                  _mod_spec(4, tb, mod_base, rows_per_cond), _mod_spec(3, tb, mod_base, rows_per_cond),
                  _full(wqt.shape), _full(keys.shape)],
        out_specs=[pl.BlockSpec((tb, D_MODEL), lambda i: (i, 0)), rspec, rspec, rspec, rspec,
                   pl.BlockSpec((ntg, PEER_HEADS, LANES), lambda i: (i, 0, 0))],
        out_shape=[jax.ShapeDtypeStruct((n, D_MODEL), BF16)] + [jax.ShapeDtypeStruct(hk, F32)] * 4
        + [jax.ShapeDtypeStruct((n // LANES, PEER_HEADS, LANES), F32)],
        scratch_shapes=[pltpu.VMEM((PEER_HEADS * 2 * PEER_NKEYS, tb), BF16), pltpu.VMEM((2, PEER_TOPK, tb), F32)],
        compiler_params=_cp(1), name="peer_route",
    )(x, g.reshape(1, D_MODEL), mod, mod, wqt, keys)


PEER_TB = 512
PEER_EC = 512


def _experts_kernel(hb_ref, s1_ref, s2_ref, e2_ref, c1_ref, tau_ref, u_ref, vt_ref, x_ref, g2_ref, fg_ref,
                    o_ref, acc_s, st_s, a_s, *, final_norm):
    e = pl.program_id(1)
    tb = hb_ref.shape[0]
    ntg = tb // LANES
    rows_per_step = PEER_EC // PEER_NKEYS

    @pl.when(e == 0)
    def _():
        acc_s[...] = jnp.zeros_like(acc_s)

    st = _dot_nt(u_ref[...], hb_ref[...])
    for tg in range(ntg):
        st_s[tg] = st[:, tg * LANES:(tg + 1) * LANES]

    groups = SUBLANES // rows_per_step
    g8 = pl.multiple_of((e // groups) * SUBLANES, SUBLANES)
    part = e % groups

    def pick(rows8, k):
        row = rows8[k:k + 1]
        for m in range(1, groups):
            row = jnp.where(part == m, rows8[m * rows_per_step + k:m * rows_per_step + k + 1], row)
        return row

    def tile(tg, _):
        s1g = [s1_ref[hd, tg, pl.ds(g8, SUBLANES), :] for hd in range(PEER_HEADS)]
        c1g = [c1_ref[hd, tg, pl.ds(g8, SUBLANES), :] for hd in range(PEER_HEADS)]
        taus = tau_ref[tg]
        for k in range(rows_per_step):
            ks = slice(k * PEER_NKEYS, (k + 1) * PEER_NKEYS)
            w = jnp.zeros((PEER_NKEYS, LANES), F32)
            for hd in range(PEER_HEADS):
                val = pick(s1g[hd], k) + s2_ref[hd, tg]
                sel = jnp.where(val >= taus[hd:hd + 1], e2_ref[hd, tg], 0.0)
                w = w + sel * pick(c1g[hd], k)
            s = st_s[tg, ks, :]
            act = 0.5 * s * (1.0 + lax.erf(s * np.float32(2.0 ** -0.5)))
            a_s[tg, ks, :] = (act * w).astype(BF16)
        return 0

    lax.fori_loop(0, ntg, tile, 0)
    act_all = jnp.concatenate([a_s[tg] for tg in range(ntg)], axis=1)
    acc_s[...] += _dot(vt_ref[...], act_all)

    @pl.when(e == pl.num_programs(1) - 1)
    def _():
        y = x_ref[...] + g2_ref[0] * acc_s[...].T
        if final_norm:
            y = _rms(y) * fg_ref[...]
        o_ref[...] = y


def _experts(hb, s1, s2, e2, c1, tau, u, vt, x, mod, fg, *, mod_base, rows_per_cond, final_norm):
    n = x.shape[0]
    tb = PEER_TB
    ntg = tb // LANES
    rspec = pl.BlockSpec((PEER_HEADS, ntg, PEER_NKEYS, LANES), lambda i, e: (0, i, 0, 0))
    return pl.pallas_call(
        functools.partial(_experts_kernel, final_norm=final_norm),
        grid=(n // tb, PEER_N // PEER_EC),
        in_specs=[pl.BlockSpec((tb, D_MODEL), lambda i, e: (i, 0)), rspec, rspec, rspec, rspec,
                  pl.BlockSpec((ntg, PEER_HEADS, LANES), lambda i, e: (i, 0, 0)),
                  pl.BlockSpec((PEER_EC, D_MODEL), lambda i, e: (e, 0)),
                  pl.BlockSpec((D_MODEL, PEER_EC), lambda i, e: (0, e)),
                  pl.BlockSpec((tb, D_MODEL), lambda i, e: (i, 0)),
                  _mod_spec(5, tb, mod_base, rows_per_cond), _full((1, D_MODEL))],
        out_specs=pl.BlockSpec((tb, D_MODEL), lambda i, e: (i, 0)),
        out_shape=jax.ShapeDtypeStruct((n, D_MODEL), F32),
        scratch_shapes=[pltpu.VMEM((D_MODEL, tb), F32), pltpu.VMEM((ntg, PEER_EC, LANES), F32),
                        pltpu.VMEM((ntg, PEER_EC, LANES), BF16)],
        compiler_params=_cp(2), name="peer_experts",
    )(hb, s1, s2, e2, c1, tau, u, vt, x, mod, fg.reshape(1, D_MODEL))


def _peer(x, g, mod, wqt, keys, u, vt, fg, *, mod_base, rows_per_cond, final_norm):
    hb, s1, s2, e2, c1, tau = _route(x, g, mod, wqt, keys, mod_base=mod_base, rows_per_cond=rows_per_cond)
    return _experts(hb, s1, s2, e2, c1, tau, u, vt, x, mod, fg, mod_base=mod_base, rows_per_cond=rows_per_cond,
                    final_norm=final_norm)


def _chunk_index(s, nc):
    return jnp.where(s < nc, s, 2 * nc - 1 - s)


def _scan_group(nseq):
    return 4 if nseq % 4 == 0 else 2


def _scan_specs(nc):
    def cur(col):
        return lambda b, s: (b, _chunk_index(s, nc), col)

    def prev(col):
        return lambda b, s: (b, jnp.maximum(_chunk_index(s, nc) * 8 - 1, 0), col)

    def nxt(col):
        return lambda b, s: (b, jnp.minimum(_chunk_index(s, nc) * 8 + 8, nc * 8 - 1), col)

    def out(col):
        return lambda b, s: (b, jnp.where(s < nc, nc - 1, 2 * nc - 1 - s), col)
    return cur, prev, nxt, out


def _conv_silu(prev, cur, nxt, w_ref, b_ref, first, last):
    win = jnp.concatenate([jnp.where(first, 0.0, prev), cur, jnp.where(last, 0.0, nxt)], axis=1)
    acc = b_ref[...] + win[:, 6:70, :] * w_ref[0:1, :]
    for k in range(1, CONV_W):
        acc = acc + win[:, 6 + k:70 + k, :] * w_ref[k:k + 1, :]
    return _silu(acc)


def _tri(bwd, G):
    r = lax.broadcasted_iota(jnp.int32, (G, CHUNK, CHUNK), 1)
    c = lax.broadcasted_iota(jnp.int32, (G, CHUNK, CHUNK), 2)
    lower = (c <= r).astype(F32)
    upper = (c >= r).astype(F32)
    return jnp.where(bwd, upper, lower), jnp.where(bwd, lower, upper)


def _bdot(a, b, contract, precision=None):
    return lax.dot_general(a, b, (((contract[0],), (contract[1],)), ((0,), (0,))), preferred_element_type=F32,
                           precision=precision)


def _ssd_kernel(z_ref, xp_ref, xc_ref, xn_ref, bp_ref, bc_ref, bn_ref, sm_ref, dtt_ref,
                cwx_ref, cbx_ref, cwb_ref, cbb_ref, arow_ref, acol_ref, brow_ref, bcol_ref, drow_ref, ng_ref,
                st0_ref, y_ref, st_ref, yf_s, h_s, *, nc):
    s = pl.program_id(1)
    bwd = s >= nc
    cidx = _chunk_index(s, nc)
    first = cidx == 0
    last = cidx == nc - 1
    nh = SSD_HEADS

    @pl.when(s == 0)
    def _():
        h_s[...] = st0_ref[:, 0]

    @pl.when(s == nc)
    def _():
        st_ref[:, 0] = h_s[...]
        h_s[...] = st0_ref[:, 1]

    G = z_ref.shape[0]
    tsel, tsel_t = _tri(bwd, G)
    mask = tsel > 0.0
    r0 = pl.multiple_of(cidx * CHUNK, CHUNK)
    hi = lax.Precision.HIGHEST

    xs = _conv_silu(xp_ref[...], xc_ref[...], xn_ref[...], cwx_ref, cbx_ref, first, last)
    bcv = _conv_silu(bp_ref[...], bc_ref[...], bn_ref[...], cwb_ref, cbb_ref, first, last)
    dt_all = _softplus(sm_ref[:, :, 64:96] + brow_ref[...])
    dtt_all = _softplus(dtt_ref[:, 0] + bcol_ref[...])
    a_all = dt_all * (-jnp.exp(arow_ref[...]))
    at_all = dtt_all * (-jnp.exp(acol_ref[...]))
    dt = jnp.where(bwd, dt_all[:, :, nh:2 * nh], dt_all[:, :, 0:nh])
    a = jnp.where(bwd, a_all[:, :, nh:2 * nh], a_all[:, :, 0:nh])
    at = jnp.where(bwd, at_all[:, nh:2 * nh, :], at_all[:, 0:nh, :])
    cum = _bdot(tsel, a, (2, 1), hi)
    cum_t = _bdot(at, tsel_t, (2, 1), hi)
    tot = jnp.where(bwd, cum[:, 0:1, :], cum[:, CHUNK - 1:CHUNK, :])

    ys = []
    for g in range(2):
        bg = bcv[:, :, g * SSD_STATE:(g + 1) * SSD_STATE]
        cg = bcv[:, :, 256 + g * SSD_STATE:256 + (g + 1) * SSD_STATE]
        cg16 = cg.astype(BF16)
        cb = _bdot(cg16, bg.astype(BF16), (2, 2))
        for hh in range(nh // 2):
            h = g * (nh // 2) + hh
            cq = cum[:, :, h:h + 1]
            decay = jnp.exp(jnp.where(mask, cq - cum_t[:, h:h + 1, :], NEG_INF))
            xd = xs[:, :, h * SSD_HEAD_DIM:(h + 1) * SSD_HEAD_DIM] * dt[:, :, h:h + 1]
            xd16 = xd.astype(BF16)
            hprev = h_s[:, h]
            y = (_bdot((cb * decay).astype(BF16), xd16, (2, 1))
                 + _bdot(cg16, hprev.astype(BF16), (2, 2)) * jnp.exp(cq))
            ys.append(y)
            th = tot[:, :, h:h + 1]
            bw = bg * jnp.exp(th - cq)
            h_s[:, h] = hprev * jnp.exp(th) + _bdot(xd16, bw.astype(BF16), (1, 1))
    y = jnp.concatenate(ys, axis=2)

    @pl.when(jnp.logical_not(bwd))
    def _():
        yf_s[:, pl.ds(r0, CHUNK), :] = y

    @pl.when(bwd)
    def _():
        yt = (yf_s[:, pl.ds(r0, CHUNK), :] + y + drow_ref[...] * xs) * _silu(z_ref[...])
        y_ref[...] = _rms(yt) * ng_ref[...]

    @pl.when(s == 2 * nc - 1)
    def _():
        st_ref[:, 1] = h_s[...]


def _ssd(p0, dtt, cwx, cbx, cwb, cbb, a_log, dt_bias, d_skip, norm_g, st0, *, L):
    nseq, nc = p0.shape[0], L // CHUNK
    G = _scan_group(nseq)
    cur, prev, nxt, out = _scan_specs(nc)
    st_shape = (G, 2, SSD_HEADS, SSD_HEAD_DIM, SSD_STATE)
    per_seq = st0 is not None
    if st0 is None:
        st0 = jnp.zeros(st_shape, F32)
    row = lambda v: v.reshape(1, -1)
    col = lambda v: v.reshape(-1, 1)
    small = [cwx, row(cbx), cwb, row(cbb), row(a_log), col(a_log), row(dt_bias), col(dt_bias),
             row(jnp.repeat(d_skip, SSD_HEAD_DIM)), row(norm_g)]
    blk = lambda rows, width, index: pl.BlockSpec((G, rows, width), index)
    return pl.pallas_call(
        functools.partial(_ssd_kernel, nc=nc),
        grid=(nseq // G, 2 * nc),
        in_specs=[blk(CHUNK, 1024, cur(0)),
                  blk(8, 1024, prev(1)), blk(CHUNK, 1024, cur(1)), blk(8, 1024, nxt(1)),
                  blk(8, 512, prev(4)), blk(CHUNK, 512, cur(4)), blk(8, 512, nxt(4)),
                  blk(CHUNK, LANES, cur(28)),
                  pl.BlockSpec((G, 1, 32, CHUNK), lambda b, s: (b, _chunk_index(s, nc), 0, 0))]
        + [_full(v.shape) for v in small]
        + [pl.BlockSpec(st_shape, lambda b, s: (b if per_seq else 0, 0, 0, 0, 0))],
        out_specs=[blk(CHUNK, SSD_D_INNER, out(0)),
                   pl.BlockSpec(st_shape, lambda b, s: (b, 0, 0, 0, 0))],
        out_shape=[jax.ShapeDtypeStruct((nseq, L, SSD_D_INNER), F32),
                   jax.ShapeDtypeStruct((nseq,) + st_shape[1:], F32)],
        scratch_shapes=[pltpu.VMEM((G, L, SSD_D_INNER), F32), pltpu.VMEM(st_shape[:1] + st_shape[2:], F32)],
        compiler_params=_cp(2), name="ssd_mixer",
    )(p0, p0, p0, p0, p0, p0, p0, p0, dtt, *small, st0)


def _rope_tables(L):
    t = jnp.arange(L)
    r = (t // GRID_W).astype(F32)
    c = (t % GRID_W).astype(F32)
    nf = 16
    inv = ROPE_BASE ** (-jnp.arange(nf, dtype=F32) / nf)
    ang = jnp.concatenate([r[:, None] * inv, c[:, None] * inv], axis=-1)
    ang = jnp.concatenate([ang, ang], axis=-1)
    return jnp.cos(ang), jnp.sin(ang)


def _rope128(x, cos, sin):
    lane = lax.broadcasted_iota(jnp.int32, x.shape, 1)
    rot = jnp.where(lane % 64 < 32, -pltpu.roll(x, 96, 1), pltpu.roll(x, 32, 1))
    return x * cos + rot * sin


def _pad_lanes(x):
    return jnp.concatenate([x, jnp.zeros_like(x)], axis=1)


MLA_QB = 256


def _mla_kernel(*refs, L, Lc, rope, emit):
    refs = list(refs)
    qa_ref, kva_ref, sm_ref, qg_ref, kvg_ref, wqn_ref, wqr_ref, wkk_ref, wkv_ref = refs[:9]
    pos = 9
    if Lc:
        cckv_ref, ckpe_ref = refs[pos:pos + 2]
        pos += 2
    if rope:
        cos_ref, sin_ref = refs[pos:pos + 2]
        pos += 2
    y_ref = refs[pos]
    pos += 1
    if emit:
        ckv_ref = refs[pos]
        pos += 1
    qn_s, qr_s, kn_s, v_s, kpe_s = refs[pos:]
    qb = pl.program_id(1)
    scale = np.float32((MLA_NOPE + MLA_ROPE) ** -0.5)

    @pl.when(qb == 0)
    def _():
        ckv = _rms(kva_ref[...]) * kvg_ref[...]
        if emit:
            ckv_ref[...] = ckv
        kpe = _pad_lanes(sm_ref[:, 0:MLA_ROPE])
        if rope:
            kpe = _rope128(kpe, cos_ref[...], sin_ref[...])
        if Lc:
            ckv = jnp.concatenate([cckv_ref[0], ckv], axis=0)
            kpe = jnp.concatenate([_pad_lanes(ckpe_ref[0]), kpe], axis=0)
        c16 = ckv.astype(BF16)
        kn_s[...] = _dot(c16, wkk_ref[...]).astype(BF16)
        v_s[...] = _dot(c16, wkv_ref[...]).astype(BF16)
        kpe_s[...] = kpe.astype(BF16)

    qn = (_rms(qa_ref[...]) * qg_ref[...]).astype(BF16)
    qn_s[...] = _dot(qn, wqn_ref[...]).astype(BF16)
    qr = _dot(qn, wqr_ref[...])
    if rope:
        q0 = pl.multiple_of(qb * MLA_QB, MLA_QB)
        cos = cos_ref[pl.ds(q0, MLA_QB), :]
        sin = sin_ref[pl.ds(q0, MLA_QB), :]
        for h in range(MLA_HEADS):
            qr_s[:, h * LANES:(h + 1) * LANES] = _rope128(qr[:, h * LANES:(h + 1) * LANES], cos, sin).astype(BF16)
    else:
        qr_s[...] = qr.astype(BF16)

    def head(h, _):
        c0 = pl.multiple_of(h * LANES, LANES)
        s = (_dot_nt(qn_s[:, pl.ds(c0, LANES)], kn_s[:, pl.ds(c0, LANES)])
             + _dot_nt(qr_s[:, pl.ds(c0, LANES)], kpe_s[...])) * scale
        p = jnp.exp(s - jnp.max(s, axis=-1, keepdims=True))
        p = p / jnp.sum(p, axis=-1, keepdims=True)
        y_ref[:, pl.ds(c0, LANES)] = _dot(p.astype(BF16), v_s[:, pl.ds(c0, LANES)])
        return 0

    lax.fori_loop(0, MLA_HEADS, head, 0)


def _mla(p0, qg, kvg, wqn, wqr, wkk, wkv, cache_ckv, cache_kpe, rope_tabs, *, L):
    n = p0.shape[0]
    nseq, nqb = n // L, L // MLA_QB
    Lc = 0 if cache_ckv is None else cache_ckv.shape[1]
    rope = rope_tabs is not None
    emit = cache_ckv is None
    Lk = L + Lc
    args = [p0, p0, p0, qg.reshape(1, -1), kvg.reshape(1, -1), wqn, wqr, wkk, wkv]
    specs = [pl.BlockSpec((MLA_QB, MLA_RANK), lambda b, q: (b * nqb + q, 5)),
             pl.BlockSpec((L, MLA_RANK), lambda b, q: (b, 6)),
             pl.BlockSpec((L, LANES), lambda b, q: (b, 28)),
             _full((1, MLA_RANK)), _full((1, MLA_RANK)), _full(wqn.shape), _full(wqr.shape), _full(wkk.shape),
             _full(wkv.shape)]
    if Lc:
        args += [cache_ckv, cache_kpe]
        specs += [pl.BlockSpec((1, Lc, MLA_RANK), lambda b, q: (b, 0, 0)),
                  pl.BlockSpec((1, Lc, MLA_ROPE), lambda b, q: (b, 0, 0))]
    if rope:
        args += list(rope_tabs)
        specs += [_full((L, LANES)), _full((L, LANES))]
    out_specs = [pl.BlockSpec((MLA_QB, 1024), lambda b, q: (b * nqb + q, 0))]
    out_shape = [jax.ShapeDtypeStruct((n, 1024), F32)]
    if emit:
        out_specs.append(pl.BlockSpec((L, MLA_RANK), lambda b, q: (b, 0)))
        out_shape.append(jax.ShapeDtypeStruct((n, MLA_RANK), F32))
    return pl.pallas_call(
        functools.partial(_mla_kernel, L=L, Lc=Lc, rope=rope, emit=emit),
        grid=(nseq, nqb), in_specs=specs, out_specs=out_specs, out_shape=out_shape,
        scratch_shapes=[pltpu.VMEM((MLA_QB, 1024), BF16), pltpu.VMEM((MLA_QB, 1024), BF16),
                        pltpu.VMEM((Lk, 1024), BF16), pltpu.VMEM((Lk, 1024), BF16), pltpu.VMEM((Lk, LANES), BF16)],
        compiler_params=_cp(2), name="mla_attention",
    )(*args)


def _gqa_ctx_kernel(q_ref, k_ref, v_ref, sink_ref, y_ref):
    q = q_ref[...]
    k = k_ref[...].astype(BF16)
    v = v_ref[...].astype(BF16)
    scale = np.float32(GQA_HEAD_DIM ** -0.5)
    group = GQA_HEADS // GQA_KV_HEADS
    outs = []
    for h in range(GQA_HEADS):
        kh = h // group
        ks = slice(kh * GQA_HEAD_DIM, (kh + 1) * GQA_HEAD_DIM)
        s = _dot_nt(q[:, h * GQA_HEAD_DIM:(h + 1) * GQA_HEAD_DIM].astype(BF16), k[:, ks]) * scale
        m = jnp.maximum(jnp.max(s, axis=-1, keepdims=True), sink_ref[h])
        p = jnp.exp(s - m)
        p = p / (jnp.sum(p, axis=-1, keepdims=True) + jnp.exp(sink_ref[h] - m))
        outs.append(_dot(p.astype(BF16), v[:, ks]))
    y_ref[...] = jnp.concatenate(outs, axis=1)


def _gqa_ctx(p1, sink, *, L):
    n = p1.shape[0]
    return pl.pallas_call(
        _gqa_ctx_kernel,
        grid=(n // L,),
        in_specs=[pl.BlockSpec((L, 1024), lambda b: (b, 2)), pl.BlockSpec((L, 256), lambda b: (b, 20)),
                  pl.BlockSpec((L, 256), lambda b: (b, 21)), pl.BlockSpec(memory_space=pltpu.SMEM)],
        out_specs=pl.BlockSpec((L, 1024), lambda b: (b, 0)),
        out_shape=jax.ShapeDtypeStruct((n, 1024), F32),
        compiler_params=_cp(1), name="gqa_context",
    )(p1, p1, p1, sink)


GQA_QB = 128
GQA_SPAN = GQA_QB + 2 * WINDOW


def _gqa_win_kernel(q_ref, k_ref, v_ref, ck_ref, cv_ref, cos_ref, sin_ref, sink_ref, y_ref, kp_s, vp_s, *, L):
    qb = pl.program_id(1)
    scale = np.float32(GQA_HEAD_DIM ** -0.5)
    group = GQA_HEADS // GQA_KV_HEADS

    @pl.when(qb == 0)
    def _():
        zeros = jnp.zeros((WINDOW, 256), BF16)
        kp_s[0:WINDOW, :] = zeros
        kp_s[WINDOW + L:, :] = zeros
        vp_s[0:WINDOW, :] = zeros
        vp_s[WINDOW + L:, :] = zeros
        for j in range(2):
            cs = slice(j * LANES, (j + 1) * LANES)
            kp_s[WINDOW:WINDOW + L, cs] = _rope128(k_ref[:, cs], cos_ref[...], sin_ref[...]).astype(BF16)
        vp_s[WINDOW:WINDOW + L, :] = v_ref[...].astype(BF16)

    q0 = pl.multiple_of(qb * GQA_QB, GQA_QB)
    cos = cos_ref[pl.ds(q0, GQA_QB), :]
    sin = sin_ref[pl.ds(q0, GQA_QB), :]
    kw = kp_s[pl.ds(q0, GQA_SPAN), :]
    vw = vp_s[pl.ds(q0, GQA_SPAN), :]
    kc = ck_ref[0].astype(BF16)
    vc = cv_ref[0].astype(BF16)
    qpos = q0 + lax.broadcasted_iota(jnp.int32, (GQA_QB, GQA_SPAN), 0)
    kpos = q0 - WINDOW + lax.broadcasted_iota(jnp.int32, (GQA_QB, GQA_SPAN), 1)
    bias = jnp.where(kpos < 0, NEG_INF, jnp.where(kpos >= L, NEG_INF, jnp.where(jnp.abs(qpos - kpos) <= WINDOW, 0.0, NEG_INF)))
    outs = []
    for j in range(GQA_HEADS * GQA_HEAD_DIM // LANES):
        qj = _rope128(q_ref[:, j * LANES:(j + 1) * LANES], cos, sin).astype(BF16)
        for half in range(2):
            h = 2 * j + half
            kh = h // group
            ks = slice(kh * GQA_HEAD_DIM, (kh + 1) * GQA_HEAD_DIM)
            qh = qj[:, half * GQA_HEAD_DIM:(half + 1) * GQA_HEAD_DIM]
            s_loc = _dot_nt(qh, kw[:, ks]) * scale + bias
            s_ctx = _dot_nt(qh, kc[:, ks]) * scale
            m = jnp.maximum(jnp.maximum(jnp.max(s_loc, axis=-1, keepdims=True), jnp.max(s_ctx, axis=-1, keepdims=True)),
                            sink_ref[h])
            p_loc = jnp.exp(s_loc - m)
            p_ctx = jnp.exp(s_ctx - m)
            inv = 1.0 / (jnp.sum(p_loc, axis=-1, keepdims=True) + jnp.sum(p_ctx, axis=-1, keepdims=True)
                         + jnp.exp(sink_ref[h] - m))
            outs.append(_dot((p_loc * inv).astype(BF16), vw[:, ks]) + _dot((p_ctx * inv).astype(BF16), vc[:, ks]))
    y_ref[...] = jnp.concatenate(outs, axis=1)


def _gqa_win(p1, cache_k, cache_v, cos, sin, sink, *, L):
    n = p1.shape[0]
    nseq, nqb = n // L, L // GQA_QB
    Lc = cache_k.shape[1]
    return pl.pallas_call(
        functools.partial(_gqa_win_kernel, L=L),
        grid=(nseq, nqb),
        in_specs=[pl.BlockSpec((GQA_QB, 1024), lambda b, q: (b * nqb + q, 2)),
                  pl.BlockSpec((L, 256), lambda b, q: (b, 20)), pl.BlockSpec((L, 256), lambda b, q: (b, 21)),
                  pl.BlockSpec((1, Lc, 256), lambda b, q: (b, 0, 0)), pl.BlockSpec((1, Lc, 256), lambda b, q: (b, 0, 0)),
                  _full((L, LANES)), _full((L, LANES)), pl.BlockSpec(memory_space=pltpu.SMEM)],
        out_specs=pl.BlockSpec((GQA_QB, 1024), lambda b, q: (b * nqb + q, 0)),
        out_shape=jax.ShapeDtypeStruct((n, 1024), F32),
        scratch_shapes=[pltpu.VMEM((L + 2 * WINDOW, 256), BF16), pltpu.VMEM((L + 2 * WINDOW, 256), BF16)],
        compiler_params=_cp(2), name="gqa_window",
    )(p1, p1, p1, cache_k, cache_v, cos, sin, sink)


def _mlstm_kernel(qp_ref, qc_ref, qn_ref, v_ref, o_ref, sm_ref, gt_ref, cw_ref, cb_ref, brow_ref, bcol_ref, ng_ref,
                  c0_ref, n0_ref, m0_ref, y_ref, cst_ref, nst_ref, mst_ref, hf_s, c_s, n_s, m_s, *, nc):
    s = pl.program_id(1)
    bwd = s >= nc
    cidx = _chunk_index(s, nc)
    nh = ML_HEADS
    hd = ML_HEAD_DIM

    @pl.when(s == 0)
    def _():
        c_s[...] = c0_ref[:, 0]
        n_s[...] = n0_ref[:, 0]
        m_s[...] = m0_ref[:, 0]

    @pl.when(s == nc)
    def _():
        cst_ref[:, 0] = c_s[...]
        nst_ref[:, 0] = n_s[...]
        mst_ref[:, 0] = m_s[...]
        c_s[...] = c0_ref[:, 1]
        n_s[...] = n0_ref[:, 1]
        m_s[...] = m0_ref[:, 1]

    G = qc_ref.shape[0]
    tsel, tsel_t = _tri(bwd, G)
    mask = tsel > 0.0
    kscale = np.float32(hd ** -0.5)
    r0 = pl.multiple_of(cidx * CHUNK, CHUNK)
    hi = lax.Precision.HIGHEST

    qk = _conv_silu(qp_ref[...], qc_ref[...], qn_ref[...], cw_ref, cb_ref, cidx == 0, cidx == nc - 1)
    gates = sm_ref[:, :, 0:4 * nh] + brow_ref[...]
    gates_t = gt_ref[:, 0] + bcol_ref[...]
    li = jnp.where(bwd, gates[:, :, nh:2 * nh], gates[:, :, 0:nh])
    li_t = jnp.where(bwd, gates_t[:, nh:2 * nh, :], gates_t[:, 0:nh, :])
    lf = -_softplus(-jnp.where(bwd, gates[:, :, 3 * nh:4 * nh], gates[:, :, 2 * nh:3 * nh]))
    lf_t = -_softplus(-jnp.where(bwd, gates_t[:, 3 * nh:4 * nh, :], gates_t[:, 2 * nh:3 * nh, :]))
    bc = _bdot(tsel, lf, (2, 1), hi)
    bc_t = _bdot(lf_t, tsel_t, (2, 1), hi)
    tot = jnp.where(bwd, bc[:, 0:1, :], bc[:, CHUNK - 1:CHUNK, :])

    hs = []
    for h in range(nh):
        cs = slice(h * hd, (h + 1) * hd)
        q = qk[:, :, cs]
        k = qk[:, :, ML_D + h * hd:ML_D + (h + 1) * hd] * kscale
        q16, k16 = q.astype(BF16), k.astype(BF16)
        v = v_ref[:, :, cs]
        bq = bc[:, :, h:h + 1]
        m_prev = m_s[:, h:h + 1, 0:1]
        log_d = jnp.where(mask, bq - bc_t[:, h:h + 1, :] + li_t[:, h:h + 1, :], NEG_INF)
        log_inter = bq + m_prev
        m_out = jnp.maximum(log_inter, jnp.max(log_d, axis=2, keepdims=True))
        sd = _bdot(q16, k16, (2, 2)) * jnp.exp(log_d - m_out)
        w_inter = jnp.exp(log_inter - m_out)
        c_prev = c_s[:, h]
        n_prev = n_s[:, h:h + 1, :]
        num = _bdot(sd.astype(BF16), v.astype(BF16), (2, 1)) + w_inter * _bdot(q16, c_prev.astype(BF16), (2, 2))
        den = jnp.sum(sd, axis=2, keepdims=True) + w_inter * jnp.sum(q * n_prev, axis=2, keepdims=True)
        hs.append(num / jnp.maximum(jnp.abs(den), jnp.exp(-m_out)))
        th = tot[:, :, h:h + 1]
        end_inter = th + m_prev
        end_intra = th - bq + li[:, :, h:h + 1]
        m_new = jnp.maximum(end_inter, jnp.max(end_intra, axis=1, keepdims=True))
        w_c = jnp.exp(end_inter - m_new)
        w_k = jnp.exp(end_intra - m_new)
        c_s[:, h] = w_c * c_prev + _bdot((v * w_k).astype(BF16), k16, (1, 1))
        n_s[:, h:h + 1, :] = w_c * n_prev + jnp.sum(k * w_k, axis=1, keepdims=True)
        m_s[:, h:h + 1, :] = jnp.broadcast_to(m_new, (G, 1, hd))
    hcat = jnp.concatenate(hs, axis=2)

    @pl.when(jnp.logical_not(bwd))
    def _():
        hf_s[:, pl.ds(r0, CHUNK), :] = hcat

    @pl.when(bwd)
    def _():
        tot_h = hf_s[:, pl.ds(r0, CHUNK), :] + hcat
        normed = jnp.concatenate([_rms(tot_h[:, :, h * hd:(h + 1) * hd]) for h in range(nh)], axis=2) * ng_ref[...]
        y_ref[...] = normed * jax.nn.sigmoid(o_ref[...])

    @pl.when(s == 2 * nc - 1)
    def _():
        cst_ref[:, 1] = c_s[...]
        nst_ref[:, 1] = n_s[...]
        mst_ref[:, 1] = m_s[...]


def _mlstm(p1, gt, conv_w, conv_b, ig_b, fg_b, norm_g, c0, n0, m0, *, L):
    nseq, nc = p1.shape[0], L // CHUNK
    G = _scan_group(nseq)
    cur, prev, nxt, out = _scan_specs(nc)
    cshape = (G, 2, ML_HEADS, ML_HEAD_DIM, ML_HEAD_DIM)
    nshape = (G, 2, ML_HEADS, ML_HEAD_DIM)
    per_seq = c0 is not None
    if c0 is None:
        c0, n0, m0 = jnp.zeros(cshape, F32), jnp.zeros(nshape, F32), jnp.zeros(nshape, F32)
    gb = jnp.concatenate([ig_b.reshape(-1), fg_b.reshape(-1)])
    small = [conv_w, conv_b.reshape(1, -1), gb.reshape(1, -1), gb.reshape(-1, 1), norm_g.reshape(1, -1)]
    st_in = lambda shape: pl.BlockSpec(shape, lambda b, s: (b if per_seq else 0,) + (0,) * (len(shape) - 1))
    st_out = lambda shape: pl.BlockSpec(shape, lambda b, s: (b,) + (0,) * (len(shape) - 1))
    blk = lambda rows, width, index: pl.BlockSpec((G, rows, width), index)
    return pl.pallas_call(
        functools.partial(_mlstm_kernel, nc=nc),
        grid=(nseq // G, 2 * nc),
        in_specs=[blk(8, 2048, prev(0)), blk(CHUNK, 2048, cur(0)), blk(8, 2048, nxt(0)),
                  blk(CHUNK, 1024, cur(3)), blk(CHUNK, 1024, cur(4)), blk(CHUNK, LANES, cur(44)),
                  pl.BlockSpec((G, 1, 32, CHUNK), lambda b, s: (b, _chunk_index(s, nc), 0, 0))]
        + [_full(v.shape) for v in small] + [st_in(cshape), st_in(nshape), st_in(nshape)],
        out_specs=[blk(CHUNK, ML_D, out(0)), st_out(cshape), st_out(nshape), st_out(nshape)],
        out_shape=[jax.ShapeDtypeStruct((nseq, L, ML_D), F32), jax.ShapeDtypeStruct((nseq,) + cshape[1:], F32),
                   jax.ShapeDtypeStruct((nseq,) + nshape[1:], F32), jax.ShapeDtypeStruct((nseq,) + nshape[1:], F32)],
        scratch_shapes=[pltpu.VMEM((G, L, ML_D), F32), pltpu.VMEM(cshape[:1] + cshape[2:], F32),
                        pltpu.VMEM(nshape[:1] + nshape[2:], F32), pltpu.VMEM(nshape[:1] + nshape[2:], F32)],
        compiler_params=_cp(2), name="mlstm_mixer",
    )(p1, p1, p1, p1, p1, p1, gt, *small, c0, n0, m0)


def _prep_layer0(w_in, w_qb, w_kvb, w_out):
    pad = jnp.zeros((D_MODEL, L0_COLS - 3680), F32)
    w = jnp.concatenate([w_in[:, :2560], w_in[:, 2592:3680], w_in[:, 2560:2592], pad], axis=1).astype(BF16)
    q3 = w_qb.reshape(MLA_RANK, MLA_HEADS, MLA_NOPE + MLA_ROPE)
    wqn = q3[:, :, :MLA_NOPE].reshape(MLA_RANK, -1).astype(BF16)
    wqr = jnp.concatenate([q3[:, :, MLA_NOPE:], jnp.zeros((MLA_RANK, MLA_HEADS, LANES - MLA_ROPE), F32)], axis=-1)
    wqr = wqr.reshape(MLA_RANK, -1).astype(BF16)
    k3 = w_kvb.reshape(MLA_RANK, MLA_HEADS, MLA_NOPE + MLA_V)
    wkk = k3[:, :, :MLA_NOPE].reshape(MLA_RANK, -1).astype(BF16)
    wkv = k3[:, :, MLA_NOPE:].reshape(MLA_RANK, -1).astype(BF16)
    return w, wqn, wqr, wkk, wkv, w_out.astype(BF16)


def _prep_layer1(w_in, w_out):
    pad = jnp.zeros((D_MODEL, L1_COLS - 5664), F32)
    w = jnp.concatenate([w_in[:, 1536:3584], w_in[:, 0:1024], w_in[:, 3584:5632], w_in[:, 1024:1536],
                         w_in[:, 5632:5664], pad], axis=1).astype(BF16)
    return w, w_out.astype(BF16)


def _prep_peer(wq, keys, u, v):
    return (wq.T.astype(BF16), keys.reshape(PEER_HEADS * 2, PEER_NKEYS, -1).astype(BF16), u.astype(BF16),
            v.T.astype(BF16))


def _chunk_transposed(cols, L):
    return cols.reshape(-1, L // CHUNK, CHUNK, cols.shape[1]).transpose(0, 1, 3, 2)


def _trunk(x, L, mod0, mod1, w0, w1, peer0, peer1, p, ctx, rope_tabs, mod_base, rows_per_cond, fg):
    kw = dict(mod_base=mod_base, rows_per_cond=rows_per_cond)
    w_in0, wqn, wqr, wkk, wkv, w_out0 = w0
    w_in1, w_out1 = w1
    nseq = x.shape[0] // L

    p0 = _proj(x, p['l0_norm1_g'], mod0, w_in0, tn=1280, **kw)
    cw, cb = p['l0_ssd_conv_w'], p['l0_ssd_conv_b']
    y_ssd, ssd_state = _ssd(p0.reshape(nseq, L, -1), _chunk_transposed(p0[:, 3648:3680], L), cw[:, :1024], cb[:1024],
                            cw[:, 1024:], cb[1024:], p['l0_ssd_A_log'], p['l0_ssd_dt_bias'], p['l0_ssd_D'],
                            p['l0_ssd_norm_g'], None if ctx is None else ctx[0], L=L)
    y_ssd = y_ssd.reshape(nseq * L, -1)
    mla_tabs = None
    if rope_tabs is not None:
        z64 = jnp.zeros_like(rope_tabs[0])
        mla_tabs = (jnp.concatenate([rope_tabs[0], z64], axis=1), jnp.concatenate([rope_tabs[1], z64], axis=1))
    mla_out = _mla(p0, p['l0_mla_q_norm_g'], p['l0_mla_kv_norm_g'], wqn, wqr, wkk, wkv,
                   None if ctx is None else ctx[1], None if ctx is None else ctx[2], mla_tabs, L=L)
    x = _outproj(y_ssd, mla_out[0], w_out0, x, mod0, **kw)
    x = _peer(x, p['l0_norm2_g'], mod0, *peer0, fg, final_norm=False, **kw)

    p1 = _proj(x, p['l1_norm1_g'], mod1, w_in1, tn=1920, **kw)
    if ctx is None:
        y_gqa = _gqa_ctx(p1, p['l1_gqa_sink'], L=L)
        c0 = n0 = m0 = None
    else:
        gqa_tabs = tuple(jnp.concatenate([t, t], axis=1) for t in rope_tabs)
        y_gqa = _gqa_win(p1, ctx[3].reshape(nseq, -1, 256), ctx[4].reshape(nseq, -1, 256), *gqa_tabs,
                         p['l1_gqa_sink'], L=L)
        c0, n0 = ctx[5], ctx[6]
        m0 = jnp.broadcast_to(ctx[7][..., None], ctx[7].shape + (ML_HEAD_DIM,))
    y_ml, mc, mn, mm = _mlstm(p1.reshape(nseq, L, -1), _chunk_transposed(p1[:, 5632:5664], L), p['l1_ml_conv_w'],
                              p['l1_ml_conv_b'], p['l1_ml_ig_b'], p['l1_ml_fg_b'], p['l1_ml_norm_g'], c0, n0, m0, L=L)
    y_ml = y_ml.reshape(nseq * L, -1)
    x = _outproj(y_gqa, y_ml, w_out1, x, mod1, **kw)
    y = _peer(x, p['l1_norm2_g'], mod1, *peer1, fg, final_norm=True, **kw)
    new = None
    if ctx is None:
        new = (ssd_state, mla_out[1].reshape(nseq, L, MLA_RANK), p0[:, 3584:3648].reshape(nseq, L, MLA_ROPE),
               p1[:, 5120:5376].reshape(nseq, L, GQA_KV_HEADS, GQA_HEAD_DIM),
               p1[:, 5376:5632].reshape(nseq, L, GQA_KV_HEADS, GQA_HEAD_DIM), mc, mn, mm[..., 0])
    return y, new


def kernel(x_prompt, x_sample, state_l0_ssd, cache_l0_mla_ckv, cache_l0_mla_kpe, cache_l1_gqa_k, cache_l1_gqa_v, state_l1_mlstm_C, state_l1_mlstm_n, state_l1_mlstm_m, c, c_ctx, final_norm_g, l0_ada_w, l0_ada_b, l0_norm1_g, l0_norm2_g, l0_w_in, l0_ssd_conv_w, l0_ssd_conv_b, l0_ssd_A_log, l0_ssd_dt_bias, l0_ssd_D, l0_ssd_norm_g, l0_mla_q_norm_g, l0_mla_w_qb, l0_mla_kv_norm_g, l0_mla_w_kvb, l0_w_out, l0_peer_wq, l0_peer_keys, l0_peer_u, l0_peer_v, l1_ada_w, l1_ada_b, l1_norm1_g, l1_norm2_g, l1_w_in, l1_gqa_sink, l1_ml_conv_w, l1_ml_conv_b, l1_ml_ig_b, l1_ml_fg_b, l1_ml_norm_g, l1_w_out, l1_peer_wq, l1_peer_keys, l1_peer_u, l1_peer_v):
    p = dict(l0_norm1_g=l0_norm1_g, l0_norm2_g=l0_norm2_g, l0_ssd_conv_w=l0_ssd_conv_w, l0_ssd_conv_b=l0_ssd_conv_b,
             l0_ssd_A_log=l0_ssd_A_log, l0_ssd_dt_bias=l0_ssd_dt_bias, l0_ssd_D=l0_ssd_D, l0_ssd_norm_g=l0_ssd_norm_g,
             l0_mla_q_norm_g=l0_mla_q_norm_g, l0_mla_kv_norm_g=l0_mla_kv_norm_g, l1_norm1_g=l1_norm1_g,
             l1_norm2_g=l1_norm2_g, l1_gqa_sink=l1_gqa_sink, l1_ml_conv_w=l1_ml_conv_w, l1_ml_conv_b=l1_ml_conv_b,
             l1_ml_ig_b=l1_ml_ig_b, l1_ml_fg_b=l1_ml_fg_b, l1_ml_norm_g=l1_ml_norm_g)
    nb, seq = x_prompt.shape[:2]
    db, dseq = x_sample.shape[:2]
    assert db <= 7 and seq % MLA_QB == 0 and dseq % MLA_QB == 0

    cond8 = jnp.zeros((8, D_MODEL), F32).at[0].set(c_ctx).at[1:1 + db].set(c)
    mod0 = _ada(cond8, l0_ada_w, l0_ada_b).reshape(8, 1, 6 * D_MODEL)
    mod1 = _ada(cond8, l1_ada_w, l1_ada_b).reshape(8, 1, 6 * D_MODEL)
    w0 = _prep_layer0(l0_w_in, l0_mla_w_qb, l0_mla_w_kvb, l0_w_out)
    w1 = _prep_layer1(l1_w_in, l1_w_out)
    peer0 = _prep_peer(l0_peer_wq, l0_peer_keys, l0_peer_u, l0_peer_v)
    peer1 = _prep_peer(l1_peer_wq, l1_peer_keys, l1_peer_u, l1_peer_v)

    y_prompt, new = _trunk(x_prompt.reshape(nb * seq, D_MODEL), seq, mod0, mod1, w0, w1, peer0, peer1, p, None, None,
                           0, nb * seq, final_norm_g)
    ctx = (state_l0_ssd, cache_l0_mla_ckv, cache_l0_mla_kpe, cache_l1_gqa_k, cache_l1_gqa_v, state_l1_mlstm_C,
           state_l1_mlstm_n, state_l1_mlstm_m)
    y_sample, _ = _trunk(x_sample.reshape(db * dseq, D_MODEL), dseq, mod0, mod1, w0, w1, peer0, peer1, p, ctx,
                         _rope_tables(dseq), 1, dseq, final_norm_g)
    return (y_prompt.reshape(nb, seq, D_MODEL), y_sample.reshape(db, dseq, D_MODEL)) + new
```

```python
import functools

import jax
import jax.numpy as jnp
import numpy as np
from jax import lax
from jax.experimental import pallas as pl
from jax.experimental.pallas import tpu as pltpu

F32 = jnp.float32
BF16 = jnp.bfloat16
NEG_INF = float("-inf")

D_MODEL = 2048
EPS = 1e-6
ROPE_BASE = 10000.0
GRID_W = 64
CONV_W = 5
CHUNK = 64
SSD_HEADS = 16
SSD_HEAD_DIM = 64
SSD_D_INNER = 1024
SSD_STATE = 128
MLA_HEADS = 8
MLA_NOPE = 128
MLA_ROPE = 64
MLA_V = 128
MLA_RANK = 512
GQA_HEADS = 16
GQA_KV_HEADS = 4
GQA_HEAD_DIM = 64
WINDOW = 128
ML_HEADS = 8
ML_HEAD_DIM = 128
ML_D = 1024
PEER_HEADS = 8
PEER_NKEYS = 128
PEER_N = PEER_NKEYS * PEER_NKEYS
PEER_TOPK = 16

LANES = 128
SUBLANES = 8
VMEM_LIMIT = 56 * 2**20

L0_COLS = 3840
L1_COLS = 5760


def _cp(n_grid, flags=None):
    return pltpu.CompilerParams(dimension_semantics=("arbitrary",) * n_grid, vmem_limit_bytes=VMEM_LIMIT, flags=flags)


def _silu(x):
    return x * jax.nn.sigmoid(x)


def _softplus(x):
    return jnp.maximum(x, 0.0) + jnp.log1p(jnp.exp(-jnp.abs(x)))


def _rms(x):
    return x * lax.rsqrt(jnp.mean(x * x, axis=-1, keepdims=True) + EPS)


def _dot(a, b):
    return jnp.dot(a, b, preferred_element_type=F32)


def _dot_nt(a, b):
    return lax.dot_general(a, b, (((1,), (1,)), ((), ())), preferred_element_type=F32)


def _dot_tn(a, b):
    return lax.dot_general(a, b, (((0,), (0,)), ((), ())), preferred_element_type=F32)


def _dot_f32(a, b):
    return jnp.dot(a, b, preferred_element_type=F32, precision=lax.Precision.HIGHEST)


def _full(shape):
    nd = len(shape)
    return pl.BlockSpec(shape, lambda *_: (0,) * nd)


def _mod_spec(chunk, tb, mod_base, rows_per_cond):
    def index(i, *_):
        return (mod_base + (i * tb) // rows_per_cond, 0, chunk)
    return pl.BlockSpec((1, 1, D_MODEL), index)


def _ada_kernel(c_ref, w_ref, b_ref, o_ref):
    s = _silu(c_ref[...])
    o_ref[...] = _dot(s.astype(BF16), w_ref[...].astype(BF16)) + b_ref[...]


def _ada(cond8, w, b):
    n = w.shape[1]
    tn = 1024
    return pl.pallas_call(
        _ada_kernel,
        grid=(n // tn,),
        in_specs=[_full((8, D_MODEL)), pl.BlockSpec((D_MODEL, tn), lambda j: (0, j)),
                  pl.BlockSpec((1, tn), lambda j: (0, j))],
        out_specs=pl.BlockSpec((8, tn), lambda j: (0, j)),
        out_shape=jax.ShapeDtypeStruct((8, n), F32),
        compiler_params=_cp(1), name="ada_table",
    )(cond8, w, b.reshape(1, n))


def _proj_kernel(x_ref, g_ref, sc_ref, sh_ref, w_ref, o_ref, h_ref):
    @pl.when(pl.program_id(1) == 0)
    def _():
        h = _rms(x_ref[...]) * g_ref[...] * (1.0 + sc_ref[0]) + sh_ref[0]
        h_ref[...] = h.astype(BF16)
    o_ref[...] = _dot(h_ref[...], w_ref[...])


def _proj(x, g, mod, w, *, tn, mod_base, rows_per_cond):
    n, ncol = x.shape[0], w.shape[1]
    tb = 512
    return pl.pallas_call(
        _proj_kernel,
        grid=(n // tb, ncol // tn),
        in_specs=[pl.BlockSpec((tb, D_MODEL), lambda i, j: (i, 0)), _full((1, D_MODEL)),
                  _mod_spec(1, tb, mod_base, rows_per_cond), _mod_spec(0, tb, mod_base, rows_per_cond),
                  pl.BlockSpec((D_MODEL, tn), lambda i, j: (0, j))],
        out_specs=pl.BlockSpec((tb, tn), lambda i, j: (i, j)),
        out_shape=jax.ShapeDtypeStruct((n, ncol), F32),
        scratch_shapes=[pltpu.VMEM((tb, D_MODEL), BF16)],
        compiler_params=_cp(2), name="norm_mod_proj",
    )(x, g.reshape(1, D_MODEL), mod, mod, w)


def _outproj_kernel(ya_ref, yb_ref, w_ref, x_ref, g1_ref, o_ref):
    half = ya_ref.shape[1]
    y = _dot(ya_ref[...].astype(BF16), w_ref[0:half, :]) + _dot(yb_ref[...].astype(BF16), w_ref[half:, :])
    o_ref[...] = x_ref[...] + g1_ref[0] * y


def _outproj(ya, yb, w, x, mod, *, mod_base, rows_per_cond):
    n = x.shape[0]
    tb = 512
    return pl.pallas_call(
        _outproj_kernel,
        grid=(n // tb,),
        in_specs=[pl.BlockSpec((tb, ya.shape[1]), lambda i: (i, 0)), pl.BlockSpec((tb, yb.shape[1]), lambda i: (i, 0)),
                  _full(w.shape), pl.BlockSpec((tb, D_MODEL), lambda i: (i, 0)),
                  _mod_spec(2, tb, mod_base, rows_per_cond)],
        out_specs=pl.BlockSpec((tb, D_MODEL), lambda i: (i, 0)),
        out_shape=jax.ShapeDtypeStruct((n, D_MODEL), F32),
        compiler_params=_cp(1), name="out_proj_residual",
    )(ya, yb, w, x, mod)


ROUTE_TB = 256
_CAND_KEEP = (16, 8, 5, 4, 3, 2, 2, 2)


def _sort_pairs(n):
    pairs = []

    def merge(lo, hi, r):
        step = r * 2
        if step < hi - lo:
            merge(lo, hi, step)
            merge(lo + r, hi, step)
            pairs.extend((i, i + r) for i in range(lo + r, hi - r, step))
        else:
            pairs.append((lo, lo + r))

    def sort(lo, hi):
        if hi - lo >= 1:
            mid = lo + (hi - lo) // 2
            sort(lo, mid)
            sort(mid + 1, hi)
            merge(lo, hi, 1)

    sort(0, 15)
    return [(i, j) for i, j in pairs if j < n]


def _top_values(x, count):
    groups = [x[r:r + SUBLANES] for r in range(0, x.shape[0], SUBLANES)]
    n = len(groups)
    for i, j in _sort_pairs(n):
        groups[i], groups[j] = jnp.maximum(groups[i], groups[j]), jnp.minimum(groups[i], groups[j])
    sub = lax.broadcasted_iota(jnp.int32, groups[0].shape, 0)
    tops = []
    for t in range(count):
        head = groups[0]
        m = jnp.max(head, axis=0, keepdims=True)
        tops.append(m)
        if t == count - 1:
            break
        first = jnp.min(jnp.where(head == m, sub, SUBLANES), axis=0, keepdims=True)
        popped = sub == first
        depth = min(count - 1 - t, n)
        for r in range(depth):
            below = groups[r + 1] if r + 1 < n else NEG_INF
            groups[r] = jnp.where(popped, below, groups[r])
    return tops


def _route_kernel(x_ref, g_ref, sc_ref, sh_ref, wqt_ref, keys_ref,
                  hb_ref, s1_ref, s2_ref, e2_ref, c1_ref, tau_ref, qt_s, top_s):
    tb = x_ref.shape[0]
    h = _rms(x_ref[...]) * g_ref[...] * (1.0 + sc_ref[0]) + sh_ref[0]
    hb = h.astype(BF16)
    hb_ref[...] = hb
    qt_s[...] = _dot_nt(wqt_ref[...], hb).astype(BF16)

    iota_8 = lax.broadcasted_iota(jnp.int32, (SUBLANES, tb), 0)

    def head(hd, _):
        scores = []
        for c in range(2):
            r0 = pl.multiple_of((hd * 2 + c) * PEER_NKEYS, PEER_NKEYS)
            s = _dot(keys_ref[hd * 2 + c], qt_s[pl.ds(r0, PEER_NKEYS), :])
            scores.append(s)
            for it, m in enumerate(_top_values(s, PEER_TOPK)):
                top_s[c, it:it + 1, :] = m
        a1 = top_s[0]
        a2 = top_s[1]
        pieces = [a1[0:1] + a2, a1[1:2] + a2[0:8]]
        for i in range(2, 8):
            pieces.append(jnp.where(iota_8 < _CAND_KEEP[i], a1[i:i + 1] + a2[0:8], NEG_INF))
        pieces.append(a1[8:16] + a2[0:1])
        cand = jnp.concatenate(pieces, axis=0)
        tau = _top_values(cand, PEER_TOPK)[-1]
        top = a1[0:1] + a2[0:1]
        z = jnp.sum(jnp.where(cand >= tau, jnp.exp(cand - top), 0.0), axis=0, keepdims=True)
        e2 = jnp.exp(scores[1] - a2[0:1])
        c1 = jnp.exp(scores[0] - a1[0:1]) / z
        for t in range(tb // LANES):
            ts = slice(t * LANES, (t + 1) * LANES)
            s1_ref[hd, t] = scores[0][:, ts]
            s2_ref[hd, t] = scores[1][:, ts]
            e2_ref[hd, t] = e2[:, ts]
            c1_ref[hd, t] = c1[:, ts]
            tau_ref[t, pl.ds(hd, 1), :] = tau[:, ts]
        return 0

    lax.fori_loop(0, PEER_HEADS, head, 0)


def _route(x, g, mod, wqt, keys, *, mod_base, rows_per_cond):
    n = x.shape[0]
    tb = ROUTE_TB
    ntg = tb // LANES
    hk = (PEER_HEADS, n // LANES, PEER_NKEYS, LANES)
    rspec = pl.BlockSpec((PEER_HEADS, ntg, PEER_NKEYS, LANES), lambda i: (0, i, 0, 0))
    return pl.pallas_call(
        _route_kernel,
        grid=(n // tb,),
        in_specs=[pl.BlockSpec((tb, D_MODEL), lambda i: (i, 0)), _full((1, D_MODEL)),
                  _mod_spec(4, tb, mod_base, rows_per_cond), _mod_spec(3, tb, mod_base, rows_per_cond),
                  _full(wqt.shape), _full(keys.shape)],
        out_specs=[pl.BlockSpec((tb, D_MODEL), lambda i: (i, 0)), rspec, rspec, rspec, rspec,
                   pl.BlockSpec((ntg, PEER_HEADS, LANES), lambda i: (i, 0, 0))],
        out_shape=[jax.ShapeDtypeStruct((n, D_MODEL), BF16)] + [jax.ShapeDtypeStruct(hk, F32)] * 4
        + [jax.ShapeDtypeStruct((n // LANES, PEER_HEADS, LANES), F32)],
        scratch_shapes=[pltpu.VMEM((PEER_HEADS * 2 * PEER_NKEYS, tb), BF16), pltpu.VMEM((2, PEER_TOPK, tb), F32)],
        compiler_params=_cp(1), name="peer_route",
    )(x, g.reshape(1, D_MODEL), mod, mod, wqt, keys)


PEER_TB = 512
PEER_EC = 1024
PEER_SUB = PEER_EC // 2


def _experts_kernel(hb_ref, s1_ref, s2_ref, e2_ref, c1_ref, tau_ref, u0_ref, ub_ref, un_ref, vp_ref, va_ref, vl_ref,
                    x_ref, g2_ref, fg_ref, o_ref, acc_s, st0_s, st1_s, a0_s, a1_s, *, final_norm):
    e = pl.program_id(1)
    tb = hb_ref.shape[0]
    ntg = tb // LANES
    rows_per_sub = PEER_SUB // PEER_NKEYS

    def scores(u_blk, st_s, h):
        res = _dot_nt(u_blk[...], hb_ref[h * 2 * LANES:(h + 1) * 2 * LANES, :])
        st_s[2 * h] = res[:, :LANES]
        st_s[2 * h + 1] = res[:, LANES:]

    def values(vt_blk, act_s, h):
        act = jnp.concatenate([act_s[2 * h], act_s[2 * h + 1]], axis=1)
        hs = slice(h * 2 * LANES, (h + 1) * 2 * LANES)
        for r in range(0, D_MODEL, PEER_SUB):
            acc_s[r:r + PEER_SUB, hs] += _dot(vt_blk[r:r + PEER_SUB, :], act)

    def gates(c, tg, st_s, act_s):
        taus = tau_ref[tg]
        sub = 32
        for k in range(rows_per_sub):
            row = c * rows_per_sub + k
            for j0 in range(0, PEER_NKEYS, sub):
                js = slice(j0, j0 + sub)
                ks = slice(k * PEER_NKEYS + j0, k * PEER_NKEYS + j0 + sub)
                w = jnp.zeros((sub, LANES), F32)
                for hd in range(PEER_HEADS):
                    val = s1_ref[hd, tg, row:row + 1, :] + s2_ref[hd, tg, js, :]
                    sel = jnp.where(val >= taus[hd:hd + 1], e2_ref[hd, tg, js, :], 0.0)
                    w = w + sel * c1_ref[hd, tg, row:row + 1, :]
                s = st_s[tg, ks, :]
                act = 0.5 * s * (1.0 + lax.erf(s * np.float32(2.0 ** -0.5)))
                act_s[tg, ks, :] = (act * w).astype(BF16)

    @pl.when(e == 0)
    def _():
        acc_s[...] = jnp.zeros_like(acc_s)
        a1_s[...] = jnp.zeros_like(a1_s)
        scores(u0_ref, st0_s, 0)
        scores(u0_ref, st0_s, 1)

    def phase(c, st_cur, act_cur, u_next, st_next, vt_prev, act_prev):
        for h in range(2):
            scores(u_next, st_next, h)
            gates(c, 2 * h, st_cur, act_cur)
            values(vt_prev, act_prev, h)
            gates(c, 2 * h + 1, st_cur, act_cur)

    phase(0, st0_s, a0_s, ub_ref, st1_s, vp_ref, a1_s)
    phase(1, st1_s, a1_s, un_ref, st0_s, va_ref, a0_s)

    @pl.when(e == pl.num_programs(1) - 1)
    def _():
        values(vl_ref, a1_s, 0)
        values(vl_ref, a1_s, 1)
        y = x_ref[...] + g2_ref[0] * acc_s[...].T
        if final_norm:
            y = _rms(y) * fg_ref[...]
        o_ref[...] = y


def _experts(hb, s1, s2, e2, c1, tau, u, vt, x, mod, fg, *, mod_base, rows_per_cond, final_norm):
    n = x.shape[0]
    tb = PEER_TB
    ntg = tb // LANES
    ne = PEER_N // PEER_EC
    once = dict(pipeline_mode=pl.Buffered(1))
    rspec = pl.BlockSpec((PEER_HEADS, ntg, PEER_NKEYS, LANES), lambda i, e: (0, i, 0, 0), **once)
    rows8 = pl.BlockSpec((PEER_HEADS, ntg, SUBLANES, LANES), lambda i, e: (0, i, e, 0))
    uspec = lambda index: pl.BlockSpec((PEER_SUB, D_MODEL), index)
    vspec = lambda index: pl.BlockSpec((D_MODEL, PEER_SUB), index)
    return pl.pallas_call(
        functools.partial(_experts_kernel, final_norm=final_norm),
        grid=(n // tb, ne),
        in_specs=[pl.BlockSpec((tb, D_MODEL), lambda i, e: (i, 0), **once), rows8, rspec, rspec, rows8,
                  pl.BlockSpec((ntg, PEER_HEADS, LANES), lambda i, e: (i, 0, 0), **once),
                  pl.BlockSpec((PEER_SUB, D_MODEL), lambda i, e: (0, 0), **once),
                  uspec(lambda i, e: (2 * e + 1, 0)),
                  uspec(lambda i, e: (jnp.minimum(2 * e + 2, 2 * ne - 2), 0)),
                  vspec(lambda i, e: (0, jnp.maximum(2 * e - 1, 0))),
                  vspec(lambda i, e: (0, 2 * e)),
                  pl.BlockSpec((D_MODEL, PEER_SUB), lambda i, e: (0, 2 * ne - 1), **once),
                  pl.BlockSpec((tb, D_MODEL), lambda i, e: (i, 0), **once),
                  _mod_spec(5, tb, mod_base, rows_per_cond), _full((1, D_MODEL))],
        out_specs=pl.BlockSpec((tb, D_MODEL), lambda i, e: (i, 0)),
        out_shape=jax.ShapeDtypeStruct((n, D_MODEL), F32),
        scratch_shapes=[pltpu.VMEM((D_MODEL, tb), F32), pltpu.VMEM((ntg, PEER_SUB, LANES), F32),
                        pltpu.VMEM((ntg, PEER_SUB, LANES), F32), pltpu.VMEM((ntg, PEER_SUB, LANES), BF16),
                        pltpu.VMEM((ntg, PEER_SUB, LANES), BF16)],
        compiler_params=_cp(2), name="peer_experts",
    )(hb, s1, s2, e2, c1, tau, u, u, u, vt, vt, vt, x, mod, fg.reshape(1, D_MODEL))


def _peer(x, g, mod, wqt, keys, u, vt, fg, *, mod_base, rows_per_cond, final_norm):
    hb, s1, s2, e2, c1, tau = _route(x, g, mod, wqt, keys, mod_base=mod_base, rows_per_cond=rows_per_cond)
    return _experts(hb, s1, s2, e2, c1, tau, u, vt, x, mod, fg, mod_base=mod_base, rows_per_cond=rows_per_cond,
                    final_norm=final_norm)


def _chunk_index(s, nc):
    return jnp.where(s < nc, s, 2 * nc - 1 - s)


def _scan_group(nseq):
    return 4 if nseq % 4 == 0 else 2


def _scan_specs(nc):
    def cur(col):
        return lambda b, s: (b, _chunk_index(s, nc), col)

    def prev(col):
        return lambda b, s: (b, jnp.maximum(_chunk_index(s, nc) * 8 - 1, 0), col)

    def nxt(col):
        return lambda b, s: (b, jnp.minimum(_chunk_index(s, nc) * 8 + 8, nc * 8 - 1), col)

    def out(col):
        return lambda b, s: (b, jnp.where(s < nc, nc - 1, 2 * nc - 1 - s), col)
    return cur, prev, nxt, out


def _conv_silu(prev, cur, nxt, w_ref, b_ref, first, last):
    win = jnp.concatenate([jnp.where(first, 0.0, prev), cur, jnp.where(last, 0.0, nxt)], axis=1)
    acc = b_ref[...] + win[:, 6:70, :] * w_ref[0:1, :]
    for k in range(1, CONV_W):
        acc = acc + win[:, 6 + k:70 + k, :] * w_ref[k:k + 1, :]
    return _silu(acc)


def _tri(bwd, G):
    r = lax.broadcasted_iota(jnp.int32, (G, CHUNK, CHUNK), 1)
    c = lax.broadcasted_iota(jnp.int32, (G, CHUNK, CHUNK), 2)
    lower = (c <= r).astype(F32)
    upper = (c >= r).astype(F32)
    return jnp.where(bwd, upper, lower), jnp.where(bwd, lower, upper)


def _bdot(a, b, contract, precision=None):
    return lax.dot_general(a, b, (((contract[0],), (contract[1],)), ((0,), (0,))), preferred_element_type=F32,
                           precision=precision)


def _ssd_kernel(z_ref, xp_ref, xc_ref, xn_ref, bp_ref, bc_ref, bn_ref, sm_ref, dtt_ref,
                cwx_ref, cbx_ref, cwb_ref, cbb_ref, arow_ref, acol_ref, brow_ref, bcol_ref, drow_ref, ng_ref,
                st0_ref, y_ref, st_ref, yf_s, h_s, *, nc):
    s = pl.program_id(1)
    bwd = s >= nc
    cidx = _chunk_index(s, nc)
    first = cidx == 0
    last = cidx == nc - 1
    nh = SSD_HEADS

    @pl.when(s == 0)
    def _():
        h_s[...] = st0_ref[:, 0]

    @pl.when(s == nc)
    def _():
        st_ref[:, 0] = h_s[...]
        h_s[...] = st0_ref[:, 1]

    G = z_ref.shape[0]
    tsel, tsel_t = _tri(bwd, G)
    mask = tsel > 0.0
    r0 = pl.multiple_of(cidx * CHUNK, CHUNK)
    hi = lax.Precision.HIGHEST

    xs = _conv_silu(xp_ref[...], xc_ref[...], xn_ref[...], cwx_ref, cbx_ref, first, last)
    bcv = _conv_silu(bp_ref[...], bc_ref[...], bn_ref[...], cwb_ref, cbb_ref, first, last)
    dt_all = _softplus(sm_ref[:, :, 64:96] + brow_ref[...])
    dtt_all = _softplus(dtt_ref[:, 0] + bcol_ref[...])
    a_all = dt_all * (-jnp.exp(arow_ref[...]))
    at_all = dtt_all * (-jnp.exp(acol_ref[...]))
    dt = jnp.where(bwd, dt_all[:, :, nh:2 * nh], dt_all[:, :, 0:nh])
    a = jnp.where(bwd, a_all[:, :, nh:2 * nh], a_all[:, :, 0:nh])
    at = jnp.where(bwd, at_all[:, nh:2 * nh, :], at_all[:, 0:nh, :])
    cum = _bdot(tsel, a, (2, 1), hi)
    cum_t = _bdot(at, tsel_t, (2, 1), hi)
    tot = jnp.where(bwd, cum[:, 0:1, :], cum[:, CHUNK - 1:CHUNK, :])

    ys = []
    for g in range(2):
        bg = bcv[:, :, g * SSD_STATE:(g + 1) * SSD_STATE]
        cg = bcv[:, :, 256 + g * SSD_STATE:256 + (g + 1) * SSD_STATE]
        cg16 = cg.astype(BF16)
        cb = _bdot(cg16, bg.astype(BF16), (2, 2))
        for hh in range(nh // 2):
            h = g * (nh // 2) + hh
            cq = cum[:, :, h:h + 1]
            decay = jnp.exp(jnp.where(mask, cq - cum_t[:, h:h + 1, :], NEG_INF))
            xd = xs[:, :, h * SSD_HEAD_DIM:(h + 1) * SSD_HEAD_DIM] * dt[:, :, h:h + 1]
            xd16 = xd.astype(BF16)
            hprev = h_s[:, h]
            y = (_bdot((cb * decay).astype(BF16), xd16, (2, 1))
                 + _bdot(cg16, hprev.astype(BF16), (2, 2)) * jnp.exp(cq))
            ys.append(y)
            th = tot[:, :, h:h + 1]
            bw = bg * jnp.exp(th - cq)
            h_s[:, h] = hprev * jnp.exp(th) + _bdot(xd16, bw.astype(BF16), (1, 1))
    y = jnp.concatenate(ys, axis=2)

    @pl.when(jnp.logical_not(bwd))
    def _():
        yf_s[:, pl.ds(r0, CHUNK), :] = y

    @pl.when(bwd)
    def _():
        yt = (yf_s[:, pl.ds(r0, CHUNK), :] + y + drow_ref[...] * xs) * _silu(z_ref[...])
        y_ref[...] = _rms(yt) * ng_ref[...]

    @pl.when(s == 2 * nc - 1)
    def _():
        st_ref[:, 1] = h_s[...]


def _ssd(p0, dtt, cwx, cbx, cwb, cbb, a_log, dt_bias, d_skip, norm_g, st0, *, L):
    nseq, nc = p0.shape[0], L // CHUNK
    G = _scan_group(nseq)
    cur, prev, nxt, out = _scan_specs(nc)
    st_shape = (G, 2, SSD_HEADS, SSD_HEAD_DIM, SSD_STATE)
    per_seq = st0 is not None
    if st0 is None:
        st0 = jnp.zeros(st_shape, F32)
    row = lambda v: v.reshape(1, -1)
    col = lambda v: v.reshape(-1, 1)
    small = [cwx, row(cbx), cwb, row(cbb), row(a_log), col(a_log), row(dt_bias), col(dt_bias),
             row(jnp.repeat(d_skip, SSD_HEAD_DIM)), row(norm_g)]
    blk = lambda rows, width, index: pl.BlockSpec((G, rows, width), index)
    return pl.pallas_call(
        functools.partial(_ssd_kernel, nc=nc),
        grid=(nseq // G, 2 * nc),
        in_specs=[blk(CHUNK, 1024, cur(0)),
                  blk(8, 1024, prev(1)), blk(CHUNK, 1024, cur(1)), blk(8, 1024, nxt(1)),
                  blk(8, 512, prev(4)), blk(CHUNK, 512, cur(4)), blk(8, 512, nxt(4)),
                  blk(CHUNK, LANES, cur(28)),
                  pl.BlockSpec((G, 1, 32, CHUNK), lambda b, s: (b, _chunk_index(s, nc), 0, 0))]
        + [_full(v.shape) for v in small]
        + [pl.BlockSpec(st_shape, lambda b, s: (b if per_seq else 0, 0, 0, 0, 0))],
        out_specs=[blk(CHUNK, SSD_D_INNER, out(0)),
                   pl.BlockSpec(st_shape, lambda b, s: (b, 0, 0, 0, 0))],
        out_shape=[jax.ShapeDtypeStruct((nseq, L, SSD_D_INNER), F32),
                   jax.ShapeDtypeStruct((nseq,) + st_shape[1:], F32)],
        scratch_shapes=[pltpu.VMEM((G, L, SSD_D_INNER), F32), pltpu.VMEM(st_shape[:1] + st_shape[2:], F32)],
        compiler_params=_cp(2), name="ssd_mixer",
    )(p0, p0, p0, p0, p0, p0, p0, p0, dtt, *small, st0)


def _rope_tables(L):
    t = jnp.arange(L)
    r = (t // GRID_W).astype(F32)
    c = (t % GRID_W).astype(F32)
    nf = 16
    inv = ROPE_BASE ** (-jnp.arange(nf, dtype=F32) / nf)
    ang = jnp.concatenate([r[:, None] * inv, c[:, None] * inv], axis=-1)
    ang = jnp.concatenate([ang, ang], axis=-1)
    return jnp.cos(ang), jnp.sin(ang)


def _rope128(x, cos, sin):
    lane = lax.broadcasted_iota(jnp.int32, x.shape, 1)
    rot = jnp.where(lane % 64 < 32, -pltpu.roll(x, 96, 1), pltpu.roll(x, 32, 1))
    return x * cos + rot * sin


def _pad_lanes(x):
    return jnp.concatenate([x, jnp.zeros_like(x)], axis=1)


MLA_QB = 256


def _mla_kernel(*refs, L, Lc, rope, emit):
    refs = list(refs)
    qa_ref, kva_ref, sm_ref, qg_ref, kvg_ref, wqn_ref, wqr_ref, wkk_ref, wkv_ref = refs[:9]
    pos = 9
    if Lc:
        cckv_ref, ckpe_ref = refs[pos:pos + 2]
        pos += 2
    if rope:
        cos_ref, sin_ref = refs[pos:pos + 2]
        pos += 2
    y_ref = refs[pos]
    pos += 1
    if emit:
        ckv_ref = refs[pos]
        pos += 1
    qn_s, qr_s, kn_s, v_s, kpe_s = refs[pos:]
    qb = pl.program_id(1)
    scale = np.float32((MLA_NOPE + MLA_ROPE) ** -0.5)

    @pl.when(qb == 0)
    def _():
        ckv = _rms(kva_ref[...]) * kvg_ref[...]
        if emit:
            ckv_ref[...] = ckv
        kpe = _pad_lanes(sm_ref[:, 0:MLA_ROPE])
        if rope:
            kpe = _rope128(kpe, cos_ref[...], sin_ref[...])
        if Lc:
            ckv = jnp.concatenate([cckv_ref[0], ckv], axis=0)
            kpe = jnp.concatenate([_pad_lanes(ckpe_ref[0]), kpe], axis=0)
        c16 = ckv.astype(BF16)
        kn_s[...] = _dot(c16, wkk_ref[...]).astype(BF16)
        v_s[...] = _dot(c16, wkv_ref[...]).astype(BF16)
        kpe_s[...] = kpe.astype(BF16)

    qn = (_rms(qa_ref[...]) * qg_ref[...]).astype(BF16)
    qn_s[...] = _dot(qn, wqn_ref[...]).astype(BF16)
    qr = _dot(qn, wqr_ref[...])
    if rope:
        q0 = pl.multiple_of(qb * MLA_QB, MLA_QB)
        cos = cos_ref[pl.ds(q0, MLA_QB), :]
        sin = sin_ref[pl.ds(q0, MLA_QB), :]
        for h in range(MLA_HEADS):
            qr_s[:, h * LANES:(h + 1) * LANES] = _rope128(qr[:, h * LANES:(h + 1) * LANES], cos, sin).astype(BF16)
    else:
        qr_s[...] = qr.astype(BF16)

    def head(h, _):
        c0 = pl.multiple_of(h * LANES, LANES)
        s = (_dot_nt(qn_s[:, pl.ds(c0, LANES)], kn_s[:, pl.ds(c0, LANES)])
             + _dot_nt(qr_s[:, pl.ds(c0, LANES)], kpe_s[...])) * scale
        p = jnp.exp(s - jnp.max(s, axis=-1, keepdims=True))
        p = p / jnp.sum(p, axis=-1, keepdims=True)
        y_ref[:, pl.ds(c0, LANES)] = _dot(p.astype(BF16), v_s[:, pl.ds(c0, LANES)])
        return 0

    lax.fori_loop(0, MLA_HEADS, head, 0)


def _mla(p0, qg, kvg, wqn, wqr, wkk, wkv, cache_ckv, cache_kpe, rope_tabs, *, L):
    n = p0.shape[0]
    nseq, nqb = n // L, L // MLA_QB
    Lc = 0 if cache_ckv is None else cache_ckv.shape[1]
    rope = rope_tabs is not None
    emit = cache_ckv is None
    Lk = L + Lc
    args = [p0, p0, p0, qg.reshape(1, -1), kvg.reshape(1, -1), wqn, wqr, wkk, wkv]
    specs = [pl.BlockSpec((MLA_QB, MLA_RANK), lambda b, q: (b * nqb + q, 5)),
             pl.BlockSpec((L, MLA_RANK), lambda b, q: (b, 6)),
             pl.BlockSpec((L, LANES), lambda b, q: (b, 28)),
             _full((1, MLA_RANK)), _full((1, MLA_RANK)), _full(wqn.shape), _full(wqr.shape), _full(wkk.shape),
             _full(wkv.shape)]
    if Lc:
        args += [cache_ckv, cache_kpe]
        specs += [pl.BlockSpec((1, Lc, MLA_RANK), lambda b, q: (b, 0, 0)),
                  pl.BlockSpec((1, Lc, MLA_ROPE), lambda b, q: (b, 0, 0))]
    if rope:
        args += list(rope_tabs)
        specs += [_full((L, LANES)), _full((L, LANES))]
    out_specs = [pl.BlockSpec((MLA_QB, 1024), lambda b, q: (b * nqb + q, 0))]
    out_shape = [jax.ShapeDtypeStruct((n, 1024), F32)]
    if emit:
        out_specs.append(pl.BlockSpec((L, MLA_RANK), lambda b, q: (b, 0)))
        out_shape.append(jax.ShapeDtypeStruct((n, MLA_RANK), F32))
    return pl.pallas_call(
        functools.partial(_mla_kernel, L=L, Lc=Lc, rope=rope, emit=emit),
        grid=(nseq, nqb), in_specs=specs, out_specs=out_specs, out_shape=out_shape,
        scratch_shapes=[pltpu.VMEM((MLA_QB, 1024), BF16), pltpu.VMEM((MLA_QB, 1024), BF16),
                        pltpu.VMEM((Lk, 1024), BF16), pltpu.VMEM((Lk, 1024), BF16), pltpu.VMEM((Lk, LANES), BF16)],
        compiler_params=_cp(2), name="mla_attention",
    )(*args)


def _gqa_ctx_kernel(q_ref, k_ref, v_ref, sink_ref, y_ref):
    q = q_ref[...]
    k = k_ref[...].astype(BF16)
    v = v_ref[...].astype(BF16)
    scale = np.float32(GQA_HEAD_DIM ** -0.5)
    group = GQA_HEADS // GQA_KV_HEADS
    outs = []
    for h in range(GQA_HEADS):
        kh = h // group
        ks = slice(kh * GQA_HEAD_DIM, (kh + 1) * GQA_HEAD_DIM)
        s = _dot_nt(q[:, h * GQA_HEAD_DIM:(h + 1) * GQA_HEAD_DIM].astype(BF16), k[:, ks]) * scale
        m = jnp.maximum(jnp.max(s, axis=-1, keepdims=True), sink_ref[h])
        p = jnp.exp(s - m)
        p = p / (jnp.sum(p, axis=-1, keepdims=True) + jnp.exp(sink_ref[h] - m))
        outs.append(_dot(p.astype(BF16), v[:, ks]))
    y_ref[...] = jnp.concatenate(outs, axis=1)


def _gqa_ctx(p1, sink, *, L):
    n = p1.shape[0]
    return pl.pallas_call(
        _gqa_ctx_kernel,
        grid=(n // L,),
        in_specs=[pl.BlockSpec((L, 1024), lambda b: (b, 2)), pl.BlockSpec((L, 256), lambda b: (b, 20)),
                  pl.BlockSpec((L, 256), lambda b: (b, 21)), pl.BlockSpec(memory_space=pltpu.SMEM)],
        out_specs=pl.BlockSpec((L, 1024), lambda b: (b, 0)),
        out_shape=jax.ShapeDtypeStruct((n, 1024), F32),
        compiler_params=_cp(1), name="gqa_context",
    )(p1, p1, p1, sink)


GQA_QB = 128
GQA_SPAN = GQA_QB + 2 * WINDOW


def _gqa_win_kernel(q_ref, k_ref, v_ref, ck_ref, cv_ref, cos_ref, sin_ref, sink_ref, y_ref, kp_s, vp_s, *, L):
    qb = pl.program_id(1)
    scale = np.float32(GQA_HEAD_DIM ** -0.5)
    group = GQA_HEADS // GQA_KV_HEADS

    @pl.when(qb == 0)
    def _():
        zeros = jnp.zeros((WINDOW, 256), BF16)
        kp_s[0:WINDOW, :] = zeros
        kp_s[WINDOW + L:, :] = zeros
        vp_s[0:WINDOW, :] = zeros
        vp_s[WINDOW + L:, :] = zeros
        for j in range(2):
            cs = slice(j * LANES, (j + 1) * LANES)
            kp_s[WINDOW:WINDOW + L, cs] = _rope128(k_ref[:, cs], cos_ref[...], sin_ref[...]).astype(BF16)
        vp_s[WINDOW:WINDOW + L, :] = v_ref[...].astype(BF16)

    q0 = pl.multiple_of(qb * GQA_QB, GQA_QB)
    cos = cos_ref[pl.ds(q0, GQA_QB), :]
    sin = sin_ref[pl.ds(q0, GQA_QB), :]
    kw = kp_s[pl.ds(q0, GQA_SPAN), :]
    vw = vp_s[pl.ds(q0, GQA_SPAN), :]
    kc = ck_ref[0].astype(BF16)
    vc = cv_ref[0].astype(BF16)
    qpos = q0 + lax.broadcasted_iota(jnp.int32, (GQA_QB, GQA_SPAN), 0)
    kpos = q0 - WINDOW + lax.broadcasted_iota(jnp.int32, (GQA_QB, GQA_SPAN), 1)
    bias = jnp.where(kpos < 0, NEG_INF, jnp.where(kpos >= L, NEG_INF, jnp.where(jnp.abs(qpos - kpos) <= WINDOW, 0.0, NEG_INF)))
    outs = []
    for j in range(GQA_HEADS * GQA_HEAD_DIM // LANES):
        qj = _rope128(q_ref[:, j * LANES:(j + 1) * LANES], cos, sin).astype(BF16)
        for half in range(2):
            h = 2 * j + half
            kh = h // group
            ks = slice(kh * GQA_HEAD_DIM, (kh + 1) * GQA_HEAD_DIM)
            qh = qj[:, half * GQA_HEAD_DIM:(half + 1) * GQA_HEAD_DIM]
            s_loc = _dot_nt(qh, kw[:, ks]) * scale + bias
            s_ctx = _dot_nt(qh, kc[:, ks]) * scale
            m = jnp.maximum(jnp.maximum(jnp.max(s_loc, axis=-1, keepdims=True), jnp.max(s_ctx, axis=-1, keepdims=True)),
                            sink_ref[h])
            p_loc = jnp.exp(s_loc - m)
            p_ctx = jnp.exp(s_ctx - m)
            inv = 1.0 / (jnp.sum(p_loc, axis=-1, keepdims=True) + jnp.sum(p_ctx, axis=-1, keepdims=True)
                         + jnp.exp(sink_ref[h] - m))
            outs.append(_dot((p_loc * inv).astype(BF16), vw[:, ks]) + _dot((p_ctx * inv).astype(BF16), vc[:, ks]))
    y_ref[...] = jnp.concatenate(outs, axis=1)


def _gqa_win(p1, cache_k, cache_v, cos, sin, sink, *, L):
    n = p1.shape[0]
    nseq, nqb = n // L, L // GQA_QB
    Lc = cache_k.shape[1]
    return pl.pallas_call(
        functools.partial(_gqa_win_kernel, L=L),
        grid=(nseq, nqb),
        in_specs=[pl.BlockSpec((GQA_QB, 1024), lambda b, q: (b * nqb + q, 2)),
                  pl.BlockSpec((L, 256), lambda b, q: (b, 20)), pl.BlockSpec((L, 256), lambda b, q: (b, 21)),
                  pl.BlockSpec((1, Lc, 256), lambda b, q: (b, 0, 0)), pl.BlockSpec((1, Lc, 256), lambda b, q: (b, 0, 0)),
                  _full((L, LANES)), _full((L, LANES)), pl.BlockSpec(memory_space=pltpu.SMEM)],
        out_specs=pl.BlockSpec((GQA_QB, 1024), lambda b, q: (b * nqb + q, 0)),
        out_shape=jax.ShapeDtypeStruct((n, 1024), F32),
        scratch_shapes=[pltpu.VMEM((L + 2 * WINDOW, 256), BF16), pltpu.VMEM((L + 2 * WINDOW, 256), BF16)],
        compiler_params=_cp(2), name="gqa_window",
    )(p1, p1, p1, cache_k, cache_v, cos, sin, sink)


def _mlstm_kernel(qp_ref, qc_ref, qn_ref, v_ref, o_ref, sm_ref, gt_ref, cw_ref, cb_ref, brow_ref, bcol_ref, ng_ref,
                  c0_ref, n0_ref, m0_ref, y_ref, cst_ref, nst_ref, mst_ref, hf_s, c_s, n_s, m_s, *, nc):
    s = pl.program_id(1)
    bwd = s >= nc
    cidx = _chunk_index(s, nc)
    nh = ML_HEADS
    hd = ML_HEAD_DIM

    @pl.when(s == 0)
    def _():
        c_s[...] = c0_ref[:, 0]
        n_s[...] = n0_ref[:, 0]
        m_s[...] = m0_ref[:, 0]

    @pl.when(s == nc)
    def _():
        cst_ref[:, 0] = c_s[...]
        nst_ref[:, 0] = n_s[...]
        mst_ref[:, 0] = m_s[...]
        c_s[...] = c0_ref[:, 1]
        n_s[...] = n0_ref[:, 1]
        m_s[...] = m0_ref[:, 1]

    G = qc_ref.shape[0]
    tsel, tsel_t = _tri(bwd, G)
    mask = tsel > 0.0
    kscale = np.float32(hd ** -0.5)
    r0 = pl.multiple_of(cidx * CHUNK, CHUNK)
    hi = lax.Precision.HIGHEST

    qk = _conv_silu(qp_ref[...], qc_ref[...], qn_ref[...], cw_ref, cb_ref, cidx == 0, cidx == nc - 1)
    gates = sm_ref[:, :, 0:4 * nh] + brow_ref[...]
    gates_t = gt_ref[:, 0] + bcol_ref[...]
    li = jnp.where(bwd, gates[:, :, nh:2 * nh], gates[:, :, 0:nh])
    li_t = jnp.where(bwd, gates_t[:, nh:2 * nh, :], gates_t[:, 0:nh, :])
    lf = -_softplus(-jnp.where(bwd, gates[:, :, 3 * nh:4 * nh], gates[:, :, 2 * nh:3 * nh]))
    lf_t = -_softplus(-jnp.where(bwd, gates_t[:, 3 * nh:4 * nh, :], gates_t[:, 2 * nh:3 * nh, :]))
    bc = _bdot(tsel, lf, (2, 1), hi)
    bc_t = _bdot(lf_t, tsel_t, (2, 1), hi)
    tot = jnp.where(bwd, bc[:, 0:1, :], bc[:, CHUNK - 1:CHUNK, :])

    hs = []
    for h in range(nh):
        cs = slice(h * hd, (h + 1) * hd)
        q = qk[:, :, cs]
        k = qk[:, :, ML_D + h * hd:ML_D + (h + 1) * hd] * kscale
        q16, k16 = q.astype(BF16), k.astype(BF16)
        v = v_ref[:, :, cs]
        bq = bc[:, :, h:h + 1]
        m_prev = m_s[:, h:h + 1, 0:1]
        log_d = jnp.where(mask, bq - bc_t[:, h:h + 1, :] + li_t[:, h:h + 1, :], NEG_INF)
        log_inter = bq + m_prev
        m_out = jnp.maximum(log_inter, jnp.max(log_d, axis=2, keepdims=True))
        sd = _bdot(q16, k16, (2, 2)) * jnp.exp(log_d - m_out)
        w_inter = jnp.exp(log_inter - m_out)
        c_prev = c_s[:, h]
        n_prev = n_s[:, h:h + 1, :]
        num = _bdot(sd.astype(BF16), v.astype(BF16), (2, 1)) + w_inter * _bdot(q16, c_prev.astype(BF16), (2, 2))
        den = jnp.sum(sd, axis=2, keepdims=True) + w_inter * jnp.sum(q * n_prev, axis=2, keepdims=True)
        hs.append(num / jnp.maximum(jnp.abs(den), jnp.exp(-m_out)))
        th = tot[:, :, h:h + 1]
        end_inter = th + m_prev
        end_intra = th - bq + li[:, :, h:h + 1]
        m_new = jnp.maximum(end_inter, jnp.max(end_intra, axis=1, keepdims=True))
        w_c = jnp.exp(end_inter - m_new)
        w_k = jnp.exp(end_intra - m_new)
        c_s[:, h] = w_c * c_prev + _bdot((v * w_k).astype(BF16), k16, (1, 1))
        n_s[:, h:h + 1, :] = w_c * n_prev + jnp.sum(k * w_k, axis=1, keepdims=True)
        m_s[:, h:h + 1, :] = jnp.broadcast_to(m_new, (G, 1, hd))
    hcat = jnp.concatenate(hs, axis=2)

    @pl.when(jnp.logical_not(bwd))
    def _():
        hf_s[:, pl.ds(r0, CHUNK), :] = hcat

    @pl.when(bwd)
    def _():
        tot_h = hf_s[:, pl.ds(r0, CHUNK), :] + hcat
        normed = jnp.concatenate([_rms(tot_h[:, :, h * hd:(h + 1) * hd]) for h in range(nh)], axis=2) * ng_ref[...]
        y_ref[...] = normed * jax.nn.sigmoid(o_ref[...])

    @pl.when(s == 2 * nc - 1)
    def _():
        cst_ref[:, 1] = c_s[...]
        nst_ref[:, 1] = n_s[...]
        mst_ref[:, 1] = m_s[...]


def _mlstm(p1, gt, conv_w, conv_b, ig_b, fg_b, norm_g, c0, n0, m0, *, L):
    nseq, nc = p1.shape[0], L // CHUNK
    G = _scan_group(nseq)
    cur, prev, nxt, out = _scan_specs(nc)
    cshape = (G, 2, ML_HEADS, ML_HEAD_DIM, ML_HEAD_DIM)
    nshape = (G, 2, ML_HEADS, ML_HEAD_DIM)
    per_seq = c0 is not None
    if c0 is None:
        c0, n0, m0 = jnp.zeros(cshape, F32), jnp.zeros(nshape, F32), jnp.zeros(nshape, F32)
    gb = jnp.concatenate([ig_b.reshape(-1), fg_b.reshape(-1)])
    small = [conv_w, conv_b.reshape(1, -1), gb.reshape(1, -1), gb.reshape(-1, 1), norm_g.reshape(1, -1)]
    st_in = lambda shape: pl.BlockSpec(shape, lambda b, s: (b if per_seq else 0,) + (0,) * (len(shape) - 1))
    st_out = lambda shape: pl.BlockSpec(shape, lambda b, s: (b,) + (0,) * (len(shape) - 1))
    blk = lambda rows, width, index: pl.BlockSpec((G, rows, width), index)
    return pl.pallas_call(
        functools.partial(_mlstm_kernel, nc=nc),
        grid=(nseq // G, 2 * nc),
        in_specs=[blk(8, 2048, prev(0)), blk(CHUNK, 2048, cur(0)), blk(8, 2048, nxt(0)),
                  blk(CHUNK, 1024, cur(3)), blk(CHUNK, 1024, cur(4)), blk(CHUNK, LANES, cur(44)),
                  pl.BlockSpec((G, 1, 32, CHUNK), lambda b, s: (b, _chunk_index(s, nc), 0, 0))]
        + [_full(v.shape) for v in small] + [st_in(cshape), st_in(nshape), st_in(nshape)],
        out_specs=[blk(CHUNK, ML_D, out(0)), st_out(cshape), st_out(nshape), st_out(nshape)],
        out_shape=[jax.ShapeDtypeStruct((nseq, L, ML_D), F32), jax.ShapeDtypeStruct((nseq,) + cshape[1:], F32),
                   jax.ShapeDtypeStruct((nseq,) + nshape[1:], F32), jax.ShapeDtypeStruct((nseq,) + nshape[1:], F32)],
        scratch_shapes=[pltpu.VMEM((G, L, ML_D), F32), pltpu.VMEM(cshape[:1] + cshape[2:], F32),
                        pltpu.VMEM(nshape[:1] + nshape[2:], F32), pltpu.VMEM(nshape[:1] + nshape[2:], F32)],
        compiler_params=_cp(2), name="mlstm_mixer",
    )(p1, p1, p1, p1, p1, p1, gt, *small, c0, n0, m0)


def _prep_layer0(w_in, w_qb, w_kvb, w_out):
    pad = jnp.zeros((D_MODEL, L0_COLS - 3680), F32)
    w = jnp.concatenate([w_in[:, :2560], w_in[:, 2592:3680], w_in[:, 2560:2592], pad], axis=1).astype(BF16)
    q3 = w_qb.reshape(MLA_RANK, MLA_HEADS, MLA_NOPE + MLA_ROPE)
    wqn = q3[:, :, :MLA_NOPE].reshape(MLA_RANK, -1).astype(BF16)
    wqr = jnp.concatenate([q3[:, :, MLA_NOPE:], jnp.zeros((MLA_RANK, MLA_HEADS, LANES - MLA_ROPE), F32)], axis=-1)
    wqr = wqr.reshape(MLA_RANK, -1).astype(BF16)
    k3 = w_kvb.reshape(MLA_RANK, MLA_HEADS, MLA_NOPE + MLA_V)
    wkk = k3[:, :, :MLA_NOPE].reshape(MLA_RANK, -1).astype(BF16)
    wkv = k3[:, :, MLA_NOPE:].reshape(MLA_RANK, -1).astype(BF16)
    return w, wqn, wqr, wkk, wkv, w_out.astype(BF16)


def _prep_layer1(w_in, w_out):
    pad = jnp.zeros((D_MODEL, L1_COLS - 5664), F32)
    w = jnp.concatenate([w_in[:, 1536:3584], w_in[:, 0:1024], w_in[:, 3584:5632], w_in[:, 1024:1536],
                         w_in[:, 5632:5664], pad], axis=1).astype(BF16)
    return w, w_out.astype(BF16)


def _prep_peer(wq, keys, u, v):
    return (wq.T.astype(BF16), keys.reshape(PEER_HEADS * 2, PEER_NKEYS, -1).astype(BF16), u.astype(BF16),
            v.T.astype(BF16))


def _chunk_transposed(cols, L):
    return cols.reshape(-1, L // CHUNK, CHUNK, cols.shape[1]).transpose(0, 1, 3, 2)


def _trunk(x, L, mod0, mod1, w0, w1, peer0, peer1, p, ctx, rope_tabs, mod_base, rows_per_cond, fg):
    kw = dict(mod_base=mod_base, rows_per_cond=rows_per_cond)
    w_in0, wqn, wqr, wkk, wkv, w_out0 = w0
    w_in1, w_out1 = w1
    nseq = x.shape[0] // L

    p0 = _proj(x, p['l0_norm1_g'], mod0, w_in0, tn=1280, **kw)
    cw, cb = p['l0_ssd_conv_w'], p['l0_ssd_conv_b']
    y_ssd, ssd_state = _ssd(p0.reshape(nseq, L, -1), _chunk_transposed(p0[:, 3648:3680], L), cw[:, :1024], cb[:1024],
                            cw[:, 1024:], cb[1024:], p['l0_ssd_A_log'], p['l0_ssd_dt_bias'], p['l0_ssd_D'],
                            p['l0_ssd_norm_g'], None if ctx is None else ctx[0], L=L)
    y_ssd = y_ssd.reshape(nseq * L, -1)
    mla_tabs = None
    if rope_tabs is not None:
        z64 = jnp.zeros_like(rope_tabs[0])
        mla_tabs = (jnp.concatenate([rope_tabs[0], z64], axis=1), jnp.concatenate([rope_tabs[1], z64], axis=1))
    mla_out = _mla(p0, p['l0_mla_q_norm_g'], p['l0_mla_kv_norm_g'], wqn, wqr, wkk, wkv,
                   None if ctx is None else ctx[1], None if ctx is None else ctx[2], mla_tabs, L=L)
    x = _outproj(y_ssd, mla_out[0], w_out0, x, mod0, **kw)
    x = _peer(x, p['l0_norm2_g'], mod0, *peer0, fg, final_norm=False, **kw)

    p1 = _proj(x, p['l1_norm1_g'], mod1, w_in1, tn=1920, **kw)
    if ctx is None:
        y_gqa = _gqa_ctx(p1, p['l1_gqa_sink'], L=L)
        c0 = n0 = m0 = None
    else:
        gqa_tabs = tuple(jnp.concatenate([t, t], axis=1) for t in rope_tabs)
        y_gqa = _gqa_win(p1, ctx[3].reshape(nseq, -1, 256), ctx[4].reshape(nseq, -1, 256), *gqa_tabs,
                         p['l1_gqa_sink'], L=L)
        c0, n0 = ctx[5], ctx[6]
        m0 = jnp.broadcast_to(ctx[7][..., None], ctx[7].shape + (ML_HEAD_DIM,))
    y_ml, mc, mn, mm = _mlstm(p1.reshape(nseq, L, -1), _chunk_transposed(p1[:, 5632:5664], L), p['l1_ml_conv_w'],
                              p['l1_ml_conv_b'], p['l1_ml_ig_b'], p['l1_ml_fg_b'], p['l1_ml_norm_g'], c0, n0, m0, L=L)
    y_ml = y_ml.reshape(nseq * L, -1)
    x = _outproj(y_gqa, y_ml, w_out1, x, mod1, **kw)
    y = _peer(x, p['l1_norm2_g'], mod1, *peer1, fg, final_norm=True, **kw)
    new = None
    if ctx is None:
        new = (ssd_state, mla_out[1].reshape(nseq, L, MLA_RANK), p0[:, 3584:3648].reshape(nseq, L, MLA_ROPE),
               p1[:, 5120:5376].reshape(nseq, L, GQA_KV_HEADS, GQA_HEAD_DIM),
               p1[:, 5376:5632].reshape(nseq, L, GQA_KV_HEADS, GQA_HEAD_DIM), mc, mn, mm[..., 0])
    return y, new


def kernel(x_prompt, x_sample, state_l0_ssd, cache_l0_mla_ckv, cache_l0_mla_kpe, cache_l1_gqa_k, cache_l1_gqa_v, state_l1_mlstm_C, state_l1_mlstm_n, state_l1_mlstm_m, c, c_ctx, final_norm_g, l0_ada_w, l0_ada_b, l0_norm1_g, l0_norm2_g, l0_w_in, l0_ssd_conv_w, l0_ssd_conv_b, l0_ssd_A_log, l0_ssd_dt_bias, l0_ssd_D, l0_ssd_norm_g, l0_mla_q_norm_g, l0_mla_w_qb, l0_mla_kv_norm_g, l0_mla_w_kvb, l0_w_out, l0_peer_wq, l0_peer_keys, l0_peer_u, l0_peer_v, l1_ada_w, l1_ada_b, l1_norm1_g, l1_norm2_g, l1_w_in, l1_gqa_sink, l1_ml_conv_w, l1_ml_conv_b, l1_ml_ig_b, l1_ml_fg_b, l1_ml_norm_g, l1_w_out, l1_peer_wq, l1_peer_keys, l1_peer_u, l1_peer_v):
    p = dict(l0_norm1_g=l0_norm1_g, l0_norm2_g=l0_norm2_g, l0_ssd_conv_w=l0_ssd_conv_w, l0_ssd_conv_b=l0_ssd_conv_b,
             l0_ssd_A_log=l0_ssd_A_log, l0_ssd_dt_bias=l0_ssd_dt_bias, l0_ssd_D=l0_ssd_D, l0_ssd_norm_g=l0_ssd_norm_g,
             l0_mla_q_norm_g=l0_mla_q_norm_g, l0_mla_kv_norm_g=l0_mla_kv_norm_g, l1_norm1_g=l1_norm1_g,
             l1_norm2_g=l1_norm2_g, l1_gqa_sink=l1_gqa_sink, l1_ml_conv_w=l1_ml_conv_w, l1_ml_conv_b=l1_ml_conv_b,
             l1_ml_ig_b=l1_ml_ig_b, l1_ml_fg_b=l1_ml_fg_b, l1_ml_norm_g=l1_ml_norm_g)
    nb, seq = x_prompt.shape[:2]
    db, dseq = x_sample.shape[:2]
    assert db <= 7 and seq % MLA_QB == 0 and dseq % MLA_QB == 0

    cond8 = jnp.zeros((8, D_MODEL), F32).at[0].set(c_ctx).at[1:1 + db].set(c)
    mod0 = _ada(cond8, l0_ada_w, l0_ada_b).reshape(8, 1, 6 * D_MODEL)
    mod1 = _ada(cond8, l1_ada_w, l1_ada_b).reshape(8, 1, 6 * D_MODEL)
    w0 = _prep_layer0(l0_w_in, l0_mla_w_qb, l0_mla_w_kvb, l0_w_out)
    w1 = _prep_layer1(l1_w_in, l1_w_out)
    peer0 = _prep_peer(l0_peer_wq, l0_peer_keys, l0_peer_u, l0_peer_v)
    peer1 = _prep_peer(l1_peer_wq, l1_peer_keys, l1_peer_u, l1_peer_v)

    y_prompt, new = _trunk(x_prompt.reshape(nb * seq, D_MODEL), seq, mod0, mod1, w0, w1, peer0, peer1, p, None, None,
                           0, nb * seq, final_norm_g)
    ctx = (state_l0_ssd, cache_l0_mla_ckv, cache_l0_mla_kpe, cache_l1_gqa_k, cache_l1_gqa_v, state_l1_mlstm_C,
           state_l1_mlstm_n, state_l1_mlstm_m)
    y_sample, _ = _trunk(x_sample.reshape(db * dseq, D_MODEL), dseq, mod0, mod1, w0, w1, peer0, peer1, p, ctx,
                         _rope_tables(dseq), 1, dseq, final_norm_g)
    return (y_prompt.reshape(nb, seq, D_MODEL), y_sample.reshape(db, dseq, D_MODEL)) + new
```

```python
import functools

import jax
import jax.numpy as jnp
import numpy as np
from jax import lax
from jax.experimental import pallas as pl
from jax.experimental.pallas import tpu as pltpu

F32 = jnp.float32
BF16 = jnp.bfloat16
NEG_INF = float("-inf")

D_MODEL = 2048
EPS = 1e-6
ROPE_BASE = 10000.0
GRID_W = 64
CONV_W = 5
CHUNK = 64
SSD_HEADS = 16
SSD_HEAD_DIM = 64
SSD_D_INNER = 1024
SSD_STATE = 128
MLA_HEADS = 8
MLA_NOPE = 128
MLA_ROPE = 64
MLA_V = 128
MLA_RANK = 512
GQA_HEADS = 16
GQA_KV_HEADS = 4
GQA_HEAD_DIM = 64
WINDOW = 128
ML_HEADS = 8
ML_HEAD_DIM = 128
ML_D = 1024
PEER_HEADS = 8
PEER_NKEYS = 128
PEER_N = PEER_NKEYS * PEER_NKEYS
PEER_TOPK = 16

LANES = 128
SUBLANES = 8
VMEM_LIMIT = 56 * 2**20

L0_COLS = 3840
L1_COLS = 5760


def _cp(n_grid, flags=None):
    return pltpu.CompilerParams(dimension_semantics=("arbitrary",) * n_grid, vmem_limit_bytes=VMEM_LIMIT, flags=flags)


def _silu(x):
    return x * jax.nn.sigmoid(x)


def _softplus(x):
    return jnp.maximum(x, 0.0) + jnp.log1p(jnp.exp(-jnp.abs(x)))


def _rms(x):
    return x * lax.rsqrt(jnp.mean(x * x, axis=-1, keepdims=True) + EPS)


def _dot(a, b):
    return jnp.dot(a, b, preferred_element_type=F32)


def _dot_nt(a, b):
    return lax.dot_general(a, b, (((1,), (1,)), ((), ())), preferred_element_type=F32)


def _dot_tn(a, b):
    return lax.dot_general(a, b, (((0,), (0,)), ((), ())), preferred_element_type=F32)


def _dot_f32(a, b):
    return jnp.dot(a, b, preferred_element_type=F32, precision=lax.Precision.HIGHEST)


def _full(shape):
    nd = len(shape)
    return pl.BlockSpec(shape, lambda *_: (0,) * nd)


def _mod_spec(chunk, tb, mod_base, rows_per_cond):
    def index(i, *_):
        return (mod_base + (i * tb) // rows_per_cond, 0, chunk)
    return pl.BlockSpec((1, 1, D_MODEL), index)


def _ada_kernel(c_ref, w_ref, b_ref, o_ref):
    s = _silu(c_ref[...])
    o_ref[...] = _dot(s.astype(BF16), w_ref[...].astype(BF16)) + b_ref[...]


def _ada(cond8, w, b):
    n = w.shape[1]
    tn = 1024
    return pl.pallas_call(
        _ada_kernel,
        grid=(n // tn,),
        in_specs=[_full((8, D_MODEL)), pl.BlockSpec((D_MODEL, tn), lambda j: (0, j)),
                  pl.BlockSpec((1, tn), lambda j: (0, j))],
        out_specs=pl.BlockSpec((8, tn), lambda j: (0, j)),
        out_shape=jax.ShapeDtypeStruct((8, n), F32),
        compiler_params=_cp(1), name="ada_table",
    )(cond8, w, b.reshape(1, n))


def _proj_kernel(x_ref, g_ref, sc_ref, sh_ref, w_ref, o_ref, h_ref):
    @pl.when(pl.program_id(1) == 0)
    def _():
        h = _rms(x_ref[...]) * g_ref[...] * (1.0 + sc_ref[0]) + sh_ref[0]
        h_ref[...] = h.astype(BF16)
    o_ref[...] = _dot(h_ref[...], w_ref[...])


def _proj(x, g, mod, w, *, tn, mod_base, rows_per_cond):
    n, ncol = x.shape[0], w.shape[1]
    tb = 512
    return pl.pallas_call(
        _proj_kernel,
        grid=(n // tb, ncol // tn),
        in_specs=[pl.BlockSpec((tb, D_MODEL), lambda i, j: (i, 0)), _full((1, D_MODEL)),
                  _mod_spec(1, tb, mod_base, rows_per_cond), _mod_spec(0, tb, mod_base, rows_per_cond),
                  pl.BlockSpec((D_MODEL, tn), lambda i, j: (0, j))],
        out_specs=pl.BlockSpec((tb, tn), lambda i, j: (i, j)),
        out_shape=jax.ShapeDtypeStruct((n, ncol), F32),
        scratch_shapes=[pltpu.VMEM((tb, D_MODEL), BF16)],
        compiler_params=_cp(2), name="norm_mod_proj",
    )(x, g.reshape(1, D_MODEL), mod, mod, w)


def _outproj_kernel(ya_ref, yb_ref, w_ref, x_ref, g1_ref, o_ref):
    half = ya_ref.shape[1]
    y = _dot(ya_ref[...].astype(BF16), w_ref[0:half, :]) + _dot(yb_ref[...].astype(BF16), w_ref[half:, :])
    o_ref[...] = x_ref[...] + g1_ref[0] * y


def _outproj(ya, yb, w, x, mod, *, mod_base, rows_per_cond):
    n = x.shape[0]
    tb = 512
    return pl.pallas_call(
        _outproj_kernel,
        grid=(n // tb,),
        in_specs=[pl.BlockSpec((tb, ya.shape[1]), lambda i: (i, 0)), pl.BlockSpec((tb, yb.shape[1]), lambda i: (i, 0)),
                  _full(w.shape), pl.BlockSpec((tb, D_MODEL), lambda i: (i, 0)),
                  _mod_spec(2, tb, mod_base, rows_per_cond)],
        out_specs=pl.BlockSpec((tb, D_MODEL), lambda i: (i, 0)),
        out_shape=jax.ShapeDtypeStruct((n, D_MODEL), F32),
        compiler_params=_cp(1), name="out_proj_residual",
    )(ya, yb, w, x, mod)


ROUTE_TB = 512
_CAND_KEEP = (16, 8, 5, 4, 3, 2, 2, 2)


def _sort_pairs(n):
    pairs = []

    def merge(lo, hi, r):
        step = r * 2
        if step < hi - lo:
            merge(lo, hi, step)
            merge(lo + r, hi, step)
            pairs.extend((i, i + r) for i in range(lo + r, hi - r, step))
        else:
            pairs.append((lo, lo + r))

    def sort(lo, hi):
        if hi - lo >= 1:
            mid = lo + (hi - lo) // 2
            sort(lo, mid)
            sort(mid + 1, hi)
            merge(lo, hi, 1)

    sort(0, 15)
    return [(i, j) for i, j in pairs if j < n]


def _top_values(x, count):
    groups = [x[r:r + SUBLANES] for r in range(0, x.shape[0], SUBLANES)]
    n = len(groups)
    for i, j in _sort_pairs(n):
        groups[i], groups[j] = jnp.maximum(groups[i], groups[j]), jnp.minimum(groups[i], groups[j])
    sub = lax.broadcasted_iota(jnp.int32, groups[0].shape, 0)
    tops = []
    for t in range(count):
        head = groups[0]
        m = jnp.max(head, axis=0, keepdims=True)
        tops.append(m)
        if t == count - 1:
            break
        first = jnp.min(jnp.where(head == m, sub, SUBLANES), axis=0, keepdims=True)
        popped = sub == first
        depth = min(count - 1 - t, n)
        for r in range(depth):
            below = groups[r + 1] if r + 1 < n else NEG_INF
            groups[r] = jnp.where(popped, below, groups[r])
    return tops


def _route_kernel(x_ref, g_ref, sc_ref, sh_ref, wqt_ref, keys_ref,
                  hb_ref, s1_ref, s2_ref, e2_ref, c1_ref, tau_ref, qt_s, top_s):
    tb = x_ref.shape[0]
    h = _rms(x_ref[...]) * g_ref[...] * (1.0 + sc_ref[0]) + sh_ref[0]
    hb = h.astype(BF16)
    hb_ref[...] = hb
    qt_s[...] = _dot_nt(wqt_ref[...], hb).astype(BF16)

    iota_8 = lax.broadcasted_iota(jnp.int32, (SUBLANES, tb), 0)

    def head(hd, _):
        scores = []
        for c in range(2):
            r0 = pl.multiple_of((hd * 2 + c) * PEER_NKEYS, PEER_NKEYS)
            s = _dot(keys_ref[hd * 2 + c], qt_s[pl.ds(r0, PEER_NKEYS), :])
            scores.append(s)
            for it, m in enumerate(_top_values(s, PEER_TOPK)):
                top_s[c, it:it + 1, :] = m
        a1 = top_s[0]
        a2 = top_s[1]
        pieces = [a1[0:1] + a2, a1[1:2] + a2[0:8]]
        for i in range(2, 8):
            pieces.append(jnp.where(iota_8 < _CAND_KEEP[i], a1[i:i + 1] + a2[0:8], NEG_INF))
        pieces.append(a1[8:16] + a2[0:1])
        cand = jnp.concatenate(pieces, axis=0)
        tau = _top_values(cand, PEER_TOPK)[-1]
        top = a1[0:1] + a2[0:1]
        z = jnp.sum(jnp.where(cand >= tau, jnp.exp(cand - top), 0.0), axis=0, keepdims=True)
        e2 = jnp.exp(scores[1] - a2[0:1])
        c1 = jnp.exp(scores[0] - a1[0:1]) / z
        for t in range(tb // LANES):
            ts = slice(t * LANES, (t + 1) * LANES)
            s1_ref[hd, t] = scores[0][:, ts]
            s2_ref[hd, t] = scores[1][:, ts]
            e2_ref[hd, t] = e2[:, ts]
            c1_ref[hd, t] = c1[:, ts]
            tau_ref[t, pl.ds(hd, 1), :] = tau[:, ts]
        return 0

    lax.fori_loop(0, PEER_HEADS, head, 0)


def _route(x, g, mod, wqt, keys, *, mod_base, rows_per_cond):
    n = x.shape[0]
    tb = ROUTE_TB
    ntg = tb // LANES
    hk = (PEER_HEADS, n // LANES, PEER_NKEYS, LANES)
    rspec = pl.BlockSpec((PEER_HEADS, ntg, PEER_NKEYS, LANES), lambda i: (0, i, 0, 0))
    return pl.pallas_call(
        _route_kernel,
        grid=(n // tb,),
        in_specs=[pl.BlockSpec((tb, D_MODEL), lambda i: (i, 0)), _full((1, D_MODEL)),
                  _mod_spec(4, tb, mod_base, rows_per_cond), _mod_spec(3, tb, mod_base, rows_per_cond),
                  _full(wqt.shape), _full(keys.shape)],
        out_specs=[pl.BlockSpec((tb, D_MODEL), lambda i: (i, 0)), rspec, rspec, rspec, rspec,
                   pl.BlockSpec((ntg, PEER_HEADS, LANES), lambda i: (i, 0, 0))],
        out_shape=[jax.ShapeDtypeStruct((n, D_MODEL), BF16)] + [jax.ShapeDtypeStruct(hk, F32)] * 4
        + [jax.ShapeDtypeStruct((n // LANES, PEER_HEADS, LANES), F32)],
        scratch_shapes=[pltpu.VMEM((PEER_HEADS * 2 * PEER_NKEYS, tb), BF16), pltpu.VMEM((2, PEER_TOPK, tb), F32)],
        compiler_params=_cp(1), name="peer_route",
    )(x, g.reshape(1, D_MODEL), mod, mod, wqt, keys)


PEER_TB = 512
PEER_EC = 1024
PEER_SUB = PEER_EC // 2


def _experts_kernel(hb_ref, s1_ref, s2_ref, e2_ref, c1_ref, tau_ref, u0_ref, ub_ref, un_ref, vp_ref, va_ref, vl_ref,
                    x_ref, g2_ref, fg_ref, o_ref, acc_s, st0_s, st1_s, a0_s, a1_s, *, final_norm):
    e = pl.program_id(1)
    tb = hb_ref.shape[0]
    ntg = tb // LANES
    rows_per_sub = PEER_SUB // PEER_NKEYS

    def scores(u_blk, st_s, h):
        res = _dot_nt(u_blk[...], hb_ref[h * 2 * LANES:(h + 1) * 2 * LANES, :])
        res = 0.5 * res * (1.0 + lax.erf(res * np.float32(2.0 ** -0.5)))
        st_s[2 * h] = res[:, :LANES]
        st_s[2 * h + 1] = res[:, LANES:]

    def values(vt_blk, act_s, h):
        act = jnp.concatenate([act_s[2 * h], act_s[2 * h + 1]], axis=1)
        hs = slice(h * 2 * LANES, (h + 1) * 2 * LANES)
        for r in range(0, D_MODEL, PEER_SUB):
            acc_s[r:r + PEER_SUB, hs] += _dot(vt_blk[r:r + PEER_SUB, :], act)

    def gates(c, tg, st_s, act_s):
        taus = tau_ref[tg]
        sub = 32
        for k in range(rows_per_sub):
            row = c * rows_per_sub + k
            for j0 in range(0, PEER_NKEYS, sub):
                js = slice(j0, j0 + sub)
                ks = slice(k * PEER_NKEYS + j0, k * PEER_NKEYS + j0 + sub)
                w = jnp.zeros((sub, LANES), F32)
                for hd in range(PEER_HEADS):
                    val = s1_ref[hd, tg, row:row + 1, :] + s2_ref[hd, tg, js, :]
                    sel = jnp.where(val >= taus[hd:hd + 1], e2_ref[hd, tg, js, :], 0.0)
                    w = w + sel * c1_ref[hd, tg, row:row + 1, :]
                act_s[tg, ks, :] = (st_s[tg, ks, :] * w).astype(BF16)

    @pl.when(e == 0)
    def _():
        acc_s[...] = jnp.zeros_like(acc_s)
        a1_s[...] = jnp.zeros_like(a1_s)
        scores(u0_ref, st0_s, 0)
        scores(u0_ref, st0_s, 1)

    def phase(c, st_cur, act_cur, u_next, st_next, vt_prev, act_prev):
        for h in range(2):
            scores(u_next, st_next, h)
            gates(c, 2 * h, st_cur, act_cur)
            values(vt_prev, act_prev, h)
            gates(c, 2 * h + 1, st_cur, act_cur)

    phase(0, st0_s, a0_s, ub_ref, st1_s, vp_ref, a1_s)
    phase(1, st1_s, a1_s, un_ref, st0_s, va_ref, a0_s)

    @pl.when(e == pl.num_programs(1) - 1)
    def _():
        values(vl_ref, a1_s, 0)
        values(vl_ref, a1_s, 1)
        y = x_ref[...] + g2_ref[0] * acc_s[...].T
        if final_norm:
            y = _rms(y) * fg_ref[...]
        o_ref[...] = y


def _experts(hb, s1, s2, e2, c1, tau, u, vt, x, mod, fg, *, mod_base, rows_per_cond, final_norm):
    n = x.shape[0]
    tb = PEER_TB
    ntg = tb // LANES
    ne = PEER_N // PEER_EC
    once = dict(pipeline_mode=pl.Buffered(1))
    rspec = pl.BlockSpec((PEER_HEADS, ntg, PEER_NKEYS, LANES), lambda i, e: (0, i, 0, 0), **once)
    rows8 = pl.BlockSpec((PEER_HEADS, ntg, SUBLANES, LANES), lambda i, e: (0, i, e, 0))
    uspec = lambda index: pl.BlockSpec((PEER_SUB, D_MODEL), index)
    vspec = lambda index: pl.BlockSpec((D_MODEL, PEER_SUB), index)
    return pl.pallas_call(
        functools.partial(_experts_kernel, final_norm=final_norm),
        grid=(n // tb, ne),
        in_specs=[pl.BlockSpec((tb, D_MODEL), lambda i, e: (i, 0), **once), rows8, rspec, rspec, rows8,
                  pl.BlockSpec((ntg, PEER_HEADS, LANES), lambda i, e: (i, 0, 0), **once),
                  pl.BlockSpec((PEER_SUB, D_MODEL), lambda i, e: (0, 0), **once),
                  uspec(lambda i, e: (2 * e + 1, 0)),
                  uspec(lambda i, e: (jnp.minimum(2 * e + 2, 2 * ne - 2), 0)),
                  vspec(lambda i, e: (0, jnp.maximum(2 * e - 1, 0))),
                  vspec(lambda i, e: (0, 2 * e)),
                  pl.BlockSpec((D_MODEL, PEER_SUB), lambda i, e: (0, 2 * ne - 1), **once),
                  pl.BlockSpec((tb, D_MODEL), lambda i, e: (i, 0), **once),
                  _mod_spec(5, tb, mod_base, rows_per_cond), _full((1, D_MODEL))],
        out_specs=pl.BlockSpec((tb, D_MODEL), lambda i, e: (i, 0)),
        out_shape=jax.ShapeDtypeStruct((n, D_MODEL), F32),
        scratch_shapes=[pltpu.VMEM((D_MODEL, tb), F32), pltpu.VMEM((ntg, PEER_SUB, LANES), F32),
                        pltpu.VMEM((ntg, PEER_SUB, LANES), F32), pltpu.VMEM((ntg, PEER_SUB, LANES), BF16),
                        pltpu.VMEM((ntg, PEER_SUB, LANES), BF16)],
        compiler_params=_cp(2), name="peer_experts",
    )(hb, s1, s2, e2, c1, tau, u, u, u, vt, vt, vt, x, mod, fg.reshape(1, D_MODEL))


def _peer(x, g, mod, wqt, keys, u, vt, fg, *, mod_base, rows_per_cond, final_norm):
    hb, s1, s2, e2, c1, tau = _route(x, g, mod, wqt, keys, mod_base=mod_base, rows_per_cond=rows_per_cond)
    return _experts(hb, s1, s2, e2, c1, tau, u, vt, x, mod, fg, mod_base=mod_base, rows_per_cond=rows_per_cond,
                    final_norm=final_norm)


def _chunk_index(s, nc):
    return jnp.where(s < nc, s, 2 * nc - 1 - s)


def _scan_group(nseq, largest):
    return max(g for g in (2, 4, 8) if g <= largest and nseq % g == 0)


def _scan_specs(nc):
    def cur(col):
        return lambda b, s: (b, _chunk_index(s, nc), col)

    def prev(col):
        return lambda b, s: (b, jnp.maximum(_chunk_index(s, nc) * 8 - 1, 0), col)

    def nxt(col):
        return lambda b, s: (b, jnp.minimum(_chunk_index(s, nc) * 8 + 8, nc * 8 - 1), col)

    def out(col):
        return lambda b, s: (b, jnp.where(s < nc, nc - 1, 2 * nc - 1 - s), col)
    return cur, prev, nxt, out


def _conv_silu(prev, cur, nxt, w_ref, b_ref, first, last):
    win = jnp.concatenate([jnp.where(first, 0.0, prev), cur, jnp.where(last, 0.0, nxt)], axis=1)
    acc = b_ref[...] + win[:, 6:70, :] * w_ref[0:1, :]
    for k in range(1, CONV_W):
        acc = acc + win[:, 6 + k:70 + k, :] * w_ref[k:k + 1, :]
    return _silu(acc)


def _tri(bwd, G):
    r = lax.broadcasted_iota(jnp.int32, (G, CHUNK, CHUNK), 1)
    c = lax.broadcasted_iota(jnp.int32, (G, CHUNK, CHUNK), 2)
    lower = (c <= r).astype(F32)
    upper = (c >= r).astype(F32)
    return jnp.where(bwd, upper, lower), jnp.where(bwd, lower, upper)


def _bdot(a, b, contract, precision=None):
    return lax.dot_general(a, b, (((contract[0],), (contract[1],)), ((0,), (0,))), preferred_element_type=F32,
                           precision=precision)


def _ssd_kernel(z_ref, xp_ref, xc_ref, xn_ref, bp_ref, bc_ref, bn_ref, sm_ref, dtt_ref,
                cwx_ref, cbx_ref, cwb_ref, cbb_ref, arow_ref, acol_ref, brow_ref, bcol_ref, drow_ref, ng_ref,
                st0_ref, y_ref, st_ref, yf_s, h_s, *, nc):
    s = pl.program_id(1)
    bwd = s >= nc
    cidx = _chunk_index(s, nc)
    first = cidx == 0
    last = cidx == nc - 1
    nh = SSD_HEADS

    @pl.when(s == 0)
    def _():
        h_s[...] = st0_ref[:, 0]

    @pl.when(s == nc)
    def _():
        st_ref[:, 0] = h_s[...]
        h_s[...] = st0_ref[:, 1]

    G = z_ref.shape[0]
    tsel, tsel_t = _tri(bwd, G)
    mask = tsel > 0.0
    r0 = pl.multiple_of(cidx * CHUNK, CHUNK)
    hi = lax.Precision.HIGHEST

    xs = _conv_silu(xp_ref[...], xc_ref[...], xn_ref[...], cwx_ref, cbx_ref, first, last)
    bcv = _conv_silu(bp_ref[...], bc_ref[...], bn_ref[...], cwb_ref, cbb_ref, first, last)
    dt_all = _softplus(sm_ref[:, :, 64:96] + brow_ref[...])
    dtt_all = _softplus(dtt_ref[:, 0] + bcol_ref[...])
    a_all = dt_all * (-jnp.exp(arow_ref[...]))
    at_all = dtt_all * (-jnp.exp(acol_ref[...]))
    dt = jnp.where(bwd, dt_all[:, :, nh:2 * nh], dt_all[:, :, 0:nh])
    a = jnp.where(bwd, a_all[:, :, nh:2 * nh], a_all[:, :, 0:nh])
    at = jnp.where(bwd, at_all[:, nh:2 * nh, :], at_all[:, 0:nh, :])
    cum = _bdot(tsel, a, (2, 1), hi)
    cum_t = _bdot(at, tsel_t, (2, 1), hi)
    tot = jnp.where(bwd, cum[:, 0:1, :], cum[:, CHUNK - 1:CHUNK, :])

    ys = []
    for g in range(2):
        bg = bcv[:, :, g * SSD_STATE:(g + 1) * SSD_STATE]
        cg = bcv[:, :, 256 + g * SSD_STATE:256 + (g + 1) * SSD_STATE]
        cg16 = cg.astype(BF16)
        cb = _bdot(cg16, bg.astype(BF16), (2, 2))
        for hh in range(nh // 2):
            h = g * (nh // 2) + hh
            cq = cum[:, :, h:h + 1]
            decay = jnp.exp(jnp.where(mask, cq - cum_t[:, h:h + 1, :], NEG_INF))
            xd = xs[:, :, h * SSD_HEAD_DIM:(h + 1) * SSD_HEAD_DIM] * dt[:, :, h:h + 1]
            xd16 = xd.astype(BF16)
            hprev = h_s[:, h]
            y = (_bdot((cb * decay).astype(BF16), xd16, (2, 1))
                 + _bdot(cg16, hprev.astype(BF16), (2, 2)) * jnp.exp(cq))
            ys.append(y)
            th = tot[:, :, h:h + 1]
            bw = bg * jnp.exp(th - cq)
            h_s[:, h] = hprev * jnp.exp(th) + _bdot(xd16, bw.astype(BF16), (1, 1))
    y = jnp.concatenate(ys, axis=2)

    @pl.when(jnp.logical_not(bwd))
    def _():
        yf_s[:, pl.ds(r0, CHUNK), :] = y

    @pl.when(bwd)
    def _():
        yt = (yf_s[:, pl.ds(r0, CHUNK), :] + y + drow_ref[...] * xs) * _silu(z_ref[...])
        y_ref[...] = _rms(yt) * ng_ref[...]

    @pl.when(s == 2 * nc - 1)
    def _():
        st_ref[:, 1] = h_s[...]


def _ssd(p0, dtt, cwx, cbx, cwb, cbb, a_log, dt_bias, d_skip, norm_g, st0, *, L):
    nseq, nc = p0.shape[0], L // CHUNK
    G = _scan_group(nseq, 4)
    cur, prev, nxt, out = _scan_specs(nc)
    st_shape = (G, 2, SSD_HEADS, SSD_HEAD_DIM, SSD_STATE)
    per_seq = st0 is not None
    if st0 is None:
        st0 = jnp.zeros(st_shape, F32)
    row = lambda v: v.reshape(1, -1)
    col = lambda v: v.reshape(-1, 1)
    small = [cwx, row(cbx), cwb, row(cbb), row(a_log), col(a_log), row(dt_bias), col(dt_bias),
             row(jnp.repeat(d_skip, SSD_HEAD_DIM)), row(norm_g)]
    blk = lambda rows, width, index: pl.BlockSpec((G, rows, width), index)
    return pl.pallas_call(
        functools.partial(_ssd_kernel, nc=nc),
        grid=(nseq // G, 2 * nc),
        in_specs=[blk(CHUNK, 1024, cur(0)),
                  blk(8, 1024, prev(1)), blk(CHUNK, 1024, cur(1)), blk(8, 1024, nxt(1)),
                  blk(8, 512, prev(4)), blk(CHUNK, 512, cur(4)), blk(8, 512, nxt(4)),
                  blk(CHUNK, LANES, cur(28)),
                  pl.BlockSpec((G, 1, 32, CHUNK), lambda b, s: (b, _chunk_index(s, nc), 0, 0))]
        + [_full(v.shape) for v in small]
        + [pl.BlockSpec(st_shape, lambda b, s: (b if per_seq else 0, 0, 0, 0, 0))],
        out_specs=[blk(CHUNK, SSD_D_INNER, out(0)),
                   pl.BlockSpec(st_shape, lambda b, s: (b, 0, 0, 0, 0))],
        out_shape=[jax.ShapeDtypeStruct((nseq, L, SSD_D_INNER), F32),
                   jax.ShapeDtypeStruct((nseq,) + st_shape[1:], F32)],
        scratch_shapes=[pltpu.VMEM((G, L, SSD_D_INNER), F32), pltpu.VMEM(st_shape[:1] + st_shape[2:], F32)],
        compiler_params=_cp(2), name="ssd_mixer",
    )(p0, p0, p0, p0, p0, p0, p0, p0, dtt, *small, st0)


def _rope_tables(L):
    t = jnp.arange(L)
    r = (t // GRID_W).astype(F32)
    c = (t % GRID_W).astype(F32)
    nf = 16
    inv = ROPE_BASE ** (-jnp.arange(nf, dtype=F32) / nf)
    ang = jnp.concatenate([r[:, None] * inv, c[:, None] * inv], axis=-1)
    ang = jnp.concatenate([ang, ang], axis=-1)
    return jnp.cos(ang), jnp.sin(ang)


def _rope128(x, cos, sin):
    lane = lax.broadcasted_iota(jnp.int32, x.shape, 1)
    rot = jnp.where(lane % 64 < 32, -pltpu.roll(x, 96, 1), pltpu.roll(x, 32, 1))
    return x * cos + rot * sin


def _pad_lanes(x):
    return jnp.concatenate([x, jnp.zeros_like(x)], axis=1)


MLA_QB = 256


def _mla_kernel(*refs, L, Lc, rope, emit):
    refs = list(refs)
    qa_ref, kva_ref, sm_ref, qg_ref, kvg_ref, wqn_ref, wqr_ref, wkk_ref, wkv_ref = refs[:9]
    pos = 9
    if Lc:
        cckv_ref, ckpe_ref = refs[pos:pos + 2]
        pos += 2
    if rope:
        cos_ref, sin_ref = refs[pos:pos + 2]
        pos += 2
    y_ref = refs[pos]
    pos += 1
    if emit:
        ckv_ref = refs[pos]
        pos += 1
    qn_s, qr_s, kn_s, v_s, kpe_s = refs[pos:]
    qb = pl.program_id(1)
    scale = np.float32((MLA_NOPE + MLA_ROPE) ** -0.5)

    @pl.when(qb == 0)
    def _():
        ckv = _rms(kva_ref[...]) * kvg_ref[...]
        if emit:
            ckv_ref[...] = ckv
        kpe = _pad_lanes(sm_ref[:, 0:MLA_ROPE])
        if rope:
            kpe = _rope128(kpe, cos_ref[...], sin_ref[...])
        if Lc:
            ckv = jnp.concatenate([cckv_ref[0], ckv], axis=0)
            kpe = jnp.concatenate([_pad_lanes(ckpe_ref[0]), kpe], axis=0)
        c16 = ckv.astype(BF16)
        kn_s[...] = _dot(c16, wkk_ref[...]).astype(BF16)
        v_s[...] = _dot(c16, wkv_ref[...]).astype(BF16)
        kpe_s[...] = kpe.astype(BF16)

    qn = (_rms(qa_ref[...]) * qg_ref[...]).astype(BF16)
    qn_s[...] = _dot(qn, wqn_ref[...]).astype(BF16)
    qr = _dot(qn, wqr_ref[...])
    if rope:
        q0 = pl.multiple_of(qb * MLA_QB, MLA_QB)
        cos = cos_ref[pl.ds(q0, MLA_QB), :]
        sin = sin_ref[pl.ds(q0, MLA_QB), :]
        for h in range(MLA_HEADS):
            qr_s[:, h * LANES:(h + 1) * LANES] = _rope128(qr[:, h * LANES:(h + 1) * LANES], cos, sin).astype(BF16)
    else:
        qr_s[...] = qr.astype(BF16)

    def head(h, _):
        c0 = pl.multiple_of(h * LANES, LANES)
        s = (_dot_nt(qn_s[:, pl.ds(c0, LANES)], kn_s[:, pl.ds(c0, LANES)])
             + _dot_nt(qr_s[:, pl.ds(c0, LANES)], kpe_s[...])) * scale
        p = jnp.exp(s - jnp.max(s, axis=-1, keepdims=True))
        p = p / jnp.sum(p, axis=-1, keepdims=True)
        y_ref[:, pl.ds(c0, LANES)] = _dot(p.astype(BF16), v_s[:, pl.ds(c0, LANES)])
        return 0

    lax.fori_loop(0, MLA_HEADS, head, 0)


def _mla(p0, qg, kvg, wqn, wqr, wkk, wkv, cache_ckv, cache_kpe, rope_tabs, *, L):
    n = p0.shape[0]
    nseq, nqb = n // L, L // MLA_QB
    Lc = 0 if cache_ckv is None else cache_ckv.shape[1]
    rope = rope_tabs is not None
    emit = cache_ckv is None
    Lk = L + Lc
    args = [p0, p0, p0, qg.reshape(1, -1), kvg.reshape(1, -1), wqn, wqr, wkk, wkv]
    specs = [pl.BlockSpec((MLA_QB, MLA_RANK), lambda b, q: (b * nqb + q, 5)),
             pl.BlockSpec((L, MLA_RANK), lambda b, q: (b, 6)),
             pl.BlockSpec((L, LANES), lambda b, q: (b, 28)),
             _full((1, MLA_RANK)), _full((1, MLA_RANK)), _full(wqn.shape), _full(wqr.shape), _full(wkk.shape),
             _full(wkv.shape)]
    if Lc:
        args += [cache_ckv, cache_kpe]
        specs += [pl.BlockSpec((1, Lc, MLA_RANK), lambda b, q: (b, 0, 0)),
                  pl.BlockSpec((1, Lc, MLA_ROPE), lambda b, q: (b, 0, 0))]
    if rope:
        args += list(rope_tabs)
        specs += [_full((L, LANES)), _full((L, LANES))]
    out_specs = [pl.BlockSpec((MLA_QB, 1024), lambda b, q: (b * nqb + q, 0))]
    out_shape = [jax.ShapeDtypeStruct((n, 1024), F32)]
    if emit:
        out_specs.append(pl.BlockSpec((L, MLA_RANK), lambda b, q: (b, 0)))
        out_shape.append(jax.ShapeDtypeStruct((n, MLA_RANK), F32))
    return pl.pallas_call(
        functools.partial(_mla_kernel, L=L, Lc=Lc, rope=rope, emit=emit),
        grid=(nseq, nqb), in_specs=specs, out_specs=out_specs, out_shape=out_shape,
        scratch_shapes=[pltpu.VMEM((MLA_QB, 1024), BF16), pltpu.VMEM((MLA_QB, 1024), BF16),
                        pltpu.VMEM((Lk, 1024), BF16), pltpu.VMEM((Lk, 1024), BF16), pltpu.VMEM((Lk, LANES), BF16)],
        compiler_params=_cp(2), name="mla_attention",
    )(*args)


def _gqa_ctx_kernel(q_ref, k_ref, v_ref, sink_ref, y_ref):
    q = q_ref[...]
    k = k_ref[...].astype(BF16)
    v = v_ref[...].astype(BF16)
    scale = np.float32(GQA_HEAD_DIM ** -0.5)
    group = GQA_HEADS // GQA_KV_HEADS
    outs = []
    for h in range(GQA_HEADS):
        kh = h // group
        ks = slice(kh * GQA_HEAD_DIM, (kh + 1) * GQA_HEAD_DIM)
        s = _dot_nt(q[:, h * GQA_HEAD_DIM:(h + 1) * GQA_HEAD_DIM].astype(BF16), k[:, ks]) * scale
        m = jnp.maximum(jnp.max(s, axis=-1, keepdims=True), sink_ref[h])
        p = jnp.exp(s - m)
        p = p / (jnp.sum(p, axis=-1, keepdims=True) + jnp.exp(sink_ref[h] - m))
        outs.append(_dot(p.astype(BF16), v[:, ks]))
    y_ref[...] = jnp.concatenate(outs, axis=1)


def _gqa_ctx(p1, sink, *, L):
    n = p1.shape[0]
    return pl.pallas_call(
        _gqa_ctx_kernel,
        grid=(n // L,),
        in_specs=[pl.BlockSpec((L, 1024), lambda b: (b, 2)), pl.BlockSpec((L, 256), lambda b: (b, 20)),
                  pl.BlockSpec((L, 256), lambda b: (b, 21)), pl.BlockSpec(memory_space=pltpu.SMEM)],
        out_specs=pl.BlockSpec((L, 1024), lambda b: (b, 0)),
        out_shape=jax.ShapeDtypeStruct((n, 1024), F32),
        compiler_params=_cp(1), name="gqa_context",
    )(p1, p1, p1, sink)


GQA_QB = 128
GQA_SPAN = GQA_QB + 2 * WINDOW


def _gqa_win_kernel(q_ref, k_ref, v_ref, ck_ref, cv_ref, cos_ref, sin_ref, sink_ref, y_ref, kp_s, vp_s, *, L):
    qb = pl.program_id(1)
    scale = np.float32(GQA_HEAD_DIM ** -0.5)
    group = GQA_HEADS // GQA_KV_HEADS

    @pl.when(qb == 0)
    def _():
        zeros = jnp.zeros((WINDOW, 256), BF16)
        kp_s[0:WINDOW, :] = zeros
        kp_s[WINDOW + L:, :] = zeros
        vp_s[0:WINDOW, :] = zeros
        vp_s[WINDOW + L:, :] = zeros
        for j in range(2):
            cs = slice(j * LANES, (j + 1) * LANES)
            kp_s[WINDOW:WINDOW + L, cs] = _rope128(k_ref[:, cs], cos_ref[...], sin_ref[...]).astype(BF16)
        vp_s[WINDOW:WINDOW + L, :] = v_ref[...].astype(BF16)

    q0 = pl.multiple_of(qb * GQA_QB, GQA_QB)
    cos = cos_ref[pl.ds(q0, GQA_QB), :]
    sin = sin_ref[pl.ds(q0, GQA_QB), :]
    kw = kp_s[pl.ds(q0, GQA_SPAN), :]
    vw = vp_s[pl.ds(q0, GQA_SPAN), :]
    kc = ck_ref[0].astype(BF16)
    vc = cv_ref[0].astype(BF16)
    qpos = q0 + lax.broadcasted_iota(jnp.int32, (GQA_QB, GQA_SPAN), 0)
    kpos = q0 - WINDOW + lax.broadcasted_iota(jnp.int32, (GQA_QB, GQA_SPAN), 1)
    bias = jnp.where(kpos < 0, NEG_INF, jnp.where(kpos >= L, NEG_INF, jnp.where(jnp.abs(qpos - kpos) <= WINDOW, 0.0, NEG_INF)))
    outs = []
    for j in range(GQA_HEADS * GQA_HEAD_DIM // LANES):
        qj = _rope128(q_ref[:, j * LANES:(j + 1) * LANES], cos, sin).astype(BF16)
        for half in range(2):
            h = 2 * j + half
            kh = h // group
            ks = slice(kh * GQA_HEAD_DIM, (kh + 1) * GQA_HEAD_DIM)
            qh = qj[:, half * GQA_HEAD_DIM:(half + 1) * GQA_HEAD_DIM]
            s_loc = _dot_nt(qh, kw[:, ks]) * scale + bias
            s_ctx = _dot_nt(qh, kc[:, ks]) * scale
            m = jnp.maximum(jnp.maximum(jnp.max(s_loc, axis=-1, keepdims=True), jnp.max(s_ctx, axis=-1, keepdims=True)),
                            sink_ref[h])
            p_loc = jnp.exp(s_loc - m)
            p_ctx = jnp.exp(s_ctx - m)
            inv = 1.0 / (jnp.sum(p_loc, axis=-1, keepdims=True) + jnp.sum(p_ctx, axis=-1, keepdims=True)
                         + jnp.exp(sink_ref[h] - m))
            outs.append(_dot((p_loc * inv).astype(BF16), vw[:, ks]) + _dot((p_ctx * inv).astype(BF16), vc[:, ks]))
    y_ref[...] = jnp.concatenate(outs, axis=1)


def _gqa_win(p1, cache_k, cache_v, cos, sin, sink, *, L):
    n = p1.shape[0]
    nseq, nqb = n // L, L // GQA_QB
    Lc = cache_k.shape[1]
    return pl.pallas_call(
        functools.partial(_gqa_win_kernel, L=L),
        grid=(nseq, nqb),
        in_specs=[pl.BlockSpec((GQA_QB, 1024), lambda b, q: (b * nqb + q, 2)),
                  pl.BlockSpec((L, 256), lambda b, q: (b, 20)), pl.BlockSpec((L, 256), lambda b, q: (b, 21)),
                  pl.BlockSpec((1, Lc, 256), lambda b, q: (b, 0, 0)), pl.BlockSpec((1, Lc, 256), lambda b, q: (b, 0, 0)),
                  _full((L, LANES)), _full((L, LANES)), pl.BlockSpec(memory_space=pltpu.SMEM)],
        out_specs=pl.BlockSpec((GQA_QB, 1024), lambda b, q: (b * nqb + q, 0)),
        out_shape=jax.ShapeDtypeStruct((n, 1024), F32),
        scratch_shapes=[pltpu.VMEM((L + 2 * WINDOW, 256), BF16), pltpu.VMEM((L + 2 * WINDOW, 256), BF16)],
        compiler_params=_cp(2), name="gqa_window",
    )(p1, p1, p1, cache_k, cache_v, cos, sin, sink)


def _mlstm_kernel(qp_ref, qc_ref, qn_ref, v_ref, o_ref, sm_ref, gt_ref, cw_ref, cb_ref, brow_ref, bcol_ref, ng_ref,
                  c0_ref, n0_ref, m0_ref, y_ref, cst_ref, nst_ref, mst_ref, hf_s, c_s, n_s, m_s, *, nc):
    s = pl.program_id(1)
    bwd = s >= nc
    cidx = _chunk_index(s, nc)
    nh = ML_HEADS
    hd = ML_HEAD_DIM

    @pl.when(s == 0)
    def _():
        c_s[...] = c0_ref[:, 0]
        n_s[...] = n0_ref[:, 0]
        m_s[...] = m0_ref[:, 0]

    @pl.when(s == nc)
    def _():
        cst_ref[:, 0] = c_s[...]
        nst_ref[:, 0] = n_s[...]
        mst_ref[:, 0] = m_s[...]
        c_s[...] = c0_ref[:, 1]
        n_s[...] = n0_ref[:, 1]
        m_s[...] = m0_ref[:, 1]

    G = qc_ref.shape[0]
    tsel, tsel_t = _tri(bwd, G)
    mask = tsel > 0.0
    kscale = np.float32(hd ** -0.5)
    r0 = pl.multiple_of(cidx * CHUNK, CHUNK)
    hi = lax.Precision.HIGHEST

    qk = _conv_silu(qp_ref[...], qc_ref[...], qn_ref[...], cw_ref, cb_ref, cidx == 0, cidx == nc - 1)
    gates = sm_ref[:, :, 0:4 * nh] + brow_ref[...]
    gates_t = gt_ref[:, 0] + bcol_ref[...]
    li = jnp.where(bwd, gates[:, :, nh:2 * nh], gates[:, :, 0:nh])
    li_t = jnp.where(bwd, gates_t[:, nh:2 * nh, :], gates_t[:, 0:nh, :])
    lf = -_softplus(-jnp.where(bwd, gates[:, :, 3 * nh:4 * nh], gates[:, :, 2 * nh:3 * nh]))
    lf_t = -_softplus(-jnp.where(bwd, gates_t[:, 3 * nh:4 * nh, :], gates_t[:, 2 * nh:3 * nh, :]))
    bc = _bdot(tsel, lf, (2, 1), hi)
    bc_t = _bdot(lf_t, tsel_t, (2, 1), hi)
    tot = jnp.where(bwd, bc[:, 0:1, :], bc[:, CHUNK - 1:CHUNK, :])

    hs = []
    for h in range(nh):
        cs = slice(h * hd, (h + 1) * hd)
        q = qk[:, :, cs]
        k = qk[:, :, ML_D + h * hd:ML_D + (h + 1) * hd] * kscale
        q16, k16 = q.astype(BF16), k.astype(BF16)
        v = v_ref[:, :, cs]
        bq = bc[:, :, h:h + 1]
        m_prev = m_s[:, h:h + 1, 0:1]
        log_d = jnp.where(mask, bq - bc_t[:, h:h + 1, :] + li_t[:, h:h + 1, :], NEG_INF)
        log_inter = bq + m_prev
        m_out = jnp.maximum(log_inter, jnp.max(log_d, axis=2, keepdims=True))
        sd = _bdot(q16, k16, (2, 2)) * jnp.exp(log_d - m_out)
        w_inter = jnp.exp(log_inter - m_out)
        c_prev = c_s[:, h]
        n_prev = n_s[:, h:h + 1, :]
        num = _bdot(sd.astype(BF16), v.astype(BF16), (2, 1)) + w_inter * _bdot(q16, c_prev.astype(BF16), (2, 2))
        den = jnp.sum(sd, axis=2, keepdims=True) + w_inter * jnp.sum(q * n_prev, axis=2, keepdims=True)
        hs.append(num / jnp.maximum(jnp.abs(den), jnp.exp(-m_out)))
        th = tot[:, :, h:h + 1]
        end_inter = th + m_prev
        end_intra = th - bq + li[:, :, h:h + 1]
        m_new = jnp.maximum(end_inter, jnp.max(end_intra, axis=1, keepdims=True))
        w_c = jnp.exp(end_inter - m_new)
        w_k = jnp.exp(end_intra - m_new)
        c_s[:, h] = w_c * c_prev + _bdot((v * w_k).astype(BF16), k16, (1, 1))
        n_s[:, h:h + 1, :] = w_c * n_prev + jnp.sum(k * w_k, axis=1, keepdims=True)
        m_s[:, h:h + 1, :] = jnp.broadcast_to(m_new, (G, 1, hd))
    hcat = jnp.concatenate(hs, axis=2)

    @pl.when(jnp.logical_not(bwd))
    def _():
        hf_s[:, pl.ds(r0, CHUNK), :] = hcat

    @pl.when(bwd)
    def _():
        tot_h = hf_s[:, pl.ds(r0, CHUNK), :] + hcat
        normed = jnp.concatenate([_rms(tot_h[:, :, h * hd:(h + 1) * hd]) for h in range(nh)], axis=2) * ng_ref[...]
        y_ref[...] = normed * jax.nn.sigmoid(o_ref[...])

    @pl.when(s == 2 * nc - 1)
    def _():
        cst_ref[:, 1] = c_s[...]
        nst_ref[:, 1] = n_s[...]
        mst_ref[:, 1] = m_s[...]


def _mlstm(p1, gt, conv_w, conv_b, ig_b, fg_b, norm_g, c0, n0, m0, *, L):
    nseq, nc = p1.shape[0], L // CHUNK
    G = _scan_group(nseq, 4)
    cur, prev, nxt, out = _scan_specs(nc)
    cshape = (G, 2, ML_HEADS, ML_HEAD_DIM, ML_HEAD_DIM)
    nshape = (G, 2, ML_HEADS, ML_HEAD_DIM)
    per_seq = c0 is not None
    if c0 is None:
        c0, n0, m0 = jnp.zeros(cshape, F32), jnp.zeros(nshape, F32), jnp.zeros(nshape, F32)
    gb = jnp.concatenate([ig_b.reshape(-1), fg_b.reshape(-1)])
    small = [conv_w, conv_b.reshape(1, -1), gb.reshape(1, -1), gb.reshape(-1, 1), norm_g.reshape(1, -1)]
    st_in = lambda shape: pl.BlockSpec(shape, lambda b, s: (b if per_seq else 0,) + (0,) * (len(shape) - 1))
    st_out = lambda shape: pl.BlockSpec(shape, lambda b, s: (b,) + (0,) * (len(shape) - 1))
    blk = lambda rows, width, index: pl.BlockSpec((G, rows, width), index)
    return pl.pallas_call(
        functools.partial(_mlstm_kernel, nc=nc),
        grid=(nseq // G, 2 * nc),
        in_specs=[blk(8, 2048, prev(0)), blk(CHUNK, 2048, cur(0)), blk(8, 2048, nxt(0)),
                  blk(CHUNK, 1024, cur(3)), blk(CHUNK, 1024, cur(4)), blk(CHUNK, LANES, cur(44)),
                  pl.BlockSpec((G, 1, 32, CHUNK), lambda b, s: (b, _chunk_index(s, nc), 0, 0))]
        + [_full(v.shape) for v in small] + [st_in(cshape), st_in(nshape), st_in(nshape)],
        out_specs=[blk(CHUNK, ML_D, out(0)), st_out(cshape), st_out(nshape), st_out(nshape)],
        out_shape=[jax.ShapeDtypeStruct((nseq, L, ML_D), F32), jax.ShapeDtypeStruct((nseq,) + cshape[1:], F32),
                   jax.ShapeDtypeStruct((nseq,) + nshape[1:], F32), jax.ShapeDtypeStruct((nseq,) + nshape[1:], F32)],
        scratch_shapes=[pltpu.VMEM((G, L, ML_D), F32), pltpu.VMEM(cshape[:1] + cshape[2:], F32),
                        pltpu.VMEM(nshape[:1] + nshape[2:], F32), pltpu.VMEM(nshape[:1] + nshape[2:], F32)],
        compiler_params=_cp(2), name="mlstm_mixer",
    )(p1, p1, p1, p1, p1, p1, gt, *small, c0, n0, m0)


def _prep_layer0(w_in, w_qb, w_kvb, w_out):
    pad = jnp.zeros((D_MODEL, L0_COLS - 3680), F32)
    w = jnp.concatenate([w_in[:, :2560], w_in[:, 2592:3680], w_in[:, 2560:2592], pad], axis=1).astype(BF16)
    q3 = w_qb.reshape(MLA_RANK, MLA_HEADS, MLA_NOPE + MLA_ROPE)
    wqn = q3[:, :, :MLA_NOPE].reshape(MLA_RANK, -1).astype(BF16)
    wqr = jnp.concatenate([q3[:, :, MLA_NOPE:], jnp.zeros((MLA_RANK, MLA_HEADS, LANES - MLA_ROPE), F32)], axis=-1)
    wqr = wqr.reshape(MLA_RANK, -1).astype(BF16)
    k3 = w_kvb.reshape(MLA_RANK, MLA_HEADS, MLA_NOPE + MLA_V)
    wkk = k3[:, :, :MLA_NOPE].reshape(MLA_RANK, -1).astype(BF16)
    wkv = k3[:, :, MLA_NOPE:].reshape(MLA_RANK, -1).astype(BF16)
    return w, wqn, wqr, wkk, wkv, w_out.astype(BF16)


def _prep_layer1(w_in, w_out):
    pad = jnp.zeros((D_MODEL, L1_COLS - 5664), F32)
    w = jnp.concatenate([w_in[:, 1536:3584], w_in[:, 0:1024], w_in[:, 3584:5632], w_in[:, 1024:1536],
                         w_in[:, 5632:5664], pad], axis=1).astype(BF16)
    return w, w_out.astype(BF16)


def _prep_peer(wq, keys, u, v):
    return (wq.T.astype(BF16), keys.reshape(PEER_HEADS * 2, PEER_NKEYS, -1).astype(BF16), u.astype(BF16),
            v.T.astype(BF16))


def _chunk_transposed(cols, L):
    return cols.reshape(-1, L // CHUNK, CHUNK, cols.shape[1]).transpose(0, 1, 3, 2)


def _trunk(x, L, mod0, mod1, w0, w1, peer0, peer1, p, ctx, rope_tabs, mod_base, rows_per_cond, fg):
    kw = dict(mod_base=mod_base, rows_per_cond=rows_per_cond)
    w_in0, wqn, wqr, wkk, wkv, w_out0 = w0
    w_in1, w_out1 = w1
    nseq = x.shape[0] // L

    p0 = _proj(x, p['l0_norm1_g'], mod0, w_in0, tn=1280, **kw)
    cw, cb = p['l0_ssd_conv_w'], p['l0_ssd_conv_b']
    y_ssd, ssd_state = _ssd(p0.reshape(nseq, L, -1), _chunk_transposed(p0[:, 3648:3680], L), cw[:, :1024], cb[:1024],
                            cw[:, 1024:], cb[1024:], p['l0_ssd_A_log'], p['l0_ssd_dt_bias'], p['l0_ssd_D'],
                            p['l0_ssd_norm_g'], None if ctx is None else ctx[0], L=L)
    y_ssd = y_ssd.reshape(nseq * L, -1)
    mla_tabs = None
    if rope_tabs is not None:
        z64 = jnp.zeros_like(rope_tabs[0])
        mla_tabs = (jnp.concatenate([rope_tabs[0], z64], axis=1), jnp.concatenate([rope_tabs[1], z64], axis=1))
    mla_out = _mla(p0, p['l0_mla_q_norm_g'], p['l0_mla_kv_norm_g'], wqn, wqr, wkk, wkv,
                   None if ctx is None else ctx[1], None if ctx is None else ctx[2], mla_tabs, L=L)
    x = _outproj(y_ssd, mla_out[0], w_out0, x, mod0, **kw)
    x = _peer(x, p['l0_norm2_g'], mod0, *peer0, fg, final_norm=False, **kw)

    p1 = _proj(x, p['l1_norm1_g'], mod1, w_in1, tn=1920, **kw)
    if ctx is None:
        y_gqa = _gqa_ctx(p1, p['l1_gqa_sink'], L=L)
        c0 = n0 = m0 = None
    else:
        gqa_tabs = tuple(jnp.concatenate([t, t], axis=1) for t in rope_tabs)
        y_gqa = _gqa_win(p1, ctx[3].reshape(nseq, -1, 256), ctx[4].reshape(nseq, -1, 256), *gqa_tabs,
                         p['l1_gqa_sink'], L=L)
        c0, n0 = ctx[5], ctx[6]
        m0 = jnp.broadcast_to(ctx[7][..., None], ctx[7].shape + (ML_HEAD_DIM,))
    y_ml, mc, mn, mm = _mlstm(p1.reshape(nseq, L, -1), _chunk_transposed(p1[:, 5632:5664], L), p['l1_ml_conv_w'],
                              p['l1_ml_conv_b'], p['l1_ml_ig_b'], p['l1_ml_fg_b'], p['l1_ml_norm_g'], c0, n0, m0, L=L)
    y_ml = y_ml.reshape(nseq * L, -1)
    x = _outproj(y_gqa, y_ml, w_out1, x, mod1, **kw)
    y = _peer(x, p['l1_norm2_g'], mod1, *peer1, fg, final_norm=True, **kw)
    new = None
    if ctx is None:
        new = (ssd_state, mla_out[1].reshape(nseq, L, MLA_RANK), p0[:, 3584:3648].reshape(nseq, L, MLA_ROPE),
               p1[:, 5120:5376].reshape(nseq, L, GQA_KV_HEADS, GQA_HEAD_DIM),
               p1[:, 5376:5632].reshape(nseq, L, GQA_KV_HEADS, GQA_HEAD_DIM), mc, mn, mm[..., 0])
    return y, new


def kernel(x_prompt, x_sample, state_l0_ssd, cache_l0_mla_ckv, cache_l0_mla_kpe, cache_l1_gqa_k, cache_l1_gqa_v, state_l1_mlstm_C, state_l1_mlstm_n, state_l1_mlstm_m, c, c_ctx, final_norm_g, l0_ada_w, l0_ada_b, l0_norm1_g, l0_norm2_g, l0_w_in, l0_ssd_conv_w, l0_ssd_conv_b, l0_ssd_A_log, l0_ssd_dt_bias, l0_ssd_D, l0_ssd_norm_g, l0_mla_q_norm_g, l0_mla_w_qb, l0_mla_kv_norm_g, l0_mla_w_kvb, l0_w_out, l0_peer_wq, l0_peer_keys, l0_peer_u, l0_peer_v, l1_ada_w, l1_ada_b, l1_norm1_g, l1_norm2_g, l1_w_in, l1_gqa_sink, l1_ml_conv_w, l1_ml_conv_b, l1_ml_ig_b, l1_ml_fg_b, l1_ml_norm_g, l1_w_out, l1_peer_wq, l1_peer_keys, l1_peer_u, l1_peer_v):
    p = dict(l0_norm1_g=l0_norm1_g, l0_norm2_g=l0_norm2_g, l0_ssd_conv_w=l0_ssd_conv_w, l0_ssd_conv_b=l0_ssd_conv_b,
             l0_ssd_A_log=l0_ssd_A_log, l0_ssd_dt_bias=l0_ssd_dt_bias, l0_ssd_D=l0_ssd_D, l0_ssd_norm_g=l0_ssd_norm_g,
             l0_mla_q_norm_g=l0_mla_q_norm_g, l0_mla_kv_norm_g=l0_mla_kv_norm_g, l1_norm1_g=l1_norm1_g,
             l1_norm2_g=l1_norm2_g, l1_gqa_sink=l1_gqa_sink, l1_ml_conv_w=l1_ml_conv_w, l1_ml_conv_b=l1_ml_conv_b,
             l1_ml_ig_b=l1_ml_ig_b, l1_ml_fg_b=l1_ml_fg_b, l1_ml_norm_g=l1_ml_norm_g)
    nb, seq = x_prompt.shape[:2]
    db, dseq = x_sample.shape[:2]
    assert db <= 7 and seq % MLA_QB == 0 and dseq % MLA_QB == 0

    cond8 = jnp.zeros((8, D_MODEL), F32).at[0].set(c_ctx).at[1:1 + db].set(c)
    mod0 = _ada(cond8, l0_ada_w, l0_ada_b).reshape(8, 1, 6 * D_MODEL)
    mod1 = _ada(cond8, l1_ada_w, l1_ada_b).reshape(8, 1, 6 * D_MODEL)
    w0 = _prep_layer0(l0_w_in, l0_mla_w_qb, l0_mla_w_kvb, l0_w_out)
    w1 = _prep_layer1(l1_w_in, l1_w_out)
    peer0 = _prep_peer(l0_peer_wq, l0_peer_keys, l0_peer_u, l0_peer_v)
    peer1 = _prep_peer(l1_peer_wq, l1_peer_keys, l1_peer_u, l1_peer_v)

    y_prompt, new = _trunk(x_prompt.reshape(nb * seq, D_MODEL), seq, mod0, mod1, w0, w1, peer0, peer1, p, None, None,
                           0, nb * seq, final_norm_g)
    ctx = (state_l0_ssd, cache_l0_mla_ckv, cache_l0_mla_kpe, cache_l1_gqa_k, cache_l1_gqa_v, state_l1_mlstm_C,
           state_l1_mlstm_n, state_l1_mlstm_m)
    y_sample, _ = _trunk(x_sample.reshape(db * dseq, D_MODEL), dseq, mod0, mod1, w0, w1, peer0, peer1, p, ctx,
                         _rope_tables(dseq), 1, dseq, final_norm_g)
    return (y_prompt.reshape(nb, seq, D_MODEL), y_sample.reshape(db, dseq, D_MODEL)) + new
```

```python
import functools

import jax
import jax.numpy as jnp
import numpy as np
from jax import lax
from jax.experimental import pallas as pl
from jax.experimental.pallas import tpu as pltpu

F32 = jnp.float32
BF16 = jnp.bfloat16
NEG_INF = float("-inf")

D_MODEL = 2048
EPS = 1e-6
ROPE_BASE = 10000.0
GRID_W = 64
CONV_W = 5
CHUNK = 64
SSD_HEADS = 16
SSD_HEAD_DIM = 64
SSD_D_INNER = 1024
SSD_STATE = 128
MLA_HEADS = 8
MLA_NOPE = 128
MLA_ROPE = 64
MLA_V = 128
MLA_RANK = 512
GQA_HEADS = 16
GQA_KV_HEADS = 4
GQA_HEAD_DIM = 64
WINDOW = 128
ML_HEADS = 8
ML_HEAD_DIM = 128
ML_D = 1024
PEER_HEADS = 8
PEER_NKEYS = 128
PEER_N = PEER_NKEYS * PEER_NKEYS
PEER_TOPK = 16

LANES = 128
SUBLANES = 8
VMEM_LIMIT = 56 * 2**20

L0_COLS = 3840
L1_COLS = 5760


def _cp(n_grid, flags=None):
    return pltpu.CompilerParams(dimension_semantics=("arbitrary",) * n_grid, vmem_limit_bytes=VMEM_LIMIT, flags=flags)


def _silu(x):
    return x * jax.nn.sigmoid(x)


def _softplus(x):
    return jnp.maximum(x, 0.0) + jnp.log1p(jnp.exp(-jnp.abs(x)))


def _rms(x):
    return x * lax.rsqrt(jnp.mean(x * x, axis=-1, keepdims=True) + EPS)


def _dot(a, b):
    return jnp.dot(a, b, preferred_element_type=F32)


def _dot_nt(a, b):
    return lax.dot_general(a, b, (((1,), (1,)), ((), ())), preferred_element_type=F32)


def _dot_tn(a, b):
    return lax.dot_general(a, b, (((0,), (0,)), ((), ())), preferred_element_type=F32)


def _dot_f32(a, b):
    return jnp.dot(a, b, preferred_element_type=F32, precision=lax.Precision.HIGHEST)


def _full(shape):
    nd = len(shape)
    return pl.BlockSpec(shape, lambda *_: (0,) * nd)


def _mod_spec(chunk, tb, mod_base, rows_per_cond):
    def index(i, *_):
        return (mod_base + (i * tb) // rows_per_cond, 0, chunk)
    return pl.BlockSpec((1, 1, D_MODEL), index)


def _ada_kernel(c_ref, w_ref, b_ref, o_ref):
    s = _silu(c_ref[...])
    o_ref[...] = _dot(s.astype(BF16), w_ref[...].astype(BF16)) + b_ref[...]


def _ada(cond8, w, b):
    n = w.shape[1]
    tn = 1024
    return pl.pallas_call(
        _ada_kernel,
        grid=(n // tn,),
        in_specs=[_full((8, D_MODEL)), pl.BlockSpec((D_MODEL, tn), lambda j: (0, j)),
                  pl.BlockSpec((1, tn), lambda j: (0, j))],
        out_specs=pl.BlockSpec((8, tn), lambda j: (0, j)),
        out_shape=jax.ShapeDtypeStruct((8, n), F32),
        compiler_params=_cp(1), name="ada_table",
    )(cond8, w, b.reshape(1, n))


def _proj_kernel(x_ref, g_ref, sc_ref, sh_ref, w_ref, o_ref, h_ref):
    @pl.when(pl.program_id(1) == 0)
    def _():
        h = _rms(x_ref[...]) * g_ref[...] * (1.0 + sc_ref[0]) + sh_ref[0]
        h_ref[...] = h.astype(BF16)
    o_ref[...] = _dot(h_ref[...], w_ref[...])


def _proj(x, g, mod, w, *, tn, mod_base, rows_per_cond):
    n, ncol = x.shape[0], w.shape[1]
    tb = 1024
    return pl.pallas_call(
        _proj_kernel,
        grid=(n // tb, ncol // tn),
        in_specs=[pl.BlockSpec((tb, D_MODEL), lambda i, j: (i, 0)), _full((1, D_MODEL)),
                  _mod_spec(1, tb, mod_base, rows_per_cond), _mod_spec(0, tb, mod_base, rows_per_cond),
                  pl.BlockSpec((D_MODEL, tn), lambda i, j: (0, j))],
        out_specs=pl.BlockSpec((tb, tn), lambda i, j: (i, j)),
        out_shape=jax.ShapeDtypeStruct((n, ncol), F32),
        scratch_shapes=[pltpu.VMEM((tb, D_MODEL), BF16)],
        compiler_params=_cp(2), name="norm_mod_proj",
    )(x, g.reshape(1, D_MODEL), mod, mod, w)


def _outproj_kernel(ya_ref, yb_ref, w_ref, x_ref, g1_ref, o_ref):
    half = ya_ref.shape[1]
    y = _dot(ya_ref[...].astype(BF16), w_ref[0:half, :]) + _dot(yb_ref[...].astype(BF16), w_ref[half:, :])
    o_ref[...] = x_ref[...] + g1_ref[0] * y


def _outproj(ya, yb, w, x, mod, *, mod_base, rows_per_cond):
    n = x.shape[0]
    tb = 512
    return pl.pallas_call(
        _outproj_kernel,
        grid=(n // tb,),
        in_specs=[pl.BlockSpec((tb, ya.shape[1]), lambda i: (i, 0)), pl.BlockSpec((tb, yb.shape[1]), lambda i: (i, 0)),
                  _full(w.shape), pl.BlockSpec((tb, D_MODEL), lambda i: (i, 0)),
                  _mod_spec(2, tb, mod_base, rows_per_cond)],
        out_specs=pl.BlockSpec((tb, D_MODEL), lambda i: (i, 0)),
        out_shape=jax.ShapeDtypeStruct((n, D_MODEL), F32),
        compiler_params=_cp(1), name="out_proj_residual",
    )(ya, yb, w, x, mod)


ROUTE_TB = 512
_CAND_KEEP = (16, 8, 5, 4, 3, 2, 2, 2)


def _sort_pairs(n):
    pairs = []

    def merge(lo, hi, r):
        step = r * 2
        if step < hi - lo:
            merge(lo, hi, step)
            merge(lo + r, hi, step)
            pairs.extend((i, i + r) for i in range(lo + r, hi - r, step))
        else:
            pairs.append((lo, lo + r))

    def sort(lo, hi):
        if hi - lo >= 1:
            mid = lo + (hi - lo) // 2
            sort(lo, mid)
            sort(mid + 1, hi)
            merge(lo, hi, 1)

    sort(0, 15)
    return [(i, j) for i, j in pairs if j < n]


def _top_values(x, count):
    groups = [x[r:r + SUBLANES] for r in range(0, x.shape[0], SUBLANES)]
    n = len(groups)
    for i, j in _sort_pairs(n):
        groups[i], groups[j] = jnp.maximum(groups[i], groups[j]), jnp.minimum(groups[i], groups[j])
    sub = lax.broadcasted_iota(jnp.int32, groups[0].shape, 0)
    tops = []
    for t in range(count):
        head = groups[0]
        m = jnp.max(head, axis=0, keepdims=True)
        tops.append(m)
        if t == count - 1:
            break
        first = jnp.min(jnp.where(head == m, sub, SUBLANES), axis=0, keepdims=True)
        popped = sub == first
        depth = min(count - 1 - t, n)
        for r in range(depth):
            below = groups[r + 1] if r + 1 < n else NEG_INF
            groups[r] = jnp.where(popped, below, groups[r])
    return tops


def _route_kernel(x_ref, g_ref, sc_ref, sh_ref, wqt_ref, keys_ref,
                  hb_ref, th_ref, s2_ref, e2_ref, c1_ref, qt_s, top_s):
    tb = x_ref.shape[0]
    h = _rms(x_ref[...]) * g_ref[...] * (1.0 + sc_ref[0]) + sh_ref[0]
    hb = h.astype(BF16)
    hb_ref[...] = hb
    qt_s[...] = _dot_nt(wqt_ref[...], hb).astype(BF16)

    iota_8 = lax.broadcasted_iota(jnp.int32, (SUBLANES, tb), 0)

    def head(hd, _):
        scores = []
        for c in range(2):
            r0 = pl.multiple_of((hd * 2 + c) * PEER_NKEYS, PEER_NKEYS)
            s = _dot(keys_ref[hd * 2 + c], qt_s[pl.ds(r0, PEER_NKEYS), :])
            scores.append(s)
            for it, m in enumerate(_top_values(s, PEER_TOPK)):
                top_s[c, it:it + 1, :] = m
        a1 = top_s[0]
        a2 = top_s[1]
        pieces = [a1[0:1] + a2, a1[1:2] + a2[0:8]]
        for i in range(2, 8):
            pieces.append(jnp.where(iota_8 < _CAND_KEEP[i], a1[i:i + 1] + a2[0:8], NEG_INF))
        pieces.append(a1[8:16] + a2[0:1])
        cand = jnp.concatenate(pieces, axis=0)
        tau = _top_values(cand, PEER_TOPK)[-1]
        top = a1[0:1] + a2[0:1]
        z = jnp.sum(jnp.where(cand >= tau, jnp.exp(cand - top), 0.0), axis=0, keepdims=True)
        e2 = jnp.exp(scores[1] - a2[0:1])
        c1 = jnp.exp(scores[0] - a1[0:1]) / z
        theta = jnp.full(scores[0].shape, jnp.inf, F32)
        for q in range(PEER_TOPK):
            a2q = a2[q:q + 1]
            phi = jnp.min(jnp.where(a1 + a2q >= tau, a1, jnp.inf), axis=0, keepdims=True)
            theta = jnp.where(scores[0] >= phi, a2q, theta)
        for t in range(tb // LANES):
            ts = slice(t * LANES, (t + 1) * LANES)
            th_ref[hd, t] = theta[:, ts]
            s2_ref[hd, t] = scores[1][:, ts]
            e2_ref[hd, t] = e2[:, ts]
            c1_ref[hd, t] = c1[:, ts]
        return 0

    lax.fori_loop(0, PEER_HEADS, head, 0)


def _route(x, g, mod, wqt, keys, *, mod_base, rows_per_cond):
    n = x.shape[0]
    tb = ROUTE_TB
    ntg = tb // LANES
    hk = (PEER_HEADS, n // LANES, PEER_NKEYS, LANES)
    rspec = pl.BlockSpec((PEER_HEADS, ntg, PEER_NKEYS, LANES), lambda i: (0, i, 0, 0))
    return pl.pallas_call(
        _route_kernel,
        grid=(n // tb,),
        in_specs=[pl.BlockSpec((tb, D_MODEL), lambda i: (i, 0)), _full((1, D_MODEL)),
                  _mod_spec(4, tb, mod_base, rows_per_cond), _mod_spec(3, tb, mod_base, rows_per_cond),
                  _full(wqt.shape), _full(keys.shape)],
        out_specs=[pl.BlockSpec((tb, D_MODEL), lambda i: (i, 0)), rspec, rspec, rspec, rspec],
        out_shape=[jax.ShapeDtypeStruct((n, D_MODEL), BF16)] + [jax.ShapeDtypeStruct(hk, F32)] * 4,
        scratch_shapes=[pltpu.VMEM((PEER_HEADS * 2 * PEER_NKEYS, tb), BF16), pltpu.VMEM((2, PEER_TOPK, tb), F32)],
        compiler_params=_cp(1), name="peer_route",
    )(x, g.reshape(1, D_MODEL), mod, mod, wqt, keys)


PEER_TB = 512
PEER_EC = 1024
PEER_SUB = PEER_EC // 2


def _experts_kernel(hb_ref, th_ref, s2_ref, e2_ref, c1_ref, u0_ref, ub_ref, un_ref, vp_ref, va_ref, vl_ref,
                    x_ref, g2_ref, fg_ref, o_ref, acc_s, st0_s, st1_s, a0_s, a1_s, *, final_norm):
    e = pl.program_id(1)
    tb = hb_ref.shape[0]
    ntg = tb // LANES
    rows_per_sub = PEER_SUB // PEER_NKEYS

    def scores(u_blk, st_s, h):
        res = _dot_nt(u_blk[...], hb_ref[h * 2 * LANES:(h + 1) * 2 * LANES, :])
        st_s[2 * h] = res[:, :LANES]
        st_s[2 * h + 1] = res[:, LANES:]

    def values(vt_blk, act_s, h):
        act = jnp.concatenate([act_s[2 * h], act_s[2 * h + 1]], axis=1)
        hs = slice(h * 2 * LANES, (h + 1) * 2 * LANES)
        for r in range(0, D_MODEL, PEER_SUB):
            acc_s[r:r + PEER_SUB, hs] += _dot(vt_blk[r:r + PEER_SUB, :], act)

    def gates(c, tg, st_s, act_s):
        sub = 32
        for k in range(rows_per_sub):
            row = c * rows_per_sub + k
            for j0 in range(0, PEER_NKEYS, sub):
                js = slice(j0, j0 + sub)
                ks = slice(k * PEER_NKEYS + j0, k * PEER_NKEYS + j0 + sub)
                w = jnp.zeros((sub, LANES), F32)
                for hd in range(PEER_HEADS):
                    chosen = s2_ref[hd, tg, js, :] >= th_ref[hd, tg, row:row + 1, :]
                    w = w + jnp.where(chosen, e2_ref[hd, tg, js, :], 0.0) * c1_ref[hd, tg, row:row + 1, :]
                s = st_s[tg, ks, :]
                act = 0.5 * s * (1.0 + lax.erf(s * np.float32(2.0 ** -0.5)))
                act_s[tg, ks, :] = (act * w).astype(BF16)

    @pl.when(e == 0)
    def _():
        acc_s[...] = jnp.zeros_like(acc_s)
        a1_s[...] = jnp.zeros_like(a1_s)
        scores(u0_ref, st0_s, 0)
        scores(u0_ref, st0_s, 1)

    def phase(c, st_cur, act_cur, u_next, st_next, vt_prev, act_prev):
        for h in range(2):
            scores(u_next, st_next, h)
            gates(c, 2 * h, st_cur, act_cur)
            values(vt_prev, act_prev, h)
            gates(c, 2 * h + 1, st_cur, act_cur)

    phase(0, st0_s, a0_s, ub_ref, st1_s, vp_ref, a1_s)
    phase(1, st1_s, a1_s, un_ref, st0_s, va_ref, a0_s)

    @pl.when(e == pl.num_programs(1) - 1)
    def _():
        values(vl_ref, a1_s, 0)
        values(vl_ref, a1_s, 1)
        y = x_ref[...] + g2_ref[0] * acc_s[...].T
        if final_norm:
            y = _rms(y) * fg_ref[...]
        o_ref[...] = y


def _experts(hb, theta, s2, e2, c1, u, vt, x, mod, fg, *, mod_base, rows_per_cond, final_norm):
    n = x.shape[0]
    tb = PEER_TB
    ntg = tb // LANES
    ne = PEER_N // PEER_EC
    once = dict(pipeline_mode=pl.Buffered(1))
    rspec = pl.BlockSpec((PEER_HEADS, ntg, PEER_NKEYS, LANES), lambda i, e: (0, i, 0, 0), **once)
    rows8 = pl.BlockSpec((PEER_HEADS, ntg, SUBLANES, LANES), lambda i, e: (0, i, e, 0))
    uspec = lambda index: pl.BlockSpec((PEER_SUB, D_MODEL), index)
    vspec = lambda index: pl.BlockSpec((D_MODEL, PEER_SUB), index)
    return pl.pallas_call(
        functools.partial(_experts_kernel, final_norm=final_norm),
        grid=(n // tb, ne),
        in_specs=[pl.BlockSpec((tb, D_MODEL), lambda i, e: (i, 0), **once), rows8, rspec, rspec, rows8,
                  pl.BlockSpec((PEER_SUB, D_MODEL), lambda i, e: (0, 0), **once),
                  uspec(lambda i, e: (2 * e + 1, 0)),
                  uspec(lambda i, e: (jnp.minimum(2 * e + 2, 2 * ne - 2), 0)),
                  vspec(lambda i, e: (0, jnp.maximum(2 * e - 1, 0))),
                  vspec(lambda i, e: (0, 2 * e)),
                  pl.BlockSpec((D_MODEL, PEER_SUB), lambda i, e: (0, 2 * ne - 1), **once),
                  pl.BlockSpec((tb, D_MODEL), lambda i, e: (i, 0), **once),
                  _mod_spec(5, tb, mod_base, rows_per_cond), _full((1, D_MODEL))],
        out_specs=pl.BlockSpec((tb, D_MODEL), lambda i, e: (i, 0)),
        out_shape=jax.ShapeDtypeStruct((n, D_MODEL), F32),
        scratch_shapes=[pltpu.VMEM((D_MODEL, tb), F32), pltpu.VMEM((ntg, PEER_SUB, LANES), F32),
                        pltpu.VMEM((ntg, PEER_SUB, LANES), F32), pltpu.VMEM((ntg, PEER_SUB, LANES), BF16),
                        pltpu.VMEM((ntg, PEER_SUB, LANES), BF16)],
        compiler_params=_cp(2), name="peer_experts",
    )(hb, theta, s2, e2, c1, u, u, u, vt, vt, vt, x, mod, fg.reshape(1, D_MODEL))


def _peer(x, g, mod, wqt, keys, u, vt, fg, *, mod_base, rows_per_cond, final_norm):
    hb, theta, s2, e2, c1 = _route(x, g, mod, wqt, keys, mod_base=mod_base, rows_per_cond=rows_per_cond)
    return _experts(hb, theta, s2, e2, c1, u, vt, x, mod, fg, mod_base=mod_base, rows_per_cond=rows_per_cond,
                    final_norm=final_norm)


def _chunk_index(s, nc):
    return jnp.where(s < nc, s, 2 * nc - 1 - s)


def _scan_group(nseq, largest):
    return max(g for g in (2, 4, 8) if g <= largest and nseq % g == 0)


def _scan_specs(nc):
    def cur(col):
        return lambda b, s: (b, _chunk_index(s, nc), col)

    def prev(col):
        return lambda b, s: (b, jnp.maximum(_chunk_index(s, nc) * 8 - 1, 0), col)

    def nxt(col):
        return lambda b, s: (b, jnp.minimum(_chunk_index(s, nc) * 8 + 8, nc * 8 - 1), col)

    def out(col):
        return lambda b, s: (b, jnp.where(s < nc, nc - 1, 2 * nc - 1 - s), col)
    return cur, prev, nxt, out


def _conv_silu(prev, cur, nxt, w_ref, b_ref, first, last):
    win = jnp.concatenate([jnp.where(first, 0.0, prev), cur, jnp.where(last, 0.0, nxt)], axis=1)
    acc = b_ref[...] + win[:, 6:70, :] * w_ref[0:1, :]
    for k in range(1, CONV_W):
        acc = acc + win[:, 6 + k:70 + k, :] * w_ref[k:k + 1, :]
    return _silu(acc)


def _tri(bwd, G):
    r = lax.broadcasted_iota(jnp.int32, (G, CHUNK, CHUNK), 1)
    c = lax.broadcasted_iota(jnp.int32, (G, CHUNK, CHUNK), 2)
    lower = (c <= r).astype(F32)
    upper = (c >= r).astype(F32)
    return jnp.where(bwd, upper, lower), jnp.where(bwd, lower, upper)


def _bdot(a, b, contract, precision=None):
    return lax.dot_general(a, b, (((contract[0],), (contract[1],)), ((0,), (0,))), preferred_element_type=F32,
                           precision=precision)


def _ssd_kernel(z_ref, xp_ref, xc_ref, xn_ref, bp_ref, bc_ref, bn_ref, sm_ref, dtt_ref,
                cwx_ref, cbx_ref, cwb_ref, cbb_ref, arow_ref, acol_ref, brow_ref, bcol_ref, drow_ref, ng_ref,
                st0_ref, y_ref, st_ref, yf_s, h_s, *, nc):
    s = pl.program_id(1)
    bwd = s >= nc
    cidx = _chunk_index(s, nc)
    first = cidx == 0
    last = cidx == nc - 1
    nh = SSD_HEADS

    @pl.when(s == 0)
    def _():
        h_s[...] = st0_ref[:, 0]

    @pl.when(s == nc)
    def _():
        st_ref[:, 0] = h_s[...]
        h_s[...] = st0_ref[:, 1]

    G = z_ref.shape[0]
    tsel, tsel_t = _tri(bwd, G)
    mask = tsel > 0.0
    r0 = pl.multiple_of(cidx * CHUNK, CHUNK)
    hi = lax.Precision.HIGHEST

    xs = _conv_silu(xp_ref[...], xc_ref[...], xn_ref[...], cwx_ref, cbx_ref, first, last)
    bcv = _conv_silu(bp_ref[...], bc_ref[...], bn_ref[...], cwb_ref, cbb_ref, first, last)
    dt_all = _softplus(sm_ref[:, :, 64:96] + brow_ref[...])
    dtt_all = _softplus(dtt_ref[:, 0] + bcol_ref[...])
    a_all = dt_all * (-jnp.exp(arow_ref[...]))
    at_all = dtt_all * (-jnp.exp(acol_ref[...]))
    dt = jnp.where(bwd, dt_all[:, :, nh:2 * nh], dt_all[:, :, 0:nh])
    a = jnp.where(bwd, a_all[:, :, nh:2 * nh], a_all[:, :, 0:nh])
    at = jnp.where(bwd, at_all[:, nh:2 * nh, :], at_all[:, 0:nh, :])
    cum = _bdot(tsel, a, (2, 1), hi)
    cum_t = _bdot(at, tsel_t, (2, 1), hi)
    tot = jnp.where(bwd, cum[:, 0:1, :], cum[:, CHUNK - 1:CHUNK, :])

    ys = []
    for g in range(2):
        bg = bcv[:, :, g * SSD_STATE:(g + 1) * SSD_STATE]
        cg = bcv[:, :, 256 + g * SSD_STATE:256 + (g + 1) * SSD_STATE]
        cg16 = cg.astype(BF16)
        cb = _bdot(cg16, bg.astype(BF16), (2, 2))
        for hh in range(nh // 2):
            h = g * (nh // 2) + hh
            cq = cum[:, :, h:h + 1]
            decay = jnp.exp(jnp.where(mask, cq - cum_t[:, h:h + 1, :], NEG_INF))
            xd = xs[:, :, h * SSD_HEAD_DIM:(h + 1) * SSD_HEAD_DIM] * dt[:, :, h:h + 1]
            xd16 = xd.astype(BF16)
            hprev = h_s[:, h]
            y = (_bdot((cb * decay).astype(BF16), xd16, (2, 1))
                 + _bdot(cg16, hprev.astype(BF16), (2, 2)) * jnp.exp(cq))
            ys.append(y)
            th = tot[:, :, h:h + 1]
            bw = bg * jnp.exp(th - cq)
            h_s[:, h] = hprev * jnp.exp(th) + _bdot(xd16, bw.astype(BF16), (1, 1))
    y = jnp.concatenate(ys, axis=2)

    @pl.when(jnp.logical_not(bwd))
    def _():
        yf_s[:, pl.ds(r0, CHUNK), :] = y

    @pl.when(bwd)
    def _():
        yt = (yf_s[:, pl.ds(r0, CHUNK), :] + y + drow_ref[...] * xs) * _silu(z_ref[...])
        y_ref[...] = _rms(yt) * ng_ref[...]

    @pl.when(s == 2 * nc - 1)
    def _():
        st_ref[:, 1] = h_s[...]


def _ssd(p0, dtt, cwx, cbx, cwb, cbb, a_log, dt_bias, d_skip, norm_g, st0, *, L):
    nseq, nc = p0.shape[0], L // CHUNK
    G = _scan_group(nseq, 4)
    cur, prev, nxt, out = _scan_specs(nc)
    st_shape = (G, 2, SSD_HEADS, SSD_HEAD_DIM, SSD_STATE)
    per_seq = st0 is not None
    if st0 is None:
        st0 = jnp.zeros(st_shape, F32)
    row = lambda v: v.reshape(1, -1)
    col = lambda v: v.reshape(-1, 1)
    small = [cwx, row(cbx), cwb, row(cbb), row(a_log), col(a_log), row(dt_bias), col(dt_bias),
             row(jnp.repeat(d_skip, SSD_HEAD_DIM)), row(norm_g)]
    blk = lambda rows, width, index: pl.BlockSpec((G, rows, width), index)
    return pl.pallas_call(
        functools.partial(_ssd_kernel, nc=nc),
        grid=(nseq // G, 2 * nc),
        in_specs=[blk(CHUNK, 1024, cur(0)),
                  blk(8, 1024, prev(1)), blk(CHUNK, 1024, cur(1)), blk(8, 1024, nxt(1)),
                  blk(8, 512, prev(4)), blk(CHUNK, 512, cur(4)), blk(8, 512, nxt(4)),
                  blk(CHUNK, LANES, cur(28)),
                  pl.BlockSpec((G, 1, 32, CHUNK), lambda b, s: (b, _chunk_index(s, nc), 0, 0))]
        + [_full(v.shape) for v in small]
        + [pl.BlockSpec(st_shape, lambda b, s: (b if per_seq else 0, 0, 0, 0, 0))],
        out_specs=[blk(CHUNK, SSD_D_INNER, out(0)),
                   pl.BlockSpec(st_shape, lambda b, s: (b, 0, 0, 0, 0))],
        out_shape=[jax.ShapeDtypeStruct((nseq, L, SSD_D_INNER), F32),
                   jax.ShapeDtypeStruct((nseq,) + st_shape[1:], F32)],
        scratch_shapes=[pltpu.VMEM((G, L, SSD_D_INNER), F32), pltpu.VMEM(st_shape[:1] + st_shape[2:], F32)],
        compiler_params=_cp(2), name="ssd_mixer",
    )(p0, p0, p0, p0, p0, p0, p0, p0, dtt, *small, st0)


def _rope_tables(L):
    t = jnp.arange(L)
    r = (t // GRID_W).astype(F32)
    c = (t % GRID_W).astype(F32)
    nf = 16
    inv = ROPE_BASE ** (-jnp.arange(nf, dtype=F32) / nf)
    ang = jnp.concatenate([r[:, None] * inv, c[:, None] * inv], axis=-1)
    ang = jnp.concatenate([ang, ang], axis=-1)
    return jnp.cos(ang), jnp.sin(ang)


def _rope128(x, cos, sin):
    lane = lax.broadcasted_iota(jnp.int32, x.shape, 1)
    rot = jnp.where(lane % 64 < 32, -pltpu.roll(x, 96, 1), pltpu.roll(x, 32, 1))
    return x * cos + rot * sin


def _pad_lanes(x):
    return jnp.concatenate([x, jnp.zeros_like(x)], axis=1)


MLA_QB = 256


def _mla_kernel(*refs, L, Lc, rope, emit):
    refs = list(refs)
    qa_ref, kva_ref, sm_ref, qg_ref, kvg_ref, wqn_ref, wqr_ref, wkk_ref, wkv_ref = refs[:9]
    pos = 9
    if Lc:
        cckv_ref, ckpe_ref = refs[pos:pos + 2]
        pos += 2
    if rope:
        cos_ref, sin_ref = refs[pos:pos + 2]
        pos += 2
    y_ref = refs[pos]
    pos += 1
    if emit:
        ckv_ref = refs[pos]
        pos += 1
    qn_s, qr_s, kn_s, v_s, kpe_s = refs[pos:]
    qb = pl.program_id(1)
    scale = np.float32((MLA_NOPE + MLA_ROPE) ** -0.5)

    @pl.when(qb == 0)
    def _():
        ckv = _rms(kva_ref[...]) * kvg_ref[...]
        if emit:
            ckv_ref[...] = ckv
        kpe = _pad_lanes(sm_ref[:, 0:MLA_ROPE])
        if rope:
            kpe = _rope128(kpe, cos_ref[...], sin_ref[...])
        if Lc:
            ckv = jnp.concatenate([cckv_ref[0], ckv], axis=0)
            kpe = jnp.concatenate([_pad_lanes(ckpe_ref[0]), kpe], axis=0)
        c16 = ckv.astype(BF16)
        kn_s[...] = _dot(c16, wkk_ref[...]).astype(BF16)
        v_s[...] = _dot(c16, wkv_ref[...]).astype(BF16)
        kpe_s[...] = kpe.astype(BF16)

    qn = (_rms(qa_ref[...]) * qg_ref[...]).astype(BF16)
    qn_s[...] = _dot(qn, wqn_ref[...]).astype(BF16)
    qr = _dot(qn, wqr_ref[...])
    if rope:
        q0 = pl.multiple_of(qb * MLA_QB, MLA_QB)
        cos = cos_ref[pl.ds(q0, MLA_QB), :]
        sin = sin_ref[pl.ds(q0, MLA_QB), :]
        for h in range(MLA_HEADS):
            qr_s[:, h * LANES:(h + 1) * LANES] = _rope128(qr[:, h * LANES:(h + 1) * LANES], cos, sin).astype(BF16)
    else:
        qr_s[...] = qr.astype(BF16)

    def head(h, _):
        c0 = pl.multiple_of(h * LANES, LANES)
        s = (_dot_nt(qn_s[:, pl.ds(c0, LANES)], kn_s[:, pl.ds(c0, LANES)])
             + _dot_nt(qr_s[:, pl.ds(c0, LANES)], kpe_s[...])) * scale
        p = jnp.exp(s - jnp.max(s, axis=-1, keepdims=True))
        p = p / jnp.sum(p, axis=-1, keepdims=True)
        y_ref[:, pl.ds(c0, LANES)] = _dot(p.astype(BF16), v_s[:, pl.ds(c0, LANES)])
        return 0

    lax.fori_loop(0, MLA_HEADS, head, 0)


def _mla(p0, qg, kvg, wqn, wqr, wkk, wkv, cache_ckv, cache_kpe, rope_tabs, *, L):
    n = p0.shape[0]
    nseq, nqb = n // L, L // MLA_QB
    Lc = 0 if cache_ckv is None else cache_ckv.shape[1]
    rope = rope_tabs is not None
    emit = cache_ckv is None
    Lk = L + Lc
    args = [p0, p0, p0, qg.reshape(1, -1), kvg.reshape(1, -1), wqn, wqr, wkk, wkv]
    specs = [pl.BlockSpec((MLA_QB, MLA_RANK), lambda b, q: (b * nqb + q, 5)),
             pl.BlockSpec((L, MLA_RANK), lambda b, q: (b, 6)),
             pl.BlockSpec((L, LANES), lambda b, q: (b, 28)),
             _full((1, MLA_RANK)), _full((1, MLA_RANK)), _full(wqn.shape), _full(wqr.shape), _full(wkk.shape),
             _full(wkv.shape)]
    if Lc:
        args += [cache_ckv, cache_kpe]
        specs += [pl.BlockSpec((1, Lc, MLA_RANK), lambda b, q: (b, 0, 0)),
                  pl.BlockSpec((1, Lc, MLA_ROPE), lambda b, q: (b, 0, 0))]
    if rope:
        args += list(rope_tabs)
        specs += [_full((L, LANES)), _full((L, LANES))]
    out_specs = [pl.BlockSpec((MLA_QB, 1024), lambda b, q: (b * nqb + q, 0))]
    out_shape = [jax.ShapeDtypeStruct((n, 1024), F32)]
    if emit:
        out_specs.append(pl.BlockSpec((L, MLA_RANK), lambda b, q: (b, 0)))
        out_shape.append(jax.ShapeDtypeStruct((n, MLA_RANK), F32))
    return pl.pallas_call(
        functools.partial(_mla_kernel, L=L, Lc=Lc, rope=rope, emit=emit),
        grid=(nseq, nqb), in_specs=specs, out_specs=out_specs, out_shape=out_shape,
        scratch_shapes=[pltpu.VMEM((MLA_QB, 1024), BF16), pltpu.VMEM((MLA_QB, 1024), BF16),
                        pltpu.VMEM((Lk, 1024), BF16), pltpu.VMEM((Lk, 1024), BF16), pltpu.VMEM((Lk, LANES), BF16)],
        compiler_params=_cp(2), name="mla_attention",
    )(*args)


def _gqa_ctx_kernel(q_ref, k_ref, v_ref, sink_ref, y_ref):
    q = q_ref[...]
    k = k_ref[...].astype(BF16)
    v = v_ref[...].astype(BF16)
    scale = np.float32(GQA_HEAD_DIM ** -0.5)
    group = GQA_HEADS // GQA_KV_HEADS
    outs = []
    for h in range(GQA_HEADS):
        kh = h // group
        ks = slice(kh * GQA_HEAD_DIM, (kh + 1) * GQA_HEAD_DIM)
        s = _dot_nt(q[:, h * GQA_HEAD_DIM:(h + 1) * GQA_HEAD_DIM].astype(BF16), k[:, ks]) * scale
        m = jnp.maximum(jnp.max(s, axis=-1, keepdims=True), sink_ref[h])
        p = jnp.exp(s - m)
        p = p / (jnp.sum(p, axis=-1, keepdims=True) + jnp.exp(sink_ref[h] - m))
        outs.append(_dot(p.astype(BF16), v[:, ks]))
    y_ref[...] = jnp.concatenate(outs, axis=1)


def _gqa_ctx(p1, sink, *, L):
    n = p1.shape[0]
    return pl.pallas_call(
        _gqa_ctx_kernel,
        grid=(n // L,),
        in_specs=[pl.BlockSpec((L, 1024), lambda b: (b, 2)), pl.BlockSpec((L, 256), lambda b: (b, 20)),
                  pl.BlockSpec((L, 256), lambda b: (b, 21)), pl.BlockSpec(memory_space=pltpu.SMEM)],
        out_specs=pl.BlockSpec((L, 1024), lambda b: (b, 0)),
        out_shape=jax.ShapeDtypeStruct((n, 1024), F32),
        compiler_params=_cp(1), name="gqa_context",
    )(p1, p1, p1, sink)


GQA_QB = 128
GQA_SPAN = GQA_QB + 2 * WINDOW


def _gqa_win_kernel(q_ref, k_ref, v_ref, ck_ref, cv_ref, cos_ref, sin_ref, sink_ref, y_ref, kp_s, vp_s, *, L):
    qb = pl.program_id(1)
    scale = np.float32(GQA_HEAD_DIM ** -0.5)
    group = GQA_HEADS // GQA_KV_HEADS

    @pl.when(qb == 0)
    def _():
        zeros = jnp.zeros((WINDOW, 256), BF16)
        kp_s[0:WINDOW, :] = zeros
        kp_s[WINDOW + L:, :] = zeros
        vp_s[0:WINDOW, :] = zeros
        vp_s[WINDOW + L:, :] = zeros
        for j in range(2):
            cs = slice(j * LANES, (j + 1) * LANES)
            kp_s[WINDOW:WINDOW + L, cs] = _rope128(k_ref[:, cs], cos_ref[...], sin_ref[...]).astype(BF16)
        vp_s[WINDOW:WINDOW + L, :] = v_ref[...].astype(BF16)

    q0 = pl.multiple_of(qb * GQA_QB, GQA_QB)
    cos = cos_ref[pl.ds(q0, GQA_QB), :]
    sin = sin_ref[pl.ds(q0, GQA_QB), :]
    kw = kp_s[pl.ds(q0, GQA_SPAN), :]
    vw = vp_s[pl.ds(q0, GQA_SPAN), :]
    kc = ck_ref[0].astype(BF16)
    vc = cv_ref[0].astype(BF16)
    qpos = q0 + lax.broadcasted_iota(jnp.int32, (GQA_QB, GQA_SPAN), 0)
    kpos = q0 - WINDOW + lax.broadcasted_iota(jnp.int32, (GQA_QB, GQA_SPAN), 1)
    bias = jnp.where(kpos < 0, NEG_INF, jnp.where(kpos >= L, NEG_INF, jnp.where(jnp.abs(qpos - kpos) <= WINDOW, 0.0, NEG_INF)))
    outs = []
    for j in range(GQA_HEADS * GQA_HEAD_DIM // LANES):
        qj = _rope128(q_ref[:, j * LANES:(j + 1) * LANES], cos, sin).astype(BF16)
        for half in range(2):
            h = 2 * j + half
            kh = h // group
            ks = slice(kh * GQA_HEAD_DIM, (kh + 1) * GQA_HEAD_DIM)
            qh = qj[:, half * GQA_HEAD_DIM:(half + 1) * GQA_HEAD_DIM]
            s_loc = _dot_nt(qh, kw[:, ks]) * scale + bias
            s_ctx = _dot_nt(qh, kc[:, ks]) * scale
            m = jnp.maximum(jnp.maximum(jnp.max(s_loc, axis=-1, keepdims=True), jnp.max(s_ctx, axis=-1, keepdims=True)),
                            sink_ref[h])
            p_loc = jnp.exp(s_loc - m)
            p_ctx = jnp.exp(s_ctx - m)
            inv = 1.0 / (jnp.sum(p_loc, axis=-1, keepdims=True) + jnp.sum(p_ctx, axis=-1, keepdims=True)
                         + jnp.exp(sink_ref[h] - m))
            outs.append(_dot((p_loc * inv).astype(BF16), vw[:, ks]) + _dot((p_ctx * inv).astype(BF16), vc[:, ks]))
    y_ref[...] = jnp.concatenate(outs, axis=1)


def _gqa_win(p1, cache_k, cache_v, cos, sin, sink, *, L):
    n = p1.shape[0]
    nseq, nqb = n // L, L // GQA_QB
    Lc = cache_k.shape[1]
    return pl.pallas_call(
        functools.partial(_gqa_win_kernel, L=L),
        grid=(nseq, nqb),
        in_specs=[pl.BlockSpec((GQA_QB, 1024), lambda b, q: (b * nqb + q, 2)),
                  pl.BlockSpec((L, 256), lambda b, q: (b, 20)), pl.BlockSpec((L, 256), lambda b, q: (b, 21)),
                  pl.BlockSpec((1, Lc, 256), lambda b, q: (b, 0, 0)), pl.BlockSpec((1, Lc, 256), lambda b, q: (b, 0, 0)),
                  _full((L, LANES)), _full((L, LANES)), pl.BlockSpec(memory_space=pltpu.SMEM)],
        out_specs=pl.BlockSpec((GQA_QB, 1024), lambda b, q: (b * nqb + q, 0)),
        out_shape=jax.ShapeDtypeStruct((n, 1024), F32),
        scratch_shapes=[pltpu.VMEM((L + 2 * WINDOW, 256), BF16), pltpu.VMEM((L + 2 * WINDOW, 256), BF16)],
        compiler_params=_cp(2), name="gqa_window",
    )(p1, p1, p1, cache_k, cache_v, cos, sin, sink)


def _mlstm_kernel(qp_ref, qc_ref, qn_ref, v_ref, o_ref, sm_ref, gt_ref, cw_ref, cb_ref, brow_ref, bcol_ref, ng_ref,
                  c0_ref, n0_ref, m0_ref, y_ref, cst_ref, nst_ref, mst_ref, hf_s, c_s, n_s, m_s, *, nc):
    s = pl.program_id(1)
    bwd = s >= nc
    cidx = _chunk_index(s, nc)
    nh = ML_HEADS
    hd = ML_HEAD_DIM

    @pl.when(s == 0)
    def _():
        c_s[...] = c0_ref[:, 0]
        n_s[...] = n0_ref[:, 0]
        m_s[...] = m0_ref[:, 0]

    @pl.when(s == nc)
    def _():
        cst_ref[:, 0] = c_s[...]
        nst_ref[:, 0] = n_s[...]
        mst_ref[:, 0] = m_s[...]
        c_s[...] = c0_ref[:, 1]
        n_s[...] = n0_ref[:, 1]
        m_s[...] = m0_ref[:, 1]

    G = qc_ref.shape[0]
    tsel, tsel_t = _tri(bwd, G)
    mask = tsel > 0.0
    kscale = np.float32(hd ** -0.5)
    r0 = pl.multiple_of(cidx * CHUNK, CHUNK)
    hi = lax.Precision.HIGHEST

    qk = _conv_silu(qp_ref[...], qc_ref[...], qn_ref[...], cw_ref, cb_ref, cidx == 0, cidx == nc - 1)
    gates = sm_ref[:, :, 0:4 * nh] + brow_ref[...]
    gates_t = gt_ref[:, 0] + bcol_ref[...]
    li = jnp.where(bwd, gates[:, :, nh:2 * nh], gates[:, :, 0:nh])
    li_t = jnp.where(bwd, gates_t[:, nh:2 * nh, :], gates_t[:, 0:nh, :])
    lf = -_softplus(-jnp.where(bwd, gates[:, :, 3 * nh:4 * nh], gates[:, :, 2 * nh:3 * nh]))
    lf_t = -_softplus(-jnp.where(bwd, gates_t[:, 3 * nh:4 * nh, :], gates_t[:, 2 * nh:3 * nh, :]))
    bc = _bdot(tsel, lf, (2, 1), hi)
    bc_t = _bdot(lf_t, tsel_t, (2, 1), hi)
    tot = jnp.where(bwd, bc[:, 0:1, :], bc[:, CHUNK - 1:CHUNK, :])

    hs = []
    for h in range(nh):
        cs = slice(h * hd, (h + 1) * hd)
        q = qk[:, :, cs]
        k = qk[:, :, ML_D + h * hd:ML_D + (h + 1) * hd] * kscale
        q16, k16 = q.astype(BF16), k.astype(BF16)
        v = v_ref[:, :, cs]
        bq = bc[:, :, h:h + 1]
        m_prev = m_s[:, h:h + 1, 0:1]
        log_d = jnp.where(mask, bq - bc_t[:, h:h + 1, :] + li_t[:, h:h + 1, :], NEG_INF)
        log_inter = bq + m_prev
        m_out = jnp.maximum(log_inter, jnp.max(log_d, axis=2, keepdims=True))
        sd = _bdot(q16, k16, (2, 2)) * jnp.exp(log_d - m_out)
        w_inter = jnp.exp(log_inter - m_out)
        c_prev = c_s[:, h]
        n_prev = n_s[:, h:h + 1, :]
        num = _bdot(sd.astype(BF16), v.astype(BF16), (2, 1)) + w_inter * _bdot(q16, c_prev.astype(BF16), (2, 2))
        den = jnp.sum(sd, axis=2, keepdims=True) + w_inter * jnp.sum(q * n_prev, axis=2, keepdims=True)
        hs.append(num / jnp.maximum(jnp.abs(den), jnp.exp(-m_out)))
        th = tot[:, :, h:h + 1]
        end_inter = th + m_prev
        end_intra = th - bq + li[:, :, h:h + 1]
        m_new = jnp.maximum(end_inter, jnp.max(end_intra, axis=1, keepdims=True))
        w_c = jnp.exp(end_inter - m_new)
        w_k = jnp.exp(end_intra - m_new)
        c_s[:, h] = w_c * c_prev + _bdot((v * w_k).astype(BF16), k16, (1, 1))
        n_s[:, h:h + 1, :] = w_c * n_prev + jnp.sum(k * w_k, axis=1, keepdims=True)
        m_s[:, h:h + 1, :] = jnp.broadcast_to(m_new, (G, 1, hd))
    hcat = jnp.concatenate(hs, axis=2)

    @pl.when(jnp.logical_not(bwd))
    def _():
        hf_s[:, pl.ds(r0, CHUNK), :] = hcat

    @pl.when(bwd)
    def _():
        tot_h = hf_s[:, pl.ds(r0, CHUNK), :] + hcat
        normed = jnp.concatenate([_rms(tot_h[:, :, h * hd:(h + 1) * hd]) for h in range(nh)], axis=2) * ng_ref[...]
        y_ref[...] = normed * jax.nn.sigmoid(o_ref[...])

    @pl.when(s == 2 * nc - 1)
    def _():
        cst_ref[:, 1] = c_s[...]
        nst_ref[:, 1] = n_s[...]
        mst_ref[:, 1] = m_s[...]


def _mlstm(p1, gt, conv_w, conv_b, ig_b, fg_b, norm_g, c0, n0, m0, *, L):
    nseq, nc = p1.shape[0], L // CHUNK
    G = _scan_group(nseq, 4)
    cur, prev, nxt, out = _scan_specs(nc)
    cshape = (G, 2, ML_HEADS, ML_HEAD_DIM, ML_HEAD_DIM)
    nshape = (G, 2, ML_HEADS, ML_HEAD_DIM)
    per_seq = c0 is not None
    if c0 is None:
        c0, n0, m0 = jnp.zeros(cshape, F32), jnp.zeros(nshape, F32), jnp.zeros(nshape, F32)
    gb = jnp.concatenate([ig_b.reshape(-1), fg_b.reshape(-1)])
    small = [conv_w, conv_b.reshape(1, -1), gb.reshape(1, -1), gb.reshape(-1, 1), norm_g.reshape(1, -1)]
    st_in = lambda shape: pl.BlockSpec(shape, lambda b, s: (b if per_seq else 0,) + (0,) * (len(shape) - 1))
    st_out = lambda shape: pl.BlockSpec(shape, lambda b, s: (b,) + (0,) * (len(shape) - 1))
    blk = lambda rows, width, index: pl.BlockSpec((G, rows, width), index)
    return pl.pallas_call(
        functools.partial(_mlstm_kernel, nc=nc),
        grid=(nseq // G, 2 * nc),
        in_specs=[blk(8, 2048, prev(0)), blk(CHUNK, 2048, cur(0)), blk(8, 2048, nxt(0)),
                  blk(CHUNK, 1024, cur(3)), blk(CHUNK, 1024, cur(4)), blk(CHUNK, LANES, cur(44)),
                  pl.BlockSpec((G, 1, 32, CHUNK), lambda b, s: (b, _chunk_index(s, nc), 0, 0))]
        + [_full(v.shape) for v in small] + [st_in(cshape), st_in(nshape), st_in(nshape)],
        out_specs=[blk(CHUNK, ML_D, out(0)), st_out(cshape), st_out(nshape), st_out(nshape)],
        out_shape=[jax.ShapeDtypeStruct((nseq, L, ML_D), F32), jax.ShapeDtypeStruct((nseq,) + cshape[1:], F32),
                   jax.ShapeDtypeStruct((nseq,) + nshape[1:], F32), jax.ShapeDtypeStruct((nseq,) + nshape[1:], F32)],
        scratch_shapes=[pltpu.VMEM((G, L, ML_D), F32), pltpu.VMEM(cshape[:1] + cshape[2:], F32),
                        pltpu.VMEM(nshape[:1] + nshape[2:], F32), pltpu.VMEM(nshape[:1] + nshape[2:], F32)],
        compiler_params=_cp(2), name="mlstm_mixer",
    )(p1, p1, p1, p1, p1, p1, gt, *small, c0, n0, m0)


def _prep_layer0(w_in, w_qb, w_kvb, w_out):
    pad = jnp.zeros((D_MODEL, L0_COLS - 3680), F32)
    w = jnp.concatenate([w_in[:, :2560], w_in[:, 2592:3680], w_in[:, 2560:2592], pad], axis=1).astype(BF16)
    q3 = w_qb.reshape(MLA_RANK, MLA_HEADS, MLA_NOPE + MLA_ROPE)
    wqn = q3[:, :, :MLA_NOPE].reshape(MLA_RANK, -1).astype(BF16)
    wqr = jnp.concatenate([q3[:, :, MLA_NOPE:], jnp.zeros((MLA_RANK, MLA_HEADS, LANES - MLA_ROPE), F32)], axis=-1)
    wqr = wqr.reshape(MLA_RANK, -1).astype(BF16)
    k3 = w_kvb.reshape(MLA_RANK, MLA_HEADS, MLA_NOPE + MLA_V)
    wkk = k3[:, :, :MLA_NOPE].reshape(MLA_RANK, -1).astype(BF16)
    wkv = k3[:, :, MLA_NOPE:].reshape(MLA_RANK, -1).astype(BF16)
    return w, wqn, wqr, wkk, wkv, w_out.astype(BF16)


def _prep_layer1(w_in, w_out):
    pad = jnp.zeros((D_MODEL, L1_COLS - 5664), F32)
    w = jnp.concatenate([w_in[:, 1536:3584], w_in[:, 0:1024], w_in[:, 3584:5632], w_in[:, 1024:1536],
                         w_in[:, 5632:5664], pad], axis=1).astype(BF16)
    return w, w_out.astype(BF16)


def _prep_peer(wq, keys, u, v):
    return (wq.T.astype(BF16), keys.reshape(PEER_HEADS * 2, PEER_NKEYS, -1).astype(BF16), u.astype(BF16),
            v.T.astype(BF16))


def _chunk_transposed(cols, L):
    return cols.reshape(-1, L // CHUNK, CHUNK, cols.shape[1]).transpose(0, 1, 3, 2)


def _trunk(x, L, mod0, mod1, w0, w1, peer0, peer1, p, ctx, rope_tabs, mod_base, rows_per_cond, fg):
    kw = dict(mod_base=mod_base, rows_per_cond=rows_per_cond)
    w_in0, wqn, wqr, wkk, wkv, w_out0 = w0
    w_in1, w_out1 = w1
    nseq = x.shape[0] // L

    p0 = _proj(x, p['l0_norm1_g'], mod0, w_in0, tn=1280, **kw)
    cw, cb = p['l0_ssd_conv_w'], p['l0_ssd_conv_b']
    y_ssd, ssd_state = _ssd(p0.reshape(nseq, L, -1), _chunk_transposed(p0[:, 3648:3680], L), cw[:, :1024], cb[:1024],
                            cw[:, 1024:], cb[1024:], p['l0_ssd_A_log'], p['l0_ssd_dt_bias'], p['l0_ssd_D'],
                            p['l0_ssd_norm_g'], None if ctx is None else ctx[0], L=L)
    y_ssd = y_ssd.reshape(nseq * L, -1)
    mla_tabs = None
    if rope_tabs is not None:
        z64 = jnp.zeros_like(rope_tabs[0])
        mla_tabs = (jnp.concatenate([rope_tabs[0], z64], axis=1), jnp.concatenate([rope_tabs[1], z64], axis=1))
    mla_out = _mla(p0, p['l0_mla_q_norm_g'], p['l0_mla_kv_norm_g'], wqn, wqr, wkk, wkv,
                   None if ctx is None else ctx[1], None if ctx is None else ctx[2], mla_tabs, L=L)
    x = _outproj(y_ssd, mla_out[0], w_out0, x, mod0, **kw)
    x = _peer(x, p['l0_norm2_g'], mod0, *peer0, fg, final_norm=False, **kw)

    p1 = _proj(x, p['l1_norm1_g'], mod1, w_in1, tn=1152, **kw)
    if ctx is None:
        y_gqa = _gqa_ctx(p1, p['l1_gqa_sink'], L=L)
        c0 = n0 = m0 = None
    else:
        gqa_tabs = tuple(jnp.concatenate([t, t], axis=1) for t in rope_tabs)
        y_gqa = _gqa_win(p1, ctx[3].reshape(nseq, -1, 256), ctx[4].reshape(nseq, -1, 256), *gqa_tabs,
                         p['l1_gqa_sink'], L=L)
        c0, n0 = ctx[5], ctx[6]
        m0 = jnp.broadcast_to(ctx[7][..., None], ctx[7].shape + (ML_HEAD_DIM,))
    y_ml, mc, mn, mm = _mlstm(p1.reshape(nseq, L, -1), _chunk_transposed(p1[:, 5632:5664], L), p['l1_ml_conv_w'],
                              p['l1_ml_conv_b'], p['l1_ml_ig_b'], p['l1_ml_fg_b'], p['l1_ml_norm_g'], c0, n0, m0, L=L)
    y_ml = y_ml.reshape(nseq * L, -1)
    x = _outproj(y_gqa, y_ml, w_out1, x, mod1, **kw)
    y = _peer(x, p['l1_norm2_g'], mod1, *peer1, fg, final_norm=True, **kw)
    new = None
    if ctx is None:
        new = (ssd_state, mla_out[1].reshape(nseq, L, MLA_RANK), p0[:, 3584:3648].reshape(nseq, L, MLA_ROPE),
               p1[:, 5120:5376].reshape(nseq, L, GQA_KV_HEADS, GQA_HEAD_DIM),
               p1[:, 5376:5632].reshape(nseq, L, GQA_KV_HEADS, GQA_HEAD_DIM), mc, mn, mm[..., 0])
    return y, new


def kernel(x_prompt, x_sample, state_l0_ssd, cache_l0_mla_ckv, cache_l0_mla_kpe, cache_l1_gqa_k, cache_l1_gqa_v, state_l1_mlstm_C, state_l1_mlstm_n, state_l1_mlstm_m, c, c_ctx, final_norm_g, l0_ada_w, l0_ada_b, l0_norm1_g, l0_norm2_g, l0_w_in, l0_ssd_conv_w, l0_ssd_conv_b, l0_ssd_A_log, l0_ssd_dt_bias, l0_ssd_D, l0_ssd_norm_g, l0_mla_q_norm_g, l0_mla_w_qb, l0_mla_kv_norm_g, l0_mla_w_kvb, l0_w_out, l0_peer_wq, l0_peer_keys, l0_peer_u, l0_peer_v, l1_ada_w, l1_ada_b, l1_norm1_g, l1_norm2_g, l1_w_in, l1_gqa_sink, l1_ml_conv_w, l1_ml_conv_b, l1_ml_ig_b, l1_ml_fg_b, l1_ml_norm_g, l1_w_out, l1_peer_wq, l1_peer_keys, l1_peer_u, l1_peer_v):
    p = dict(l0_norm1_g=l0_norm1_g, l0_norm2_g=l0_norm2_g, l0_ssd_conv_w=l0_ssd_conv_w, l0_ssd_conv_b=l0_ssd_conv_b,
             l0_ssd_A_log=l0_ssd_A_log, l0_ssd_dt_bias=l0_ssd_dt_bias, l0_ssd_D=l0_ssd_D, l0_ssd_norm_g=l0_ssd_norm_g,
             l0_mla_q_norm_g=l0_mla_q_norm_g, l0_mla_kv_norm_g=l0_mla_kv_norm_g, l1_norm1_g=l1_norm1_g,
             l1_norm2_g=l1_norm2_g, l1_gqa_sink=l1_gqa_sink, l1_ml_conv_w=l1_ml_conv_w, l1_ml_conv_b=l1_ml_conv_b,
             l1_ml_ig_b=l1_ml_ig_b, l1_ml_fg_b=l1_ml_fg_b, l1_ml_norm_g=l1_ml_norm_g)
    nb, seq = x_prompt.shape[:2]
    db, dseq = x_sample.shape[:2]
    assert db <= 7 and seq % MLA_QB == 0 and dseq % MLA_QB == 0

    cond8 = jnp.zeros((8, D_MODEL), F32).at[0].set(c_ctx).at[1:1 + db].set(c)
    mod0 = _ada(cond8, l0_ada_w, l0_ada_b).reshape(8, 1, 6 * D_MODEL)
    mod1 = _ada(cond8, l1_ada_w, l1_ada_b).reshape(8, 1, 6 * D_MODEL)
    w0 = _prep_layer0(l0_w_in, l0_mla_w_qb, l0_mla_w_kvb, l0_w_out)
    w1 = _prep_layer1(l1_w_in, l1_w_out)
    peer0 = _prep_peer(l0_peer_wq, l0_peer_keys, l0_peer_u, l0_peer_v)
    peer1 = _prep_peer(l1_peer_wq, l1_peer_keys, l1_peer_u, l1_peer_v)

    y_prompt, new = _trunk(x_prompt.reshape(nb * seq, D_MODEL), seq, mod0, mod1, w0, w1, peer0, peer1, p, None, None,
                           0, nb * seq, final_norm_g)
    ctx = (state_l0_ssd, cache_l0_mla_ckv, cache_l0_mla_kpe, cache_l1_gqa_k, cache_l1_gqa_v, state_l1_mlstm_C,
           state_l1_mlstm_n, state_l1_mlstm_m)
    y_sample, _ = _trunk(x_sample.reshape(db * dseq, D_MODEL), dseq, mod0, mod1, w0, w1, peer0, peer1, p, ctx,
                         _rope_tables(dseq), 1, dseq, final_norm_g)
    return (y_prompt.reshape(nb, seq, D_MODEL), y_sample.reshape(db, dseq, D_MODEL)) + new
```

```python
import functools

import jax
import jax.numpy as jnp
import numpy as np
from jax import lax
from jax.experimental import pallas as pl
from jax.experimental.pallas import tpu as pltpu

F32 = jnp.float32
BF16 = jnp.bfloat16
NEG_INF = float("-inf")

D_MODEL = 2048
EPS = 1e-6
ROPE_BASE = 10000.0
GRID_W = 64
CONV_W = 5
CHUNK = 64
SSD_HEADS = 16
SSD_HEAD_DIM = 64
SSD_D_INNER = 1024
SSD_STATE = 128
MLA_HEADS = 8
MLA_NOPE = 128
MLA_ROPE = 64
MLA_V = 128
MLA_RANK = 512
GQA_HEADS = 16
GQA_KV_HEADS = 4
GQA_HEAD_DIM = 64
WINDOW = 128
ML_HEADS = 8
ML_HEAD_DIM = 128
ML_D = 1024
PEER_HEADS = 8
PEER_NKEYS = 128
PEER_N = PEER_NKEYS * PEER_NKEYS
PEER_TOPK = 16

LANES = 128
SUBLANES = 8
VMEM_LIMIT = 56 * 2**20

L0_COLS = 3840
L1_COLS = 5760


def _cp(n_grid, flags=None):
    return pltpu.CompilerParams(dimension_semantics=("arbitrary",) * n_grid, vmem_limit_bytes=VMEM_LIMIT, flags=flags)


def _silu(x):
    return x * jax.nn.sigmoid(x)


def _softplus(x):
    return jnp.maximum(x, 0.0) + jnp.log1p(jnp.exp(-jnp.abs(x)))


def _rms(x):
    return x * lax.rsqrt(jnp.mean(x * x, axis=-1, keepdims=True) + EPS)


def _dot(a, b):
    return jnp.dot(a, b, preferred_element_type=F32)


def _dot_nt(a, b):
    return lax.dot_general(a, b, (((1,), (1,)), ((), ())), preferred_element_type=F32)


def _dot_tn(a, b):
    return lax.dot_general(a, b, (((0,), (0,)), ((), ())), preferred_element_type=F32)


def _dot_f32(a, b):
    return jnp.dot(a, b, preferred_element_type=F32, precision=lax.Precision.HIGHEST)


def _full(shape):
    nd = len(shape)
    return pl.BlockSpec(shape, lambda *_: (0,) * nd)


def _mod_spec(chunk, tb, mod_base, rows_per_cond):
    def index(i, *_):
        return (mod_base + (i * tb) // rows_per_cond, 0, chunk)
    return pl.BlockSpec((1, 1, D_MODEL), index)


def _ada_kernel(c_ref, w_ref, b_ref, o_ref):
    s = _silu(c_ref[...])
    o_ref[...] = _dot(s.astype(BF16), w_ref[...].astype(BF16)) + b_ref[...]


def _ada(cond8, w, b):
    n = w.shape[1]
    tn = 1024
    return pl.pallas_call(
        _ada_kernel,
        grid=(n // tn,),
        in_specs=[_full((8, D_MODEL)), pl.BlockSpec((D_MODEL, tn), lambda j: (0, j)),
                  pl.BlockSpec((1, tn), lambda j: (0, j))],
        out_specs=pl.BlockSpec((8, tn), lambda j: (0, j)),
        out_shape=jax.ShapeDtypeStruct((8, n), F32),
        compiler_params=_cp(1), name="ada_table",
    )(cond8, w, b.reshape(1, n))


def _proj_kernel(x_ref, g_ref, sc_ref, sh_ref, w_ref, o_ref, h_ref):
    @pl.when(pl.program_id(1) == 0)
    def _():
        h = _rms(x_ref[...]) * g_ref[...] * (1.0 + sc_ref[0]) + sh_ref[0]
        h_ref[...] = h.astype(BF16)
    o_ref[...] = _dot(h_ref[...], w_ref[...])


def _proj(x, g, mod, w, *, tn, mod_base, rows_per_cond):
    n, ncol = x.shape[0], w.shape[1]
    tb = 1024
    return pl.pallas_call(
        _proj_kernel,
        grid=(n // tb, ncol // tn),
        in_specs=[pl.BlockSpec((tb, D_MODEL), lambda i, j: (i, 0)), _full((1, D_MODEL)),
                  _mod_spec(1, tb, mod_base, rows_per_cond), _mod_spec(0, tb, mod_base, rows_per_cond),
                  pl.BlockSpec((D_MODEL, tn), lambda i, j: (0, j))],
        out_specs=pl.BlockSpec((tb, tn), lambda i, j: (i, j)),
        out_shape=jax.ShapeDtypeStruct((n, ncol), F32),
        scratch_shapes=[pltpu.VMEM((tb, D_MODEL), BF16)],
        compiler_params=_cp(2), name="norm_mod_proj",
    )(x, g.reshape(1, D_MODEL), mod, mod, w)


def _outproj_kernel(ya_ref, yb_ref, w_ref, x_ref, g1_ref, o_ref):
    half = ya_ref.shape[1]
    y = _dot(ya_ref[...].astype(BF16), w_ref[0:half, :]) + _dot(yb_ref[...].astype(BF16), w_ref[half:, :])
    o_ref[...] = x_ref[...] + g1_ref[0] * y


def _outproj(ya, yb, w, x, mod, *, mod_base, rows_per_cond):
    n = x.shape[0]
    tb = 512
    return pl.pallas_call(
        _outproj_kernel,
        grid=(n // tb,),
        in_specs=[pl.BlockSpec((tb, ya.shape[1]), lambda i: (i, 0)), pl.BlockSpec((tb, yb.shape[1]), lambda i: (i, 0)),
                  _full(w.shape), pl.BlockSpec((tb, D_MODEL), lambda i: (i, 0)),
                  _mod_spec(2, tb, mod_base, rows_per_cond)],
        out_specs=pl.BlockSpec((tb, D_MODEL), lambda i: (i, 0)),
        out_shape=jax.ShapeDtypeStruct((n, D_MODEL), F32),
        compiler_params=_cp(1), name="out_proj_residual",
    )(ya, yb, w, x, mod)


ROUTE_TB = 512
_CAND_KEEP = (16, 8, 5, 4, 3, 2, 2, 2)


def _sort_pairs(n):
    pairs = []

    def merge(lo, hi, r):
        step = r * 2
        if step < hi - lo:
            merge(lo, hi, step)
            merge(lo + r, hi, step)
            pairs.extend((i, i + r) for i in range(lo + r, hi - r, step))
        else:
            pairs.append((lo, lo + r))

    def sort(lo, hi):
        if hi - lo >= 1:
            mid = lo + (hi - lo) // 2
            sort(lo, mid)
            sort(mid + 1, hi)
            merge(lo, hi, 1)

    sort(0, 15)
    return [(i, j) for i, j in pairs if j < n]


def _top_values(x, count):
    groups = [x[r:r + SUBLANES] for r in range(0, x.shape[0], SUBLANES)]
    n = len(groups)
    for i, j in _sort_pairs(n):
        groups[i], groups[j] = jnp.maximum(groups[i], groups[j]), jnp.minimum(groups[i], groups[j])
    sub = lax.broadcasted_iota(jnp.int32, groups[0].shape, 0)
    tops = []
    for t in range(count):
        head = groups[0]
        m = jnp.max(head, axis=0, keepdims=True)
        tops.append(m)
        if t == count - 1:
            break
        first = jnp.min(jnp.where(head == m, sub, SUBLANES), axis=0, keepdims=True)
        popped = sub == first
        depth = min(count - 1 - t, n)
        for r in range(depth):
            below = groups[r + 1] if r + 1 < n else NEG_INF
            groups[r] = jnp.where(popped, below, groups[r])
    return tops


def _route_kernel(x_ref, g_ref, sc_ref, sh_ref, wqt_ref, keys_ref,
                  hb_ref, th_ref, s2_ref, e2_ref, c1_ref, qt_s, top_s):
    tb = x_ref.shape[0]
    h = _rms(x_ref[...]) * g_ref[...] * (1.0 + sc_ref[0]) + sh_ref[0]
    hb = h.astype(BF16)
    hb_ref[...] = hb
    qt_s[...] = _dot_nt(wqt_ref[...], hb).astype(BF16)

    iota_8 = lax.broadcasted_iota(jnp.int32, (SUBLANES, tb), 0)

    def head(hd, _):
        scores = []
        for c in range(2):
            r0 = pl.multiple_of((hd * 2 + c) * PEER_NKEYS, PEER_NKEYS)
            s = _dot(keys_ref[hd * 2 + c], qt_s[pl.ds(r0, PEER_NKEYS), :])
            scores.append(s)
            for it, m in enumerate(_top_values(s, PEER_TOPK)):
                top_s[c, it:it + 1, :] = m
        a1 = top_s[0]
        a2 = top_s[1]
        pieces = [a1[0:1] + a2, a1[1:2] + a2[0:8]]
        for i in range(2, 8):
            pieces.append(jnp.where(iota_8 < _CAND_KEEP[i], a1[i:i + 1] + a2[0:8], NEG_INF))
        pieces.append(a1[8:16] + a2[0:1])
        cand = jnp.concatenate(pieces, axis=0)
        tau = _top_values(cand, PEER_TOPK)[-1]
        top = a1[0:1] + a2[0:1]
        z = jnp.sum(jnp.where(cand >= tau, jnp.exp(cand - top), 0.0), axis=0, keepdims=True)
        e2 = jnp.exp(scores[1] - a2[0:1])
        c1 = jnp.exp(scores[0] - a1[0:1]) / z
        theta = jnp.full(scores[0].shape, jnp.inf, F32)
        for q in range(PEER_TOPK):
            a2q = a2[q:q + 1]
            phi = jnp.min(jnp.where(a1 + a2q >= tau, a1, jnp.inf), axis=0, keepdims=True)
            theta = jnp.where(scores[0] >= phi, a2q, theta)
        for t in range(tb // LANES):
            ts = slice(t * LANES, (t + 1) * LANES)
            th_ref[hd, t] = theta[:, ts]
            s2_ref[hd, t] = scores[1][:, ts]
            e2_ref[hd, t] = e2[:, ts]
            c1_ref[hd, t] = c1[:, ts]
        return 0

    lax.fori_loop(0, PEER_HEADS, head, 0)


def _route(x, g, mod, wqt, keys, *, mod_base, rows_per_cond):
    n = x.shape[0]
    tb = ROUTE_TB
    ntg = tb // LANES
    hk = (PEER_HEADS, n // LANES, PEER_NKEYS, LANES)
    rspec = pl.BlockSpec((PEER_HEADS, ntg, PEER_NKEYS, LANES), lambda i: (0, i, 0, 0))
    return pl.pallas_call(
        _route_kernel,
        grid=(n // tb,),
        in_specs=[pl.BlockSpec((tb, D_MODEL), lambda i: (i, 0)), _full((1, D_MODEL)),
                  _mod_spec(4, tb, mod_base, rows_per_cond), _mod_spec(3, tb, mod_base, rows_per_cond),
                  _full(wqt.shape), _full(keys.shape)],
        out_specs=[pl.BlockSpec((tb, D_MODEL), lambda i: (i, 0)), rspec, rspec, rspec, rspec],
        out_shape=[jax.ShapeDtypeStruct((n, D_MODEL), BF16)] + [jax.ShapeDtypeStruct(hk, F32)] * 4,
        scratch_shapes=[pltpu.VMEM((PEER_HEADS * 2 * PEER_NKEYS, tb), BF16), pltpu.VMEM((2, PEER_TOPK, tb), F32)],
        compiler_params=_cp(1), name="peer_route",
    )(x, g.reshape(1, D_MODEL), mod, mod, wqt, keys)


PEER_TB = 512
PEER_EC = 1024
PEER_SUB = PEER_EC // 2


def _experts_kernel(hb_ref, th_ref, s2_ref, e2_ref, c1_ref, u0_ref, ub_ref, un_ref, vp_ref, va_ref, vl_ref,
                    x_ref, g2_ref, fg_ref, o_ref, acc_s, st0_s, st1_s, a0_s, a1_s, *, final_norm):
    e = pl.program_id(1)
    tb = hb_ref.shape[0]
    ntg = tb // LANES
    rows_per_sub = PEER_SUB // PEER_NKEYS

    def scores(u_blk, st_s, h):
        res = _dot_nt(u_blk[...], hb_ref[h * 2 * LANES:(h + 1) * 2 * LANES, :])
        st_s[2 * h] = res[:, :LANES]
        st_s[2 * h + 1] = res[:, LANES:]

    def values(vt_blk, act_s, h):
        act = jnp.concatenate([act_s[2 * h], act_s[2 * h + 1]], axis=1)
        hs = slice(h * 2 * LANES, (h + 1) * 2 * LANES)
        for r in range(0, D_MODEL, PEER_SUB):
            acc_s[r:r + PEER_SUB, hs] += _dot(vt_blk[r:r + PEER_SUB, :], act)

    def gates(c, tg, st_s, act_s):
        sub = 32
        for k in range(rows_per_sub):
            row = c * rows_per_sub + k
            for j0 in range(0, PEER_NKEYS, sub):
                js = slice(j0, j0 + sub)
                ks = slice(k * PEER_NKEYS + j0, k * PEER_NKEYS + j0 + sub)
                w = jnp.zeros((sub, LANES), F32)
                for hd in range(PEER_HEADS):
                    chosen = s2_ref[hd, tg, js, :] >= th_ref[hd, tg, row:row + 1, :]
                    w = w + jnp.where(chosen, e2_ref[hd, tg, js, :], 0.0) * c1_ref[hd, tg, row:row + 1, :]
                s = st_s[tg, ks, :]
                act = 0.5 * s * (1.0 + lax.erf(s * np.float32(2.0 ** -0.5)))
                act_s[tg, ks, :] = (act * w).astype(BF16)

    @pl.when(e == 0)
    def _():
        acc_s[...] = jnp.zeros_like(acc_s)
        a1_s[...] = jnp.zeros_like(a1_s)
        scores(u0_ref, st0_s, 0)
        scores(u0_ref, st0_s, 1)

    def phase(c, st_cur, act_cur, u_next, st_next, vt_prev, act_prev):
        for h in range(2):
            scores(u_next, st_next, h)
            gates(c, 2 * h, st_cur, act_cur)
            values(vt_prev, act_prev, h)
            gates(c, 2 * h + 1, st_cur, act_cur)

    phase(0, st0_s, a0_s, ub_ref, st1_s, vp_ref, a1_s)
    phase(1, st1_s, a1_s, un_ref, st0_s, va_ref, a0_s)

    @pl.when(e == pl.num_programs(1) - 1)
    def _():
        values(vl_ref, a1_s, 0)
        values(vl_ref, a1_s, 1)
        y = x_ref[...] + g2_ref[0] * acc_s[...].T
        if final_norm:
            y = _rms(y) * fg_ref[...]
        o_ref[...] = y


def _experts(hb, theta, s2, e2, c1, u, vt, x, mod, fg, *, mod_base, rows_per_cond, final_norm):
    n = x.shape[0]
    tb = PEER_TB
    ntg = tb // LANES
    ne = PEER_N // PEER_EC
    once = dict(pipeline_mode=pl.Buffered(1))
    rspec = pl.BlockSpec((PEER_HEADS, ntg, PEER_NKEYS, LANES), lambda i, e: (0, i, 0, 0), **once)
    rows8 = pl.BlockSpec((PEER_HEADS, ntg, SUBLANES, LANES), lambda i, e: (0, i, e, 0))
    uspec = lambda index: pl.BlockSpec((PEER_SUB, D_MODEL), index)
    vspec = lambda index: pl.BlockSpec((D_MODEL, PEER_SUB), index)
    return pl.pallas_call(
        functools.partial(_experts_kernel, final_norm=final_norm),
        grid=(n // tb, ne),
        in_specs=[pl.BlockSpec((tb, D_MODEL), lambda i, e: (i, 0), **once), rows8, rspec, rspec, rows8,
                  pl.BlockSpec((PEER_SUB, D_MODEL), lambda i, e: (0, 0), **once),
                  uspec(lambda i, e: (2 * e + 1, 0)),
                  uspec(lambda i, e: (jnp.minimum(2 * e + 2, 2 * ne - 2), 0)),
                  vspec(lambda i, e: (0, jnp.maximum(2 * e - 1, 0))),
                  vspec(lambda i, e: (0, 2 * e)),
                  pl.BlockSpec((D_MODEL, PEER_SUB), lambda i, e: (0, 2 * ne - 1), **once),
                  pl.BlockSpec((tb, D_MODEL), lambda i, e: (i, 0), **once),
                  _mod_spec(5, tb, mod_base, rows_per_cond), _full((1, D_MODEL))],
        out_specs=pl.BlockSpec((tb, D_MODEL), lambda i, e: (i, 0)),
        out_shape=jax.ShapeDtypeStruct((n, D_MODEL), F32),
        scratch_shapes=[pltpu.VMEM((D_MODEL, tb), F32), pltpu.VMEM((ntg, PEER_SUB, LANES), F32),
                        pltpu.VMEM((ntg, PEER_SUB, LANES), F32), pltpu.VMEM((ntg, PEER_SUB, LANES), BF16),
                        pltpu.VMEM((ntg, PEER_SUB, LANES), BF16)],
        compiler_params=_cp(2), name="peer_experts",
    )(hb, theta, s2, e2, c1, u, u, u, vt, vt, vt, x, mod, fg.reshape(1, D_MODEL))


def _peer(x, g, mod, wqt, keys, u, vt, fg, *, mod_base, rows_per_cond, final_norm):
    hb, theta, s2, e2, c1 = _route(x, g, mod, wqt, keys, mod_base=mod_base, rows_per_cond=rows_per_cond)
    return _experts(hb, theta, s2, e2, c1, u, vt, x, mod, fg, mod_base=mod_base, rows_per_cond=rows_per_cond,
                    final_norm=final_norm)


def _chunk_index(s, nc):
    return jnp.where(s < nc, s, 2 * nc - 1 - s)


def _scan_group(nseq, largest):
    return max(g for g in (2, 4, 8) if g <= largest and nseq % g == 0)


def _scan_specs(nc):
    def cur(col):
        return lambda b, s: (b, _chunk_index(s, nc), col)

    def prev(col):
        return lambda b, s: (b, jnp.maximum(_chunk_index(s, nc) * 8 - 1, 0), col)

    def nxt(col):
        return lambda b, s: (b, jnp.minimum(_chunk_index(s, nc) * 8 + 8, nc * 8 - 1), col)

    def out(col):
        return lambda b, s: (b, jnp.where(s < nc, nc - 1, 2 * nc - 1 - s), col)
    return cur, prev, nxt, out


def _conv_silu(prev, cur, nxt, w_ref, b_ref, first, last):
    win = jnp.concatenate([jnp.where(first, 0.0, prev), cur, jnp.where(last, 0.0, nxt)], axis=1)
    acc = b_ref[...] + win[:, 6:70, :] * w_ref[0:1, :]
    for k in range(1, CONV_W):
        acc = acc + win[:, 6 + k:70 + k, :] * w_ref[k:k + 1, :]
    return _silu(acc)


def _tri(bwd, G):
    r = lax.broadcasted_iota(jnp.int32, (G, CHUNK, CHUNK), 1)
    c = lax.broadcasted_iota(jnp.int32, (G, CHUNK, CHUNK), 2)
    lower = (c <= r).astype(F32)
    upper = (c >= r).astype(F32)
    return jnp.where(bwd, upper, lower), jnp.where(bwd, lower, upper)


def _bdot(a, b, contract, precision=None):
    return lax.dot_general(a, b, (((contract[0],), (contract[1],)), ((0,), (0,))), preferred_element_type=F32,
                           precision=precision)


def _ssd_kernel(z_ref, xp_ref, xc_ref, xn_ref, bp_ref, bc_ref, bn_ref, sm_ref, dtt_ref,
                cwx_ref, cbx_ref, cwb_ref, cbb_ref, arow_ref, acol_ref, brow_ref, bcol_ref, drow_ref, ng_ref,
                st0_ref, y_ref, st_ref, yf_s, h_s, xs_s, bc_s, *, nc):
    s = pl.program_id(1)
    bwd = s >= nc
    cidx = _chunk_index(s, nc)
    first = cidx == 0
    last = cidx == nc - 1
    nh = SSD_HEADS

    @pl.when(s == 0)
    def _():
        h_s[...] = st0_ref[:, 0]

    @pl.when(s == nc)
    def _():
        st_ref[:, 0] = h_s[...]
        h_s[...] = st0_ref[:, 1]

    G = z_ref.shape[0]
    tsel, tsel_t = _tri(bwd, G)
    mask = tsel > 0.0
    r0 = pl.multiple_of(cidx * CHUNK, CHUNK)
    hi = lax.Precision.HIGHEST

    @pl.when(jnp.logical_not(bwd))
    def _():
        xs_s[:, pl.ds(r0, CHUNK), :] = _conv_silu(xp_ref[...], xc_ref[...], xn_ref[...], cwx_ref, cbx_ref, first, last)
        bc_s[:, pl.ds(r0, CHUNK), :] = _conv_silu(bp_ref[...], bc_ref[...], bn_ref[...], cwb_ref, cbb_ref, first, last)
    xs = xs_s[:, pl.ds(r0, CHUNK), :]
    bcv = bc_s[:, pl.ds(r0, CHUNK), :]
    dt_all = _softplus(sm_ref[:, :, 64:96] + brow_ref[...])
    dtt_all = _softplus(dtt_ref[:, 0] + bcol_ref[...])
    a_all = dt_all * (-jnp.exp(arow_ref[...]))
    at_all = dtt_all * (-jnp.exp(acol_ref[...]))
    dt = jnp.where(bwd, dt_all[:, :, nh:2 * nh], dt_all[:, :, 0:nh])
    a = jnp.where(bwd, a_all[:, :, nh:2 * nh], a_all[:, :, 0:nh])
    at = jnp.where(bwd, at_all[:, nh:2 * nh, :], at_all[:, 0:nh, :])
    cum = _bdot(tsel, a, (2, 1), hi)
    cum_t = _bdot(at, tsel_t, (2, 1), hi)
    tot = jnp.where(bwd, cum[:, 0:1, :], cum[:, CHUNK - 1:CHUNK, :])

    ys = []
    for g in range(2):
        bg = bcv[:, :, g * SSD_STATE:(g + 1) * SSD_STATE]
        cg = bcv[:, :, 256 + g * SSD_STATE:256 + (g + 1) * SSD_STATE]
        cg16 = cg.astype(BF16)
        cb = _bdot(cg16, bg.astype(BF16), (2, 2))
        for hh in range(nh // 2):
            h = g * (nh // 2) + hh
            cq = cum[:, :, h:h + 1]
            decay = jnp.exp(jnp.where(mask, cq - cum_t[:, h:h + 1, :], NEG_INF))
            xd = xs[:, :, h * SSD_HEAD_DIM:(h + 1) * SSD_HEAD_DIM] * dt[:, :, h:h + 1]
            xd16 = xd.astype(BF16)
            hprev = h_s[:, h]
            y = (_bdot((cb * decay).astype(BF16), xd16, (2, 1))
                 + _bdot(cg16, hprev.astype(BF16), (2, 2)) * jnp.exp(cq))
            ys.append(y)
            th = tot[:, :, h:h + 1]
            bw = bg * jnp.exp(th - cq)
            h_s[:, h] = hprev * jnp.exp(th) + _bdot(xd16, bw.astype(BF16), (1, 1))
    y = jnp.concatenate(ys, axis=2)

    @pl.when(jnp.logical_not(bwd))
    def _():
        yf_s[:, pl.ds(r0, CHUNK), :] = y

    @pl.when(bwd)
    def _():
        yt = (yf_s[:, pl.ds(r0, CHUNK), :] + y + drow_ref[...] * xs) * _silu(z_ref[...])
        y_ref[...] = _rms(yt) * ng_ref[...]

    @pl.when(s == 2 * nc - 1)
    def _():
        st_ref[:, 1] = h_s[...]


def _ssd(p0, dtt, cwx, cbx, cwb, cbb, a_log, dt_bias, d_skip, norm_g, st0, *, L):
    nseq, nc = p0.shape[0], L // CHUNK
    G = _scan_group(nseq, 4)
    cur, prev, nxt, out = _scan_specs(nc)
    st_shape = (G, 2, SSD_HEADS, SSD_HEAD_DIM, SSD_STATE)
    per_seq = st0 is not None
    if st0 is None:
        st0 = jnp.zeros(st_shape, F32)
    row = lambda v: v.reshape(1, -1)
    col = lambda v: v.reshape(-1, 1)
    small = [cwx, row(cbx), cwb, row(cbb), row(a_log), col(a_log), row(dt_bias), col(dt_bias),
             row(jnp.repeat(d_skip, SSD_HEAD_DIM)), row(norm_g)]
    blk = lambda rows, width, index: pl.BlockSpec((G, rows, width), index)
    return pl.pallas_call(
        functools.partial(_ssd_kernel, nc=nc),
        grid=(nseq // G, 2 * nc),
        in_specs=[blk(CHUNK, 1024, cur(0)),
                  blk(8, 1024, prev(1)), blk(CHUNK, 1024, cur(1)), blk(8, 1024, nxt(1)),
                  blk(8, 512, prev(4)), blk(CHUNK, 512, cur(4)), blk(8, 512, nxt(4)),
                  blk(CHUNK, LANES, cur(28)),
                  pl.BlockSpec((G, 1, 32, CHUNK), lambda b, s: (b, _chunk_index(s, nc), 0, 0))]
        + [_full(v.shape) for v in small]
        + [pl.BlockSpec(st_shape, lambda b, s: (b if per_seq else 0, 0, 0, 0, 0))],
        out_specs=[blk(CHUNK, SSD_D_INNER, out(0)),
                   pl.BlockSpec(st_shape, lambda b, s: (b, 0, 0, 0, 0))],
        out_shape=[jax.ShapeDtypeStruct((nseq, L, SSD_D_INNER), F32),
                   jax.ShapeDtypeStruct((nseq,) + st_shape[1:], F32)],
        scratch_shapes=[pltpu.VMEM((G, L, SSD_D_INNER), F32), pltpu.VMEM(st_shape[:1] + st_shape[2:], F32),
                        pltpu.VMEM((G, L, SSD_D_INNER), F32), pltpu.VMEM((G, L, 512), F32)],
        compiler_params=_cp(2), name="ssd_mixer",
    )(p0, p0, p0, p0, p0, p0, p0, p0, dtt, *small, st0)


def _rope_tables(L):
    t = jnp.arange(L)
    r = (t // GRID_W).astype(F32)
    c = (t % GRID_W).astype(F32)
    nf = 16
    inv = ROPE_BASE ** (-jnp.arange(nf, dtype=F32) / nf)
    ang = jnp.concatenate([r[:, None] * inv, c[:, None] * inv], axis=-1)
    ang = jnp.concatenate([ang, ang], axis=-1)
    return jnp.cos(ang), jnp.sin(ang)


def _rope128(x, cos, sin):
    lane = lax.broadcasted_iota(jnp.int32, x.shape, 1)
    rot = jnp.where(lane % 64 < 32, -pltpu.roll(x, 96, 1), pltpu.roll(x, 32, 1))
    return x * cos + rot * sin


def _pad_lanes(x):
    return jnp.concatenate([x, jnp.zeros_like(x)], axis=1)


MLA_QB = 256


def _mla_kernel(*refs, L, Lc, rope, emit):
    refs = list(refs)
    qa_ref, kva_ref, sm_ref, qg_ref, kvg_ref, wqn_ref, wqr_ref, wkk_ref, wkv_ref = refs[:9]
    pos = 9
    if Lc:
        cckv_ref, ckpe_ref = refs[pos:pos + 2]
        pos += 2
    if rope:
        cos_ref, sin_ref = refs[pos:pos + 2]
        pos += 2
    y_ref = refs[pos]
    pos += 1
    if emit:
        ckv_ref = refs[pos]
        pos += 1
    qn_s, qr_s, kn_s, v_s, kpe_s = refs[pos:]
    qb = pl.program_id(1)
    scale = np.float32((MLA_NOPE + MLA_ROPE) ** -0.5)

    @pl.when(qb == 0)
    def _():
        ckv = _rms(kva_ref[...]) * kvg_ref[...]
        if emit:
            ckv_ref[...] = ckv
        kpe = _pad_lanes(sm_ref[:, 0:MLA_ROPE])
        if rope:
            kpe = _rope128(kpe, cos_ref[...], sin_ref[...])
        if Lc:
            ckv = jnp.concatenate([cckv_ref[0], ckv], axis=0)
            kpe = jnp.concatenate([_pad_lanes(ckpe_ref[0]), kpe], axis=0)
        c16 = ckv.astype(BF16)
        kn_s[...] = _dot(c16, wkk_ref[...]).astype(BF16)
        v_s[...] = _dot(c16, wkv_ref[...]).astype(BF16)
        kpe_s[...] = kpe.astype(BF16)

    qn = (_rms(qa_ref[...]) * qg_ref[...]).astype(BF16)
    qn_s[...] = _dot(qn, wqn_ref[...]).astype(BF16)
    qr = _dot(qn, wqr_ref[...])
    if rope:
        q0 = pl.multiple_of(qb * MLA_QB, MLA_QB)
        cos = cos_ref[pl.ds(q0, MLA_QB), :]
        sin = sin_ref[pl.ds(q0, MLA_QB), :]
        for h in range(MLA_HEADS):
            qr_s[:, h * LANES:(h + 1) * LANES] = _rope128(qr[:, h * LANES:(h + 1) * LANES], cos, sin).astype(BF16)
    else:
        qr_s[...] = qr.astype(BF16)

    def head(h, _):
        c0 = h * LANES if isinstance(h, int) else pl.multiple_of(h * LANES, LANES)
        s = (_dot_nt(qn_s[:, pl.ds(c0, LANES)], kn_s[:, pl.ds(c0, LANES)])
             + _dot_nt(qr_s[:, pl.ds(c0, LANES)], kpe_s[...])) * scale
        p = jnp.exp(s - jnp.max(s, axis=-1, keepdims=True))
        p = p / jnp.sum(p, axis=-1, keepdims=True)
        y_ref[:, pl.ds(c0, LANES)] = _dot(p.astype(BF16), v_s[:, pl.ds(c0, LANES)])
        return 0

    if L + Lc <= 512:
        for h in range(MLA_HEADS):
            head(h, 0)
    else:
        lax.fori_loop(0, MLA_HEADS, head, 0)


def _mla(p0, qg, kvg, wqn, wqr, wkk, wkv, cache_ckv, cache_kpe, rope_tabs, *, L):
    n = p0.shape[0]
    nseq, nqb = n // L, L // MLA_QB
    Lc = 0 if cache_ckv is None else cache_ckv.shape[1]
    rope = rope_tabs is not None
    emit = cache_ckv is None
    Lk = L + Lc
    args = [p0, p0, p0, qg.reshape(1, -1), kvg.reshape(1, -1), wqn, wqr, wkk, wkv]
    specs = [pl.BlockSpec((MLA_QB, MLA_RANK), lambda b, q: (b * nqb + q, 5)),
             pl.BlockSpec((L, MLA_RANK), lambda b, q: (b, 6)),
             pl.BlockSpec((L, LANES), lambda b, q: (b, 28)),
             _full((1, MLA_RANK)), _full((1, MLA_RANK)), _full(wqn.shape), _full(wqr.shape), _full(wkk.shape),
             _full(wkv.shape)]
    if Lc:
        args += [cache_ckv, cache_kpe]
        specs += [pl.BlockSpec((1, Lc, MLA_RANK), lambda b, q: (b, 0, 0)),
                  pl.BlockSpec((1, Lc, MLA_ROPE), lambda b, q: (b, 0, 0))]
    if rope:
        args += list(rope_tabs)
        specs += [_full((L, LANES)), _full((L, LANES))]
    out_specs = [pl.BlockSpec((MLA_QB, 1024), lambda b, q: (b * nqb + q, 0))]
    out_shape = [jax.ShapeDtypeStruct((n, 1024), F32)]
    if emit:
        out_specs.append(pl.BlockSpec((L, MLA_RANK), lambda b, q: (b, 0)))
        out_shape.append(jax.ShapeDtypeStruct((n, MLA_RANK), F32))
    return pl.pallas_call(
        functools.partial(_mla_kernel, L=L, Lc=Lc, rope=rope, emit=emit),
        grid=(nseq, nqb), in_specs=specs, out_specs=out_specs, out_shape=out_shape,
        scratch_shapes=[pltpu.VMEM((MLA_QB, 1024), BF16), pltpu.VMEM((MLA_QB, 1024), BF16),
                        pltpu.VMEM((Lk, 1024), BF16), pltpu.VMEM((Lk, 1024), BF16), pltpu.VMEM((Lk, LANES), BF16)],
        compiler_params=_cp(2), name="mla_attention",
    )(*args)


def _gqa_ctx_kernel(q_ref, k_ref, v_ref, sink_ref, y_ref):
    q = q_ref[...]
    k = k_ref[...].astype(BF16)
    v = v_ref[...].astype(BF16)
    scale = np.float32(GQA_HEAD_DIM ** -0.5)
    group = GQA_HEADS // GQA_KV_HEADS
    outs = []
    for h in range(GQA_HEADS):
        kh = h // group
        ks = slice(kh * GQA_HEAD_DIM, (kh + 1) * GQA_HEAD_DIM)
        s = _dot_nt(q[:, h * GQA_HEAD_DIM:(h + 1) * GQA_HEAD_DIM].astype(BF16), k[:, ks]) * scale
        m = jnp.maximum(jnp.max(s, axis=-1, keepdims=True), sink_ref[h])
        p = jnp.exp(s - m)
        p = p / (jnp.sum(p, axis=-1, keepdims=True) + jnp.exp(sink_ref[h] - m))
        outs.append(_dot(p.astype(BF16), v[:, ks]))
    y_ref[...] = jnp.concatenate(outs, axis=1)


def _gqa_ctx(p1, sink, *, L):
    n = p1.shape[0]
    return pl.pallas_call(
        _gqa_ctx_kernel,
        grid=(n // L,),
        in_specs=[pl.BlockSpec((L, 1024), lambda b: (b, 2)), pl.BlockSpec((L, 256), lambda b: (b, 20)),
                  pl.BlockSpec((L, 256), lambda b: (b, 21)), pl.BlockSpec(memory_space=pltpu.SMEM)],
        out_specs=pl.BlockSpec((L, 1024), lambda b: (b, 0)),
        out_shape=jax.ShapeDtypeStruct((n, 1024), F32),
        compiler_params=_cp(1), name="gqa_context",
    )(p1, p1, p1, sink)


GQA_QB = 128
GQA_SPAN = GQA_QB + 2 * WINDOW


def _gqa_win_kernel(q_ref, k_ref, v_ref, ck_ref, cv_ref, cos_ref, sin_ref, sink_ref, y_ref, kp_s, vp_s, *, L):
    qb = pl.program_id(1)
    scale = np.float32(GQA_HEAD_DIM ** -0.5)
    group = GQA_HEADS // GQA_KV_HEADS

    @pl.when(qb == 0)
    def _():
        zeros = jnp.zeros((WINDOW, 256), BF16)
        kp_s[0:WINDOW, :] = zeros
        kp_s[WINDOW + L:, :] = zeros
        vp_s[0:WINDOW, :] = zeros
        vp_s[WINDOW + L:, :] = zeros
        for j in range(2):
            cs = slice(j * LANES, (j + 1) * LANES)
            kp_s[WINDOW:WINDOW + L, cs] = _rope128(k_ref[:, cs], cos_ref[...], sin_ref[...]).astype(BF16)
        vp_s[WINDOW:WINDOW + L, :] = v_ref[...].astype(BF16)

    q0 = pl.multiple_of(qb * GQA_QB, GQA_QB)
    cos = cos_ref[pl.ds(q0, GQA_QB), :]
    sin = sin_ref[pl.ds(q0, GQA_QB), :]
    kw = kp_s[pl.ds(q0, GQA_SPAN), :]
    vw = vp_s[pl.ds(q0, GQA_SPAN), :]
    kc = ck_ref[0].astype(BF16)
    vc = cv_ref[0].astype(BF16)
    qpos = q0 + lax.broadcasted_iota(jnp.int32, (GQA_QB, GQA_SPAN), 0)
    kpos = q0 - WINDOW + lax.broadcasted_iota(jnp.int32, (GQA_QB, GQA_SPAN), 1)
    bias = jnp.where(kpos < 0, NEG_INF, jnp.where(kpos >= L, NEG_INF, jnp.where(jnp.abs(qpos - kpos) <= WINDOW, 0.0, NEG_INF)))
    outs = []
    for j in range(GQA_HEADS * GQA_HEAD_DIM // LANES):
        qj = _rope128(q_ref[:, j * LANES:(j + 1) * LANES], cos, sin).astype(BF16)
        for half in range(2):
            h = 2 * j + half
            kh = h // group
            ks = slice(kh * GQA_HEAD_DIM, (kh + 1) * GQA_HEAD_DIM)
            qh = qj[:, half * GQA_HEAD_DIM:(half + 1) * GQA_HEAD_DIM]
            s_loc = _dot_nt(qh, kw[:, ks]) * scale + bias
            s_ctx = _dot_nt(qh, kc[:, ks]) * scale
            m = jnp.maximum(jnp.maximum(jnp.max(s_loc, axis=-1, keepdims=True), jnp.max(s_ctx, axis=-1, keepdims=True)),
                            sink_ref[h])
            p_loc = jnp.exp(s_loc - m)
            p_ctx = jnp.exp(s_ctx - m)
            inv = 1.0 / (jnp.sum(p_loc, axis=-1, keepdims=True) + jnp.sum(p_ctx, axis=-1, keepdims=True)
                         + jnp.exp(sink_ref[h] - m))
            outs.append(_dot((p_loc * inv).astype(BF16), vw[:, ks]) + _dot((p_ctx * inv).astype(BF16), vc[:, ks]))
    y_ref[...] = jnp.concatenate(outs, axis=1)


def _gqa_win(p1, cache_k, cache_v, cos, sin, sink, *, L):
    n = p1.shape[0]
    nseq, nqb = n // L, L // GQA_QB
    Lc = cache_k.shape[1]
    return pl.pallas_call(
        functools.partial(_gqa_win_kernel, L=L),
        grid=(nseq, nqb),
        in_specs=[pl.BlockSpec((GQA_QB, 1024), lambda b, q: (b * nqb + q, 2)),
                  pl.BlockSpec((L, 256), lambda b, q: (b, 20)), pl.BlockSpec((L, 256), lambda b, q: (b, 21)),
                  pl.BlockSpec((1, Lc, 256), lambda b, q: (b, 0, 0)), pl.BlockSpec((1, Lc, 256), lambda b, q: (b, 0, 0)),
                  _full((L, LANES)), _full((L, LANES)), pl.BlockSpec(memory_space=pltpu.SMEM)],
        out_specs=pl.BlockSpec((GQA_QB, 1024), lambda b, q: (b * nqb + q, 0)),
        out_shape=jax.ShapeDtypeStruct((n, 1024), F32),
        scratch_shapes=[pltpu.VMEM((L + 2 * WINDOW, 256), BF16), pltpu.VMEM((L + 2 * WINDOW, 256), BF16)],
        compiler_params=_cp(2), name="gqa_window",
    )(p1, p1, p1, cache_k, cache_v, cos, sin, sink)


def _mlstm_kernel(qp_ref, qc_ref, qn_ref, v_ref, o_ref, sm_ref, gt_ref, cw_ref, cb_ref, brow_ref, bcol_ref, ng_ref,
                  c0_ref, n0_ref, m0_ref, y_ref, cst_ref, nst_ref, mst_ref, hf_s, c_s, n_s, m_s, qk_s, *, nc):
    s = pl.program_id(1)
    bwd = s >= nc
    cidx = _chunk_index(s, nc)
    nh = ML_HEADS
    hd = ML_HEAD_DIM

    @pl.when(s == 0)
    def _():
        c_s[...] = c0_ref[:, 0]
        n_s[...] = n0_ref[:, 0]
        m_s[...] = m0_ref[:, 0]

    @pl.when(s == nc)
    def _():
        cst_ref[:, 0] = c_s[...]
        nst_ref[:, 0] = n_s[...]
        mst_ref[:, 0] = m_s[...]
        c_s[...] = c0_ref[:, 1]
        n_s[...] = n0_ref[:, 1]
        m_s[...] = m0_ref[:, 1]

    G = qc_ref.shape[0]
    tsel, tsel_t = _tri(bwd, G)
    mask = tsel > 0.0
    kscale = np.float32(hd ** -0.5)
    r0 = pl.multiple_of(cidx * CHUNK, CHUNK)
    hi = lax.Precision.HIGHEST

    @pl.when(jnp.logical_not(bwd))
    def _():
        qk_s[:, pl.ds(r0, CHUNK), :] = _conv_silu(qp_ref[...], qc_ref[...], qn_ref[...], cw_ref, cb_ref,
                                                  cidx == 0, cidx == nc - 1)
    qk = qk_s[:, pl.ds(r0, CHUNK), :]
    gates = sm_ref[:, :, 0:4 * nh] + brow_ref[...]
    gates_t = gt_ref[:, 0] + bcol_ref[...]
    li = jnp.where(bwd, gates[:, :, nh:2 * nh], gates[:, :, 0:nh])
    li_t = jnp.where(bwd, gates_t[:, nh:2 * nh, :], gates_t[:, 0:nh, :])
    lf = -_softplus(-jnp.where(bwd, gates[:, :, 3 * nh:4 * nh], gates[:, :, 2 * nh:3 * nh]))
    lf_t = -_softplus(-jnp.where(bwd, gates_t[:, 3 * nh:4 * nh, :], gates_t[:, 2 * nh:3 * nh, :]))
    bc = _bdot(tsel, lf, (2, 1), hi)
    bc_t = _bdot(lf_t, tsel_t, (2, 1), hi)
    tot = jnp.where(bwd, bc[:, 0:1, :], bc[:, CHUNK - 1:CHUNK, :])

    hs = []
    for h in range(nh):
        cs = slice(h * hd, (h + 1) * hd)
        q = qk[:, :, cs]
        k = qk[:, :, ML_D + h * hd:ML_D + (h + 1) * hd] * kscale
        q16, k16 = q.astype(BF16), k.astype(BF16)
        v = v_ref[:, :, cs]
        bq = bc[:, :, h:h + 1]
        m_prev = m_s[:, h:h + 1, 0:1]
        log_d = jnp.where(mask, bq - bc_t[:, h:h + 1, :] + li_t[:, h:h + 1, :], NEG_INF)
        log_inter = bq + m_prev
        m_out = jnp.maximum(log_inter, jnp.max(log_d, axis=2, keepdims=True))
        sd = _bdot(q16, k16, (2, 2)) * jnp.exp(log_d - m_out)
        w_inter = jnp.exp(log_inter - m_out)
        c_prev = c_s[:, h]
        n_prev = n_s[:, h:h + 1, :]
        num = _bdot(sd.astype(BF16), v.astype(BF16), (2, 1)) + w_inter * _bdot(q16, c_prev.astype(BF16), (2, 2))
        den = jnp.sum(sd, axis=2, keepdims=True) + w_inter * jnp.sum(q * n_prev, axis=2, keepdims=True)
        hs.append(num / jnp.maximum(jnp.abs(den), jnp.exp(-m_out)))
        th = tot[:, :, h:h + 1]
        end_inter = th + m_prev
        end_intra = th - bq + li[:, :, h:h + 1]
        m_new = jnp.maximum(end_inter, jnp.max(end_intra, axis=1, keepdims=True))
        w_c = jnp.exp(end_inter - m_new)
        w_k = jnp.exp(end_intra - m_new)
        c_s[:, h] = w_c * c_prev + _bdot((v * w_k).astype(BF16), k16, (1, 1))
        n_s[:, h:h + 1, :] = w_c * n_prev + jnp.sum(k * w_k, axis=1, keepdims=True)
        m_s[:, h:h + 1, :] = jnp.broadcast_to(m_new, (G, 1, hd))
    hcat = jnp.concatenate(hs, axis=2)

    @pl.when(jnp.logical_not(bwd))
    def _():
        hf_s[:, pl.ds(r0, CHUNK), :] = hcat

    @pl.when(bwd)
    def _():
        tot_h = hf_s[:, pl.ds(r0, CHUNK), :] + hcat
        normed = jnp.concatenate([_rms(tot_h[:, :, h * hd:(h + 1) * hd]) for h in range(nh)], axis=2) * ng_ref[...]
        y_ref[...] = normed * jax.nn.sigmoid(o_ref[...])

    @pl.when(s == 2 * nc - 1)
    def _():
        cst_ref[:, 1] = c_s[...]
        nst_ref[:, 1] = n_s[...]
        mst_ref[:, 1] = m_s[...]


def _mlstm(p1, gt, conv_w, conv_b, ig_b, fg_b, norm_g, c0, n0, m0, *, L):
    nseq, nc = p1.shape[0], L // CHUNK
    G = _scan_group(nseq, 4)
    cur, prev, nxt, out = _scan_specs(nc)
    cshape = (G, 2, ML_HEADS, ML_HEAD_DIM, ML_HEAD_DIM)
    nshape = (G, 2, ML_HEADS, ML_HEAD_DIM)
    per_seq = c0 is not None
    if c0 is None:
        c0, n0, m0 = jnp.zeros(cshape, F32), jnp.zeros(nshape, F32), jnp.zeros(nshape, F32)
    gb = jnp.concatenate([ig_b.reshape(-1), fg_b.reshape(-1)])
    small = [conv_w, conv_b.reshape(1, -1), gb.reshape(1, -1), gb.reshape(-1, 1), norm_g.reshape(1, -1)]
    st_in = lambda shape: pl.BlockSpec(shape, lambda b, s: (b if per_seq else 0,) + (0,) * (len(shape) - 1))
    st_out = lambda shape: pl.BlockSpec(shape, lambda b, s: (b,) + (0,) * (len(shape) - 1))
    blk = lambda rows, width, index: pl.BlockSpec((G, rows, width), index)
    return pl.pallas_call(
        functools.partial(_mlstm_kernel, nc=nc),
        grid=(nseq // G, 2 * nc),
        in_specs=[blk(8, 2048, prev(0)), blk(CHUNK, 2048, cur(0)), blk(8, 2048, nxt(0)),
                  blk(CHUNK, 1024, cur(3)), blk(CHUNK, 1024, cur(4)), blk(CHUNK, LANES, cur(44)),
                  pl.BlockSpec((G, 1, 32, CHUNK), lambda b, s: (b, _chunk_index(s, nc), 0, 0))]
        + [_full(v.shape) for v in small] + [st_in(cshape), st_in(nshape), st_in(nshape)],
        out_specs=[blk(CHUNK, ML_D, out(0)), st_out(cshape), st_out(nshape), st_out(nshape)],
        out_shape=[jax.ShapeDtypeStruct((nseq, L, ML_D), F32), jax.ShapeDtypeStruct((nseq,) + cshape[1:], F32),
                   jax.ShapeDtypeStruct((nseq,) + nshape[1:], F32), jax.ShapeDtypeStruct((nseq,) + nshape[1:], F32)],
        scratch_shapes=[pltpu.VMEM((G, L, ML_D), F32), pltpu.VMEM(cshape[:1] + cshape[2:], F32),
                        pltpu.VMEM(nshape[:1] + nshape[2:], F32), pltpu.VMEM(nshape[:1] + nshape[2:], F32),
                        pltpu.VMEM((G, L, 2 * ML_D), F32)],
        compiler_params=_cp(2), name="mlstm_mixer",
    )(p1, p1, p1, p1, p1, p1, gt, *small, c0, n0, m0)


def _prep_layer0(w_in, w_qb, w_kvb, w_out):
    pad = jnp.zeros((D_MODEL, L0_COLS - 3680), F32)
    w = jnp.concatenate([w_in[:, :2560], w_in[:, 2592:3680], w_in[:, 2560:2592], pad], axis=1).astype(BF16)
    q3 = w_qb.reshape(MLA_RANK, MLA_HEADS, MLA_NOPE + MLA_ROPE)
    wqn = q3[:, :, :MLA_NOPE].reshape(MLA_RANK, -1).astype(BF16)
    wqr = jnp.concatenate([q3[:, :, MLA_NOPE:], jnp.zeros((MLA_RANK, MLA_HEADS, LANES - MLA_ROPE), F32)], axis=-1)
    wqr = wqr.reshape(MLA_RANK, -1).astype(BF16)
    k3 = w_kvb.reshape(MLA_RANK, MLA_HEADS, MLA_NOPE + MLA_V)
    wkk = k3[:, :, :MLA_NOPE].reshape(MLA_RANK, -1).astype(BF16)
    wkv = k3[:, :, MLA_NOPE:].reshape(MLA_RANK, -1).astype(BF16)
    return w, wqn, wqr, wkk, wkv, w_out.astype(BF16)


def _prep_layer1(w_in, w_out):
    pad = jnp.zeros((D_MODEL, L1_COLS - 5664), F32)
    w = jnp.concatenate([w_in[:, 1536:3584], w_in[:, 0:1024], w_in[:, 3584:5632], w_in[:, 1024:1536],
                         w_in[:, 5632:5664], pad], axis=1).astype(BF16)
    return w, w_out.astype(BF16)


def _prep_peer(wq, keys, u, v):
    return (wq.T.astype(BF16), keys.reshape(PEER_HEADS * 2, PEER_NKEYS, -1).astype(BF16), u.astype(BF16),
            v.T.astype(BF16))


def _chunk_transposed(cols, L):
    return cols.reshape(-1, L // CHUNK, CHUNK, cols.shape[1]).transpose(0, 1, 3, 2)


def _trunk(x, L, mod0, mod1, w0, w1, peer0, peer1, p, ctx, rope_tabs, mod_base, rows_per_cond, fg):
    kw = dict(mod_base=mod_base, rows_per_cond=rows_per_cond)
    w_in0, wqn, wqr, wkk, wkv, w_out0 = w0
    w_in1, w_out1 = w1
    nseq = x.shape[0] // L

    p0 = _proj(x, p['l0_norm1_g'], mod0, w_in0, tn=1280, **kw)
    cw, cb = p['l0_ssd_conv_w'], p['l0_ssd_conv_b']
    y_ssd, ssd_state = _ssd(p0.reshape(nseq, L, -1), _chunk_transposed(p0[:, 3648:3680], L), cw[:, :1024], cb[:1024],
                            cw[:, 1024:], cb[1024:], p['l0_ssd_A_log'], p['l0_ssd_dt_bias'], p['l0_ssd_D'],
                            p['l0_ssd_norm_g'], None if ctx is None else ctx[0], L=L)
    y_ssd = y_ssd.reshape(nseq * L, -1)
    mla_tabs = None
    if rope_tabs is not None:
        z64 = jnp.zeros_like(rope_tabs[0])
        mla_tabs = (jnp.concatenate([rope_tabs[0], z64], axis=1), jnp.concatenate([rope_tabs[1], z64], axis=1))
    mla_out = _mla(p0, p['l0_mla_q_norm_g'], p['l0_mla_kv_norm_g'], wqn, wqr, wkk, wkv,
                   None if ctx is None else ctx[1], None if ctx is None else ctx[2], mla_tabs, L=L)
    x = _outproj(y_ssd, mla_out[0], w_out0, x, mod0, **kw)
    x = _peer(x, p['l0_norm2_g'], mod0, *peer0, fg, final_norm=False, **kw)

    p1 = _proj(x, p['l1_norm1_g'], mod1, w_in1, tn=1152, **kw)
    if ctx is None:
        y_gqa = _gqa_ctx(p1, p['l1_gqa_sink'], L=L)
        c0 = n0 = m0 = None
    else:
        gqa_tabs = tuple(jnp.concatenate([t, t], axis=1) for t in rope_tabs)
        y_gqa = _gqa_win(p1, ctx[3].reshape(nseq, -1, 256), ctx[4].reshape(nseq, -1, 256), *gqa_tabs,
                         p['l1_gqa_sink'], L=L)
        c0, n0 = ctx[5], ctx[6]
        m0 = jnp.broadcast_to(ctx[7][..., None], ctx[7].shape + (ML_HEAD_DIM,))
    y_ml, mc, mn, mm = _mlstm(p1.reshape(nseq, L, -1), _chunk_transposed(p1[:, 5632:5664], L), p['l1_ml_conv_w'],
                              p['l1_ml_conv_b'], p['l1_ml_ig_b'], p['l1_ml_fg_b'], p['l1_ml_norm_g'], c0, n0, m0, L=L)
    y_ml = y_ml.reshape(nseq * L, -1)
    x = _outproj(y_gqa, y_ml, w_out1, x, mod1, **kw)
    y = _peer(x, p['l1_norm2_g'], mod1, *peer1, fg, final_norm=True, **kw)
    new = None
    if ctx is None:
        new = (ssd_state, mla_out[1].reshape(nseq, L, MLA_RANK), p0[:, 3584:3648].reshape(nseq, L, MLA_ROPE),
               p1[:, 5120:5376].reshape(nseq, L, GQA_KV_HEADS, GQA_HEAD_DIM),
               p1[:, 5376:5632].reshape(nseq, L, GQA_KV_HEADS, GQA_HEAD_DIM), mc, mn, mm[..., 0])
    return y, new


def kernel(x_prompt, x_sample, state_l0_ssd, cache_l0_mla_ckv, cache_l0_mla_kpe, cache_l1_gqa_k, cache_l1_gqa_v, state_l1_mlstm_C, state_l1_mlstm_n, state_l1_mlstm_m, c, c_ctx, final_norm_g, l0_ada_w, l0_ada_b, l0_norm1_g, l0_norm2_g, l0_w_in, l0_ssd_conv_w, l0_ssd_conv_b, l0_ssd_A_log, l0_ssd_dt_bias, l0_ssd_D, l0_ssd_norm_g, l0_mla_q_norm_g, l0_mla_w_qb, l0_mla_kv_norm_g, l0_mla_w_kvb, l0_w_out, l0_peer_wq, l0_peer_keys, l0_peer_u, l0_peer_v, l1_ada_w, l1_ada_b, l1_norm1_g, l1_norm2_g, l1_w_in, l1_gqa_sink, l1_ml_conv_w, l1_ml_conv_b, l1_ml_ig_b, l1_ml_fg_b, l1_ml_norm_g, l1_w_out, l1_peer_wq, l1_peer_keys, l1_peer_u, l1_peer_v):
    p = dict(l0_norm1_g=l0_norm1_g, l0_norm2_g=l0_norm2_g, l0_ssd_conv_w=l0_ssd_conv_w, l0_ssd_conv_b=l0_ssd_conv_b,
             l0_ssd_A_log=l0_ssd_A_log, l0_ssd_dt_bias=l0_ssd_dt_bias, l0_ssd_D=l0_ssd_D, l0_ssd_norm_g=l0_ssd_norm_g,
             l0_mla_q_norm_g=l0_mla_q_norm_g, l0_mla_kv_norm_g=l0_mla_kv_norm_g, l1_norm1_g=l1_norm1_g,
             l1_norm2_g=l1_norm2_g, l1_gqa_sink=l1_gqa_sink, l1_ml_conv_w=l1_ml_conv_w, l1_ml_conv_b=l1_ml_conv_b,
             l1_ml_ig_b=l1_ml_ig_b, l1_ml_fg_b=l1_ml_fg_b, l1_ml_norm_g=l1_ml_norm_g)
    nb, seq = x_prompt.shape[:2]
    db, dseq = x_sample.shape[:2]
    assert db <= 7 and seq % MLA_QB == 0 and dseq % MLA_QB == 0

    cond8 = jnp.zeros((8, D_MODEL), F32).at[0].set(c_ctx).at[1:1 + db].set(c)
    mod0 = _ada(cond8, l0_ada_w, l0_ada_b).reshape(8, 1, 6 * D_MODEL)
    mod1 = _ada(cond8, l1_ada_w, l1_ada_b).reshape(8, 1, 6 * D_MODEL)
    w0 = _prep_layer0(l0_w_in, l0_mla_w_qb, l0_mla_w_kvb, l0_w_out)
    w1 = _prep_layer1(l1_w_in, l1_w_out)
    peer0 = _prep_peer(l0_peer_wq, l0_peer_keys, l0_peer_u, l0_peer_v)
    peer1 = _prep_peer(l1_peer_wq, l1_peer_keys, l1_peer_u, l1_peer_v)

    y_prompt, new = _trunk(x_prompt.reshape(nb * seq, D_MODEL), seq, mod0, mod1, w0, w1, peer0, peer1, p, None, None,
                           0, nb * seq, final_norm_g)
    ctx = (state_l0_ssd, cache_l0_mla_ckv, cache_l0_mla_kpe, cache_l1_gqa_k, cache_l1_gqa_v, state_l1_mlstm_C,
           state_l1_mlstm_n, state_l1_mlstm_m)
    y_sample, _ = _trunk(x_sample.reshape(db * dseq, D_MODEL), dseq, mod0, mod1, w0, w1, peer0, peer1, p, ctx,
                         _rope_tables(dseq), 1, dseq, final_norm_g)
    return (y_prompt.reshape(nb, seq, D_MODEL), y_sample.reshape(db, dseq, D_MODEL)) + new
```

```python
import functools

import jax
import jax.numpy as jnp
import numpy as np
from jax import lax
from jax.experimental import pallas as pl
from jax.experimental.pallas import tpu as pltpu

F32 = jnp.float32
BF16 = jnp.bfloat16
NEG_INF = float("-inf")

D_MODEL = 2048
EPS = 1e-6
ROPE_BASE = 10000.0
GRID_W = 64
CONV_W = 5
CHUNK = 64
SSD_HEADS = 16
SSD_HEAD_DIM = 64
SSD_D_INNER = 1024
SSD_STATE = 128
MLA_HEADS = 8
MLA_NOPE = 128
MLA_ROPE = 64
MLA_V = 128
MLA_RANK = 512
GQA_HEADS = 16
GQA_KV_HEADS = 4
GQA_HEAD_DIM = 64
WINDOW = 128
ML_HEADS = 8
ML_HEAD_DIM = 128
ML_D = 1024
PEER_HEADS = 8
PEER_NKEYS = 128
PEER_N = PEER_NKEYS * PEER_NKEYS
PEER_TOPK = 16

LANES = 128
SUBLANES = 8
VMEM_LIMIT = 56 * 2**20

L0_COLS = 3840
L1_COLS = 5760


def _cp(n_grid, flags=None):
    return pltpu.CompilerParams(dimension_semantics=("arbitrary",) * n_grid, vmem_limit_bytes=VMEM_LIMIT, flags=flags)


def _silu(x):
    return x * jax.nn.sigmoid(x)


def _softplus(x):
    return jnp.maximum(x, 0.0) + jnp.log1p(jnp.exp(-jnp.abs(x)))


def _rms(x):
    return x * lax.rsqrt(jnp.mean(x * x, axis=-1, keepdims=True) + EPS)


def _dot(a, b):
    return jnp.dot(a, b, preferred_element_type=F32)


def _dot_nt(a, b):
    return lax.dot_general(a, b, (((1,), (1,)), ((), ())), preferred_element_type=F32)


def _dot_tn(a, b):
    return lax.dot_general(a, b, (((0,), (0,)), ((), ())), preferred_element_type=F32)


def _dot_f32(a, b):
    return jnp.dot(a, b, preferred_element_type=F32, precision=lax.Precision.HIGHEST)


def _full(shape):
    nd = len(shape)
    return pl.BlockSpec(shape, lambda *_: (0,) * nd)


def _mod_spec(chunk, tb, mod_base, rows_per_cond):
    def index(i, *_):
        return (mod_base + (i * tb) // rows_per_cond, 0, chunk)
    return pl.BlockSpec((1, 1, D_MODEL), index)


def _ada_kernel(c_ref, w_ref, b_ref, o_ref):
    s = _silu(c_ref[...])
    o_ref[...] = _dot(s.astype(BF16), w_ref[...].astype(BF16)) + b_ref[...]


def _ada(cond8, w, b):
    n = w.shape[1]
    tn = 1024
    return pl.pallas_call(
        _ada_kernel,
        grid=(n // tn,),
        in_specs=[_full((8, D_MODEL)), pl.BlockSpec((D_MODEL, tn), lambda j: (0, j)),
                  pl.BlockSpec((1, tn), lambda j: (0, j))],
        out_specs=pl.BlockSpec((8, tn), lambda j: (0, j)),
        out_shape=jax.ShapeDtypeStruct((8, n), F32),
        compiler_params=_cp(1), name="ada_table",
    )(cond8, w, b.reshape(1, n))


def _proj_kernel(x_ref, g_ref, sc_ref, sh_ref, w_ref, o_ref, h_ref):
    @pl.when(pl.program_id(1) == 0)
    def _():
        h = _rms(x_ref[...]) * g_ref[...] * (1.0 + sc_ref[0]) + sh_ref[0]
        h_ref[...] = h.astype(BF16)
    o_ref[...] = _dot(h_ref[...], w_ref[...])


def _proj(x, g, mod, w, *, tn, mod_base, rows_per_cond):
    n, ncol = x.shape[0], w.shape[1]
    tb = 1024
    return pl.pallas_call(
        _proj_kernel,
        grid=(n // tb, ncol // tn),
        in_specs=[pl.BlockSpec((tb, D_MODEL), lambda i, j: (i, 0)), _full((1, D_MODEL)),
                  _mod_spec(1, tb, mod_base, rows_per_cond), _mod_spec(0, tb, mod_base, rows_per_cond),
                  pl.BlockSpec((D_MODEL, tn), lambda i, j: (0, j))],
        out_specs=pl.BlockSpec((tb, tn), lambda i, j: (i, j)),
        out_shape=jax.ShapeDtypeStruct((n, ncol), F32),
        scratch_shapes=[pltpu.VMEM((tb, D_MODEL), BF16)],
        compiler_params=_cp(2), name="norm_mod_proj",
    )(x, g.reshape(1, D_MODEL), mod, mod, w)


def _outproj_kernel(ya_ref, yb_ref, w_ref, x_ref, g1_ref, o_ref):
    half = ya_ref.shape[1]
    y = _dot(ya_ref[...].astype(BF16), w_ref[0:half, :]) + _dot(yb_ref[...].astype(BF16), w_ref[half:, :])
    o_ref[...] = x_ref[...] + g1_ref[0] * y


def _outproj(ya, yb, w, x, mod, *, mod_base, rows_per_cond):
    n = x.shape[0]
    tb = 512
    return pl.pallas_call(
        _outproj_kernel,
        grid=(n // tb,),
        in_specs=[pl.BlockSpec((tb, ya.shape[1]), lambda i: (i, 0)), pl.BlockSpec((tb, yb.shape[1]), lambda i: (i, 0)),
                  _full(w.shape), pl.BlockSpec((tb, D_MODEL), lambda i: (i, 0)),
                  _mod_spec(2, tb, mod_base, rows_per_cond)],
        out_specs=pl.BlockSpec((tb, D_MODEL), lambda i: (i, 0)),
        out_shape=jax.ShapeDtypeStruct((n, D_MODEL), F32),
        compiler_params=_cp(1), name="out_proj_residual",
    )(ya, yb, w, x, mod)


ROUTE_TB = 512
_CAND_KEEP = (16, 8, 5, 4, 3, 2, 2, 2)


def _sort_pairs(n):
    pairs = []

    def merge(lo, hi, r):
        step = r * 2
        if step < hi - lo:
            merge(lo, hi, step)
            merge(lo + r, hi, step)
            pairs.extend((i, i + r) for i in range(lo + r, hi - r, step))
        else:
            pairs.append((lo, lo + r))

    def sort(lo, hi):
        if hi - lo >= 1:
            mid = lo + (hi - lo) // 2
            sort(lo, mid)
            sort(mid + 1, hi)
            merge(lo, hi, 1)

    sort(0, 15)
    return [(i, j) for i, j in pairs if j < n]


def _top_values(x, count):
    groups = [x[r:r + SUBLANES] for r in range(0, x.shape[0], SUBLANES)]
    n = len(groups)
    for i, j in _sort_pairs(n):
        groups[i], groups[j] = jnp.maximum(groups[i], groups[j]), jnp.minimum(groups[i], groups[j])
    sub = lax.broadcasted_iota(jnp.int32, groups[0].shape, 0)
    tops = []
    for t in range(count):
        head = groups[0]
        m = jnp.max(head, axis=0, keepdims=True)
        tops.append(m)
        if t == count - 1:
            break
        first = jnp.min(jnp.where(head == m, sub, SUBLANES), axis=0, keepdims=True)
        popped = sub == first
        depth = min(count - 1 - t, n)
        for r in range(depth):
            below = groups[r + 1] if r + 1 < n else NEG_INF
            groups[r] = jnp.where(popped, below, groups[r])
    return tops


def _route_kernel(x_ref, g_ref, sc_ref, sh_ref, wqt_ref, keys_ref,
                  hb_ref, th_ref, s2_ref, e2_ref, c1_ref, qt_s, top_s):
    tb = x_ref.shape[0]
    h = _rms(x_ref[...]) * g_ref[...] * (1.0 + sc_ref[0]) + sh_ref[0]
    hb = h.astype(BF16)
    hb_ref[...] = hb
    qt_s[...] = _dot_nt(wqt_ref[...], hb).astype(BF16)

    iota_8 = lax.broadcasted_iota(jnp.int32, (SUBLANES, tb), 0)

    def head(hd, _):
        scores = []
        for c in range(2):
            r0 = pl.multiple_of((hd * 2 + c) * PEER_NKEYS, PEER_NKEYS)
            s = _dot(keys_ref[hd * 2 + c], qt_s[pl.ds(r0, PEER_NKEYS), :])
            scores.append(s)
            for it, m in enumerate(_top_values(s, PEER_TOPK)):
                top_s[c, it:it + 1, :] = m
        a1 = top_s[0]
        a2 = top_s[1]
        pieces = [a1[0:1] + a2, a1[1:2] + a2[0:8]]
        for i in range(2, 8):
            pieces.append(jnp.where(iota_8 < _CAND_KEEP[i], a1[i:i + 1] + a2[0:8], NEG_INF))
        pieces.append(a1[8:16] + a2[0:1])
        cand = jnp.concatenate(pieces, axis=0)
        tau = _top_values(cand, PEER_TOPK)[-1]
        top = a1[0:1] + a2[0:1]
        z = jnp.sum(jnp.where(cand >= tau, jnp.exp(cand - top), 0.0), axis=0, keepdims=True)
        e2 = jnp.exp(scores[1] - a2[0:1])
        c1 = jnp.exp(scores[0] - a1[0:1]) / z
        theta = jnp.full(scores[0].shape, jnp.inf, F32)
        for q in range(PEER_TOPK):
            a2q = a2[q:q + 1]
            phi = jnp.min(jnp.where(a1 + a2q >= tau, a1, jnp.inf), axis=0, keepdims=True)
            theta = jnp.where(scores[0] >= phi, a2q, theta)
        for t in range(tb // LANES):
            ts = slice(t * LANES, (t + 1) * LANES)
            th_ref[hd, t] = theta[:, ts]
            s2_ref[hd, t] = scores[1][:, ts]
            e2_ref[hd, t] = e2[:, ts]
            c1_ref[hd, t] = c1[:, ts]
        return 0

    lax.fori_loop(0, PEER_HEADS, head, 0)


def _route(x, g, mod, wqt, keys, *, mod_base, rows_per_cond):
    n = x.shape[0]
    tb = ROUTE_TB
    ntg = tb // LANES
    hk = (PEER_HEADS, n // LANES, PEER_NKEYS, LANES)
    rspec = pl.BlockSpec((PEER_HEADS, ntg, PEER_NKEYS, LANES), lambda i: (0, i, 0, 0))
    return pl.pallas_call(
        _route_kernel,
        grid=(n // tb,),
        in_specs=[pl.BlockSpec((tb, D_MODEL), lambda i: (i, 0)), _full((1, D_MODEL)),
                  _mod_spec(4, tb, mod_base, rows_per_cond), _mod_spec(3, tb, mod_base, rows_per_cond),
                  _full(wqt.shape), _full(keys.shape)],
        out_specs=[pl.BlockSpec((tb, D_MODEL), lambda i: (i, 0)), rspec, rspec, rspec, rspec],
        out_shape=[jax.ShapeDtypeStruct((n, D_MODEL), BF16)] + [jax.ShapeDtypeStruct(hk, F32)] * 4,
        scratch_shapes=[pltpu.VMEM((PEER_HEADS * 2 * PEER_NKEYS, tb), BF16), pltpu.VMEM((2, PEER_TOPK, tb), F32)],
        compiler_params=_cp(1), name="peer_route",
    )(x, g.reshape(1, D_MODEL), mod, mod, wqt, keys)


PEER_TB = 512
PEER_EC = 1024
PEER_SUB = PEER_EC // 2


def _experts_kernel(hb_ref, th_ref, s2_ref, e2_ref, c1_ref, u0_ref, ub_ref, un_ref, vp_ref, va_ref, vl_ref,
                    x_ref, g2_ref, fg_ref, o_ref, acc_s, st0_s, st1_s, a0_s, a1_s, *, final_norm):
    e = pl.program_id(1)
    tb = hb_ref.shape[0]
    ntg = tb // LANES
    rows_per_sub = PEER_SUB // PEER_NKEYS

    def scores(u_blk, st_s, h):
        res = _dot_nt(u_blk[...], hb_ref[h * 2 * LANES:(h + 1) * 2 * LANES, :])
        st_s[2 * h] = res[:, :LANES]
        st_s[2 * h + 1] = res[:, LANES:]

    def values(vt_blk, act_s, h):
        act = jnp.concatenate([act_s[2 * h], act_s[2 * h + 1]], axis=1)
        hs = slice(h * 2 * LANES, (h + 1) * 2 * LANES)
        for r in range(0, D_MODEL, PEER_SUB):
            acc_s[r:r + PEER_SUB, hs] += _dot(vt_blk[r:r + PEER_SUB, :], act)

    def gates(c, tg, st_s, act_s):
        sub = 32
        for k in range(rows_per_sub):
            row = c * rows_per_sub + k
            for j0 in range(0, PEER_NKEYS, sub):
                js = slice(j0, j0 + sub)
                ks = slice(k * PEER_NKEYS + j0, k * PEER_NKEYS + j0 + sub)
                w = jnp.zeros((sub, LANES), F32)
                for hd in range(PEER_HEADS):
                    chosen = s2_ref[hd, tg, js, :] >= th_ref[hd, tg, row:row + 1, :]
                    w = w + jnp.where(chosen, e2_ref[hd, tg, js, :], 0.0) * c1_ref[hd, tg, row:row + 1, :]
                s = st_s[tg, ks, :]
                act = 0.5 * s * (1.0 + lax.erf(s * np.float32(2.0 ** -0.5)))
                act_s[tg, ks, :] = (act * w).astype(BF16)

    @pl.when(e == 0)
    def _():
        acc_s[...] = jnp.zeros_like(acc_s)
        a1_s[...] = jnp.zeros_like(a1_s)
        scores(u0_ref, st0_s, 0)
        scores(u0_ref, st0_s, 1)

    def phase(c, st_cur, act_cur, u_next, st_next, vt_prev, act_prev):
        for h in range(2):
            scores(u_next, st_next, h)
            gates(c, 2 * h, st_cur, act_cur)
            values(vt_prev, act_prev, h)
            gates(c, 2 * h + 1, st_cur, act_cur)

    phase(0, st0_s, a0_s, ub_ref, st1_s, vp_ref, a1_s)
    phase(1, st1_s, a1_s, un_ref, st0_s, va_ref, a0_s)

    @pl.when(e == pl.num_programs(1) - 1)
    def _():
        values(vl_ref, a1_s, 0)
        values(vl_ref, a1_s, 1)
        y = x_ref[...] + g2_ref[0] * acc_s[...].T
        if final_norm:
            y = _rms(y) * fg_ref[...]
        o_ref[...] = y


def _experts(hb, theta, s2, e2, c1, u, vt, x, mod, fg, *, mod_base, rows_per_cond, final_norm):
    n = x.shape[0]
    tb = PEER_TB
    ntg = tb // LANES
    ne = PEER_N // PEER_EC
    once = dict(pipeline_mode=pl.Buffered(1))
    rspec = pl.BlockSpec((PEER_HEADS, ntg, PEER_NKEYS, LANES), lambda i, e: (0, i, 0, 0), **once)
    rows8 = pl.BlockSpec((PEER_HEADS, ntg, SUBLANES, LANES), lambda i, e: (0, i, e, 0))
    uspec = lambda index: pl.BlockSpec((PEER_SUB, D_MODEL), index)
    vspec = lambda index: pl.BlockSpec((D_MODEL, PEER_SUB), index)
    return pl.pallas_call(
        functools.partial(_experts_kernel, final_norm=final_norm),
        grid=(n // tb, ne),
        in_specs=[pl.BlockSpec((tb, D_MODEL), lambda i, e: (i, 0), **once), rows8, rspec, rspec, rows8,
                  pl.BlockSpec((PEER_SUB, D_MODEL), lambda i, e: (0, 0), **once),
                  uspec(lambda i, e: (2 * e + 1, 0)),
                  uspec(lambda i, e: (jnp.minimum(2 * e + 2, 2 * ne - 2), 0)),
                  vspec(lambda i, e: (0, jnp.maximum(2 * e - 1, 0))),
                  vspec(lambda i, e: (0, 2 * e)),
                  pl.BlockSpec((D_MODEL, PEER_SUB), lambda i, e: (0, 2 * ne - 1), **once),
                  pl.BlockSpec((tb, D_MODEL), lambda i, e: (i, 0), **once),
                  _mod_spec(5, tb, mod_base, rows_per_cond), _full((1, D_MODEL))],
        out_specs=pl.BlockSpec((tb, D_MODEL), lambda i, e: (i, 0)),
        out_shape=jax.ShapeDtypeStruct((n, D_MODEL), F32),
        scratch_shapes=[pltpu.VMEM((D_MODEL, tb), F32), pltpu.VMEM((ntg, PEER_SUB, LANES), F32),
                        pltpu.VMEM((ntg, PEER_SUB, LANES), F32), pltpu.VMEM((ntg, PEER_SUB, LANES), BF16),
                        pltpu.VMEM((ntg, PEER_SUB, LANES), BF16)],
        compiler_params=_cp(2), name="peer_experts",
    )(hb, theta, s2, e2, c1, u, u, u, vt, vt, vt, x, mod, fg.reshape(1, D_MODEL))


def _peer(x, g, mod, wqt, keys, u, vt, fg, *, mod_base, rows_per_cond, final_norm):
    hb, theta, s2, e2, c1 = _route(x, g, mod, wqt, keys, mod_base=mod_base, rows_per_cond=rows_per_cond)
    return _experts(hb, theta, s2, e2, c1, u, vt, x, mod, fg, mod_base=mod_base, rows_per_cond=rows_per_cond,
                    final_norm=final_norm)


def _chunk_index(s, nc):
    return jnp.where(s < nc, s, 2 * nc - 1 - s)


def _scan_group(nseq, largest):
    return max(g for g in (2, 4, 8) if g <= largest and nseq % g == 0)


def _scan_specs(nc):
    def cur(col):
        return lambda b, s: (b, _chunk_index(s, nc), col)

    def prev(col):
        return lambda b, s: (b, jnp.maximum(_chunk_index(s, nc) * 8 - 1, 0), col)

    def nxt(col):
        return lambda b, s: (b, jnp.minimum(_chunk_index(s, nc) * 8 + 8, nc * 8 - 1), col)

    def out(col):
        return lambda b, s: (b, jnp.where(s < nc, nc - 1, 2 * nc - 1 - s), col)
    return cur, prev, nxt, out


def _conv_silu(prev, cur, nxt, w_ref, b_ref, first, last):
    win = jnp.concatenate([jnp.where(first, 0.0, prev), cur, jnp.where(last, 0.0, nxt)], axis=1)
    acc = b_ref[...] + win[:, 6:70, :] * w_ref[0:1, :]
    for k in range(1, CONV_W):
        acc = acc + win[:, 6 + k:70 + k, :] * w_ref[k:k + 1, :]
    return _silu(acc)


def _tri(bwd, G):
    r = lax.broadcasted_iota(jnp.int32, (G, CHUNK, CHUNK), 1)
    c = lax.broadcasted_iota(jnp.int32, (G, CHUNK, CHUNK), 2)
    lower = (c <= r).astype(F32)
    upper = (c >= r).astype(F32)
    return jnp.where(bwd, upper, lower), jnp.where(bwd, lower, upper)


def _bdot(a, b, contract, precision=None):
    return lax.dot_general(a, b, (((contract[0],), (contract[1],)), ((0,), (0,))), preferred_element_type=F32,
                           precision=precision)


def _ssd_kernel(z_ref, xp_ref, xc_ref, xn_ref, bp_ref, bc_ref, bn_ref, sm_ref, dtt_ref,
                cwx_ref, cbx_ref, cwb_ref, cbb_ref, arow_ref, acol_ref, brow_ref, bcol_ref, drow_ref, ng_ref,
                st0_ref, y_ref, st_ref, yf_s, h_s, *, nc):
    s = pl.program_id(1)
    bwd = s >= nc
    cidx = _chunk_index(s, nc)
    first = cidx == 0
    last = cidx == nc - 1
    nh = SSD_HEADS

    @pl.when(s == 0)
    def _():
        h_s[...] = st0_ref[:, 0]

    @pl.when(s == nc)
    def _():
        st_ref[:, 0] = h_s[...]
        h_s[...] = st0_ref[:, 1]

    G = z_ref.shape[0]
    tsel, tsel_t = _tri(bwd, G)
    mask = tsel > 0.0
    r0 = pl.multiple_of(cidx * CHUNK, CHUNK)
    hi = lax.Precision.HIGHEST

    xs = _conv_silu(xp_ref[...], xc_ref[...], xn_ref[...], cwx_ref, cbx_ref, first, last)
    bcv = _conv_silu(bp_ref[...], bc_ref[...], bn_ref[...], cwb_ref, cbb_ref, first, last)
    dt_all = _softplus(sm_ref[:, :, 64:96] + brow_ref[...])
    dtt_all = _softplus(dtt_ref[:, 0] + bcol_ref[...])
    a_all = dt_all * (-jnp.exp(arow_ref[...]))
    at_all = dtt_all * (-jnp.exp(acol_ref[...]))
    dt = jnp.where(bwd, dt_all[:, :, nh:2 * nh], dt_all[:, :, 0:nh])
    a = jnp.where(bwd, a_all[:, :, nh:2 * nh], a_all[:, :, 0:nh])
    at = jnp.where(bwd, at_all[:, nh:2 * nh, :], at_all[:, 0:nh, :])
    cum = _bdot(tsel, a, (2, 1), hi)
    cum_t = _bdot(at, tsel_t, (2, 1), hi)
    tot = jnp.where(bwd, cum[:, 0:1, :], cum[:, CHUNK - 1:CHUNK, :])

    ys = []
    for g in range(2):
        bg = bcv[:, :, g * SSD_STATE:(g + 1) * SSD_STATE]
        cg = bcv[:, :, 256 + g * SSD_STATE:256 + (g + 1) * SSD_STATE]
        cg16 = cg.astype(BF16)
        cb = _bdot(cg16, bg.astype(BF16), (2, 2))
        for hh in range(nh // 2):
            h = g * (nh // 2) + hh
            cq = cum[:, :, h:h + 1]
            decay = jnp.exp(jnp.where(mask, cq - cum_t[:, h:h + 1, :], NEG_INF))
            xd = xs[:, :, h * SSD_HEAD_DIM:(h + 1) * SSD_HEAD_DIM] * dt[:, :, h:h + 1]
            xd16 = xd.astype(BF16)
            hprev = h_s[:, h]
            y = (_bdot((cb * decay).astype(BF16), xd16, (2, 1))
                 + _bdot(cg16, hprev.astype(BF16), (2, 2)) * jnp.exp(cq))
            ys.append(y)
            th = tot[:, :, h:h + 1]
            bw = bg * jnp.exp(th - cq)
            h_s[:, h] = hprev * jnp.exp(th) + _bdot(xd16, bw.astype(BF16), (1, 1))
    y = jnp.concatenate(ys, axis=2)

    @pl.when(jnp.logical_not(bwd))
    def _():
        yf_s[:, pl.ds(r0, CHUNK), :] = y

    @pl.when(bwd)
    def _():
        yt = (yf_s[:, pl.ds(r0, CHUNK), :] + y + drow_ref[...] * xs) * _silu(z_ref[...])
        y_ref[...] = _rms(yt) * ng_ref[...]

    @pl.when(s == 2 * nc - 1)
    def _():
        st_ref[:, 1] = h_s[...]


def _ssd(p0, dtt, cwx, cbx, cwb, cbb, a_log, dt_bias, d_skip, norm_g, st0, *, L):
    nseq, nc = p0.shape[0], L // CHUNK
    G = _scan_group(nseq, 4)
    cur, prev, nxt, out = _scan_specs(nc)
    st_shape = (G, 2, SSD_HEADS, SSD_HEAD_DIM, SSD_STATE)
    per_seq = st0 is not None
    if st0 is None:
        st0 = jnp.zeros(st_shape, F32)
    row = lambda v: v.reshape(1, -1)
    col = lambda v: v.reshape(-1, 1)
    small = [cwx, row(cbx), cwb, row(cbb), row(a_log), col(a_log), row(dt_bias), col(dt_bias),
             row(jnp.repeat(d_skip, SSD_HEAD_DIM)), row(norm_g)]
    blk = lambda rows, width, index: pl.BlockSpec((G, rows, width), index)
    return pl.pallas_call(
        functools.partial(_ssd_kernel, nc=nc),
        grid=(nseq // G, 2 * nc),
        in_specs=[blk(CHUNK, 1024, cur(0)),
                  blk(8, 1024, prev(1)), blk(CHUNK, 1024, cur(1)), blk(8, 1024, nxt(1)),
                  blk(8, 512, prev(4)), blk(CHUNK, 512, cur(4)), blk(8, 512, nxt(4)),
                  blk(CHUNK, LANES, cur(28)),
                  pl.BlockSpec((G, 1, 32, CHUNK), lambda b, s: (b, _chunk_index(s, nc), 0, 0))]
        + [_full(v.shape) for v in small]
        + [pl.BlockSpec(st_shape, lambda b, s: (b if per_seq else 0, 0, 0, 0, 0))],
        out_specs=[blk(CHUNK, SSD_D_INNER, out(0)),
                   pl.BlockSpec(st_shape, lambda b, s: (b, 0, 0, 0, 0))],
        out_shape=[jax.ShapeDtypeStruct((nseq, L, SSD_D_INNER), F32),
                   jax.ShapeDtypeStruct((nseq,) + st_shape[1:], F32)],
        scratch_shapes=[pltpu.VMEM((G, L, SSD_D_INNER), F32), pltpu.VMEM(st_shape[:1] + st_shape[2:], F32)],
        compiler_params=_cp(2), name="ssd_mixer",
    )(p0, p0, p0, p0, p0, p0, p0, p0, dtt, *small, st0)


def _rope_tables(L):
    t = jnp.arange(L)
    r = (t // GRID_W).astype(F32)
    c = (t % GRID_W).astype(F32)
    nf = 16
    inv = ROPE_BASE ** (-jnp.arange(nf, dtype=F32) / nf)
    ang = jnp.concatenate([r[:, None] * inv, c[:, None] * inv], axis=-1)
    ang = jnp.concatenate([ang, ang], axis=-1)
    return jnp.cos(ang), jnp.sin(ang)


def _rope128(x, cos, sin):
    lane = lax.broadcasted_iota(jnp.int32, x.shape, 1)
    rot = jnp.where(lane % 64 < 32, -pltpu.roll(x, 96, 1), pltpu.roll(x, 32, 1))
    return x * cos + rot * sin


def _pad_lanes(x):
    return jnp.concatenate([x, jnp.zeros_like(x)], axis=1)


MLA_QB = 256


def _mla_kernel(*refs, L, Lc, rope, emit):
    refs = list(refs)
    qa_ref, kva_ref, sm_ref, qg_ref, kvg_ref, wqn_ref, wqr_ref, wkk_ref, wkv_ref = refs[:9]
    pos = 9
    if Lc:
        cckv_ref, ckpe_ref = refs[pos:pos + 2]
        pos += 2
    if rope:
        cos_ref, sin_ref = refs[pos:pos + 2]
        pos += 2
    y_ref = refs[pos]
    pos += 1
    if emit:
        ckv_ref = refs[pos]
        pos += 1
    qn_s, qr_s, kn_s, v_s, kpe_s = refs[pos:]
    qb = pl.program_id(1)
    scale = np.float32((MLA_NOPE + MLA_ROPE) ** -0.5)

    @pl.when(qb == 0)
    def _():
        ckv = _rms(kva_ref[...]) * kvg_ref[...]
        if emit:
            ckv_ref[...] = ckv
        kpe = _pad_lanes(sm_ref[:, 0:MLA_ROPE])
        if rope:
            kpe = _rope128(kpe, cos_ref[...], sin_ref[...])
        if Lc:
            ckv = jnp.concatenate([cckv_ref[0], ckv], axis=0)
            kpe = jnp.concatenate([_pad_lanes(ckpe_ref[0]), kpe], axis=0)
        c16 = ckv.astype(BF16)
        kn_s[...] = _dot(c16, wkk_ref[...]).astype(BF16)
        v_s[...] = _dot(c16, wkv_ref[...]).astype(BF16)
        kpe_s[...] = kpe.astype(BF16)

    qn = (_rms(qa_ref[...]) * qg_ref[...]).astype(BF16)
    qn_s[...] = _dot(qn, wqn_ref[...]).astype(BF16)
    qr = _dot(qn, wqr_ref[...])
    if rope:
        q0 = pl.multiple_of(qb * MLA_QB, MLA_QB)
        cos = cos_ref[pl.ds(q0, MLA_QB), :]
        sin = sin_ref[pl.ds(q0, MLA_QB), :]
        for h in range(MLA_HEADS):
            qr_s[:, h * LANES:(h + 1) * LANES] = _rope128(qr[:, h * LANES:(h + 1) * LANES], cos, sin).astype(BF16)
    else:
        qr_s[...] = qr.astype(BF16)

    def head(h, _):
        c0 = h * LANES if isinstance(h, int) else pl.multiple_of(h * LANES, LANES)
        s = (_dot_nt(qn_s[:, pl.ds(c0, LANES)], kn_s[:, pl.ds(c0, LANES)])
             + _dot_nt(qr_s[:, pl.ds(c0, LANES)], kpe_s[...])) * scale
        p = jnp.exp(s - jnp.max(s, axis=-1, keepdims=True))
        p = p / jnp.sum(p, axis=-1, keepdims=True)
        y_ref[:, pl.ds(c0, LANES)] = _dot(p.astype(BF16), v_s[:, pl.ds(c0, LANES)])
        return 0

    if L + Lc <= 512:
        for h in range(MLA_HEADS):
            head(h, 0)
    else:
        lax.fori_loop(0, MLA_HEADS, head, 0)


def _mla(p0, qg, kvg, wqn, wqr, wkk, wkv, cache_ckv, cache_kpe, rope_tabs, *, L):
    n = p0.shape[0]
    nseq, nqb = n // L, L // MLA_QB
    Lc = 0 if cache_ckv is None else cache_ckv.shape[1]
    rope = rope_tabs is not None
    emit = cache_ckv is None
    Lk = L + Lc
    args = [p0, p0, p0, qg.reshape(1, -1), kvg.reshape(1, -1), wqn, wqr, wkk, wkv]
    specs = [pl.BlockSpec((MLA_QB, MLA_RANK), lambda b, q: (b * nqb + q, 5)),
             pl.BlockSpec((L, MLA_RANK), lambda b, q: (b, 6)),
             pl.BlockSpec((L, LANES), lambda b, q: (b, 28)),
             _full((1, MLA_RANK)), _full((1, MLA_RANK)), _full(wqn.shape), _full(wqr.shape), _full(wkk.shape),
             _full(wkv.shape)]
    if Lc:
        args += [cache_ckv, cache_kpe]
        specs += [pl.BlockSpec((1, Lc, MLA_RANK), lambda b, q: (b, 0, 0)),
                  pl.BlockSpec((1, Lc, MLA_ROPE), lambda b, q: (b, 0, 0))]
    if rope:
        args += list(rope_tabs)
        specs += [_full((L, LANES)), _full((L, LANES))]
    out_specs = [pl.BlockSpec((MLA_QB, 1024), lambda b, q: (b * nqb + q, 0))]
    out_shape = [jax.ShapeDtypeStruct((n, 1024), F32)]
    if emit:
        out_specs.append(pl.BlockSpec((L, MLA_RANK), lambda b, q: (b, 0)))
        out_shape.append(jax.ShapeDtypeStruct((n, MLA_RANK), F32))
    return pl.pallas_call(
        functools.partial(_mla_kernel, L=L, Lc=Lc, rope=rope, emit=emit),
        grid=(nseq, nqb), in_specs=specs, out_specs=out_specs, out_shape=out_shape,
        scratch_shapes=[pltpu.VMEM((MLA_QB, 1024), BF16), pltpu.VMEM((MLA_QB, 1024), BF16),
                        pltpu.VMEM((Lk, 1024), BF16), pltpu.VMEM((Lk, 1024), BF16), pltpu.VMEM((Lk, LANES), BF16)],
        compiler_params=_cp(2), name="mla_attention",
    )(*args)


def _gqa_ctx_kernel(q_ref, k_ref, v_ref, sink_ref, y_ref):
    q = q_ref[...]
    k = k_ref[...].astype(BF16)
    v = v_ref[...].astype(BF16)
    scale = np.float32(GQA_HEAD_DIM ** -0.5)
    group = GQA_HEADS // GQA_KV_HEADS
    outs = []
    for h in range(GQA_HEADS):
        kh = h // group
        ks = slice(kh * GQA_HEAD_DIM, (kh + 1) * GQA_HEAD_DIM)
        s = _dot_nt(q[:, h * GQA_HEAD_DIM:(h + 1) * GQA_HEAD_DIM].astype(BF16), k[:, ks]) * scale
        m = jnp.maximum(jnp.max(s, axis=-1, keepdims=True), sink_ref[h])
        p = jnp.exp(s - m)
        p = p / (jnp.sum(p, axis=-1, keepdims=True) + jnp.exp(sink_ref[h] - m))
        outs.append(_dot(p.astype(BF16), v[:, ks]))
    y_ref[...] = jnp.concatenate(outs, axis=1)


def _gqa_ctx(p1, sink, *, L):
    n = p1.shape[0]
    return pl.pallas_call(
        _gqa_ctx_kernel,
        grid=(n // L,),
        in_specs=[pl.BlockSpec((L, 1024), lambda b: (b, 2)), pl.BlockSpec((L, 256), lambda b: (b, 20)),
                  pl.BlockSpec((L, 256), lambda b: (b, 21)), pl.BlockSpec(memory_space=pltpu.SMEM)],
        out_specs=pl.BlockSpec((L, 1024), lambda b: (b, 0)),
        out_shape=jax.ShapeDtypeStruct((n, 1024), F32),
        compiler_params=_cp(1), name="gqa_context",
    )(p1, p1, p1, sink)


GQA_QB = 128
GQA_SPAN = GQA_QB + 2 * WINDOW


def _gqa_win_kernel(q_ref, k_ref, v_ref, ck_ref, cv_ref, cos_ref, sin_ref, sink_ref, y_ref, kp_s, vp_s, *, L):
    qb = pl.program_id(1)
    scale = np.float32(GQA_HEAD_DIM ** -0.5)
    group = GQA_HEADS // GQA_KV_HEADS

    @pl.when(qb == 0)
    def _():
        zeros = jnp.zeros((WINDOW, 256), BF16)
        kp_s[0:WINDOW, :] = zeros
        kp_s[WINDOW + L:, :] = zeros
        vp_s[0:WINDOW, :] = zeros
        vp_s[WINDOW + L:, :] = zeros
        for j in range(2):
            cs = slice(j * LANES, (j + 1) * LANES)
            kp_s[WINDOW:WINDOW + L, cs] = _rope128(k_ref[:, cs], cos_ref[...], sin_ref[...]).astype(BF16)
        vp_s[WINDOW:WINDOW + L, :] = v_ref[...].astype(BF16)

    q0 = pl.multiple_of(qb * GQA_QB, GQA_QB)
    cos = cos_ref[pl.ds(q0, GQA_QB), :]
    sin = sin_ref[pl.ds(q0, GQA_QB), :]
    kw = kp_s[pl.ds(q0, GQA_SPAN), :]
    vw = vp_s[pl.ds(q0, GQA_SPAN), :]
    kc = ck_ref[0].astype(BF16)
    vc = cv_ref[0].astype(BF16)
    qpos = q0 + lax.broadcasted_iota(jnp.int32, (GQA_QB, GQA_SPAN), 0)
    kpos = q0 - WINDOW + lax.broadcasted_iota(jnp.int32, (GQA_QB, GQA_SPAN), 1)
    bias = jnp.where(kpos < 0, NEG_INF, jnp.where(kpos >= L, NEG_INF, jnp.where(jnp.abs(qpos - kpos) <= WINDOW, 0.0, NEG_INF)))
    outs = []
    for j in range(GQA_HEADS * GQA_HEAD_DIM // LANES):
        qj = _rope128(q_ref[:, j * LANES:(j + 1) * LANES], cos, sin).astype(BF16)
        for half in range(2):
            h = 2 * j + half
            kh = h // group
            ks = slice(kh * GQA_HEAD_DIM, (kh + 1) * GQA_HEAD_DIM)
            qh = qj[:, half * GQA_HEAD_DIM:(half + 1) * GQA_HEAD_DIM]
            s_loc = _dot_nt(qh, kw[:, ks]) * scale + bias
            s_ctx = _dot_nt(qh, kc[:, ks]) * scale
            m = jnp.maximum(jnp.maximum(jnp.max(s_loc, axis=-1, keepdims=True), jnp.max(s_ctx, axis=-1, keepdims=True)),
                            sink_ref[h])
            p_loc = jnp.exp(s_loc - m)
            p_ctx = jnp.exp(s_ctx - m)
            inv = 1.0 / (jnp.sum(p_loc, axis=-1, keepdims=True) + jnp.sum(p_ctx, axis=-1, keepdims=True)
                         + jnp.exp(sink_ref[h] - m))
            outs.append(_dot((p_loc * inv).astype(BF16), vw[:, ks]) + _dot((p_ctx * inv).astype(BF16), vc[:, ks]))
    y_ref[...] = jnp.concatenate(outs, axis=1)


def _gqa_win(p1, cache_k, cache_v, cos, sin, sink, *, L):
    n = p1.shape[0]
    nseq, nqb = n // L, L // GQA_QB
    Lc = cache_k.shape[1]
    return pl.pallas_call(
        functools.partial(_gqa_win_kernel, L=L),
        grid=(nseq, nqb),
        in_specs=[pl.BlockSpec((GQA_QB, 1024), lambda b, q: (b * nqb + q, 2)),
                  pl.BlockSpec((L, 256), lambda b, q: (b, 20)), pl.BlockSpec((L, 256), lambda b, q: (b, 21)),
                  pl.BlockSpec((1, Lc, 256), lambda b, q: (b, 0, 0)), pl.BlockSpec((1, Lc, 256), lambda b, q: (b, 0, 0)),
                  _full((L, LANES)), _full((L, LANES)), pl.BlockSpec(memory_space=pltpu.SMEM)],
        out_specs=pl.BlockSpec((GQA_QB, 1024), lambda b, q: (b * nqb + q, 0)),
        out_shape=jax.ShapeDtypeStruct((n, 1024), F32),
        scratch_shapes=[pltpu.VMEM((L + 2 * WINDOW, 256), BF16), pltpu.VMEM((L + 2 * WINDOW, 256), BF16)],
        compiler_params=_cp(2), name="gqa_window",
    )(p1, p1, p1, cache_k, cache_v, cos, sin, sink)


def _mlstm_kernel(qp_ref, qc_ref, qn_ref, v_ref, o_ref, sm_ref, gt_ref, cw_ref, cb_ref, brow_ref, bcol_ref, ng_ref,
                  c0_ref, n0_ref, m0_ref, y_ref, cst_ref, nst_ref, mst_ref, hf_s, c_s, n_s, m_s, *, nc):
    s = pl.program_id(1)
    bwd = s >= nc
    cidx = _chunk_index(s, nc)
    nh = ML_HEADS
    hd = ML_HEAD_DIM

    @pl.when(s == 0)
    def _():
        c_s[...] = c0_ref[:, 0]
        n_s[...] = n0_ref[:, 0]
        m_s[...] = m0_ref[:, 0]

    @pl.when(s == nc)
    def _():
        cst_ref[:, 0] = c_s[...]
        nst_ref[:, 0] = n_s[...]
        mst_ref[:, 0] = m_s[...]
        c_s[...] = c0_ref[:, 1]
        n_s[...] = n0_ref[:, 1]
        m_s[...] = m0_ref[:, 1]

    G = qc_ref.shape[0]
    tsel, tsel_t = _tri(bwd, G)
    mask = tsel > 0.0
    kscale = np.float32(hd ** -0.5)
    r0 = pl.multiple_of(cidx * CHUNK, CHUNK)
    hi = lax.Precision.HIGHEST

    qk = _conv_silu(qp_ref[...], qc_ref[...], qn_ref[...], cw_ref, cb_ref, cidx == 0, cidx == nc - 1)
    gates = sm_ref[:, :, 0:4 * nh] + brow_ref[...]
    gates_t = gt_ref[:, 0] + bcol_ref[...]
    li = jnp.where(bwd, gates[:, :, nh:2 * nh], gates[:, :, 0:nh])
    li_t = jnp.where(bwd, gates_t[:, nh:2 * nh, :], gates_t[:, 0:nh, :])
    lf = -_softplus(-jnp.where(bwd, gates[:, :, 3 * nh:4 * nh], gates[:, :, 2 * nh:3 * nh]))
    lf_t = -_softplus(-jnp.where(bwd, gates_t[:, 3 * nh:4 * nh, :], gates_t[:, 2 * nh:3 * nh, :]))
    bc = _bdot(tsel, lf, (2, 1), hi)
    bc_t = _bdot(lf_t, tsel_t, (2, 1), hi)
    tot = jnp.where(bwd, bc[:, 0:1, :], bc[:, CHUNK - 1:CHUNK, :])

    hs = []
    for h in range(nh):
        cs = slice(h * hd, (h + 1) * hd)
        q = qk[:, :, cs]
        k = qk[:, :, ML_D + h * hd:ML_D + (h + 1) * hd] * kscale
        q16, k16 = q.astype(BF16), k.astype(BF16)
        v = v_ref[:, :, cs]
        bq = bc[:, :, h:h + 1]
        m_prev = m_s[:, h:h + 1, 0:1]
        log_d = jnp.where(mask, bq - bc_t[:, h:h + 1, :] + li_t[:, h:h + 1, :], NEG_INF)
        log_inter = bq + m_prev
        m_out = jnp.maximum(log_inter, jnp.max(log_d, axis=2, keepdims=True))
        sd = _bdot(q16, k16, (2, 2)) * jnp.exp(log_d - m_out)
        w_inter = jnp.exp(log_inter - m_out)
        c_prev = c_s[:, h]
        n_prev = n_s[:, h:h + 1, :]
        num = _bdot(sd.astype(BF16), v.astype(BF16), (2, 1)) + w_inter * _bdot(q16, c_prev.astype(BF16), (2, 2))
        den = jnp.sum(sd, axis=2, keepdims=True) + w_inter * jnp.sum(q * n_prev, axis=2, keepdims=True)
        hs.append(num / jnp.maximum(jnp.abs(den), jnp.exp(-m_out)))
        th = tot[:, :, h:h + 1]
        end_inter = th + m_prev
        end_intra = th - bq + li[:, :, h:h + 1]
        m_new = jnp.maximum(end_inter, jnp.max(end_intra, axis=1, keepdims=True))
        w_c = jnp.exp(end_inter - m_new)
        w_k = jnp.exp(end_intra - m_new)
        c_s[:, h] = w_c * c_prev + _bdot((v * w_k).astype(BF16), k16, (1, 1))
        n_s[:, h:h + 1, :] = w_c * n_prev + jnp.sum(k * w_k, axis=1, keepdims=True)
        m_s[:, h:h + 1, :] = jnp.broadcast_to(m_new, (G, 1, hd))
    hcat = jnp.concatenate(hs, axis=2)

    @pl.when(jnp.logical_not(bwd))
    def _():
        hf_s[:, pl.ds(r0, CHUNK), :] = hcat

    @pl.when(bwd)
    def _():
        tot_h = hf_s[:, pl.ds(r0, CHUNK), :] + hcat
        normed = jnp.concatenate([_rms(tot_h[:, :, h * hd:(h + 1) * hd]) for h in range(nh)], axis=2) * ng_ref[...]
        y_ref[...] = normed * jax.nn.sigmoid(o_ref[...])

    @pl.when(s == 2 * nc - 1)
    def _():
        cst_ref[:, 1] = c_s[...]
        nst_ref[:, 1] = n_s[...]
        mst_ref[:, 1] = m_s[...]


def _mlstm(p1, gt, conv_w, conv_b, ig_b, fg_b, norm_g, c0, n0, m0, *, L):
    nseq, nc = p1.shape[0], L // CHUNK
    G = _scan_group(nseq, 4)
    cur, prev, nxt, out = _scan_specs(nc)
    cshape = (G, 2, ML_HEADS, ML_HEAD_DIM, ML_HEAD_DIM)
    nshape = (G, 2, ML_HEADS, ML_HEAD_DIM)
    per_seq = c0 is not None
    if c0 is None:
        c0, n0, m0 = jnp.zeros(cshape, F32), jnp.zeros(nshape, F32), jnp.zeros(nshape, F32)
    gb = jnp.concatenate([ig_b.reshape(-1), fg_b.reshape(-1)])
    small = [conv_w, conv_b.reshape(1, -1), gb.reshape(1, -1), gb.reshape(-1, 1), norm_g.reshape(1, -1)]
    st_in = lambda shape: pl.BlockSpec(shape, lambda b, s: (b if per_seq else 0,) + (0,) * (len(shape) - 1))
    st_out = lambda shape: pl.BlockSpec(shape, lambda b, s: (b,) + (0,) * (len(shape) - 1))
    blk = lambda rows, width, index: pl.BlockSpec((G, rows, width), index)
    return pl.pallas_call(
        functools.partial(_mlstm_kernel, nc=nc),
        grid=(nseq // G, 2 * nc),
        in_specs=[blk(8, 2048, prev(0)), blk(CHUNK, 2048, cur(0)), blk(8, 2048, nxt(0)),
                  blk(CHUNK, 1024, cur(3)), blk(CHUNK, 1024, cur(4)), blk(CHUNK, LANES, cur(44)),
                  pl.BlockSpec((G, 1, 32, CHUNK), lambda b, s: (b, _chunk_index(s, nc), 0, 0))]
        + [_full(v.shape) for v in small] + [st_in(cshape), st_in(nshape), st_in(nshape)],
        out_specs=[blk(CHUNK, ML_D, out(0)), st_out(cshape), st_out(nshape), st_out(nshape)],
        out_shape=[jax.ShapeDtypeStruct((nseq, L, ML_D), F32), jax.ShapeDtypeStruct((nseq,) + cshape[1:], F32),
                   jax.ShapeDtypeStruct((nseq,) + nshape[1:], F32), jax.ShapeDtypeStruct((nseq,) + nshape[1:], F32)],
        scratch_shapes=[pltpu.VMEM((G, L, ML_D), F32), pltpu.VMEM(cshape[:1] + cshape[2:], F32),
                        pltpu.VMEM(nshape[:1] + nshape[2:], F32), pltpu.VMEM(nshape[:1] + nshape[2:], F32)],
        compiler_params=_cp(2), name="mlstm_mixer",
    )(p1, p1, p1, p1, p1, p1, gt, *small, c0, n0, m0)


def _prep_layer0(w_in, w_qb, w_kvb, w_out):
    pad = jnp.zeros((D_MODEL, L0_COLS - 3680), F32)
    w = jnp.concatenate([w_in[:, :2560], w_in[:, 2592:3680], w_in[:, 2560:2592], pad], axis=1).astype(BF16)
    q3 = w_qb.reshape(MLA_RANK, MLA_HEADS, MLA_NOPE + MLA_ROPE)
    wqn = q3[:, :, :MLA_NOPE].reshape(MLA_RANK, -1).astype(BF16)
    wqr = jnp.concatenate([q3[:, :, MLA_NOPE:], jnp.zeros((MLA_RANK, MLA_HEADS, LANES - MLA_ROPE), F32)], axis=-1)
    wqr = wqr.reshape(MLA_RANK, -1).astype(BF16)
    k3 = w_kvb.reshape(MLA_RANK, MLA_HEADS, MLA_NOPE + MLA_V)
    wkk = k3[:, :, :MLA_NOPE].reshape(MLA_RANK, -1).astype(BF16)
    wkv = k3[:, :, MLA_NOPE:].reshape(MLA_RANK, -1).astype(BF16)
    return w, wqn, wqr, wkk, wkv, w_out.astype(BF16)


def _prep_layer1(w_in, w_out):
    pad = jnp.zeros((D_MODEL, L1_COLS - 5664), F32)
    w = jnp.concatenate([w_in[:, 1536:3584], w_in[:, 0:1024], w_in[:, 3584:5632], w_in[:, 1024:1536],
                         w_in[:, 5632:5664], pad], axis=1).astype(BF16)
    return w, w_out.astype(BF16)


def _prep_peer(wq, keys, u, v):
    return (wq.T.astype(BF16), keys.reshape(PEER_HEADS * 2, PEER_NKEYS, -1).astype(BF16), u.astype(BF16),
            v.T.astype(BF16))


def _chunk_transposed(cols, L):
    return cols.reshape(-1, L // CHUNK, CHUNK, cols.shape[1]).transpose(0, 1, 3, 2)


def _trunk(x, L, mod0, mod1, w0, w1, peer0, peer1, p, ctx, rope_tabs, mod_base, rows_per_cond, fg):
    kw = dict(mod_base=mod_base, rows_per_cond=rows_per_cond)
    w_in0, wqn, wqr, wkk, wkv, w_out0 = w0
    w_in1, w_out1 = w1
    nseq = x.shape[0] // L

    p0 = _proj(x, p['l0_norm1_g'], mod0, w_in0, tn=1280, **kw)
    cw, cb = p['l0_ssd_conv_w'], p['l0_ssd_conv_b']
    y_ssd, ssd_state = _ssd(p0.reshape(nseq, L, -1), _chunk_transposed(p0[:, 3648:3680], L), cw[:, :1024], cb[:1024],
                            cw[:, 1024:], cb[1024:], p['l0_ssd_A_log'], p['l0_ssd_dt_bias'], p['l0_ssd_D'],
                            p['l0_ssd_norm_g'], None if ctx is None else ctx[0], L=L)
    y_ssd = y_ssd.reshape(nseq * L, -1)
    mla_tabs = None
    if rope_tabs is not None:
        z64 = jnp.zeros_like(rope_tabs[0])
        mla_tabs = (jnp.concatenate([rope_tabs[0], z64], axis=1), jnp.concatenate([rope_tabs[1], z64], axis=1))
    mla_out = _mla(p0, p['l0_mla_q_norm_g'], p['l0_mla_kv_norm_g'], wqn, wqr, wkk, wkv,
                   None if ctx is None else ctx[1], None if ctx is None else ctx[2], mla_tabs, L=L)
    x = _outproj(y_ssd, mla_out[0], w_out0, x, mod0, **kw)
    x = _peer(x, p['l0_norm2_g'], mod0, *peer0, fg, final_norm=False, **kw)

    p1 = _proj(x, p['l1_norm1_g'], mod1, w_in1, tn=1152, **kw)
    if ctx is None:
        y_gqa = _gqa_ctx(p1, p['l1_gqa_sink'], L=L)
        c0 = n0 = m0 = None
    else:
        gqa_tabs = tuple(jnp.concatenate([t, t], axis=1) for t in rope_tabs)
        y_gqa = _gqa_win(p1, ctx[3].reshape(nseq, -1, 256), ctx[4].reshape(nseq, -1, 256), *gqa_tabs,
                         p['l1_gqa_sink'], L=L)
        c0, n0 = ctx[5], ctx[6]
        m0 = jnp.broadcast_to(ctx[7][..., None], ctx[7].shape + (ML_HEAD_DIM,))
    y_ml, mc, mn, mm = _mlstm(p1.reshape(nseq, L, -1), _chunk_transposed(p1[:, 5632:5664], L), p['l1_ml_conv_w'],
                              p['l1_ml_conv_b'], p['l1_ml_ig_b'], p['l1_ml_fg_b'], p['l1_ml_norm_g'], c0, n0, m0, L=L)
    y_ml = y_ml.reshape(nseq * L, -1)
    x = _outproj(y_gqa, y_ml, w_out1, x, mod1, **kw)
    y = _peer(x, p['l1_norm2_g'], mod1, *peer1, fg, final_norm=True, **kw)
    new = None
    if ctx is None:
        new = (ssd_state, mla_out[1].reshape(nseq, L, MLA_RANK), p0[:, 3584:3648].reshape(nseq, L, MLA_ROPE),
               p1[:, 5120:5376].reshape(nseq, L, GQA_KV_HEADS, GQA_HEAD_DIM),
               p1[:, 5376:5632].reshape(nseq, L, GQA_KV_HEADS, GQA_HEAD_DIM), mc, mn, mm[..., 0])
    return y, new


def kernel(x_prompt, x_sample, state_l0_ssd, cache_l0_mla_ckv, cache_l0_mla_kpe, cache_l1_gqa_k, cache_l1_gqa_v, state_l1_mlstm_C, state_l1_mlstm_n, state_l1_mlstm_m, c, c_ctx, final_norm_g, l0_ada_w, l0_ada_b, l0_norm1_g, l0_norm2_g, l0_w_in, l0_ssd_conv_w, l0_ssd_conv_b, l0_ssd_A_log, l0_ssd_dt_bias, l0_ssd_D, l0_ssd_norm_g, l0_mla_q_norm_g, l0_mla_w_qb, l0_mla_kv_norm_g, l0_mla_w_kvb, l0_w_out, l0_peer_wq, l0_peer_keys, l0_peer_u, l0_peer_v, l1_ada_w, l1_ada_b, l1_norm1_g, l1_norm2_g, l1_w_in, l1_gqa_sink, l1_ml_conv_w, l1_ml_conv_b, l1_ml_ig_b, l1_ml_fg_b, l1_ml_norm_g, l1_w_out, l1_peer_wq, l1_peer_keys, l1_peer_u, l1_peer_v):
    p = dict(l0_norm1_g=l0_norm1_g, l0_norm2_g=l0_norm2_g, l0_ssd_conv_w=l0_ssd_conv_w, l0_ssd_conv_b=l0_ssd_conv_b,
             l0_ssd_A_log=l0_ssd_A_log, l0_ssd_dt_bias=l0_ssd_dt_bias, l0_ssd_D=l0_ssd_D, l0_ssd_norm_g=l0_ssd_norm_g,
             l0_mla_q_norm_g=l0_mla_q_norm_g, l0_mla_kv_norm_g=l0_mla_kv_norm_g, l1_norm1_g=l1_norm1_g,
             l1_norm2_g=l1_norm2_g, l1_gqa_sink=l1_gqa_sink, l1_ml_conv_w=l1_ml_conv_w, l1_ml_conv_b=l1_ml_conv_b,
             l1_ml_ig_b=l1_ml_ig_b, l1_ml_fg_b=l1_ml_fg_b, l1_ml_norm_g=l1_ml_norm_g)
    nb, seq = x_prompt.shape[:2]
    db, dseq = x_sample.shape[:2]
    assert db <= 7 and seq % MLA_QB == 0 and dseq % MLA_QB == 0

    cond8 = jnp.zeros((8, D_MODEL), F32).at[0].set(c_ctx).at[1:1 + db].set(c)
    mod0 = _ada(cond8, l0_ada_w, l0_ada_b).reshape(8, 1, 6 * D_MODEL)
    mod1 = _ada(cond8, l1_ada_w, l1_ada_b).reshape(8, 1, 6 * D_MODEL)
    w0 = _prep_layer0(l0_w_in, l0_mla_w_qb, l0_mla_w_kvb, l0_w_out)
    w1 = _prep_layer1(l1_w_in, l1_w_out)
    peer0 = _prep_peer(l0_peer_wq, l0_peer_keys, l0_peer_u, l0_peer_v)
    peer1 = _prep_peer(l1_peer_wq, l1_peer_keys, l1_peer_u, l1_peer_v)

    y_prompt, new = _trunk(x_prompt.reshape(nb * seq, D_MODEL), seq, mod0, mod1, w0, w1, peer0, peer1, p, None, None,
                           0, nb * seq, final_norm_g)
    ctx = (state_l0_ssd, cache_l0_mla_ckv, cache_l0_mla_kpe, cache_l1_gqa_k, cache_l1_gqa_v, state_l1_mlstm_C,
           state_l1_mlstm_n, state_l1_mlstm_m)
    y_sample, _ = _trunk(x_sample.reshape(db * dseq, D_MODEL), dseq, mod0, mod1, w0, w1, peer0, peer1, p, ctx,
                         _rope_tables(dseq), 1, dseq, final_norm_g)
    return (y_prompt.reshape(nb, seq, D_MODEL), y_sample.reshape(db, dseq, D_MODEL)) + new
```

```python
import functools

import jax
import jax.numpy as jnp
import numpy as np
from jax import lax
from jax.experimental import pallas as pl
from jax.experimental.pallas import tpu as pltpu

F32 = jnp.float32
BF16 = jnp.bfloat16
NEG_INF = float("-inf")

D_MODEL = 2048
EPS = 1e-6
ROPE_BASE = 10000.0
GRID_W = 64
CONV_W = 5
CHUNK = 64
SSD_HEADS = 16
SSD_HEAD_DIM = 64
SSD_D_INNER = 1024
SSD_STATE = 128
MLA_HEADS = 8
MLA_NOPE = 128
MLA_ROPE = 64
MLA_V = 128
MLA_RANK = 512
GQA_HEADS = 16
GQA_KV_HEADS = 4
GQA_HEAD_DIM = 64
WINDOW = 128
ML_HEADS = 8
ML_HEAD_DIM = 128
ML_D = 1024
PEER_HEADS = 8
PEER_NKEYS = 128
PEER_N = PEER_NKEYS * PEER_NKEYS
PEER_TOPK = 16

LANES = 128
SUBLANES = 8
VMEM_LIMIT = 56 * 2**20

L0_COLS = 3840
L1_COLS = 5760


def _cp(n_grid, flags=None):
    return pltpu.CompilerParams(dimension_semantics=("arbitrary",) * n_grid, vmem_limit_bytes=VMEM_LIMIT, flags=flags)


def _silu(x):
    return x * jax.nn.sigmoid(x)


def _softplus(x):
    return jnp.maximum(x, 0.0) + jnp.log1p(jnp.exp(-jnp.abs(x)))


def _rms(x):
    return x * lax.rsqrt(jnp.mean(x * x, axis=-1, keepdims=True) + EPS)


def _dot(a, b):
    return jnp.dot(a, b, preferred_element_type=F32)


def _dot_nt(a, b):
    return lax.dot_general(a, b, (((1,), (1,)), ((), ())), preferred_element_type=F32)


def _dot_tn(a, b):
    return lax.dot_general(a, b, (((0,), (0,)), ((), ())), preferred_element_type=F32)


def _dot_f32(a, b):
    return jnp.dot(a, b, preferred_element_type=F32, precision=lax.Precision.HIGHEST)


def _full(shape):
    nd = len(shape)
    return pl.BlockSpec(shape, lambda *_: (0,) * nd)


def _mod_spec(chunk, tb, mod_base, rows_per_cond):
    def index(i, *_):
        return (mod_base + (i * tb) // rows_per_cond, 0, chunk)
    return pl.BlockSpec((1, 1, D_MODEL), index)


def _ada_kernel(c_ref, w_ref, b_ref, o_ref):
    s = _silu(c_ref[...])
    o_ref[...] = _dot(s.astype(BF16), w_ref[...].astype(BF16)) + b_ref[...]


def _ada(cond8, w, b):
    n = w.shape[1]
    tn = 1024
    return pl.pallas_call(
        _ada_kernel,
        grid=(n // tn,),
        in_specs=[_full((8, D_MODEL)), pl.BlockSpec((D_MODEL, tn), lambda j: (0, j)),
                  pl.BlockSpec((1, tn), lambda j: (0, j))],
        out_specs=pl.BlockSpec((8, tn), lambda j: (0, j)),
        out_shape=jax.ShapeDtypeStruct((8, n), F32),
        compiler_params=_cp(1), name="ada_table",
    )(cond8, w, b.reshape(1, n))


def _proj_kernel(x_ref, g_ref, sc_ref, sh_ref, w_ref, o_ref, h_ref):
    @pl.when(pl.program_id(1) == 0)
    def _():
        h = _rms(x_ref[...]) * g_ref[...] * (1.0 + sc_ref[0]) + sh_ref[0]
        h_ref[...] = h.astype(BF16)
    o_ref[...] = _dot(h_ref[...], w_ref[...])


def _proj(x, g, mod, w, *, tn, mod_base, rows_per_cond):
    n, ncol = x.shape[0], w.shape[1]
    tb = 1024
    return pl.pallas_call(
        _proj_kernel,
        grid=(n // tb, ncol // tn),
        in_specs=[pl.BlockSpec((tb, D_MODEL), lambda i, j: (i, 0)), _full((1, D_MODEL)),
                  _mod_spec(1, tb, mod_base, rows_per_cond), _mod_spec(0, tb, mod_base, rows_per_cond),
                  pl.BlockSpec((D_MODEL, tn), lambda i, j: (0, j))],
        out_specs=pl.BlockSpec((tb, tn), lambda i, j: (i, j)),
        out_shape=jax.ShapeDtypeStruct((n, ncol), F32),
        scratch_shapes=[pltpu.VMEM((tb, D_MODEL), BF16)],
        compiler_params=_cp(2), name="norm_mod_proj",
    )(x, g.reshape(1, D_MODEL), mod, mod, w)


def _outproj_kernel(ya_ref, yb_ref, w_ref, x_ref, g1_ref, o_ref):
    half = ya_ref.shape[1]
    y = _dot(ya_ref[...].astype(BF16), w_ref[0:half, :]) + _dot(yb_ref[...].astype(BF16), w_ref[half:, :])
    o_ref[...] = x_ref[...] + g1_ref[0] * y


def _outproj(ya, yb, w, x, mod, *, mod_base, rows_per_cond):
    n = x.shape[0]
    tb = 512
    return pl.pallas_call(
        _outproj_kernel,
        grid=(n // tb,),
        in_specs=[pl.BlockSpec((tb, ya.shape[1]), lambda i: (i, 0)), pl.BlockSpec((tb, yb.shape[1]), lambda i: (i, 0)),
                  _full(w.shape), pl.BlockSpec((tb, D_MODEL), lambda i: (i, 0)),
                  _mod_spec(2, tb, mod_base, rows_per_cond)],
        out_specs=pl.BlockSpec((tb, D_MODEL), lambda i: (i, 0)),
        out_shape=jax.ShapeDtypeStruct((n, D_MODEL), F32),
        compiler_params=_cp(1), name="out_proj_residual",
    )(ya, yb, w, x, mod)


ROUTE_TB = 512
_CAND_KEEP = (16, 8, 5, 4, 3, 2, 2, 2)


def _sort_pairs(n):
    pairs = []

    def merge(lo, hi, r):
        step = r * 2
        if step < hi - lo:
            merge(lo, hi, step)
            merge(lo + r, hi, step)
            pairs.extend((i, i + r) for i in range(lo + r, hi - r, step))
        else:
            pairs.append((lo, lo + r))

    def sort(lo, hi):
        if hi - lo >= 1:
            mid = lo + (hi - lo) // 2
            sort(lo, mid)
            sort(mid + 1, hi)
            merge(lo, hi, 1)

    sort(0, 15)
    return [(i, j) for i, j in pairs if j < n]


def _top_values(x, count):
    groups = [x[r:r + SUBLANES] for r in range(0, x.shape[0], SUBLANES)]
    n = len(groups)
    for i, j in _sort_pairs(n):
        groups[i], groups[j] = jnp.maximum(groups[i], groups[j]), jnp.minimum(groups[i], groups[j])
    sub = lax.broadcasted_iota(jnp.int32, groups[0].shape, 0)
    tops = []
    for t in range(count):
        head = groups[0]
        m = jnp.max(head, axis=0, keepdims=True)
        tops.append(m)
        if t == count - 1:
            break
        first = jnp.min(jnp.where(head == m, sub, SUBLANES), axis=0, keepdims=True)
        popped = sub == first
        depth = min(count - 1 - t, n)
        for r in range(depth):
            below = groups[r + 1] if r + 1 < n else NEG_INF
            groups[r] = jnp.where(popped, below, groups[r])
    return tops


def _route_kernel(x_ref, g_ref, sc_ref, sh_ref, wqt_ref, keys_ref,
                  hb_ref, th_ref, s2_ref, e2_ref, c1_ref, qt_s, top_s):
    tb = x_ref.shape[0]
    h = _rms(x_ref[...]) * g_ref[...] * (1.0 + sc_ref[0]) + sh_ref[0]
    hb = h.astype(BF16)
    hb_ref[...] = hb
    qt_s[...] = _dot_nt(wqt_ref[...], hb).astype(BF16)

    iota_8 = lax.broadcasted_iota(jnp.int32, (SUBLANES, tb), 0)

    def head(hd, _):
        scores = []
        for c in range(2):
            r0 = pl.multiple_of((hd * 2 + c) * PEER_NKEYS, PEER_NKEYS)
            s = _dot(keys_ref[hd * 2 + c], qt_s[pl.ds(r0, PEER_NKEYS), :])
            scores.append(s)
            for it, m in enumerate(_top_values(s, PEER_TOPK)):
                top_s[c, it:it + 1, :] = m
        a1 = top_s[0]
        a2 = top_s[1]
        pieces = [a1[0:1] + a2, a1[1:2] + a2[0:8]]
        for i in range(2, 8):
            pieces.append(jnp.where(iota_8 < _CAND_KEEP[i], a1[i:i + 1] + a2[0:8], NEG_INF))
        pieces.append(a1[8:16] + a2[0:1])
        cand = jnp.concatenate(pieces, axis=0)
        tau = _top_values(cand, PEER_TOPK)[-1]
        top = a1[0:1] + a2[0:1]
        z = jnp.sum(jnp.where(cand >= tau, jnp.exp(cand - top), 0.0), axis=0, keepdims=True)
        e2 = jnp.exp(scores[1] - a2[0:1])
        c1 = jnp.exp(scores[0] - a1[0:1]) / z
        theta = jnp.full(scores[0].shape, jnp.inf, F32)
        for q in range(PEER_TOPK):
            a2q = a2[q:q + 1]
            phi = jnp.min(jnp.where(a1 + a2q >= tau, a1, jnp.inf), axis=0, keepdims=True)
            theta = jnp.where(scores[0] >= phi, a2q, theta)
        for t in range(tb // LANES):
            ts = slice(t * LANES, (t + 1) * LANES)
            th_ref[hd, t] = theta[:, ts]
            s2_ref[hd, t] = scores[1][:, ts]
            e2_ref[hd, t] = e2[:, ts]
            c1_ref[hd, t] = c1[:, ts]
        return 0

    lax.fori_loop(0, PEER_HEADS, head, 0)


def _route(x, g, mod, wqt, keys, *, mod_base, rows_per_cond):
    n = x.shape[0]
    tb = ROUTE_TB
    ntg = tb // LANES
    hk = (PEER_HEADS, n // LANES, PEER_NKEYS, LANES)
    rspec = pl.BlockSpec((PEER_HEADS, ntg, PEER_NKEYS, LANES), lambda i: (0, i, 0, 0))
    return pl.pallas_call(
        _route_kernel,
        grid=(n // tb,),
        in_specs=[pl.BlockSpec((tb, D_MODEL), lambda i: (i, 0)), _full((1, D_MODEL)),
                  _mod_spec(4, tb, mod_base, rows_per_cond), _mod_spec(3, tb, mod_base, rows_per_cond),
                  _full(wqt.shape), _full(keys.shape)],
        out_specs=[pl.BlockSpec((tb, D_MODEL), lambda i: (i, 0)), rspec, rspec, rspec, rspec],
        out_shape=[jax.ShapeDtypeStruct((n, D_MODEL), BF16)] + [jax.ShapeDtypeStruct(hk, F32)] * 4,
        scratch_shapes=[pltpu.VMEM((PEER_HEADS * 2 * PEER_NKEYS, tb), BF16), pltpu.VMEM((2, PEER_TOPK, tb), F32)],
        compiler_params=_cp(1), name="peer_route",
    )(x, g.reshape(1, D_MODEL), mod, mod, wqt, keys)


PEER_TB = 512
PEER_EC = 1024
PEER_SUB = PEER_EC // 2


def _experts_kernel(hb_ref, th_ref, s2_ref, e2_ref, c1_ref, u0_ref, ub_ref, un_ref, vp_ref, va_ref, vl_ref,
                    x_ref, g2_ref, fg_ref, o_ref, acc_s, st0_s, st1_s, a0_s, a1_s, *, final_norm):
    e = pl.program_id(1)
    tb = hb_ref.shape[0]
    ntg = tb // LANES
    rows_per_sub = PEER_SUB // PEER_NKEYS

    def scores(u_blk, st_s, h):
        res = _dot_nt(u_blk[...], hb_ref[h * 2 * LANES:(h + 1) * 2 * LANES, :])
        st_s[2 * h] = res[:, :LANES]
        st_s[2 * h + 1] = res[:, LANES:]

    def values(vt_blk, act_s, h):
        act = jnp.concatenate([act_s[2 * h], act_s[2 * h + 1]], axis=1)
        hs = slice(h * 2 * LANES, (h + 1) * 2 * LANES)
        for r in range(0, D_MODEL, PEER_SUB):
            acc_s[r:r + PEER_SUB, hs] += _dot(vt_blk[r:r + PEER_SUB, :], act)

    def gates(c, tg, st_s, act_s):
        sub = 32
        for k in range(rows_per_sub):
            row = c * rows_per_sub + k
            for j0 in range(0, PEER_NKEYS, sub):
                js = slice(j0, j0 + sub)
                ks = slice(k * PEER_NKEYS + j0, k * PEER_NKEYS + j0 + sub)
                w = jnp.zeros((sub, LANES), F32)
                for hd in range(PEER_HEADS):
                    chosen = s2_ref[hd, tg, js, :] >= th_ref[hd, tg, row:row + 1, :]
                    w = w + jnp.where(chosen, e2_ref[hd, tg, js, :], 0.0) * c1_ref[hd, tg, row:row + 1, :]
                s = st_s[tg, ks, :]
                act = 0.5 * s * (1.0 + lax.erf(s * np.float32(2.0 ** -0.5)))
                act_s[tg, ks, :] = (act * w).astype(BF16)

    @pl.when(e == 0)
    def _():
        acc_s[...] = jnp.zeros_like(acc_s)
        a1_s[...] = jnp.zeros_like(a1_s)
        scores(u0_ref, st0_s, 0)
        scores(u0_ref, st0_s, 1)

    def phase(c, st_cur, act_cur, u_next, st_next, vt_prev, act_prev):
        for h in range(2):
            scores(u_next, st_next, h)
            gates(c, 2 * h, st_cur, act_cur)
            values(vt_prev, act_prev, h)
            gates(c, 2 * h + 1, st_cur, act_cur)

    phase(0, st0_s, a0_s, ub_ref, st1_s, vp_ref, a1_s)
    phase(1, st1_s, a1_s, un_ref, st0_s, va_ref, a0_s)

    @pl.when(e == pl.num_programs(1) - 1)
    def _():
        values(vl_ref, a1_s, 0)
        values(vl_ref, a1_s, 1)
        y = x_ref[...] + g2_ref[0] * acc_s[...].T
        if final_norm:
            y = _rms(y) * fg_ref[...]
        o_ref[...] = y


def _experts(hb, theta, s2, e2, c1, u, vt, x, mod, fg, *, mod_base, rows_per_cond, final_norm):
    n = x.shape[0]
    tb = PEER_TB
    ntg = tb // LANES
    ne = PEER_N // PEER_EC
    once = dict(pipeline_mode=pl.Buffered(1))
    rspec = pl.BlockSpec((PEER_HEADS, ntg, PEER_NKEYS, LANES), lambda i, e: (0, i, 0, 0), **once)
    rows8 = pl.BlockSpec((PEER_HEADS, ntg, SUBLANES, LANES), lambda i, e: (0, i, e, 0))
    uspec = lambda index: pl.BlockSpec((PEER_SUB, D_MODEL), index)
    vspec = lambda index: pl.BlockSpec((D_MODEL, PEER_SUB), index)
    return pl.pallas_call(
        functools.partial(_experts_kernel, final_norm=final_norm),
        grid=(n // tb, ne),
        in_specs=[pl.BlockSpec((tb, D_MODEL), lambda i, e: (i, 0), **once), rows8, rspec, rspec, rows8,
                  pl.BlockSpec((PEER_SUB, D_MODEL), lambda i, e: (0, 0), **once),
                  uspec(lambda i, e: (2 * e + 1, 0)),
                  uspec(lambda i, e: (jnp.minimum(2 * e + 2, 2 * ne - 2), 0)),
                  vspec(lambda i, e: (0, jnp.maximum(2 * e - 1, 0))),
                  vspec(lambda i, e: (0, 2 * e)),
                  pl.BlockSpec((D_MODEL, PEER_SUB), lambda i, e: (0, 2 * ne - 1), **once),
                  pl.BlockSpec((tb, D_MODEL), lambda i, e: (i, 0), **once),
                  _mod_spec(5, tb, mod_base, rows_per_cond), _full((1, D_MODEL))],
        out_specs=pl.BlockSpec((tb, D_MODEL), lambda i, e: (i, 0)),
        out_shape=jax.ShapeDtypeStruct((n, D_MODEL), F32),
        scratch_shapes=[pltpu.VMEM((D_MODEL, tb), F32), pltpu.VMEM((ntg, PEER_SUB, LANES), F32),
                        pltpu.VMEM((ntg, PEER_SUB, LANES), F32), pltpu.VMEM((ntg, PEER_SUB, LANES), BF16),
                        pltpu.VMEM((ntg, PEER_SUB, LANES), BF16)],
        compiler_params=_cp(2), name="peer_experts",
    )(hb, theta, s2, e2, c1, u, u, u, vt, vt, vt, x, mod, fg.reshape(1, D_MODEL))


def _peer(x, g, mod, wqt, keys, u, vt, fg, *, mod_base, rows_per_cond, final_norm):
    hb, theta, s2, e2, c1 = _route(x, g, mod, wqt, keys, mod_base=mod_base, rows_per_cond=rows_per_cond)
    return _experts(hb, theta, s2, e2, c1, u, vt, x, mod, fg, mod_base=mod_base, rows_per_cond=rows_per_cond,
                    final_norm=final_norm)


def _chunk_index(s, nc):
    return jnp.where(s < nc, s, 2 * nc - 1 - s)


def _scan_group(nseq, largest):
    return max(g for g in (2, 4, 8) if g <= largest and nseq % g == 0)


def _scan_specs(nc):
    def cur(col):
        return lambda b, s: (b, _chunk_index(s, nc), col)

    def prev(col):
        return lambda b, s: (b, jnp.maximum(_chunk_index(s, nc) * 8 - 1, 0), col)

    def nxt(col):
        return lambda b, s: (b, jnp.minimum(_chunk_index(s, nc) * 8 + 8, nc * 8 - 1), col)

    def out(col):
        return lambda b, s: (b, jnp.where(s < nc, nc - 1, 2 * nc - 1 - s), col)
    return cur, prev, nxt, out


def _conv_silu(prev, cur, nxt, w_ref, b_ref, first, last):
    win = jnp.concatenate([jnp.where(first, 0.0, prev), cur, jnp.where(last, 0.0, nxt)], axis=1)
    acc = b_ref[...] + win[:, 6:70, :] * w_ref[0:1, :]
    for k in range(1, CONV_W):
        acc = acc + win[:, 6 + k:70 + k, :] * w_ref[k:k + 1, :]
    return _silu(acc)


def _tri(bwd, G):
    r = lax.broadcasted_iota(jnp.int32, (G, CHUNK, CHUNK), 1)
    c = lax.broadcasted_iota(jnp.int32, (G, CHUNK, CHUNK), 2)
    lower = (c <= r).astype(F32)
    upper = (c >= r).astype(F32)
    return jnp.where(bwd, upper, lower), jnp.where(bwd, lower, upper)


def _bdot(a, b, contract, precision=None):
    return lax.dot_general(a, b, (((contract[0],), (contract[1],)), ((0,), (0,))), preferred_element_type=F32,
                           precision=precision)


def _ssd_kernel(z_ref, xp_ref, xc_ref, xn_ref, bp_ref, bc_ref, bn_ref, sm_ref, dtt_ref,
                cwx_ref, cbx_ref, cwb_ref, cbb_ref, arow_ref, acol_ref, brow_ref, bcol_ref, drow_ref, ng_ref,
                *rest, nc, zero_init):
    st0_ref = None if zero_init else rest[0]
    y_ref, st_ref, yf_s, h_s = rest[0 if zero_init else 1:]
    s = pl.program_id(1)
    bwd = s >= nc
    cidx = _chunk_index(s, nc)
    first = cidx == 0
    last = cidx == nc - 1
    nh = SSD_HEADS

    def initial(direction):
        return jnp.zeros(h_s.shape, F32) if zero_init else st0_ref[:, direction]

    @pl.when(s == 0)
    def _():
        h_s[...] = initial(0)

    @pl.when(s == nc)
    def _():
        st_ref[:, 0] = h_s[...]
        h_s[...] = initial(1)

    G = z_ref.shape[0]
    tsel, tsel_t = _tri(bwd, G)
    mask = tsel > 0.0
    r0 = pl.multiple_of(cidx * CHUNK, CHUNK)
    hi = lax.Precision.HIGHEST

    xs = _conv_silu(xp_ref[...], xc_ref[...], xn_ref[...], cwx_ref, cbx_ref, first, last)
    bcv = _conv_silu(bp_ref[...], bc_ref[...], bn_ref[...], cwb_ref, cbb_ref, first, last)
    dt_all = _softplus(sm_ref[:, :, 64:96] + brow_ref[...])
    dtt_all = _softplus(dtt_ref[:, 0] + bcol_ref[...])
    a_all = dt_all * (-jnp.exp(arow_ref[...]))
    at_all = dtt_all * (-jnp.exp(acol_ref[...]))
    dt = jnp.where(bwd, dt_all[:, :, nh:2 * nh], dt_all[:, :, 0:nh])
    a = jnp.where(bwd, a_all[:, :, nh:2 * nh], a_all[:, :, 0:nh])
    at = jnp.where(bwd, at_all[:, nh:2 * nh, :], at_all[:, 0:nh, :])
    cum = _bdot(tsel, a, (2, 1), hi)
    cum_t = _bdot(at, tsel_t, (2, 1), hi)
    tot = jnp.where(bwd, cum[:, 0:1, :], cum[:, CHUNK - 1:CHUNK, :])

    ys = []
    for g in range(2):
        bg = bcv[:, :, g * SSD_STATE:(g + 1) * SSD_STATE]
        cg = bcv[:, :, 256 + g * SSD_STATE:256 + (g + 1) * SSD_STATE]
        cg16 = cg.astype(BF16)
        cb = _bdot(cg16, bg.astype(BF16), (2, 2))
        for hh in range(nh // 2):
            h = g * (nh // 2) + hh
            cq = cum[:, :, h:h + 1]
            decay = jnp.exp(jnp.where(mask, cq - cum_t[:, h:h + 1, :], NEG_INF))
            xd = xs[:, :, h * SSD_HEAD_DIM:(h + 1) * SSD_HEAD_DIM] * dt[:, :, h:h + 1]
            xd16 = xd.astype(BF16)
            hprev = h_s[:, h]
            y = (_bdot((cb * decay).astype(BF16), xd16, (2, 1))
                 + _bdot(cg16, hprev.astype(BF16), (2, 2)) * jnp.exp(cq))
            ys.append(y)
            th = tot[:, :, h:h + 1]
            bw = bg * jnp.exp(th - cq)
            h_s[:, h] = hprev * jnp.exp(th) + _bdot(xd16, bw.astype(BF16), (1, 1))
    y = jnp.concatenate(ys, axis=2)

    @pl.when(jnp.logical_not(bwd))
    def _():
        yf_s[:, pl.ds(r0, CHUNK), :] = y

    @pl.when(bwd)
    def _():
        yt = (yf_s[:, pl.ds(r0, CHUNK), :] + y + drow_ref[...] * xs) * _silu(z_ref[...])
        y_ref[...] = _rms(yt) * ng_ref[...]

    @pl.when(s == 2 * nc - 1)
    def _():
        st_ref[:, 1] = h_s[...]


def _ssd(p0, dtt, cwx, cbx, cwb, cbb, a_log, dt_bias, d_skip, norm_g, st0, *, L):
    nseq, nc = p0.shape[0], L // CHUNK
    zero_init = st0 is None
    G = _scan_group(nseq, 4)
    cur, prev, nxt, out = _scan_specs(nc)
    st_shape = (G, 2, SSD_HEADS, SSD_HEAD_DIM, SSD_STATE)
    states = [] if zero_init else [st0]
    row = lambda v: v.reshape(1, -1)
    col = lambda v: v.reshape(-1, 1)
    small = [cwx, row(cbx), cwb, row(cbb), row(a_log), col(a_log), row(dt_bias), col(dt_bias),
             row(jnp.repeat(d_skip, SSD_HEAD_DIM)), row(norm_g)]
    blk = lambda rows, width, index: pl.BlockSpec((G, rows, width), index)
    return pl.pallas_call(
        functools.partial(_ssd_kernel, nc=nc, zero_init=zero_init),
        grid=(nseq // G, 2 * nc),
        in_specs=[blk(CHUNK, 1024, cur(0)),
                  blk(8, 1024, prev(1)), blk(CHUNK, 1024, cur(1)), blk(8, 1024, nxt(1)),
                  blk(8, 512, prev(4)), blk(CHUNK, 512, cur(4)), blk(8, 512, nxt(4)),
                  blk(CHUNK, LANES, cur(28)),
                  pl.BlockSpec((G, 1, 32, CHUNK), lambda b, s: (b, _chunk_index(s, nc), 0, 0))]
        + [_full(v.shape) for v in small]
        + [pl.BlockSpec(st_shape, lambda b, s: (b, 0, 0, 0, 0)) for _ in states],
        out_specs=[blk(CHUNK, SSD_D_INNER, out(0)),
                   pl.BlockSpec(st_shape, lambda b, s: (b, 0, 0, 0, 0))],
        out_shape=[jax.ShapeDtypeStruct((nseq, L, SSD_D_INNER), F32),
                   jax.ShapeDtypeStruct((nseq,) + st_shape[1:], F32)],
        scratch_shapes=[pltpu.VMEM((G, L, SSD_D_INNER), F32), pltpu.VMEM(st_shape[:1] + st_shape[2:], F32)],
        compiler_params=_cp(2), name="ssd_mixer",
    )(p0, p0, p0, p0, p0, p0, p0, p0, dtt, *small, *states)


def _rope_tables(L):
    t = jnp.arange(L)
    r = (t // GRID_W).astype(F32)
    c = (t % GRID_W).astype(F32)
    nf = 16
    inv = ROPE_BASE ** (-jnp.arange(nf, dtype=F32) / nf)
    ang = jnp.concatenate([r[:, None] * inv, c[:, None] * inv], axis=-1)
    ang = jnp.concatenate([ang, ang], axis=-1)
    return jnp.cos(ang), jnp.sin(ang)


def _rope128(x, cos, sin):
    lane = lax.broadcasted_iota(jnp.int32, x.shape, 1)
    rot = jnp.where(lane % 64 < 32, -pltpu.roll(x, 96, 1), pltpu.roll(x, 32, 1))
    return x * cos + rot * sin


def _pad_lanes(x):
    return jnp.concatenate([x, jnp.zeros_like(x)], axis=1)


MLA_QB = 256


def _mla_kernel(*refs, L, Lc, rope, emit):
    refs = list(refs)
    qa_ref, kva_ref, sm_ref, qg_ref, kvg_ref, wqn_ref, wqr_ref, wkk_ref, wkv_ref = refs[:9]
    pos = 9
    if Lc:
        cckv_ref, ckpe_ref = refs[pos:pos + 2]
        pos += 2
    if rope:
        cos_ref, sin_ref = refs[pos:pos + 2]
        pos += 2
    y_ref = refs[pos]
    pos += 1
    if emit:
        ckv_ref = refs[pos]
        pos += 1
    qn_s, qr_s, kn_s, v_s, kpe_s = refs[pos:]
    qb = pl.program_id(1)
    scale = np.float32((MLA_NOPE + MLA_ROPE) ** -0.5)

    @pl.when(qb == 0)
    def _():
        ckv = _rms(kva_ref[...]) * kvg_ref[...]
        if emit:
            ckv_ref[...] = ckv
        kpe = _pad_lanes(sm_ref[:, 0:MLA_ROPE])
        if rope:
            kpe = _rope128(kpe, cos_ref[...], sin_ref[...])
        if Lc:
            ckv = jnp.concatenate([cckv_ref[0], ckv], axis=0)
            kpe = jnp.concatenate([_pad_lanes(ckpe_ref[0]), kpe], axis=0)
        c16 = ckv.astype(BF16)
        kn_s[...] = _dot(c16, wkk_ref[...]).astype(BF16)
        v_s[...] = _dot(c16, wkv_ref[...]).astype(BF16)
        kpe_s[...] = kpe.astype(BF16)

    qn = (_rms(qa_ref[...]) * qg_ref[...]).astype(BF16)
    qn_s[...] = _dot(qn, wqn_ref[...]).astype(BF16)
    qr = _dot(qn, wqr_ref[...])
    if rope:
        q0 = pl.multiple_of(qb * MLA_QB, MLA_QB)
        cos = cos_ref[pl.ds(q0, MLA_QB), :]
        sin = sin_ref[pl.ds(q0, MLA_QB), :]
        for h in range(MLA_HEADS):
            qr_s[:, h * LANES:(h + 1) * LANES] = _rope128(qr[:, h * LANES:(h + 1) * LANES], cos, sin).astype(BF16)
    else:
        qr_s[...] = qr.astype(BF16)

    def head(h, _):
        c0 = h * LANES if isinstance(h, int) else pl.multiple_of(h * LANES, LANES)
        s = (_dot_nt(qn_s[:, pl.ds(c0, LANES)], kn_s[:, pl.ds(c0, LANES)])
             + _dot_nt(qr_s[:, pl.ds(c0, LANES)], kpe_s[...])) * scale
        p = jnp.exp(s - jnp.max(s, axis=-1, keepdims=True))
        p = p / jnp.sum(p, axis=-1, keepdims=True)
        y_ref[:, pl.ds(c0, LANES)] = _dot(p.astype(BF16), v_s[:, pl.ds(c0, LANES)])
        return 0

    if L + Lc <= 512:
        for h in range(MLA_HEADS):
            head(h, 0)
    else:
        lax.fori_loop(0, MLA_HEADS, head, 0)


def _mla(p0, qg, kvg, wqn, wqr, wkk, wkv, cache_ckv, cache_kpe, rope_tabs, *, L):
    n = p0.shape[0]
    nseq, nqb = n // L, L // MLA_QB
    Lc = 0 if cache_ckv is None else cache_ckv.shape[1]
    rope = rope_tabs is not None
    emit = cache_ckv is None
    Lk = L + Lc
    args = [p0, p0, p0, qg.reshape(1, -1), kvg.reshape(1, -1), wqn, wqr, wkk, wkv]
    specs = [pl.BlockSpec((MLA_QB, MLA_RANK), lambda b, q: (b * nqb + q, 5)),
             pl.BlockSpec((L, MLA_RANK), lambda b, q: (b, 6)),
             pl.BlockSpec((L, LANES), lambda b, q: (b, 28)),
             _full((1, MLA_RANK)), _full((1, MLA_RANK)), _full(wqn.shape), _full(wqr.shape), _full(wkk.shape),
             _full(wkv.shape)]
    if Lc:
        args += [cache_ckv, cache_kpe]
        specs += [pl.BlockSpec((1, Lc, MLA_RANK), lambda b, q: (b, 0, 0)),
                  pl.BlockSpec((1, Lc, MLA_ROPE), lambda b, q: (b, 0, 0))]
    if rope:
        args += list(rope_tabs)
        specs += [_full((L, LANES)), _full((L, LANES))]
    out_specs = [pl.BlockSpec((MLA_QB, 1024), lambda b, q: (b * nqb + q, 0))]
    out_shape = [jax.ShapeDtypeStruct((n, 1024), F32)]
    if emit:
        out_specs.append(pl.BlockSpec((L, MLA_RANK), lambda b, q: (b, 0)))
        out_shape.append(jax.ShapeDtypeStruct((n, MLA_RANK), F32))
    return pl.pallas_call(
        functools.partial(_mla_kernel, L=L, Lc=Lc, rope=rope, emit=emit),
        grid=(nseq, nqb), in_specs=specs, out_specs=out_specs, out_shape=out_shape,
        scratch_shapes=[pltpu.VMEM((MLA_QB, 1024), BF16), pltpu.VMEM((MLA_QB, 1024), BF16),
                        pltpu.VMEM((Lk, 1024), BF16), pltpu.VMEM((Lk, 1024), BF16), pltpu.VMEM((Lk, LANES), BF16)],
        compiler_params=_cp(2), name="mla_attention",
    )(*args)


def _gqa_ctx_kernel(q_ref, k_ref, v_ref, sink_ref, y_ref):
    q = q_ref[...]
    k = k_ref[...].astype(BF16)
    v = v_ref[...].astype(BF16)
    scale = np.float32(GQA_HEAD_DIM ** -0.5)
    group = GQA_HEADS // GQA_KV_HEADS
    outs = []
    for h in range(GQA_HEADS):
        kh = h // group
        ks = slice(kh * GQA_HEAD_DIM, (kh + 1) * GQA_HEAD_DIM)
        s = _dot_nt(q[:, h * GQA_HEAD_DIM:(h + 1) * GQA_HEAD_DIM].astype(BF16), k[:, ks]) * scale
        m = jnp.maximum(jnp.max(s, axis=-1, keepdims=True), sink_ref[h])
        p = jnp.exp(s - m)
        p = p / (jnp.sum(p, axis=-1, keepdims=True) + jnp.exp(sink_ref[h] - m))
        outs.append(_dot(p.astype(BF16), v[:, ks]))
    y_ref[...] = jnp.concatenate(outs, axis=1)


def _gqa_ctx(p1, sink, *, L):
    n = p1.shape[0]
    return pl.pallas_call(
        _gqa_ctx_kernel,
        grid=(n // L,),
        in_specs=[pl.BlockSpec((L, 1024), lambda b: (b, 2)), pl.BlockSpec((L, 256), lambda b: (b, 20)),
                  pl.BlockSpec((L, 256), lambda b: (b, 21)), pl.BlockSpec(memory_space=pltpu.SMEM)],
        out_specs=pl.BlockSpec((L, 1024), lambda b: (b, 0)),
        out_shape=jax.ShapeDtypeStruct((n, 1024), F32),
        compiler_params=_cp(1), name="gqa_context",
    )(p1, p1, p1, sink)


GQA_QB = 128
GQA_SPAN = GQA_QB + 2 * WINDOW


def _gqa_win_kernel(q_ref, k_ref, v_ref, ck_ref, cv_ref, cos_ref, sin_ref, sink_ref, y_ref, kp_s, vp_s, *, L):
    qb = pl.program_id(1)
    scale = np.float32(GQA_HEAD_DIM ** -0.5)
    group = GQA_HEADS // GQA_KV_HEADS

    @pl.when(qb == 0)
    def _():
        zeros = jnp.zeros((WINDOW, 256), BF16)
        kp_s[0:WINDOW, :] = zeros
        kp_s[WINDOW + L:, :] = zeros
        vp_s[0:WINDOW, :] = zeros
        vp_s[WINDOW + L:, :] = zeros
        for j in range(2):
            cs = slice(j * LANES, (j + 1) * LANES)
            kp_s[WINDOW:WINDOW + L, cs] = _rope128(k_ref[:, cs], cos_ref[...], sin_ref[...]).astype(BF16)
        vp_s[WINDOW:WINDOW + L, :] = v_ref[...].astype(BF16)

    q0 = pl.multiple_of(qb * GQA_QB, GQA_QB)
    cos = cos_ref[pl.ds(q0, GQA_QB), :]
    sin = sin_ref[pl.ds(q0, GQA_QB), :]
    kw = kp_s[pl.ds(q0, GQA_SPAN), :]
    vw = vp_s[pl.ds(q0, GQA_SPAN), :]
    kc = ck_ref[0].astype(BF16)
    vc = cv_ref[0].astype(BF16)
    qpos = q0 + lax.broadcasted_iota(jnp.int32, (GQA_QB, GQA_SPAN), 0)
    kpos = q0 - WINDOW + lax.broadcasted_iota(jnp.int32, (GQA_QB, GQA_SPAN), 1)
    bias = jnp.where(kpos < 0, NEG_INF, jnp.where(kpos >= L, NEG_INF, jnp.where(jnp.abs(qpos - kpos) <= WINDOW, 0.0, NEG_INF)))
    outs = []
    for j in range(GQA_HEADS * GQA_HEAD_DIM // LANES):
        qj = _rope128(q_ref[:, j * LANES:(j + 1) * LANES], cos, sin).astype(BF16)
        for half in range(2):
            h = 2 * j + half
            kh = h // group
            ks = slice(kh * GQA_HEAD_DIM, (kh + 1) * GQA_HEAD_DIM)
            qh = qj[:, half * GQA_HEAD_DIM:(half + 1) * GQA_HEAD_DIM]
            s_loc = _dot_nt(qh, kw[:, ks]) * scale + bias
            s_ctx = _dot_nt(qh, kc[:, ks]) * scale
            m = jnp.maximum(jnp.maximum(jnp.max(s_loc, axis=-1, keepdims=True), jnp.max(s_ctx, axis=-1, keepdims=True)),
                            sink_ref[h])
            p_loc = jnp.exp(s_loc - m)
            p_ctx = jnp.exp(s_ctx - m)
            inv = 1.0 / (jnp.sum(p_loc, axis=-1, keepdims=True) + jnp.sum(p_ctx, axis=-1, keepdims=True)
                         + jnp.exp(sink_ref[h] - m))
            outs.append(_dot((p_loc * inv).astype(BF16), vw[:, ks]) + _dot((p_ctx * inv).astype(BF16), vc[:, ks]))
    y_ref[...] = jnp.concatenate(outs, axis=1)


def _gqa_win(p1, cache_k, cache_v, cos, sin, sink, *, L):
    n = p1.shape[0]
    nseq, nqb = n // L, L // GQA_QB
    Lc = cache_k.shape[1]
    return pl.pallas_call(
        functools.partial(_gqa_win_kernel, L=L),
        grid=(nseq, nqb),
        in_specs=[pl.BlockSpec((GQA_QB, 1024), lambda b, q: (b * nqb + q, 2)),
                  pl.BlockSpec((L, 256), lambda b, q: (b, 20)), pl.BlockSpec((L, 256), lambda b, q: (b, 21)),
                  pl.BlockSpec((1, Lc, 256), lambda b, q: (b, 0, 0)), pl.BlockSpec((1, Lc, 256), lambda b, q: (b, 0, 0)),
                  _full((L, LANES)), _full((L, LANES)), pl.BlockSpec(memory_space=pltpu.SMEM)],
        out_specs=pl.BlockSpec((GQA_QB, 1024), lambda b, q: (b * nqb + q, 0)),
        out_shape=jax.ShapeDtypeStruct((n, 1024), F32),
        scratch_shapes=[pltpu.VMEM((L + 2 * WINDOW, 256), BF16), pltpu.VMEM((L + 2 * WINDOW, 256), BF16)],
        compiler_params=_cp(2), name="gqa_window",
    )(p1, p1, p1, cache_k, cache_v, cos, sin, sink)


def _mlstm_kernel(qp_ref, qc_ref, qn_ref, v_ref, o_ref, sm_ref, gt_ref, cw_ref, cb_ref, brow_ref, bcol_ref, ng_ref,
                  *rest, nc, zero_init):
    init_refs = () if zero_init else rest[:3]
    y_ref, cst_ref, nst_ref, mst_ref, hf_s, c_s, n_s, m_s = rest[len(init_refs):]
    s = pl.program_id(1)
    bwd = s >= nc
    cidx = _chunk_index(s, nc)
    nh = ML_HEADS
    hd = ML_HEAD_DIM

    def load_initial(direction):
        for k, dst in enumerate((c_s, n_s, m_s)):
            dst[...] = jnp.zeros(dst.shape, F32) if zero_init else init_refs[k][:, direction]

    @pl.when(s == 0)
    def _():
        load_initial(0)

    @pl.when(s == nc)
    def _():
        cst_ref[:, 0] = c_s[...]
        nst_ref[:, 0] = n_s[...]
        mst_ref[:, 0] = m_s[...]
        load_initial(1)

    G = qc_ref.shape[0]
    tsel, tsel_t = _tri(bwd, G)
    mask = tsel > 0.0
    kscale = np.float32(hd ** -0.5)
    r0 = pl.multiple_of(cidx * CHUNK, CHUNK)
    hi = lax.Precision.HIGHEST

    qk = _conv_silu(qp_ref[...], qc_ref[...], qn_ref[...], cw_ref, cb_ref, cidx == 0, cidx == nc - 1)
    gates = sm_ref[:, :, 0:4 * nh] + brow_ref[...]
    gates_t = gt_ref[:, 0] + bcol_ref[...]
    li = jnp.where(bwd, gates[:, :, nh:2 * nh], gates[:, :, 0:nh])
    li_t = jnp.where(bwd, gates_t[:, nh:2 * nh, :], gates_t[:, 0:nh, :])
    lf = -_softplus(-jnp.where(bwd, gates[:, :, 3 * nh:4 * nh], gates[:, :, 2 * nh:3 * nh]))
    lf_t = -_softplus(-jnp.where(bwd, gates_t[:, 3 * nh:4 * nh, :], gates_t[:, 2 * nh:3 * nh, :]))
    bc = _bdot(tsel, lf, (2, 1), hi)
    bc_t = _bdot(lf_t, tsel_t, (2, 1), hi)
    tot = jnp.where(bwd, bc[:, 0:1, :], bc[:, CHUNK - 1:CHUNK, :])

    hs = []
    for h in range(nh):
        cs = slice(h * hd, (h + 1) * hd)
        q = qk[:, :, cs]
        k = qk[:, :, ML_D + h * hd:ML_D + (h + 1) * hd] * kscale
        q16, k16 = q.astype(BF16), k.astype(BF16)
        v = v_ref[:, :, cs]
        bq = bc[:, :, h:h + 1]
        m_prev = m_s[:, h:h + 1, 0:1]
        log_d = jnp.where(mask, bq - bc_t[:, h:h + 1, :] + li_t[:, h:h + 1, :], NEG_INF)
        log_inter = bq + m_prev
        m_out = jnp.maximum(log_inter, jnp.max(log_d, axis=2, keepdims=True))
        sd = _bdot(q16, k16, (2, 2)) * jnp.exp(log_d - m_out)
        w_inter = jnp.exp(log_inter - m_out)
        c_prev = c_s[:, h]
        n_prev = n_s[:, h:h + 1, :]
        num = _bdot(sd.astype(BF16), v.astype(BF16), (2, 1)) + w_inter * _bdot(q16, c_prev.astype(BF16), (2, 2))
        den = jnp.sum(sd, axis=2, keepdims=True) + w_inter * jnp.sum(q * n_prev, axis=2, keepdims=True)
        hs.append(num / jnp.maximum(jnp.abs(den), jnp.exp(-m_out)))
        th = tot[:, :, h:h + 1]
        end_inter = th + m_prev
        end_intra = th - bq + li[:, :, h:h + 1]
        m_new = jnp.maximum(end_inter, jnp.max(end_intra, axis=1, keepdims=True))
        w_c = jnp.exp(end_inter - m_new)
        w_k = jnp.exp(end_intra - m_new)
        c_s[:, h] = w_c * c_prev + _bdot((v * w_k).astype(BF16), k16, (1, 1))
        n_s[:, h:h + 1, :] = w_c * n_prev + jnp.sum(k * w_k, axis=1, keepdims=True)
        m_s[:, h:h + 1, :] = jnp.broadcast_to(m_new, (G, 1, hd))
    hcat = jnp.concatenate(hs, axis=2)

    @pl.when(jnp.logical_not(bwd))
    def _():
        hf_s[:, pl.ds(r0, CHUNK), :] = hcat

    @pl.when(bwd)
    def _():
        tot_h = hf_s[:, pl.ds(r0, CHUNK), :] + hcat
        normed = jnp.concatenate([_rms(tot_h[:, :, h * hd:(h + 1) * hd]) for h in range(nh)], axis=2) * ng_ref[...]
        y_ref[...] = normed * jax.nn.sigmoid(o_ref[...])

    @pl.when(s == 2 * nc - 1)
    def _():
        cst_ref[:, 1] = c_s[...]
        nst_ref[:, 1] = n_s[...]
        mst_ref[:, 1] = m_s[...]


def _mlstm(p1, gt, conv_w, conv_b, ig_b, fg_b, norm_g, c0, n0, m0, *, L):
    nseq, nc = p1.shape[0], L // CHUNK
    zero_init = c0 is None
    G = _scan_group(nseq, 8 if zero_init else 4)
    cur, prev, nxt, out = _scan_specs(nc)
    cshape = (G, 2, ML_HEADS, ML_HEAD_DIM, ML_HEAD_DIM)
    nshape = (G, 2, ML_HEADS, ML_HEAD_DIM)
    states = [] if zero_init else [c0, n0, m0]
    gb = jnp.concatenate([ig_b.reshape(-1), fg_b.reshape(-1)])
    small = [conv_w, conv_b.reshape(1, -1), gb.reshape(1, -1), gb.reshape(-1, 1), norm_g.reshape(1, -1)]
    st = lambda shape: pl.BlockSpec(shape, lambda b, s: (b,) + (0,) * (len(shape) - 1))
    blk = lambda rows, width, index: pl.BlockSpec((G, rows, width), index)
    return pl.pallas_call(
        functools.partial(_mlstm_kernel, nc=nc, zero_init=zero_init),
        grid=(nseq // G, 2 * nc),
        in_specs=[blk(8, 2048, prev(0)), blk(CHUNK, 2048, cur(0)), blk(8, 2048, nxt(0)),
                  blk(CHUNK, 1024, cur(3)), blk(CHUNK, 1024, cur(4)), blk(CHUNK, LANES, cur(44)),
                  pl.BlockSpec((G, 1, 32, CHUNK), lambda b, s: (b, _chunk_index(s, nc), 0, 0))]
        + [_full(v.shape) for v in small] + ([] if zero_init else [st(cshape), st(nshape), st(nshape)]),
        out_specs=[blk(CHUNK, ML_D, out(0)), st(cshape), st(nshape), st(nshape)],
        out_shape=[jax.ShapeDtypeStruct((nseq, L, ML_D), F32), jax.ShapeDtypeStruct((nseq,) + cshape[1:], F32),
                   jax.ShapeDtypeStruct((nseq,) + nshape[1:], F32), jax.ShapeDtypeStruct((nseq,) + nshape[1:], F32)],
        scratch_shapes=[pltpu.VMEM((G, L, ML_D), F32), pltpu.VMEM(cshape[:1] + cshape[2:], F32),
                        pltpu.VMEM(nshape[:1] + nshape[2:], F32), pltpu.VMEM(nshape[:1] + nshape[2:], F32)],
        compiler_params=_cp(2), name="mlstm_mixer",
    )(p1, p1, p1, p1, p1, p1, gt, *small, *states)


def _prep_layer0(w_in, w_qb, w_kvb, w_out):
    pad = jnp.zeros((D_MODEL, L0_COLS - 3680), F32)
    w = jnp.concatenate([w_in[:, :2560], w_in[:, 2592:3680], w_in[:, 2560:2592], pad], axis=1).astype(BF16)
    q3 = w_qb.reshape(MLA_RANK, MLA_HEADS, MLA_NOPE + MLA_ROPE)
    wqn = q3[:, :, :MLA_NOPE].reshape(MLA_RANK, -1).astype(BF16)
    wqr = jnp.concatenate([q3[:, :, MLA_NOPE:], jnp.zeros((MLA_RANK, MLA_HEADS, LANES - MLA_ROPE), F32)], axis=-1)
    wqr = wqr.reshape(MLA_RANK, -1).astype(BF16)
    k3 = w_kvb.reshape(MLA_RANK, MLA_HEADS, MLA_NOPE + MLA_V)
    wkk = k3[:, :, :MLA_NOPE].reshape(MLA_RANK, -1).astype(BF16)
    wkv = k3[:, :, MLA_NOPE:].reshape(MLA_RANK, -1).astype(BF16)
    return w, wqn, wqr, wkk, wkv, w_out.astype(BF16)


def _prep_layer1(w_in, w_out):
    pad = jnp.zeros((D_MODEL, L1_COLS - 5664), F32)
    w = jnp.concatenate([w_in[:, 1536:3584], w_in[:, 0:1024], w_in[:, 3584:5632], w_in[:, 1024:1536],
                         w_in[:, 5632:5664], pad], axis=1).astype(BF16)
    return w, w_out.astype(BF16)


def _prep_peer(wq, keys, u, v):
    return (wq.T.astype(BF16), keys.reshape(PEER_HEADS * 2, PEER_NKEYS, -1).astype(BF16), u.astype(BF16),
            v.T.astype(BF16))


def _chunk_transposed(cols, L):
    return cols.reshape(-1, L // CHUNK, CHUNK, cols.shape[1]).transpose(0, 1, 3, 2)


def _trunk(x, L, mod0, mod1, w0, w1, peer0, peer1, p, ctx, rope_tabs, mod_base, rows_per_cond, fg):
    kw = dict(mod_base=mod_base, rows_per_cond=rows_per_cond)
    w_in0, wqn, wqr, wkk, wkv, w_out0 = w0
    w_in1, w_out1 = w1
    nseq = x.shape[0] // L

    p0 = _proj(x, p['l0_norm1_g'], mod0, w_in0, tn=1280, **kw)
    cw, cb = p['l0_ssd_conv_w'], p['l0_ssd_conv_b']
    y_ssd, ssd_state = _ssd(p0.reshape(nseq, L, -1), _chunk_transposed(p0[:, 3648:3680], L), cw[:, :1024], cb[:1024],
                            cw[:, 1024:], cb[1024:], p['l0_ssd_A_log'], p['l0_ssd_dt_bias'], p['l0_ssd_D'],
                            p['l0_ssd_norm_g'], None if ctx is None else ctx[0], L=L)
    y_ssd = y_ssd.reshape(nseq * L, -1)
    mla_tabs = None
    if rope_tabs is not None:
        z64 = jnp.zeros_like(rope_tabs[0])
        mla_tabs = (jnp.concatenate([rope_tabs[0], z64], axis=1), jnp.concatenate([rope_tabs[1], z64], axis=1))
    mla_out = _mla(p0, p['l0_mla_q_norm_g'], p['l0_mla_kv_norm_g'], wqn, wqr, wkk, wkv,
                   None if ctx is None else ctx[1], None if ctx is None else ctx[2], mla_tabs, L=L)
    x = _outproj(y_ssd, mla_out[0], w_out0, x, mod0, **kw)
    x = _peer(x, p['l0_norm2_g'], mod0, *peer0, fg, final_norm=False, **kw)

    p1 = _proj(x, p['l1_norm1_g'], mod1, w_in1, tn=1152, **kw)
    if ctx is None:
        y_gqa = _gqa_ctx(p1, p['l1_gqa_sink'], L=L)
        c0 = n0 = m0 = None
    else:
        gqa_tabs = tuple(jnp.concatenate([t, t], axis=1) for t in rope_tabs)
        y_gqa = _gqa_win(p1, ctx[3].reshape(nseq, -1, 256), ctx[4].reshape(nseq, -1, 256), *gqa_tabs,
                         p['l1_gqa_sink'], L=L)
        c0, n0 = ctx[5], ctx[6]
        m0 = jnp.broadcast_to(ctx[7][..., None], ctx[7].shape + (ML_HEAD_DIM,))
    y_ml, mc, mn, mm = _mlstm(p1.reshape(nseq, L, -1), _chunk_transposed(p1[:, 5632:5664], L), p['l1_ml_conv_w'],
                              p['l1_ml_conv_b'], p['l1_ml_ig_b'], p['l1_ml_fg_b'], p['l1_ml_norm_g'], c0, n0, m0, L=L)
    y_ml = y_ml.reshape(nseq * L, -1)
    x = _outproj(y_gqa, y_ml, w_out1, x, mod1, **kw)
    y = _peer(x, p['l1_norm2_g'], mod1, *peer1, fg, final_norm=True, **kw)
    new = None
    if ctx is None:
        new = (ssd_state, mla_out[1].reshape(nseq, L, MLA_RANK), p0[:, 3584:3648].reshape(nseq, L, MLA_ROPE),
               p1[:, 5120:5376].reshape(nseq, L, GQA_KV_HEADS, GQA_HEAD_DIM),
               p1[:, 5376:5632].reshape(nseq, L, GQA_KV_HEADS, GQA_HEAD_DIM), mc, mn, mm[..., 0])
    return y, new


def kernel(x_prompt, x_sample, state_l0_ssd, cache_l0_mla_ckv, cache_l0_mla_kpe, cache_l1_gqa_k, cache_l1_gqa_v, state_l1_mlstm_C, state_l1_mlstm_n, state_l1_mlstm_m, c, c_ctx, final_norm_g, l0_ada_w, l0_ada_b, l0_norm1_g, l0_norm2_g, l0_w_in, l0_ssd_conv_w, l0_ssd_conv_b, l0_ssd_A_log, l0_ssd_dt_bias, l0_ssd_D, l0_ssd_norm_g, l0_mla_q_norm_g, l0_mla_w_qb, l0_mla_kv_norm_g, l0_mla_w_kvb, l0_w_out, l0_peer_wq, l0_peer_keys, l0_peer_u, l0_peer_v, l1_ada_w, l1_ada_b, l1_norm1_g, l1_norm2_g, l1_w_in, l1_gqa_sink, l1_ml_conv_w, l1_ml_conv_b, l1_ml_ig_b, l1_ml_fg_b, l1_ml_norm_g, l1_w_out, l1_peer_wq, l1_peer_keys, l1_peer_u, l1_peer_v):
    p = dict(l0_norm1_g=l0_norm1_g, l0_norm2_g=l0_norm2_g, l0_ssd_conv_w=l0_ssd_conv_w, l0_ssd_conv_b=l0_ssd_conv_b,
             l0_ssd_A_log=l0_ssd_A_log, l0_ssd_dt_bias=l0_ssd_dt_bias, l0_ssd_D=l0_ssd_D, l0_ssd_norm_g=l0_ssd_norm_g,
             l0_mla_q_norm_g=l0_mla_q_norm_g, l0_mla_kv_norm_g=l0_mla_kv_norm_g, l1_norm1_g=l1_norm1_g,
             l1_norm2_g=l1_norm2_g, l1_gqa_sink=l1_gqa_sink, l1_ml_conv_w=l1_ml_conv_w, l1_ml_conv_b=l1_ml_conv_b,
             l1_ml_ig_b=l1_ml_ig_b, l1_ml_fg_b=l1_ml_fg_b, l1_ml_norm_g=l1_ml_norm_g)
    nb, seq = x_prompt.shape[:2]
    db, dseq = x_sample.shape[:2]
    assert db <= 7 and seq % MLA_QB == 0 and dseq % MLA_QB == 0

    cond8 = jnp.zeros((8, D_MODEL), F32).at[0].set(c_ctx).at[1:1 + db].set(c)
    mod0 = _ada(cond8, l0_ada_w, l0_ada_b).reshape(8, 1, 6 * D_MODEL)
    mod1 = _ada(cond8, l1_ada_w, l1_ada_b).reshape(8, 1, 6 * D_MODEL)
    w0 = _prep_layer0(l0_w_in, l0_mla_w_qb, l0_mla_w_kvb, l0_w_out)
    w1 = _prep_layer1(l1_w_in, l1_w_out)
    peer0 = _prep_peer(l0_peer_wq, l0_peer_keys, l0_peer_u, l0_peer_v)
    peer1 = _prep_peer(l1_peer_wq, l1_peer_keys, l1_peer_u, l1_peer_v)

    y_prompt, new = _trunk(x_prompt.reshape(nb * seq, D_MODEL), seq, mod0, mod1, w0, w1, peer0, peer1, p, None, None,
                           0, nb * seq, final_norm_g)
    ctx = (state_l0_ssd, cache_l0_mla_ckv, cache_l0_mla_kpe, cache_l1_gqa_k, cache_l1_gqa_v, state_l1_mlstm_C,
           state_l1_mlstm_n, state_l1_mlstm_m)
    y_sample, _ = _trunk(x_sample.reshape(db * dseq, D_MODEL), dseq, mod0, mod1, w0, w1, peer0, peer1, p, ctx,
                         _rope_tables(dseq), 1, dseq, final_norm_g)
    return (y_prompt.reshape(nb, seq, D_MODEL), y_sample.reshape(db, dseq, D_MODEL)) + new
```

```python
import functools

import jax
import jax.numpy as jnp
import numpy as np
from jax import lax
from jax.experimental import pallas as pl
from jax.experimental.pallas import tpu as pltpu

F32 = jnp.float32
BF16 = jnp.bfloat16
NEG_INF = float("-inf")

D_MODEL = 2048
EPS = 1e-6
ROPE_BASE = 10000.0
GRID_W = 64
CONV_W = 5
CHUNK = 64
SSD_HEADS = 16
SSD_HEAD_DIM = 64
SSD_D_INNER = 1024
SSD_STATE = 128
MLA_HEADS = 8
MLA_NOPE = 128
MLA_ROPE = 64
MLA_V = 128
MLA_RANK = 512
GQA_HEADS = 16
GQA_KV_HEADS = 4
GQA_HEAD_DIM = 64
WINDOW = 128
ML_HEADS = 8
ML_HEAD_DIM = 128
ML_D = 1024
PEER_HEADS = 8
PEER_NKEYS = 128
PEER_N = PEER_NKEYS * PEER_NKEYS
PEER_TOPK = 16

LANES = 128
SUBLANES = 8
VMEM_LIMIT = 56 * 2**20

L0_COLS = 3840
L1_COLS = 5760


def _cp(n_grid, flags=None):
    return pltpu.CompilerParams(dimension_semantics=("arbitrary",) * n_grid, vmem_limit_bytes=VMEM_LIMIT, flags=flags)


def _silu(x):
    return x * jax.nn.sigmoid(x)


def _softplus(x):
    return jnp.maximum(x, 0.0) + jnp.log1p(jnp.exp(-jnp.abs(x)))


def _rms(x):
    return x * lax.rsqrt(jnp.mean(x * x, axis=-1, keepdims=True) + EPS)


def _dot(a, b):
    return jnp.dot(a, b, preferred_element_type=F32)


def _dot_nt(a, b):
    return lax.dot_general(a, b, (((1,), (1,)), ((), ())), preferred_element_type=F32)


def _dot_tn(a, b):
    return lax.dot_general(a, b, (((0,), (0,)), ((), ())), preferred_element_type=F32)


def _dot_f32(a, b):
    return jnp.dot(a, b, preferred_element_type=F32, precision=lax.Precision.HIGHEST)


def _full(shape):
    nd = len(shape)
    return pl.BlockSpec(shape, lambda *_: (0,) * nd)


def _mod_spec(chunk, tb, mod_base, rows_per_cond):
    def index(i, *_):
        return (mod_base + (i * tb) // rows_per_cond, 0, chunk)
    return pl.BlockSpec((1, 1, D_MODEL), index)


def _ada_kernel(c_ref, w_ref, b_ref, o_ref):
    s = _silu(c_ref[...])
    o_ref[...] = _dot(s.astype(BF16), w_ref[...].astype(BF16)) + b_ref[...]


def _ada(cond8, w, b):
    n = w.shape[1]
    tn = 1024
    return pl.pallas_call(
        _ada_kernel,
        grid=(n // tn,),
        in_specs=[_full((8, D_MODEL)), pl.BlockSpec((D_MODEL, tn), lambda j: (0, j)),
                  pl.BlockSpec((1, tn), lambda j: (0, j))],
        out_specs=pl.BlockSpec((8, tn), lambda j: (0, j)),
        out_shape=jax.ShapeDtypeStruct((8, n), F32),
        compiler_params=_cp(1), name="ada_table",
    )(cond8, w, b.reshape(1, n))


def _proj_kernel(x_ref, g_ref, sc_ref, sh_ref, w_ref, o_ref, h_ref):
    @pl.when(pl.program_id(1) == 0)
    def _():
        h = _rms(x_ref[...]) * g_ref[...] * (1.0 + sc_ref[0]) + sh_ref[0]
        h_ref[...] = h.astype(BF16)
    o_ref[...] = _dot(h_ref[...], w_ref[...])


def _proj(x, g, mod, w, *, tn, mod_base, rows_per_cond):
    n, ncol = x.shape[0], w.shape[1]
    tb = 1024
    return pl.pallas_call(
        _proj_kernel,
        grid=(n // tb, ncol // tn),
        in_specs=[pl.BlockSpec((tb, D_MODEL), lambda i, j: (i, 0)), _full((1, D_MODEL)),
                  _mod_spec(1, tb, mod_base, rows_per_cond), _mod_spec(0, tb, mod_base, rows_per_cond),
                  pl.BlockSpec((D_MODEL, tn), lambda i, j: (0, j))],
        out_specs=pl.BlockSpec((tb, tn), lambda i, j: (i, j)),
        out_shape=jax.ShapeDtypeStruct((n, ncol), F32),
        scratch_shapes=[pltpu.VMEM((tb, D_MODEL), BF16)],
        compiler_params=_cp(2), name="norm_mod_proj",
    )(x, g.reshape(1, D_MODEL), mod, mod, w)


def _outproj_kernel(ya_ref, yb_ref, w_ref, x_ref, g1_ref, o_ref):
    half = ya_ref.shape[1]
    y = _dot(ya_ref[...].astype(BF16), w_ref[0:half, :]) + _dot(yb_ref[...].astype(BF16), w_ref[half:, :])
    o_ref[...] = x_ref[...] + g1_ref[0] * y


def _outproj(ya, yb, w, x, mod, *, mod_base, rows_per_cond):
    n = x.shape[0]
    tb = 512
    return pl.pallas_call(
        _outproj_kernel,
        grid=(n // tb,),
        in_specs=[pl.BlockSpec((tb, ya.shape[1]), lambda i: (i, 0)), pl.BlockSpec((tb, yb.shape[1]), lambda i: (i, 0)),
                  _full(w.shape), pl.BlockSpec((tb, D_MODEL), lambda i: (i, 0)),
                  _mod_spec(2, tb, mod_base, rows_per_cond)],
        out_specs=pl.BlockSpec((tb, D_MODEL), lambda i: (i, 0)),
        out_shape=jax.ShapeDtypeStruct((n, D_MODEL), F32),
        compiler_params=_cp(1), name="out_proj_residual",
    )(ya, yb, w, x, mod)


ROUTE_TB = 512
_CAND_KEEP = (16, 8, 5, 4, 3, 2, 2, 2)


def _sort_pairs(n):
    pairs = []

    def merge(lo, hi, r):
        step = r * 2
        if step < hi - lo:
            merge(lo, hi, step)
            merge(lo + r, hi, step)
            pairs.extend((i, i + r) for i in range(lo + r, hi - r, step))
        else:
            pairs.append((lo, lo + r))

    def sort(lo, hi):
        if hi - lo >= 1:
            mid = lo + (hi - lo) // 2
            sort(lo, mid)
            sort(mid + 1, hi)
            merge(lo, hi, 1)

    sort(0, 15)
    return [(i, j) for i, j in pairs if j < n]


def _top_values(x, count):
    groups = [x[r:r + SUBLANES] for r in range(0, x.shape[0], SUBLANES)]
    n = len(groups)
    for i, j in _sort_pairs(n):
        groups[i], groups[j] = jnp.maximum(groups[i], groups[j]), jnp.minimum(groups[i], groups[j])
    sub = lax.broadcasted_iota(jnp.int32, groups[0].shape, 0)
    tops = []
    for t in range(count):
        head = groups[0]
        m = jnp.max(head, axis=0, keepdims=True)
        tops.append(m)
        if t == count - 1:
            break
        first = jnp.min(jnp.where(head == m, sub, SUBLANES), axis=0, keepdims=True)
        popped = sub == first
        depth = min(count - 1 - t, n)
        for r in range(depth):
            below = groups[r + 1] if r + 1 < n else NEG_INF
            groups[r] = jnp.where(popped, below, groups[r])
    return tops


def _route_kernel(x_ref, g_ref, sc_ref, sh_ref, wqt_ref, keys_ref,
                  hb_ref, th_ref, s2_ref, e2_ref, c1_ref, qt_s, top_s):
    tb = x_ref.shape[0]
    h = _rms(x_ref[...]) * g_ref[...] * (1.0 + sc_ref[0]) + sh_ref[0]
    hb = h.astype(BF16)
    hb_ref[...] = hb
    qt_s[...] = _dot_nt(wqt_ref[...], hb).astype(BF16)

    iota_8 = lax.broadcasted_iota(jnp.int32, (SUBLANES, tb), 0)

    def head(hd, _):
        scores = []
        for c in range(2):
            r0 = pl.multiple_of((hd * 2 + c) * PEER_NKEYS, PEER_NKEYS)
            s = _dot(keys_ref[hd * 2 + c], qt_s[pl.ds(r0, PEER_NKEYS), :])
            scores.append(s)
            for it, m in enumerate(_top_values(s, PEER_TOPK)):
                top_s[c, it:it + 1, :] = m
        a1 = top_s[0]
        a2 = top_s[1]
        pieces = [a1[0:1] + a2, a1[1:2] + a2[0:8]]
        for i in range(2, 8):
            pieces.append(jnp.where(iota_8 < _CAND_KEEP[i], a1[i:i + 1] + a2[0:8], NEG_INF))
        pieces.append(a1[8:16] + a2[0:1])
        cand = jnp.concatenate(pieces, axis=0)
        tau = _top_values(cand, PEER_TOPK)[-1]
        top = a1[0:1] + a2[0:1]
        z = jnp.sum(jnp.where(cand >= tau, jnp.exp(cand - top), 0.0), axis=0, keepdims=True)
        e2 = jnp.exp(scores[1] - a2[0:1])
        c1 = jnp.exp(scores[0] - a1[0:1]) / z
        theta = jnp.full(scores[0].shape, jnp.inf, F32)
        for q in range(PEER_TOPK):
            a2q = a2[q:q + 1]
            phi = jnp.min(jnp.where(a1 + a2q >= tau, a1, jnp.inf), axis=0, keepdims=True)
            theta = jnp.where(scores[0] >= phi, a2q, theta)
        for t in range(tb // LANES):
            ts = slice(t * LANES, (t + 1) * LANES)
            th_ref[hd, t] = theta[:, ts]
            s2_ref[hd, t] = scores[1][:, ts]
            e2_ref[hd, t] = e2[:, ts]
            c1_ref[hd, t] = c1[:, ts]
        return 0

    lax.fori_loop(0, PEER_HEADS, head, 0)


def _route(x, g, mod, wqt, keys, *, mod_base, rows_per_cond):
    n = x.shape[0]
    tb = ROUTE_TB
    ntg = tb // LANES
    hk = (PEER_HEADS, n // LANES, PEER_NKEYS, LANES)
    rspec = pl.BlockSpec((PEER_HEADS, ntg, PEER_NKEYS, LANES), lambda i: (0, i, 0, 0))
    return pl.pallas_call(
        _route_kernel,
        grid=(n // tb,),
        in_specs=[pl.BlockSpec((tb, D_MODEL), lambda i: (i, 0)), _full((1, D_MODEL)),
                  _mod_spec(4, tb, mod_base, rows_per_cond), _mod_spec(3, tb, mod_base, rows_per_cond),
                  _full(wqt.shape), _full(keys.shape)],
        out_specs=[pl.BlockSpec((tb, D_MODEL), lambda i: (i, 0)), rspec, rspec, rspec, rspec],
        out_shape=[jax.ShapeDtypeStruct((n, D_MODEL), BF16)] + [jax.ShapeDtypeStruct(hk, F32)] * 4,
        scratch_shapes=[pltpu.VMEM((PEER_HEADS * 2 * PEER_NKEYS, tb), BF16), pltpu.VMEM((2, PEER_TOPK, tb), F32)],
        compiler_params=_cp(1), name="peer_route",
    )(x, g.reshape(1, D_MODEL), mod, mod, wqt, keys)


PEER_TB = 512
PEER_EC = 1024
PEER_SUB = PEER_EC // 2


def _experts_kernel(hb_ref, th_ref, s2_ref, e2_ref, c1_ref, u0_ref, ub_ref, un_ref, vp_ref, va_ref, vl_ref,
                    x_ref, g2_ref, fg_ref, o_ref, acc_s, st0_s, st1_s, a0_s, a1_s, *, final_norm):
    e = pl.program_id(1)
    tb = hb_ref.shape[0]
    ntg = tb // LANES
    rows_per_sub = PEER_SUB // PEER_NKEYS

    def scores(u_blk, st_s, h):
        res = _dot_nt(u_blk[...], hb_ref[h * 2 * LANES:(h + 1) * 2 * LANES, :])
        st_s[2 * h] = res[:, :LANES]
        st_s[2 * h + 1] = res[:, LANES:]

    def values(vt_blk, act_s, h):
        act = jnp.concatenate([act_s[2 * h], act_s[2 * h + 1]], axis=1)
        hs = slice(h * 2 * LANES, (h + 1) * 2 * LANES)
        for r in range(0, D_MODEL, PEER_SUB):
            acc_s[r:r + PEER_SUB, hs] += _dot(vt_blk[r:r + PEER_SUB, :], act)

    def gates(c, tg, st_s, act_s):
        sub = 32
        for k in range(rows_per_sub):
            row = c * rows_per_sub + k
            for j0 in range(0, PEER_NKEYS, sub):
                js = slice(j0, j0 + sub)
                ks = slice(k * PEER_NKEYS + j0, k * PEER_NKEYS + j0 + sub)
                w = jnp.zeros((sub, LANES), F32)
                for hd in range(PEER_HEADS):
                    chosen = s2_ref[hd, tg, js, :] >= th_ref[hd, tg, row:row + 1, :]
                    w = w + jnp.where(chosen, e2_ref[hd, tg, js, :], 0.0) * c1_ref[hd, tg, row:row + 1, :]
                s = st_s[tg, ks, :]
                act = 0.5 * s * (1.0 + lax.erf(s * np.float32(2.0 ** -0.5)))
                act_s[tg, ks, :] = (act * w).astype(BF16)

    @pl.when(e == 0)
    def _():
        acc_s[...] = jnp.zeros_like(acc_s)
        a1_s[...] = jnp.zeros_like(a1_s)
        scores(u0_ref, st0_s, 0)
        scores(u0_ref, st0_s, 1)

    def phase(c, st_cur, act_cur, u_next, st_next, vt_prev, act_prev):
        for h in range(2):
            scores(u_next, st_next, h)
            gates(c, 2 * h, st_cur, act_cur)
            values(vt_prev, act_prev, h)
            gates(c, 2 * h + 1, st_cur, act_cur)

    phase(0, st0_s, a0_s, ub_ref, st1_s, vp_ref, a1_s)
    phase(1, st1_s, a1_s, un_ref, st0_s, va_ref, a0_s)

    @pl.when(e == pl.num_programs(1) - 1)
    def _():
        values(vl_ref, a1_s, 0)
        values(vl_ref, a1_s, 1)
        y = x_ref[...] + g2_ref[0] * acc_s[...].T
        if final_norm:
            y = _rms(y) * fg_ref[...]
        o_ref[...] = y


def _experts(hb, theta, s2, e2, c1, u, vt, x, mod, fg, *, mod_base, rows_per_cond, final_norm):
    n = x.shape[0]
    tb = PEER_TB
    ntg = tb // LANES
    ne = PEER_N // PEER_EC
    once = dict(pipeline_mode=pl.Buffered(1))
    rspec = pl.BlockSpec((PEER_HEADS, ntg, PEER_NKEYS, LANES), lambda i, e: (0, i, 0, 0), **once)
    rows8 = pl.BlockSpec((PEER_HEADS, ntg, SUBLANES, LANES), lambda i, e: (0, i, e, 0))
    uspec = lambda index: pl.BlockSpec((PEER_SUB, D_MODEL), index)
    vspec = lambda index: pl.BlockSpec((D_MODEL, PEER_SUB), index)
    return pl.pallas_call(
        functools.partial(_experts_kernel, final_norm=final_norm),
        grid=(n // tb, ne),
        in_specs=[pl.BlockSpec((tb, D_MODEL), lambda i, e: (i, 0)), rows8, rspec, rspec, rows8,
                  pl.BlockSpec((PEER_SUB, D_MODEL), lambda i, e: (0, 0), **once),
                  uspec(lambda i, e: (2 * e + 1, 0)),
                  uspec(lambda i, e: (jnp.minimum(2 * e + 2, 2 * ne - 2), 0)),
                  vspec(lambda i, e: (0, jnp.maximum(2 * e - 1, 0))),
                  vspec(lambda i, e: (0, 2 * e)),
                  pl.BlockSpec((D_MODEL, PEER_SUB), lambda i, e: (0, 2 * ne - 1), **once),
                  pl.BlockSpec((tb, D_MODEL), lambda i, e: (i, 0)),
                  _mod_spec(5, tb, mod_base, rows_per_cond), _full((1, D_MODEL))],
        out_specs=pl.BlockSpec((tb, D_MODEL), lambda i, e: (i, 0)),
        out_shape=jax.ShapeDtypeStruct((n, D_MODEL), F32),
        scratch_shapes=[pltpu.VMEM((D_MODEL, tb), F32), pltpu.VMEM((ntg, PEER_SUB, LANES), F32),
                        pltpu.VMEM((ntg, PEER_SUB, LANES), F32), pltpu.VMEM((ntg, PEER_SUB, LANES), BF16),
                        pltpu.VMEM((ntg, PEER_SUB, LANES), BF16)],
        compiler_params=_cp(2), name="peer_experts",
    )(hb, theta, s2, e2, c1, u, u, u, vt, vt, vt, x, mod, fg.reshape(1, D_MODEL))


def _peer(x, g, mod, wqt, keys, u, vt, fg, *, mod_base, rows_per_cond, final_norm):
    hb, theta, s2, e2, c1 = _route(x, g, mod, wqt, keys, mod_base=mod_base, rows_per_cond=rows_per_cond)
    return _experts(hb, theta, s2, e2, c1, u, vt, x, mod, fg, mod_base=mod_base, rows_per_cond=rows_per_cond,
                    final_norm=final_norm)


def _chunk_index(s, nc):
    return jnp.where(s < nc, s, 2 * nc - 1 - s)


def _scan_group(nseq, largest):
    return max(g for g in (2, 4, 8) if g <= largest and nseq % g == 0)


def _scan_specs(nc):
    def cur(col):
        return lambda b, s: (b, _chunk_index(s, nc), col)

    def prev(col):
        return lambda b, s: (b, jnp.maximum(_chunk_index(s, nc) * 8 - 1, 0), col)

    def nxt(col):
        return lambda b, s: (b, jnp.minimum(_chunk_index(s, nc) * 8 + 8, nc * 8 - 1), col)

    def out(col):
        return lambda b, s: (b, jnp.where(s < nc, nc - 1, 2 * nc - 1 - s), col)
    return cur, prev, nxt, out


def _conv_silu(prev, cur, nxt, w_ref, b_ref, first, last):
    win = jnp.concatenate([jnp.where(first, 0.0, prev), cur, jnp.where(last, 0.0, nxt)], axis=1)
    acc = b_ref[...] + win[:, 6:70, :] * w_ref[0:1, :]
    for k in range(1, CONV_W):
        acc = acc + win[:, 6 + k:70 + k, :] * w_ref[k:k + 1, :]
    return _silu(acc)


def _tri(bwd, G):
    r = lax.broadcasted_iota(jnp.int32, (G, CHUNK, CHUNK), 1)
    c = lax.broadcasted_iota(jnp.int32, (G, CHUNK, CHUNK), 2)
    lower = (c <= r).astype(F32)
    upper = (c >= r).astype(F32)
    return jnp.where(bwd, upper, lower), jnp.where(bwd, lower, upper)


def _bdot(a, b, contract, precision=None):
    return lax.dot_general(a, b, (((contract[0],), (contract[1],)), ((0,), (0,))), preferred_element_type=F32,
                           precision=precision)


def _ssd_kernel(z_ref, xp_ref, xc_ref, xn_ref, bp_ref, bc_ref, bn_ref, sm_ref, dtt_ref,
                cwx_ref, cbx_ref, cwb_ref, cbb_ref, arow_ref, acol_ref, brow_ref, bcol_ref, drow_ref, ng_ref,
                *rest, nc, zero_init):
    st0_ref = None if zero_init else rest[0]
    y_ref, st_ref, yf_s, h_s = rest[0 if zero_init else 1:]
    s = pl.program_id(1)
    bwd = s >= nc
    cidx = _chunk_index(s, nc)
    first = cidx == 0
    last = cidx == nc - 1
    nh = SSD_HEADS

    def initial(direction):
        return jnp.zeros(h_s.shape, F32) if zero_init else st0_ref[:, direction]

    @pl.when(s == 0)
    def _():
        h_s[...] = initial(0)

    @pl.when(s == nc)
    def _():
        st_ref[:, 0] = h_s[...]
        h_s[...] = initial(1)

    G = z_ref.shape[0]
    tsel, tsel_t = _tri(bwd, G)
    mask = tsel > 0.0
    r0 = pl.multiple_of(cidx * CHUNK, CHUNK)
    hi = lax.Precision.HIGHEST

    xs = _conv_silu(xp_ref[...], xc_ref[...], xn_ref[...], cwx_ref, cbx_ref, first, last)
    bcv = _conv_silu(bp_ref[...], bc_ref[...], bn_ref[...], cwb_ref, cbb_ref, first, last)
    dt_all = _softplus(sm_ref[:, :, 64:96] + brow_ref[...])
    dtt_all = _softplus(dtt_ref[:, 0] + bcol_ref[...])
    a_all = dt_all * (-jnp.exp(arow_ref[...]))
    at_all = dtt_all * (-jnp.exp(acol_ref[...]))
    dt = jnp.where(bwd, dt_all[:, :, nh:2 * nh], dt_all[:, :, 0:nh])
    a = jnp.where(bwd, a_all[:, :, nh:2 * nh], a_all[:, :, 0:nh])
    at = jnp.where(bwd, at_all[:, nh:2 * nh, :], at_all[:, 0:nh, :])
    cum = _bdot(tsel, a, (2, 1), hi)
    cum_t = _bdot(at, tsel_t, (2, 1), hi)
    tot = jnp.where(bwd, cum[:, 0:1, :], cum[:, CHUNK - 1:CHUNK, :])

    ys = []
    for g in range(2):
        bg = bcv[:, :, g * SSD_STATE:(g + 1) * SSD_STATE]
        cg = bcv[:, :, 256 + g * SSD_STATE:256 + (g + 1) * SSD_STATE]
        cg16 = cg.astype(BF16)
        cb = _bdot(cg16, bg.astype(BF16), (2, 2))
        for hh in range(nh // 2):
            h = g * (nh // 2) + hh
            cq = cum[:, :, h:h + 1]
            decay = jnp.exp(jnp.where(mask, cq - cum_t[:, h:h + 1, :], NEG_INF))
            xd = xs[:, :, h * SSD_HEAD_DIM:(h + 1) * SSD_HEAD_DIM] * dt[:, :, h:h + 1]
            xd16 = xd.astype(BF16)
            hprev = h_s[:, h]
            y = (_bdot((cb * decay).astype(BF16), xd16, (2, 1))
                 + _bdot(cg16, hprev.astype(BF16), (2, 2)) * jnp.exp(cq))
            ys.append(y)
            th = tot[:, :, h:h + 1]
            bw = bg * jnp.exp(th - cq)
            h_s[:, h] = hprev * jnp.exp(th) + _bdot(xd16, bw.astype(BF16), (1, 1))
    y = jnp.concatenate(ys, axis=2)

    @pl.when(jnp.logical_not(bwd))
    def _():
        yf_s[:, pl.ds(r0, CHUNK), :] = y

    @pl.when(bwd)
    def _():
        yt = (yf_s[:, pl.ds(r0, CHUNK), :] + y + drow_ref[...] * xs) * _silu(z_ref[...])
        y_ref[...] = _rms(yt) * ng_ref[...]

    @pl.when(s == 2 * nc - 1)
    def _():
        st_ref[:, 1] = h_s[...]


def _ssd(p0, dtt, cwx, cbx, cwb, cbb, a_log, dt_bias, d_skip, norm_g, st0, *, L):
    nseq, nc = p0.shape[0], L // CHUNK
    zero_init = st0 is None
    G = _scan_group(nseq, 4)
    cur, prev, nxt, out = _scan_specs(nc)
    st_shape = (G, 2, SSD_HEADS, SSD_HEAD_DIM, SSD_STATE)
    states = [] if zero_init else [st0]
    row = lambda v: v.reshape(1, -1)
    col = lambda v: v.reshape(-1, 1)
    small = [cwx, row(cbx), cwb, row(cbb), row(a_log), col(a_log), row(dt_bias), col(dt_bias),
             row(jnp.repeat(d_skip, SSD_HEAD_DIM)), row(norm_g)]
    blk = lambda rows, width, index: pl.BlockSpec((G, rows, width), index)
    return pl.pallas_call(
        functools.partial(_ssd_kernel, nc=nc, zero_init=zero_init),
        grid=(nseq // G, 2 * nc),
        in_specs=[blk(CHUNK, 1024, cur(0)),
                  blk(8, 1024, prev(1)), blk(CHUNK, 1024, cur(1)), blk(8, 1024, nxt(1)),
                  blk(8, 512, prev(4)), blk(CHUNK, 512, cur(4)), blk(8, 512, nxt(4)),
                  blk(CHUNK, LANES, cur(28)),
                  pl.BlockSpec((G, 1, 32, CHUNK), lambda b, s: (b, _chunk_index(s, nc), 0, 0))]
        + [_full(v.shape) for v in small]
        + [pl.BlockSpec(st_shape, lambda b, s: (b, 0, 0, 0, 0)) for _ in states],
        out_specs=[blk(CHUNK, SSD_D_INNER, out(0)),
                   pl.BlockSpec(st_shape, lambda b, s: (b, 0, 0, 0, 0))],
        out_shape=[jax.ShapeDtypeStruct((nseq, L, SSD_D_INNER), F32),
                   jax.ShapeDtypeStruct((nseq,) + st_shape[1:], F32)],
        scratch_shapes=[pltpu.VMEM((G, L, SSD_D_INNER), F32), pltpu.VMEM(st_shape[:1] + st_shape[2:], F32)],
        compiler_params=_cp(2), name="ssd_mixer",
    )(p0, p0, p0, p0, p0, p0, p0, p0, dtt, *small, *states)


def _rope_tables(L):
    t = jnp.arange(L)
    r = (t // GRID_W).astype(F32)
    c = (t % GRID_W).astype(F32)
    nf = 16
    inv = ROPE_BASE ** (-jnp.arange(nf, dtype=F32) / nf)
    ang = jnp.concatenate([r[:, None] * inv, c[:, None] * inv], axis=-1)
    ang = jnp.concatenate([ang, ang], axis=-1)
    return jnp.cos(ang), jnp.sin(ang)


def _rope128(x, cos, sin):
    lane = lax.broadcasted_iota(jnp.int32, x.shape, 1)
    rot = jnp.where(lane % 64 < 32, -pltpu.roll(x, 96, 1), pltpu.roll(x, 32, 1))
    return x * cos + rot * sin


def _pad_lanes(x):
    return jnp.concatenate([x, jnp.zeros_like(x)], axis=1)


MLA_QB = 256


def _mla_kernel(*refs, L, Lc, rope, emit):
    refs = list(refs)
    qa_ref, kva_ref, sm_ref, qg_ref, kvg_ref, wqn_ref, wqr_ref, wkk_ref, wkv_ref = refs[:9]
    pos = 9
    if Lc:
        cckv_ref, ckpe_ref = refs[pos:pos + 2]
        pos += 2
    if rope:
        cos_ref, sin_ref = refs[pos:pos + 2]
        pos += 2
    y_ref = refs[pos]
    pos += 1
    if emit:
        ckv_ref = refs[pos]
        pos += 1
    qn_s, qr_s, kn_s, v_s, kpe_s = refs[pos:]
    qb = pl.program_id(1)
    scale = np.float32((MLA_NOPE + MLA_ROPE) ** -0.5)

    @pl.when(qb == 0)
    def _():
        ckv = _rms(kva_ref[...]) * kvg_ref[...]
        if emit:
            ckv_ref[...] = ckv
        kpe = _pad_lanes(sm_ref[:, 0:MLA_ROPE])
        if rope:
            kpe = _rope128(kpe, cos_ref[...], sin_ref[...])
        if Lc:
            ckv = jnp.concatenate([cckv_ref[0], ckv], axis=0)
            kpe = jnp.concatenate([_pad_lanes(ckpe_ref[0]), kpe], axis=0)
        c16 = ckv.astype(BF16)
        kn_s[...] = _dot(c16, wkk_ref[...]).astype(BF16)
        v_s[...] = _dot(c16, wkv_ref[...]).astype(BF16)
        kpe_s[...] = kpe.astype(BF16)

    qn = (_rms(qa_ref[...]) * qg_ref[...]).astype(BF16)
    qn_s[...] = _dot(qn, wqn_ref[...]).astype(BF16)
    qr = _dot(qn, wqr_ref[...])
    if rope:
        q0 = pl.multiple_of(qb * MLA_QB, MLA_QB)
        cos = cos_ref[pl.ds(q0, MLA_QB), :]
        sin = sin_ref[pl.ds(q0, MLA_QB), :]
        for h in range(MLA_HEADS):
            qr_s[:, h * LANES:(h + 1) * LANES] = _rope128(qr[:, h * LANES:(h + 1) * LANES], cos, sin).astype(BF16)
    else:
        qr_s[...] = qr.astype(BF16)

    def head(h, _):
        c0 = h * LANES if isinstance(h, int) else pl.multiple_of(h * LANES, LANES)
        s = (_dot_nt(qn_s[:, pl.ds(c0, LANES)], kn_s[:, pl.ds(c0, LANES)])
             + _dot_nt(qr_s[:, pl.ds(c0, LANES)], kpe_s[...])) * scale
        p = jnp.exp(s - jnp.max(s, axis=-1, keepdims=True))
        p = p / jnp.sum(p, axis=-1, keepdims=True)
        y_ref[:, pl.ds(c0, LANES)] = _dot(p.astype(BF16), v_s[:, pl.ds(c0, LANES)])
        return 0

    if L + Lc <= 512:
        for h in range(MLA_HEADS):
            head(h, 0)
    else:
        lax.fori_loop(0, MLA_HEADS, head, 0)


def _mla(p0, qg, kvg, wqn, wqr, wkk, wkv, cache_ckv, cache_kpe, rope_tabs, *, L):
    n = p0.shape[0]
    nseq, nqb = n // L, L // MLA_QB
    Lc = 0 if cache_ckv is None else cache_ckv.shape[1]
    rope = rope_tabs is not None
    emit = cache_ckv is None
    Lk = L + Lc
    args = [p0, p0, p0, qg.reshape(1, -1), kvg.reshape(1, -1), wqn, wqr, wkk, wkv]
    specs = [pl.BlockSpec((MLA_QB, MLA_RANK), lambda b, q: (b * nqb + q, 5)),
             pl.BlockSpec((L, MLA_RANK), lambda b, q: (b, 6)),
             pl.BlockSpec((L, LANES), lambda b, q: (b, 28)),
             _full((1, MLA_RANK)), _full((1, MLA_RANK)), _full(wqn.shape), _full(wqr.shape), _full(wkk.shape),
             _full(wkv.shape)]
    if Lc:
        args += [cache_ckv, cache_kpe]
        specs += [pl.BlockSpec((1, Lc, MLA_RANK), lambda b, q: (b, 0, 0)),
                  pl.BlockSpec((1, Lc, MLA_ROPE), lambda b, q: (b, 0, 0))]
    if rope:
        args += list(rope_tabs)
        specs += [_full((L, LANES)), _full((L, LANES))]
    out_specs = [pl.BlockSpec((MLA_QB, 1024), lambda b, q: (b * nqb + q, 0))]
    out_shape = [jax.ShapeDtypeStruct((n, 1024), F32)]
    if emit:
        out_specs.append(pl.BlockSpec((L, MLA_RANK), lambda b, q: (b, 0)))
        out_shape.append(jax.ShapeDtypeStruct((n, MLA_RANK), F32))
    return pl.pallas_call(
        functools.partial(_mla_kernel, L=L, Lc=Lc, rope=rope, emit=emit),
        grid=(nseq, nqb), in_specs=specs, out_specs=out_specs, out_shape=out_shape,
        scratch_shapes=[pltpu.VMEM((MLA_QB, 1024), BF16), pltpu.VMEM((MLA_QB, 1024), BF16),
                        pltpu.VMEM((Lk, 1024), BF16), pltpu.VMEM((Lk, 1024), BF16), pltpu.VMEM((Lk, LANES), BF16)],
        compiler_params=_cp(2), name="mla_attention",
    )(*args)


def _gqa_ctx_kernel(q_ref, k_ref, v_ref, sink_ref, y_ref):
    q = q_ref[...]
    k = k_ref[...].astype(BF16)
    v = v_ref[...].astype(BF16)
    scale = np.float32(GQA_HEAD_DIM ** -0.5)
    group = GQA_HEADS // GQA_KV_HEADS
    rows = q.shape[0]
    outs = []
    for kh in range(GQA_KV_HEADS):
        heads = range(kh * group, (kh + 1) * group)
        ks = slice(kh * GQA_HEAD_DIM, (kh + 1) * GQA_HEAD_DIM)
        qg = jnp.concatenate([q[:, h * GQA_HEAD_DIM:(h + 1) * GQA_HEAD_DIM] for h in heads], axis=0).astype(BF16)
        sink = jnp.concatenate([jnp.full((rows, 1), sink_ref[h], F32) for h in heads], axis=0)
        s = _dot_nt(qg, k[:, ks]) * scale
        m = jnp.maximum(jnp.max(s, axis=-1, keepdims=True), sink)
        p = jnp.exp(s - m)
        p = p / (jnp.sum(p, axis=-1, keepdims=True) + jnp.exp(sink - m))
        o = _dot(p.astype(BF16), v[:, ks])
        outs.extend(o[i * rows:(i + 1) * rows] for i in range(group))
    y_ref[...] = jnp.concatenate(outs, axis=1)


def _gqa_ctx(p1, sink, *, L):
    n = p1.shape[0]
    return pl.pallas_call(
        _gqa_ctx_kernel,
        grid=(n // L,),
        in_specs=[pl.BlockSpec((L, 1024), lambda b: (b, 2)), pl.BlockSpec((L, 256), lambda b: (b, 20)),
                  pl.BlockSpec((L, 256), lambda b: (b, 21)), pl.BlockSpec(memory_space=pltpu.SMEM)],
        out_specs=pl.BlockSpec((L, 1024), lambda b: (b, 0)),
        out_shape=jax.ShapeDtypeStruct((n, 1024), F32),
        compiler_params=_cp(1), name="gqa_context",
    )(p1, p1, p1, sink)


GQA_QB = 128
GQA_SPAN = GQA_QB + 2 * WINDOW


def _gqa_win_kernel(q_ref, k_ref, v_ref, ck_ref, cv_ref, cos_ref, sin_ref, sink_ref, y_ref, kp_s, vp_s, *, L):
    qb = pl.program_id(1)
    scale = np.float32(GQA_HEAD_DIM ** -0.5)
    group = GQA_HEADS // GQA_KV_HEADS

    @pl.when(qb == 0)
    def _():
        zeros = jnp.zeros((WINDOW, 256), BF16)
        kp_s[0:WINDOW, :] = zeros
        kp_s[WINDOW + L:, :] = zeros
        vp_s[0:WINDOW, :] = zeros
        vp_s[WINDOW + L:, :] = zeros
        for j in range(2):
            cs = slice(j * LANES, (j + 1) * LANES)
            kp_s[WINDOW:WINDOW + L, cs] = _rope128(k_ref[:, cs], cos_ref[...], sin_ref[...]).astype(BF16)
        vp_s[WINDOW:WINDOW + L, :] = v_ref[...].astype(BF16)

    q0 = pl.multiple_of(qb * GQA_QB, GQA_QB)
    cos = cos_ref[pl.ds(q0, GQA_QB), :]
    sin = sin_ref[pl.ds(q0, GQA_QB), :]
    kw = kp_s[pl.ds(q0, GQA_SPAN), :]
    vw = vp_s[pl.ds(q0, GQA_SPAN), :]
    kc = ck_ref[0].astype(BF16)
    vc = cv_ref[0].astype(BF16)
    qpos = q0 + lax.broadcasted_iota(jnp.int32, (GQA_QB, GQA_SPAN), 0)
    kpos = q0 - WINDOW + lax.broadcasted_iota(jnp.int32, (GQA_QB, GQA_SPAN), 1)
    bias = jnp.where(kpos < 0, NEG_INF, jnp.where(kpos >= L, NEG_INF, jnp.where(jnp.abs(qpos - kpos) <= WINDOW, 0.0, NEG_INF)))
    bias_g = jnp.concatenate([bias] * group, axis=0)
    outs = []
    for kh in range(GQA_KV_HEADS):
        heads = range(kh * group, (kh + 1) * group)
        ks = slice(kh * GQA_HEAD_DIM, (kh + 1) * GQA_HEAD_DIM)
        pieces = []
        for j in range(kh * group // 2, (kh + 1) * group // 2):
            qj = _rope128(q_ref[:, j * LANES:(j + 1) * LANES], cos, sin).astype(BF16)
            pieces += [qj[:, :GQA_HEAD_DIM], qj[:, GQA_HEAD_DIM:]]
        qg = jnp.concatenate(pieces, axis=0)
        sink = jnp.concatenate([jnp.full((GQA_QB, 1), sink_ref[h], F32) for h in heads], axis=0)
        s_loc = _dot_nt(qg, kw[:, ks]) * scale + bias_g
        s_ctx = _dot_nt(qg, kc[:, ks]) * scale
        m = jnp.maximum(jnp.maximum(jnp.max(s_loc, axis=-1, keepdims=True), jnp.max(s_ctx, axis=-1, keepdims=True)), sink)
        p_loc = jnp.exp(s_loc - m)
        p_ctx = jnp.exp(s_ctx - m)
        inv = 1.0 / (jnp.sum(p_loc, axis=-1, keepdims=True) + jnp.sum(p_ctx, axis=-1, keepdims=True)
                     + jnp.exp(sink - m))
        o = _dot((p_loc * inv).astype(BF16), vw[:, ks]) + _dot((p_ctx * inv).astype(BF16), vc[:, ks])
        outs.extend(o[i * GQA_QB:(i + 1) * GQA_QB] for i in range(group))
    y_ref[...] = jnp.concatenate(outs, axis=1)


def _gqa_win(p1, cache_k, cache_v, cos, sin, sink, *, L):
    n = p1.shape[0]
    nseq, nqb = n // L, L // GQA_QB
    Lc = cache_k.shape[1]
    return pl.pallas_call(
        functools.partial(_gqa_win_kernel, L=L),
        grid=(nseq, nqb),
        in_specs=[pl.BlockSpec((GQA_QB, 1024), lambda b, q: (b * nqb + q, 2)),
                  pl.BlockSpec((L, 256), lambda b, q: (b, 20)), pl.BlockSpec((L, 256), lambda b, q: (b, 21)),
                  pl.BlockSpec((1, Lc, 256), lambda b, q: (b, 0, 0)), pl.BlockSpec((1, Lc, 256), lambda b, q: (b, 0, 0)),
                  _full((L, LANES)), _full((L, LANES)), pl.BlockSpec(memory_space=pltpu.SMEM)],
        out_specs=pl.BlockSpec((GQA_QB, 1024), lambda b, q: (b * nqb + q, 0)),
        out_shape=jax.ShapeDtypeStruct((n, 1024), F32),
        scratch_shapes=[pltpu.VMEM((L + 2 * WINDOW, 256), BF16), pltpu.VMEM((L + 2 * WINDOW, 256), BF16)],
        compiler_params=_cp(2), name="gqa_window",
    )(p1, p1, p1, cache_k, cache_v, cos, sin, sink)


def _mlstm_kernel(qp_ref, qc_ref, qn_ref, v_ref, o_ref, sm_ref, gt_ref, cw_ref, cb_ref, brow_ref, bcol_ref, ng_ref,
                  *rest, nc, zero_init):
    init_refs = () if zero_init else rest[:3]
    y_ref, cst_ref, nst_ref, mst_ref, hf_s, c_s, n_s, m_s = rest[len(init_refs):]
    s = pl.program_id(1)
    bwd = s >= nc
    cidx = _chunk_index(s, nc)
    nh = ML_HEADS
    hd = ML_HEAD_DIM

    def load_initial(direction):
        for k, dst in enumerate((c_s, n_s, m_s)):
            dst[...] = jnp.zeros(dst.shape, F32) if zero_init else init_refs[k][:, direction]

    @pl.when(s == 0)
    def _():
        load_initial(0)

    @pl.when(s == nc)
    def _():
        cst_ref[:, 0] = c_s[...]
        nst_ref[:, 0] = n_s[...]
        mst_ref[:, 0] = m_s[...]
        load_initial(1)

    G = qc_ref.shape[0]
    tsel, tsel_t = _tri(bwd, G)
    mask = tsel > 0.0
    kscale = np.float32(hd ** -0.5)
    r0 = pl.multiple_of(cidx * CHUNK, CHUNK)
    hi = lax.Precision.HIGHEST

    qk = _conv_silu(qp_ref[...], qc_ref[...], qn_ref[...], cw_ref, cb_ref, cidx == 0, cidx == nc - 1)
    gates = sm_ref[:, :, 0:4 * nh] + brow_ref[...]
    gates_t = gt_ref[:, 0] + bcol_ref[...]
    li = jnp.where(bwd, gates[:, :, nh:2 * nh], gates[:, :, 0:nh])
    li_t = jnp.where(bwd, gates_t[:, nh:2 * nh, :], gates_t[:, 0:nh, :])
    lf = -_softplus(-jnp.where(bwd, gates[:, :, 3 * nh:4 * nh], gates[:, :, 2 * nh:3 * nh]))
    lf_t = -_softplus(-jnp.where(bwd, gates_t[:, 3 * nh:4 * nh, :], gates_t[:, 2 * nh:3 * nh, :]))
    bc = _bdot(tsel, lf, (2, 1), hi)
    bc_t = _bdot(lf_t, tsel_t, (2, 1), hi)
    tot = jnp.where(bwd, bc[:, 0:1, :], bc[:, CHUNK - 1:CHUNK, :])

    hs = []
    for h in range(nh):
        cs = slice(h * hd, (h + 1) * hd)
        q = qk[:, :, cs]
        k = qk[:, :, ML_D + h * hd:ML_D + (h + 1) * hd] * kscale
        q16, k16 = q.astype(BF16), k.astype(BF16)
        v = v_ref[:, :, cs]
        bq = bc[:, :, h:h + 1]
        m_prev = m_s[:, h:h + 1, 0:1]
        log_d = jnp.where(mask, bq - bc_t[:, h:h + 1, :] + li_t[:, h:h + 1, :], NEG_INF)
        log_inter = bq + m_prev
        m_out = jnp.maximum(log_inter, jnp.max(log_d, axis=2, keepdims=True))
        sd = _bdot(q16, k16, (2, 2)) * jnp.exp(log_d - m_out)
        w_inter = jnp.exp(log_inter - m_out)
        c_prev = c_s[:, h]
        n_prev = n_s[:, h:h + 1, :]
        num = _bdot(sd.astype(BF16), v.astype(BF16), (2, 1)) + w_inter * _bdot(q16, c_prev.astype(BF16), (2, 2))
        den = jnp.sum(sd, axis=2, keepdims=True) + w_inter * jnp.sum(q * n_prev, axis=2, keepdims=True)
        hs.append(num / jnp.maximum(jnp.abs(den), jnp.exp(-m_out)))
        th = tot[:, :, h:h + 1]
        end_inter = th + m_prev
        end_intra = th - bq + li[:, :, h:h + 1]
        m_new = jnp.maximum(end_inter, jnp.max(end_intra, axis=1, keepdims=True))
        w_c = jnp.exp(end_inter - m_new)
        w_k = jnp.exp(end_intra - m_new)
        c_s[:, h] = w_c * c_prev + _bdot((v * w_k).astype(BF16), k16, (1, 1))
        n_s[:, h:h + 1, :] = w_c * n_prev + jnp.sum(k * w_k, axis=1, keepdims=True)
        m_s[:, h:h + 1, :] = jnp.broadcast_to(m_new, (G, 1, hd))
    hcat = jnp.concatenate(hs, axis=2)

    @pl.when(jnp.logical_not(bwd))
    def _():
        hf_s[:, pl.ds(r0, CHUNK), :] = hcat

    @pl.when(bwd)
    def _():
        tot_h = hf_s[:, pl.ds(r0, CHUNK), :] + hcat
        normed = jnp.concatenate([_rms(tot_h[:, :, h * hd:(h + 1) * hd]) for h in range(nh)], axis=2) * ng_ref[...]
        y_ref[...] = normed * jax.nn.sigmoid(o_ref[...])

    @pl.when(s == 2 * nc - 1)
    def _():
        cst_ref[:, 1] = c_s[...]
        nst_ref[:, 1] = n_s[...]
        mst_ref[:, 1] = m_s[...]


def _mlstm(p1, gt, conv_w, conv_b, ig_b, fg_b, norm_g, c0, n0, m0, *, L):
    nseq, nc = p1.shape[0], L // CHUNK
    zero_init = c0 is None
    G = _scan_group(nseq, 8 if zero_init else 4)
    cur, prev, nxt, out = _scan_specs(nc)
    cshape = (G, 2, ML_HEADS, ML_HEAD_DIM, ML_HEAD_DIM)
    nshape = (G, 2, ML_HEADS, ML_HEAD_DIM)
    states = [] if zero_init else [c0, n0, m0]
    gb = jnp.concatenate([ig_b.reshape(-1), fg_b.reshape(-1)])
    small = [conv_w, conv_b.reshape(1, -1), gb.reshape(1, -1), gb.reshape(-1, 1), norm_g.reshape(1, -1)]
    st = lambda shape: pl.BlockSpec(shape, lambda b, s: (b,) + (0,) * (len(shape) - 1))
    blk = lambda rows, width, index: pl.BlockSpec((G, rows, width), index)
    return pl.pallas_call(
        functools.partial(_mlstm_kernel, nc=nc, zero_init=zero_init),
        grid=(nseq // G, 2 * nc),
        in_specs=[blk(8, 2048, prev(0)), blk(CHUNK, 2048, cur(0)), blk(8, 2048, nxt(0)),
                  blk(CHUNK, 1024, cur(3)), blk(CHUNK, 1024, cur(4)), blk(CHUNK, LANES, cur(44)),
                  pl.BlockSpec((G, 1, 32, CHUNK), lambda b, s: (b, _chunk_index(s, nc), 0, 0))]
        + [_full(v.shape) for v in small] + ([] if zero_init else [st(cshape), st(nshape), st(nshape)]),
        out_specs=[blk(CHUNK, ML_D, out(0)), st(cshape), st(nshape), st(nshape)],
        out_shape=[jax.ShapeDtypeStruct((nseq, L, ML_D), F32), jax.ShapeDtypeStruct((nseq,) + cshape[1:], F32),
                   jax.ShapeDtypeStruct((nseq,) + nshape[1:], F32), jax.ShapeDtypeStruct((nseq,) + nshape[1:], F32)],
        scratch_shapes=[pltpu.VMEM((G, L, ML_D), F32), pltpu.VMEM(cshape[:1] + cshape[2:], F32),
                        pltpu.VMEM(nshape[:1] + nshape[2:], F32), pltpu.VMEM(nshape[:1] + nshape[2:], F32)],
        compiler_params=_cp(2), name="mlstm_mixer",
    )(p1, p1, p1, p1, p1, p1, gt, *small, *states)


def _prep_layer0(w_in, w_qb, w_kvb, w_out):
    pad = jnp.zeros((D_MODEL, L0_COLS - 3680), F32)
    w = jnp.concatenate([w_in[:, :2560], w_in[:, 2592:3680], w_in[:, 2560:2592], pad], axis=1).astype(BF16)
    q3 = w_qb.reshape(MLA_RANK, MLA_HEADS, MLA_NOPE + MLA_ROPE)
    wqn = q3[:, :, :MLA_NOPE].reshape(MLA_RANK, -1).astype(BF16)
    wqr = jnp.concatenate([q3[:, :, MLA_NOPE:], jnp.zeros((MLA_RANK, MLA_HEADS, LANES - MLA_ROPE), F32)], axis=-1)
    wqr = wqr.reshape(MLA_RANK, -1).astype(BF16)
    k3 = w_kvb.reshape(MLA_RANK, MLA_HEADS, MLA_NOPE + MLA_V)
    wkk = k3[:, :, :MLA_NOPE].reshape(MLA_RANK, -1).astype(BF16)
    wkv = k3[:, :, MLA_NOPE:].reshape(MLA_RANK, -1).astype(BF16)
    return w, wqn, wqr, wkk, wkv, w_out.astype(BF16)


def _prep_layer1(w_in, w_out):
    pad = jnp.zeros((D_MODEL, L1_COLS - 5664), F32)
    w = jnp.concatenate([w_in[:, 1536:3584], w_in[:, 0:1024], w_in[:, 3584:5632], w_in[:, 1024:1536],
                         w_in[:, 5632:5664], pad], axis=1).astype(BF16)
    return w, w_out.astype(BF16)


def _prep_peer(wq, keys, u, v):
    return (wq.T.astype(BF16), keys.reshape(PEER_HEADS * 2, PEER_NKEYS, -1).astype(BF16), u.astype(BF16),
            v.T.astype(BF16))


def _chunk_transposed(cols, L):
    return cols.reshape(-1, L // CHUNK, CHUNK, cols.shape[1]).transpose(0, 1, 3, 2)


def _trunk(x, L, mod0, mod1, w0, w1, peer0, peer1, p, ctx, rope_tabs, mod_base, rows_per_cond, fg):
    kw = dict(mod_base=mod_base, rows_per_cond=rows_per_cond)
    w_in0, wqn, wqr, wkk, wkv, w_out0 = w0
    w_in1, w_out1 = w1
    nseq = x.shape[0] // L

    p0 = _proj(x, p['l0_norm1_g'], mod0, w_in0, tn=1280, **kw)
    cw, cb = p['l0_ssd_conv_w'], p['l0_ssd_conv_b']
    y_ssd, ssd_state = _ssd(p0.reshape(nseq, L, -1), _chunk_transposed(p0[:, 3648:3680], L), cw[:, :1024], cb[:1024],
                            cw[:, 1024:], cb[1024:], p['l0_ssd_A_log'], p['l0_ssd_dt_bias'], p['l0_ssd_D'],
                            p['l0_ssd_norm_g'], None if ctx is None else ctx[0], L=L)
    y_ssd = y_ssd.reshape(nseq * L, -1)
    mla_tabs = None
    if rope_tabs is not None:
        z64 = jnp.zeros_like(rope_tabs[0])
        mla_tabs = (jnp.concatenate([rope_tabs[0], z64], axis=1), jnp.concatenate([rope_tabs[1], z64], axis=1))
    mla_out = _mla(p0, p['l0_mla_q_norm_g'], p['l0_mla_kv_norm_g'], wqn, wqr, wkk, wkv,
                   None if ctx is None else ctx[1], None if ctx is None else ctx[2], mla_tabs, L=L)
    x = _outproj(y_ssd, mla_out[0], w_out0, x, mod0, **kw)
    x = _peer(x, p['l0_norm2_g'], mod0, *peer0, fg, final_norm=False, **kw)

    p1 = _proj(x, p['l1_norm1_g'], mod1, w_in1, tn=1152, **kw)
    if ctx is None:
        y_gqa = _gqa_ctx(p1, p['l1_gqa_sink'], L=L)
        c0 = n0 = m0 = None
    else:
        gqa_tabs = tuple(jnp.concatenate([t, t], axis=1) for t in rope_tabs)
        y_gqa = _gqa_win(p1, ctx[3].reshape(nseq, -1, 256), ctx[4].reshape(nseq, -1, 256), *gqa_tabs,
                         p['l1_gqa_sink'], L=L)
        c0, n0 = ctx[5], ctx[6]
        m0 = jnp.broadcast_to(ctx[7][..., None], ctx[7].shape + (ML_HEAD_DIM,))
    y_ml, mc, mn, mm = _mlstm(p1.reshape(nseq, L, -1), _chunk_transposed(p1[:, 5632:5664], L), p['l1_ml_conv_w'],
                              p['l1_ml_conv_b'], p['l1_ml_ig_b'], p['l1_ml_fg_b'], p['l1_ml_norm_g'], c0, n0, m0, L=L)
    y_ml = y_ml.reshape(nseq * L, -1)
    x = _outproj(y_gqa, y_ml, w_out1, x, mod1, **kw)
    y = _peer(x, p['l1_norm2_g'], mod1, *peer1, fg, final_norm=True, **kw)
    new = None
    if ctx is None:
        new = (ssd_state, mla_out[1].reshape(nseq, L, MLA_RANK), p0[:, 3584:3648].reshape(nseq, L, MLA_ROPE),
               p1[:, 5120:5376].reshape(nseq, L, GQA_KV_HEADS, GQA_HEAD_DIM),
               p1[:, 5376:5632].reshape(nseq, L, GQA_KV_HEADS, GQA_HEAD_DIM), mc, mn, mm[..., 0])
    return y, new


def kernel(x_prompt, x_sample, state_l0_ssd, cache_l0_mla_ckv, cache_l0_mla_kpe, cache_l1_gqa_k, cache_l1_gqa_v, state_l1_mlstm_C, state_l1_mlstm_n, state_l1_mlstm_m, c, c_ctx, final_norm_g, l0_ada_w, l0_ada_b, l0_norm1_g, l0_norm2_g, l0_w_in, l0_ssd_conv_w, l0_ssd_conv_b, l0_ssd_A_log, l0_ssd_dt_bias, l0_ssd_D, l0_ssd_norm_g, l0_mla_q_norm_g, l0_mla_w_qb, l0_mla_kv_norm_g, l0_mla_w_kvb, l0_w_out, l0_peer_wq, l0_peer_keys, l0_peer_u, l0_peer_v, l1_ada_w, l1_ada_b, l1_norm1_g, l1_norm2_g, l1_w_in, l1_gqa_sink, l1_ml_conv_w, l1_ml_conv_b, l1_ml_ig_b, l1_ml_fg_b, l1_ml_norm_g, l1_w_out, l1_peer_wq, l1_peer_keys, l1_peer_u, l1_peer_v):
    p = dict(l0_norm1_g=l0_norm1_g, l0_norm2_g=l0_norm2_g, l0_ssd_conv_w=l0_ssd_conv_w, l0_ssd_conv_b=l0_ssd_conv_b,
             l0_ssd_A_log=l0_ssd_A_log, l0_ssd_dt_bias=l0_ssd_dt_bias, l0_ssd_D=l0_ssd_D, l0_ssd_norm_g=l0_ssd_norm_g,
             l0_mla_q_norm_g=l0_mla_q_norm_g, l0_mla_kv_norm_g=l0_mla_kv_norm_g, l1_norm1_g=l1_norm1_g,
             l1_norm2_g=l1_norm2_g, l1_gqa_sink=l1_gqa_sink, l1_ml_conv_w=l1_ml_conv_w, l1_ml_conv_b=l1_ml_conv_b,
             l1_ml_ig_b=l1_ml_ig_b, l1_ml_fg_b=l1_ml_fg_b, l1_ml_norm_g=l1_ml_norm_g)
    nb, seq = x_prompt.shape[:2]
    db, dseq = x_sample.shape[:2]
    assert db <= 7 and seq % MLA_QB == 0 and dseq % MLA_QB == 0

    cond8 = jnp.zeros((8, D_MODEL), F32).at[0].set(c_ctx).at[1:1 + db].set(c)
    mod0 = _ada(cond8, l0_ada_w, l0_ada_b).reshape(8, 1, 6 * D_MODEL)
    mod1 = _ada(cond8, l1_ada_w, l1_ada_b).reshape(8, 1, 6 * D_MODEL)
    w0 = _prep_layer0(l0_w_in, l0_mla_w_qb, l0_mla_w_kvb, l0_w_out)
    w1 = _prep_layer1(l1_w_in, l1_w_out)
    peer0 = _prep_peer(l0_peer_wq, l0_peer_keys, l0_peer_u, l0_peer_v)
    peer1 = _prep_peer(l1_peer_wq, l1_peer_keys, l1_peer_u, l1_peer_v)

    y_prompt, new = _trunk(x_prompt.reshape(nb * seq, D_MODEL), seq, mod0, mod1, w0, w1, peer0, peer1, p, None, None,
                           0, nb * seq, final_norm_g)
    ctx = (state_l0_ssd, cache_l0_mla_ckv, cache_l0_mla_kpe, cache_l1_gqa_k, cache_l1_gqa_v, state_l1_mlstm_C,
           state_l1_mlstm_n, state_l1_mlstm_m)
    y_sample, _ = _trunk(x_sample.reshape(db * dseq, D_MODEL), dseq, mod0, mod1, w0, w1, peer0, peer1, p, ctx,
                         _rope_tables(dseq), 1, dseq, final_norm_g)
    return (y_prompt.reshape(nb, seq, D_MODEL), y_sample.reshape(db, dseq, D_MODEL)) + new
```

```python
import functools

import jax
import jax.numpy as jnp
import numpy as np
from jax import lax
from jax.experimental import pallas as pl
from jax.experimental.pallas import tpu as pltpu

F32 = jnp.float32
BF16 = jnp.bfloat16
NEG_INF = float("-inf")

D_MODEL = 2048
EPS = 1e-6
ROPE_BASE = 10000.0
GRID_W = 64
CONV_W = 5
CHUNK = 64
SSD_HEADS = 16
SSD_HEAD_DIM = 64
SSD_D_INNER = 1024
SSD_STATE = 128
MLA_HEADS = 8
MLA_NOPE = 128
MLA_ROPE = 64
MLA_V = 128
MLA_RANK = 512
GQA_HEADS = 16
GQA_KV_HEADS = 4
GQA_HEAD_DIM = 64
WINDOW = 128
ML_HEADS = 8
ML_HEAD_DIM = 128
ML_D = 1024
PEER_HEADS = 8
PEER_NKEYS = 128
PEER_N = PEER_NKEYS * PEER_NKEYS
PEER_TOPK = 16

LANES = 128
SUBLANES = 8
VMEM_LIMIT = 56 * 2**20

L0_COLS = 3840
L1_COLS = 5760


def _cp(n_grid, flags=None):
    return pltpu.CompilerParams(dimension_semantics=("arbitrary",) * n_grid, vmem_limit_bytes=VMEM_LIMIT, flags=flags)


def _silu(x):
    return x * jax.nn.sigmoid(x)


def _softplus(x):
    return jnp.maximum(x, 0.0) + jnp.log1p(jnp.exp(-jnp.abs(x)))


def _rms(x):
    return x * lax.rsqrt(jnp.mean(x * x, axis=-1, keepdims=True) + EPS)


def _dot(a, b):
    return jnp.dot(a, b, preferred_element_type=F32)


def _dot_nt(a, b):
    return lax.dot_general(a, b, (((1,), (1,)), ((), ())), preferred_element_type=F32)


def _full(shape):
    nd = len(shape)
    return pl.BlockSpec(shape, lambda *_: (0,) * nd)


def _mod_spec(chunk, tb, mod_base, rows_per_cond):
    def index(i, *_):
        return (mod_base + (i * tb) // rows_per_cond, 0, chunk)
    return pl.BlockSpec((1, 1, D_MODEL), index)


def _ada_kernel(c_ref, w_ref, b_ref, o_ref):
    s = _silu(c_ref[...])
    o_ref[...] = _dot(s.astype(BF16), w_ref[...].astype(BF16)) + b_ref[...]


def _ada(cond8, w, b):
    n = w.shape[1]
    tn = 1024
    return pl.pallas_call(
        _ada_kernel,
        grid=(n // tn,),
        in_specs=[_full((8, D_MODEL)), pl.BlockSpec((D_MODEL, tn), lambda j: (0, j)),
                  pl.BlockSpec((1, tn), lambda j: (0, j))],
        out_specs=pl.BlockSpec((8, tn), lambda j: (0, j)),
        out_shape=jax.ShapeDtypeStruct((8, n), F32),
        compiler_params=_cp(1), name="ada_table",
    )(cond8, w, b.reshape(1, n))


def _proj_kernel(x_ref, g_ref, sc_ref, sh_ref, w_ref, o_ref, h_ref):
    @pl.when(pl.program_id(1) == 0)
    def _():
        h = _rms(x_ref[...]) * g_ref[...] * (1.0 + sc_ref[0]) + sh_ref[0]
        h_ref[...] = h.astype(BF16)
    o_ref[...] = _dot(h_ref[...], w_ref[...])


def _proj(x, g, mod, w, *, tn, mod_base, rows_per_cond):
    n, ncol = x.shape[0], w.shape[1]
    tb = 1024
    return pl.pallas_call(
        _proj_kernel,
        grid=(n // tb, ncol // tn),
        in_specs=[pl.BlockSpec((tb, D_MODEL), lambda i, j: (i, 0)), _full((1, D_MODEL)),
                  _mod_spec(1, tb, mod_base, rows_per_cond), _mod_spec(0, tb, mod_base, rows_per_cond),
                  pl.BlockSpec((D_MODEL, tn), lambda i, j: (0, j))],
        out_specs=pl.BlockSpec((tb, tn), lambda i, j: (i, j)),
        out_shape=jax.ShapeDtypeStruct((n, ncol), F32),
        scratch_shapes=[pltpu.VMEM((tb, D_MODEL), BF16)],
        compiler_params=_cp(2), name="norm_mod_proj",
    )(x, g.reshape(1, D_MODEL), mod, mod, w)


def _outproj_kernel(ya_ref, yb_ref, w_ref, x_ref, g1_ref, o_ref):
    half = ya_ref.shape[1]
    y = _dot(ya_ref[...].astype(BF16), w_ref[0:half, :]) + _dot(yb_ref[...].astype(BF16), w_ref[half:, :])
    o_ref[...] = x_ref[...] + g1_ref[0] * y


def _outproj(ya, yb, w, x, mod, *, mod_base, rows_per_cond):
    n = x.shape[0]
    tb = 512
    return pl.pallas_call(
        _outproj_kernel,
        grid=(n // tb,),
        in_specs=[pl.BlockSpec((tb, ya.shape[1]), lambda i: (i, 0)), pl.BlockSpec((tb, yb.shape[1]), lambda i: (i, 0)),
                  _full(w.shape), pl.BlockSpec((tb, D_MODEL), lambda i: (i, 0)),
                  _mod_spec(2, tb, mod_base, rows_per_cond)],
        out_specs=pl.BlockSpec((tb, D_MODEL), lambda i: (i, 0)),
        out_shape=jax.ShapeDtypeStruct((n, D_MODEL), F32),
        compiler_params=_cp(1), name="out_proj_residual",
    )(ya, yb, w, x, mod)


ROUTE_TB = 512
_CAND_KEEP = (16, 8, 5, 4, 3, 2, 2, 2)


def _sort_pairs(n):
    pairs = []

    def merge(lo, hi, r):
        step = r * 2
        if step < hi - lo:
            merge(lo, hi, step)
            merge(lo + r, hi, step)
            pairs.extend((i, i + r) for i in range(lo + r, hi - r, step))
        else:
            pairs.append((lo, lo + r))

    def sort(lo, hi):
        if hi - lo >= 1:
            mid = lo + (hi - lo) // 2
            sort(lo, mid)
            sort(mid + 1, hi)
            merge(lo, hi, 1)

    sort(0, 15)
    return [(i, j) for i, j in pairs if j < n]


def _top_values(x, count):
    groups = [x[r:r + SUBLANES] for r in range(0, x.shape[0], SUBLANES)]
    n = len(groups)
    for i, j in _sort_pairs(n):
        groups[i], groups[j] = jnp.maximum(groups[i], groups[j]), jnp.minimum(groups[i], groups[j])
    sub = lax.broadcasted_iota(jnp.int32, groups[0].shape, 0)
    tops = []
    for t in range(count):
        head = groups[0]
        m = jnp.max(head, axis=0, keepdims=True)
        tops.append(m)
        if t == count - 1:
            break
        first = jnp.min(jnp.where(head == m, sub, SUBLANES), axis=0, keepdims=True)
        popped = sub == first
        depth = min(count - 1 - t, n)
        for r in range(depth):
            below = groups[r + 1] if r + 1 < n else NEG_INF
            groups[r] = jnp.where(popped, below, groups[r])
    return tops


def _route_kernel(x_ref, g_ref, sc_ref, sh_ref, wqt_ref, keys_ref,
                  hb_ref, th_ref, s2_ref, e2_ref, c1_ref, qt_s, top_s):
    tb = x_ref.shape[0]
    h = _rms(x_ref[...]) * g_ref[...] * (1.0 + sc_ref[0]) + sh_ref[0]
    hb = h.astype(BF16)
    hb_ref[...] = hb
    qt_s[...] = _dot_nt(wqt_ref[...], hb).astype(BF16)

    iota_8 = lax.broadcasted_iota(jnp.int32, (SUBLANES, tb), 0)

    def head(hd, _):
        scores = []
        for c in range(2):
            r0 = pl.multiple_of((hd * 2 + c) * PEER_NKEYS, PEER_NKEYS)
            s = _dot(keys_ref[hd * 2 + c], qt_s[pl.ds(r0, PEER_NKEYS), :])
            scores.append(s)
            for it, m in enumerate(_top_values(s, PEER_TOPK)):
                top_s[c, it:it + 1, :] = m
        a1 = top_s[0]
        a2 = top_s[1]
        pieces = [a1[0:1] + a2, a1[1:2] + a2[0:8]]
        for i in range(2, 8):
            pieces.append(jnp.where(iota_8 < _CAND_KEEP[i], a1[i:i + 1] + a2[0:8], NEG_INF))
        pieces.append(a1[8:16] + a2[0:1])
        cand = jnp.concatenate(pieces, axis=0)
        tau = _top_values(cand, PEER_TOPK)[-1]
        top = a1[0:1] + a2[0:1]
        z = jnp.sum(jnp.where(cand >= tau, jnp.exp(cand - top), 0.0), axis=0, keepdims=True)
        e2 = jnp.exp(scores[1] - a2[0:1])
        c1 = jnp.exp(scores[0] - a1[0:1]) / z
        theta = jnp.full(scores[0].shape, jnp.inf, F32)
        for q in range(PEER_TOPK):
            a2q = a2[q:q + 1]
            phi = jnp.min(jnp.where(a1 + a2q >= tau, a1, jnp.inf), axis=0, keepdims=True)
            theta = jnp.where(scores[0] >= phi, a2q, theta)
        for t in range(tb // LANES):
            ts = slice(t * LANES, (t + 1) * LANES)
            th_ref[hd, t] = theta[:, ts]
            s2_ref[hd, t] = scores[1][:, ts]
            e2_ref[hd, t] = e2[:, ts]
            c1_ref[hd, t] = c1[:, ts]
        return 0

    lax.fori_loop(0, PEER_HEADS, head, 0)


def _route(x, g, mod, wqt, keys, *, mod_base, rows_per_cond):
    n = x.shape[0]
    tb = ROUTE_TB
    ntg = tb // LANES
    hk = (PEER_HEADS, n // LANES, PEER_NKEYS, LANES)
    rspec = pl.BlockSpec((PEER_HEADS, ntg, PEER_NKEYS, LANES), lambda i: (0, i, 0, 0))
    return pl.pallas_call(
        _route_kernel,
        grid=(n // tb,),
        in_specs=[pl.BlockSpec((tb, D_MODEL), lambda i: (i, 0)), _full((1, D_MODEL)),
                  _mod_spec(4, tb, mod_base, rows_per_cond), _mod_spec(3, tb, mod_base, rows_per_cond),
                  _full(wqt.shape), _full(keys.shape)],
        out_specs=[pl.BlockSpec((tb, D_MODEL), lambda i: (i, 0)), rspec, rspec, rspec, rspec],
        out_shape=[jax.ShapeDtypeStruct((n, D_MODEL), BF16)] + [jax.ShapeDtypeStruct(hk, F32)] * 4,
        scratch_shapes=[pltpu.VMEM((PEER_HEADS * 2 * PEER_NKEYS, tb), BF16), pltpu.VMEM((2, PEER_TOPK, tb), F32)],
        compiler_params=_cp(1), name="peer_route",
    )(x, g.reshape(1, D_MODEL), mod, mod, wqt, keys)


PEER_TB = 512
PEER_EC = 1024
PEER_SUB = PEER_EC // 2


def _experts_kernel(hb_ref, th_ref, s2_ref, e2_ref, c1_ref, u0_ref, ub_ref, un_ref, vp_ref, va_ref, vl_ref,
                    x_ref, g2_ref, fg_ref, o_ref, acc_s, st0_s, st1_s, a0_s, a1_s, *, final_norm):
    e = pl.program_id(1)
    tb = hb_ref.shape[0]
    ntg = tb // LANES
    rows_per_sub = PEER_SUB // PEER_NKEYS

    def scores(u_blk, st_s, h):
        res = _dot_nt(u_blk[...], hb_ref[h * 2 * LANES:(h + 1) * 2 * LANES, :])
        st_s[2 * h] = res[:, :LANES]
        st_s[2 * h + 1] = res[:, LANES:]

    def values(vt_blk, act_s, h):
        act = jnp.concatenate([act_s[2 * h], act_s[2 * h + 1]], axis=1)
        hs = slice(h * 2 * LANES, (h + 1) * 2 * LANES)
        for r in range(0, D_MODEL, PEER_SUB):
            acc_s[r:r + PEER_SUB, hs] += _dot(vt_blk[r:r + PEER_SUB, :], act)

    def gates(c, tg, st_s, act_s):
        sub = 32
        for k in range(rows_per_sub):
            row = c * rows_per_sub + k
            for j0 in range(0, PEER_NKEYS, sub):
                js = slice(j0, j0 + sub)
                ks = slice(k * PEER_NKEYS + j0, k * PEER_NKEYS + j0 + sub)
                w = jnp.zeros((sub, LANES), F32)
                for hd in range(PEER_HEADS):
                    chosen = s2_ref[hd, tg, js, :] >= th_ref[hd, tg, row:row + 1, :]
                    w = w + jnp.where(chosen, e2_ref[hd, tg, js, :], 0.0) * c1_ref[hd, tg, row:row + 1, :]
                s = st_s[tg, ks, :]
                act = 0.5 * s * (1.0 + lax.erf(s * np.float32(2.0 ** -0.5)))
                act_s[tg, ks, :] = (act * w).astype(BF16)

    @pl.when(e == 0)
    def _():
        acc_s[...] = jnp.zeros_like(acc_s)
        a1_s[...] = jnp.zeros_like(a1_s)
        scores(u0_ref, st0_s, 0)
        scores(u0_ref, st0_s, 1)

    def phase(c, st_cur, act_cur, u_next, st_next, vt_prev, act_prev):
        for h in range(2):
            scores(u_next, st_next, h)
            gates(c, 2 * h, st_cur, act_cur)
            values(vt_prev, act_prev, h)
            gates(c, 2 * h + 1, st_cur, act_cur)

    phase(0, st0_s, a0_s, ub_ref, st1_s, vp_ref, a1_s)
    phase(1, st1_s, a1_s, un_ref, st0_s, va_ref, a0_s)

    @pl.when(e == pl.num_programs(1) - 1)
    def _():
        values(vl_ref, a1_s, 0)
        values(vl_ref, a1_s, 1)
        y = x_ref[...] + g2_ref[0] * acc_s[...].T
        if final_norm:
            y = _rms(y) * fg_ref[...]
        o_ref[...] = y


def _experts(hb, theta, s2, e2, c1, u, vt, x, mod, fg, *, mod_base, rows_per_cond, final_norm):
    n = x.shape[0]
    tb = PEER_TB
    ntg = tb // LANES
    ne = PEER_N // PEER_EC
    once = dict(pipeline_mode=pl.Buffered(1))
    rspec = pl.BlockSpec((PEER_HEADS, ntg, PEER_NKEYS, LANES), lambda i, e: (0, i, 0, 0), **once)
    rows8 = pl.BlockSpec((PEER_HEADS, ntg, SUBLANES, LANES), lambda i, e: (0, i, e, 0))
    uspec = lambda index: pl.BlockSpec((PEER_SUB, D_MODEL), index)
    vspec = lambda index: pl.BlockSpec((D_MODEL, PEER_SUB), index)
    return pl.pallas_call(
        functools.partial(_experts_kernel, final_norm=final_norm),
        grid=(n // tb, ne),
        in_specs=[pl.BlockSpec((tb, D_MODEL), lambda i, e: (i, 0)), rows8, rspec, rspec, rows8,
                  pl.BlockSpec((PEER_SUB, D_MODEL), lambda i, e: (0, 0), **once),
                  uspec(lambda i, e: (2 * e + 1, 0)),
                  uspec(lambda i, e: (jnp.minimum(2 * e + 2, 2 * ne - 2), 0)),
                  vspec(lambda i, e: (0, jnp.maximum(2 * e - 1, 0))),
                  vspec(lambda i, e: (0, 2 * e)),
                  pl.BlockSpec((D_MODEL, PEER_SUB), lambda i, e: (0, 2 * ne - 1), **once),
                  pl.BlockSpec((tb, D_MODEL), lambda i, e: (i, 0)),
                  _mod_spec(5, tb, mod_base, rows_per_cond), _full((1, D_MODEL))],
        out_specs=pl.BlockSpec((tb, D_MODEL), lambda i, e: (i, 0)),
        out_shape=jax.ShapeDtypeStruct((n, D_MODEL), F32),
        scratch_shapes=[pltpu.VMEM((D_MODEL, tb), F32), pltpu.VMEM((ntg, PEER_SUB, LANES), F32),
                        pltpu.VMEM((ntg, PEER_SUB, LANES), F32), pltpu.VMEM((ntg, PEER_SUB, LANES), BF16),
                        pltpu.VMEM((ntg, PEER_SUB, LANES), BF16)],
        compiler_params=_cp(2), name="peer_experts",
    )(hb, theta, s2, e2, c1, u, u, u, vt, vt, vt, x, mod, fg.reshape(1, D_MODEL))


def _peer(x, g, mod, wqt, keys, u, vt, fg, *, mod_base, rows_per_cond, final_norm):
    hb, theta, s2, e2, c1 = _route(x, g, mod, wqt, keys, mod_base=mod_base, rows_per_cond=rows_per_cond)
    return _experts(hb, theta, s2, e2, c1, u, vt, x, mod, fg, mod_base=mod_base, rows_per_cond=rows_per_cond,
                    final_norm=final_norm)


def _chunk_index(s, nc):
    return jnp.where(s < nc, s, 2 * nc - 1 - s)


def _scan_group(nseq, largest):
    return max(g for g in (2, 4, 8) if g <= largest and nseq % g == 0)


def _scan_specs(nc):
    def cur(col):
        return lambda b, s: (b, _chunk_index(s, nc), col)

    def prev(col):
        return lambda b, s: (b, jnp.maximum(_chunk_index(s, nc) * 8 - 1, 0), col)

    def nxt(col):
        return lambda b, s: (b, jnp.minimum(_chunk_index(s, nc) * 8 + 8, nc * 8 - 1), col)

    def out(col):
        return lambda b, s: (b, jnp.where(s < nc, nc - 1, 2 * nc - 1 - s), col)
    return cur, prev, nxt, out


def _conv_silu(prev, cur, nxt, w_ref, b_ref, first, last):
    win = jnp.concatenate([jnp.where(first, 0.0, prev), cur, jnp.where(last, 0.0, nxt)], axis=1)
    acc = b_ref[...] + win[:, 6:70, :] * w_ref[0:1, :]
    for k in range(1, CONV_W):
        acc = acc + win[:, 6 + k:70 + k, :] * w_ref[k:k + 1, :]
    return _silu(acc)


def _tri(bwd, G):
    r = lax.broadcasted_iota(jnp.int32, (G, CHUNK, CHUNK), 1)
    c = lax.broadcasted_iota(jnp.int32, (G, CHUNK, CHUNK), 2)
    lower = (c <= r).astype(F32)
    upper = (c >= r).astype(F32)
    return jnp.where(bwd, upper, lower), jnp.where(bwd, lower, upper)


def _bdot(a, b, contract, precision=None):
    return lax.dot_general(a, b, (((contract[0],), (contract[1],)), ((0,), (0,))), preferred_element_type=F32,
                           precision=precision)


def _ssd_kernel(z_ref, xp_ref, xc_ref, xn_ref, bp_ref, bc_ref, bn_ref, sm_ref, dtt_ref,
                cwx_ref, cbx_ref, cwb_ref, cbb_ref, arow_ref, acol_ref, brow_ref, bcol_ref, drow_ref, ng_ref,
                *rest, nc, zero_init):
    st0_ref = None if zero_init else rest[0]
    y_ref, st_ref, yf_s, h_s = rest[0 if zero_init else 1:]
    s = pl.program_id(1)
    bwd = s >= nc
    cidx = _chunk_index(s, nc)
    first = cidx == 0
    last = cidx == nc - 1
    nh = SSD_HEADS

    def initial(direction):
        return jnp.zeros(h_s.shape, F32) if zero_init else st0_ref[:, direction]

    @pl.when(s == 0)
    def _():
        h_s[...] = initial(0)

    @pl.when(s == nc)
    def _():
        st_ref[:, 0] = h_s[...]
        h_s[...] = initial(1)

    G = z_ref.shape[0]
    tsel, tsel_t = _tri(bwd, G)
    mask = tsel > 0.0
    r0 = pl.multiple_of(cidx * CHUNK, CHUNK)
    hi = lax.Precision.HIGHEST

    xs = _conv_silu(xp_ref[...], xc_ref[...], xn_ref[...], cwx_ref, cbx_ref, first, last)
    bcv = _conv_silu(bp_ref[...], bc_ref[...], bn_ref[...], cwb_ref, cbb_ref, first, last)
    dt_all = _softplus(sm_ref[:, :, 64:96] + brow_ref[...])
    dtt_all = _softplus(dtt_ref[:, 0] + bcol_ref[...])
    a_all = dt_all * (-jnp.exp(arow_ref[...]))
    at_all = dtt_all * (-jnp.exp(acol_ref[...]))
    dt = jnp.where(bwd, dt_all[:, :, nh:2 * nh], dt_all[:, :, 0:nh])
    a = jnp.where(bwd, a_all[:, :, nh:2 * nh], a_all[:, :, 0:nh])
    at = jnp.where(bwd, at_all[:, nh:2 * nh, :], at_all[:, 0:nh, :])
    cum = _bdot(tsel, a, (2, 1), hi)
    cum_t = _bdot(at, tsel_t, (2, 1), hi)
    tot = jnp.where(bwd, cum[:, 0:1, :], cum[:, CHUNK - 1:CHUNK, :])

    ys = []
    for g in range(2):
        bg = bcv[:, :, g * SSD_STATE:(g + 1) * SSD_STATE]
        cg = bcv[:, :, 256 + g * SSD_STATE:256 + (g + 1) * SSD_STATE]
        cg16 = cg.astype(BF16)
        cb = _bdot(cg16, bg.astype(BF16), (2, 2))
        for hh in range(nh // 2):
            h = g * (nh // 2) + hh
            cq = cum[:, :, h:h + 1]
            decay = jnp.exp(jnp.where(mask, cq - cum_t[:, h:h + 1, :], NEG_INF))
            xd = xs[:, :, h * SSD_HEAD_DIM:(h + 1) * SSD_HEAD_DIM] * dt[:, :, h:h + 1]
            xd16 = xd.astype(BF16)
            hprev = h_s[:, h]
            y = (_bdot((cb * decay).astype(BF16), xd16, (2, 1))
                 + _bdot(cg16, hprev.astype(BF16), (2, 2)) * jnp.exp(cq))
            ys.append(y)
            th = tot[:, :, h:h + 1]
            bw = bg * jnp.exp(th - cq)
            h_s[:, h] = hprev * jnp.exp(th) + _bdot(xd16, bw.astype(BF16), (1, 1))
    y = jnp.concatenate(ys, axis=2)

    @pl.when(jnp.logical_not(bwd))
    def _():
        yf_s[:, pl.ds(r0, CHUNK), :] = y

    @pl.when(bwd)
    def _():
        yt = (yf_s[:, pl.ds(r0, CHUNK), :] + y + drow_ref[...] * xs) * _silu(z_ref[...])
        y_ref[...] = _rms(yt) * ng_ref[...]

    @pl.when(s == 2 * nc - 1)
    def _():
        st_ref[:, 1] = h_s[...]


def _ssd(p0, dtt, cwx, cbx, cwb, cbb, a_log, dt_bias, d_skip, norm_g, st0, *, L):
    nseq, nc = p0.shape[0], L // CHUNK
    zero_init = st0 is None
    G = _scan_group(nseq, 4)
    cur, prev, nxt, out = _scan_specs(nc)
    st_shape = (G, 2, SSD_HEADS, SSD_HEAD_DIM, SSD_STATE)
    states = [] if zero_init else [st0]
    row = lambda v: v.reshape(1, -1)
    col = lambda v: v.reshape(-1, 1)
    small = [cwx, row(cbx), cwb, row(cbb), row(a_log), col(a_log), row(dt_bias), col(dt_bias),
             row(jnp.repeat(d_skip, SSD_HEAD_DIM)), row(norm_g)]
    blk = lambda rows, width, index: pl.BlockSpec((G, rows, width), index)
    return pl.pallas_call(
        functools.partial(_ssd_kernel, nc=nc, zero_init=zero_init),
        grid=(nseq // G, 2 * nc),
        in_specs=[blk(CHUNK, 1024, cur(0)),
                  blk(8, 1024, prev(1)), blk(CHUNK, 1024, cur(1)), blk(8, 1024, nxt(1)),
                  blk(8, 512, prev(4)), blk(CHUNK, 512, cur(4)), blk(8, 512, nxt(4)),
                  blk(CHUNK, LANES, cur(28)),
                  pl.BlockSpec((G, 1, 32, CHUNK), lambda b, s: (b, _chunk_index(s, nc), 0, 0))]
        + [_full(v.shape) for v in small]
        + [pl.BlockSpec(st_shape, lambda b, s: (b, 0, 0, 0, 0)) for _ in states],
        out_specs=[blk(CHUNK, SSD_D_INNER, out(0)),
                   pl.BlockSpec(st_shape, lambda b, s: (b, 0, 0, 0, 0))],
        out_shape=[jax.ShapeDtypeStruct((nseq, L, SSD_D_INNER), F32),
                   jax.ShapeDtypeStruct((nseq,) + st_shape[1:], F32)],
        scratch_shapes=[pltpu.VMEM((G, L, SSD_D_INNER), F32), pltpu.VMEM(st_shape[:1] + st_shape[2:], F32)],
        compiler_params=_cp(2), name="ssd_mixer",
    )(p0, p0, p0, p0, p0, p0, p0, p0, dtt, *small, *states)


def _rope_tables(L):
    t = jnp.arange(L)
    r = (t // GRID_W).astype(F32)
    c = (t % GRID_W).astype(F32)
    nf = 16
    inv = ROPE_BASE ** (-jnp.arange(nf, dtype=F32) / nf)
    ang = jnp.concatenate([r[:, None] * inv, c[:, None] * inv], axis=-1)
    ang = jnp.concatenate([ang, ang], axis=-1)
    return jnp.cos(ang), jnp.sin(ang)


def _rope128(x, cos, sin):
    lane = lax.broadcasted_iota(jnp.int32, x.shape, 1)
    rot = jnp.where(lane % 64 < 32, -pltpu.roll(x, 96, 1), pltpu.roll(x, 32, 1))
    return x * cos + rot * sin


def _pad_lanes(x):
    return jnp.concatenate([x, jnp.zeros_like(x)], axis=1)


MLA_QB = 256


def _mla_kernel(*refs, L, Lc, rope, emit):
    refs = list(refs)
    qa_ref, kva_ref, sm_ref, qg_ref, kvg_ref, wqn_ref, wqr_ref, wkk_ref, wkv_ref = refs[:9]
    pos = 9
    if Lc:
        cckv_ref, ckpe_ref = refs[pos:pos + 2]
        pos += 2
    if rope:
        cos_ref, sin_ref = refs[pos:pos + 2]
        pos += 2
    y_ref = refs[pos]
    pos += 1
    if emit:
        ckv_ref = refs[pos]
        pos += 1
    qn_s, qr_s, kn_s, v_s, kpe_s = refs[pos:]
    qb = pl.program_id(1)
    scale = np.float32((MLA_NOPE + MLA_ROPE) ** -0.5)

    @pl.when(qb == 0)
    def _():
        ckv = _rms(kva_ref[...]) * kvg_ref[...]
        if emit:
            ckv_ref[...] = ckv
        kpe = _pad_lanes(sm_ref[:, 0:MLA_ROPE])
        if rope:
            kpe = _rope128(kpe, cos_ref[...], sin_ref[...])
        if Lc:
            ckv = jnp.concatenate([cckv_ref[0], ckv], axis=0)
            kpe = jnp.concatenate([_pad_lanes(ckpe_ref[0]), kpe], axis=0)
        c16 = ckv.astype(BF16)
        kn_s[...] = _dot(c16, wkk_ref[...]).astype(BF16)
        v_s[...] = _dot(c16, wkv_ref[...]).astype(BF16)
        kpe_s[...] = kpe.astype(BF16)

    qn = (_rms(qa_ref[...]) * qg_ref[...]).astype(BF16)
    qn_s[...] = _dot(qn, wqn_ref[...]).astype(BF16)
    qr = _dot(qn, wqr_ref[...])
    if rope:
        q0 = pl.multiple_of(qb * MLA_QB, MLA_QB)
        cos = cos_ref[pl.ds(q0, MLA_QB), :]
        sin = sin_ref[pl.ds(q0, MLA_QB), :]
        for h in range(MLA_HEADS):
            qr_s[:, h * LANES:(h + 1) * LANES] = _rope128(qr[:, h * LANES:(h + 1) * LANES], cos, sin).astype(BF16)
    else:
        qr_s[...] = qr.astype(BF16)

    def head(h, _):
        c0 = h * LANES if isinstance(h, int) else pl.multiple_of(h * LANES, LANES)
        s = (_dot_nt(qn_s[:, pl.ds(c0, LANES)], kn_s[:, pl.ds(c0, LANES)])
             + _dot_nt(qr_s[:, pl.ds(c0, LANES)], kpe_s[...])) * scale
        p = jnp.exp(s - jnp.max(s, axis=-1, keepdims=True))
        p = p / jnp.sum(p, axis=-1, keepdims=True)
        y_ref[:, pl.ds(c0, LANES)] = _dot(p.astype(BF16), v_s[:, pl.ds(c0, LANES)])
        return 0

    if L + Lc <= 512:
        for h in range(MLA_HEADS):
            head(h, 0)
    else:
        lax.fori_loop(0, MLA_HEADS, head, 0)


def _mla(p0, qg, kvg, wqn, wqr, wkk, wkv, cache_ckv, cache_kpe, rope_tabs, *, L):
    n = p0.shape[0]
    nseq, nqb = n // L, L // MLA_QB
    Lc = 0 if cache_ckv is None else cache_ckv.shape[1]
    rope = rope_tabs is not None
    emit = cache_ckv is None
    Lk = L + Lc
    args = [p0, p0, p0, qg.reshape(1, -1), kvg.reshape(1, -1), wqn, wqr, wkk, wkv]
    specs = [pl.BlockSpec((MLA_QB, MLA_RANK), lambda b, q: (b * nqb + q, 5)),
             pl.BlockSpec((L, MLA_RANK), lambda b, q: (b, 6)),
             pl.BlockSpec((L, LANES), lambda b, q: (b, 28)),
             _full((1, MLA_RANK)), _full((1, MLA_RANK)), _full(wqn.shape), _full(wqr.shape), _full(wkk.shape),
             _full(wkv.shape)]
    if Lc:
        args += [cache_ckv, cache_kpe]
        specs += [pl.BlockSpec((1, Lc, MLA_RANK), lambda b, q: (b, 0, 0)),
                  pl.BlockSpec((1, Lc, MLA_ROPE), lambda b, q: (b, 0, 0))]
    if rope:
        args += list(rope_tabs)
        specs += [_full((L, LANES)), _full((L, LANES))]
    out_specs = [pl.BlockSpec((MLA_QB, 1024), lambda b, q: (b * nqb + q, 0))]
    out_shape = [jax.ShapeDtypeStruct((n, 1024), F32)]
    if emit:
        out_specs.append(pl.BlockSpec((L, MLA_RANK), lambda b, q: (b, 0)))
        out_shape.append(jax.ShapeDtypeStruct((n, MLA_RANK), F32))
    return pl.pallas_call(
        functools.partial(_mla_kernel, L=L, Lc=Lc, rope=rope, emit=emit),
        grid=(nseq, nqb), in_specs=specs, out_specs=out_specs, out_shape=out_shape,
        scratch_shapes=[pltpu.VMEM((MLA_QB, 1024), BF16), pltpu.VMEM((MLA_QB, 1024), BF16),
                        pltpu.VMEM((Lk, 1024), BF16), pltpu.VMEM((Lk, 1024), BF16), pltpu.VMEM((Lk, LANES), BF16)],
        compiler_params=_cp(2), name="mla_attention",
    )(*args)


def _gqa_ctx_kernel(q_ref, k_ref, v_ref, sink_ref, y_ref):
    q = q_ref[...]
    k = k_ref[...].astype(BF16)
    v = v_ref[...].astype(BF16)
    scale = np.float32(GQA_HEAD_DIM ** -0.5)
    group = GQA_HEADS // GQA_KV_HEADS
    outs = []
    for h in range(GQA_HEADS):
        kh = h // group
        ks = slice(kh * GQA_HEAD_DIM, (kh + 1) * GQA_HEAD_DIM)
        s = _dot_nt(q[:, h * GQA_HEAD_DIM:(h + 1) * GQA_HEAD_DIM].astype(BF16), k[:, ks]) * scale
        m = jnp.maximum(jnp.max(s, axis=-1, keepdims=True), sink_ref[h])
        p = jnp.exp(s - m)
        p = p / (jnp.sum(p, axis=-1, keepdims=True) + jnp.exp(sink_ref[h] - m))
        outs.append(_dot(p.astype(BF16), v[:, ks]))
    y_ref[...] = jnp.concatenate(outs, axis=1)


def _gqa_ctx(p1, sink, *, L):
    n = p1.shape[0]
    return pl.pallas_call(
        _gqa_ctx_kernel,
        grid=(n // L,),
        in_specs=[pl.BlockSpec((L, 1024), lambda b: (b, 2)), pl.BlockSpec((L, 256), lambda b: (b, 20)),
                  pl.BlockSpec((L, 256), lambda b: (b, 21)), pl.BlockSpec(memory_space=pltpu.SMEM)],
        out_specs=pl.BlockSpec((L, 1024), lambda b: (b, 0)),
        out_shape=jax.ShapeDtypeStruct((n, 1024), F32),
        compiler_params=_cp(1), name="gqa_context",
    )(p1, p1, p1, sink)


GQA_QB = 128
GQA_SPAN = GQA_QB + 2 * WINDOW


def _gqa_win_kernel(q_ref, k_ref, v_ref, ck_ref, cv_ref, cos_ref, sin_ref, sink_ref, y_ref, kp_s, vp_s, *, L):
    qb = pl.program_id(1)
    scale = np.float32(GQA_HEAD_DIM ** -0.5)
    group = GQA_HEADS // GQA_KV_HEADS

    @pl.when(qb == 0)
    def _():
        zeros = jnp.zeros((WINDOW, 256), BF16)
        kp_s[0:WINDOW, :] = zeros
        kp_s[WINDOW + L:, :] = zeros
        vp_s[0:WINDOW, :] = zeros
        vp_s[WINDOW + L:, :] = zeros
        for j in range(2):
            cs = slice(j * LANES, (j + 1) * LANES)
            kp_s[WINDOW:WINDOW + L, cs] = _rope128(k_ref[:, cs], cos_ref[...], sin_ref[...]).astype(BF16)
        vp_s[WINDOW:WINDOW + L, :] = v_ref[...].astype(BF16)

    q0 = pl.multiple_of(qb * GQA_QB, GQA_QB)
    cos = cos_ref[pl.ds(q0, GQA_QB), :]
    sin = sin_ref[pl.ds(q0, GQA_QB), :]
    kw = kp_s[pl.ds(q0, GQA_SPAN), :]
    vw = vp_s[pl.ds(q0, GQA_SPAN), :]
    kc = ck_ref[0].astype(BF16)
    vc = cv_ref[0].astype(BF16)
    qpos = q0 + lax.broadcasted_iota(jnp.int32, (GQA_QB, GQA_SPAN), 0)
    kpos = q0 - WINDOW + lax.broadcasted_iota(jnp.int32, (GQA_QB, GQA_SPAN), 1)
    bias = jnp.where(kpos < 0, NEG_INF, jnp.where(kpos >= L, NEG_INF, jnp.where(jnp.abs(qpos - kpos) <= WINDOW, 0.0, NEG_INF)))
    outs = []
    for j in range(GQA_HEADS * GQA_HEAD_DIM // LANES):
        qj = _rope128(q_ref[:, j * LANES:(j + 1) * LANES], cos, sin).astype(BF16)
        for half in range(2):
            h = 2 * j + half
            kh = h // group
            ks = slice(kh * GQA_HEAD_DIM, (kh + 1) * GQA_HEAD_DIM)
            qh = qj[:, half * GQA_HEAD_DIM:(half + 1) * GQA_HEAD_DIM]
            s_loc = _dot_nt(qh, kw[:, ks]) * scale + bias
            s_ctx = _dot_nt(qh, kc[:, ks]) * scale
            m = jnp.maximum(jnp.maximum(jnp.max(s_loc, axis=-1, keepdims=True), jnp.max(s_ctx, axis=-1, keepdims=True)),
                            sink_ref[h])
            p_loc = jnp.exp(s_loc - m)
            p_ctx = jnp.exp(s_ctx - m)
            inv = 1.0 / (jnp.sum(p_loc, axis=-1, keepdims=True) + jnp.sum(p_ctx, axis=-1, keepdims=True)
                         + jnp.exp(sink_ref[h] - m))
            outs.append(_dot((p_loc * inv).astype(BF16), vw[:, ks]) + _dot((p_ctx * inv).astype(BF16), vc[:, ks]))
    y_ref[...] = jnp.concatenate(outs, axis=1)


def _gqa_win(p1, cache_k, cache_v, cos, sin, sink, *, L):
    n = p1.shape[0]
    nseq, nqb = n // L, L // GQA_QB
    Lc = cache_k.shape[1]
    return pl.pallas_call(
        functools.partial(_gqa_win_kernel, L=L),
        grid=(nseq, nqb),
        in_specs=[pl.BlockSpec((GQA_QB, 1024), lambda b, q: (b * nqb + q, 2)),
                  pl.BlockSpec((L, 256), lambda b, q: (b, 20)), pl.BlockSpec((L, 256), lambda b, q: (b, 21)),
                  pl.BlockSpec((1, Lc, 256), lambda b, q: (b, 0, 0)), pl.BlockSpec((1, Lc, 256), lambda b, q: (b, 0, 0)),
                  _full((L, LANES)), _full((L, LANES)), pl.BlockSpec(memory_space=pltpu.SMEM)],
        out_specs=pl.BlockSpec((GQA_QB, 1024), lambda b, q: (b * nqb + q, 0)),
        out_shape=jax.ShapeDtypeStruct((n, 1024), F32),
        scratch_shapes=[pltpu.VMEM((L + 2 * WINDOW, 256), BF16), pltpu.VMEM((L + 2 * WINDOW, 256), BF16)],
        compiler_params=_cp(2), name="gqa_window",
    )(p1, p1, p1, cache_k, cache_v, cos, sin, sink)


def _mlstm_kernel(qp_ref, qc_ref, qn_ref, v_ref, o_ref, sm_ref, gt_ref, cw_ref, cb_ref, brow_ref, bcol_ref, ng_ref,
                  *rest, nc, zero_init):
    init_refs = () if zero_init else rest[:3]
    y_ref, cst_ref, nst_ref, mst_ref, hf_s, c_s, n_s, m_s = rest[len(init_refs):]
    s = pl.program_id(1)
    bwd = s >= nc
    cidx = _chunk_index(s, nc)
    nh = ML_HEADS
    hd = ML_HEAD_DIM

    def load_initial(direction):
        for k, dst in enumerate((c_s, n_s, m_s)):
            dst[...] = jnp.zeros(dst.shape, F32) if zero_init else init_refs[k][:, direction]

    @pl.when(s == 0)
    def _():
        load_initial(0)

    @pl.when(s == nc)
    def _():
        cst_ref[:, 0] = c_s[...]
        nst_ref[:, 0] = n_s[...]
        mst_ref[:, 0] = m_s[...]
        load_initial(1)

    G = qc_ref.shape[0]
    tsel, tsel_t = _tri(bwd, G)
    mask = tsel > 0.0
    kscale = np.float32(hd ** -0.5)
    r0 = pl.multiple_of(cidx * CHUNK, CHUNK)
    hi = lax.Precision.HIGHEST

    qk = _conv_silu(qp_ref[...], qc_ref[...], qn_ref[...], cw_ref, cb_ref, cidx == 0, cidx == nc - 1)
    gates = sm_ref[:, :, 0:4 * nh] + brow_ref[...]
    gates_t = gt_ref[:, 0] + bcol_ref[...]
    li = jnp.where(bwd, gates[:, :, nh:2 * nh], gates[:, :, 0:nh])
    li_t = jnp.where(bwd, gates_t[:, nh:2 * nh, :], gates_t[:, 0:nh, :])
    lf = -_softplus(-jnp.where(bwd, gates[:, :, 3 * nh:4 * nh], gates[:, :, 2 * nh:3 * nh]))
    lf_t = -_softplus(-jnp.where(bwd, gates_t[:, 3 * nh:4 * nh, :], gates_t[:, 2 * nh:3 * nh, :]))
    bc = _bdot(tsel, lf, (2, 1), hi)
    bc_t = _bdot(lf_t, tsel_t, (2, 1), hi)
    tot = jnp.where(bwd, bc[:, 0:1, :], bc[:, CHUNK - 1:CHUNK, :])

    hs = []
    for h in range(nh):
        cs = slice(h * hd, (h + 1) * hd)
        q = qk[:, :, cs]
        k = qk[:, :, ML_D + h * hd:ML_D + (h + 1) * hd] * kscale
        q16, k16 = q.astype(BF16), k.astype(BF16)
        v = v_ref[:, :, cs]
        bq = bc[:, :, h:h + 1]
        m_prev = m_s[:, h:h + 1, 0:1]
        log_d = jnp.where(mask, bq - bc_t[:, h:h + 1, :] + li_t[:, h:h + 1, :], NEG_INF)
        log_inter = bq + m_prev
        m_out = jnp.maximum(log_inter, jnp.max(log_d, axis=2, keepdims=True))
        sd = _bdot(q16, k16, (2, 2)) * jnp.exp(log_d - m_out)
        w_inter = jnp.exp(log_inter - m_out)
        c_prev = c_s[:, h]
        n_prev = n_s[:, h:h + 1, :]
        num = _bdot(sd.astype(BF16), v.astype(BF16), (2, 1)) + w_inter * _bdot(q16, c_prev.astype(BF16), (2, 2))
        den = jnp.sum(sd, axis=2, keepdims=True) + w_inter * jnp.sum(q * n_prev, axis=2, keepdims=True)
        hs.append(num / jnp.maximum(jnp.abs(den), jnp.exp(-m_out)))
        th = tot[:, :, h:h + 1]
        end_inter = th + m_prev
        end_intra = th - bq + li[:, :, h:h + 1]
        m_new = jnp.maximum(end_inter, jnp.max(end_intra, axis=1, keepdims=True))
        w_c = jnp.exp(end_inter - m_new)
        w_k = jnp.exp(end_intra - m_new)
        c_s[:, h] = w_c * c_prev + _bdot((v * w_k).astype(BF16), k16, (1, 1))
        n_s[:, h:h + 1, :] = w_c * n_prev + jnp.sum(k * w_k, axis=1, keepdims=True)
        m_s[:, h:h + 1, :] = jnp.broadcast_to(m_new, (G, 1, hd))
    hcat = jnp.concatenate(hs, axis=2)

    @pl.when(jnp.logical_not(bwd))
    def _():
        hf_s[:, pl.ds(r0, CHUNK), :] = hcat

    @pl.when(bwd)
    def _():
        tot_h = hf_s[:, pl.ds(r0, CHUNK), :] + hcat
        normed = jnp.concatenate([_rms(tot_h[:, :, h * hd:(h + 1) * hd]) for h in range(nh)], axis=2) * ng_ref[...]
        y_ref[...] = normed * jax.nn.sigmoid(o_ref[...])

    @pl.when(s == 2 * nc - 1)
    def _():
        cst_ref[:, 1] = c_s[...]
        nst_ref[:, 1] = n_s[...]
        mst_ref[:, 1] = m_s[...]


def _mlstm(p1, gt, conv_w, conv_b, ig_b, fg_b, norm_g, c0, n0, m0, *, L):
    nseq, nc = p1.shape[0], L // CHUNK
    zero_init = c0 is None
    G = _scan_group(nseq, 8 if zero_init else 4)
    cur, prev, nxt, out = _scan_specs(nc)
    cshape = (G, 2, ML_HEADS, ML_HEAD_DIM, ML_HEAD_DIM)
    nshape = (G, 2, ML_HEADS, ML_HEAD_DIM)
    states = [] if zero_init else [c0, n0, m0]
    gb = jnp.concatenate([ig_b.reshape(-1), fg_b.reshape(-1)])
    small = [conv_w, conv_b.reshape(1, -1), gb.reshape(1, -1), gb.reshape(-1, 1), norm_g.reshape(1, -1)]
    st = lambda shape: pl.BlockSpec(shape, lambda b, s: (b,) + (0,) * (len(shape) - 1))
    blk = lambda rows, width, index: pl.BlockSpec((G, rows, width), index)
    return pl.pallas_call(
        functools.partial(_mlstm_kernel, nc=nc, zero_init=zero_init),
        grid=(nseq // G, 2 * nc),
        in_specs=[blk(8, 2048, prev(0)), blk(CHUNK, 2048, cur(0)), blk(8, 2048, nxt(0)),
                  blk(CHUNK, 1024, cur(3)), blk(CHUNK, 1024, cur(4)), blk(CHUNK, LANES, cur(44)),
                  pl.BlockSpec((G, 1, 32, CHUNK), lambda b, s: (b, _chunk_index(s, nc), 0, 0))]
        + [_full(v.shape) for v in small] + ([] if zero_init else [st(cshape), st(nshape), st(nshape)]),
        out_specs=[blk(CHUNK, ML_D, out(0)), st(cshape), st(nshape), st(nshape)],
        out_shape=[jax.ShapeDtypeStruct((nseq, L, ML_D), F32), jax.ShapeDtypeStruct((nseq,) + cshape[1:], F32),
                   jax.ShapeDtypeStruct((nseq,) + nshape[1:], F32), jax.ShapeDtypeStruct((nseq,) + nshape[1:], F32)],
        scratch_shapes=[pltpu.VMEM((G, L, ML_D), F32), pltpu.VMEM(cshape[:1] + cshape[2:], F32),
                        pltpu.VMEM(nshape[:1] + nshape[2:], F32), pltpu.VMEM(nshape[:1] + nshape[2:], F32)],
        compiler_params=_cp(2), name="mlstm_mixer",
    )(p1, p1, p1, p1, p1, p1, gt, *small, *states)


def _prep_layer0(w_in, w_qb, w_kvb, w_out):
    pad = jnp.zeros((D_MODEL, L0_COLS - 3680), F32)
    w = jnp.concatenate([w_in[:, :2560], w_in[:, 2592:3680], w_in[:, 2560:2592], pad], axis=1).astype(BF16)
    q3 = w_qb.reshape(MLA_RANK, MLA_HEADS, MLA_NOPE + MLA_ROPE)
    wqn = q3[:, :, :MLA_NOPE].reshape(MLA_RANK, -1).astype(BF16)
    wqr = jnp.concatenate([q3[:, :, MLA_NOPE:], jnp.zeros((MLA_RANK, MLA_HEADS, LANES - MLA_ROPE), F32)], axis=-1)
    wqr = wqr.reshape(MLA_RANK, -1).astype(BF16)
    k3 = w_kvb.reshape(MLA_RANK, MLA_HEADS, MLA_NOPE + MLA_V)
    wkk = k3[:, :, :MLA_NOPE].reshape(MLA_RANK, -1).astype(BF16)
    wkv = k3[:, :, MLA_NOPE:].reshape(MLA_RANK, -1).astype(BF16)
    return w, wqn, wqr, wkk, wkv, w_out.astype(BF16)


def _prep_layer1(w_in, w_out):
    pad = jnp.zeros((D_MODEL, L1_COLS - 5664), F32)
    w = jnp.concatenate([w_in[:, 1536:3584], w_in[:, 0:1024], w_in[:, 3584:5632], w_in[:, 1024:1536],
                         w_in[:, 5632:5664], pad], axis=1).astype(BF16)
    return w, w_out.astype(BF16)


def _prep_peer(wq, keys, u, v):
    return (wq.T.astype(BF16), keys.reshape(PEER_HEADS * 2, PEER_NKEYS, -1).astype(BF16), u.astype(BF16),
            v.T.astype(BF16))


def _chunk_transposed(cols, L):
    return cols.reshape(-1, L // CHUNK, CHUNK, cols.shape[1]).transpose(0, 1, 3, 2)


def _trunk(x, L, mod0, mod1, w0, w1, peer0, peer1, p, ctx, rope_tabs, mod_base, rows_per_cond, fg):
    kw = dict(mod_base=mod_base, rows_per_cond=rows_per_cond)
    w_in0, wqn, wqr, wkk, wkv, w_out0 = w0
    w_in1, w_out1 = w1
    nseq = x.shape[0] // L

    p0 = _proj(x, p['l0_norm1_g'], mod0, w_in0, tn=1280, **kw)
    cw, cb = p['l0_ssd_conv_w'], p['l0_ssd_conv_b']
    y_ssd, ssd_state = _ssd(p0.reshape(nseq, L, -1), _chunk_transposed(p0[:, 3648:3680], L), cw[:, :1024], cb[:1024],
                            cw[:, 1024:], cb[1024:], p['l0_ssd_A_log'], p['l0_ssd_dt_bias'], p['l0_ssd_D'],
                            p['l0_ssd_norm_g'], None if ctx is None else ctx[0], L=L)
    y_ssd = y_ssd.reshape(nseq * L, -1)
    mla_tabs = None
    if rope_tabs is not None:
        z64 = jnp.zeros_like(rope_tabs[0])
        mla_tabs = (jnp.concatenate([rope_tabs[0], z64], axis=1), jnp.concatenate([rope_tabs[1], z64], axis=1))
    mla_out = _mla(p0, p['l0_mla_q_norm_g'], p['l0_mla_kv_norm_g'], wqn, wqr, wkk, wkv,
                   None if ctx is None else ctx[1], None if ctx is None else ctx[2], mla_tabs, L=L)
    x = _outproj(y_ssd, mla_out[0], w_out0, x, mod0, **kw)
    x = _peer(x, p['l0_norm2_g'], mod0, *peer0, fg, final_norm=False, **kw)

    p1 = _proj(x, p['l1_norm1_g'], mod1, w_in1, tn=1152, **kw)
    if ctx is None:
        y_gqa = _gqa_ctx(p1, p['l1_gqa_sink'], L=L)
        c0 = n0 = m0 = None
    else:
        gqa_tabs = tuple(jnp.concatenate([t, t], axis=1) for t in rope_tabs)
        y_gqa = _gqa_win(p1, ctx[3].reshape(nseq, -1, 256), ctx[4].reshape(nseq, -1, 256), *gqa_tabs,
                         p['l1_gqa_sink'], L=L)
        c0, n0 = ctx[5], ctx[6]
        m0 = jnp.broadcast_to(ctx[7][..., None], ctx[7].shape + (ML_HEAD_DIM,))
    y_ml, mc, mn, mm = _mlstm(p1.reshape(nseq, L, -1), _chunk_transposed(p1[:, 5632:5664], L), p['l1_ml_conv_w'],
                              p['l1_ml_conv_b'], p['l1_ml_ig_b'], p['l1_ml_fg_b'], p['l1_ml_norm_g'], c0, n0, m0, L=L)
    y_ml = y_ml.reshape(nseq * L, -1)
    x = _outproj(y_gqa, y_ml, w_out1, x, mod1, **kw)
    y = _peer(x, p['l1_norm2_g'], mod1, *peer1, fg, final_norm=True, **kw)
    new = None
    if ctx is None:
        new = (ssd_state, mla_out[1].reshape(nseq, L, MLA_RANK), p0[:, 3584:3648].reshape(nseq, L, MLA_ROPE),
               p1[:, 5120:5376].reshape(nseq, L, GQA_KV_HEADS, GQA_HEAD_DIM),
               p1[:, 5376:5632].reshape(nseq, L, GQA_KV_HEADS, GQA_HEAD_DIM), mc, mn, mm[..., 0])
    return y, new


def kernel(x_prompt, x_sample, state_l0_ssd, cache_l0_mla_ckv, cache_l0_mla_kpe, cache_l1_gqa_k, cache_l1_gqa_v, state_l1_mlstm_C, state_l1_mlstm_n, state_l1_mlstm_m, c, c_ctx, final_norm_g, l0_ada_w, l0_ada_b, l0_norm1_g, l0_norm2_g, l0_w_in, l0_ssd_conv_w, l0_ssd_conv_b, l0_ssd_A_log, l0_ssd_dt_bias, l0_ssd_D, l0_ssd_norm_g, l0_mla_q_norm_g, l0_mla_w_qb, l0_mla_kv_norm_g, l0_mla_w_kvb, l0_w_out, l0_peer_wq, l0_peer_keys, l0_peer_u, l0_peer_v, l1_ada_w, l1_ada_b, l1_norm1_g, l1_norm2_g, l1_w_in, l1_gqa_sink, l1_ml_conv_w, l1_ml_conv_b, l1_ml_ig_b, l1_ml_fg_b, l1_ml_norm_g, l1_w_out, l1_peer_wq, l1_peer_keys, l1_peer_u, l1_peer_v):
    p = dict(l0_norm1_g=l0_norm1_g, l0_norm2_g=l0_norm2_g, l0_ssd_conv_w=l0_ssd_conv_w, l0_ssd_conv_b=l0_ssd_conv_b,
             l0_ssd_A_log=l0_ssd_A_log, l0_ssd_dt_bias=l0_ssd_dt_bias, l0_ssd_D=l0_ssd_D, l0_ssd_norm_g=l0_ssd_norm_g,
             l0_mla_q_norm_g=l0_mla_q_norm_g, l0_mla_kv_norm_g=l0_mla_kv_norm_g, l1_norm1_g=l1_norm1_g,
             l1_norm2_g=l1_norm2_g, l1_gqa_sink=l1_gqa_sink, l1_ml_conv_w=l1_ml_conv_w, l1_ml_conv_b=l1_ml_conv_b,
             l1_ml_ig_b=l1_ml_ig_b, l1_ml_fg_b=l1_ml_fg_b, l1_ml_norm_g=l1_ml_norm_g)
    nb, seq = x_prompt.shape[:2]
    db, dseq = x_sample.shape[:2]
    assert db <= 7 and seq % MLA_QB == 0 and dseq % MLA_QB == 0

    cond8 = jnp.zeros((8, D_MODEL), F32).at[0].set(c_ctx).at[1:1 + db].set(c)
    mod0 = _ada(cond8, l0_ada_w, l0_ada_b).reshape(8, 1, 6 * D_MODEL)
    mod1 = _ada(cond8, l1_ada_w, l1_ada_b).reshape(8, 1, 6 * D_MODEL)
    w0 = _prep_layer0(l0_w_in, l0_mla_w_qb, l0_mla_w_kvb, l0_w_out)
    w1 = _prep_layer1(l1_w_in, l1_w_out)
    peer0 = _prep_peer(l0_peer_wq, l0_peer_keys, l0_peer_u, l0_peer_v)
    peer1 = _prep_peer(l1_peer_wq, l1_peer_keys, l1_peer_u, l1_peer_v)

    y_prompt, new = _trunk(x_prompt.reshape(nb * seq, D_MODEL), seq, mod0, mod1, w0, w1, peer0, peer1, p, None, None,
                           0, nb * seq, final_norm_g)
    ctx = (state_l0_ssd, cache_l0_mla_ckv, cache_l0_mla_kpe, cache_l1_gqa_k, cache_l1_gqa_v, state_l1_mlstm_C,
           state_l1_mlstm_n, state_l1_mlstm_m)
    y_sample, _ = _trunk(x_sample.reshape(db * dseq, D_MODEL), dseq, mod0, mod1, w0, w1, peer0, peer1, p, ctx,
                         _rope_tables(dseq), 1, dseq, final_norm_g)
    return (y_prompt.reshape(nb, seq, D_MODEL), y_sample.reshape(db, dseq, D_MODEL)) + new
```

```python
import functools

import jax
import jax.numpy as jnp
import numpy as np
from jax import lax
from jax.experimental import pallas as pl
from jax.experimental.pallas import tpu as pltpu

F32 = jnp.float32
BF16 = jnp.bfloat16
NEG_INF = float("-inf")

D_MODEL = 2048
EPS = 1e-6
ROPE_BASE = 10000.0
GRID_W = 64
CONV_W = 5
CHUNK = 64
SSD_HEADS = 16
SSD_HEAD_DIM = 64
SSD_D_INNER = 1024
SSD_STATE = 128
MLA_HEADS = 8
MLA_NOPE = 128
MLA_ROPE = 64
MLA_V = 128
MLA_RANK = 512
GQA_HEADS = 16
GQA_KV_HEADS = 4
GQA_HEAD_DIM = 64
WINDOW = 128
ML_HEADS = 8
ML_HEAD_DIM = 128
ML_D = 1024
PEER_HEADS = 8
PEER_NKEYS = 128
PEER_N = PEER_NKEYS * PEER_NKEYS
PEER_TOPK = 16

LANES = 128
SUBLANES = 8
VMEM_LIMIT = 56 * 2**20

L0_COLS = 3840
L1_COLS = 5760


def _cp(n_grid, flags=None):
    return pltpu.CompilerParams(dimension_semantics=("arbitrary",) * n_grid, vmem_limit_bytes=VMEM_LIMIT, flags=flags)


def _silu(x):
    return x * jax.nn.sigmoid(x)


def _softplus(x):
    return jnp.maximum(x, 0.0) + jnp.log1p(jnp.exp(-jnp.abs(x)))


def _rms(x):
    return x * lax.rsqrt(jnp.mean(x * x, axis=-1, keepdims=True) + EPS)


def _dot(a, b):
    return jnp.dot(a, b, preferred_element_type=F32)


def _dot_nt(a, b):
    return lax.dot_general(a, b, (((1,), (1,)), ((), ())), preferred_element_type=F32)


def _full(shape):
    nd = len(shape)
    return pl.BlockSpec(shape, lambda *_: (0,) * nd)


def _mod_spec(chunk, tb, mod_base, rows_per_cond):
    def index(i, *_):
        return (mod_base + (i * tb) // rows_per_cond, 0, chunk)
    return pl.BlockSpec((1, 1, D_MODEL), index)


def _ada_kernel(c_ref, w_ref, b_ref, o_ref):
    s = _silu(c_ref[...])
    o_ref[...] = _dot(s.astype(BF16), w_ref[...].astype(BF16)) + b_ref[...]


def _ada(cond8, w, b):
    n = w.shape[1]
    tn = 1024
    return pl.pallas_call(
        _ada_kernel,
        grid=(n // tn,),
        in_specs=[_full((8, D_MODEL)), pl.BlockSpec((D_MODEL, tn), lambda j: (0, j)),
                  pl.BlockSpec((1, tn), lambda j: (0, j))],
        out_specs=pl.BlockSpec((8, tn), lambda j: (0, j)),
        out_shape=jax.ShapeDtypeStruct((8, n), F32),
        compiler_params=_cp(1), name="ada_table",
    )(cond8, w, b.reshape(1, n))


def _proj_kernel(x_ref, g_ref, sc_ref, sh_ref, w_ref, o_ref, h_ref):
    @pl.when(pl.program_id(1) == 0)
    def _():
        h = _rms(x_ref[...]) * g_ref[...] * (1.0 + sc_ref[0]) + sh_ref[0]
        h_ref[...] = h.astype(BF16)
    o_ref[...] = _dot(h_ref[...], w_ref[...])


def _proj(x, g, mod, w, *, tn, mod_base, rows_per_cond):
    n, ncol = x.shape[0], w.shape[1]
    tb = 1024
    return pl.pallas_call(
        _proj_kernel,
        grid=(n // tb, ncol // tn),
        in_specs=[pl.BlockSpec((tb, D_MODEL), lambda i, j: (i, 0)), _full((1, D_MODEL)),
                  _mod_spec(1, tb, mod_base, rows_per_cond), _mod_spec(0, tb, mod_base, rows_per_cond),
                  pl.BlockSpec((D_MODEL, tn), lambda i, j: (0, j))],
        out_specs=pl.BlockSpec((tb, tn), lambda i, j: (i, j)),
        out_shape=jax.ShapeDtypeStruct((n, ncol), F32),
        scratch_shapes=[pltpu.VMEM((tb, D_MODEL), BF16)],
        compiler_params=_cp(2), name="norm_mod_proj",
    )(x, g.reshape(1, D_MODEL), mod, mod, w)


def _outproj_kernel(ya_ref, yb_ref, w_ref, x_ref, g1_ref, o_ref):
    half = ya_ref.shape[1]
    y = _dot(ya_ref[...].astype(BF16), w_ref[0:half, :]) + _dot(yb_ref[...].astype(BF16), w_ref[half:, :])
    o_ref[...] = x_ref[...] + g1_ref[0] * y


def _outproj(ya, yb, w, x, mod, *, mod_base, rows_per_cond):
    n = x.shape[0]
    tb = 512
    return pl.pallas_call(
        _outproj_kernel,
        grid=(n // tb,),
        in_specs=[pl.BlockSpec((tb, ya.shape[1]), lambda i: (i, 0)), pl.BlockSpec((tb, yb.shape[1]), lambda i: (i, 0)),
                  _full(w.shape), pl.BlockSpec((tb, D_MODEL), lambda i: (i, 0)),
                  _mod_spec(2, tb, mod_base, rows_per_cond)],
        out_specs=pl.BlockSpec((tb, D_MODEL), lambda i: (i, 0)),
        out_shape=jax.ShapeDtypeStruct((n, D_MODEL), F32),
        compiler_params=_cp(1), name="out_proj_residual",
    )(ya, yb, w, x, mod)


ROUTE_TB = 512
_CAND_KEEP = (16, 8, 5, 4, 3, 2, 2, 2)


def _sort_pairs(n):
    pairs = []

    def merge(lo, hi, r):
        step = r * 2
        if step < hi - lo:
            merge(lo, hi, step)
            merge(lo + r, hi, step)
            pairs.extend((i, i + r) for i in range(lo + r, hi - r, step))
        else:
            pairs.append((lo, lo + r))

    def sort(lo, hi):
        if hi - lo >= 1:
            mid = lo + (hi - lo) // 2
            sort(lo, mid)
            sort(mid + 1, hi)
            merge(lo, hi, 1)

    sort(0, 15)
    return [(i, j) for i, j in pairs if j < n]


def _top_values(x, count):
    groups = [x[r:r + SUBLANES] for r in range(0, x.shape[0], SUBLANES)]
    n = len(groups)
    for i, j in _sort_pairs(n):
        groups[i], groups[j] = jnp.maximum(groups[i], groups[j]), jnp.minimum(groups[i], groups[j])
    sub = lax.broadcasted_iota(jnp.int32, groups[0].shape, 0)
    tops = []
    for t in range(count):
        head = groups[0]
        m = jnp.max(head, axis=0, keepdims=True)
        tops.append(m)
        if t == count - 1:
            break
        first = jnp.min(jnp.where(head == m, sub, SUBLANES), axis=0, keepdims=True)
        popped = sub == first
        depth = min(count - 1 - t, n)
        for r in range(depth):
            below = groups[r + 1] if r + 1 < n else NEG_INF
            groups[r] = jnp.where(popped, below, groups[r])
    return tops


def _route_kernel(x_ref, g_ref, sc_ref, sh_ref, wqt_ref, keys_ref,
                  hb_ref, th_ref, s2_ref, e2_ref, c1_ref, qt_s, top_s):
    tb = x_ref.shape[0]
    h = _rms(x_ref[...]) * g_ref[...] * (1.0 + sc_ref[0]) + sh_ref[0]
    hb = h.astype(BF16)
    hb_ref[...] = hb
    qt_s[...] = _dot_nt(wqt_ref[...], hb).astype(BF16)

    iota_8 = lax.broadcasted_iota(jnp.int32, (SUBLANES, tb), 0)

    def head(hd, _):
        scores = []
        for c in range(2):
            r0 = pl.multiple_of((hd * 2 + c) * PEER_NKEYS, PEER_NKEYS)
            s = _dot(keys_ref[hd * 2 + c], qt_s[pl.ds(r0, PEER_NKEYS), :])
            scores.append(s)
            for it, m in enumerate(_top_values(s, PEER_TOPK)):
                top_s[c, it:it + 1, :] = m
        a1 = top_s[0]
        a2 = top_s[1]
        pieces = [a1[0:1] + a2, a1[1:2] + a2[0:8]]
        for i in range(2, 8):
            pieces.append(jnp.where(iota_8 < _CAND_KEEP[i], a1[i:i + 1] + a2[0:8], NEG_INF))
        pieces.append(a1[8:16] + a2[0:1])
        cand = jnp.concatenate(pieces, axis=0)
        tau = _top_values(cand, PEER_TOPK)[-1]
        top = a1[0:1] + a2[0:1]
        z = jnp.sum(jnp.where(cand >= tau, jnp.exp(cand - top), 0.0), axis=0, keepdims=True)
        e2 = jnp.exp(scores[1] - a2[0:1])
        c1 = jnp.exp(scores[0] - a1[0:1]) / z
        theta = jnp.full(scores[0].shape, jnp.inf, F32)
        for q in range(PEER_TOPK):
            a2q = a2[q:q + 1]
            phi = jnp.min(jnp.where(a1 + a2q >= tau, a1, jnp.inf), axis=0, keepdims=True)
            theta = jnp.where(scores[0] >= phi, a2q, theta)
        for t in range(tb // LANES):
            ts = slice(t * LANES, (t + 1) * LANES)
            th_ref[hd, t] = theta[:, ts]
            s2_ref[hd, t] = scores[1][:, ts]
            e2_ref[hd, t] = e2[:, ts]
            c1_ref[hd, t] = c1[:, ts]
        return 0

    lax.fori_loop(0, PEER_HEADS, head, 0)


def _route(x, g, mod, wqt, keys, *, mod_base, rows_per_cond):
    n = x.shape[0]
    tb = ROUTE_TB
    ntg = tb // LANES
    hk = (PEER_HEADS, n // LANES, PEER_NKEYS, LANES)
    rspec = pl.BlockSpec((PEER_HEADS, ntg, PEER_NKEYS, LANES), lambda i: (0, i, 0, 0))
    return pl.pallas_call(
        _route_kernel,
        grid=(n // tb,),
        in_specs=[pl.BlockSpec((tb, D_MODEL), lambda i: (i, 0)), _full((1, D_MODEL)),
                  _mod_spec(4, tb, mod_base, rows_per_cond), _mod_spec(3, tb, mod_base, rows_per_cond),
                  _full(wqt.shape), _full(keys.shape)],
        out_specs=[pl.BlockSpec((tb, D_MODEL), lambda i: (i, 0)), rspec, rspec, rspec, rspec],
        out_shape=[jax.ShapeDtypeStruct((n, D_MODEL), BF16)] + [jax.ShapeDtypeStruct(hk, F32)] * 4,
        scratch_shapes=[pltpu.VMEM((PEER_HEADS * 2 * PEER_NKEYS, tb), BF16), pltpu.VMEM((2, PEER_TOPK, tb), F32)],
        compiler_params=_cp(1), name="peer_route",
    )(x, g.reshape(1, D_MODEL), mod, mod, wqt, keys)


PEER_TB = 512
PEER_EC = 1024
PEER_SUB = PEER_EC // 2


def _experts_kernel(hb_ref, th_ref, s2_ref, e2_ref, c1_ref, u0_ref, ub_ref, un_ref, vp_ref, va_ref, vl_ref,
                    x_ref, g2_ref, fg_ref, o_ref, acc_s, st0_s, st1_s, a0_s, a1_s, *, final_norm):
    e = pl.program_id(1)
    tb = hb_ref.shape[0]
    ntg = tb // LANES
    rows_per_sub = PEER_SUB // PEER_NKEYS

    def scores(u_blk, st_s, h):
        res = _dot_nt(u_blk[...], hb_ref[h * 2 * LANES:(h + 1) * 2 * LANES, :])
        st_s[2 * h] = res[:, :LANES]
        st_s[2 * h + 1] = res[:, LANES:]

    def values(vt_blk, act_s, h):
        act = jnp.concatenate([act_s[2 * h], act_s[2 * h + 1]], axis=1)
        for r in range(0, D_MODEL, PEER_SUB):
            res = _dot(vt_blk[r:r + PEER_SUB, :], act)
            acc_s[2 * h, r:r + PEER_SUB, :] += res[:, :LANES]
            acc_s[2 * h + 1, r:r + PEER_SUB, :] += res[:, LANES:]

    def gates(c, tg, st_s, act_s):
        sub = 32
        for k in range(rows_per_sub):
            row = c * rows_per_sub + k
            for j0 in range(0, PEER_NKEYS, sub):
                js = slice(j0, j0 + sub)
                ks = slice(k * PEER_NKEYS + j0, k * PEER_NKEYS + j0 + sub)
                w = jnp.zeros((sub, LANES), F32)
                for hd in range(PEER_HEADS):
                    chosen = s2_ref[hd, tg, js, :] >= th_ref[hd, tg, row:row + 1, :]
                    w = w + jnp.where(chosen, e2_ref[hd, tg, js, :], 0.0) * c1_ref[hd, tg, row:row + 1, :]
                s = st_s[tg, ks, :]
                act = 0.5 * s * (1.0 + lax.erf(s * np.float32(2.0 ** -0.5)))
                act_s[tg, ks, :] = (act * w).astype(BF16)

    @pl.when(e == 0)
    def _():
        acc_s[...] = jnp.zeros_like(acc_s)
        a1_s[...] = jnp.zeros_like(a1_s)
        scores(u0_ref, st0_s, 0)
        scores(u0_ref, st0_s, 1)

    def phase(c, st_cur, act_cur, u_next, st_next, vt_prev, act_prev):
        for h in range(2):
            scores(u_next, st_next, h)
            gates(c, 2 * h, st_cur, act_cur)
            values(vt_prev, act_prev, h)
            gates(c, 2 * h + 1, st_cur, act_cur)

    phase(0, st0_s, a0_s, ub_ref, st1_s, vp_ref, a1_s)
    phase(1, st1_s, a1_s, un_ref, st0_s, va_ref, a0_s)

    @pl.when(e == pl.num_programs(1) - 1)
    def _():
        values(vl_ref, a1_s, 0)
        values(vl_ref, a1_s, 1)
        for tg in range(ntg):
            ts = slice(tg * LANES, (tg + 1) * LANES)
            y = x_ref[ts, :] + g2_ref[0] * acc_s[tg].T
            if final_norm:
                y = _rms(y) * fg_ref[...]
            o_ref[ts, :] = y


def _experts(hb, theta, s2, e2, c1, u, vt, x, mod, fg, *, mod_base, rows_per_cond, final_norm):
    n = x.shape[0]
    tb = PEER_TB
    ntg = tb // LANES
    ne = PEER_N // PEER_EC
    once = dict(pipeline_mode=pl.Buffered(1))
    rspec = pl.BlockSpec((PEER_HEADS, ntg, PEER_NKEYS, LANES), lambda i, e: (0, i, 0, 0), **once)
    rows8 = pl.BlockSpec((PEER_HEADS, ntg, SUBLANES, LANES), lambda i, e: (0, i, e, 0))
    uspec = lambda index: pl.BlockSpec((PEER_SUB, D_MODEL), index)
    vspec = lambda index: pl.BlockSpec((D_MODEL, PEER_SUB), index)
    return pl.pallas_call(
        functools.partial(_experts_kernel, final_norm=final_norm),
        grid=(n // tb, ne),
        in_specs=[pl.BlockSpec((tb, D_MODEL), lambda i, e: (i, 0)), rows8, rspec, rspec, rows8,
                  pl.BlockSpec((PEER_SUB, D_MODEL), lambda i, e: (0, 0), **once),
                  uspec(lambda i, e: (2 * e + 1, 0)),
                  uspec(lambda i, e: (jnp.minimum(2 * e + 2, 2 * ne - 2), 0)),
                  vspec(lambda i, e: (0, jnp.maximum(2 * e - 1, 0))),
                  vspec(lambda i, e: (0, 2 * e)),
                  pl.BlockSpec((D_MODEL, PEER_SUB), lambda i, e: (0, 2 * ne - 1), **once),
                  pl.BlockSpec((tb, D_MODEL), lambda i, e: (i, 0)),
                  _mod_spec(5, tb, mod_base, rows_per_cond), _full((1, D_MODEL))],
        out_specs=pl.BlockSpec((tb, D_MODEL), lambda i, e: (i, 0)),
        out_shape=jax.ShapeDtypeStruct((n, D_MODEL), F32),
        scratch_shapes=[pltpu.VMEM((ntg, D_MODEL, LANES), F32), pltpu.VMEM((ntg, PEER_SUB, LANES), F32),
                        pltpu.VMEM((ntg, PEER_SUB, LANES), F32), pltpu.VMEM((ntg, PEER_SUB, LANES), BF16),
                        pltpu.VMEM((ntg, PEER_SUB, LANES), BF16)],
        compiler_params=_cp(2), name="peer_experts",
    )(hb, theta, s2, e2, c1, u, u, u, vt, vt, vt, x, mod, fg.reshape(1, D_MODEL))


def _peer(x, g, mod, wqt, keys, u, vt, fg, *, mod_base, rows_per_cond, final_norm):
    hb, theta, s2, e2, c1 = _route(x, g, mod, wqt, keys, mod_base=mod_base, rows_per_cond=rows_per_cond)
    return _experts(hb, theta, s2, e2, c1, u, vt, x, mod, fg, mod_base=mod_base, rows_per_cond=rows_per_cond,
                    final_norm=final_norm)


def _chunk_index(s, nc):
    return jnp.where(s < nc, s, 2 * nc - 1 - s)


def _scan_group(nseq, largest):
    return max(g for g in (2, 4, 8) if g <= largest and nseq % g == 0)


def _scan_specs(nc):
    def cur(col):
        return lambda b, s: (b, _chunk_index(s, nc), col)

    def prev(col):
        return lambda b, s: (b, jnp.maximum(_chunk_index(s, nc) * 8 - 1, 0), col)

    def nxt(col):
        return lambda b, s: (b, jnp.minimum(_chunk_index(s, nc) * 8 + 8, nc * 8 - 1), col)

    def out(col):
        return lambda b, s: (b, jnp.where(s < nc, nc - 1, 2 * nc - 1 - s), col)
    return cur, prev, nxt, out


def _conv_silu(prev, cur, nxt, w_ref, b_ref, first, last):
    win = jnp.concatenate([jnp.where(first, 0.0, prev), cur, jnp.where(last, 0.0, nxt)], axis=1)
    acc = b_ref[...] + win[:, 6:70, :] * w_ref[0:1, :]
    for k in range(1, CONV_W):
        acc = acc + win[:, 6 + k:70 + k, :] * w_ref[k:k + 1, :]
    return _silu(acc)


def _tri(bwd, G):
    r = lax.broadcasted_iota(jnp.int32, (G, CHUNK, CHUNK), 1)
    c = lax.broadcasted_iota(jnp.int32, (G, CHUNK, CHUNK), 2)
    lower = (c <= r).astype(F32)
    upper = (c >= r).astype(F32)
    return jnp.where(bwd, upper, lower), jnp.where(bwd, lower, upper)


def _bdot(a, b, contract, precision=None):
    return lax.dot_general(a, b, (((contract[0],), (contract[1],)), ((0,), (0,))), preferred_element_type=F32,
                           precision=precision)


def _ssd_kernel(z_ref, xp_ref, xc_ref, xn_ref, bp_ref, bc_ref, bn_ref, sm_ref, dtt_ref,
                cwx_ref, cbx_ref, cwb_ref, cbb_ref, arow_ref, acol_ref, brow_ref, bcol_ref, drow_ref, ng_ref,
                *rest, nc, zero_init):
    st0_ref = None if zero_init else rest[0]
    y_ref, st_ref, yf_s, h_s = rest[0 if zero_init else 1:]
    s = pl.program_id(1)
    bwd = s >= nc
    cidx = _chunk_index(s, nc)
    first = cidx == 0
    last = cidx == nc - 1
    nh = SSD_HEADS

    def initial(direction):
        return jnp.zeros(h_s.shape, F32) if zero_init else st0_ref[:, direction]

    @pl.when(s == 0)
    def _():
        h_s[...] = initial(0)

    @pl.when(s == nc)
    def _():
        st_ref[:, 0] = h_s[...]
        h_s[...] = initial(1)

    G = z_ref.shape[0]
    tsel, tsel_t = _tri(bwd, G)
    mask = tsel > 0.0
    r0 = pl.multiple_of(cidx * CHUNK, CHUNK)
    hi = lax.Precision.HIGHEST

    xs = _conv_silu(xp_ref[...], xc_ref[...], xn_ref[...], cwx_ref, cbx_ref, first, last)
    bcv = _conv_silu(bp_ref[...], bc_ref[...], bn_ref[...], cwb_ref, cbb_ref, first, last)
    dt_all = _softplus(sm_ref[:, :, 64:96] + brow_ref[...])
    dtt_all = _softplus(dtt_ref[:, 0] + bcol_ref[...])
    a_all = dt_all * (-jnp.exp(arow_ref[...]))
    at_all = dtt_all * (-jnp.exp(acol_ref[...]))
    dt = jnp.where(bwd, dt_all[:, :, nh:2 * nh], dt_all[:, :, 0:nh])
    a = jnp.where(bwd, a_all[:, :, nh:2 * nh], a_all[:, :, 0:nh])
    at = jnp.where(bwd, at_all[:, nh:2 * nh, :], at_all[:, 0:nh, :])
    cum = _bdot(tsel, a, (2, 1), hi)
    cum_t = _bdot(at, tsel_t, (2, 1), hi)
    tot = jnp.where(bwd, cum[:, 0:1, :], cum[:, CHUNK - 1:CHUNK, :])

    ys = []
    for g in range(2):
        bg = bcv[:, :, g * SSD_STATE:(g + 1) * SSD_STATE]
        cg = bcv[:, :, 256 + g * SSD_STATE:256 + (g + 1) * SSD_STATE]
        cg16 = cg.astype(BF16)
        cb = _bdot(cg16, bg.astype(BF16), (2, 2))
        for hh in range(nh // 2):
            h = g * (nh // 2) + hh
            cq = cum[:, :, h:h + 1]
            decay = jnp.exp(jnp.where(mask, cq - cum_t[:, h:h + 1, :], NEG_INF))
            xd = xs[:, :, h * SSD_HEAD_DIM:(h + 1) * SSD_HEAD_DIM] * dt[:, :, h:h + 1]
            xd16 = xd.astype(BF16)
            hprev = h_s[:, h]
            y = (_bdot((cb * decay).astype(BF16), xd16, (2, 1))
                 + _bdot(cg16, hprev.astype(BF16), (2, 2)) * jnp.exp(cq))
            ys.append(y)
            th = tot[:, :, h:h + 1]
            bw = bg * jnp.exp(th - cq)
            h_s[:, h] = hprev * jnp.exp(th) + _bdot(xd16, bw.astype(BF16), (1, 1))
    y = jnp.concatenate(ys, axis=2)

    @pl.when(jnp.logical_not(bwd))
    def _():
        yf_s[:, pl.ds(r0, CHUNK), :] = y

    @pl.when(bwd)
    def _():
        yt = (yf_s[:, pl.ds(r0, CHUNK), :] + y + drow_ref[...] * xs) * _silu(z_ref[...])
        y_ref[...] = _rms(yt) * ng_ref[...]

    @pl.when(s == 2 * nc - 1)
    def _():
        st_ref[:, 1] = h_s[...]


def _ssd(p0, dtt, cwx, cbx, cwb, cbb, a_log, dt_bias, d_skip, norm_g, st0, *, L):
    nseq, nc = p0.shape[0], L // CHUNK
    zero_init = st0 is None
    G = _scan_group(nseq, 4)
    cur, prev, nxt, out = _scan_specs(nc)
    st_shape = (G, 2, SSD_HEADS, SSD_HEAD_DIM, SSD_STATE)
    states = [] if zero_init else [st0]
    row = lambda v: v.reshape(1, -1)
    col = lambda v: v.reshape(-1, 1)
    small = [cwx, row(cbx), cwb, row(cbb), row(a_log), col(a_log), row(dt_bias), col(dt_bias),
             row(jnp.repeat(d_skip, SSD_HEAD_DIM)), row(norm_g)]
    blk = lambda rows, width, index: pl.BlockSpec((G, rows, width), index)
    return pl.pallas_call(
        functools.partial(_ssd_kernel, nc=nc, zero_init=zero_init),
        grid=(nseq // G, 2 * nc),
        in_specs=[blk(CHUNK, 1024, cur(0)),
                  blk(8, 1024, prev(1)), blk(CHUNK, 1024, cur(1)), blk(8, 1024, nxt(1)),
                  blk(8, 512, prev(4)), blk(CHUNK, 512, cur(4)), blk(8, 512, nxt(4)),
                  blk(CHUNK, LANES, cur(28)),
                  pl.BlockSpec((G, 1, 32, CHUNK), lambda b, s: (b, _chunk_index(s, nc), 0, 0))]
        + [_full(v.shape) for v in small]
        + [pl.BlockSpec(st_shape, lambda b, s: (b, 0, 0, 0, 0)) for _ in states],
        out_specs=[blk(CHUNK, SSD_D_INNER, out(0)),
                   pl.BlockSpec(st_shape, lambda b, s: (b, 0, 0, 0, 0))],
        out_shape=[jax.ShapeDtypeStruct((nseq, L, SSD_D_INNER), F32),
                   jax.ShapeDtypeStruct((nseq,) + st_shape[1:], F32)],
        scratch_shapes=[pltpu.VMEM((G, L, SSD_D_INNER), F32), pltpu.VMEM(st_shape[:1] + st_shape[2:], F32)],
        compiler_params=_cp(2), name="ssd_mixer",
    )(p0, p0, p0, p0, p0, p0, p0, p0, dtt, *small, *states)


def _rope_tables(L):
    t = jnp.arange(L)
    r = (t // GRID_W).astype(F32)
    c = (t % GRID_W).astype(F32)
    nf = 16
    inv = ROPE_BASE ** (-jnp.arange(nf, dtype=F32) / nf)
    ang = jnp.concatenate([r[:, None] * inv, c[:, None] * inv], axis=-1)
    ang = jnp.concatenate([ang, ang], axis=-1)
    return jnp.cos(ang), jnp.sin(ang)


def _rope128(x, cos, sin):
    lane = lax.broadcasted_iota(jnp.int32, x.shape, 1)
    rot = jnp.where(lane % 64 < 32, -pltpu.roll(x, 96, 1), pltpu.roll(x, 32, 1))
    return x * cos + rot * sin


def _pad_lanes(x):
    return jnp.concatenate([x, jnp.zeros_like(x)], axis=1)


MLA_QB = 256


def _mla_kernel(*refs, L, Lc, rope, emit):
    refs = list(refs)
    qa_ref, kva_ref, sm_ref, qg_ref, kvg_ref, wqn_ref, wqr_ref, wkk_ref, wkv_ref = refs[:9]
    pos = 9
    if Lc:
        cckv_ref, ckpe_ref = refs[pos:pos + 2]
        pos += 2
    if rope:
        cos_ref, sin_ref = refs[pos:pos + 2]
        pos += 2
    y_ref = refs[pos]
    pos += 1
    if emit:
        ckv_ref = refs[pos]
        pos += 1
    qn_s, qr_s, kn_s, v_s, kpe_s = refs[pos:]
    qb = pl.program_id(1)
    scale = np.float32((MLA_NOPE + MLA_ROPE) ** -0.5)

    @pl.when(qb == 0)
    def _():
        ckv = _rms(kva_ref[...]) * kvg_ref[...]
        if emit:
            ckv_ref[...] = ckv
        kpe = _pad_lanes(sm_ref[:, 0:MLA_ROPE])
        if rope:
            kpe = _rope128(kpe, cos_ref[...], sin_ref[...])
        if Lc:
            ckv = jnp.concatenate([cckv_ref[0], ckv], axis=0)
            kpe = jnp.concatenate([_pad_lanes(ckpe_ref[0]), kpe], axis=0)
        c16 = ckv.astype(BF16)
        kn_s[...] = _dot(c16, wkk_ref[...]).astype(BF16)
        v_s[...] = _dot(c16, wkv_ref[...]).astype(BF16)
        kpe_s[...] = kpe.astype(BF16)

    qn = (_rms(qa_ref[...]) * qg_ref[...]).astype(BF16)
    qn_s[...] = _dot(qn, wqn_ref[...]).astype(BF16)
    qr = _dot(qn, wqr_ref[...])
    if rope:
        q0 = pl.multiple_of(qb * MLA_QB, MLA_QB)
        cos = cos_ref[pl.ds(q0, MLA_QB), :]
        sin = sin_ref[pl.ds(q0, MLA_QB), :]
        for h in range(MLA_HEADS):
            qr_s[:, h * LANES:(h + 1) * LANES] = _rope128(qr[:, h * LANES:(h + 1) * LANES], cos, sin).astype(BF16)
    else:
        qr_s[...] = qr.astype(BF16)

    def head(h, _):
        c0 = h * LANES if isinstance(h, int) else pl.multiple_of(h * LANES, LANES)
        s = (_dot_nt(qn_s[:, pl.ds(c0, LANES)], kn_s[:, pl.ds(c0, LANES)])
             + _dot_nt(qr_s[:, pl.ds(c0, LANES)], kpe_s[...])) * scale
        p = jnp.exp(s - jnp.max(s, axis=-1, keepdims=True))
        p = p / jnp.sum(p, axis=-1, keepdims=True)
        y_ref[:, pl.ds(c0, LANES)] = _dot(p.astype(BF16), v_s[:, pl.ds(c0, LANES)])
        return 0

    if L + Lc <= 512:
        for h in range(MLA_HEADS):
            head(h, 0)
    else:
        lax.fori_loop(0, MLA_HEADS, head, 0)


def _mla(p0, qg, kvg, wqn, wqr, wkk, wkv, cache_ckv, cache_kpe, rope_tabs, *, L):
    n = p0.shape[0]
    nseq, nqb = n // L, L // MLA_QB
    Lc = 0 if cache_ckv is None else cache_ckv.shape[1]
    rope = rope_tabs is not None
    emit = cache_ckv is None
    Lk = L + Lc
    args = [p0, p0, p0, qg.reshape(1, -1), kvg.reshape(1, -1), wqn, wqr, wkk, wkv]
    specs = [pl.BlockSpec((MLA_QB, MLA_RANK), lambda b, q: (b * nqb + q, 5)),
             pl.BlockSpec((L, MLA_RANK), lambda b, q: (b, 6)),
             pl.BlockSpec((L, LANES), lambda b, q: (b, 28)),
             _full((1, MLA_RANK)), _full((1, MLA_RANK)), _full(wqn.shape), _full(wqr.shape), _full(wkk.shape),
             _full(wkv.shape)]
    if Lc:
        args += [cache_ckv, cache_kpe]
        specs += [pl.BlockSpec((1, Lc, MLA_RANK), lambda b, q: (b, 0, 0)),
                  pl.BlockSpec((1, Lc, MLA_ROPE), lambda b, q: (b, 0, 0))]
    if rope:
        args += list(rope_tabs)
        specs += [_full((L, LANES)), _full((L, LANES))]
    out_specs = [pl.BlockSpec((MLA_QB, 1024), lambda b, q: (b * nqb + q, 0))]
    out_shape = [jax.ShapeDtypeStruct((n, 1024), F32)]
    if emit:
        out_specs.append(pl.BlockSpec((L, MLA_RANK), lambda b, q: (b, 0)))
        out_shape.append(jax.ShapeDtypeStruct((n, MLA_RANK), F32))
    return pl.pallas_call(
        functools.partial(_mla_kernel, L=L, Lc=Lc, rope=rope, emit=emit),
        grid=(nseq, nqb), in_specs=specs, out_specs=out_specs, out_shape=out_shape,
        scratch_shapes=[pltpu.VMEM((MLA_QB, 1024), BF16), pltpu.VMEM((MLA_QB, 1024), BF16),
                        pltpu.VMEM((Lk, 1024), BF16), pltpu.VMEM((Lk, 1024), BF16), pltpu.VMEM((Lk, LANES), BF16)],
        compiler_params=_cp(2), name="mla_attention",
    )(*args)


def _gqa_ctx_kernel(q_ref, k_ref, v_ref, sink_ref, y_ref):
    q = q_ref[...]
    k = k_ref[...].astype(BF16)
    v = v_ref[...].astype(BF16)
    scale = np.float32(GQA_HEAD_DIM ** -0.5)
    group = GQA_HEADS // GQA_KV_HEADS
    outs = []
    for h in range(GQA_HEADS):
        kh = h // group
        ks = slice(kh * GQA_HEAD_DIM, (kh + 1) * GQA_HEAD_DIM)
        s = _dot_nt(q[:, h * GQA_HEAD_DIM:(h + 1) * GQA_HEAD_DIM].astype(BF16), k[:, ks]) * scale
        m = jnp.maximum(jnp.max(s, axis=-1, keepdims=True), sink_ref[h])
        p = jnp.exp(s - m)
        p = p / (jnp.sum(p, axis=-1, keepdims=True) + jnp.exp(sink_ref[h] - m))
        outs.append(_dot(p.astype(BF16), v[:, ks]))
    y_ref[...] = jnp.concatenate(outs, axis=1)


def _gqa_ctx(p1, sink, *, L):
    n = p1.shape[0]
    return pl.pallas_call(
        _gqa_ctx_kernel,
        grid=(n // L,),
        in_specs=[pl.BlockSpec((L, 1024), lambda b: (b, 2)), pl.BlockSpec((L, 256), lambda b: (b, 20)),
                  pl.BlockSpec((L, 256), lambda b: (b, 21)), pl.BlockSpec(memory_space=pltpu.SMEM)],
        out_specs=pl.BlockSpec((L, 1024), lambda b: (b, 0)),
        out_shape=jax.ShapeDtypeStruct((n, 1024), F32),
        compiler_params=_cp(1), name="gqa_context",
    )(p1, p1, p1, sink)


GQA_QB = 128
GQA_SPAN = GQA_QB + 2 * WINDOW


def _gqa_win_kernel(q_ref, k_ref, v_ref, ck_ref, cv_ref, cos_ref, sin_ref, sink_ref, y_ref, kp_s, vp_s, *, L):
    qb = pl.program_id(1)
    scale = np.float32(GQA_HEAD_DIM ** -0.5)
    group = GQA_HEADS // GQA_KV_HEADS

    @pl.when(qb == 0)
    def _():
        zeros = jnp.zeros((WINDOW, 256), BF16)
        kp_s[0:WINDOW, :] = zeros
        kp_s[WINDOW + L:, :] = zeros
        vp_s[0:WINDOW, :] = zeros
        vp_s[WINDOW + L:, :] = zeros
        for j in range(2):
            cs = slice(j * LANES, (j + 1) * LANES)
            kp_s[WINDOW:WINDOW + L, cs] = _rope128(k_ref[:, cs], cos_ref[...], sin_ref[...]).astype(BF16)
        vp_s[WINDOW:WINDOW + L, :] = v_ref[...].astype(BF16)

    q0 = pl.multiple_of(qb * GQA_QB, GQA_QB)
    cos = cos_ref[pl.ds(q0, GQA_QB), :]
    sin = sin_ref[pl.ds(q0, GQA_QB), :]
    kw = kp_s[pl.ds(q0, GQA_SPAN), :]
    vw = vp_s[pl.ds(q0, GQA_SPAN), :]
    kc = ck_ref[0].astype(BF16)
    vc = cv_ref[0].astype(BF16)
    qpos = q0 + lax.broadcasted_iota(jnp.int32, (GQA_QB, GQA_SPAN), 0)
    kpos = q0 - WINDOW + lax.broadcasted_iota(jnp.int32, (GQA_QB, GQA_SPAN), 1)
    bias = jnp.where(kpos < 0, NEG_INF, jnp.where(kpos >= L, NEG_INF, jnp.where(jnp.abs(qpos - kpos) <= WINDOW, 0.0, NEG_INF)))
    outs = []
    for j in range(GQA_HEADS * GQA_HEAD_DIM // LANES):
        qj = _rope128(q_ref[:, j * LANES:(j + 1) * LANES], cos, sin).astype(BF16)
        for half in range(2):
            h = 2 * j + half
            kh = h // group
            ks = slice(kh * GQA_HEAD_DIM, (kh + 1) * GQA_HEAD_DIM)
            qh = qj[:, half * GQA_HEAD_DIM:(half + 1) * GQA_HEAD_DIM]
            s_loc = _dot_nt(qh, kw[:, ks]) * scale + bias
            s_ctx = _dot_nt(qh, kc[:, ks]) * scale
            m = jnp.maximum(jnp.maximum(jnp.max(s_loc, axis=-1, keepdims=True), jnp.max(s_ctx, axis=-1, keepdims=True)),
                            sink_ref[h])
            p_loc = jnp.exp(s_loc - m)
            p_ctx = jnp.exp(s_ctx - m)
            inv = 1.0 / (jnp.sum(p_loc, axis=-1, keepdims=True) + jnp.sum(p_ctx, axis=-1, keepdims=True)
                         + jnp.exp(sink_ref[h] - m))
            outs.append(_dot((p_loc * inv).astype(BF16), vw[:, ks]) + _dot((p_ctx * inv).astype(BF16), vc[:, ks]))
    y_ref[...] = jnp.concatenate(outs, axis=1)


def _gqa_win(p1, cache_k, cache_v, cos, sin, sink, *, L):
    n = p1.shape[0]
    nseq, nqb = n // L, L // GQA_QB
    Lc = cache_k.shape[1]
    return pl.pallas_call(
        functools.partial(_gqa_win_kernel, L=L),
        grid=(nseq, nqb),
        in_specs=[pl.BlockSpec((GQA_QB, 1024), lambda b, q: (b * nqb + q, 2)),
                  pl.BlockSpec((L, 256), lambda b, q: (b, 20)), pl.BlockSpec((L, 256), lambda b, q: (b, 21)),
                  pl.BlockSpec((1, Lc, 256), lambda b, q: (b, 0, 0)), pl.BlockSpec((1, Lc, 256), lambda b, q: (b, 0, 0)),
                  _full((L, LANES)), _full((L, LANES)), pl.BlockSpec(memory_space=pltpu.SMEM)],
        out_specs=pl.BlockSpec((GQA_QB, 1024), lambda b, q: (b * nqb + q, 0)),
        out_shape=jax.ShapeDtypeStruct((n, 1024), F32),
        scratch_shapes=[pltpu.VMEM((L + 2 * WINDOW, 256), BF16), pltpu.VMEM((L + 2 * WINDOW, 256), BF16)],
        compiler_params=_cp(2), name="gqa_window",
    )(p1, p1, p1, cache_k, cache_v, cos, sin, sink)


def _mlstm_kernel(qp_ref, qc_ref, qn_ref, v_ref, o_ref, sm_ref, gt_ref, cw_ref, cb_ref, brow_ref, bcol_ref, ng_ref,
                  *rest, nc, zero_init):
    init_refs = () if zero_init else rest[:3]
    y_ref, cst_ref, nst_ref, mst_ref, hf_s, c_s, n_s, m_s = rest[len(init_refs):]
    s = pl.program_id(1)
    bwd = s >= nc
    cidx = _chunk_index(s, nc)
    nh = ML_HEADS
    hd = ML_HEAD_DIM

    def load_initial(direction):
        for k, dst in enumerate((c_s, n_s, m_s)):
            dst[...] = jnp.zeros(dst.shape, F32) if zero_init else init_refs[k][:, direction]

    @pl.when(s == 0)
    def _():
        load_initial(0)

    @pl.when(s == nc)
    def _():
        cst_ref[:, 0] = c_s[...]
        nst_ref[:, 0] = n_s[...]
        mst_ref[:, 0] = m_s[...]
        load_initial(1)

    G = qc_ref.shape[0]
    tsel, tsel_t = _tri(bwd, G)
    mask = tsel > 0.0
    kscale = np.float32(hd ** -0.5)
    r0 = pl.multiple_of(cidx * CHUNK, CHUNK)
    hi = lax.Precision.HIGHEST

    qk = _conv_silu(qp_ref[...], qc_ref[...], qn_ref[...], cw_ref, cb_ref, cidx == 0, cidx == nc - 1)
    gates = sm_ref[:, :, 0:4 * nh] + brow_ref[...]
    gates_t = gt_ref[:, 0] + bcol_ref[...]
    li = jnp.where(bwd, gates[:, :, nh:2 * nh], gates[:, :, 0:nh])
    li_t = jnp.where(bwd, gates_t[:, nh:2 * nh, :], gates_t[:, 0:nh, :])
    lf = -_softplus(-jnp.where(bwd, gates[:, :, 3 * nh:4 * nh], gates[:, :, 2 * nh:3 * nh]))
    lf_t = -_softplus(-jnp.where(bwd, gates_t[:, 3 * nh:4 * nh, :], gates_t[:, 2 * nh:3 * nh, :]))
    bc = _bdot(tsel, lf, (2, 1), hi)
    bc_t = _bdot(lf_t, tsel_t, (2, 1), hi)
    tot = jnp.where(bwd, bc[:, 0:1, :], bc[:, CHUNK - 1:CHUNK, :])

    hs = []
    for h in range(nh):
        cs = slice(h * hd, (h + 1) * hd)
        q = qk[:, :, cs]
        k = qk[:, :, ML_D + h * hd:ML_D + (h + 1) * hd] * kscale
        q16, k16 = q.astype(BF16), k.astype(BF16)
        v = v_ref[:, :, cs]
        bq = bc[:, :, h:h + 1]
        m_prev = m_s[:, h:h + 1, 0:1]
        log_d = jnp.where(mask, bq - bc_t[:, h:h + 1, :] + li_t[:, h:h + 1, :], NEG_INF)
        log_inter = bq + m_prev
        m_out = jnp.maximum(log_inter, jnp.max(log_d, axis=2, keepdims=True))
        sd = _bdot(q16, k16, (2, 2)) * jnp.exp(log_d - m_out)
        w_inter = jnp.exp(log_inter - m_out)
        c_prev = c_s[:, h]
        n_prev = n_s[:, h:h + 1, :]
        num = _bdot(sd.astype(BF16), v.astype(BF16), (2, 1)) + w_inter * _bdot(q16, c_prev.astype(BF16), (2, 2))
        den = jnp.sum(sd, axis=2, keepdims=True) + w_inter * jnp.sum(q * n_prev, axis=2, keepdims=True)
        hs.append(num / jnp.maximum(jnp.abs(den), jnp.exp(-m_out)))
        th = tot[:, :, h:h + 1]
        end_inter = th + m_prev
        end_intra = th - bq + li[:, :, h:h + 1]
        m_new = jnp.maximum(end_inter, jnp.max(end_intra, axis=1, keepdims=True))
        w_c = jnp.exp(end_inter - m_new)
        w_k = jnp.exp(end_intra - m_new)
        c_s[:, h] = w_c * c_prev + _bdot((v * w_k).astype(BF16), k16, (1, 1))
        n_s[:, h:h + 1, :] = w_c * n_prev + jnp.sum(k * w_k, axis=1, keepdims=True)
        m_s[:, h:h + 1, :] = jnp.broadcast_to(m_new, (G, 1, hd))
    hcat = jnp.concatenate(hs, axis=2)

    @pl.when(jnp.logical_not(bwd))
    def _():
        hf_s[:, pl.ds(r0, CHUNK), :] = hcat

    @pl.when(bwd)
    def _():
        tot_h = hf_s[:, pl.ds(r0, CHUNK), :] + hcat
        normed = jnp.concatenate([_rms(tot_h[:, :, h * hd:(h + 1) * hd]) for h in range(nh)], axis=2) * ng_ref[...]
        y_ref[...] = normed * jax.nn.sigmoid(o_ref[...])

    @pl.when(s == 2 * nc - 1)
    def _():
        cst_ref[:, 1] = c_s[...]
        nst_ref[:, 1] = n_s[...]
        mst_ref[:, 1] = m_s[...]


def _mlstm(p1, gt, conv_w, conv_b, ig_b, fg_b, norm_g, c0, n0, m0, *, L):
    nseq, nc = p1.shape[0], L // CHUNK
    zero_init = c0 is None
    G = _scan_group(nseq, 8 if zero_init else 4)
    cur, prev, nxt, out = _scan_specs(nc)
    cshape = (G, 2, ML_HEADS, ML_HEAD_DIM, ML_HEAD_DIM)
    nshape = (G, 2, ML_HEADS, ML_HEAD_DIM)
    states = [] if zero_init else [c0, n0, m0]
    gb = jnp.concatenate([ig_b.reshape(-1), fg_b.reshape(-1)])
    small = [conv_w, conv_b.reshape(1, -1), gb.reshape(1, -1), gb.reshape(-1, 1), norm_g.reshape(1, -1)]
    st = lambda shape: pl.BlockSpec(shape, lambda b, s: (b,) + (0,) * (len(shape) - 1))
    blk = lambda rows, width, index: pl.BlockSpec((G, rows, width), index)
    return pl.pallas_call(
        functools.partial(_mlstm_kernel, nc=nc, zero_init=zero_init),
        grid=(nseq // G, 2 * nc),
        in_specs=[blk(8, 2048, prev(0)), blk(CHUNK, 2048, cur(0)), blk(8, 2048, nxt(0)),
                  blk(CHUNK, 1024, cur(3)), blk(CHUNK, 1024, cur(4)), blk(CHUNK, LANES, cur(44)),
                  pl.BlockSpec((G, 1, 32, CHUNK), lambda b, s: (b, _chunk_index(s, nc), 0, 0))]
        + [_full(v.shape) for v in small] + ([] if zero_init else [st(cshape), st(nshape), st(nshape)]),
        out_specs=[blk(CHUNK, ML_D, out(0)), st(cshape), st(nshape), st(nshape)],
        out_shape=[jax.ShapeDtypeStruct((nseq, L, ML_D), F32), jax.ShapeDtypeStruct((nseq,) + cshape[1:], F32),
                   jax.ShapeDtypeStruct((nseq,) + nshape[1:], F32), jax.ShapeDtypeStruct((nseq,) + nshape[1:], F32)],
        scratch_shapes=[pltpu.VMEM((G, L, ML_D), F32), pltpu.VMEM(cshape[:1] + cshape[2:], F32),
                        pltpu.VMEM(nshape[:1] + nshape[2:], F32), pltpu.VMEM(nshape[:1] + nshape[2:], F32)],
        compiler_params=_cp(2), name="mlstm_mixer",
    )(p1, p1, p1, p1, p1, p1, gt, *small, *states)


def _prep_layer0(w_in, w_qb, w_kvb, w_out):
    pad = jnp.zeros((D_MODEL, L0_COLS - 3680), F32)
    w = jnp.concatenate([w_in[:, :2560], w_in[:, 2592:3680], w_in[:, 2560:2592], pad], axis=1).astype(BF16)
    q3 = w_qb.reshape(MLA_RANK, MLA_HEADS, MLA_NOPE + MLA_ROPE)
    wqn = q3[:, :, :MLA_NOPE].reshape(MLA_RANK, -1).astype(BF16)
    wqr = jnp.concatenate([q3[:, :, MLA_NOPE:], jnp.zeros((MLA_RANK, MLA_HEADS, LANES - MLA_ROPE), F32)], axis=-1)
    wqr = wqr.reshape(MLA_RANK, -1).astype(BF16)
    k3 = w_kvb.reshape(MLA_RANK, MLA_HEADS, MLA_NOPE + MLA_V)
    wkk = k3[:, :, :MLA_NOPE].reshape(MLA_RANK, -1).astype(BF16)
    wkv = k3[:, :, MLA_NOPE:].reshape(MLA_RANK, -1).astype(BF16)
    return w, wqn, wqr, wkk, wkv, w_out.astype(BF16)


def _prep_layer1(w_in, w_out):
    pad = jnp.zeros((D_MODEL, L1_COLS - 5664), F32)
    w = jnp.concatenate([w_in[:, 1536:3584], w_in[:, 0:1024], w_in[:, 3584:5632], w_in[:, 1024:1536],
                         w_in[:, 5632:5664], pad], axis=1).astype(BF16)
    return w, w_out.astype(BF16)


def _prep_peer(wq, keys, u, v):
    return (wq.T.astype(BF16), keys.reshape(PEER_HEADS * 2, PEER_NKEYS, -1).astype(BF16), u.astype(BF16),
            v.T.astype(BF16))


def _chunk_transposed(cols, L):
    return cols.reshape(-1, L // CHUNK, CHUNK, cols.shape[1]).transpose(0, 1, 3, 2)


def _trunk(x, L, mod0, mod1, w0, w1, peer0, peer1, p, ctx, rope_tabs, mod_base, rows_per_cond, fg):
    kw = dict(mod_base=mod_base, rows_per_cond=rows_per_cond)
    w_in0, wqn, wqr, wkk, wkv, w_out0 = w0
    w_in1, w_out1 = w1
    nseq = x.shape[0] // L

    p0 = _proj(x, p['l0_norm1_g'], mod0, w_in0, tn=1280, **kw)
    cw, cb = p['l0_ssd_conv_w'], p['l0_ssd_conv_b']
    y_ssd, ssd_state = _ssd(p0.reshape(nseq, L, -1), _chunk_transposed(p0[:, 3648:3680], L), cw[:, :1024], cb[:1024],
                            cw[:, 1024:], cb[1024:], p['l0_ssd_A_log'], p['l0_ssd_dt_bias'], p['l0_ssd_D'],
                            p['l0_ssd_norm_g'], None if ctx is None else ctx[0], L=L)
    y_ssd = y_ssd.reshape(nseq * L, -1)
    mla_tabs = None
    if rope_tabs is not None:
        z64 = jnp.zeros_like(rope_tabs[0])
        mla_tabs = (jnp.concatenate([rope_tabs[0], z64], axis=1), jnp.concatenate([rope_tabs[1], z64], axis=1))
    mla_out = _mla(p0, p['l0_mla_q_norm_g'], p['l0_mla_kv_norm_g'], wqn, wqr, wkk, wkv,
                   None if ctx is None else ctx[1], None if ctx is None else ctx[2], mla_tabs, L=L)
    x = _outproj(y_ssd, mla_out[0], w_out0, x, mod0, **kw)
    x = _peer(x, p['l0_norm2_g'], mod0, *peer0, fg, final_norm=False, **kw)

    p1 = _proj(x, p['l1_norm1_g'], mod1, w_in1, tn=1152, **kw)
    if ctx is None:
        y_gqa = _gqa_ctx(p1, p['l1_gqa_sink'], L=L)
        c0 = n0 = m0 = None
    else:
        gqa_tabs = tuple(jnp.concatenate([t, t], axis=1) for t in rope_tabs)
        y_gqa = _gqa_win(p1, ctx[3].reshape(nseq, -1, 256), ctx[4].reshape(nseq, -1, 256), *gqa_tabs,
                         p['l1_gqa_sink'], L=L)
        c0, n0 = ctx[5], ctx[6]
        m0 = jnp.broadcast_to(ctx[7][..., None], ctx[7].shape + (ML_HEAD_DIM,))
    y_ml, mc, mn, mm = _mlstm(p1.reshape(nseq, L, -1), _chunk_transposed(p1[:, 5632:5664], L), p['l1_ml_conv_w'],
                              p['l1_ml_conv_b'], p['l1_ml_ig_b'], p['l1_ml_fg_b'], p['l1_ml_norm_g'], c0, n0, m0, L=L)
    y_ml = y_ml.reshape(nseq * L, -1)
    x = _outproj(y_gqa, y_ml, w_out1, x, mod1, **kw)
    y = _peer(x, p['l1_norm2_g'], mod1, *peer1, fg, final_norm=True, **kw)
    new = None
    if ctx is None:
        new = (ssd_state, mla_out[1].reshape(nseq, L, MLA_RANK), p0[:, 3584:3648].reshape(nseq, L, MLA_ROPE),
               p1[:, 5120:5376].reshape(nseq, L, GQA_KV_HEADS, GQA_HEAD_DIM),
               p1[:, 5376:5632].reshape(nseq, L, GQA_KV_HEADS, GQA_HEAD_DIM), mc, mn, mm[..., 0])
    return y, new


def kernel(x_prompt, x_sample, state_l0_ssd, cache_l0_mla_ckv, cache_l0_mla_kpe, cache_l1_gqa_k, cache_l1_gqa_v, state_l1_mlstm_C, state_l1_mlstm_n, state_l1_mlstm_m, c, c_ctx, final_norm_g, l0_ada_w, l0_ada_b, l0_norm1_g, l0_norm2_g, l0_w_in, l0_ssd_conv_w, l0_ssd_conv_b, l0_ssd_A_log, l0_ssd_dt_bias, l0_ssd_D, l0_ssd_norm_g, l0_mla_q_norm_g, l0_mla_w_qb, l0_mla_kv_norm_g, l0_mla_w_kvb, l0_w_out, l0_peer_wq, l0_peer_keys, l0_peer_u, l0_peer_v, l1_ada_w, l1_ada_b, l1_norm1_g, l1_norm2_g, l1_w_in, l1_gqa_sink, l1_ml_conv_w, l1_ml_conv_b, l1_ml_ig_b, l1_ml_fg_b, l1_ml_norm_g, l1_w_out, l1_peer_wq, l1_peer_keys, l1_peer_u, l1_peer_v):
    p = dict(l0_norm1_g=l0_norm1_g, l0_norm2_g=l0_norm2_g, l0_ssd_conv_w=l0_ssd_conv_w, l0_ssd_conv_b=l0_ssd_conv_b,
             l0_ssd_A_log=l0_ssd_A_log, l0_ssd_dt_bias=l0_ssd_dt_bias, l0_ssd_D=l0_ssd_D, l0_ssd_norm_g=l0_ssd_norm_g,
             l0_mla_q_norm_g=l0_mla_q_norm_g, l0_mla_kv_norm_g=l0_mla_kv_norm_g, l1_norm1_g=l1_norm1_g,
             l1_norm2_g=l1_norm2_g, l1_gqa_sink=l1_gqa_sink, l1_ml_conv_w=l1_ml_conv_w, l1_ml_conv_b=l1_ml_conv_b,
             l1_ml_ig_b=l1_ml_ig_b, l1_ml_fg_b=l1_ml_fg_b, l1_ml_norm_g=l1_ml_norm_g)
    nb, seq = x_prompt.shape[:2]
    db, dseq = x_sample.shape[:2]
    assert db <= 7 and seq % MLA_QB == 0 and dseq % MLA_QB == 0

    cond8 = jnp.zeros((8, D_MODEL), F32).at[0].set(c_ctx).at[1:1 + db].set(c)
    mod0 = _ada(cond8, l0_ada_w, l0_ada_b).reshape(8, 1, 6 * D_MODEL)
    mod1 = _ada(cond8, l1_ada_w, l1_ada_b).reshape(8, 1, 6 * D_MODEL)
    w0 = _prep_layer0(l0_w_in, l0_mla_w_qb, l0_mla_w_kvb, l0_w_out)
    w1 = _prep_layer1(l1_w_in, l1_w_out)
    peer0 = _prep_peer(l0_peer_wq, l0_peer_keys, l0_peer_u, l0_peer_v)
    peer1 = _prep_peer(l1_peer_wq, l1_peer_keys, l1_peer_u, l1_peer_v)

    y_prompt, new = _trunk(x_prompt.reshape(nb * seq, D_MODEL), seq, mod0, mod1, w0, w1, peer0, peer1, p, None, None,
                           0, nb * seq, final_norm_g)
    ctx = (state_l0_ssd, cache_l0_mla_ckv, cache_l0_mla_kpe, cache_l1_gqa_k, cache_l1_gqa_v, state_l1_mlstm_C,
           state_l1_mlstm_n, state_l1_mlstm_m)
    y_sample, _ = _trunk(x_sample.reshape(db * dseq, D_MODEL), dseq, mod0, mod1, w0, w1, peer0, peer1, p, ctx,
                         _rope_tables(dseq), 1, dseq, final_norm_g)
    return (y_prompt.reshape(nb, seq, D_MODEL), y_sample.reshape(db, dseq, D_MODEL)) + new
```

```python
import functools

import jax
import jax.numpy as jnp
import numpy as np
from jax import lax
from jax.experimental import pallas as pl
from jax.experimental.pallas import tpu as pltpu

F32 = jnp.float32
BF16 = jnp.bfloat16
NEG_INF = float("-inf")

D_MODEL = 2048
EPS = 1e-6
ROPE_BASE = 10000.0
GRID_W = 64
CONV_W = 5
CHUNK = 64
SSD_HEADS = 16
SSD_HEAD_DIM = 64
SSD_D_INNER = 1024
SSD_STATE = 128
MLA_HEADS = 8
MLA_NOPE = 128
MLA_ROPE = 64
MLA_V = 128
MLA_RANK = 512
GQA_HEADS = 16
GQA_KV_HEADS = 4
GQA_HEAD_DIM = 64
WINDOW = 128
ML_HEADS = 8
ML_HEAD_DIM = 128
ML_D = 1024
PEER_HEADS = 8
PEER_NKEYS = 128
PEER_N = PEER_NKEYS * PEER_NKEYS
PEER_TOPK = 16

LANES = 128
SUBLANES = 8
VMEM_LIMIT = 56 * 2**20

L0_COLS = 3840
L1_COLS = 5760


def _cp(n_grid, flags=None):
    return pltpu.CompilerParams(dimension_semantics=("arbitrary",) * n_grid, vmem_limit_bytes=VMEM_LIMIT, flags=flags)


def _silu(x):
    return x * jax.nn.sigmoid(x)


def _softplus(x):
    return jnp.maximum(x, 0.0) + jnp.log1p(jnp.exp(-jnp.abs(x)))


def _rms(x):
    return x * lax.rsqrt(jnp.mean(x * x, axis=-1, keepdims=True) + EPS)


def _dot(a, b):
    return jnp.dot(a, b, preferred_element_type=F32)


def _dot_nt(a, b):
    return lax.dot_general(a, b, (((1,), (1,)), ((), ())), preferred_element_type=F32)


def _full(shape):
    nd = len(shape)
    return pl.BlockSpec(shape, lambda *_: (0,) * nd)


def _mod_spec(chunk, tb, mod_base, rows_per_cond):
    def index(i, *_):
        return (mod_base + (i * tb) // rows_per_cond, 0, chunk)
    return pl.BlockSpec((1, 1, D_MODEL), index)


def _ada_kernel(c_ref, w_ref, b_ref, o_ref):
    s = _silu(c_ref[...])
    o_ref[...] = _dot(s.astype(BF16), w_ref[...].astype(BF16)) + b_ref[...]


def _ada(cond8, w, b):
    n = w.shape[1]
    tn = 1024
    return pl.pallas_call(
        _ada_kernel,
        grid=(n // tn,),
        in_specs=[_full((8, D_MODEL)), pl.BlockSpec((D_MODEL, tn), lambda j: (0, j)),
                  pl.BlockSpec((1, tn), lambda j: (0, j))],
        out_specs=pl.BlockSpec((8, tn), lambda j: (0, j)),
        out_shape=jax.ShapeDtypeStruct((8, n), F32),
        compiler_params=_cp(1), name="ada_table",
    )(cond8, w, b.reshape(1, n))


def _proj_kernel(x_ref, g_ref, sc_ref, sh_ref, w_ref, o_ref, h_ref):
    @pl.when(pl.program_id(1) == 0)
    def _():
        h = _rms(x_ref[...]) * g_ref[...] * (1.0 + sc_ref[0]) + sh_ref[0]
        h_ref[...] = h.astype(BF16)
    o_ref[...] = _dot(h_ref[...], w_ref[...])


def _proj(x, g, mod, w, *, tn, mod_base, rows_per_cond):
    n, ncol = x.shape[0], w.shape[1]
    tb = 1024
    return pl.pallas_call(
        _proj_kernel,
        grid=(n // tb, ncol // tn),
        in_specs=[pl.BlockSpec((tb, D_MODEL), lambda i, j: (i, 0)), _full((1, D_MODEL)),
                  _mod_spec(1, tb, mod_base, rows_per_cond), _mod_spec(0, tb, mod_base, rows_per_cond),
                  pl.BlockSpec((D_MODEL, tn), lambda i, j: (0, j))],
        out_specs=pl.BlockSpec((tb, tn), lambda i, j: (i, j)),
        out_shape=jax.ShapeDtypeStruct((n, ncol), F32),
        scratch_shapes=[pltpu.VMEM((tb, D_MODEL), BF16)],
        compiler_params=_cp(2), name="norm_mod_proj",
    )(x, g.reshape(1, D_MODEL), mod, mod, w)


def _outproj_kernel(ya_ref, yb_ref, w_ref, x_ref, g1_ref, o_ref):
    half = ya_ref.shape[1]
    y = _dot(ya_ref[...].astype(BF16), w_ref[0:half, :]) + _dot(yb_ref[...].astype(BF16), w_ref[half:, :])
    o_ref[...] = x_ref[...] + g1_ref[0] * y


def _outproj(ya, yb, w, x, mod, *, mod_base, rows_per_cond):
    n = x.shape[0]
    tb = 512
    return pl.pallas_call(
        _outproj_kernel,
        grid=(n // tb,),
        in_specs=[pl.BlockSpec((tb, ya.shape[1]), lambda i: (i, 0)), pl.BlockSpec((tb, yb.shape[1]), lambda i: (i, 0)),
                  _full(w.shape), pl.BlockSpec((tb, D_MODEL), lambda i: (i, 0)),
                  _mod_spec(2, tb, mod_base, rows_per_cond)],
        out_specs=pl.BlockSpec((tb, D_MODEL), lambda i: (i, 0)),
        out_shape=jax.ShapeDtypeStruct((n, D_MODEL), F32),
        compiler_params=_cp(1), name="out_proj_residual",
    )(ya, yb, w, x, mod)


ROUTE_TB = 512
_CAND_KEEP = (16, 8, 5, 4, 3, 2, 2, 2)


def _sort_pairs(n):
    pairs = []

    def merge(lo, hi, r):
        step = r * 2
        if step < hi - lo:
            merge(lo, hi, step)
            merge(lo + r, hi, step)
            pairs.extend((i, i + r) for i in range(lo + r, hi - r, step))
        else:
            pairs.append((lo, lo + r))

    def sort(lo, hi):
        if hi - lo >= 1:
            mid = lo + (hi - lo) // 2
            sort(lo, mid)
            sort(mid + 1, hi)
            merge(lo, hi, 1)

    sort(0, 15)
    return [(i, j) for i, j in pairs if j < n]


def _top_values(x, count):
    groups = [x[r:r + SUBLANES] for r in range(0, x.shape[0], SUBLANES)]
    n = len(groups)
    for i, j in _sort_pairs(n):
        groups[i], groups[j] = jnp.maximum(groups[i], groups[j]), jnp.minimum(groups[i], groups[j])
    sub = lax.broadcasted_iota(jnp.int32, groups[0].shape, 0)
    tops = []
    for t in range(count):
        head = groups[0]
        m = jnp.max(head, axis=0, keepdims=True)
        tops.append(m)
        if t == count - 1:
            break
        first = jnp.min(jnp.where(head == m, sub, SUBLANES), axis=0, keepdims=True)
        popped = sub == first
        depth = min(count - 1 - t, n)
        for r in range(depth):
            below = groups[r + 1] if r + 1 < n else NEG_INF
            groups[r] = jnp.where(popped, below, groups[r])
    return tops


def _route_kernel(x_ref, g_ref, sc_ref, sh_ref, wqt_ref, keys_ref,
                  hb_ref, th_ref, s2_ref, e2_ref, c1_ref, qt_s, top_s):
    tb = x_ref.shape[0]
    h = _rms(x_ref[...]) * g_ref[...] * (1.0 + sc_ref[0]) + sh_ref[0]
    hb = h.astype(BF16)
    hb_ref[...] = hb
    qt_s[...] = _dot_nt(wqt_ref[...], hb).astype(BF16)

    iota_8 = lax.broadcasted_iota(jnp.int32, (SUBLANES, tb), 0)

    def head(hd, _):
        scores = []
        for c in range(2):
            r0 = pl.multiple_of((hd * 2 + c) * PEER_NKEYS, PEER_NKEYS)
            s = _dot(keys_ref[hd * 2 + c], qt_s[pl.ds(r0, PEER_NKEYS), :])
            scores.append(s)
            for it, m in enumerate(_top_values(s, PEER_TOPK)):
                top_s[c, it:it + 1, :] = m
        a1 = top_s[0]
        a2 = top_s[1]
        pieces = [a1[0:1] + a2, a1[1:2] + a2[0:8]]
        for i in range(2, 8):
            pieces.append(jnp.where(iota_8 < _CAND_KEEP[i], a1[i:i + 1] + a2[0:8], NEG_INF))
        pieces.append(a1[8:16] + a2[0:1])
        cand = jnp.concatenate(pieces, axis=0)
        tau = _top_values(cand, PEER_TOPK)[-1]
        top = a1[0:1] + a2[0:1]
        z = jnp.sum(jnp.where(cand >= tau, jnp.exp(cand - top), 0.0), axis=0, keepdims=True)
        e2 = jnp.exp(scores[1] - a2[0:1])
        c1 = jnp.exp(scores[0] - a1[0:1]) / z
        theta = jnp.full(scores[0].shape, jnp.inf, F32)
        for q in range(PEER_TOPK):
            a2q = a2[q:q + 1]
            phi = jnp.min(jnp.where(a1 + a2q >= tau, a1, jnp.inf), axis=0, keepdims=True)
            theta = jnp.where(scores[0] >= phi, a2q, theta)
        for t in range(tb // LANES):
            ts = slice(t * LANES, (t + 1) * LANES)
            th_ref[hd, t] = theta[:, ts]
            s2_ref[hd, t] = scores[1][:, ts]
            e2_ref[hd, t] = e2[:, ts]
            c1_ref[hd, t] = c1[:, ts]
        return 0

    lax.fori_loop(0, PEER_HEADS, head, 0)


def _route(x, g, mod, wqt, keys, *, mod_base, rows_per_cond):
    n = x.shape[0]
    tb = ROUTE_TB
    ntg = tb // LANES
    hk = (PEER_HEADS, n // LANES, PEER_NKEYS, LANES)
    rspec = pl.BlockSpec((PEER_HEADS, ntg, PEER_NKEYS, LANES), lambda i: (0, i, 0, 0))
    return pl.pallas_call(
        _route_kernel,
        grid=(n // tb,),
        in_specs=[pl.BlockSpec((tb, D_MODEL), lambda i: (i, 0)), _full((1, D_MODEL)),
                  _mod_spec(4, tb, mod_base, rows_per_cond), _mod_spec(3, tb, mod_base, rows_per_cond),
                  _full(wqt.shape), _full(keys.shape)],
        out_specs=[pl.BlockSpec((tb, D_MODEL), lambda i: (i, 0)), rspec, rspec, rspec, rspec],
        out_shape=[jax.ShapeDtypeStruct((n, D_MODEL), BF16)] + [jax.ShapeDtypeStruct(hk, F32)] * 4,
        scratch_shapes=[pltpu.VMEM((PEER_HEADS * 2 * PEER_NKEYS, tb), BF16), pltpu.VMEM((2, PEER_TOPK, tb), F32)],
        compiler_params=_cp(1), name="peer_route",
    )(x, g.reshape(1, D_MODEL), mod, mod, wqt, keys)


PEER_TB = 512
PEER_EC = 1024
PEER_SUB = PEER_EC // 2


def _experts_kernel(hb_ref, th_ref, s2_ref, e2_ref, c1_ref, u0_ref, ub_ref, un_ref, vp_ref, va_ref, vl_ref,
                    x_ref, g2_ref, fg_ref, o_ref, acc_s, st0_s, st1_s, a0_s, a1_s, *, final_norm):
    e = pl.program_id(1)
    tb = hb_ref.shape[0]
    ntg = tb // LANES
    rows_per_sub = PEER_SUB // PEER_NKEYS

    def scores(u_blk, st_s, h):
        res = _dot_nt(u_blk[...], hb_ref[h * 2 * LANES:(h + 1) * 2 * LANES, :])
        st_s[2 * h] = res[:, :LANES]
        st_s[2 * h + 1] = res[:, LANES:]

    def values(vt_blk, act_s, h):
        act = jnp.concatenate([act_s[2 * h], act_s[2 * h + 1]], axis=1)
        hs = slice(h * 2 * LANES, (h + 1) * 2 * LANES)
        for r in range(0, D_MODEL, PEER_SUB):
            acc_s[r:r + PEER_SUB, hs] += _dot(vt_blk[r:r + PEER_SUB, :], act)

    def gates(c, tg, st_s, act_s):
        sub = 32
        for k in range(rows_per_sub):
            row = c * rows_per_sub + k
            for j0 in range(0, PEER_NKEYS, sub):
                js = slice(j0, j0 + sub)
                ks = slice(k * PEER_NKEYS + j0, k * PEER_NKEYS + j0 + sub)
                w = jnp.zeros((sub, LANES), F32)
                for hd in range(PEER_HEADS):
                    chosen = s2_ref[hd, tg, js, :] >= th_ref[hd, tg, row:row + 1, :]
                    w = w + jnp.where(chosen, e2_ref[hd, tg, js, :], 0.0) * c1_ref[hd, tg, row:row + 1, :]
                s = st_s[tg, ks, :]
                act = 0.5 * s * (1.0 + lax.erf(s * np.float32(2.0 ** -0.5)))
                act_s[tg, ks, :] = (act * w).astype(BF16)

    @pl.when(e == 0)
    def _():
        acc_s[...] = jnp.zeros_like(acc_s)
        a1_s[...] = jnp.zeros_like(a1_s)
        scores(u0_ref, st0_s, 0)
        scores(u0_ref, st0_s, 1)

    def phase(c, st_cur, act_cur, u_next, st_next, vt_prev, act_prev):
        for h in range(2):
            scores(u_next, st_next, h)
            gates(c, 2 * h, st_cur, act_cur)
            values(vt_prev, act_prev, h)
            gates(c, 2 * h + 1, st_cur, act_cur)

    phase(0, st0_s, a0_s, ub_ref, st1_s, vp_ref, a1_s)
    phase(1, st1_s, a1_s, un_ref, st0_s, va_ref, a0_s)

    @pl.when(e == pl.num_programs(1) - 1)
    def _():
        values(vl_ref, a1_s, 0)
        values(vl_ref, a1_s, 1)
        y = x_ref[...] + g2_ref[0] * acc_s[...].T
        if final_norm:
            y = _rms(y) * fg_ref[...]
        o_ref[...] = y


def _experts(hb, theta, s2, e2, c1, u, vt, x, mod, fg, *, mod_base, rows_per_cond, final_norm):
    n = x.shape[0]
    tb = PEER_TB
    ntg = tb // LANES
    ne = PEER_N // PEER_EC
    once = dict(pipeline_mode=pl.Buffered(1))
    rspec = pl.BlockSpec((PEER_HEADS, ntg, PEER_NKEYS, LANES), lambda i, e: (0, i, 0, 0), **once)
    rows8 = pl.BlockSpec((PEER_HEADS, ntg, SUBLANES, LANES), lambda i, e: (0, i, e, 0))
    uspec = lambda index: pl.BlockSpec((PEER_SUB, D_MODEL), index)
    vspec = lambda index: pl.BlockSpec((D_MODEL, PEER_SUB), index)
    return pl.pallas_call(
        functools.partial(_experts_kernel, final_norm=final_norm),
        grid=(n // tb, ne),
        in_specs=[pl.BlockSpec((tb, D_MODEL), lambda i, e: (i, 0)), rows8, rspec, rspec, rows8,
                  pl.BlockSpec((PEER_SUB, D_MODEL), lambda i, e: (0, 0), **once),
                  uspec(lambda i, e: (2 * e + 1, 0)),
                  uspec(lambda i, e: (jnp.minimum(2 * e + 2, 2 * ne - 2), 0)),
                  vspec(lambda i, e: (0, jnp.maximum(2 * e - 1, 0))),
                  vspec(lambda i, e: (0, 2 * e)),
                  pl.BlockSpec((D_MODEL, PEER_SUB), lambda i, e: (0, 2 * ne - 1), **once),
                  pl.BlockSpec((tb, D_MODEL), lambda i, e: (i, 0)),
                  _mod_spec(5, tb, mod_base, rows_per_cond), _full((1, D_MODEL))],
        out_specs=pl.BlockSpec((tb, D_MODEL), lambda i, e: (i, 0)),
        out_shape=jax.ShapeDtypeStruct((n, D_MODEL), F32),
        scratch_shapes=[pltpu.VMEM((D_MODEL, tb), F32), pltpu.VMEM((ntg, PEER_SUB, LANES), F32),
                        pltpu.VMEM((ntg, PEER_SUB, LANES), F32), pltpu.VMEM((ntg, PEER_SUB, LANES), BF16),
                        pltpu.VMEM((ntg, PEER_SUB, LANES), BF16)],
        compiler_params=_cp(2), name="peer_experts",
    )(hb, theta, s2, e2, c1, u, u, u, vt, vt, vt, x, mod, fg.reshape(1, D_MODEL))


def _peer(x, g, mod, wqt, keys, u, vt, fg, *, mod_base, rows_per_cond, final_norm):
    hb, theta, s2, e2, c1 = _route(x, g, mod, wqt, keys, mod_base=mod_base, rows_per_cond=rows_per_cond)
    return _experts(hb, theta, s2, e2, c1, u, vt, x, mod, fg, mod_base=mod_base, rows_per_cond=rows_per_cond,
                    final_norm=final_norm)


def _chunk_index(s, nc):
    return jnp.where(s < nc, s, 2 * nc - 1 - s)


def _scan_group(nseq, largest):
    return max(g for g in (2, 4, 8) if g <= largest and nseq % g == 0)


def _scan_specs(nc):
    def cur(col):
        return lambda b, s: (b, _chunk_index(s, nc), col)

    def prev(col):
        return lambda b, s: (b, jnp.maximum(_chunk_index(s, nc) * 8 - 1, 0), col)

    def nxt(col):
        return lambda b, s: (b, jnp.minimum(_chunk_index(s, nc) * 8 + 8, nc * 8 - 1), col)

    def out(col):
        return lambda b, s: (b, jnp.where(s < nc, nc - 1, 2 * nc - 1 - s), col)
    return cur, prev, nxt, out


def _conv_silu(prev, cur, nxt, w_ref, b_ref, first, last):
    win = jnp.concatenate([jnp.where(first, 0.0, prev), cur, jnp.where(last, 0.0, nxt)], axis=1)
    acc = b_ref[...] + win[:, 6:70, :] * w_ref[0:1, :]
    for k in range(1, CONV_W):
        acc = acc + win[:, 6 + k:70 + k, :] * w_ref[k:k + 1, :]
    return _silu(acc)


def _tri(bwd, G):
    r = lax.broadcasted_iota(jnp.int32, (G, CHUNK, CHUNK), 1)
    c = lax.broadcasted_iota(jnp.int32, (G, CHUNK, CHUNK), 2)
    lower = (c <= r).astype(F32)
    upper = (c >= r).astype(F32)
    return jnp.where(bwd, upper, lower), jnp.where(bwd, lower, upper)


def _bdot(a, b, contract, precision=None):
    return lax.dot_general(a, b, (((contract[0],), (contract[1],)), ((0,), (0,))), preferred_element_type=F32,
                           precision=precision)


def _ssd_kernel(z_ref, xp_ref, xc_ref, xn_ref, bp_ref, bc_ref, bn_ref, sm_ref, dtt_ref,
                cwx_ref, cbx_ref, cwb_ref, cbb_ref, arow_ref, acol_ref, brow_ref, bcol_ref, drow_ref, ng_ref,
                *rest, nc, zero_init):
    st0_ref = None if zero_init else rest[0]
    y_ref, st_ref, yf_s, h_s = rest[0 if zero_init else 1:]
    s = pl.program_id(1)
    bwd = s >= nc
    cidx = _chunk_index(s, nc)
    first = cidx == 0
    last = cidx == nc - 1
    nh = SSD_HEADS

    def initial(direction):
        return jnp.zeros(h_s.shape, F32) if zero_init else st0_ref[:, direction]

    @pl.when(s == 0)
    def _():
        h_s[...] = initial(0)

    @pl.when(s == nc)
    def _():
        h_s[...] = initial(1)

    G = z_ref.shape[0]
    tsel, tsel_t = _tri(bwd, G)
    mask = tsel > 0.0
    r0 = pl.multiple_of(cidx * CHUNK, CHUNK)
    hi = lax.Precision.HIGHEST

    xs = _conv_silu(xp_ref[...], xc_ref[...], xn_ref[...], cwx_ref, cbx_ref, first, last)
    bcv = _conv_silu(bp_ref[...], bc_ref[...], bn_ref[...], cwb_ref, cbb_ref, first, last)
    dt_all = _softplus(sm_ref[:, :, 64:96] + brow_ref[...])
    dtt_all = _softplus(dtt_ref[:, 0] + bcol_ref[...])
    a_all = dt_all * (-jnp.exp(arow_ref[...]))
    at_all = dtt_all * (-jnp.exp(acol_ref[...]))
    dt = jnp.where(bwd, dt_all[:, :, nh:2 * nh], dt_all[:, :, 0:nh])
    a = jnp.where(bwd, a_all[:, :, nh:2 * nh], a_all[:, :, 0:nh])
    at = jnp.where(bwd, at_all[:, nh:2 * nh, :], at_all[:, 0:nh, :])
    cum = _bdot(tsel, a, (2, 1), hi)
    cum_t = _bdot(at, tsel_t, (2, 1), hi)
    tot = jnp.where(bwd, cum[:, 0:1, :], cum[:, CHUNK - 1:CHUNK, :])

    ys = []
    for g in range(2):
        bg = bcv[:, :, g * SSD_STATE:(g + 1) * SSD_STATE]
        cg = bcv[:, :, 256 + g * SSD_STATE:256 + (g + 1) * SSD_STATE]
        cg16 = cg.astype(BF16)
        cb = _bdot(cg16, bg.astype(BF16), (2, 2))
        for hh in range(nh // 2):
            h = g * (nh // 2) + hh
            cq = cum[:, :, h:h + 1]
            decay = jnp.exp(jnp.where(mask, cq - cum_t[:, h:h + 1, :], NEG_INF))
            xd = xs[:, :, h * SSD_HEAD_DIM:(h + 1) * SSD_HEAD_DIM] * dt[:, :, h:h + 1]
            xd16 = xd.astype(BF16)
            hprev = h_s[:, h]
            y = (_bdot((cb * decay).astype(BF16), xd16, (2, 1))
                 + _bdot(cg16, hprev.astype(BF16), (2, 2)) * jnp.exp(cq))
            ys.append(y)
            th = tot[:, :, h:h + 1]
            bw = bg * jnp.exp(th - cq)
            h_s[:, h] = hprev * jnp.exp(th) + _bdot(xd16, bw.astype(BF16), (1, 1))
    y = jnp.concatenate(ys, axis=2)

    @pl.when(jnp.logical_not(bwd))
    def _():
        yf_s[:, pl.ds(r0, CHUNK), :] = y

    @pl.when(bwd)
    def _():
        yt = (yf_s[:, pl.ds(r0, CHUNK), :] + y + drow_ref[...] * xs) * _silu(z_ref[...])
        y_ref[...] = _rms(yt) * ng_ref[...]

    @pl.when(jnp.logical_or(s == nc - 1, s == 2 * nc - 1))
    def _():
        st_ref[:, 0] = h_s[...]


def _ssd(p0, dtt, cwx, cbx, cwb, cbb, a_log, dt_bias, d_skip, norm_g, st0, *, L):
    nseq, nc = p0.shape[0], L // CHUNK
    zero_init = st0 is None
    G = _scan_group(nseq, 8 if zero_init else 4)
    cur, prev, nxt, out = _scan_specs(nc)
    st_shape = (G, 2, SSD_HEADS, SSD_HEAD_DIM, SSD_STATE)
    states = [] if zero_init else [st0]
    row = lambda v: v.reshape(1, -1)
    col = lambda v: v.reshape(-1, 1)
    small = [cwx, row(cbx), cwb, row(cbb), row(a_log), col(a_log), row(dt_bias), col(dt_bias),
             row(jnp.repeat(d_skip, SSD_HEAD_DIM)), row(norm_g)]
    blk = lambda rows, width, index: pl.BlockSpec((G, rows, width), index)
    return pl.pallas_call(
        functools.partial(_ssd_kernel, nc=nc, zero_init=zero_init),
        grid=(nseq // G, 2 * nc),
        in_specs=[blk(CHUNK, 1024, cur(0)),
                  blk(8, 1024, prev(1)), blk(CHUNK, 1024, cur(1)), blk(8, 1024, nxt(1)),
                  blk(8, 512, prev(4)), blk(CHUNK, 512, cur(4)), blk(8, 512, nxt(4)),
                  blk(CHUNK, LANES, cur(28)),
                  pl.BlockSpec((G, 1, 32, CHUNK), lambda b, s: (b, _chunk_index(s, nc), 0, 0))]
        + [_full(v.shape) for v in small]
        + [pl.BlockSpec(st_shape, lambda b, s: (b, 0, 0, 0, 0)) for _ in states],
        out_specs=[blk(CHUNK, SSD_D_INNER, out(0)),
                   pl.BlockSpec((G, 1) + st_shape[2:], lambda b, s: (b, jnp.where(s < nc, 0, 1), 0, 0, 0))],
        out_shape=[jax.ShapeDtypeStruct((nseq, L, SSD_D_INNER), F32),
                   jax.ShapeDtypeStruct((nseq,) + st_shape[1:], F32)],
        scratch_shapes=[pltpu.VMEM((G, L, SSD_D_INNER), F32), pltpu.VMEM(st_shape[:1] + st_shape[2:], F32)],
        compiler_params=_cp(2), name="ssd_mixer",
    )(p0, p0, p0, p0, p0, p0, p0, p0, dtt, *small, *states)


def _rope_tables(L):
    t = jnp.arange(L)
    r = (t // GRID_W).astype(F32)
    c = (t % GRID_W).astype(F32)
    nf = 16
    inv = ROPE_BASE ** (-jnp.arange(nf, dtype=F32) / nf)
    ang = jnp.concatenate([r[:, None] * inv, c[:, None] * inv], axis=-1)
    ang = jnp.concatenate([ang, ang], axis=-1)
    return jnp.cos(ang), jnp.sin(ang)


def _rope128(x, cos, sin):
    lane = lax.broadcasted_iota(jnp.int32, x.shape, 1)
    rot = jnp.where(lane % 64 < 32, -pltpu.roll(x, 96, 1), pltpu.roll(x, 32, 1))
    return x * cos + rot * sin


def _pad_lanes(x):
    return jnp.concatenate([x, jnp.zeros_like(x)], axis=1)


MLA_QB = 256


def _mla_kernel(*refs, L, Lc, rope, emit):
    refs = list(refs)
    qa_ref, kva_ref, sm_ref, qg_ref, kvg_ref, wqn_ref, wqr_ref, wkk_ref, wkv_ref = refs[:9]
    pos = 9
    if Lc:
        cckv_ref, ckpe_ref = refs[pos:pos + 2]
        pos += 2
    if rope:
        cos_ref, sin_ref = refs[pos:pos + 2]
        pos += 2
    y_ref = refs[pos]
    pos += 1
    if emit:
        ckv_ref = refs[pos]
        pos += 1
    qn_s, qr_s, kn_s, v_s, kpe_s = refs[pos:]
    qb = pl.program_id(1)
    scale = np.float32((MLA_NOPE + MLA_ROPE) ** -0.5)

    @pl.when(qb == 0)
    def _():
        ckv = _rms(kva_ref[...]) * kvg_ref[...]
        if emit:
            ckv_ref[...] = ckv
        kpe = _pad_lanes(sm_ref[:, 0:MLA_ROPE])
        if rope:
            kpe = _rope128(kpe, cos_ref[...], sin_ref[...])
        if Lc:
            ckv = jnp.concatenate([cckv_ref[0], ckv], axis=0)
            kpe = jnp.concatenate([_pad_lanes(ckpe_ref[0]), kpe], axis=0)
        c16 = ckv.astype(BF16)
        kn_s[...] = _dot(c16, wkk_ref[...]).astype(BF16)
        v_s[...] = _dot(c16, wkv_ref[...]).astype(BF16)
        kpe_s[...] = kpe.astype(BF16)

    qn = (_rms(qa_ref[...]) * qg_ref[...]).astype(BF16)
    qn_s[...] = _dot(qn, wqn_ref[...]).astype(BF16)
    qr = _dot(qn, wqr_ref[...])
    if rope:
        q0 = pl.multiple_of(qb * MLA_QB, MLA_QB)
        cos = cos_ref[pl.ds(q0, MLA_QB), :]
        sin = sin_ref[pl.ds(q0, MLA_QB), :]
        for h in range(MLA_HEADS):
            qr_s[:, h * LANES:(h + 1) * LANES] = _rope128(qr[:, h * LANES:(h + 1) * LANES], cos, sin).astype(BF16)
    else:
        qr_s[...] = qr.astype(BF16)

    def head(h, _):
        c0 = h * LANES if isinstance(h, int) else pl.multiple_of(h * LANES, LANES)
        s = (_dot_nt(qn_s[:, pl.ds(c0, LANES)], kn_s[:, pl.ds(c0, LANES)])
             + _dot_nt(qr_s[:, pl.ds(c0, LANES)], kpe_s[...])) * scale
        p = jnp.exp(s - jnp.max(s, axis=-1, keepdims=True))
        p = p / jnp.sum(p, axis=-1, keepdims=True)
        y_ref[:, pl.ds(c0, LANES)] = _dot(p.astype(BF16), v_s[:, pl.ds(c0, LANES)])
        return 0

    if L + Lc <= 512:
        for h in range(MLA_HEADS):
            head(h, 0)
    else:
        lax.fori_loop(0, MLA_HEADS, head, 0)


def _mla(p0, qg, kvg, wqn, wqr, wkk, wkv, cache_ckv, cache_kpe, rope_tabs, *, L):
    n = p0.shape[0]
    nseq, nqb = n // L, L // MLA_QB
    Lc = 0 if cache_ckv is None else cache_ckv.shape[1]
    rope = rope_tabs is not None
    emit = cache_ckv is None
    Lk = L + Lc
    args = [p0, p0, p0, qg.reshape(1, -1), kvg.reshape(1, -1), wqn, wqr, wkk, wkv]
    specs = [pl.BlockSpec((MLA_QB, MLA_RANK), lambda b, q: (b * nqb + q, 5)),
             pl.BlockSpec((L, MLA_RANK), lambda b, q: (b, 6)),
             pl.BlockSpec((L, LANES), lambda b, q: (b, 28)),
             _full((1, MLA_RANK)), _full((1, MLA_RANK)), _full(wqn.shape), _full(wqr.shape), _full(wkk.shape),
             _full(wkv.shape)]
    if Lc:
        args += [cache_ckv, cache_kpe]
        specs += [pl.BlockSpec((1, Lc, MLA_RANK), lambda b, q: (b, 0, 0)),
                  pl.BlockSpec((1, Lc, MLA_ROPE), lambda b, q: (b, 0, 0))]
    if rope:
        args += list(rope_tabs)
        specs += [_full((L, LANES)), _full((L, LANES))]
    out_specs = [pl.BlockSpec((MLA_QB, 1024), lambda b, q: (b * nqb + q, 0))]
    out_shape = [jax.ShapeDtypeStruct((n, 1024), F32)]
    if emit:
        out_specs.append(pl.BlockSpec((L, MLA_RANK), lambda b, q: (b, 0)))
        out_shape.append(jax.ShapeDtypeStruct((n, MLA_RANK), F32))
    return pl.pallas_call(
        functools.partial(_mla_kernel, L=L, Lc=Lc, rope=rope, emit=emit),
        grid=(nseq, nqb), in_specs=specs, out_specs=out_specs, out_shape=out_shape,
        scratch_shapes=[pltpu.VMEM((MLA_QB, 1024), BF16), pltpu.VMEM((MLA_QB, 1024), BF16),
                        pltpu.VMEM((Lk, 1024), BF16), pltpu.VMEM((Lk, 1024), BF16), pltpu.VMEM((Lk, LANES), BF16)],
        compiler_params=_cp(2), name="mla_attention",
    )(*args)


def _gqa_ctx_kernel(q_ref, k_ref, v_ref, sink_ref, y_ref):
    q = q_ref[...]
    k = k_ref[...].astype(BF16)
    v = v_ref[...].astype(BF16)
    scale = np.float32(GQA_HEAD_DIM ** -0.5)
    group = GQA_HEADS // GQA_KV_HEADS
    outs = []
    for h in range(GQA_HEADS):
        kh = h // group
        ks = slice(kh * GQA_HEAD_DIM, (kh + 1) * GQA_HEAD_DIM)
        s = _dot_nt(q[:, h * GQA_HEAD_DIM:(h + 1) * GQA_HEAD_DIM].astype(BF16), k[:, ks]) * scale
        m = jnp.maximum(jnp.max(s, axis=-1, keepdims=True), sink_ref[h])
        p = jnp.exp(s - m)
        p = p / (jnp.sum(p, axis=-1, keepdims=True) + jnp.exp(sink_ref[h] - m))
        outs.append(_dot(p.astype(BF16), v[:, ks]))
    y_ref[...] = jnp.concatenate(outs, axis=1)


def _gqa_ctx(p1, sink, *, L):
    n = p1.shape[0]
    return pl.pallas_call(
        _gqa_ctx_kernel,
        grid=(n // L,),
        in_specs=[pl.BlockSpec((L, 1024), lambda b: (b, 2)), pl.BlockSpec((L, 256), lambda b: (b, 20)),
                  pl.BlockSpec((L, 256), lambda b: (b, 21)), pl.BlockSpec(memory_space=pltpu.SMEM)],
        out_specs=pl.BlockSpec((L, 1024), lambda b: (b, 0)),
        out_shape=jax.ShapeDtypeStruct((n, 1024), F32),
        compiler_params=_cp(1), name="gqa_context",
    )(p1, p1, p1, sink)


GQA_QB = 128
GQA_SPAN = GQA_QB + 2 * WINDOW


def _gqa_win_kernel(q_ref, k_ref, v_ref, ck_ref, cv_ref, cos_ref, sin_ref, sink_ref, y_ref, kp_s, vp_s, *, L):
    qb = pl.program_id(1)
    scale = np.float32(GQA_HEAD_DIM ** -0.5)
    group = GQA_HEADS // GQA_KV_HEADS

    @pl.when(qb == 0)
    def _():
        zeros = jnp.zeros((WINDOW, 256), BF16)
        kp_s[0:WINDOW, :] = zeros
        kp_s[WINDOW + L:, :] = zeros
        vp_s[0:WINDOW, :] = zeros
        vp_s[WINDOW + L:, :] = zeros
        for j in range(2):
            cs = slice(j * LANES, (j + 1) * LANES)
            kp_s[WINDOW:WINDOW + L, cs] = _rope128(k_ref[:, cs], cos_ref[...], sin_ref[...]).astype(BF16)
        vp_s[WINDOW:WINDOW + L, :] = v_ref[...].astype(BF16)

    q0 = pl.multiple_of(qb * GQA_QB, GQA_QB)
    cos = cos_ref[pl.ds(q0, GQA_QB), :]
    sin = sin_ref[pl.ds(q0, GQA_QB), :]
    kw = kp_s[pl.ds(q0, GQA_SPAN), :]
    vw = vp_s[pl.ds(q0, GQA_SPAN), :]
    kc = ck_ref[0].astype(BF16)
    vc = cv_ref[0].astype(BF16)
    qpos = q0 + lax.broadcasted_iota(jnp.int32, (GQA_QB, GQA_SPAN), 0)
    kpos = q0 - WINDOW + lax.broadcasted_iota(jnp.int32, (GQA_QB, GQA_SPAN), 1)
    bias = jnp.where(kpos < 0, NEG_INF, jnp.where(kpos >= L, NEG_INF, jnp.where(jnp.abs(qpos - kpos) <= WINDOW, 0.0, NEG_INF)))
    outs = []
    for j in range(GQA_HEADS * GQA_HEAD_DIM // LANES):
        qj = _rope128(q_ref[:, j * LANES:(j + 1) * LANES], cos, sin).astype(BF16)
        for half in range(2):
            h = 2 * j + half
            kh = h // group
            ks = slice(kh * GQA_HEAD_DIM, (kh + 1) * GQA_HEAD_DIM)
            qh = qj[:, half * GQA_HEAD_DIM:(half + 1) * GQA_HEAD_DIM]
            s_loc = _dot_nt(qh, kw[:, ks]) * scale + bias
            s_ctx = _dot_nt(qh, kc[:, ks]) * scale
            m = jnp.maximum(jnp.maximum(jnp.max(s_loc, axis=-1, keepdims=True), jnp.max(s_ctx, axis=-1, keepdims=True)),
                            sink_ref[h])
            p_loc = jnp.exp(s_loc - m)
            p_ctx = jnp.exp(s_ctx - m)
            inv = 1.0 / (jnp.sum(p_loc, axis=-1, keepdims=True) + jnp.sum(p_ctx, axis=-1, keepdims=True)
                         + jnp.exp(sink_ref[h] - m))
            outs.append(_dot((p_loc * inv).astype(BF16), vw[:, ks]) + _dot((p_ctx * inv).astype(BF16), vc[:, ks]))
    y_ref[...] = jnp.concatenate(outs, axis=1)


def _gqa_win(p1, cache_k, cache_v, cos, sin, sink, *, L):
    n = p1.shape[0]
    nseq, nqb = n // L, L // GQA_QB
    Lc = cache_k.shape[1]
    return pl.pallas_call(
        functools.partial(_gqa_win_kernel, L=L),
        grid=(nseq, nqb),
        in_specs=[pl.BlockSpec((GQA_QB, 1024), lambda b, q: (b * nqb + q, 2)),
                  pl.BlockSpec((L, 256), lambda b, q: (b, 20)), pl.BlockSpec((L, 256), lambda b, q: (b, 21)),
                  pl.BlockSpec((1, Lc, 256), lambda b, q: (b, 0, 0)), pl.BlockSpec((1, Lc, 256), lambda b, q: (b, 0, 0)),
                  _full((L, LANES)), _full((L, LANES)), pl.BlockSpec(memory_space=pltpu.SMEM)],
        out_specs=pl.BlockSpec((GQA_QB, 1024), lambda b, q: (b * nqb + q, 0)),
        out_shape=jax.ShapeDtypeStruct((n, 1024), F32),
        scratch_shapes=[pltpu.VMEM((L + 2 * WINDOW, 256), BF16), pltpu.VMEM((L + 2 * WINDOW, 256), BF16)],
        compiler_params=_cp(2), name="gqa_window",
    )(p1, p1, p1, cache_k, cache_v, cos, sin, sink)


def _mlstm_kernel(qp_ref, qc_ref, qn_ref, v_ref, o_ref, sm_ref, gt_ref, cw_ref, cb_ref, brow_ref, bcol_ref, ng_ref,
                  *rest, nc, zero_init):
    init_refs = () if zero_init else rest[:3]
    y_ref, cst_ref, nst_ref, mst_ref, hf_s, c_s, n_s, m_s = rest[len(init_refs):]
    s = pl.program_id(1)
    bwd = s >= nc
    cidx = _chunk_index(s, nc)
    nh = ML_HEADS
    hd = ML_HEAD_DIM

    def load_initial(direction):
        for k, dst in enumerate((c_s, n_s, m_s)):
            dst[...] = jnp.zeros(dst.shape, F32) if zero_init else init_refs[k][:, direction]

    @pl.when(s == 0)
    def _():
        load_initial(0)

    @pl.when(s == nc)
    def _():
        cst_ref[:, 0] = c_s[...]
        nst_ref[:, 0] = n_s[...]
        mst_ref[:, 0] = m_s[...]
        load_initial(1)

    G = qc_ref.shape[0]
    tsel, tsel_t = _tri(bwd, G)
    mask = tsel > 0.0
    kscale = np.float32(hd ** -0.5)
    r0 = pl.multiple_of(cidx * CHUNK, CHUNK)
    hi = lax.Precision.HIGHEST

    qk = _conv_silu(qp_ref[...], qc_ref[...], qn_ref[...], cw_ref, cb_ref, cidx == 0, cidx == nc - 1)
    gates = sm_ref[:, :, 0:4 * nh] + brow_ref[...]
    gates_t = gt_ref[:, 0] + bcol_ref[...]
    li = jnp.where(bwd, gates[:, :, nh:2 * nh], gates[:, :, 0:nh])
    li_t = jnp.where(bwd, gates_t[:, nh:2 * nh, :], gates_t[:, 0:nh, :])
    lf = -_softplus(-jnp.where(bwd, gates[:, :, 3 * nh:4 * nh], gates[:, :, 2 * nh:3 * nh]))
    lf_t = -_softplus(-jnp.where(bwd, gates_t[:, 3 * nh:4 * nh, :], gates_t[:, 2 * nh:3 * nh, :]))
    bc = _bdot(tsel, lf, (2, 1), hi)
    bc_t = _bdot(lf_t, tsel_t, (2, 1), hi)
    tot = jnp.where(bwd, bc[:, 0:1, :], bc[:, CHUNK - 1:CHUNK, :])

    hs = []
    for h in range(nh):
        cs = slice(h * hd, (h + 1) * hd)
        q = qk[:, :, cs]
        k = qk[:, :, ML_D + h * hd:ML_D + (h + 1) * hd] * kscale
        q16, k16 = q.astype(BF16), k.astype(BF16)
        v = v_ref[:, :, cs]
        bq = bc[:, :, h:h + 1]
        m_prev = m_s[:, h:h + 1, 0:1]
        log_d = jnp.where(mask, bq - bc_t[:, h:h + 1, :] + li_t[:, h:h + 1, :], NEG_INF)
        log_inter = bq + m_prev
        m_out = jnp.maximum(log_inter, jnp.max(log_d, axis=2, keepdims=True))
        sd = _bdot(q16, k16, (2, 2)) * jnp.exp(log_d - m_out)
        w_inter = jnp.exp(log_inter - m_out)
        c_prev = c_s[:, h]
        n_prev = n_s[:, h:h + 1, :]
        num = _bdot(sd.astype(BF16), v.astype(BF16), (2, 1)) + w_inter * _bdot(q16, c_prev.astype(BF16), (2, 2))
        den = jnp.sum(sd, axis=2, keepdims=True) + w_inter * jnp.sum(q * n_prev, axis=2, keepdims=True)
        hs.append(num / jnp.maximum(jnp.abs(den), jnp.exp(-m_out)))
        th = tot[:, :, h:h + 1]
        end_inter = th + m_prev
        end_intra = th - bq + li[:, :, h:h + 1]
        m_new = jnp.maximum(end_inter, jnp.max(end_intra, axis=1, keepdims=True))
        w_c = jnp.exp(end_inter - m_new)
        w_k = jnp.exp(end_intra - m_new)
        c_s[:, h] = w_c * c_prev + _bdot((v * w_k).astype(BF16), k16, (1, 1))
        n_s[:, h:h + 1, :] = w_c * n_prev + jnp.sum(k * w_k, axis=1, keepdims=True)
        m_s[:, h:h + 1, :] = jnp.broadcast_to(m_new, (G, 1, hd))
    hcat = jnp.concatenate(hs, axis=2)

    @pl.when(jnp.logical_not(bwd))
    def _():
        hf_s[:, pl.ds(r0, CHUNK), :] = hcat

    @pl.when(bwd)
    def _():
        tot_h = hf_s[:, pl.ds(r0, CHUNK), :] + hcat
        normed = jnp.concatenate([_rms(tot_h[:, :, h * hd:(h + 1) * hd]) for h in range(nh)], axis=2) * ng_ref[...]
        y_ref[...] = normed * jax.nn.sigmoid(o_ref[...])

    @pl.when(s == 2 * nc - 1)
    def _():
        cst_ref[:, 1] = c_s[...]
        nst_ref[:, 1] = n_s[...]
        mst_ref[:, 1] = m_s[...]


def _mlstm(p1, gt, conv_w, conv_b, ig_b, fg_b, norm_g, c0, n0, m0, *, L):
    nseq, nc = p1.shape[0], L // CHUNK
    zero_init = c0 is None
    G = _scan_group(nseq, 8 if zero_init else 4)
    cur, prev, nxt, out = _scan_specs(nc)
    cshape = (G, 2, ML_HEADS, ML_HEAD_DIM, ML_HEAD_DIM)
    nshape = (G, 2, ML_HEADS, ML_HEAD_DIM)
    states = [] if zero_init else [c0, n0, m0]
    gb = jnp.concatenate([ig_b.reshape(-1), fg_b.reshape(-1)])
    small = [conv_w, conv_b.reshape(1, -1), gb.reshape(1, -1), gb.reshape(-1, 1), norm_g.reshape(1, -1)]
    st = lambda shape: pl.BlockSpec(shape, lambda b, s: (b,) + (0,) * (len(shape) - 1))
    blk = lambda rows, width, index: pl.BlockSpec((G, rows, width), index)
    return pl.pallas_call(
        functools.partial(_mlstm_kernel, nc=nc, zero_init=zero_init),
        grid=(nseq // G, 2 * nc),
        in_specs=[blk(8, 2048, prev(0)), blk(CHUNK, 2048, cur(0)), blk(8, 2048, nxt(0)),
                  blk(CHUNK, 1024, cur(3)), blk(CHUNK, 1024, cur(4)), blk(CHUNK, LANES, cur(44)),
                  pl.BlockSpec((G, 1, 32, CHUNK), lambda b, s: (b, _chunk_index(s, nc), 0, 0))]
        + [_full(v.shape) for v in small] + ([] if zero_init else [st(cshape), st(nshape), st(nshape)]),
        out_specs=[blk(CHUNK, ML_D, out(0)), st(cshape), st(nshape), st(nshape)],
        out_shape=[jax.ShapeDtypeStruct((nseq, L, ML_D), F32), jax.ShapeDtypeStruct((nseq,) + cshape[1:], F32),
                   jax.ShapeDtypeStruct((nseq,) + nshape[1:], F32), jax.ShapeDtypeStruct((nseq,) + nshape[1:], F32)],
        scratch_shapes=[pltpu.VMEM((G, L, ML_D), F32), pltpu.VMEM(cshape[:1] + cshape[2:], F32),
                        pltpu.VMEM(nshape[:1] + nshape[2:], F32), pltpu.VMEM(nshape[:1] + nshape[2:], F32)],
        compiler_params=_cp(2), name="mlstm_mixer",
    )(p1, p1, p1, p1, p1, p1, gt, *small, *states)


def _prep_layer0(w_in, w_qb, w_kvb, w_out):
    pad = jnp.zeros((D_MODEL, L0_COLS - 3680), F32)
    w = jnp.concatenate([w_in[:, :2560], w_in[:, 2592:3680], w_in[:, 2560:2592], pad], axis=1).astype(BF16)
    q3 = w_qb.reshape(MLA_RANK, MLA_HEADS, MLA_NOPE + MLA_ROPE)
    wqn = q3[:, :, :MLA_NOPE].reshape(MLA_RANK, -1).astype(BF16)
    wqr = jnp.concatenate([q3[:, :, MLA_NOPE:], jnp.zeros((MLA_RANK, MLA_HEADS, LANES - MLA_ROPE), F32)], axis=-1)
    wqr = wqr.reshape(MLA_RANK, -1).astype(BF16)
    k3 = w_kvb.reshape(MLA_RANK, MLA_HEADS, MLA_NOPE + MLA_V)
    wkk = k3[:, :, :MLA_NOPE].reshape(MLA_RANK, -1).astype(BF16)
    wkv = k3[:, :, MLA_NOPE:].reshape(MLA_RANK, -1).astype(BF16)
    return w, wqn, wqr, wkk, wkv, w_out.astype(BF16)


def _prep_layer1(w_in, w_out):
    pad = jnp.zeros((D_MODEL, L1_COLS - 5664), F32)
    w = jnp.concatenate([w_in[:, 1536:3584], w_in[:, 0:1024], w_in[:, 3584:5632], w_in[:, 1024:1536],
                         w_in[:, 5632:5664], pad], axis=1).astype(BF16)
    return w, w_out.astype(BF16)


def _prep_peer(wq, keys, u, v):
    return (wq.T.astype(BF16), keys.reshape(PEER_HEADS * 2, PEER_NKEYS, -1).astype(BF16), u.astype(BF16),
            v.T.astype(BF16))


def _chunk_transposed(cols, L):
    return cols.reshape(-1, L // CHUNK, CHUNK, cols.shape[1]).transpose(0, 1, 3, 2)


def _trunk(x, L, mod0, mod1, w0, w1, peer0, peer1, p, ctx, rope_tabs, mod_base, rows_per_cond, fg):
    kw = dict(mod_base=mod_base, rows_per_cond=rows_per_cond)
    w_in0, wqn, wqr, wkk, wkv, w_out0 = w0
    w_in1, w_out1 = w1
    nseq = x.shape[0] // L

    p0 = _proj(x, p['l0_norm1_g'], mod0, w_in0, tn=1280, **kw)
    cw, cb = p['l0_ssd_conv_w'], p['l0_ssd_conv_b']
    y_ssd, ssd_state = _ssd(p0.reshape(nseq, L, -1), _chunk_transposed(p0[:, 3648:3680], L), cw[:, :1024], cb[:1024],
                            cw[:, 1024:], cb[1024:], p['l0_ssd_A_log'], p['l0_ssd_dt_bias'], p['l0_ssd_D'],
                            p['l0_ssd_norm_g'], None if ctx is None else ctx[0], L=L)
    y_ssd = y_ssd.reshape(nseq * L, -1)
    mla_tabs = None
    if rope_tabs is not None:
        z64 = jnp.zeros_like(rope_tabs[0])
        mla_tabs = (jnp.concatenate([rope_tabs[0], z64], axis=1), jnp.concatenate([rope_tabs[1], z64], axis=1))
    mla_out = _mla(p0, p['l0_mla_q_norm_g'], p['l0_mla_kv_norm_g'], wqn, wqr, wkk, wkv,
                   None if ctx is None else ctx[1], None if ctx is None else ctx[2], mla_tabs, L=L)
    x = _outproj(y_ssd, mla_out[0], w_out0, x, mod0, **kw)
    x = _peer(x, p['l0_norm2_g'], mod0, *peer0, fg, final_norm=False, **kw)

    p1 = _proj(x, p['l1_norm1_g'], mod1, w_in1, tn=1152, **kw)
    if ctx is None:
        y_gqa = _gqa_ctx(p1, p['l1_gqa_sink'], L=L)
        c0 = n0 = m0 = None
    else:
        gqa_tabs = tuple(jnp.concatenate([t, t], axis=1) for t in rope_tabs)
        y_gqa = _gqa_win(p1, ctx[3].reshape(nseq, -1, 256), ctx[4].reshape(nseq, -1, 256), *gqa_tabs,
                         p['l1_gqa_sink'], L=L)
        c0, n0 = ctx[5], ctx[6]
        m0 = jnp.broadcast_to(ctx[7][..., None], ctx[7].shape + (ML_HEAD_DIM,))
    y_ml, mc, mn, mm = _mlstm(p1.reshape(nseq, L, -1), _chunk_transposed(p1[:, 5632:5664], L), p['l1_ml_conv_w'],
                              p['l1_ml_conv_b'], p['l1_ml_ig_b'], p['l1_ml_fg_b'], p['l1_ml_norm_g'], c0, n0, m0, L=L)
    y_ml = y_ml.reshape(nseq * L, -1)
    x = _outproj(y_gqa, y_ml, w_out1, x, mod1, **kw)
    y = _peer(x, p['l1_norm2_g'], mod1, *peer1, fg, final_norm=True, **kw)
    new = None
    if ctx is None:
        new = (ssd_state, mla_out[1].reshape(nseq, L, MLA_RANK), p0[:, 3584:3648].reshape(nseq, L, MLA_ROPE),
               p1[:, 5120:5376].reshape(nseq, L, GQA_KV_HEADS, GQA_HEAD_DIM),
               p1[:, 5376:5632].reshape(nseq, L, GQA_KV_HEADS, GQA_HEAD_DIM), mc, mn, mm[..., 0])
    return y, new


def kernel(x_prompt, x_sample, state_l0_ssd, cache_l0_mla_ckv, cache_l0_mla_kpe, cache_l1_gqa_k, cache_l1_gqa_v, state_l1_mlstm_C, state_l1_mlstm_n, state_l1_mlstm_m, c, c_ctx, final_norm_g, l0_ada_w, l0_ada_b, l0_norm1_g, l0_norm2_g, l0_w_in, l0_ssd_conv_w, l0_ssd_conv_b, l0_ssd_A_log, l0_ssd_dt_bias, l0_ssd_D, l0_ssd_norm_g, l0_mla_q_norm_g, l0_mla_w_qb, l0_mla_kv_norm_g, l0_mla_w_kvb, l0_w_out, l0_peer_wq, l0_peer_keys, l0_peer_u, l0_peer_v, l1_ada_w, l1_ada_b, l1_norm1_g, l1_norm2_g, l1_w_in, l1_gqa_sink, l1_ml_conv_w, l1_ml_conv_b, l1_ml_ig_b, l1_ml_fg_b, l1_ml_norm_g, l1_w_out, l1_peer_wq, l1_peer_keys, l1_peer_u, l1_peer_v):
    p = dict(l0_norm1_g=l0_norm1_g, l0_norm2_g=l0_norm2_g, l0_ssd_conv_w=l0_ssd_conv_w, l0_ssd_conv_b=l0_ssd_conv_b,
             l0_ssd_A_log=l0_ssd_A_log, l0_ssd_dt_bias=l0_ssd_dt_bias, l0_ssd_D=l0_ssd_D, l0_ssd_norm_g=l0_ssd_norm_g,
             l0_mla_q_norm_g=l0_mla_q_norm_g, l0_mla_kv_norm_g=l0_mla_kv_norm_g, l1_norm1_g=l1_norm1_g,
             l1_norm2_g=l1_norm2_g, l1_gqa_sink=l1_gqa_sink, l1_ml_conv_w=l1_ml_conv_w, l1_ml_conv_b=l1_ml_conv_b,
             l1_ml_ig_b=l1_ml_ig_b, l1_ml_fg_b=l1_ml_fg_b, l1_ml_norm_g=l1_ml_norm_g)
    nb, seq = x_prompt.shape[:2]
    db, dseq = x_sample.shape[:2]
    assert db <= 7 and seq % MLA_QB == 0 and dseq % MLA_QB == 0

    cond8 = jnp.zeros((8, D_MODEL), F32).at[0].set(c_ctx).at[1:1 + db].set(c)
    mod0 = _ada(cond8, l0_ada_w, l0_ada_b).reshape(8, 1, 6 * D_MODEL)
    mod1 = _ada(cond8, l1_ada_w, l1_ada_b).reshape(8, 1, 6 * D_MODEL)
    w0 = _prep_layer0(l0_w_in, l0_mla_w_qb, l0_mla_w_kvb, l0_w_out)
    w1 = _prep_layer1(l1_w_in, l1_w_out)
    peer0 = _prep_peer(l0_peer_wq, l0_peer_keys, l0_peer_u, l0_peer_v)
    peer1 = _prep_peer(l1_peer_wq, l1_peer_keys, l1_peer_u, l1_peer_v)

    y_prompt, new = _trunk(x_prompt.reshape(nb * seq, D_MODEL), seq, mod0, mod1, w0, w1, peer0, peer1, p, None, None,
                           0, nb * seq, final_norm_g)
    ctx = (state_l0_ssd, cache_l0_mla_ckv, cache_l0_mla_kpe, cache_l1_gqa_k, cache_l1_gqa_v, state_l1_mlstm_C,
           state_l1_mlstm_n, state_l1_mlstm_m)
    y_sample, _ = _trunk(x_sample.reshape(db * dseq, D_MODEL), dseq, mod0, mod1, w0, w1, peer0, peer1, p, ctx,
                         _rope_tables(dseq), 1, dseq, final_norm_g)
    return (y_prompt.reshape(nb, seq, D_MODEL), y_sample.reshape(db, dseq, D_MODEL)) + new
```

```python
import functools

import jax
import jax.numpy as jnp
import numpy as np
from jax import lax
from jax.experimental import pallas as pl
from jax.experimental.pallas import tpu as pltpu

F32 = jnp.float32
BF16 = jnp.bfloat16
NEG_INF = float("-inf")

D_MODEL = 2048
EPS = 1e-6
ROPE_BASE = 10000.0
GRID_W = 64
CONV_W = 5
CHUNK = 64
SSD_HEADS = 16
SSD_HEAD_DIM = 64
SSD_D_INNER = 1024
SSD_STATE = 128
MLA_HEADS = 8
MLA_NOPE = 128
MLA_ROPE = 64
MLA_V = 128
MLA_RANK = 512
GQA_HEADS = 16
GQA_KV_HEADS = 4
GQA_HEAD_DIM = 64
WINDOW = 128
ML_HEADS = 8
ML_HEAD_DIM = 128
ML_D = 1024
PEER_HEADS = 8
PEER_NKEYS = 128
PEER_N = PEER_NKEYS * PEER_NKEYS
PEER_TOPK = 16

LANES = 128
SUBLANES = 8
VMEM_LIMIT = 56 * 2**20

L0_COLS = 3840
L1_COLS = 5760


def _cp(n_grid, flags=None):
    return pltpu.CompilerParams(dimension_semantics=("arbitrary",) * n_grid, vmem_limit_bytes=VMEM_LIMIT, flags=flags)


def _silu(x):
    return x * jax.nn.sigmoid(x)


def _softplus(x):
    return jnp.maximum(x, 0.0) + jnp.log1p(jnp.exp(-jnp.abs(x)))


def _rms(x):
    return x * lax.rsqrt(jnp.mean(x * x, axis=-1, keepdims=True) + EPS)


def _dot(a, b):
    return jnp.dot(a, b, preferred_element_type=F32)


def _dot_nt(a, b):
    return lax.dot_general(a, b, (((1,), (1,)), ((), ())), preferred_element_type=F32)


def _full(shape):
    nd = len(shape)
    return pl.BlockSpec(shape, lambda *_: (0,) * nd)


def _mod_spec(chunk, tb, mod_base, rows_per_cond):
    def index(i, *_):
        return (mod_base + (i * tb) // rows_per_cond, 0, chunk)
    return pl.BlockSpec((1, 1, D_MODEL), index)


def _ada_kernel(c_ref, w_ref, b_ref, o_ref):
    s = _silu(c_ref[...])
    o_ref[...] = _dot(s.astype(BF16), w_ref[...].astype(BF16)) + b_ref[...]


def _ada(cond8, w, b):
    n = w.shape[1]
    tn = 1024
    return pl.pallas_call(
        _ada_kernel,
        grid=(n // tn,),
        in_specs=[_full((8, D_MODEL)), pl.BlockSpec((D_MODEL, tn), lambda j: (0, j)),
                  pl.BlockSpec((1, tn), lambda j: (0, j))],
        out_specs=pl.BlockSpec((8, tn), lambda j: (0, j)),
        out_shape=jax.ShapeDtypeStruct((8, n), F32),
        compiler_params=_cp(1), name="ada_table",
    )(cond8, w, b.reshape(1, n))


def _proj_kernel(x_ref, g_ref, sc_ref, sh_ref, w_ref, o_ref, h_ref):
    @pl.when(pl.program_id(1) == 0)
    def _():
        h = _rms(x_ref[...]) * g_ref[...] * (1.0 + sc_ref[0]) + sh_ref[0]
        h_ref[...] = h.astype(BF16)
    o_ref[...] = _dot(h_ref[...], w_ref[...])


def _proj(x, g, mod, w, *, tn, mod_base, rows_per_cond):
    n, ncol = x.shape[0], w.shape[1]
    tb = 1024
    return pl.pallas_call(
        _proj_kernel,
        grid=(n // tb, ncol // tn),
        in_specs=[pl.BlockSpec((tb, D_MODEL), lambda i, j: (i, 0)), _full((1, D_MODEL)),
                  _mod_spec(1, tb, mod_base, rows_per_cond), _mod_spec(0, tb, mod_base, rows_per_cond),
                  pl.BlockSpec((D_MODEL, tn), lambda i, j: (0, j))],
        out_specs=pl.BlockSpec((tb, tn), lambda i, j: (i, j)),
        out_shape=jax.ShapeDtypeStruct((n, ncol), F32),
        scratch_shapes=[pltpu.VMEM((tb, D_MODEL), BF16)],
        compiler_params=_cp(2), name="norm_mod_proj",
    )(x, g.reshape(1, D_MODEL), mod, mod, w)


def _outproj_kernel(ya_ref, yb_ref, w_ref, x_ref, g1_ref, o_ref):
    half = ya_ref.shape[1]
    y = _dot(ya_ref[...].astype(BF16), w_ref[0:half, :]) + _dot(yb_ref[...].astype(BF16), w_ref[half:, :])
    o_ref[...] = x_ref[...] + g1_ref[0] * y


def _outproj(ya, yb, w, x, mod, *, mod_base, rows_per_cond):
    n = x.shape[0]
    tb = 512
    return pl.pallas_call(
        _outproj_kernel,
        grid=(n // tb,),
        in_specs=[pl.BlockSpec((tb, ya.shape[1]), lambda i: (i, 0)), pl.BlockSpec((tb, yb.shape[1]), lambda i: (i, 0)),
                  _full(w.shape), pl.BlockSpec((tb, D_MODEL), lambda i: (i, 0)),
                  _mod_spec(2, tb, mod_base, rows_per_cond)],
        out_specs=pl.BlockSpec((tb, D_MODEL), lambda i: (i, 0)),
        out_shape=jax.ShapeDtypeStruct((n, D_MODEL), F32),
        compiler_params=_cp(1), name="out_proj_residual",
    )(ya, yb, w, x, mod)


ROUTE_TB = 512
_CAND_KEEP = (16, 8, 5, 4, 3, 2, 2, 2)


def _sort_pairs(n):
    pairs = []

    def merge(lo, hi, r):
        step = r * 2
        if step < hi - lo:
            merge(lo, hi, step)
            merge(lo + r, hi, step)
            pairs.extend((i, i + r) for i in range(lo + r, hi - r, step))
        else:
            pairs.append((lo, lo + r))

    def sort(lo, hi):
        if hi - lo >= 1:
            mid = lo + (hi - lo) // 2
            sort(lo, mid)
            sort(mid + 1, hi)
            merge(lo, hi, 1)

    sort(0, 15)
    return [(i, j) for i, j in pairs if j < n]


def _top_values(x, count):
    groups = [x[r:r + SUBLANES] for r in range(0, x.shape[0], SUBLANES)]
    n = len(groups)
    for i, j in _sort_pairs(n):
        groups[i], groups[j] = jnp.maximum(groups[i], groups[j]), jnp.minimum(groups[i], groups[j])
    sub = lax.broadcasted_iota(jnp.int32, groups[0].shape, 0)
    tops = []
    for t in range(count):
        head = groups[0]
        m = jnp.max(head, axis=0, keepdims=True)
        tops.append(m)
        if t == count - 1:
            break
        first = jnp.min(jnp.where(head == m, sub, SUBLANES), axis=0, keepdims=True)
        popped = sub == first
        depth = min(count - 1 - t, n)
        for r in range(depth):
            below = groups[r + 1] if r + 1 < n else NEG_INF
            groups[r] = jnp.where(popped, below, groups[r])
    return tops


def _route_kernel(x_ref, g_ref, sc_ref, sh_ref, wqt_ref, keys_ref,
                  hb_ref, th_ref, s2_ref, e2_ref, c1_ref, qt_s, top_s):
    tb = x_ref.shape[0]
    h = _rms(x_ref[...]) * g_ref[...] * (1.0 + sc_ref[0]) + sh_ref[0]
    hb = h.astype(BF16)
    hb_ref[...] = hb
    qt_s[...] = _dot_nt(wqt_ref[...], hb).astype(BF16)

    iota_8 = lax.broadcasted_iota(jnp.int32, (SUBLANES, tb), 0)

    def head(hd, _):
        scores = []
        for c in range(2):
            r0 = pl.multiple_of((hd * 2 + c) * PEER_NKEYS, PEER_NKEYS)
            s = _dot(keys_ref[hd * 2 + c], qt_s[pl.ds(r0, PEER_NKEYS), :])
            scores.append(s)
            for it, m in enumerate(_top_values(s, PEER_TOPK)):
                top_s[c, it:it + 1, :] = m
        a1 = top_s[0]
        a2 = top_s[1]
        pieces = [a1[0:1] + a2, a1[1:2] + a2[0:8]]
        for i in range(2, 8):
            pieces.append(jnp.where(iota_8 < _CAND_KEEP[i], a1[i:i + 1] + a2[0:8], NEG_INF))
        pieces.append(a1[8:16] + a2[0:1])
        cand = jnp.concatenate(pieces, axis=0)
        tau = _top_values(cand, PEER_TOPK)[-1]
        top = a1[0:1] + a2[0:1]
        z = jnp.sum(jnp.where(cand >= tau, jnp.exp(cand - top), 0.0), axis=0, keepdims=True)
        e2 = jnp.exp(scores[1] - a2[0:1])
        c1 = jnp.exp(scores[0] - a1[0:1]) / (2.0 * z)
        theta = jnp.full(scores[0].shape, jnp.inf, F32)
        for q in range(PEER_TOPK):
            a2q = a2[q:q + 1]
            phi = jnp.min(jnp.where(a1 + a2q >= tau, a1, jnp.inf), axis=0, keepdims=True)
            theta = jnp.where(scores[0] >= phi, a2q, theta)
        for t in range(tb // LANES):
            ts = slice(t * LANES, (t + 1) * LANES)
            th_ref[hd, t] = theta[:, ts]
            s2_ref[hd, t] = scores[1][:, ts]
            e2_ref[hd, t] = e2[:, ts]
            c1_ref[hd, t] = c1[:, ts]
        return 0

    lax.fori_loop(0, PEER_HEADS, head, 0)


def _route(x, g, mod, wqt, keys, *, mod_base, rows_per_cond):
    n = x.shape[0]
    tb = ROUTE_TB
    ntg = tb // LANES
    hk = (PEER_HEADS, n // LANES, PEER_NKEYS, LANES)
    rspec = pl.BlockSpec((PEER_HEADS, ntg, PEER_NKEYS, LANES), lambda i: (0, i, 0, 0))
    return pl.pallas_call(
        _route_kernel,
        grid=(n // tb,),
        in_specs=[pl.BlockSpec((tb, D_MODEL), lambda i: (i, 0)), _full((1, D_MODEL)),
                  _mod_spec(4, tb, mod_base, rows_per_cond), _mod_spec(3, tb, mod_base, rows_per_cond),
                  _full(wqt.shape), _full(keys.shape)],
        out_specs=[pl.BlockSpec((tb, D_MODEL), lambda i: (i, 0)), rspec, rspec, rspec, rspec],
        out_shape=[jax.ShapeDtypeStruct((n, D_MODEL), BF16)] + [jax.ShapeDtypeStruct(hk, F32)] * 4,
        scratch_shapes=[pltpu.VMEM((PEER_HEADS * 2 * PEER_NKEYS, tb), BF16), pltpu.VMEM((2, PEER_TOPK, tb), F32)],
        compiler_params=_cp(1), name="peer_route",
    )(x, g.reshape(1, D_MODEL), mod, mod, wqt, keys)


PEER_TB = 512
PEER_EC = 1024
PEER_SUB = PEER_EC // 2


def _experts_kernel(hb_ref, th_ref, s2_ref, e2_ref, c1_ref, u0_ref, ub_ref, un_ref, vp_ref, va_ref, vl_ref,
                    x_ref, g2_ref, fg_ref, o_ref, acc_s, st0_s, st1_s, a0_s, a1_s, *, final_norm):
    e = pl.program_id(1)
    tb = hb_ref.shape[0]
    ntg = tb // LANES
    rows_per_sub = PEER_SUB // PEER_NKEYS

    def scores(u_blk, st_s, h):
        res = _dot_nt(u_blk[...], hb_ref[h * 2 * LANES:(h + 1) * 2 * LANES, :])
        st_s[2 * h] = res[:, :LANES]
        st_s[2 * h + 1] = res[:, LANES:]

    def values(vt_blk, act_s, h):
        act = jnp.concatenate([act_s[2 * h], act_s[2 * h + 1]], axis=1)
        hs = slice(h * 2 * LANES, (h + 1) * 2 * LANES)
        for r in range(0, D_MODEL, PEER_SUB):
            acc_s[r:r + PEER_SUB, hs] += _dot(vt_blk[r:r + PEER_SUB, :], act)

    def gates(c, tg, st_s, act_s):
        sub = 32
        for k in range(rows_per_sub):
            row = c * rows_per_sub + k
            for j0 in range(0, PEER_NKEYS, sub):
                js = slice(j0, j0 + sub)
                ks = slice(k * PEER_NKEYS + j0, k * PEER_NKEYS + j0 + sub)
                w = None
                for hd in range(PEER_HEADS):
                    chosen = s2_ref[hd, tg, js, :] >= th_ref[hd, tg, row:row + 1, :]
                    term = jnp.where(chosen, e2_ref[hd, tg, js, :], 0.0) * c1_ref[hd, tg, row:row + 1, :]
                    w = term if w is None else w + term
                s = st_s[tg, ks, :]
                act = s * (1.0 + lax.erf(s * np.float32(2.0 ** -0.5)))
                act_s[tg, ks, :] = (act * w).astype(BF16)

    @pl.when(e == 0)
    def _():
        acc_s[...] = jnp.zeros_like(acc_s)
        a1_s[...] = jnp.zeros_like(a1_s)
        scores(u0_ref, st0_s, 0)
        scores(u0_ref, st0_s, 1)

    def phase(c, st_cur, act_cur, u_next, st_next, vt_prev, act_prev):
        for h in range(2):
            scores(u_next, st_next, h)
            gates(c, 2 * h, st_cur, act_cur)
            values(vt_prev, act_prev, h)
            gates(c, 2 * h + 1, st_cur, act_cur)

    phase(0, st0_s, a0_s, ub_ref, st1_s, vp_ref, a1_s)
    phase(1, st1_s, a1_s, un_ref, st0_s, va_ref, a0_s)

    @pl.when(e == pl.num_programs(1) - 1)
    def _():
        values(vl_ref, a1_s, 0)
        values(vl_ref, a1_s, 1)
        y = x_ref[...] + g2_ref[0] * acc_s[...].T
        if final_norm:
            y = _rms(y) * fg_ref[...]
        o_ref[...] = y


def _experts(hb, theta, s2, e2, c1, u, vt, x, mod, fg, *, mod_base, rows_per_cond, final_norm):
    n = x.shape[0]
    tb = PEER_TB
    ntg = tb // LANES
    ne = PEER_N // PEER_EC
    once = dict(pipeline_mode=pl.Buffered(1))
    rspec = pl.BlockSpec((PEER_HEADS, ntg, PEER_NKEYS, LANES), lambda i, e: (0, i, 0, 0), **once)
    rows8 = pl.BlockSpec((PEER_HEADS, ntg, SUBLANES, LANES), lambda i, e: (0, i, e, 0))
    uspec = lambda index: pl.BlockSpec((PEER_SUB, D_MODEL), index)
    vspec = lambda index: pl.BlockSpec((D_MODEL, PEER_SUB), index)
    return pl.pallas_call(
        functools.partial(_experts_kernel, final_norm=final_norm),
        grid=(n // tb, ne),
        in_specs=[pl.BlockSpec((tb, D_MODEL), lambda i, e: (i, 0)), rows8, rspec, rspec, rows8,
                  pl.BlockSpec((PEER_SUB, D_MODEL), lambda i, e: (0, 0), **once),
                  uspec(lambda i, e: (2 * e + 1, 0)),
                  uspec(lambda i, e: (jnp.minimum(2 * e + 2, 2 * ne - 2), 0)),
                  vspec(lambda i, e: (0, jnp.maximum(2 * e - 1, 0))),
                  vspec(lambda i, e: (0, 2 * e)),
                  pl.BlockSpec((D_MODEL, PEER_SUB), lambda i, e: (0, 2 * ne - 1), **once),
                  pl.BlockSpec((tb, D_MODEL), lambda i, e: (i, 0)),
                  _mod_spec(5, tb, mod_base, rows_per_cond), _full((1, D_MODEL))],
        out_specs=pl.BlockSpec((tb, D_MODEL), lambda i, e: (i, 0)),
        out_shape=jax.ShapeDtypeStruct((n, D_MODEL), F32),
        scratch_shapes=[pltpu.VMEM((D_MODEL, tb), F32), pltpu.VMEM((ntg, PEER_SUB, LANES), F32),
                        pltpu.VMEM((ntg, PEER_SUB, LANES), F32), pltpu.VMEM((ntg, PEER_SUB, LANES), BF16),
                        pltpu.VMEM((ntg, PEER_SUB, LANES), BF16)],
        compiler_params=_cp(2), name="peer_experts",
    )(hb, theta, s2, e2, c1, u, u, u, vt, vt, vt, x, mod, fg.reshape(1, D_MODEL))


def _peer(x, g, mod, wqt, keys, u, vt, fg, *, mod_base, rows_per_cond, final_norm):
    hb, theta, s2, e2, c1 = _route(x, g, mod, wqt, keys, mod_base=mod_base, rows_per_cond=rows_per_cond)
    return _experts(hb, theta, s2, e2, c1, u, vt, x, mod, fg, mod_base=mod_base, rows_per_cond=rows_per_cond,
                    final_norm=final_norm)


def _chunk_index(s, nc):
    return jnp.where(s < nc, s, 2 * nc - 1 - s)


def _scan_group(nseq, largest):
    return max(g for g in (2, 4, 8) if g <= largest and nseq % g == 0)


def _scan_specs(nc):
    def cur(col):
        return lambda b, s: (b, _chunk_index(s, nc), col)

    def prev(col):
        return lambda b, s: (b, jnp.maximum(_chunk_index(s, nc) * 8 - 1, 0), col)

    def nxt(col):
        return lambda b, s: (b, jnp.minimum(_chunk_index(s, nc) * 8 + 8, nc * 8 - 1), col)

    def out(col):
        return lambda b, s: (b, jnp.where(s < nc, nc - 1, 2 * nc - 1 - s), col)
    return cur, prev, nxt, out


def _conv_silu(prev, cur, nxt, w_ref, b_ref, first, last):
    win = jnp.concatenate([jnp.where(first, 0.0, prev), cur, jnp.where(last, 0.0, nxt)], axis=1)
    acc = b_ref[...] + win[:, 6:70, :] * w_ref[0:1, :]
    for k in range(1, CONV_W):
        acc = acc + win[:, 6 + k:70 + k, :] * w_ref[k:k + 1, :]
    return _silu(acc)


def _tri(bwd, G):
    r = lax.broadcasted_iota(jnp.int32, (G, CHUNK, CHUNK), 1)
    c = lax.broadcasted_iota(jnp.int32, (G, CHUNK, CHUNK), 2)
    lower = (c <= r).astype(F32)
    upper = (c >= r).astype(F32)
    return jnp.where(bwd, upper, lower), jnp.where(bwd, lower, upper)


def _bdot(a, b, contract, precision=None):
    return lax.dot_general(a, b, (((contract[0],), (contract[1],)), ((0,), (0,))), preferred_element_type=F32,
                           precision=precision)


def _ssd_kernel(z_ref, xp_ref, xc_ref, xn_ref, bp_ref, bc_ref, bn_ref, sm_ref, dtt_ref,
                cwx_ref, cbx_ref, cwb_ref, cbb_ref, arow_ref, acol_ref, brow_ref, bcol_ref, drow_ref, ng_ref,
                *rest, nc, zero_init):
    st0_ref = None if zero_init else rest[0]
    y_ref, st_ref, yf_s, h_s = rest[0 if zero_init else 1:]
    s = pl.program_id(1)
    bwd = s >= nc
    cidx = _chunk_index(s, nc)
    first = cidx == 0
    last = cidx == nc - 1
    nh = SSD_HEADS

    def initial(direction):
        return jnp.zeros(h_s.shape, F32) if zero_init else st0_ref[:, direction]

    @pl.when(s == 0)
    def _():
        h_s[...] = initial(0)

    @pl.when(s == nc)
    def _():
        h_s[...] = initial(1)

    G = z_ref.shape[0]
    tsel, tsel_t = _tri(bwd, G)
    mask = tsel > 0.0
    r0 = pl.multiple_of(cidx * CHUNK, CHUNK)
    hi = lax.Precision.HIGHEST

    xs = _conv_silu(xp_ref[...], xc_ref[...], xn_ref[...], cwx_ref, cbx_ref, first, last)
    bcv = _conv_silu(bp_ref[...], bc_ref[...], bn_ref[...], cwb_ref, cbb_ref, first, last)
    dt_all = _softplus(sm_ref[:, :, 64:96] + brow_ref[...])
    dtt_all = _softplus(dtt_ref[:, 0] + bcol_ref[...])
    a_all = dt_all * (-jnp.exp(arow_ref[...]))
    at_all = dtt_all * (-jnp.exp(acol_ref[...]))
    dt = jnp.where(bwd, dt_all[:, :, nh:2 * nh], dt_all[:, :, 0:nh])
    a = jnp.where(bwd, a_all[:, :, nh:2 * nh], a_all[:, :, 0:nh])
    at = jnp.where(bwd, at_all[:, nh:2 * nh, :], at_all[:, 0:nh, :])
    cum = _bdot(tsel, a, (2, 1), hi)
    cum_t = _bdot(at, tsel_t, (2, 1), hi)
    tot = jnp.where(bwd, cum[:, 0:1, :], cum[:, CHUNK - 1:CHUNK, :])

    ys = []
    for g in range(2):
        bg = bcv[:, :, g * SSD_STATE:(g + 1) * SSD_STATE]
        cg = bcv[:, :, 256 + g * SSD_STATE:256 + (g + 1) * SSD_STATE]
        cg16 = cg.astype(BF16)
        cb = _bdot(cg16, bg.astype(BF16), (2, 2))
        for hh in range(nh // 2):
            h = g * (nh // 2) + hh
            cq = cum[:, :, h:h + 1]
            decay = jnp.exp(jnp.where(mask, cq - cum_t[:, h:h + 1, :], NEG_INF))
            xd = xs[:, :, h * SSD_HEAD_DIM:(h + 1) * SSD_HEAD_DIM] * dt[:, :, h:h + 1]
            xd16 = xd.astype(BF16)
            hprev = h_s[:, h]
            y = (_bdot((cb * decay).astype(BF16), xd16, (2, 1))
                 + _bdot(cg16, hprev.astype(BF16), (2, 2)) * jnp.exp(cq))
            ys.append(y)
            th = tot[:, :, h:h + 1]
            bw = bg * jnp.exp(th - cq)
            h_s[:, h] = hprev * jnp.exp(th) + _bdot(xd16, bw.astype(BF16), (1, 1))
    y = jnp.concatenate(ys, axis=2)

    @pl.when(jnp.logical_not(bwd))
    def _():
        yf_s[:, pl.ds(r0, CHUNK), :] = y

    @pl.when(bwd)
    def _():
        yt = (yf_s[:, pl.ds(r0, CHUNK), :] + y + drow_ref[...] * xs) * _silu(z_ref[...])
        y_ref[...] = _rms(yt) * ng_ref[...]

    @pl.when(jnp.logical_or(s == nc - 1, s == 2 * nc - 1))
    def _():
        st_ref[:, 0] = h_s[...]


def _ssd(p0, dtt, cwx, cbx, cwb, cbb, a_log, dt_bias, d_skip, norm_g, st0, *, L):
    nseq, nc = p0.shape[0], L // CHUNK
    zero_init = st0 is None
    G = _scan_group(nseq, 8 if zero_init else 4)
    cur, prev, nxt, out = _scan_specs(nc)
    st_shape = (G, 2, SSD_HEADS, SSD_HEAD_DIM, SSD_STATE)
    states = [] if zero_init else [st0]
    row = lambda v: v.reshape(1, -1)
    col = lambda v: v.reshape(-1, 1)
    small = [cwx, row(cbx), cwb, row(cbb), row(a_log), col(a_log), row(dt_bias), col(dt_bias),
             row(jnp.repeat(d_skip, SSD_HEAD_DIM)), row(norm_g)]
    blk = lambda rows, width, index: pl.BlockSpec((G, rows, width), index)
    return pl.pallas_call(
        functools.partial(_ssd_kernel, nc=nc, zero_init=zero_init),
        grid=(nseq // G, 2 * nc),
        in_specs=[blk(CHUNK, 1024, cur(0)),
                  blk(8, 1024, prev(1)), blk(CHUNK, 1024, cur(1)), blk(8, 1024, nxt(1)),
                  blk(8, 512, prev(4)), blk(CHUNK, 512, cur(4)), blk(8, 512, nxt(4)),
                  blk(CHUNK, LANES, cur(28)),
                  pl.BlockSpec((G, 1, 32, CHUNK), lambda b, s: (b, _chunk_index(s, nc), 0, 0))]
        + [_full(v.shape) for v in small]
        + [pl.BlockSpec(st_shape, lambda b, s: (b, 0, 0, 0, 0)) for _ in states],
        out_specs=[blk(CHUNK, SSD_D_INNER, out(0)),
                   pl.BlockSpec((G, 1) + st_shape[2:], lambda b, s: (b, jnp.where(s < nc, 0, 1), 0, 0, 0))],
        out_shape=[jax.ShapeDtypeStruct((nseq, L, SSD_D_INNER), F32),
                   jax.ShapeDtypeStruct((nseq,) + st_shape[1:], F32)],
        scratch_shapes=[pltpu.VMEM((G, L, SSD_D_INNER), F32), pltpu.VMEM(st_shape[:1] + st_shape[2:], F32)],
        compiler_params=_cp(2), name="ssd_mixer",
    )(p0, p0, p0, p0, p0, p0, p0, p0, dtt, *small, *states)


def _rope_tables(L):
    t = jnp.arange(L)
    r = (t // GRID_W).astype(F32)
    c = (t % GRID_W).astype(F32)
    nf = 16
    inv = ROPE_BASE ** (-jnp.arange(nf, dtype=F32) / nf)
    ang = jnp.concatenate([r[:, None] * inv, c[:, None] * inv], axis=-1)
    ang = jnp.concatenate([ang, ang], axis=-1)
    return jnp.cos(ang), jnp.sin(ang)


def _rope128(x, cos, sin):
    lane = lax.broadcasted_iota(jnp.int32, x.shape, 1)
    rot = jnp.where(lane % 64 < 32, -pltpu.roll(x, 96, 1), pltpu.roll(x, 32, 1))
    return x * cos + rot * sin


def _pad_lanes(x):
    return jnp.concatenate([x, jnp.zeros_like(x)], axis=1)


MLA_QB = 256


def _mla_kernel(*refs, L, Lc, rope, emit):
    refs = list(refs)
    qa_ref, kva_ref, sm_ref, qg_ref, kvg_ref, wqn_ref, wqr_ref, wkk_ref, wkv_ref = refs[:9]
    pos = 9
    if Lc:
        cckv_ref, ckpe_ref = refs[pos:pos + 2]
        pos += 2
    if rope:
        cos_ref, sin_ref = refs[pos:pos + 2]
        pos += 2
    y_ref = refs[pos]
    pos += 1
    if emit:
        ckv_ref = refs[pos]
        pos += 1
    qn_s, qr_s, kn_s, v_s, kpe_s = refs[pos:]
    qb = pl.program_id(1)
    scale = np.float32((MLA_NOPE + MLA_ROPE) ** -0.5)

    @pl.when(qb == 0)
    def _():
        ckv = _rms(kva_ref[...]) * kvg_ref[...]
        if emit:
            ckv_ref[...] = ckv
        kpe = _pad_lanes(sm_ref[:, 0:MLA_ROPE])
        if rope:
            kpe = _rope128(kpe, cos_ref[...], sin_ref[...])
        if Lc:
            ckv = jnp.concatenate([cckv_ref[0], ckv], axis=0)
            kpe = jnp.concatenate([_pad_lanes(ckpe_ref[0]), kpe], axis=0)
        c16 = ckv.astype(BF16)
        kn_s[...] = _dot(c16, wkk_ref[...]).astype(BF16)
        v_s[...] = _dot(c16, wkv_ref[...]).astype(BF16)
        kpe_s[...] = kpe.astype(BF16)

    qn = (_rms(qa_ref[...]) * qg_ref[...]).astype(BF16)
    qn_s[...] = _dot(qn, wqn_ref[...]).astype(BF16)
    qr = _dot(qn, wqr_ref[...])
    if rope:
        q0 = pl.multiple_of(qb * MLA_QB, MLA_QB)
        cos = cos_ref[pl.ds(q0, MLA_QB), :]
        sin = sin_ref[pl.ds(q0, MLA_QB), :]
        for h in range(MLA_HEADS):
            qr_s[:, h * LANES:(h + 1) * LANES] = _rope128(qr[:, h * LANES:(h + 1) * LANES], cos, sin).astype(BF16)
    else:
        qr_s[...] = qr.astype(BF16)

    def head(h, _):
        c0 = h * LANES if isinstance(h, int) else pl.multiple_of(h * LANES, LANES)
        s = (_dot_nt(qn_s[:, pl.ds(c0, LANES)], kn_s[:, pl.ds(c0, LANES)])
             + _dot_nt(qr_s[:, pl.ds(c0, LANES)], kpe_s[...])) * scale
        p = jnp.exp(s - jnp.max(s, axis=-1, keepdims=True))
        p = p / jnp.sum(p, axis=-1, keepdims=True)
        y_ref[:, pl.ds(c0, LANES)] = _dot(p.astype(BF16), v_s[:, pl.ds(c0, LANES)])
        return 0

    if L + Lc <= 512:
        for h in range(MLA_HEADS):
            head(h, 0)
    else:
        lax.fori_loop(0, MLA_HEADS, head, 0)


def _mla(p0, qg, kvg, wqn, wqr, wkk, wkv, cache_ckv, cache_kpe, rope_tabs, *, L):
    n = p0.shape[0]
    nseq, nqb = n // L, L // MLA_QB
    Lc = 0 if cache_ckv is None else cache_ckv.shape[1]
    rope = rope_tabs is not None
    emit = cache_ckv is None
    Lk = L + Lc
    args = [p0, p0, p0, qg.reshape(1, -1), kvg.reshape(1, -1), wqn, wqr, wkk, wkv]
    specs = [pl.BlockSpec((MLA_QB, MLA_RANK), lambda b, q: (b * nqb + q, 5)),
             pl.BlockSpec((L, MLA_RANK), lambda b, q: (b, 6)),
             pl.BlockSpec((L, LANES), lambda b, q: (b, 28)),
             _full((1, MLA_RANK)), _full((1, MLA_RANK)), _full(wqn.shape), _full(wqr.shape), _full(wkk.shape),
             _full(wkv.shape)]
    if Lc:
        args += [cache_ckv, cache_kpe]
        specs += [pl.BlockSpec((1, Lc, MLA_RANK), lambda b, q: (b, 0, 0)),
                  pl.BlockSpec((1, Lc, MLA_ROPE), lambda b, q: (b, 0, 0))]
    if rope:
        args += list(rope_tabs)
        specs += [_full((L, LANES)), _full((L, LANES))]
    out_specs = [pl.BlockSpec((MLA_QB, 1024), lambda b, q: (b * nqb + q, 0))]
    out_shape = [jax.ShapeDtypeStruct((n, 1024), F32)]
    if emit:
        out_specs.append(pl.BlockSpec((L, MLA_RANK), lambda b, q: (b, 0)))
        out_shape.append(jax.ShapeDtypeStruct((n, MLA_RANK), F32))
    return pl.pallas_call(
        functools.partial(_mla_kernel, L=L, Lc=Lc, rope=rope, emit=emit),
        grid=(nseq, nqb), in_specs=specs, out_specs=out_specs, out_shape=out_shape,
        scratch_shapes=[pltpu.VMEM((MLA_QB, 1024), BF16), pltpu.VMEM((MLA_QB, 1024), BF16),
                        pltpu.VMEM((Lk, 1024), BF16), pltpu.VMEM((Lk, 1024), BF16), pltpu.VMEM((Lk, LANES), BF16)],
        compiler_params=_cp(2), name="mla_attention",
    )(*args)


def _gqa_ctx_kernel(q_ref, k_ref, v_ref, sink_ref, y_ref):
    q = q_ref[...]
    k = k_ref[...].astype(BF16)
    v = v_ref[...].astype(BF16)
    scale = np.float32(GQA_HEAD_DIM ** -0.5)
    group = GQA_HEADS // GQA_KV_HEADS
    outs = []
    for h in range(GQA_HEADS):
        kh = h // group
        ks = slice(kh * GQA_HEAD_DIM, (kh + 1) * GQA_HEAD_DIM)
        s = _dot_nt(q[:, h * GQA_HEAD_DIM:(h + 1) * GQA_HEAD_DIM].astype(BF16), k[:, ks]) * scale
        m = jnp.maximum(jnp.max(s, axis=-1, keepdims=True), sink_ref[h])
        p = jnp.exp(s - m)
        p = p / (jnp.sum(p, axis=-1, keepdims=True) + jnp.exp(sink_ref[h] - m))
        outs.append(_dot(p.astype(BF16), v[:, ks]))
    y_ref[...] = jnp.concatenate(outs, axis=1)


def _gqa_ctx(p1, sink, *, L):
    n = p1.shape[0]
    return pl.pallas_call(
        _gqa_ctx_kernel,
        grid=(n // L,),
        in_specs=[pl.BlockSpec((L, 1024), lambda b: (b, 2)), pl.BlockSpec((L, 256), lambda b: (b, 20)),
                  pl.BlockSpec((L, 256), lambda b: (b, 21)), pl.BlockSpec(memory_space=pltpu.SMEM)],
        out_specs=pl.BlockSpec((L, 1024), lambda b: (b, 0)),
        out_shape=jax.ShapeDtypeStruct((n, 1024), F32),
        compiler_params=_cp(1), name="gqa_context",
    )(p1, p1, p1, sink)


GQA_QB = 128
GQA_SPAN = GQA_QB + 2 * WINDOW


def _gqa_win_kernel(q_ref, k_ref, v_ref, ck_ref, cv_ref, cos_ref, sin_ref, sink_ref, y_ref, kp_s, vp_s, *, L):
    qb = pl.program_id(1)
    scale = np.float32(GQA_HEAD_DIM ** -0.5)
    group = GQA_HEADS // GQA_KV_HEADS

    @pl.when(qb == 0)
    def _():
        zeros = jnp.zeros((WINDOW, 256), BF16)
        kp_s[0:WINDOW, :] = zeros
        kp_s[WINDOW + L:, :] = zeros
        vp_s[0:WINDOW, :] = zeros
        vp_s[WINDOW + L:, :] = zeros
        for j in range(2):
            cs = slice(j * LANES, (j + 1) * LANES)
            kp_s[WINDOW:WINDOW + L, cs] = _rope128(k_ref[:, cs], cos_ref[...], sin_ref[...]).astype(BF16)
        vp_s[WINDOW:WINDOW + L, :] = v_ref[...].astype(BF16)

    q0 = pl.multiple_of(qb * GQA_QB, GQA_QB)
    cos = cos_ref[pl.ds(q0, GQA_QB), :]
    sin = sin_ref[pl.ds(q0, GQA_QB), :]
    kw = kp_s[pl.ds(q0, GQA_SPAN), :]
    vw = vp_s[pl.ds(q0, GQA_SPAN), :]
    kc = ck_ref[0].astype(BF16)
    vc = cv_ref[0].astype(BF16)
    qpos = q0 + lax.broadcasted_iota(jnp.int32, (GQA_QB, GQA_SPAN), 0)
    kpos = q0 - WINDOW + lax.broadcasted_iota(jnp.int32, (GQA_QB, GQA_SPAN), 1)
    bias = jnp.where(kpos < 0, NEG_INF, jnp.where(kpos >= L, NEG_INF, jnp.where(jnp.abs(qpos - kpos) <= WINDOW, 0.0, NEG_INF)))
    outs = []
    for j in range(GQA_HEADS * GQA_HEAD_DIM // LANES):
        qj = _rope128(q_ref[:, j * LANES:(j + 1) * LANES], cos, sin).astype(BF16)
        for half in range(2):
            h = 2 * j + half
            kh = h // group
            ks = slice(kh * GQA_HEAD_DIM, (kh + 1) * GQA_HEAD_DIM)
            qh = qj[:, half * GQA_HEAD_DIM:(half + 1) * GQA_HEAD_DIM]
            s_loc = _dot_nt(qh, kw[:, ks]) * scale + bias
            s_ctx = _dot_nt(qh, kc[:, ks]) * scale
            m = jnp.maximum(jnp.maximum(jnp.max(s_loc, axis=-1, keepdims=True), jnp.max(s_ctx, axis=-1, keepdims=True)),
                            sink_ref[h])
            p_loc = jnp.exp(s_loc - m)
            p_ctx = jnp.exp(s_ctx - m)
            inv = 1.0 / (jnp.sum(p_loc, axis=-1, keepdims=True) + jnp.sum(p_ctx, axis=-1, keepdims=True)
                         + jnp.exp(sink_ref[h] - m))
            outs.append(_dot((p_loc * inv).astype(BF16), vw[:, ks]) + _dot((p_ctx * inv).astype(BF16), vc[:, ks]))
    y_ref[...] = jnp.concatenate(outs, axis=1)


def _gqa_win(p1, cache_k, cache_v, cos, sin, sink, *, L):
    n = p1.shape[0]
    nseq, nqb = n // L, L // GQA_QB
    Lc = cache_k.shape[1]
    return pl.pallas_call(
        functools.partial(_gqa_win_kernel, L=L),
        grid=(nseq, nqb),
        in_specs=[pl.BlockSpec((GQA_QB, 1024), lambda b, q: (b * nqb + q, 2)),
                  pl.BlockSpec((L, 256), lambda b, q: (b, 20)), pl.BlockSpec((L, 256), lambda b, q: (b, 21)),
                  pl.BlockSpec((1, Lc, 256), lambda b, q: (b, 0, 0)), pl.BlockSpec((1, Lc, 256), lambda b, q: (b, 0, 0)),
                  _full((L, LANES)), _full((L, LANES)), pl.BlockSpec(memory_space=pltpu.SMEM)],
        out_specs=pl.BlockSpec((GQA_QB, 1024), lambda b, q: (b * nqb + q, 0)),
        out_shape=jax.ShapeDtypeStruct((n, 1024), F32),
        scratch_shapes=[pltpu.VMEM((L + 2 * WINDOW, 256), BF16), pltpu.VMEM((L + 2 * WINDOW, 256), BF16)],
        compiler_params=_cp(2), name="gqa_window",
    )(p1, p1, p1, cache_k, cache_v, cos, sin, sink)


def _mlstm_kernel(qp_ref, qc_ref, qn_ref, v_ref, o_ref, sm_ref, gt_ref, cw_ref, cb_ref, brow_ref, bcol_ref, ng_ref,
                  *rest, nc, zero_init):
    init_refs = () if zero_init else rest[:3]
    y_ref, cst_ref, nst_ref, mst_ref, hf_s, c_s, n_s, m_s = rest[len(init_refs):]
    s = pl.program_id(1)
    bwd = s >= nc
    cidx = _chunk_index(s, nc)
    nh = ML_HEADS
    hd = ML_HEAD_DIM

    def load_initial(direction):
        for k, dst in enumerate((c_s, n_s, m_s)):
            dst[...] = jnp.zeros(dst.shape, F32) if zero_init else init_refs[k][:, direction]

    @pl.when(s == 0)
    def _():
        load_initial(0)

    @pl.when(s == nc)
    def _():
        cst_ref[:, 0] = c_s[...]
        nst_ref[:, 0] = n_s[...]
        mst_ref[:, 0] = m_s[...]
        load_initial(1)

    G = qc_ref.shape[0]
    tsel, tsel_t = _tri(bwd, G)
    mask = tsel > 0.0
    kscale = np.float32(hd ** -0.5)
    r0 = pl.multiple_of(cidx * CHUNK, CHUNK)
    hi = lax.Precision.HIGHEST

    qk = _conv_silu(qp_ref[...], qc_ref[...], qn_ref[...], cw_ref, cb_ref, cidx == 0, cidx == nc - 1)
    gates = sm_ref[:, :, 0:4 * nh] + brow_ref[...]
    gates_t = gt_ref[:, 0] + bcol_ref[...]
    li = jnp.where(bwd, gates[:, :, nh:2 * nh], gates[:, :, 0:nh])
    li_t = jnp.where(bwd, gates_t[:, nh:2 * nh, :], gates_t[:, 0:nh, :])
    lf = -_softplus(-jnp.where(bwd, gates[:, :, 3 * nh:4 * nh], gates[:, :, 2 * nh:3 * nh]))
    lf_t = -_softplus(-jnp.where(bwd, gates_t[:, 3 * nh:4 * nh, :], gates_t[:, 2 * nh:3 * nh, :]))
    bc = _bdot(tsel, lf, (2, 1), hi)
    bc_t = _bdot(lf_t, tsel_t, (2, 1), hi)
    tot = jnp.where(bwd, bc[:, 0:1, :], bc[:, CHUNK - 1:CHUNK, :])

    hs = []
    for h in range(nh):
        cs = slice(h * hd, (h + 1) * hd)
        q = qk[:, :, cs]
        k = qk[:, :, ML_D + h * hd:ML_D + (h + 1) * hd] * kscale
        q16, k16 = q.astype(BF16), k.astype(BF16)
        v = v_ref[:, :, cs]
        bq = bc[:, :, h:h + 1]
        m_prev = m_s[:, h:h + 1, 0:1]
        log_d = jnp.where(mask, bq - bc_t[:, h:h + 1, :] + li_t[:, h:h + 1, :], NEG_INF)
        log_inter = bq + m_prev
        m_out = jnp.maximum(log_inter, jnp.max(log_d, axis=2, keepdims=True))
        sd = _bdot(q16, k16, (2, 2)) * jnp.exp(log_d - m_out)
        w_inter = jnp.exp(log_inter - m_out)
        c_prev = c_s[:, h]
        n_prev = n_s[:, h:h + 1, :]
        num = _bdot(sd.astype(BF16), v.astype(BF16), (2, 1)) + w_inter * _bdot(q16, c_prev.astype(BF16), (2, 2))
        den = jnp.sum(sd, axis=2, keepdims=True) + w_inter * jnp.sum(q * n_prev, axis=2, keepdims=True)
        hs.append(num / jnp.maximum(jnp.abs(den), jnp.exp(-m_out)))
        th = tot[:, :, h:h + 1]
        end_inter = th + m_prev
        end_intra = th - bq + li[:, :, h:h + 1]
        m_new = jnp.maximum(end_inter, jnp.max(end_intra, axis=1, keepdims=True))
        w_c = jnp.exp(end_inter - m_new)
        w_k = jnp.exp(end_intra - m_new)
        c_s[:, h] = w_c * c_prev + _bdot((v * w_k).astype(BF16), k16, (1, 1))
        n_s[:, h:h + 1, :] = w_c * n_prev + jnp.sum(k * w_k, axis=1, keepdims=True)
        m_s[:, h:h + 1, :] = jnp.broadcast_to(m_new, (G, 1, hd))
    hcat = jnp.concatenate(hs, axis=2)

    @pl.when(jnp.logical_not(bwd))
    def _():
        hf_s[:, pl.ds(r0, CHUNK), :] = hcat

    @pl.when(bwd)
    def _():
        tot_h = hf_s[:, pl.ds(r0, CHUNK), :] + hcat
        normed = jnp.concatenate([_rms(tot_h[:, :, h * hd:(h + 1) * hd]) for h in range(nh)], axis=2) * ng_ref[...]
        y_ref[...] = normed * jax.nn.sigmoid(o_ref[...])

    @pl.when(s == 2 * nc - 1)
    def _():
        cst_ref[:, 1] = c_s[...]
        nst_ref[:, 1] = n_s[...]
        mst_ref[:, 1] = m_s[...]


def _mlstm(p1, gt, conv_w, conv_b, ig_b, fg_b, norm_g, c0, n0, m0, *, L):
    nseq, nc = p1.shape[0], L // CHUNK
    zero_init = c0 is None
    G = _scan_group(nseq, 8 if zero_init else 4)
    cur, prev, nxt, out = _scan_specs(nc)
    cshape = (G, 2, ML_HEADS, ML_HEAD_DIM, ML_HEAD_DIM)
    nshape = (G, 2, ML_HEADS, ML_HEAD_DIM)
    states = [] if zero_init else [c0, n0, m0]
    gb = jnp.concatenate([ig_b.reshape(-1), fg_b.reshape(-1)])
    small = [conv_w, conv_b.reshape(1, -1), gb.reshape(1, -1), gb.reshape(-1, 1), norm_g.reshape(1, -1)]
    st = lambda shape: pl.BlockSpec(shape, lambda b, s: (b,) + (0,) * (len(shape) - 1))
    blk = lambda rows, width, index: pl.BlockSpec((G, rows, width), index)
    return pl.pallas_call(
        functools.partial(_mlstm_kernel, nc=nc, zero_init=zero_init),
        grid=(nseq // G, 2 * nc),
        in_specs=[blk(8, 2048, prev(0)), blk(CHUNK, 2048, cur(0)), blk(8, 2048, nxt(0)),
                  blk(CHUNK, 1024, cur(3)), blk(CHUNK, 1024, cur(4)), blk(CHUNK, LANES, cur(44)),
                  pl.BlockSpec((G, 1, 32, CHUNK), lambda b, s: (b, _chunk_index(s, nc), 0, 0))]
        + [_full(v.shape) for v in small] + ([] if zero_init else [st(cshape), st(nshape), st(nshape)]),
        out_specs=[blk(CHUNK, ML_D, out(0)), st(cshape), st(nshape), st(nshape)],
        out_shape=[jax.ShapeDtypeStruct((nseq, L, ML_D), F32), jax.ShapeDtypeStruct((nseq,) + cshape[1:], F32),
                   jax.ShapeDtypeStruct((nseq,) + nshape[1:], F32), jax.ShapeDtypeStruct((nseq,) + nshape[1:], F32)],
        scratch_shapes=[pltpu.VMEM((G, L, ML_D), F32), pltpu.VMEM(cshape[:1] + cshape[2:], F32),
                        pltpu.VMEM(nshape[:1] + nshape[2:], F32), pltpu.VMEM(nshape[:1] + nshape[2:], F32)],
        compiler_params=_cp(2), name="mlstm_mixer",
    )(p1, p1, p1, p1, p1, p1, gt, *small, *states)


def _prep_layer0(w_in, w_qb, w_kvb, w_out):
    pad = jnp.zeros((D_MODEL, L0_COLS - 3680), F32)
    w = jnp.concatenate([w_in[:, :2560], w_in[:, 2592:3680], w_in[:, 2560:2592], pad], axis=1).astype(BF16)
    q3 = w_qb.reshape(MLA_RANK, MLA_HEADS, MLA_NOPE + MLA_ROPE)
    wqn = q3[:, :, :MLA_NOPE].reshape(MLA_RANK, -1).astype(BF16)
    wqr = jnp.concatenate([q3[:, :, MLA_NOPE:], jnp.zeros((MLA_RANK, MLA_HEADS, LANES - MLA_ROPE), F32)], axis=-1)
    wqr = wqr.reshape(MLA_RANK, -1).astype(BF16)
    k3 = w_kvb.reshape(MLA_RANK, MLA_HEADS, MLA_NOPE + MLA_V)
    wkk = k3[:, :, :MLA_NOPE].reshape(MLA_RANK, -1).astype(BF16)
    wkv = k3[:, :, MLA_NOPE:].reshape(MLA_RANK, -1).astype(BF16)
    return w, wqn, wqr, wkk, wkv, w_out.astype(BF16)


def _prep_layer1(w_in, w_out):
    pad = jnp.zeros((D_MODEL, L1_COLS - 5664), F32)
    w = jnp.concatenate([w_in[:, 1536:3584], w_in[:, 0:1024], w_in[:, 3584:5632], w_in[:, 1024:1536],
                         w_in[:, 5632:5664], pad], axis=1).astype(BF16)
    return w, w_out.astype(BF16)


def _prep_peer(wq, keys, u, v):
    return (wq.T.astype(BF16), keys.reshape(PEER_HEADS * 2, PEER_NKEYS, -1).astype(BF16), u.astype(BF16),
            v.T.astype(BF16))


def _chunk_transposed(cols, L):
    return cols.reshape(-1, L // CHUNK, CHUNK, cols.shape[1]).transpose(0, 1, 3, 2)


def _trunk(x, L, mod0, mod1, w0, w1, peer0, peer1, p, ctx, rope_tabs, mod_base, rows_per_cond, fg):
    kw = dict(mod_base=mod_base, rows_per_cond=rows_per_cond)
    w_in0, wqn, wqr, wkk, wkv, w_out0 = w0
    w_in1, w_out1 = w1
    nseq = x.shape[0] // L

    p0 = _proj(x, p['l0_norm1_g'], mod0, w_in0, tn=1280, **kw)
    cw, cb = p['l0_ssd_conv_w'], p['l0_ssd_conv_b']
    y_ssd, ssd_state = _ssd(p0.reshape(nseq, L, -1), _chunk_transposed(p0[:, 3648:3680], L), cw[:, :1024], cb[:1024],
                            cw[:, 1024:], cb[1024:], p['l0_ssd_A_log'], p['l0_ssd_dt_bias'], p['l0_ssd_D'],
                            p['l0_ssd_norm_g'], None if ctx is None else ctx[0], L=L)
    y_ssd = y_ssd.reshape(nseq * L, -1)
    mla_tabs = None
    if rope_tabs is not None:
        z64 = jnp.zeros_like(rope_tabs[0])
        mla_tabs = (jnp.concatenate([rope_tabs[0], z64], axis=1), jnp.concatenate([rope_tabs[1], z64], axis=1))
    mla_out = _mla(p0, p['l0_mla_q_norm_g'], p['l0_mla_kv_norm_g'], wqn, wqr, wkk, wkv,
                   None if ctx is None else ctx[1], None if ctx is None else ctx[2], mla_tabs, L=L)
    x = _outproj(y_ssd, mla_out[0], w_out0, x, mod0, **kw)
    x = _peer(x, p['l0_norm2_g'], mod0, *peer0, fg, final_norm=False, **kw)

    p1 = _proj(x, p['l1_norm1_g'], mod1, w_in1, tn=1152, **kw)
    if ctx is None:
        y_gqa = _gqa_ctx(p1, p['l1_gqa_sink'], L=L)
        c0 = n0 = m0 = None
    else:
        gqa_tabs = tuple(jnp.concatenate([t, t], axis=1) for t in rope_tabs)
        y_gqa = _gqa_win(p1, ctx[3].reshape(nseq, -1, 256), ctx[4].reshape(nseq, -1, 256), *gqa_tabs,
                         p['l1_gqa_sink'], L=L)
        c0, n0 = ctx[5], ctx[6]
        m0 = jnp.broadcast_to(ctx[7][..., None], ctx[7].shape + (ML_HEAD_DIM,))
    y_ml, mc, mn, mm = _mlstm(p1.reshape(nseq, L, -1), _chunk_transposed(p1[:, 5632:5664], L), p['l1_ml_conv_w'],
                              p['l1_ml_conv_b'], p['l1_ml_ig_b'], p['l1_ml_fg_b'], p['l1_ml_norm_g'], c0, n0, m0, L=L)
    y_ml = y_ml.reshape(nseq * L, -1)
    x = _outproj(y_gqa, y_ml, w_out1, x, mod1, **kw)
    y = _peer(x, p['l1_norm2_g'], mod1, *peer1, fg, final_norm=True, **kw)
    new = None
    if ctx is None:
        new = (ssd_state, mla_out[1].reshape(nseq, L, MLA_RANK), p0[:, 3584:3648].reshape(nseq, L, MLA_ROPE),
               p1[:, 5120:5376].reshape(nseq, L, GQA_KV_HEADS, GQA_HEAD_DIM),
               p1[:, 5376:5632].reshape(nseq, L, GQA_KV_HEADS, GQA_HEAD_DIM), mc, mn, mm[..., 0])
    return y, new


def kernel(x_prompt, x_sample, state_l0_ssd, cache_l0_mla_ckv, cache_l0_mla_kpe, cache_l1_gqa_k, cache_l1_gqa_v, state_l1_mlstm_C, state_l1_mlstm_n, state_l1_mlstm_m, c, c_ctx, final_norm_g, l0_ada_w, l0_ada_b, l0_norm1_g, l0_norm2_g, l0_w_in, l0_ssd_conv_w, l0_ssd_conv_b, l0_ssd_A_log, l0_ssd_dt_bias, l0_ssd_D, l0_ssd_norm_g, l0_mla_q_norm_g, l0_mla_w_qb, l0_mla_kv_norm_g, l0_mla_w_kvb, l0_w_out, l0_peer_wq, l0_peer_keys, l0_peer_u, l0_peer_v, l1_ada_w, l1_ada_b, l1_norm1_g, l1_norm2_g, l1_w_in, l1_gqa_sink, l1_ml_conv_w, l1_ml_conv_b, l1_ml_ig_b, l1_ml_fg_b, l1_ml_norm_g, l1_w_out, l1_peer_wq, l1_peer_keys, l1_peer_u, l1_peer_v):
    p = dict(l0_norm1_g=l0_norm1_g, l0_norm2_g=l0_norm2_g, l0_ssd_conv_w=l0_ssd_conv_w, l0_ssd_conv_b=l0_ssd_conv_b,
             l0_ssd_A_log=l0_ssd_A_log, l0_ssd_dt_bias=l0_ssd_dt_bias, l0_ssd_D=l0_ssd_D, l0_ssd_norm_g=l0_ssd_norm_g,
             l0_mla_q_norm_g=l0_mla_q_norm_g, l0_mla_kv_norm_g=l0_mla_kv_norm_g, l1_norm1_g=l1_norm1_g,
             l1_norm2_g=l1_norm2_g, l1_gqa_sink=l1_gqa_sink, l1_ml_conv_w=l1_ml_conv_w, l1_ml_conv_b=l1_ml_conv_b,
             l1_ml_ig_b=l1_ml_ig_b, l1_ml_fg_b=l1_ml_fg_b, l1_ml_norm_g=l1_ml_norm_g)
    nb, seq = x_prompt.shape[:2]
    db, dseq = x_sample.shape[:2]
    assert db <= 7 and seq % MLA_QB == 0 and dseq % MLA_QB == 0

    cond8 = jnp.zeros((8, D_MODEL), F32).at[0].set(c_ctx).at[1:1 + db].set(c)
    mod0 = _ada(cond8, l0_ada_w, l0_ada_b).reshape(8, 1, 6 * D_MODEL)
    mod1 = _ada(cond8, l1_ada_w, l1_ada_b).reshape(8, 1, 6 * D_MODEL)
    w0 = _prep_layer0(l0_w_in, l0_mla_w_qb, l0_mla_w_kvb, l0_w_out)
    w1 = _prep_layer1(l1_w_in, l1_w_out)
    peer0 = _prep_peer(l0_peer_wq, l0_peer_keys, l0_peer_u, l0_peer_v)
    peer1 = _prep_peer(l1_peer_wq, l1_peer_keys, l1_peer_u, l1_peer_v)

    y_prompt, new = _trunk(x_prompt.reshape(nb * seq, D_MODEL), seq, mod0, mod1, w0, w1, peer0, peer1, p, None, None,
                           0, nb * seq, final_norm_g)
    ctx = (state_l0_ssd, cache_l0_mla_ckv, cache_l0_mla_kpe, cache_l1_gqa_k, cache_l1_gqa_v, state_l1_mlstm_C,
           state_l1_mlstm_n, state_l1_mlstm_m)
    y_sample, _ = _trunk(x_sample.reshape(db * dseq, D_MODEL), dseq, mod0, mod1, w0, w1, peer0, peer1, p, ctx,
                         _rope_tables(dseq), 1, dseq, final_norm_g)
    return (y_prompt.reshape(nb, seq, D_MODEL), y_sample.reshape(db, dseq, D_MODEL)) + new
```

```python
import functools

import jax
import jax.numpy as jnp
import numpy as np
from jax import lax
from jax.experimental import pallas as pl
from jax.experimental.pallas import tpu as pltpu

F32 = jnp.float32
BF16 = jnp.bfloat16
NEG_INF = float("-inf")

D_MODEL = 2048
EPS = 1e-6
ROPE_BASE = 10000.0
GRID_W = 64
CONV_W = 5
CHUNK = 64
SSD_HEADS = 16
SSD_HEAD_DIM = 64
SSD_D_INNER = 1024
SSD_STATE = 128
MLA_HEADS = 8
MLA_NOPE = 128
MLA_ROPE = 64
MLA_V = 128
MLA_RANK = 512
GQA_HEADS = 16
GQA_KV_HEADS = 4
GQA_HEAD_DIM = 64
WINDOW = 128
ML_HEADS = 8
ML_HEAD_DIM = 128
ML_D = 1024
PEER_HEADS = 8
PEER_NKEYS = 128
PEER_N = PEER_NKEYS * PEER_NKEYS
PEER_TOPK = 16

LANES = 128
SUBLANES = 8
VMEM_LIMIT = 56 * 2**20

L0_COLS = 3840
L1_COLS = 5760


def _cp(n_grid, flags=None):
    return pltpu.CompilerParams(dimension_semantics=("arbitrary",) * n_grid, vmem_limit_bytes=VMEM_LIMIT, flags=flags)


def _silu(x):
    return x * jax.nn.sigmoid(x)


def _softplus(x):
    return jnp.maximum(x, 0.0) + jnp.log1p(jnp.exp(-jnp.abs(x)))


def _rms(x):
    return x * lax.rsqrt(jnp.mean(x * x, axis=-1, keepdims=True) + EPS)


def _dot(a, b):
    return jnp.dot(a, b, preferred_element_type=F32)


def _dot_nt(a, b):
    return lax.dot_general(a, b, (((1,), (1,)), ((), ())), preferred_element_type=F32)


def _full(shape):
    nd = len(shape)
    return pl.BlockSpec(shape, lambda *_: (0,) * nd)


def _mod_spec(chunk, tb, mod_base, rows_per_cond):
    def index(i, *_):
        return (mod_base + (i * tb) // rows_per_cond, 0, chunk)
    return pl.BlockSpec((1, 1, D_MODEL), index)


def _ada_kernel(c_ref, w_ref, b_ref, o_ref):
    s = _silu(c_ref[...])
    o_ref[...] = _dot(s.astype(BF16), w_ref[...].astype(BF16)) + b_ref[...]


def _ada(cond8, w, b):
    n = w.shape[1]
    tn = 1024
    return pl.pallas_call(
        _ada_kernel,
        grid=(n // tn,),
        in_specs=[_full((8, D_MODEL)), pl.BlockSpec((D_MODEL, tn), lambda j: (0, j)),
                  pl.BlockSpec((1, tn), lambda j: (0, j))],
        out_specs=pl.BlockSpec((8, tn), lambda j: (0, j)),
        out_shape=jax.ShapeDtypeStruct((8, n), F32),
        compiler_params=_cp(1), name="ada_table",
    )(cond8, w, b.reshape(1, n))


def _proj_kernel(x_ref, g_ref, sc_ref, sh_ref, w_ref, o_ref, h_ref):
    @pl.when(pl.program_id(1) == 0)
    def _():
        h = _rms(x_ref[...]) * g_ref[...] * (1.0 + sc_ref[0]) + sh_ref[0]
        h_ref[...] = h.astype(BF16)
    o_ref[...] = _dot(h_ref[...], w_ref[...])


def _proj(x, g, mod, w, *, tn, mod_base, rows_per_cond):
    n, ncol = x.shape[0], w.shape[1]
    tb = 1024
    return pl.pallas_call(
        _proj_kernel,
        grid=(n // tb, ncol // tn),
        in_specs=[pl.BlockSpec((tb, D_MODEL), lambda i, j: (i, 0)), _full((1, D_MODEL)),
                  _mod_spec(1, tb, mod_base, rows_per_cond), _mod_spec(0, tb, mod_base, rows_per_cond),
                  pl.BlockSpec((D_MODEL, tn), lambda i, j: (0, j))],
        out_specs=pl.BlockSpec((tb, tn), lambda i, j: (i, j)),
        out_shape=jax.ShapeDtypeStruct((n, ncol), F32),
        scratch_shapes=[pltpu.VMEM((tb, D_MODEL), BF16)],
        compiler_params=_cp(2), name="norm_mod_proj",
    )(x, g.reshape(1, D_MODEL), mod, mod, w)


def _outproj_kernel(ya_ref, yb_ref, w_ref, x_ref, g1_ref, o_ref):
    half = ya_ref.shape[1]
    y = _dot(ya_ref[...].astype(BF16), w_ref[0:half, :]) + _dot(yb_ref[...].astype(BF16), w_ref[half:, :])
    o_ref[...] = x_ref[...] + g1_ref[0] * y


def _outproj(ya, yb, w, x, mod, *, mod_base, rows_per_cond):
    n = x.shape[0]
    tb = 512
    return pl.pallas_call(
        _outproj_kernel,
        grid=(n // tb,),
        in_specs=[pl.BlockSpec((tb, ya.shape[1]), lambda i: (i, 0)), pl.BlockSpec((tb, yb.shape[1]), lambda i: (i, 0)),
                  _full(w.shape), pl.BlockSpec((tb, D_MODEL), lambda i: (i, 0)),
                  _mod_spec(2, tb, mod_base, rows_per_cond)],
        out_specs=pl.BlockSpec((tb, D_MODEL), lambda i: (i, 0)),
        out_shape=jax.ShapeDtypeStruct((n, D_MODEL), F32),
        compiler_params=_cp(1), name="out_proj_residual",
    )(ya, yb, w, x, mod)


ROUTE_TB = 512
_CAND_KEEP = (16, 8, 5, 4, 3, 2, 2, 2)


def _sort_pairs(n):
    pairs = []

    def merge(lo, hi, r):
        step = r * 2
        if step < hi - lo:
            merge(lo, hi, step)
            merge(lo + r, hi, step)
            pairs.extend((i, i + r) for i in range(lo + r, hi - r, step))
        else:
            pairs.append((lo, lo + r))

    def sort(lo, hi):
        if hi - lo >= 1:
            mid = lo + (hi - lo) // 2
            sort(lo, mid)
            sort(mid + 1, hi)
            merge(lo, hi, 1)

    sort(0, 15)
    return [(i, j) for i, j in pairs if j < n]


def _top_values(x, count):
    groups = [x[r:r + SUBLANES] for r in range(0, x.shape[0], SUBLANES)]
    n = len(groups)
    for i, j in _sort_pairs(n):
        groups[i], groups[j] = jnp.maximum(groups[i], groups[j]), jnp.minimum(groups[i], groups[j])
    sub = lax.broadcasted_iota(jnp.int32, groups[0].shape, 0)
    tops = []
    for t in range(count):
        head = groups[0]
        m = jnp.max(head, axis=0, keepdims=True)
        tops.append(m)
        if t == count - 1:
            break
        first = jnp.min(jnp.where(head == m, sub, SUBLANES), axis=0, keepdims=True)
        popped = sub == first
        depth = min(count - 1 - t, n)
        for r in range(depth):
            below = groups[r + 1] if r + 1 < n else NEG_INF
            groups[r] = jnp.where(popped, below, groups[r])
    return tops


def _route_kernel(x_ref, g_ref, sc_ref, sh_ref, wqt_ref, keys_ref,
                  hb_ref, th_ref, s2_ref, e2_ref, c1_ref, qt_s, top_s):
    tb = x_ref.shape[0]
    h = _rms(x_ref[...]) * g_ref[...] * (1.0 + sc_ref[0]) + sh_ref[0]
    hb = h.astype(BF16)
    hb_ref[...] = hb
    qt_s[...] = _dot_nt(wqt_ref[...], hb).astype(BF16)

    iota_8 = lax.broadcasted_iota(jnp.int32, (SUBLANES, tb), 0)

    def head(hd, _):
        scores = []
        for c in range(2):
            r0 = pl.multiple_of((hd * 2 + c) * PEER_NKEYS, PEER_NKEYS)
            s = _dot(keys_ref[hd * 2 + c], qt_s[pl.ds(r0, PEER_NKEYS), :])
            scores.append(s)
            for it, m in enumerate(_top_values(s, PEER_TOPK)):
                top_s[c, it:it + 1, :] = m
        a1 = top_s[0]
        a2 = top_s[1]
        pieces = [a1[0:1] + a2, a1[1:2] + a2[0:8]]
        for i in range(2, 8):
            pieces.append(jnp.where(iota_8 < _CAND_KEEP[i], a1[i:i + 1] + a2[0:8], NEG_INF))
        pieces.append(a1[8:16] + a2[0:1])
        cand = jnp.concatenate(pieces, axis=0)
        tau = _top_values(cand, PEER_TOPK)[-1]
        top = a1[0:1] + a2[0:1]
        z = jnp.sum(jnp.where(cand >= tau, jnp.exp(cand - top), 0.0), axis=0, keepdims=True)
        e2 = jnp.exp(scores[1] - a2[0:1])
        c1 = jnp.exp(scores[0] - a1[0:1]) / (2.0 * z)
        theta = jnp.full(scores[0].shape, jnp.inf, F32)
        for q in range(PEER_TOPK):
            a2q = a2[q:q + 1]
            phi = jnp.min(jnp.where(a1 + a2q >= tau, a1, jnp.inf), axis=0, keepdims=True)
            theta = jnp.where(scores[0] >= phi, a2q, theta)
        for t in range(tb // LANES):
            ts = slice(t * LANES, (t + 1) * LANES)
            th_ref[hd, t] = theta[:, ts]
            s2_ref[hd, t] = scores[1][:, ts]
            e2_ref[hd, t] = e2[:, ts]
            c1_ref[hd, t] = c1[:, ts]
        return 0

    lax.fori_loop(0, PEER_HEADS, head, 0)


def _route(x, g, mod, wqt, keys, *, mod_base, rows_per_cond):
    n = x.shape[0]
    tb = ROUTE_TB
    ntg = tb // LANES
    hk = (PEER_HEADS, n // LANES, PEER_NKEYS, LANES)
    rspec = pl.BlockSpec((PEER_HEADS, ntg, PEER_NKEYS, LANES), lambda i: (0, i, 0, 0))
    return pl.pallas_call(
        _route_kernel,
        grid=(n // tb,),
        in_specs=[pl.BlockSpec((tb, D_MODEL), lambda i: (i, 0)), _full((1, D_MODEL)),
                  _mod_spec(4, tb, mod_base, rows_per_cond), _mod_spec(3, tb, mod_base, rows_per_cond),
                  _full(wqt.shape), _full(keys.shape)],
        out_specs=[pl.BlockSpec((tb, D_MODEL), lambda i: (i, 0)), rspec, rspec, rspec, rspec],
        out_shape=[jax.ShapeDtypeStruct((n, D_MODEL), BF16)] + [jax.ShapeDtypeStruct(hk, F32)] * 4,
        scratch_shapes=[pltpu.VMEM((PEER_HEADS * 2 * PEER_NKEYS, tb), BF16), pltpu.VMEM((2, PEER_TOPK, tb), F32)],
        compiler_params=_cp(1), name="peer_route",
    )(x, g.reshape(1, D_MODEL), mod, mod, wqt, keys)


PEER_TB = 512
PEER_EC = 1024
PEER_SUB = PEER_EC // 2


def _experts_kernel(hb_ref, th_ref, s2_ref, e2_ref, c1_ref, u0_ref, ub_ref, un_ref, vp_ref, va_ref, vl_ref,
                    x_ref, g2_ref, fg_ref, o_ref, acc_s, st0_s, st1_s, a0_s, a1_s, *, final_norm):
    e = pl.program_id(1)
    tb = hb_ref.shape[0]
    ntg = tb // LANES
    rows_per_sub = PEER_SUB // PEER_NKEYS

    def scores(u_blk, st_s, h):
        res = _dot_nt(u_blk[...], hb_ref[h * 2 * LANES:(h + 1) * 2 * LANES, :])
        st_s[2 * h] = res[:, :LANES]
        st_s[2 * h + 1] = res[:, LANES:]

    def values(vt_blk, act_s, h):
        act = jnp.concatenate([act_s[2 * h], act_s[2 * h + 1]], axis=1)
        hs = slice(h * 2 * LANES, (h + 1) * 2 * LANES)
        for r in range(0, D_MODEL, PEER_SUB):
            acc_s[r:r + PEER_SUB, hs] += _dot(vt_blk[r:r + PEER_SUB, :], act)

    def gates(c, tg, st_s, act_s):
        sub = 32
        for k in range(rows_per_sub):
            row = c * rows_per_sub + k
            for j0 in range(0, PEER_NKEYS, sub):
                js = slice(j0, j0 + sub)
                ks = slice(k * PEER_NKEYS + j0, k * PEER_NKEYS + j0 + sub)
                w = None
                for hd in range(PEER_HEADS):
                    chosen = s2_ref[hd, tg, js, :] >= th_ref[hd, tg, row:row + 1, :]
                    term = jnp.where(chosen, e2_ref[hd, tg, js, :], 0.0) * c1_ref[hd, tg, row:row + 1, :]
                    w = term if w is None else w + term
                s = st_s[tg, ks, :]
                act = s * (1.0 + lax.erf(s * np.float32(2.0 ** -0.5)))
                act_s[tg, ks, :] = (act * w).astype(BF16)

    @pl.when(e == 0)
    def _():
        acc_s[...] = jnp.zeros_like(acc_s)
        a1_s[...] = jnp.zeros_like(a1_s)
        scores(u0_ref, st0_s, 0)
        scores(u0_ref, st0_s, 1)

    def phase(c, st_cur, act_cur, u_next, st_next, vt_prev, act_prev):
        for h in range(2):
            scores(u_next, st_next, h)
            gates(c, 2 * h, st_cur, act_cur)
            values(vt_prev, act_prev, h)
            gates(c, 2 * h + 1, st_cur, act_cur)

    phase(0, st0_s, a0_s, ub_ref, st1_s, vp_ref, a1_s)
    phase(1, st1_s, a1_s, un_ref, st0_s, va_ref, a0_s)

    @pl.when(e == pl.num_programs(1) - 1)
    def _():
        values(vl_ref, a1_s, 0)
        values(vl_ref, a1_s, 1)
        y = x_ref[...] + g2_ref[0] * acc_s[...].T
        if final_norm:
            y = _rms(y) * fg_ref[...]
        o_ref[...] = y


def _experts(hb, theta, s2, e2, c1, u, vt, x, mod, fg, *, mod_base, rows_per_cond, final_norm):
    n = x.shape[0]
    tb = PEER_TB
    ntg = tb // LANES
    ne = PEER_N // PEER_EC
    once = dict(pipeline_mode=pl.Buffered(1))
    rspec = pl.BlockSpec((PEER_HEADS, ntg, PEER_NKEYS, LANES), lambda i, e: (0, i, 0, 0), **once)
    rows8 = pl.BlockSpec((PEER_HEADS, ntg, SUBLANES, LANES), lambda i, e: (0, i, e, 0))
    uspec = lambda index: pl.BlockSpec((PEER_SUB, D_MODEL), index)
    vspec = lambda index: pl.BlockSpec((D_MODEL, PEER_SUB), index)
    return pl.pallas_call(
        functools.partial(_experts_kernel, final_norm=final_norm),
        grid=(n // tb, ne),
        in_specs=[pl.BlockSpec((tb, D_MODEL), lambda i, e: (i, 0)), rows8, rspec, rspec, rows8,
                  pl.BlockSpec((PEER_SUB, D_MODEL), lambda i, e: (0, 0), **once),
                  uspec(lambda i, e: (2 * e + 1, 0)),
                  uspec(lambda i, e: (jnp.minimum(2 * e + 2, 2 * ne - 2), 0)),
                  vspec(lambda i, e: (0, jnp.maximum(2 * e - 1, 0))),
                  vspec(lambda i, e: (0, 2 * e)),
                  pl.BlockSpec((D_MODEL, PEER_SUB), lambda i, e: (0, 2 * ne - 1), **once),
                  pl.BlockSpec((tb, D_MODEL), lambda i, e: (i, 0)),
                  _mod_spec(5, tb, mod_base, rows_per_cond), _full((1, D_MODEL))],
        out_specs=pl.BlockSpec((tb, D_MODEL), lambda i, e: (i, 0)),
        out_shape=jax.ShapeDtypeStruct((n, D_MODEL), F32),
        scratch_shapes=[pltpu.VMEM((D_MODEL, tb), F32), pltpu.VMEM((ntg, PEER_SUB, LANES), F32),
                        pltpu.VMEM((ntg, PEER_SUB, LANES), F32), pltpu.VMEM((ntg, PEER_SUB, LANES), BF16),
                        pltpu.VMEM((ntg, PEER_SUB, LANES), BF16)],
        compiler_params=_cp(2), name="peer_experts",
    )(hb, theta, s2, e2, c1, u, u, u, vt, vt, vt, x, mod, fg.reshape(1, D_MODEL))


def _peer(x, g, mod, wqt, keys, u, vt, fg, *, mod_base, rows_per_cond, final_norm):
    hb, theta, s2, e2, c1 = _route(x, g, mod, wqt, keys, mod_base=mod_base, rows_per_cond=rows_per_cond)
    return _experts(hb, theta, s2, e2, c1, u, vt, x, mod, fg, mod_base=mod_base, rows_per_cond=rows_per_cond,
                    final_norm=final_norm)


def _chunk_index(s, nc):
    return jnp.where(s < nc, s, 2 * nc - 1 - s)


def _scan_group(nseq, largest):
    return max(g for g in (2, 4, 8) if g <= largest and nseq % g == 0)


def _scan_specs(nc):
    def cur(col):
        return lambda b, s: (b, _chunk_index(s, nc), col)

    def prev(col):
        return lambda b, s: (b, jnp.maximum(_chunk_index(s, nc) * 8 - 1, 0), col)

    def nxt(col):
        return lambda b, s: (b, jnp.minimum(_chunk_index(s, nc) * 8 + 8, nc * 8 - 1), col)

    def out(col):
        return lambda b, s: (b, jnp.where(s < nc, nc - 1, 2 * nc - 1 - s), col)
    return cur, prev, nxt, out


def _conv_silu(prev, cur, nxt, w_ref, b_ref, first, last):
    win = jnp.concatenate([jnp.where(first, 0.0, prev), cur, jnp.where(last, 0.0, nxt)], axis=1)
    acc = b_ref[...] + win[:, 6:70, :] * w_ref[0:1, :]
    for k in range(1, CONV_W):
        acc = acc + win[:, 6 + k:70 + k, :] * w_ref[k:k + 1, :]
    return _silu(acc)


def _tri(bwd, G):
    r = lax.broadcasted_iota(jnp.int32, (G, CHUNK, CHUNK), 1)
    c = lax.broadcasted_iota(jnp.int32, (G, CHUNK, CHUNK), 2)
    lower = (c <= r).astype(F32)
    upper = (c >= r).astype(F32)
    return jnp.where(bwd, upper, lower), jnp.where(bwd, lower, upper)


def _bdot(a, b, contract, precision=None):
    return lax.dot_general(a, b, (((contract[0],), (contract[1],)), ((0,), (0,))), preferred_element_type=F32,
                           precision=precision)


def _ssd_bidir_kernel(*refs, nc, zero_init):
    f_in, b_in = refs[0:9], refs[9:18]
    (cwx_ref, cbx_ref, cwb_ref, cbb_ref, arow_ref, acol_ref, brow_ref, bcol_ref, drow_ref, ng_ref) = refs[18:28]
    rest = refs[28:]
    st0_ref = None if zero_init else rest[0]
    ya_ref, yb_ref, st_ref, ys_s, h_s = rest[0 if zero_init else 1:]
    j = pl.program_id(1)
    nh = SSD_HEADS
    G = f_in[0].shape[0]
    hi = lax.Precision.HIGHEST

    @pl.when(j == 0)
    def _():
        if zero_init:
            h_s[...] = jnp.zeros(h_s.shape, F32)
        else:
            h_s[0:G] = st0_ref[:, 0]
            h_s[G:] = st0_ref[:, 1]

    def conv_half(refs9, first, last):
        _, xp, xc, xn, bp, bc, bn, _, _ = refs9
        return (_conv_silu(xp[...], xc[...], xn[...], cwx_ref, cbx_ref, first, last),
                _conv_silu(bp[...], bc[...], bn[...], cwb_ref, cbb_ref, first, last))

    xs_f, bc_f = conv_half(f_in, j == 0, j == nc - 1)
    xs_b, bc_b = conv_half(b_in, j == nc - 1, j == 0)
    xs = jnp.concatenate([xs_f, xs_b], axis=0)
    bcv = jnp.concatenate([bc_f, bc_b], axis=0)

    def gate_cols(refs9, lo):
        dt_all = _softplus(refs9[7][:, :, 64:96] + brow_ref[...])
        dtt_all = _softplus(refs9[8][:, 0] + bcol_ref[...])
        a_all = dt_all * (-jnp.exp(arow_ref[...]))
        at_all = dtt_all * (-jnp.exp(acol_ref[...]))
        return dt_all[:, :, lo:lo + nh], a_all[:, :, lo:lo + nh], at_all[:, lo:lo + nh, :]

    dt_f, a_f, at_f = gate_cols(f_in, 0)
    dt_b, a_b, at_b = gate_cols(b_in, nh)
    dt = jnp.concatenate([dt_f, dt_b], axis=0)
    a = jnp.concatenate([a_f, a_b], axis=0)
    at = jnp.concatenate([at_f, at_b], axis=0)

    shape = (2 * G, CHUNK, CHUNK)
    ent = lax.broadcasted_iota(jnp.int32, shape, 0)
    r = lax.broadcasted_iota(jnp.int32, shape, 1)
    c = lax.broadcasted_iota(jnp.int32, shape, 2)
    fwd = ent < G
    lower = (c <= r).astype(F32)
    upper = (c >= r).astype(F32)
    tsel = jnp.where(fwd, lower, upper)
    tsel_t = jnp.where(fwd, upper, lower)
    mask = tsel > 0.0
    cum = _bdot(tsel, a, (2, 1), hi)
    cum_t = _bdot(at, tsel_t, (2, 1), hi)
    tot = jnp.concatenate([cum[0:G, CHUNK - 1:CHUNK, :], cum[G:, 0:1, :]], axis=0)

    ys = []
    for g in range(2):
        bg = bcv[:, :, g * SSD_STATE:(g + 1) * SSD_STATE]
        cg = bcv[:, :, 256 + g * SSD_STATE:256 + (g + 1) * SSD_STATE]
        cg16 = cg.astype(BF16)
        cb = _bdot(cg16, bg.astype(BF16), (2, 2))
        for hh in range(nh // 2):
            h = g * (nh // 2) + hh
            cq = cum[:, :, h:h + 1]
            decay = jnp.exp(jnp.where(mask, cq - cum_t[:, h:h + 1, :], NEG_INF))
            xd = xs[:, :, h * SSD_HEAD_DIM:(h + 1) * SSD_HEAD_DIM] * dt[:, :, h:h + 1]
            xd16 = xd.astype(BF16)
            hprev = h_s[:, h]
            y = (_bdot((cb * decay).astype(BF16), xd16, (2, 1))
                 + _bdot(cg16, hprev.astype(BF16), (2, 2)) * jnp.exp(cq))
            ys.append(y)
            th = tot[:, :, h:h + 1]
            bw = bg * jnp.exp(th - cq)
            h_s[:, h] = hprev * jnp.exp(th) + _bdot(xd16, bw.astype(BF16), (1, 1))
    y = jnp.concatenate(ys, axis=2)
    rf = pl.multiple_of(j * CHUNK, CHUNK)
    rb = pl.multiple_of((nc - 1 - j) * CHUNK, CHUNK)

    @pl.when(j < nc // 2)
    def _():
        ys_s[:, pl.ds(rf, CHUNK), :] = y[0:G]
        ys_s[:, pl.ds(rb, CHUNK), :] = y[G:]

    @pl.when(j >= nc // 2)
    def _():
        def finish(now, stored, x_half, z_ref):
            yt = (stored + now + drow_ref[...] * x_half) * _silu(z_ref[...])
            return _rms(yt) * ng_ref[...]
        ya_ref[...] = finish(y[0:G], ys_s[:, pl.ds(rf, CHUNK), :], xs_f, f_in[0])
        yb_ref[...] = finish(y[G:], ys_s[:, pl.ds(rb, CHUNK), :], xs_b, b_in[0])

    @pl.when(j == nc - 1)
    def _():
        st_ref[:, 0] = h_s[0:G]
        st_ref[:, 1] = h_s[G:]


def _ssd_bidir(p0, dtt, cwx, cbx, cwb, cbb, a_log, dt_bias, d_skip, norm_g, st0, *, L):
    nseq, nc = p0.shape[0], L // CHUNK
    zero_init = st0 is None
    G = _scan_group(nseq, 4)
    st_shape = (G, 2, SSD_HEADS, SSD_HEAD_DIM, SSD_STATE)
    states = [] if zero_init else [st0]
    row = lambda v: v.reshape(1, -1)
    col = lambda v: v.reshape(-1, 1)
    small = [cwx, row(cbx), cwb, row(cbb), row(a_log), col(a_log), row(dt_bias), col(dt_bias),
             row(jnp.repeat(d_skip, SSD_HEAD_DIM)), row(norm_g)]

    def specs(chunk):
        cur = lambda colb: (lambda b, j: (b, chunk(j), colb))
        prev = lambda colb: (lambda b, j: (b, jnp.maximum(chunk(j) * 8 - 1, 0), colb))
        nxt = lambda colb: (lambda b, j: (b, jnp.minimum(chunk(j) * 8 + 8, nc * 8 - 1), colb))
        blk = lambda rows, width, index: pl.BlockSpec((G, rows, width), index)
        return [blk(CHUNK, 1024, cur(0)),
                blk(8, 1024, prev(1)), blk(CHUNK, 1024, cur(1)), blk(8, 1024, nxt(1)),
                blk(8, 512, prev(4)), blk(CHUNK, 512, cur(4)), blk(8, 512, nxt(4)),
                blk(CHUNK, LANES, cur(28)),
                pl.BlockSpec((G, 1, 32, CHUNK), lambda b, j: (b, chunk(j), 0, 0))]

    half = nc // 2
    y_sds = jax.ShapeDtypeStruct((nseq, L, SSD_D_INNER), F32)
    ya, yb, st = pl.pallas_call(
        functools.partial(_ssd_bidir_kernel, nc=nc, zero_init=zero_init),
        grid=(nseq // G, nc),
        in_specs=specs(lambda j: j) + specs(lambda j: nc - 1 - j) + [_full(v.shape) for v in small]
        + [pl.BlockSpec(st_shape, lambda b, j: (b, 0, 0, 0, 0)) for _ in states],
        out_specs=[pl.BlockSpec((G, CHUNK, SSD_D_INNER), lambda b, j: (b, jnp.maximum(j, half), 0)),
                   pl.BlockSpec((G, CHUNK, SSD_D_INNER), lambda b, j: (b, jnp.minimum(nc - 1 - j, half - 1), 0)),
                   pl.BlockSpec(st_shape, lambda b, j: (b, 0, 0, 0, 0))],
        out_shape=[y_sds, y_sds, jax.ShapeDtypeStruct((nseq,) + st_shape[1:], F32)],
        scratch_shapes=[pltpu.VMEM((G, L, SSD_D_INNER), F32), pltpu.VMEM((2 * G,) + st_shape[2:], F32)],
        compiler_params=_cp(2), name="ssd_mixer",
    )(*([p0] * 8 + [dtt]) * 2, *small, *states)
    return jnp.concatenate([yb[:, :L // 2], ya[:, L // 2:]], axis=1), st


def _rope_tables(L):
    t = jnp.arange(L)
    r = (t // GRID_W).astype(F32)
    c = (t % GRID_W).astype(F32)
    nf = 16
    inv = ROPE_BASE ** (-jnp.arange(nf, dtype=F32) / nf)
    ang = jnp.concatenate([r[:, None] * inv, c[:, None] * inv], axis=-1)
    ang = jnp.concatenate([ang, ang], axis=-1)
    return jnp.cos(ang), jnp.sin(ang)


def _rope128(x, cos, sin):
    lane = lax.broadcasted_iota(jnp.int32, x.shape, 1)
    rot = jnp.where(lane % 64 < 32, -pltpu.roll(x, 96, 1), pltpu.roll(x, 32, 1))
    return x * cos + rot * sin


def _pad_lanes(x):
    return jnp.concatenate([x, jnp.zeros_like(x)], axis=1)


MLA_QB = 256


def _mla_kernel(*refs, L, Lc, rope, emit):
    refs = list(refs)
    qa_ref, kva_ref, sm_ref, qg_ref, kvg_ref, wqn_ref, wqr_ref, wkk_ref, wkv_ref = refs[:9]
    pos = 9
    if Lc:
        cckv_ref, ckpe_ref = refs[pos:pos + 2]
        pos += 2
    if rope:
        cos_ref, sin_ref = refs[pos:pos + 2]
        pos += 2
    y_ref = refs[pos]
    pos += 1
    if emit:
        ckv_ref = refs[pos]
        pos += 1
    qn_s, qr_s, kn_s, v_s, kpe_s = refs[pos:]
    qb = pl.program_id(1)
    scale = np.float32((MLA_NOPE + MLA_ROPE) ** -0.5)

    @pl.when(qb == 0)
    def _():
        ckv = _rms(kva_ref[...]) * kvg_ref[...]
        if emit:
            ckv_ref[...] = ckv
        kpe = _pad_lanes(sm_ref[:, 0:MLA_ROPE])
        if rope:
            kpe = _rope128(kpe, cos_ref[...], sin_ref[...])
        if Lc:
            ckv = jnp.concatenate([cckv_ref[0], ckv], axis=0)
            kpe = jnp.concatenate([_pad_lanes(ckpe_ref[0]), kpe], axis=0)
        c16 = ckv.astype(BF16)
        kn_s[...] = _dot(c16, wkk_ref[...]).astype(BF16)
        v_s[...] = _dot(c16, wkv_ref[...]).astype(BF16)
        kpe_s[...] = kpe.astype(BF16)

    qn = (_rms(qa_ref[...]) * qg_ref[...]).astype(BF16)
    qn_s[...] = _dot(qn, wqn_ref[...]).astype(BF16)
    qr = _dot(qn, wqr_ref[...])
    if rope:
        q0 = pl.multiple_of(qb * MLA_QB, MLA_QB)
        cos = cos_ref[pl.ds(q0, MLA_QB), :]
        sin = sin_ref[pl.ds(q0, MLA_QB), :]
        for h in range(MLA_HEADS):
            qr_s[:, h * LANES:(h + 1) * LANES] = _rope128(qr[:, h * LANES:(h + 1) * LANES], cos, sin).astype(BF16)
    else:
        qr_s[...] = qr.astype(BF16)

    def head(h, _):
        c0 = h * LANES if isinstance(h, int) else pl.multiple_of(h * LANES, LANES)
        s = (_dot_nt(qn_s[:, pl.ds(c0, LANES)], kn_s[:, pl.ds(c0, LANES)])
             + _dot_nt(qr_s[:, pl.ds(c0, LANES)], kpe_s[...])) * scale
        p = jnp.exp(s - jnp.max(s, axis=-1, keepdims=True))
        p = p / jnp.sum(p, axis=-1, keepdims=True)
        y_ref[:, pl.ds(c0, LANES)] = _dot(p.astype(BF16), v_s[:, pl.ds(c0, LANES)])
        return 0

    if L + Lc <= 512:
        for h in range(MLA_HEADS):
            head(h, 0)
    else:
        lax.fori_loop(0, MLA_HEADS, head, 0)


def _mla(p0, qg, kvg, wqn, wqr, wkk, wkv, cache_ckv, cache_kpe, rope_tabs, *, L):
    n = p0.shape[0]
    nseq, nqb = n // L, L // MLA_QB
    Lc = 0 if cache_ckv is None else cache_ckv.shape[1]
    rope = rope_tabs is not None
    emit = cache_ckv is None
    Lk = L + Lc
    args = [p0, p0, p0, qg.reshape(1, -1), kvg.reshape(1, -1), wqn, wqr, wkk, wkv]
    specs = [pl.BlockSpec((MLA_QB, MLA_RANK), lambda b, q: (b * nqb + q, 5)),
             pl.BlockSpec((L, MLA_RANK), lambda b, q: (b, 6)),
             pl.BlockSpec((L, LANES), lambda b, q: (b, 28)),
             _full((1, MLA_RANK)), _full((1, MLA_RANK)), _full(wqn.shape), _full(wqr.shape), _full(wkk.shape),
             _full(wkv.shape)]
    if Lc:
        args += [cache_ckv, cache_kpe]
        specs += [pl.BlockSpec((1, Lc, MLA_RANK), lambda b, q: (b, 0, 0)),
                  pl.BlockSpec((1, Lc, MLA_ROPE), lambda b, q: (b, 0, 0))]
    if rope:
        args += list(rope_tabs)
        specs += [_full((L, LANES)), _full((L, LANES))]
    out_specs = [pl.BlockSpec((MLA_QB, 1024), lambda b, q: (b * nqb + q, 0))]
    out_shape = [jax.ShapeDtypeStruct((n, 1024), F32)]
    if emit:
        out_specs.append(pl.BlockSpec((L, MLA_RANK), lambda b, q: (b, 0)))
        out_shape.append(jax.ShapeDtypeStruct((n, MLA_RANK), F32))
    return pl.pallas_call(
        functools.partial(_mla_kernel, L=L, Lc=Lc, rope=rope, emit=emit),
        grid=(nseq, nqb), in_specs=specs, out_specs=out_specs, out_shape=out_shape,
        scratch_shapes=[pltpu.VMEM((MLA_QB, 1024), BF16), pltpu.VMEM((MLA_QB, 1024), BF16),
                        pltpu.VMEM((Lk, 1024), BF16), pltpu.VMEM((Lk, 1024), BF16), pltpu.VMEM((Lk, LANES), BF16)],
        compiler_params=_cp(2), name="mla_attention",
    )(*args)


def _gqa_ctx_kernel(q_ref, k_ref, v_ref, sink_ref, y_ref):
    q = q_ref[...]
    k = k_ref[...].astype(BF16)
    v = v_ref[...].astype(BF16)
    scale = np.float32(GQA_HEAD_DIM ** -0.5)
    group = GQA_HEADS // GQA_KV_HEADS
    outs = []
    for h in range(GQA_HEADS):
        kh = h // group
        ks = slice(kh * GQA_HEAD_DIM, (kh + 1) * GQA_HEAD_DIM)
        s = _dot_nt(q[:, h * GQA_HEAD_DIM:(h + 1) * GQA_HEAD_DIM].astype(BF16), k[:, ks]) * scale
        m = jnp.maximum(jnp.max(s, axis=-1, keepdims=True), sink_ref[h])
        p = jnp.exp(s - m)
        p = p / (jnp.sum(p, axis=-1, keepdims=True) + jnp.exp(sink_ref[h] - m))
        outs.append(_dot(p.astype(BF16), v[:, ks]))
    y_ref[...] = jnp.concatenate(outs, axis=1)


def _gqa_ctx(p1, sink, *, L):
    n = p1.shape[0]
    return pl.pallas_call(
        _gqa_ctx_kernel,
        grid=(n // L,),
        in_specs=[pl.BlockSpec((L, 1024), lambda b: (b, 2)), pl.BlockSpec((L, 256), lambda b: (b, 20)),
                  pl.BlockSpec((L, 256), lambda b: (b, 21)), pl.BlockSpec(memory_space=pltpu.SMEM)],
        out_specs=pl.BlockSpec((L, 1024), lambda b: (b, 0)),
        out_shape=jax.ShapeDtypeStruct((n, 1024), F32),
        compiler_params=_cp(1), name="gqa_context",
    )(p1, p1, p1, sink)


GQA_QB = 128
GQA_SPAN = GQA_QB + 2 * WINDOW


def _gqa_win_kernel(q_ref, k_ref, v_ref, ck_ref, cv_ref, cos_ref, sin_ref, sink_ref, y_ref, kp_s, vp_s, *, L):
    qb = pl.program_id(1)
    scale = np.float32(GQA_HEAD_DIM ** -0.5)
    group = GQA_HEADS // GQA_KV_HEADS

    @pl.when(qb == 0)
    def _():
        zeros = jnp.zeros((WINDOW, 256), BF16)
        kp_s[0:WINDOW, :] = zeros
        kp_s[WINDOW + L:, :] = zeros
        vp_s[0:WINDOW, :] = zeros
        vp_s[WINDOW + L:, :] = zeros
        for j in range(2):
            cs = slice(j * LANES, (j + 1) * LANES)
            kp_s[WINDOW:WINDOW + L, cs] = _rope128(k_ref[:, cs], cos_ref[...], sin_ref[...]).astype(BF16)
        vp_s[WINDOW:WINDOW + L, :] = v_ref[...].astype(BF16)

    q0 = pl.multiple_of(qb * GQA_QB, GQA_QB)
    cos = cos_ref[pl.ds(q0, GQA_QB), :]
    sin = sin_ref[pl.ds(q0, GQA_QB), :]
    kw = kp_s[pl.ds(q0, GQA_SPAN), :]
    vw = vp_s[pl.ds(q0, GQA_SPAN), :]
    kc = ck_ref[0].astype(BF16)
    vc = cv_ref[0].astype(BF16)
    qpos = q0 + lax.broadcasted_iota(jnp.int32, (GQA_QB, GQA_SPAN), 0)
    kpos = q0 - WINDOW + lax.broadcasted_iota(jnp.int32, (GQA_QB, GQA_SPAN), 1)
    bias = jnp.where(kpos < 0, NEG_INF, jnp.where(kpos >= L, NEG_INF, jnp.where(jnp.abs(qpos - kpos) <= WINDOW, 0.0, NEG_INF)))
    outs = []
    for j in range(GQA_HEADS * GQA_HEAD_DIM // LANES):
        qj = _rope128(q_ref[:, j * LANES:(j + 1) * LANES], cos, sin).astype(BF16)
        for half in range(2):
            h = 2 * j + half
            kh = h // group
            ks = slice(kh * GQA_HEAD_DIM, (kh + 1) * GQA_HEAD_DIM)
            qh = qj[:, half * GQA_HEAD_DIM:(half + 1) * GQA_HEAD_DIM]
            s_loc = _dot_nt(qh, kw[:, ks]) * scale + bias
            s_ctx = _dot_nt(qh, kc[:, ks]) * scale
            m = jnp.maximum(jnp.maximum(jnp.max(s_loc, axis=-1, keepdims=True), jnp.max(s_ctx, axis=-1, keepdims=True)),
                            sink_ref[h])
            p_loc = jnp.exp(s_loc - m)
            p_ctx = jnp.exp(s_ctx - m)
            inv = 1.0 / (jnp.sum(p_loc, axis=-1, keepdims=True) + jnp.sum(p_ctx, axis=-1, keepdims=True)
                         + jnp.exp(sink_ref[h] - m))
            outs.append(_dot((p_loc * inv).astype(BF16), vw[:, ks]) + _dot((p_ctx * inv).astype(BF16), vc[:, ks]))
    y_ref[...] = jnp.concatenate(outs, axis=1)


def _gqa_win(p1, cache_k, cache_v, cos, sin, sink, *, L):
    n = p1.shape[0]
    nseq, nqb = n // L, L // GQA_QB
    Lc = cache_k.shape[1]
    return pl.pallas_call(
        functools.partial(_gqa_win_kernel, L=L),
        grid=(nseq, nqb),
        in_specs=[pl.BlockSpec((GQA_QB, 1024), lambda b, q: (b * nqb + q, 2)),
                  pl.BlockSpec((L, 256), lambda b, q: (b, 20)), pl.BlockSpec((L, 256), lambda b, q: (b, 21)),
                  pl.BlockSpec((1, Lc, 256), lambda b, q: (b, 0, 0)), pl.BlockSpec((1, Lc, 256), lambda b, q: (b, 0, 0)),
                  _full((L, LANES)), _full((L, LANES)), pl.BlockSpec(memory_space=pltpu.SMEM)],
        out_specs=pl.BlockSpec((GQA_QB, 1024), lambda b, q: (b * nqb + q, 0)),
        out_shape=jax.ShapeDtypeStruct((n, 1024), F32),
        scratch_shapes=[pltpu.VMEM((L + 2 * WINDOW, 256), BF16), pltpu.VMEM((L + 2 * WINDOW, 256), BF16)],
        compiler_params=_cp(2), name="gqa_window",
    )(p1, p1, p1, cache_k, cache_v, cos, sin, sink)


def _mlstm_kernel(qp_ref, qc_ref, qn_ref, v_ref, o_ref, sm_ref, gt_ref, cw_ref, cb_ref, brow_ref, bcol_ref, ng_ref,
                  *rest, nc, zero_init):
    init_refs = () if zero_init else rest[:3]
    y_ref, cst_ref, nst_ref, mst_ref, hf_s, c_s, n_s, m_s = rest[len(init_refs):]
    s = pl.program_id(1)
    bwd = s >= nc
    cidx = _chunk_index(s, nc)
    nh = ML_HEADS
    hd = ML_HEAD_DIM

    def load_initial(direction):
        for k, dst in enumerate((c_s, n_s, m_s)):
            dst[...] = jnp.zeros(dst.shape, F32) if zero_init else init_refs[k][:, direction]

    @pl.when(s == 0)
    def _():
        load_initial(0)

    @pl.when(s == nc)
    def _():
        cst_ref[:, 0] = c_s[...]
        nst_ref[:, 0] = n_s[...]
        mst_ref[:, 0] = m_s[...]
        load_initial(1)

    G = qc_ref.shape[0]
    tsel, tsel_t = _tri(bwd, G)
    mask = tsel > 0.0
    kscale = np.float32(hd ** -0.5)
    r0 = pl.multiple_of(cidx * CHUNK, CHUNK)
    hi = lax.Precision.HIGHEST

    qk = _conv_silu(qp_ref[...], qc_ref[...], qn_ref[...], cw_ref, cb_ref, cidx == 0, cidx == nc - 1)
    gates = sm_ref[:, :, 0:4 * nh] + brow_ref[...]
    gates_t = gt_ref[:, 0] + bcol_ref[...]
    li = jnp.where(bwd, gates[:, :, nh:2 * nh], gates[:, :, 0:nh])
    li_t = jnp.where(bwd, gates_t[:, nh:2 * nh, :], gates_t[:, 0:nh, :])
    lf = -_softplus(-jnp.where(bwd, gates[:, :, 3 * nh:4 * nh], gates[:, :, 2 * nh:3 * nh]))
    lf_t = -_softplus(-jnp.where(bwd, gates_t[:, 3 * nh:4 * nh, :], gates_t[:, 2 * nh:3 * nh, :]))
    bc = _bdot(tsel, lf, (2, 1), hi)
    bc_t = _bdot(lf_t, tsel_t, (2, 1), hi)
    tot = jnp.where(bwd, bc[:, 0:1, :], bc[:, CHUNK - 1:CHUNK, :])

    hs = []
    for h in range(nh):
        cs = slice(h * hd, (h + 1) * hd)
        q = qk[:, :, cs]
        k = qk[:, :, ML_D + h * hd:ML_D + (h + 1) * hd] * kscale
        q16, k16 = q.astype(BF16), k.astype(BF16)
        v = v_ref[:, :, cs]
        bq = bc[:, :, h:h + 1]
        m_prev = m_s[:, h:h + 1, 0:1]
        log_d = jnp.where(mask, bq - bc_t[:, h:h + 1, :] + li_t[:, h:h + 1, :], NEG_INF)
        log_inter = bq + m_prev
        m_out = jnp.maximum(log_inter, jnp.max(log_d, axis=2, keepdims=True))
        sd = _bdot(q16, k16, (2, 2)) * jnp.exp(log_d - m_out)
        w_inter = jnp.exp(log_inter - m_out)
        c_prev = c_s[:, h]
        n_prev = n_s[:, h:h + 1, :]
        num = _bdot(sd.astype(BF16), v.astype(BF16), (2, 1)) + w_inter * _bdot(q16, c_prev.astype(BF16), (2, 2))
        den = jnp.sum(sd, axis=2, keepdims=True) + w_inter * jnp.sum(q * n_prev, axis=2, keepdims=True)
        hs.append(num / jnp.maximum(jnp.abs(den), jnp.exp(-m_out)))
        th = tot[:, :, h:h + 1]
        end_inter = th + m_prev
        end_intra = th - bq + li[:, :, h:h + 1]
        m_new = jnp.maximum(end_inter, jnp.max(end_intra, axis=1, keepdims=True))
        w_c = jnp.exp(end_inter - m_new)
        w_k = jnp.exp(end_intra - m_new)
        c_s[:, h] = w_c * c_prev + _bdot((v * w_k).astype(BF16), k16, (1, 1))
        n_s[:, h:h + 1, :] = w_c * n_prev + jnp.sum(k * w_k, axis=1, keepdims=True)
        m_s[:, h:h + 1, :] = jnp.broadcast_to(m_new, (G, 1, hd))
    hcat = jnp.concatenate(hs, axis=2)

    @pl.when(jnp.logical_not(bwd))
    def _():
        hf_s[:, pl.ds(r0, CHUNK), :] = hcat

    @pl.when(bwd)
    def _():
        tot_h = hf_s[:, pl.ds(r0, CHUNK), :] + hcat
        normed = jnp.concatenate([_rms(tot_h[:, :, h * hd:(h + 1) * hd]) for h in range(nh)], axis=2) * ng_ref[...]
        y_ref[...] = normed * jax.nn.sigmoid(o_ref[...])

    @pl.when(s == 2 * nc - 1)
    def _():
        cst_ref[:, 1] = c_s[...]
        nst_ref[:, 1] = n_s[...]
        mst_ref[:, 1] = m_s[...]


def _mlstm(p1, gt, conv_w, conv_b, ig_b, fg_b, norm_g, c0, n0, m0, *, L):
    nseq, nc = p1.shape[0], L // CHUNK
    zero_init = c0 is None
    G = _scan_group(nseq, 8 if zero_init else 4)
    cur, prev, nxt, out = _scan_specs(nc)
    cshape = (G, 2, ML_HEADS, ML_HEAD_DIM, ML_HEAD_DIM)
    nshape = (G, 2, ML_HEADS, ML_HEAD_DIM)
    states = [] if zero_init else [c0, n0, m0]
    gb = jnp.concatenate([ig_b.reshape(-1), fg_b.reshape(-1)])
    small = [conv_w, conv_b.reshape(1, -1), gb.reshape(1, -1), gb.reshape(-1, 1), norm_g.reshape(1, -1)]
    st = lambda shape: pl.BlockSpec(shape, lambda b, s: (b,) + (0,) * (len(shape) - 1))
    blk = lambda rows, width, index: pl.BlockSpec((G, rows, width), index)
    return pl.pallas_call(
        functools.partial(_mlstm_kernel, nc=nc, zero_init=zero_init),
        grid=(nseq // G, 2 * nc),
        in_specs=[blk(8, 2048, prev(0)), blk(CHUNK, 2048, cur(0)), blk(8, 2048, nxt(0)),
                  blk(CHUNK, 1024, cur(3)), blk(CHUNK, 1024, cur(4)), blk(CHUNK, LANES, cur(44)),
                  pl.BlockSpec((G, 1, 32, CHUNK), lambda b, s: (b, _chunk_index(s, nc), 0, 0))]
        + [_full(v.shape) for v in small] + ([] if zero_init else [st(cshape), st(nshape), st(nshape)]),
        out_specs=[blk(CHUNK, ML_D, out(0)), st(cshape), st(nshape), st(nshape)],
        out_shape=[jax.ShapeDtypeStruct((nseq, L, ML_D), F32), jax.ShapeDtypeStruct((nseq,) + cshape[1:], F32),
                   jax.ShapeDtypeStruct((nseq,) + nshape[1:], F32), jax.ShapeDtypeStruct((nseq,) + nshape[1:], F32)],
        scratch_shapes=[pltpu.VMEM((G, L, ML_D), F32), pltpu.VMEM(cshape[:1] + cshape[2:], F32),
                        pltpu.VMEM(nshape[:1] + nshape[2:], F32), pltpu.VMEM(nshape[:1] + nshape[2:], F32)],
        compiler_params=_cp(2), name="mlstm_mixer",
    )(p1, p1, p1, p1, p1, p1, gt, *small, *states)


def _prep_layer0(w_in, w_qb, w_kvb, w_out):
    pad = jnp.zeros((D_MODEL, L0_COLS - 3680), F32)
    w = jnp.concatenate([w_in[:, :2560], w_in[:, 2592:3680], w_in[:, 2560:2592], pad], axis=1).astype(BF16)
    q3 = w_qb.reshape(MLA_RANK, MLA_HEADS, MLA_NOPE + MLA_ROPE)
    wqn = q3[:, :, :MLA_NOPE].reshape(MLA_RANK, -1).astype(BF16)
    wqr = jnp.concatenate([q3[:, :, MLA_NOPE:], jnp.zeros((MLA_RANK, MLA_HEADS, LANES - MLA_ROPE), F32)], axis=-1)
    wqr = wqr.reshape(MLA_RANK, -1).astype(BF16)
    k3 = w_kvb.reshape(MLA_RANK, MLA_HEADS, MLA_NOPE + MLA_V)
    wkk = k3[:, :, :MLA_NOPE].reshape(MLA_RANK, -1).astype(BF16)
    wkv = k3[:, :, MLA_NOPE:].reshape(MLA_RANK, -1).astype(BF16)
    return w, wqn, wqr, wkk, wkv, w_out.astype(BF16)


def _prep_layer1(w_in, w_out):
    pad = jnp.zeros((D_MODEL, L1_COLS - 5664), F32)
    w = jnp.concatenate([w_in[:, 1536:3584], w_in[:, 0:1024], w_in[:, 3584:5632], w_in[:, 1024:1536],
                         w_in[:, 5632:5664], pad], axis=1).astype(BF16)
    return w, w_out.astype(BF16)


def _prep_peer(wq, keys, u, v):
    return (wq.T.astype(BF16), keys.reshape(PEER_HEADS * 2, PEER_NKEYS, -1).astype(BF16), u.astype(BF16),
            v.T.astype(BF16))


def _chunk_transposed(cols, L):
    return cols.reshape(-1, L // CHUNK, CHUNK, cols.shape[1]).transpose(0, 1, 3, 2)


def _trunk(x, L, mod0, mod1, w0, w1, peer0, peer1, p, ctx, rope_tabs, mod_base, rows_per_cond, fg):
    kw = dict(mod_base=mod_base, rows_per_cond=rows_per_cond)
    w_in0, wqn, wqr, wkk, wkv, w_out0 = w0
    w_in1, w_out1 = w1
    nseq = x.shape[0] // L

    p0 = _proj(x, p['l0_norm1_g'], mod0, w_in0, tn=1280, **kw)
    cw, cb = p['l0_ssd_conv_w'], p['l0_ssd_conv_b']
    y_ssd, ssd_state = _ssd_bidir(p0.reshape(nseq, L, -1), _chunk_transposed(p0[:, 3648:3680], L), cw[:, :1024],
                                  cb[:1024], cw[:, 1024:], cb[1024:], p['l0_ssd_A_log'], p['l0_ssd_dt_bias'],
                                  p['l0_ssd_D'], p['l0_ssd_norm_g'], None if ctx is None else ctx[0], L=L)
    y_ssd = y_ssd.reshape(nseq * L, -1)
    mla_tabs = None
    if rope_tabs is not None:
        z64 = jnp.zeros_like(rope_tabs[0])
        mla_tabs = (jnp.concatenate([rope_tabs[0], z64], axis=1), jnp.concatenate([rope_tabs[1], z64], axis=1))
    mla_out = _mla(p0, p['l0_mla_q_norm_g'], p['l0_mla_kv_norm_g'], wqn, wqr, wkk, wkv,
                   None if ctx is None else ctx[1], None if ctx is None else ctx[2], mla_tabs, L=L)
    x = _outproj(y_ssd, mla_out[0], w_out0, x, mod0, **kw)
    x = _peer(x, p['l0_norm2_g'], mod0, *peer0, fg, final_norm=False, **kw)

    p1 = _proj(x, p['l1_norm1_g'], mod1, w_in1, tn=1152, **kw)
    if ctx is None:
        y_gqa = _gqa_ctx(p1, p['l1_gqa_sink'], L=L)
        c0 = n0 = m0 = None
    else:
        gqa_tabs = tuple(jnp.concatenate([t, t], axis=1) for t in rope_tabs)
        y_gqa = _gqa_win(p1, ctx[3].reshape(nseq, -1, 256), ctx[4].reshape(nseq, -1, 256), *gqa_tabs,
                         p['l1_gqa_sink'], L=L)
        c0, n0 = ctx[5], ctx[6]
        m0 = jnp.broadcast_to(ctx[7][..., None], ctx[7].shape + (ML_HEAD_DIM,))
    y_ml, mc, mn, mm = _mlstm(p1.reshape(nseq, L, -1), _chunk_transposed(p1[:, 5632:5664], L), p['l1_ml_conv_w'],
                              p['l1_ml_conv_b'], p['l1_ml_ig_b'], p['l1_ml_fg_b'], p['l1_ml_norm_g'], c0, n0, m0, L=L)
    y_ml = y_ml.reshape(nseq * L, -1)
    x = _outproj(y_gqa, y_ml, w_out1, x, mod1, **kw)
    y = _peer(x, p['l1_norm2_g'], mod1, *peer1, fg, final_norm=True, **kw)
    new = None
    if ctx is None:
        new = (ssd_state, mla_out[1].reshape(nseq, L, MLA_RANK), p0[:, 3584:3648].reshape(nseq, L, MLA_ROPE),
               p1[:, 5120:5376].reshape(nseq, L, GQA_KV_HEADS, GQA_HEAD_DIM),
               p1[:, 5376:5632].reshape(nseq, L, GQA_KV_HEADS, GQA_HEAD_DIM), mc, mn, mm[..., 0])
    return y, new


def kernel(x_prompt, x_sample, state_l0_ssd, cache_l0_mla_ckv, cache_l0_mla_kpe, cache_l1_gqa_k, cache_l1_gqa_v, state_l1_mlstm_C, state_l1_mlstm_n, state_l1_mlstm_m, c, c_ctx, final_norm_g, l0_ada_w, l0_ada_b, l0_norm1_g, l0_norm2_g, l0_w_in, l0_ssd_conv_w, l0_ssd_conv_b, l0_ssd_A_log, l0_ssd_dt_bias, l0_ssd_D, l0_ssd_norm_g, l0_mla_q_norm_g, l0_mla_w_qb, l0_mla_kv_norm_g, l0_mla_w_kvb, l0_w_out, l0_peer_wq, l0_peer_keys, l0_peer_u, l0_peer_v, l1_ada_w, l1_ada_b, l1_norm1_g, l1_norm2_g, l1_w_in, l1_gqa_sink, l1_ml_conv_w, l1_ml_conv_b, l1_ml_ig_b, l1_ml_fg_b, l1_ml_norm_g, l1_w_out, l1_peer_wq, l1_peer_keys, l1_peer_u, l1_peer_v):
    p = dict(l0_norm1_g=l0_norm1_g, l0_norm2_g=l0_norm2_g, l0_ssd_conv_w=l0_ssd_conv_w, l0_ssd_conv_b=l0_ssd_conv_b,
             l0_ssd_A_log=l0_ssd_A_log, l0_ssd_dt_bias=l0_ssd_dt_bias, l0_ssd_D=l0_ssd_D, l0_ssd_norm_g=l0_ssd_norm_g,
             l0_mla_q_norm_g=l0_mla_q_norm_g, l0_mla_kv_norm_g=l0_mla_kv_norm_g, l1_norm1_g=l1_norm1_g,
             l1_norm2_g=l1_norm2_g, l1_gqa_sink=l1_gqa_sink, l1_ml_conv_w=l1_ml_conv_w, l1_ml_conv_b=l1_ml_conv_b,
             l1_ml_ig_b=l1_ml_ig_b, l1_ml_fg_b=l1_ml_fg_b, l1_ml_norm_g=l1_ml_norm_g)
    nb, seq = x_prompt.shape[:2]
    db, dseq = x_sample.shape[:2]
    assert db <= 7 and seq % MLA_QB == 0 and dseq % MLA_QB == 0

    cond8 = jnp.zeros((8, D_MODEL), F32).at[0].set(c_ctx).at[1:1 + db].set(c)
    mod0 = _ada(cond8, l0_ada_w, l0_ada_b).reshape(8, 1, 6 * D_MODEL)
    mod1 = _ada(cond8, l1_ada_w, l1_ada_b).reshape(8, 1, 6 * D_MODEL)
    w0 = _prep_layer0(l0_w_in, l0_mla_w_qb, l0_mla_w_kvb, l0_w_out)
    w1 = _prep_layer1(l1_w_in, l1_w_out)
    peer0 = _prep_peer(l0_peer_wq, l0_peer_keys, l0_peer_u, l0_peer_v)
    peer1 = _prep_peer(l1_peer_wq, l1_peer_keys, l1_peer_u, l1_peer_v)

    y_prompt, new = _trunk(x_prompt.reshape(nb * seq, D_MODEL), seq, mod0, mod1, w0, w1, peer0, peer1, p, None, None,
                           0, nb * seq, final_norm_g)
    ctx = (state_l0_ssd, cache_l0_mla_ckv, cache_l0_mla_kpe, cache_l1_gqa_k, cache_l1_gqa_v, state_l1_mlstm_C,
           state_l1_mlstm_n, state_l1_mlstm_m)
    y_sample, _ = _trunk(x_sample.reshape(db * dseq, D_MODEL), dseq, mod0, mod1, w0, w1, peer0, peer1, p, ctx,
                         _rope_tables(dseq), 1, dseq, final_norm_g)
    return (y_prompt.reshape(nb, seq, D_MODEL), y_sample.reshape(db, dseq, D_MODEL)) + new
```

```python
import functools

import jax
import jax.numpy as jnp
import numpy as np
from jax import lax
from jax.experimental import pallas as pl
from jax.experimental.pallas import tpu as pltpu

F32 = jnp.float32
BF16 = jnp.bfloat16
NEG_INF = float("-inf")

D_MODEL = 2048
EPS = 1e-6
ROPE_BASE = 10000.0
GRID_W = 64
CONV_W = 5
CHUNK = 64
SSD_HEADS = 16
SSD_HEAD_DIM = 64
SSD_D_INNER = 1024
SSD_STATE = 128
MLA_HEADS = 8
MLA_NOPE = 128
MLA_ROPE = 64
MLA_V = 128
MLA_RANK = 512
GQA_HEADS = 16
GQA_KV_HEADS = 4
GQA_HEAD_DIM = 64
WINDOW = 128
ML_HEADS = 8
ML_HEAD_DIM = 128
ML_D = 1024
PEER_HEADS = 8
PEER_NKEYS = 128
PEER_N = PEER_NKEYS * PEER_NKEYS
PEER_TOPK = 16

LANES = 128
SUBLANES = 8
VMEM_LIMIT = 56 * 2**20

L0_COLS = 3840
L1_COLS = 5760


def _cp(n_grid, flags=None):
    return pltpu.CompilerParams(dimension_semantics=("arbitrary",) * n_grid, vmem_limit_bytes=VMEM_LIMIT, flags=flags)


def _silu(x):
    return x * jax.nn.sigmoid(x)


def _softplus(x):
    return jnp.maximum(x, 0.0) + jnp.log1p(jnp.exp(-jnp.abs(x)))


def _rms(x):
    return x * lax.rsqrt(jnp.mean(x * x, axis=-1, keepdims=True) + EPS)


def _dot(a, b):
    return jnp.dot(a, b, preferred_element_type=F32)


def _dot_nt(a, b):
    return lax.dot_general(a, b, (((1,), (1,)), ((), ())), preferred_element_type=F32)


def _full(shape):
    nd = len(shape)
    return pl.BlockSpec(shape, lambda *_: (0,) * nd)


def _mod_spec(chunk, tb, mod_base, rows_per_cond):
    def index(i, *_):
        return (mod_base + (i * tb) // rows_per_cond, 0, chunk)
    return pl.BlockSpec((1, 1, D_MODEL), index)


def _ada_kernel(c_ref, w_ref, b_ref, o_ref):
    s = _silu(c_ref[...])
    o_ref[...] = _dot(s.astype(BF16), w_ref[...].astype(BF16)) + b_ref[...]


def _ada(cond8, w, b):
    n = w.shape[1]
    tn = 1024
    return pl.pallas_call(
        _ada_kernel,
        grid=(n // tn,),
        in_specs=[_full((8, D_MODEL)), pl.BlockSpec((D_MODEL, tn), lambda j: (0, j)),
                  pl.BlockSpec((1, tn), lambda j: (0, j))],
        out_specs=pl.BlockSpec((8, tn), lambda j: (0, j)),
        out_shape=jax.ShapeDtypeStruct((8, n), F32),
        compiler_params=_cp(1), name="ada_table",
    )(cond8, w, b.reshape(1, n))


def _proj_kernel(x_ref, g_ref, sc_ref, sh_ref, w_ref, o_ref, h_ref):
    @pl.when(pl.program_id(1) == 0)
    def _():
        h = _rms(x_ref[...]) * g_ref[...] * (1.0 + sc_ref[0]) + sh_ref[0]
        h_ref[...] = h.astype(BF16)
    o_ref[...] = _dot(h_ref[...], w_ref[...])


def _proj(x, g, mod, w, *, tn, mod_base, rows_per_cond):
    n, ncol = x.shape[0], w.shape[1]
    tb = 1024
    return pl.pallas_call(
        _proj_kernel,
        grid=(n // tb, ncol // tn),
        in_specs=[pl.BlockSpec((tb, D_MODEL), lambda i, j: (i, 0)), _full((1, D_MODEL)),
                  _mod_spec(1, tb, mod_base, rows_per_cond), _mod_spec(0, tb, mod_base, rows_per_cond),
                  pl.BlockSpec((D_MODEL, tn), lambda i, j: (0, j))],
        out_specs=pl.BlockSpec((tb, tn), lambda i, j: (i, j)),
        out_shape=jax.ShapeDtypeStruct((n, ncol), F32),
        scratch_shapes=[pltpu.VMEM((tb, D_MODEL), BF16)],
        compiler_params=_cp(2), name="norm_mod_proj",
    )(x, g.reshape(1, D_MODEL), mod, mod, w)


def _outproj_kernel(ya_ref, yb_ref, w_ref, x_ref, g1_ref, o_ref):
    half = ya_ref.shape[1]
    y = _dot(ya_ref[...].astype(BF16), w_ref[0:half, :]) + _dot(yb_ref[...].astype(BF16), w_ref[half:, :])
    o_ref[...] = x_ref[...] + g1_ref[0] * y


def _outproj(ya, yb, w, x, mod, *, mod_base, rows_per_cond):
    n = x.shape[0]
    tb = 512
    return pl.pallas_call(
        _outproj_kernel,
        grid=(n // tb,),
        in_specs=[pl.BlockSpec((tb, ya.shape[1]), lambda i: (i, 0)), pl.BlockSpec((tb, yb.shape[1]), lambda i: (i, 0)),
                  _full(w.shape), pl.BlockSpec((tb, D_MODEL), lambda i: (i, 0)),
                  _mod_spec(2, tb, mod_base, rows_per_cond)],
        out_specs=pl.BlockSpec((tb, D_MODEL), lambda i: (i, 0)),
        out_shape=jax.ShapeDtypeStruct((n, D_MODEL), F32),
        compiler_params=_cp(1), name="out_proj_residual",
    )(ya, yb, w, x, mod)


ROUTE_TB = 512
_CAND_KEEP = (16, 8, 5, 4, 3, 2, 2, 2)


def _sort_pairs(n):
    pairs = []

    def merge(lo, hi, r):
        step = r * 2
        if step < hi - lo:
            merge(lo, hi, step)
            merge(lo + r, hi, step)
            pairs.extend((i, i + r) for i in range(lo + r, hi - r, step))
        else:
            pairs.append((lo, lo + r))

    def sort(lo, hi):
        if hi - lo >= 1:
            mid = lo + (hi - lo) // 2
            sort(lo, mid)
            sort(mid + 1, hi)
            merge(lo, hi, 1)

    sort(0, 15)
    return [(i, j) for i, j in pairs if j < n]


def _top_values(x, count):
    groups = [x[r:r + SUBLANES] for r in range(0, x.shape[0], SUBLANES)]
    n = len(groups)
    for i, j in _sort_pairs(n):
        groups[i], groups[j] = jnp.maximum(groups[i], groups[j]), jnp.minimum(groups[i], groups[j])
    sub = lax.broadcasted_iota(jnp.int32, groups[0].shape, 0)
    tops = []
    for t in range(count):
        head = groups[0]
        m = jnp.max(head, axis=0, keepdims=True)
        tops.append(m)
        if t == count - 1:
            break
        first = jnp.min(jnp.where(head == m, sub, SUBLANES), axis=0, keepdims=True)
        popped = sub == first
        depth = min(count - 1 - t, n)
        for r in range(depth):
            below = groups[r + 1] if r + 1 < n else NEG_INF
            groups[r] = jnp.where(popped, below, groups[r])
    return tops


def _route_kernel(x_ref, g_ref, sc_ref, sh_ref, wqt_ref, keys_ref,
                  hb_ref, th_ref, s2_ref, e2_ref, c1_ref, qt_s, top_s):
    tb = x_ref.shape[0]
    h = _rms(x_ref[...]) * g_ref[...] * (1.0 + sc_ref[0]) + sh_ref[0]
    hb = h.astype(BF16)
    hb_ref[...] = hb
    qt_s[...] = _dot_nt(wqt_ref[...], hb).astype(BF16)

    iota_8 = lax.broadcasted_iota(jnp.int32, (SUBLANES, tb), 0)

    def head(hd, _):
        scores = []
        for c in range(2):
            r0 = pl.multiple_of((hd * 2 + c) * PEER_NKEYS, PEER_NKEYS)
            s = _dot(keys_ref[hd * 2 + c], qt_s[pl.ds(r0, PEER_NKEYS), :])
            scores.append(s)
            for it, m in enumerate(_top_values(s, PEER_TOPK)):
                top_s[c, it:it + 1, :] = m
        a1 = top_s[0]
        a2 = top_s[1]
        pieces = [a1[0:1] + a2, a1[1:2] + a2[0:8]]
        for i in range(2, 8):
            pieces.append(jnp.where(iota_8 < _CAND_KEEP[i], a1[i:i + 1] + a2[0:8], NEG_INF))
        pieces.append(a1[8:16] + a2[0:1])
        cand = jnp.concatenate(pieces, axis=0)
        tau = _top_values(cand, PEER_TOPK)[-1]
        top = a1[0:1] + a2[0:1]
        z = jnp.sum(jnp.where(cand >= tau, jnp.exp(cand - top), 0.0), axis=0, keepdims=True)
        e2 = jnp.exp(scores[1] - a2[0:1])
        c1 = jnp.exp(scores[0] - a1[0:1]) / (2.0 * z)
        theta = jnp.full(scores[0].shape, jnp.inf, F32)
        for q in range(PEER_TOPK):
            a2q = a2[q:q + 1]
            phi = jnp.min(jnp.where(a1 + a2q >= tau, a1, jnp.inf), axis=0, keepdims=True)
            theta = jnp.where(scores[0] >= phi, a2q, theta)
        for t in range(tb // LANES):
            ts = slice(t * LANES, (t + 1) * LANES)
            th_ref[hd, t] = theta[:, ts]
            s2_ref[hd, t] = scores[1][:, ts]
            e2_ref[hd, t] = e2[:, ts]
            c1_ref[hd, t] = c1[:, ts]
        return 0

    lax.fori_loop(0, PEER_HEADS, head, 0)


def _route(x, g, mod, wqt, keys, *, mod_base, rows_per_cond):
    n = x.shape[0]
    tb = ROUTE_TB
    ntg = tb // LANES
    hk = (PEER_HEADS, n // LANES, PEER_NKEYS, LANES)
    rspec = pl.BlockSpec((PEER_HEADS, ntg, PEER_NKEYS, LANES), lambda i: (0, i, 0, 0))
    return pl.pallas_call(
        _route_kernel,
        grid=(n // tb,),
        in_specs=[pl.BlockSpec((tb, D_MODEL), lambda i: (i, 0)), _full((1, D_MODEL)),
                  _mod_spec(4, tb, mod_base, rows_per_cond), _mod_spec(3, tb, mod_base, rows_per_cond),
                  _full(wqt.shape), _full(keys.shape)],
        out_specs=[pl.BlockSpec((tb, D_MODEL), lambda i: (i, 0)), rspec, rspec, rspec, rspec],
        out_shape=[jax.ShapeDtypeStruct((n, D_MODEL), BF16)] + [jax.ShapeDtypeStruct(hk, F32)] * 4,
        scratch_shapes=[pltpu.VMEM((PEER_HEADS * 2 * PEER_NKEYS, tb), BF16), pltpu.VMEM((2, PEER_TOPK, tb), F32)],
        compiler_params=_cp(1), name="peer_route",
    )(x, g.reshape(1, D_MODEL), mod, mod, wqt, keys)


PEER_TB = 512
PEER_EC = 1024
PEER_SUB = PEER_EC // 2


def _experts_kernel(hb_ref, th_ref, s2_ref, e2_ref, c1_ref, u0_ref, ub_ref, un_ref, vp_ref, va_ref, vl_ref,
                    x_ref, g2_ref, fg_ref, o_ref, acc_s, st0_s, st1_s, a0_s, a1_s, *, final_norm):
    e = pl.program_id(1)
    tb = hb_ref.shape[0]
    ntg = tb // LANES
    rows_per_sub = PEER_SUB // PEER_NKEYS

    def scores(u_blk, st_s, h):
        res = _dot_nt(u_blk[...], hb_ref[h * 2 * LANES:(h + 1) * 2 * LANES, :])
        st_s[2 * h] = res[:, :LANES]
        st_s[2 * h + 1] = res[:, LANES:]

    def values(vt_blk, act_s, h):
        act = jnp.concatenate([act_s[2 * h], act_s[2 * h + 1]], axis=1)
        hs = slice(h * 2 * LANES, (h + 1) * 2 * LANES)
        for r in range(0, D_MODEL, PEER_SUB):
            acc_s[r:r + PEER_SUB, hs] += _dot(vt_blk[r:r + PEER_SUB, :], act)

    def gates(c, tg, st_s, act_s):
        sub = 32
        for k in range(rows_per_sub):
            row = c * rows_per_sub + k
            for j0 in range(0, PEER_NKEYS, sub):
                js = slice(j0, j0 + sub)
                ks = slice(k * PEER_NKEYS + j0, k * PEER_NKEYS + j0 + sub)
                w = None
                for hd in range(PEER_HEADS):
                    chosen = s2_ref[hd, tg, js, :] >= th_ref[hd, tg, row:row + 1, :]
                    term = jnp.where(chosen, e2_ref[hd, tg, js, :], 0.0) * c1_ref[hd, tg, row:row + 1, :]
                    w = term if w is None else w + term
                s = st_s[tg, ks, :]
                act = s * (1.0 + lax.erf(s * np.float32(2.0 ** -0.5)))
                act_s[tg, ks, :] = (act * w).astype(BF16)

    @pl.when(e == 0)
    def _():
        acc_s[...] = jnp.zeros_like(acc_s)
        a1_s[...] = jnp.zeros_like(a1_s)
        scores(u0_ref, st0_s, 0)
        scores(u0_ref, st0_s, 1)

    def phase(c, st_cur, act_cur, u_next, st_next, vt_prev, act_prev):
        for h in range(2):
            scores(u_next, st_next, h)
            gates(c, 2 * h, st_cur, act_cur)
            values(vt_prev, act_prev, h)
            gates(c, 2 * h + 1, st_cur, act_cur)

    phase(0, st0_s, a0_s, ub_ref, st1_s, vp_ref, a1_s)
    phase(1, st1_s, a1_s, un_ref, st0_s, va_ref, a0_s)

    @pl.when(e == pl.num_programs(1) - 1)
    def _():
        values(vl_ref, a1_s, 0)
        values(vl_ref, a1_s, 1)
        y = x_ref[...] + g2_ref[0] * acc_s[...].T
        if final_norm:
            y = _rms(y) * fg_ref[...]
        o_ref[...] = y


def _experts(hb, theta, s2, e2, c1, u, vt, x, mod, fg, *, mod_base, rows_per_cond, final_norm):
    n = x.shape[0]
    tb = PEER_TB
    ntg = tb // LANES
    ne = PEER_N // PEER_EC
    once = dict(pipeline_mode=pl.Buffered(1))
    rspec = pl.BlockSpec((PEER_HEADS, ntg, PEER_NKEYS, LANES), lambda i, e: (0, i, 0, 0), **once)
    rows8 = pl.BlockSpec((PEER_HEADS, ntg, SUBLANES, LANES), lambda i, e: (0, i, e, 0))
    uspec = lambda index: pl.BlockSpec((PEER_SUB, D_MODEL), index)
    vspec = lambda index: pl.BlockSpec((D_MODEL, PEER_SUB), index)
    return pl.pallas_call(
        functools.partial(_experts_kernel, final_norm=final_norm),
        grid=(n // tb, ne),
        in_specs=[pl.BlockSpec((tb, D_MODEL), lambda i, e: (i, 0)), rows8, rspec, rspec, rows8,
                  pl.BlockSpec((PEER_SUB, D_MODEL), lambda i, e: (0, 0), **once),
                  uspec(lambda i, e: (2 * e + 1, 0)),
                  uspec(lambda i, e: (jnp.minimum(2 * e + 2, 2 * ne - 2), 0)),
                  vspec(lambda i, e: (0, jnp.maximum(2 * e - 1, 0))),
                  vspec(lambda i, e: (0, 2 * e)),
                  pl.BlockSpec((D_MODEL, PEER_SUB), lambda i, e: (0, 2 * ne - 1), **once),
                  pl.BlockSpec((tb, D_MODEL), lambda i, e: (i, 0)),
                  _mod_spec(5, tb, mod_base, rows_per_cond), _full((1, D_MODEL))],
        out_specs=pl.BlockSpec((tb, D_MODEL), lambda i, e: (i, 0)),
        out_shape=jax.ShapeDtypeStruct((n, D_MODEL), F32),
        scratch_shapes=[pltpu.VMEM((D_MODEL, tb), F32), pltpu.VMEM((ntg, PEER_SUB, LANES), F32),
                        pltpu.VMEM((ntg, PEER_SUB, LANES), F32), pltpu.VMEM((ntg, PEER_SUB, LANES), BF16),
                        pltpu.VMEM((ntg, PEER_SUB, LANES), BF16)],
        compiler_params=_cp(2), name="peer_experts",
    )(hb, theta, s2, e2, c1, u, u, u, vt, vt, vt, x, mod, fg.reshape(1, D_MODEL))


def _peer(x, g, mod, wqt, keys, u, vt, fg, *, mod_base, rows_per_cond, final_norm):
    hb, theta, s2, e2, c1 = _route(x, g, mod, wqt, keys, mod_base=mod_base, rows_per_cond=rows_per_cond)
    return _experts(hb, theta, s2, e2, c1, u, vt, x, mod, fg, mod_base=mod_base, rows_per_cond=rows_per_cond,
                    final_norm=final_norm)


def _chunk_index(s, nc):
    return jnp.where(s < nc, s, 2 * nc - 1 - s)


def _scan_group(nseq, largest):
    return max(g for g in (2, 4, 8) if g <= largest and nseq % g == 0)


def _scan_specs(nc):
    def cur(col):
        return lambda b, s: (b, _chunk_index(s, nc), col)

    def prev(col):
        return lambda b, s: (b, jnp.maximum(_chunk_index(s, nc) * 8 - 1, 0), col)

    def nxt(col):
        return lambda b, s: (b, jnp.minimum(_chunk_index(s, nc) * 8 + 8, nc * 8 - 1), col)

    def out(col):
        return lambda b, s: (b, jnp.where(s < nc, nc - 1, 2 * nc - 1 - s), col)
    return cur, prev, nxt, out


def _conv_silu(prev, cur, nxt, w_ref, b_ref, first, last):
    win = jnp.concatenate([jnp.where(first, 0.0, prev), cur, jnp.where(last, 0.0, nxt)], axis=1)
    acc = b_ref[...] + win[:, 6:70, :] * w_ref[0:1, :]
    for k in range(1, CONV_W):
        acc = acc + win[:, 6 + k:70 + k, :] * w_ref[k:k + 1, :]
    return _silu(acc)


def _tri(bwd, G):
    r = lax.broadcasted_iota(jnp.int32, (G, CHUNK, CHUNK), 1)
    c = lax.broadcasted_iota(jnp.int32, (G, CHUNK, CHUNK), 2)
    lower = (c <= r).astype(F32)
    upper = (c >= r).astype(F32)
    return jnp.where(bwd, upper, lower), jnp.where(bwd, lower, upper)


def _bdot(a, b, contract, precision=None):
    return lax.dot_general(a, b, (((contract[0],), (contract[1],)), ((0,), (0,))), preferred_element_type=F32,
                           precision=precision)


def _ssd_kernel(z_ref, xp_ref, xc_ref, xn_ref, bp_ref, bc_ref, bn_ref, sm_ref, dtt_ref,
                cwx_ref, cbx_ref, cwb_ref, cbb_ref, arow_ref, acol_ref, brow_ref, bcol_ref, drow_ref, ng_ref,
                *rest, nc, zero_init):
    st0_ref = None if zero_init else rest[0]
    y_ref, st_ref, yf_s, h_s = rest[0 if zero_init else 1:]
    s = pl.program_id(1)
    bwd = s >= nc
    cidx = _chunk_index(s, nc)
    first = cidx == 0
    last = cidx == nc - 1
    nh = SSD_HEADS

    def initial(direction):
        return jnp.zeros(h_s.shape, F32) if zero_init else st0_ref[:, direction]

    @pl.when(s == 0)
    def _():
        h_s[...] = initial(0)

    @pl.when(s == nc)
    def _():
        h_s[...] = initial(1)

    G = z_ref.shape[0]
    tsel, tsel_t = _tri(bwd, G)
    mask = tsel > 0.0
    r0 = pl.multiple_of(cidx * CHUNK, CHUNK)
    hi = lax.Precision.HIGHEST

    xs = _conv_silu(xp_ref[...], xc_ref[...], xn_ref[...], cwx_ref, cbx_ref, first, last)
    bcv = _conv_silu(bp_ref[...], bc_ref[...], bn_ref[...], cwb_ref, cbb_ref, first, last)
    dt_all = _softplus(sm_ref[:, :, 64:96] + brow_ref[...])
    dtt_all = _softplus(dtt_ref[:, 0] + bcol_ref[...])
    a_all = dt_all * (-jnp.exp(arow_ref[...]))
    at_all = dtt_all * (-jnp.exp(acol_ref[...]))
    dt = jnp.where(bwd, dt_all[:, :, nh:2 * nh], dt_all[:, :, 0:nh])
    a = jnp.where(bwd, a_all[:, :, nh:2 * nh], a_all[:, :, 0:nh])
    at = jnp.where(bwd, at_all[:, nh:2 * nh, :], at_all[:, 0:nh, :])
    cum = _bdot(tsel, a, (2, 1), hi)
    cum_t = _bdot(at, tsel_t, (2, 1), hi)
    tot = jnp.where(bwd, cum[:, 0:1, :], cum[:, CHUNK - 1:CHUNK, :])

    ys = []
    for g in range(2):
        bg = bcv[:, :, g * SSD_STATE:(g + 1) * SSD_STATE]
        cg = bcv[:, :, 256 + g * SSD_STATE:256 + (g + 1) * SSD_STATE]
        cg16 = cg.astype(BF16)
        cb = _bdot(cg16, bg.astype(BF16), (2, 2))
        for hh in range(nh // 2):
            h = g * (nh // 2) + hh
            cq = cum[:, :, h:h + 1]
            decay = jnp.exp(jnp.where(mask, cq - cum_t[:, h:h + 1, :], NEG_INF))
            xd = xs[:, :, h * SSD_HEAD_DIM:(h + 1) * SSD_HEAD_DIM] * dt[:, :, h:h + 1]
            xd16 = xd.astype(BF16)
            hprev = h_s[:, h]
            y = (_bdot((cb * decay).astype(BF16), xd16, (2, 1))
                 + _bdot(cg16, hprev.astype(BF16), (2, 2)) * jnp.exp(cq))
            ys.append(y)
            th = tot[:, :, h:h + 1]
            bw = bg * jnp.exp(th - cq)
            h_s[:, h] = hprev * jnp.exp(th) + _bdot(xd16, bw.astype(BF16), (1, 1))
    y = jnp.concatenate(ys, axis=2)

    @pl.when(jnp.logical_not(bwd))
    def _():
        yf_s[:, pl.ds(r0, CHUNK), :] = y

    @pl.when(bwd)
    def _():
        yt = (yf_s[:, pl.ds(r0, CHUNK), :] + y + drow_ref[...] * xs) * _silu(z_ref[...])
        y_ref[...] = _rms(yt) * ng_ref[...]

    @pl.when(jnp.logical_or(s == nc - 1, s == 2 * nc - 1))
    def _():
        st_ref[:, 0] = h_s[...]


def _ssd(p0, dtt, cwx, cbx, cwb, cbb, a_log, dt_bias, d_skip, norm_g, st0, *, L):
    nseq, nc = p0.shape[0], L // CHUNK
    zero_init = st0 is None
    G = _scan_group(nseq, 8 if zero_init else 4)
    cur, prev, nxt, out = _scan_specs(nc)
    st_shape = (G, 2, SSD_HEADS, SSD_HEAD_DIM, SSD_STATE)
    states = [] if zero_init else [st0]
    row = lambda v: v.reshape(1, -1)
    col = lambda v: v.reshape(-1, 1)
    small = [cwx, row(cbx), cwb, row(cbb), row(a_log), col(a_log), row(dt_bias), col(dt_bias),
             row(jnp.repeat(d_skip, SSD_HEAD_DIM)), row(norm_g)]
    blk = lambda rows, width, index: pl.BlockSpec((G, rows, width), index)
    return pl.pallas_call(
        functools.partial(_ssd_kernel, nc=nc, zero_init=zero_init),
        grid=(nseq // G, 2 * nc),
        in_specs=[blk(CHUNK, 1024, cur(0)),
                  blk(8, 1024, prev(1)), blk(CHUNK, 1024, cur(1)), blk(8, 1024, nxt(1)),
                  blk(8, 512, prev(4)), blk(CHUNK, 512, cur(4)), blk(8, 512, nxt(4)),
                  blk(CHUNK, LANES, cur(28)),
                  pl.BlockSpec((G, 1, 32, CHUNK), lambda b, s: (b, _chunk_index(s, nc), 0, 0))]
        + [_full(v.shape) for v in small]
        + [pl.BlockSpec(st_shape, lambda b, s: (b, 0, 0, 0, 0)) for _ in states],
        out_specs=[blk(CHUNK, SSD_D_INNER, out(0)),
                   pl.BlockSpec((G, 1) + st_shape[2:], lambda b, s: (b, jnp.where(s < nc, 0, 1), 0, 0, 0))],
        out_shape=[jax.ShapeDtypeStruct((nseq, L, SSD_D_INNER), F32),
                   jax.ShapeDtypeStruct((nseq,) + st_shape[1:], F32)],
        scratch_shapes=[pltpu.VMEM((G, L, SSD_D_INNER), F32), pltpu.VMEM(st_shape[:1] + st_shape[2:], F32)],
        compiler_params=_cp(2), name="ssd_mixer",
    )(p0, p0, p0, p0, p0, p0, p0, p0, dtt, *small, *states)


def _ssd_bidir_kernel(*refs, nc, zero_init):
    f_in, b_in = refs[0:9], refs[9:18]
    (cwx_ref, cbx_ref, cwb_ref, cbb_ref, arow_ref, acol_ref, brow_ref, bcol_ref, drow_ref, ng_ref) = refs[18:28]
    rest = refs[28:]
    st0_ref = None if zero_init else rest[0]
    ya_ref, yb_ref, st_ref, ys_s, h_s = rest[0 if zero_init else 1:]
    j = pl.program_id(1)
    nh = SSD_HEADS
    G = f_in[0].shape[0]
    hi = lax.Precision.HIGHEST

    @pl.when(j == 0)
    def _():
        if zero_init:
            h_s[...] = jnp.zeros(h_s.shape, F32)
        else:
            h_s[0:G] = st0_ref[:, 0]
            h_s[G:] = st0_ref[:, 1]

    def conv_half(refs9, first, last):
        _, xp, xc, xn, bp, bc, bn, _, _ = refs9
        return (_conv_silu(xp[...], xc[...], xn[...], cwx_ref, cbx_ref, first, last),
                _conv_silu(bp[...], bc[...], bn[...], cwb_ref, cbb_ref, first, last))

    xs_f, bc_f = conv_half(f_in, j == 0, j == nc - 1)
    xs_b, bc_b = conv_half(b_in, j == nc - 1, j == 0)
    xs = jnp.concatenate([xs_f, xs_b], axis=0)
    bcv = jnp.concatenate([bc_f, bc_b], axis=0)

    def gate_cols(refs9, lo):
        dt_all = _softplus(refs9[7][:, :, 64:96] + brow_ref[...])
        dtt_all = _softplus(refs9[8][:, 0] + bcol_ref[...])
        a_all = dt_all * (-jnp.exp(arow_ref[...]))
        at_all = dtt_all * (-jnp.exp(acol_ref[...]))
        return dt_all[:, :, lo:lo + nh], a_all[:, :, lo:lo + nh], at_all[:, lo:lo + nh, :]

    dt_f, a_f, at_f = gate_cols(f_in, 0)
    dt_b, a_b, at_b = gate_cols(b_in, nh)
    dt = jnp.concatenate([dt_f, dt_b], axis=0)
    a = jnp.concatenate([a_f, a_b], axis=0)
    at = jnp.concatenate([at_f, at_b], axis=0)

    shape = (2 * G, CHUNK, CHUNK)
    ent = lax.broadcasted_iota(jnp.int32, shape, 0)
    r = lax.broadcasted_iota(jnp.int32, shape, 1)
    c = lax.broadcasted_iota(jnp.int32, shape, 2)
    fwd = ent < G
    lower = (c <= r).astype(F32)
    upper = (c >= r).astype(F32)
    tsel = jnp.where(fwd, lower, upper)
    tsel_t = jnp.where(fwd, upper, lower)
    mask = tsel > 0.0
    cum = _bdot(tsel, a, (2, 1), hi)
    cum_t = _bdot(at, tsel_t, (2, 1), hi)
    tot = jnp.concatenate([cum[0:G, CHUNK - 1:CHUNK, :], cum[G:, 0:1, :]], axis=0)

    ys = []
    for g in range(2):
        bg = bcv[:, :, g * SSD_STATE:(g + 1) * SSD_STATE]
        cg = bcv[:, :, 256 + g * SSD_STATE:256 + (g + 1) * SSD_STATE]
        cg16 = cg.astype(BF16)
        cb = _bdot(cg16, bg.astype(BF16), (2, 2))
        for hh in range(nh // 2):
            h = g * (nh // 2) + hh
            cq = cum[:, :, h:h + 1]
            decay = jnp.exp(jnp.where(mask, cq - cum_t[:, h:h + 1, :], NEG_INF))
            xd = xs[:, :, h * SSD_HEAD_DIM:(h + 1) * SSD_HEAD_DIM] * dt[:, :, h:h + 1]
            xd16 = xd.astype(BF16)
            hprev = h_s[:, h]
            y = (_bdot((cb * decay).astype(BF16), xd16, (2, 1))
                 + _bdot(cg16, hprev.astype(BF16), (2, 2)) * jnp.exp(cq))
            ys.append(y)
            th = tot[:, :, h:h + 1]
            bw = bg * jnp.exp(th - cq)
            h_s[:, h] = hprev * jnp.exp(th) + _bdot(xd16, bw.astype(BF16), (1, 1))
    y = jnp.concatenate(ys, axis=2)
    rf = pl.multiple_of(j * CHUNK, CHUNK)
    rb = pl.multiple_of((nc - 1 - j) * CHUNK, CHUNK)

    @pl.when(j < nc // 2)
    def _():
        ys_s[:, pl.ds(rf, CHUNK), :] = y[0:G]
        ys_s[:, pl.ds(rb, CHUNK), :] = y[G:]

    @pl.when(j >= nc // 2)
    def _():
        def finish(now, stored, x_half, z_ref):
            yt = (stored + now + drow_ref[...] * x_half) * _silu(z_ref[...])
            return _rms(yt) * ng_ref[...]
        ya_ref[...] = finish(y[0:G], ys_s[:, pl.ds(rf, CHUNK), :], xs_f, f_in[0])
        yb_ref[...] = finish(y[G:], ys_s[:, pl.ds(rb, CHUNK), :], xs_b, b_in[0])

    @pl.when(j == nc - 1)
    def _():
        st_ref[:, 0] = h_s[0:G]
        st_ref[:, 1] = h_s[G:]


def _ssd_bidir(p0, dtt, cwx, cbx, cwb, cbb, a_log, dt_bias, d_skip, norm_g, st0, *, L):
    nseq, nc = p0.shape[0], L // CHUNK
    zero_init = st0 is None
    G = _scan_group(nseq, 4)
    st_shape = (G, 2, SSD_HEADS, SSD_HEAD_DIM, SSD_STATE)
    states = [] if zero_init else [st0]
    row = lambda v: v.reshape(1, -1)
    col = lambda v: v.reshape(-1, 1)
    small = [cwx, row(cbx), cwb, row(cbb), row(a_log), col(a_log), row(dt_bias), col(dt_bias),
             row(jnp.repeat(d_skip, SSD_HEAD_DIM)), row(norm_g)]

    def specs(chunk):
        cur = lambda colb: (lambda b, j: (b, chunk(j), colb))
        prev = lambda colb: (lambda b, j: (b, jnp.maximum(chunk(j) * 8 - 1, 0), colb))
        nxt = lambda colb: (lambda b, j: (b, jnp.minimum(chunk(j) * 8 + 8, nc * 8 - 1), colb))
        blk = lambda rows, width, index: pl.BlockSpec((G, rows, width), index)
        return [blk(CHUNK, 1024, cur(0)),
                blk(8, 1024, prev(1)), blk(CHUNK, 1024, cur(1)), blk(8, 1024, nxt(1)),
                blk(8, 512, prev(4)), blk(CHUNK, 512, cur(4)), blk(8, 512, nxt(4)),
                blk(CHUNK, LANES, cur(28)),
                pl.BlockSpec((G, 1, 32, CHUNK), lambda b, j: (b, chunk(j), 0, 0))]

    half = nc // 2
    y_sds = jax.ShapeDtypeStruct((nseq, L, SSD_D_INNER), F32)
    ya, yb, st = pl.pallas_call(
        functools.partial(_ssd_bidir_kernel, nc=nc, zero_init=zero_init),
        grid=(nseq // G, nc),
        in_specs=specs(lambda j: j) + specs(lambda j: nc - 1 - j) + [_full(v.shape) for v in small]
        + [pl.BlockSpec(st_shape, lambda b, j: (b, 0, 0, 0, 0)) for _ in states],
        out_specs=[pl.BlockSpec((G, CHUNK, SSD_D_INNER), lambda b, j: (b, jnp.maximum(j, half), 0)),
                   pl.BlockSpec((G, CHUNK, SSD_D_INNER), lambda b, j: (b, jnp.minimum(nc - 1 - j, half - 1), 0)),
                   pl.BlockSpec(st_shape, lambda b, j: (b, 0, 0, 0, 0))],
        out_shape=[y_sds, y_sds, jax.ShapeDtypeStruct((nseq,) + st_shape[1:], F32)],
        scratch_shapes=[pltpu.VMEM((G, L, SSD_D_INNER), F32), pltpu.VMEM((2 * G,) + st_shape[2:], F32)],
        compiler_params=_cp(2), name="ssd_mixer",
    )(*([p0] * 8 + [dtt]) * 2, *small, *states)
    return jnp.concatenate([yb[:, :L // 2], ya[:, L // 2:]], axis=1), st


def _rope_tables(L):
    t = jnp.arange(L)
    r = (t // GRID_W).astype(F32)
    c = (t % GRID_W).astype(F32)
    nf = 16
    inv = ROPE_BASE ** (-jnp.arange(nf, dtype=F32) / nf)
    ang = jnp.concatenate([r[:, None] * inv, c[:, None] * inv], axis=-1)
    ang = jnp.concatenate([ang, ang], axis=-1)
    return jnp.cos(ang), jnp.sin(ang)


def _rope128(x, cos, sin):
    lane = lax.broadcasted_iota(jnp.int32, x.shape, 1)
    rot = jnp.where(lane % 64 < 32, -pltpu.roll(x, 96, 1), pltpu.roll(x, 32, 1))
    return x * cos + rot * sin


def _pad_lanes(x):
    return jnp.concatenate([x, jnp.zeros_like(x)], axis=1)


MLA_QB = 256


def _mla_kernel(*refs, L, Lc, rope, emit):
    refs = list(refs)
    qa_ref, kva_ref, sm_ref, qg_ref, kvg_ref, wqn_ref, wqr_ref, wkk_ref, wkv_ref = refs[:9]
    pos = 9
    if Lc:
        cckv_ref, ckpe_ref = refs[pos:pos + 2]
        pos += 2
    if rope:
        cos_ref, sin_ref = refs[pos:pos + 2]
        pos += 2
    y_ref = refs[pos]
    pos += 1
    if emit:
        ckv_ref = refs[pos]
        pos += 1
    qn_s, qr_s, kn_s, v_s, kpe_s = refs[pos:]
    qb = pl.program_id(1)
    scale = np.float32((MLA_NOPE + MLA_ROPE) ** -0.5)

    @pl.when(qb == 0)
    def _():
        ckv = _rms(kva_ref[...]) * kvg_ref[...]
        if emit:
            ckv_ref[...] = ckv
        kpe = _pad_lanes(sm_ref[:, 0:MLA_ROPE])
        if rope:
            kpe = _rope128(kpe, cos_ref[...], sin_ref[...])
        if Lc:
            ckv = jnp.concatenate([cckv_ref[0], ckv], axis=0)
            kpe = jnp.concatenate([_pad_lanes(ckpe_ref[0]), kpe], axis=0)
        c16 = ckv.astype(BF16)
        kn_s[...] = _dot(c16, wkk_ref[...]).astype(BF16)
        v_s[...] = _dot(c16, wkv_ref[...]).astype(BF16)
        kpe_s[...] = kpe.astype(BF16)

    qn = (_rms(qa_ref[...]) * qg_ref[...]).astype(BF16)
    qn_s[...] = _dot(qn, wqn_ref[...]).astype(BF16)
    qr = _dot(qn, wqr_ref[...])
    if rope:
        q0 = pl.multiple_of(qb * MLA_QB, MLA_QB)
        cos = cos_ref[pl.ds(q0, MLA_QB), :]
        sin = sin_ref[pl.ds(q0, MLA_QB), :]
        for h in range(MLA_HEADS):
            qr_s[:, h * LANES:(h + 1) * LANES] = _rope128(qr[:, h * LANES:(h + 1) * LANES], cos, sin).astype(BF16)
    else:
        qr_s[...] = qr.astype(BF16)

    def head(h, _):
        c0 = h * LANES if isinstance(h, int) else pl.multiple_of(h * LANES, LANES)
        s = (_dot_nt(qn_s[:, pl.ds(c0, LANES)], kn_s[:, pl.ds(c0, LANES)])
             + _dot_nt(qr_s[:, pl.ds(c0, LANES)], kpe_s[...])) * scale
        p = jnp.exp(s - jnp.max(s, axis=-1, keepdims=True))
        p = p / jnp.sum(p, axis=-1, keepdims=True)
        y_ref[:, pl.ds(c0, LANES)] = _dot(p.astype(BF16), v_s[:, pl.ds(c0, LANES)])
        return 0

    if L + Lc <= 512:
        for h in range(MLA_HEADS):
            head(h, 0)
    else:
        lax.fori_loop(0, MLA_HEADS, head, 0)


def _mla(p0, qg, kvg, wqn, wqr, wkk, wkv, cache_ckv, cache_kpe, rope_tabs, *, L):
    n = p0.shape[0]
    nseq, nqb = n // L, L // MLA_QB
    Lc = 0 if cache_ckv is None else cache_ckv.shape[1]
    rope = rope_tabs is not None
    emit = cache_ckv is None
    Lk = L + Lc
    args = [p0, p0, p0, qg.reshape(1, -1), kvg.reshape(1, -1), wqn, wqr, wkk, wkv]
    specs = [pl.BlockSpec((MLA_QB, MLA_RANK), lambda b, q: (b * nqb + q, 5)),
             pl.BlockSpec((L, MLA_RANK), lambda b, q: (b, 6)),
             pl.BlockSpec((L, LANES), lambda b, q: (b, 28)),
             _full((1, MLA_RANK)), _full((1, MLA_RANK)), _full(wqn.shape), _full(wqr.shape), _full(wkk.shape),
             _full(wkv.shape)]
    if Lc:
        args += [cache_ckv, cache_kpe]
        specs += [pl.BlockSpec((1, Lc, MLA_RANK), lambda b, q: (b, 0, 0)),
                  pl.BlockSpec((1, Lc, MLA_ROPE), lambda b, q: (b, 0, 0))]
    if rope:
        args += list(rope_tabs)
        specs += [_full((L, LANES)), _full((L, LANES))]
    out_specs = [pl.BlockSpec((MLA_QB, 1024), lambda b, q: (b * nqb + q, 0))]
    out_shape = [jax.ShapeDtypeStruct((n, 1024), F32)]
    if emit:
        out_specs.append(pl.BlockSpec((L, MLA_RANK), lambda b, q: (b, 0)))
        out_shape.append(jax.ShapeDtypeStruct((n, MLA_RANK), F32))
    return pl.pallas_call(
        functools.partial(_mla_kernel, L=L, Lc=Lc, rope=rope, emit=emit),
        grid=(nseq, nqb), in_specs=specs, out_specs=out_specs, out_shape=out_shape,
        scratch_shapes=[pltpu.VMEM((MLA_QB, 1024), BF16), pltpu.VMEM((MLA_QB, 1024), BF16),
                        pltpu.VMEM((Lk, 1024), BF16), pltpu.VMEM((Lk, 1024), BF16), pltpu.VMEM((Lk, LANES), BF16)],
        compiler_params=_cp(2), name="mla_attention",
    )(*args)


def _gqa_ctx_kernel(q_ref, k_ref, v_ref, sink_ref, y_ref):
    q = q_ref[...]
    k = k_ref[...].astype(BF16)
    v = v_ref[...].astype(BF16)
    scale = np.float32(GQA_HEAD_DIM ** -0.5)
    group = GQA_HEADS // GQA_KV_HEADS
    outs = []
    for h in range(GQA_HEADS):
        kh = h // group
        ks = slice(kh * GQA_HEAD_DIM, (kh + 1) * GQA_HEAD_DIM)
        s = _dot_nt(q[:, h * GQA_HEAD_DIM:(h + 1) * GQA_HEAD_DIM].astype(BF16), k[:, ks]) * scale
        m = jnp.maximum(jnp.max(s, axis=-1, keepdims=True), sink_ref[h])
        p = jnp.exp(s - m)
        p = p / (jnp.sum(p, axis=-1, keepdims=True) + jnp.exp(sink_ref[h] - m))
        outs.append(_dot(p.astype(BF16), v[:, ks]))
    y_ref[...] = jnp.concatenate(outs, axis=1)


def _gqa_ctx(p1, sink, *, L):
    n = p1.shape[0]
    return pl.pallas_call(
        _gqa_ctx_kernel,
        grid=(n // L,),
        in_specs=[pl.BlockSpec((L, 1024), lambda b: (b, 2)), pl.BlockSpec((L, 256), lambda b: (b, 20)),
                  pl.BlockSpec((L, 256), lambda b: (b, 21)), pl.BlockSpec(memory_space=pltpu.SMEM)],
        out_specs=pl.BlockSpec((L, 1024), lambda b: (b, 0)),
        out_shape=jax.ShapeDtypeStruct((n, 1024), F32),
        compiler_params=_cp(1), name="gqa_context",
    )(p1, p1, p1, sink)


GQA_QB = 128
GQA_SPAN = GQA_QB + 2 * WINDOW


def _gqa_win_kernel(q_ref, k_ref, v_ref, ck_ref, cv_ref, cos_ref, sin_ref, sink_ref, y_ref, kp_s, vp_s, *, L):
    qb = pl.program_id(1)
    scale = np.float32(GQA_HEAD_DIM ** -0.5)
    group = GQA_HEADS // GQA_KV_HEADS

    @pl.when(qb == 0)
    def _():
        zeros = jnp.zeros((WINDOW, 256), BF16)
        kp_s[0:WINDOW, :] = zeros
        kp_s[WINDOW + L:, :] = zeros
        vp_s[0:WINDOW, :] = zeros
        vp_s[WINDOW + L:, :] = zeros
        for j in range(2):
            cs = slice(j * LANES, (j + 1) * LANES)
            kp_s[WINDOW:WINDOW + L, cs] = _rope128(k_ref[:, cs], cos_ref[...], sin_ref[...]).astype(BF16)
        vp_s[WINDOW:WINDOW + L, :] = v_ref[...].astype(BF16)

    q0 = pl.multiple_of(qb * GQA_QB, GQA_QB)
    cos = cos_ref[pl.ds(q0, GQA_QB), :]
    sin = sin_ref[pl.ds(q0, GQA_QB), :]
    kw = kp_s[pl.ds(q0, GQA_SPAN), :]
    vw = vp_s[pl.ds(q0, GQA_SPAN), :]
    kc = ck_ref[0].astype(BF16)
    vc = cv_ref[0].astype(BF16)
    qpos = q0 + lax.broadcasted_iota(jnp.int32, (GQA_QB, GQA_SPAN), 0)
    kpos = q0 - WINDOW + lax.broadcasted_iota(jnp.int32, (GQA_QB, GQA_SPAN), 1)
    bias = jnp.where(kpos < 0, NEG_INF, jnp.where(kpos >= L, NEG_INF, jnp.where(jnp.abs(qpos - kpos) <= WINDOW, 0.0, NEG_INF)))
    outs = []
    for j in range(GQA_HEADS * GQA_HEAD_DIM // LANES):
        qj = _rope128(q_ref[:, j * LANES:(j + 1) * LANES], cos, sin).astype(BF16)
        for half in range(2):
            h = 2 * j + half
            kh = h // group
            ks = slice(kh * GQA_HEAD_DIM, (kh + 1) * GQA_HEAD_DIM)
            qh = qj[:, half * GQA_HEAD_DIM:(half + 1) * GQA_HEAD_DIM]
            s_loc = _dot_nt(qh, kw[:, ks]) * scale + bias
            s_ctx = _dot_nt(qh, kc[:, ks]) * scale
            m = jnp.maximum(jnp.maximum(jnp.max(s_loc, axis=-1, keepdims=True), jnp.max(s_ctx, axis=-1, keepdims=True)),
                            sink_ref[h])
            p_loc = jnp.exp(s_loc - m)
            p_ctx = jnp.exp(s_ctx - m)
            inv = 1.0 / (jnp.sum(p_loc, axis=-1, keepdims=True) + jnp.sum(p_ctx, axis=-1, keepdims=True)
                         + jnp.exp(sink_ref[h] - m))
            outs.append(_dot((p_loc * inv).astype(BF16), vw[:, ks]) + _dot((p_ctx * inv).astype(BF16), vc[:, ks]))
    y_ref[...] = jnp.concatenate(outs, axis=1)


def _gqa_win(p1, cache_k, cache_v, cos, sin, sink, *, L):
    n = p1.shape[0]
    nseq, nqb = n // L, L // GQA_QB
    Lc = cache_k.shape[1]
    return pl.pallas_call(
        functools.partial(_gqa_win_kernel, L=L),
        grid=(nseq, nqb),
        in_specs=[pl.BlockSpec((GQA_QB, 1024), lambda b, q: (b * nqb + q, 2)),
                  pl.BlockSpec((L, 256), lambda b, q: (b, 20)), pl.BlockSpec((L, 256), lambda b, q: (b, 21)),
                  pl.BlockSpec((1, Lc, 256), lambda b, q: (b, 0, 0)), pl.BlockSpec((1, Lc, 256), lambda b, q: (b, 0, 0)),
                  _full((L, LANES)), _full((L, LANES)), pl.BlockSpec(memory_space=pltpu.SMEM)],
        out_specs=pl.BlockSpec((GQA_QB, 1024), lambda b, q: (b * nqb + q, 0)),
        out_shape=jax.ShapeDtypeStruct((n, 1024), F32),
        scratch_shapes=[pltpu.VMEM((L + 2 * WINDOW, 256), BF16), pltpu.VMEM((L + 2 * WINDOW, 256), BF16)],
        compiler_params=_cp(2), name="gqa_window",
    )(p1, p1, p1, cache_k, cache_v, cos, sin, sink)


def _mlstm_kernel(qp_ref, qc_ref, qn_ref, v_ref, o_ref, sm_ref, gt_ref, cw_ref, cb_ref, brow_ref, bcol_ref, ng_ref,
                  *rest, nc, zero_init):
    init_refs = () if zero_init else rest[:3]
    y_ref, cst_ref, nst_ref, mst_ref, hf_s, c_s, n_s, m_s = rest[len(init_refs):]
    s = pl.program_id(1)
    bwd = s >= nc
    cidx = _chunk_index(s, nc)
    nh = ML_HEADS
    hd = ML_HEAD_DIM

    def load_initial(direction):
        for k, dst in enumerate((c_s, n_s, m_s)):
            dst[...] = jnp.zeros(dst.shape, F32) if zero_init else init_refs[k][:, direction]

    @pl.when(s == 0)
    def _():
        load_initial(0)

    @pl.when(s == nc)
    def _():
        cst_ref[:, 0] = c_s[...]
        nst_ref[:, 0] = n_s[...]
        mst_ref[:, 0] = m_s[...]
        load_initial(1)

    G = qc_ref.shape[0]
    tsel, tsel_t = _tri(bwd, G)
    mask = tsel > 0.0
    kscale = np.float32(hd ** -0.5)
    r0 = pl.multiple_of(cidx * CHUNK, CHUNK)
    hi = lax.Precision.HIGHEST

    qk = _conv_silu(qp_ref[...], qc_ref[...], qn_ref[...], cw_ref, cb_ref, cidx == 0, cidx == nc - 1)
    gates = sm_ref[:, :, 0:4 * nh] + brow_ref[...]
    gates_t = gt_ref[:, 0] + bcol_ref[...]
    li = jnp.where(bwd, gates[:, :, nh:2 * nh], gates[:, :, 0:nh])
    li_t = jnp.where(bwd, gates_t[:, nh:2 * nh, :], gates_t[:, 0:nh, :])
    lf = -_softplus(-jnp.where(bwd, gates[:, :, 3 * nh:4 * nh], gates[:, :, 2 * nh:3 * nh]))
    lf_t = -_softplus(-jnp.where(bwd, gates_t[:, 3 * nh:4 * nh, :], gates_t[:, 2 * nh:3 * nh, :]))
    bc = _bdot(tsel, lf, (2, 1), hi)
    bc_t = _bdot(lf_t, tsel_t, (2, 1), hi)
    tot = jnp.where(bwd, bc[:, 0:1, :], bc[:, CHUNK - 1:CHUNK, :])

    hs = []
    for h in range(nh):
        cs = slice(h * hd, (h + 1) * hd)
        q = qk[:, :, cs]
        k = qk[:, :, ML_D + h * hd:ML_D + (h + 1) * hd] * kscale
        q16, k16 = q.astype(BF16), k.astype(BF16)
        v = v_ref[:, :, cs]
        bq = bc[:, :, h:h + 1]
        m_prev = m_s[:, h:h + 1, 0:1]
        log_d = jnp.where(mask, bq - bc_t[:, h:h + 1, :] + li_t[:, h:h + 1, :], NEG_INF)
        log_inter = bq + m_prev
        m_out = jnp.maximum(log_inter, jnp.max(log_d, axis=2, keepdims=True))
        sd = _bdot(q16, k16, (2, 2)) * jnp.exp(log_d - m_out)
        w_inter = jnp.exp(log_inter - m_out)
        c_prev = c_s[:, h]
        n_prev = n_s[:, h:h + 1, :]
        num = _bdot(sd.astype(BF16), v.astype(BF16), (2, 1)) + w_inter * _bdot(q16, c_prev.astype(BF16), (2, 2))
        den = jnp.sum(sd, axis=2, keepdims=True) + w_inter * jnp.sum(q * n_prev, axis=2, keepdims=True)
        hs.append(num / jnp.maximum(jnp.abs(den), jnp.exp(-m_out)))
        th = tot[:, :, h:h + 1]
        end_inter = th + m_prev
        end_intra = th - bq + li[:, :, h:h + 1]
        m_new = jnp.maximum(end_inter, jnp.max(end_intra, axis=1, keepdims=True))
        w_c = jnp.exp(end_inter - m_new)
        w_k = jnp.exp(end_intra - m_new)
        c_s[:, h] = w_c * c_prev + _bdot((v * w_k).astype(BF16), k16, (1, 1))
        n_s[:, h:h + 1, :] = w_c * n_prev + jnp.sum(k * w_k, axis=1, keepdims=True)
        m_s[:, h:h + 1, :] = jnp.broadcast_to(m_new, (G, 1, hd))
    hcat = jnp.concatenate(hs, axis=2)

    @pl.when(jnp.logical_not(bwd))
    def _():
        hf_s[:, pl.ds(r0, CHUNK), :] = hcat

    @pl.when(bwd)
    def _():
        tot_h = hf_s[:, pl.ds(r0, CHUNK), :] + hcat
        normed = jnp.concatenate([_rms(tot_h[:, :, h * hd:(h + 1) * hd]) for h in range(nh)], axis=2) * ng_ref[...]
        y_ref[...] = normed * jax.nn.sigmoid(o_ref[...])

    @pl.when(s == 2 * nc - 1)
    def _():
        cst_ref[:, 1] = c_s[...]
        nst_ref[:, 1] = n_s[...]
        mst_ref[:, 1] = m_s[...]


def _mlstm(p1, gt, conv_w, conv_b, ig_b, fg_b, norm_g, c0, n0, m0, *, L):
    nseq, nc = p1.shape[0], L // CHUNK
    zero_init = c0 is None
    G = _scan_group(nseq, 8 if zero_init else 4)
    cur, prev, nxt, out = _scan_specs(nc)
    cshape = (G, 2, ML_HEADS, ML_HEAD_DIM, ML_HEAD_DIM)
    nshape = (G, 2, ML_HEADS, ML_HEAD_DIM)
    states = [] if zero_init else [c0, n0, m0]
    gb = jnp.concatenate([ig_b.reshape(-1), fg_b.reshape(-1)])
    small = [conv_w, conv_b.reshape(1, -1), gb.reshape(1, -1), gb.reshape(-1, 1), norm_g.reshape(1, -1)]
    st = lambda shape: pl.BlockSpec(shape, lambda b, s: (b,) + (0,) * (len(shape) - 1))
    blk = lambda rows, width, index: pl.BlockSpec((G, rows, width), index)
    return pl.pallas_call(
        functools.partial(_mlstm_kernel, nc=nc, zero_init=zero_init),
        grid=(nseq // G, 2 * nc),
        in_specs=[blk(8, 2048, prev(0)), blk(CHUNK, 2048, cur(0)), blk(8, 2048, nxt(0)),
                  blk(CHUNK, 1024, cur(3)), blk(CHUNK, 1024, cur(4)), blk(CHUNK, LANES, cur(44)),
                  pl.BlockSpec((G, 1, 32, CHUNK), lambda b, s: (b, _chunk_index(s, nc), 0, 0))]
        + [_full(v.shape) for v in small] + ([] if zero_init else [st(cshape), st(nshape), st(nshape)]),
        out_specs=[blk(CHUNK, ML_D, out(0)), st(cshape), st(nshape), st(nshape)],
        out_shape=[jax.ShapeDtypeStruct((nseq, L, ML_D), F32), jax.ShapeDtypeStruct((nseq,) + cshape[1:], F32),
                   jax.ShapeDtypeStruct((nseq,) + nshape[1:], F32), jax.ShapeDtypeStruct((nseq,) + nshape[1:], F32)],
        scratch_shapes=[pltpu.VMEM((G, L, ML_D), F32), pltpu.VMEM(cshape[:1] + cshape[2:], F32),
                        pltpu.VMEM(nshape[:1] + nshape[2:], F32), pltpu.VMEM(nshape[:1] + nshape[2:], F32)],
        compiler_params=_cp(2), name="mlstm_mixer",
    )(p1, p1, p1, p1, p1, p1, gt, *small, *states)


def _prep_layer0(w_in, w_qb, w_kvb, w_out):
    pad = jnp.zeros((D_MODEL, L0_COLS - 3680), F32)
    w = jnp.concatenate([w_in[:, :2560], w_in[:, 2592:3680], w_in[:, 2560:2592], pad], axis=1).astype(BF16)
    q3 = w_qb.reshape(MLA_RANK, MLA_HEADS, MLA_NOPE + MLA_ROPE)
    wqn = q3[:, :, :MLA_NOPE].reshape(MLA_RANK, -1).astype(BF16)
    wqr = jnp.concatenate([q3[:, :, MLA_NOPE:], jnp.zeros((MLA_RANK, MLA_HEADS, LANES - MLA_ROPE), F32)], axis=-1)
    wqr = wqr.reshape(MLA_RANK, -1).astype(BF16)
    k3 = w_kvb.reshape(MLA_RANK, MLA_HEADS, MLA_NOPE + MLA_V)
    wkk = k3[:, :, :MLA_NOPE].reshape(MLA_RANK, -1).astype(BF16)
    wkv = k3[:, :, MLA_NOPE:].reshape(MLA_RANK, -1).astype(BF16)
    return w, wqn, wqr, wkk, wkv, w_out.astype(BF16)


def _prep_layer1(w_in, w_out):
    pad = jnp.zeros((D_MODEL, L1_COLS - 5664), F32)
    w = jnp.concatenate([w_in[:, 1536:3584], w_in[:, 0:1024], w_in[:, 3584:5632], w_in[:, 1024:1536],
                         w_in[:, 5632:5664], pad], axis=1).astype(BF16)
    return w, w_out.astype(BF16)


def _prep_peer(wq, keys, u, v):
    return (wq.T.astype(BF16), keys.reshape(PEER_HEADS * 2, PEER_NKEYS, -1).astype(BF16), u.astype(BF16),
            v.T.astype(BF16))


def _chunk_transposed(cols, L):
    return cols.reshape(-1, L // CHUNK, CHUNK, cols.shape[1]).transpose(0, 1, 3, 2)


def _trunk(x, L, mod0, mod1, w0, w1, peer0, peer1, p, ctx, rope_tabs, mod_base, rows_per_cond, fg):
    kw = dict(mod_base=mod_base, rows_per_cond=rows_per_cond)
    w_in0, wqn, wqr, wkk, wkv, w_out0 = w0
    w_in1, w_out1 = w1
    nseq = x.shape[0] // L

    p0 = _proj(x, p['l0_norm1_g'], mod0, w_in0, tn=1280, **kw)
    cw, cb = p['l0_ssd_conv_w'], p['l0_ssd_conv_b']
    ssd = _ssd if nseq >= 8 else _ssd_bidir
    y_ssd, ssd_state = ssd(p0.reshape(nseq, L, -1), _chunk_transposed(p0[:, 3648:3680], L), cw[:, :1024], cb[:1024],
                           cw[:, 1024:], cb[1024:], p['l0_ssd_A_log'], p['l0_ssd_dt_bias'], p['l0_ssd_D'],
                           p['l0_ssd_norm_g'], None if ctx is None else ctx[0], L=L)
    y_ssd = y_ssd.reshape(nseq * L, -1)
    mla_tabs = None
    if rope_tabs is not None:
        z64 = jnp.zeros_like(rope_tabs[0])
        mla_tabs = (jnp.concatenate([rope_tabs[0], z64], axis=1), jnp.concatenate([rope_tabs[1], z64], axis=1))
    mla_out = _mla(p0, p['l0_mla_q_norm_g'], p['l0_mla_kv_norm_g'], wqn, wqr, wkk, wkv,
                   None if ctx is None else ctx[1], None if ctx is None else ctx[2], mla_tabs, L=L)
    x = _outproj(y_ssd, mla_out[0], w_out0, x, mod0, **kw)
    x = _peer(x, p['l0_norm2_g'], mod0, *peer0, fg, final_norm=False, **kw)

    p1 = _proj(x, p['l1_norm1_g'], mod1, w_in1, tn=1152, **kw)
    if ctx is None:
        y_gqa = _gqa_ctx(p1, p['l1_gqa_sink'], L=L)
        c0 = n0 = m0 = None
    else:
        gqa_tabs = tuple(jnp.concatenate([t, t], axis=1) for t in rope_tabs)
        y_gqa = _gqa_win(p1, ctx[3].reshape(nseq, -1, 256), ctx[4].reshape(nseq, -1, 256), *gqa_tabs,
                         p['l1_gqa_sink'], L=L)
        c0, n0 = ctx[5], ctx[6]
        m0 = jnp.broadcast_to(ctx[7][..., None], ctx[7].shape + (ML_HEAD_DIM,))
    y_ml, mc, mn, mm = _mlstm(p1.reshape(nseq, L, -1), _chunk_transposed(p1[:, 5632:5664], L), p['l1_ml_conv_w'],
                              p['l1_ml_conv_b'], p['l1_ml_ig_b'], p['l1_ml_fg_b'], p['l1_ml_norm_g'], c0, n0, m0, L=L)
    y_ml = y_ml.reshape(nseq * L, -1)
    x = _outproj(y_gqa, y_ml, w_out1, x, mod1, **kw)
    y = _peer(x, p['l1_norm2_g'], mod1, *peer1, fg, final_norm=True, **kw)
    new = None
    if ctx is None:
        new = (ssd_state, mla_out[1].reshape(nseq, L, MLA_RANK), p0[:, 3584:3648].reshape(nseq, L, MLA_ROPE),
               p1[:, 5120:5376].reshape(nseq, L, GQA_KV_HEADS, GQA_HEAD_DIM),
               p1[:, 5376:5632].reshape(nseq, L, GQA_KV_HEADS, GQA_HEAD_DIM), mc, mn, mm[..., 0])
    return y, new


def kernel(x_prompt, x_sample, state_l0_ssd, cache_l0_mla_ckv, cache_l0_mla_kpe, cache_l1_gqa_k, cache_l1_gqa_v, state_l1_mlstm_C, state_l1_mlstm_n, state_l1_mlstm_m, c, c_ctx, final_norm_g, l0_ada_w, l0_ada_b, l0_norm1_g, l0_norm2_g, l0_w_in, l0_ssd_conv_w, l0_ssd_conv_b, l0_ssd_A_log, l0_ssd_dt_bias, l0_ssd_D, l0_ssd_norm_g, l0_mla_q_norm_g, l0_mla_w_qb, l0_mla_kv_norm_g, l0_mla_w_kvb, l0_w_out, l0_peer_wq, l0_peer_keys, l0_peer_u, l0_peer_v, l1_ada_w, l1_ada_b, l1_norm1_g, l1_norm2_g, l1_w_in, l1_gqa_sink, l1_ml_conv_w, l1_ml_conv_b, l1_ml_ig_b, l1_ml_fg_b, l1_ml_norm_g, l1_w_out, l1_peer_wq, l1_peer_keys, l1_peer_u, l1_peer_v):
    p = dict(l0_norm1_g=l0_norm1_g, l0_norm2_g=l0_norm2_g, l0_ssd_conv_w=l0_ssd_conv_w, l0_ssd_conv_b=l0_ssd_conv_b,
             l0_ssd_A_log=l0_ssd_A_log, l0_ssd_dt_bias=l0_ssd_dt_bias, l0_ssd_D=l0_ssd_D, l0_ssd_norm_g=l0_ssd_norm_g,
             l0_mla_q_norm_g=l0_mla_q_norm_g, l0_mla_kv_norm_g=l0_mla_kv_norm_g, l1_norm1_g=l1_norm1_g,
             l1_norm2_g=l1_norm2_g, l1_gqa_sink=l1_gqa_sink, l1_ml_conv_w=l1_ml_conv_w, l1_ml_conv_b=l1_ml_conv_b,
             l1_ml_ig_b=l1_ml_ig_b, l1_ml_fg_b=l1_ml_fg_b, l1_ml_norm_g=l1_ml_norm_g)
    nb, seq = x_prompt.shape[:2]
    db, dseq = x_sample.shape[:2]
    assert db <= 7 and seq % MLA_QB == 0 and dseq % MLA_QB == 0

    cond8 = jnp.zeros((8, D_MODEL), F32).at[0].set(c_ctx).at[1:1 + db].set(c)
    mod0 = _ada(cond8, l0_ada_w, l0_ada_b).reshape(8, 1, 6 * D_MODEL)
    mod1 = _ada(cond8, l1_ada_w, l1_ada_b).reshape(8, 1, 6 * D_MODEL)
    w0 = _prep_layer0(l0_w_in, l0_mla_w_qb, l0_mla_w_kvb, l0_w_out)
    w1 = _prep_layer1(l1_w_in, l1_w_out)
    peer0 = _prep_peer(l0_peer_wq, l0_peer_keys, l0_peer_u, l0_peer_v)
    peer1 = _prep_peer(l1_peer_wq, l1_peer_keys, l1_peer_u, l1_peer_v)

    y_prompt, new = _trunk(x_prompt.reshape(nb * seq, D_MODEL), seq, mod0, mod1, w0, w1, peer0, peer1, p, None, None,
                           0, nb * seq, final_norm_g)
    ctx = (state_l0_ssd, cache_l0_mla_ckv, cache_l0_mla_kpe, cache_l1_gqa_k, cache_l1_gqa_v, state_l1_mlstm_C,
           state_l1_mlstm_n, state_l1_mlstm_m)
    y_sample, _ = _trunk(x_sample.reshape(db * dseq, D_MODEL), dseq, mod0, mod1, w0, w1, peer0, peer1, p, ctx,
                         _rope_tables(dseq), 1, dseq, final_norm_g)
    return (y_prompt.reshape(nb, seq, D_MODEL), y_sample.reshape(db, dseq, D_MODEL)) + new
```

```python
import functools

import jax
import jax.numpy as jnp
import numpy as np
from jax import lax
from jax.experimental import pallas as pl
from jax.experimental.pallas import tpu as pltpu

F32 = jnp.float32
BF16 = jnp.bfloat16
NEG_INF = float("-inf")

D_MODEL = 2048
EPS = 1e-6
ROPE_BASE = 10000.0
GRID_W = 64
CONV_W = 5
CHUNK = 64
SSD_HEADS = 16
SSD_HEAD_DIM = 64
SSD_D_INNER = 1024
SSD_STATE = 128
MLA_HEADS = 8
MLA_NOPE = 128
MLA_ROPE = 64
MLA_V = 128
MLA_RANK = 512
GQA_HEADS = 16
GQA_KV_HEADS = 4
GQA_HEAD_DIM = 64
WINDOW = 128
ML_HEADS = 8
ML_HEAD_DIM = 128
ML_D = 1024
PEER_HEADS = 8
PEER_NKEYS = 128
PEER_N = PEER_NKEYS * PEER_NKEYS
PEER_TOPK = 16

LANES = 128
SUBLANES = 8
VMEM_LIMIT = 56 * 2**20

L0_COLS = 3840
L1_COLS = 5760


def _cp(n_grid, flags=None):
    return pltpu.CompilerParams(dimension_semantics=("arbitrary",) * n_grid, vmem_limit_bytes=VMEM_LIMIT, flags=flags)


def _silu(x):
    return x * jax.nn.sigmoid(x)


def _softplus(x):
    return jnp.maximum(x, 0.0) + jnp.log1p(jnp.exp(-jnp.abs(x)))


def _rms(x):
    return x * lax.rsqrt(jnp.mean(x * x, axis=-1, keepdims=True) + EPS)


def _dot(a, b):
    return jnp.dot(a, b, preferred_element_type=F32)


def _dot_nt(a, b):
    return lax.dot_general(a, b, (((1,), (1,)), ((), ())), preferred_element_type=F32)


def _full(shape):
    nd = len(shape)
    return pl.BlockSpec(shape, lambda *_: (0,) * nd)


def _mod_spec(chunk, tb, mod_base, rows_per_cond):
    def index(i, *_):
        return (mod_base + (i * tb) // rows_per_cond, 0, chunk)
    return pl.BlockSpec((1, 1, D_MODEL), index)


def _ada_kernel(c_ref, w_ref, b_ref, o_ref):
    s = _silu(c_ref[...])
    o_ref[...] = _dot(s.astype(BF16), w_ref[...].astype(BF16)) + b_ref[...]


def _ada(cond8, w, b):
    n = w.shape[1]
    tn = 1024
    return pl.pallas_call(
        _ada_kernel,
        grid=(n // tn,),
        in_specs=[_full((8, D_MODEL)), pl.BlockSpec((D_MODEL, tn), lambda j: (0, j)),
                  pl.BlockSpec((1, tn), lambda j: (0, j))],
        out_specs=pl.BlockSpec((8, tn), lambda j: (0, j)),
        out_shape=jax.ShapeDtypeStruct((8, n), F32),
        compiler_params=_cp(1), name="ada_table",
    )(cond8, w, b.reshape(1, n))


def _proj_kernel(x_ref, g_ref, sc_ref, sh_ref, w_ref, o_ref, h_ref):
    @pl.when(pl.program_id(1) == 0)
    def _():
        h = _rms(x_ref[...]) * g_ref[...] * (1.0 + sc_ref[0]) + sh_ref[0]
        h_ref[...] = h.astype(BF16)
    o_ref[...] = _dot(h_ref[...], w_ref[...])


def _proj(x, g, mod, w, *, tn, mod_base, rows_per_cond):
    n, ncol = x.shape[0], w.shape[1]
    tb = 1024
    return pl.pallas_call(
        _proj_kernel,
        grid=(n // tb, ncol // tn),
        in_specs=[pl.BlockSpec((tb, D_MODEL), lambda i, j: (i, 0)), _full((1, D_MODEL)),
                  _mod_spec(1, tb, mod_base, rows_per_cond), _mod_spec(0, tb, mod_base, rows_per_cond),
                  pl.BlockSpec((D_MODEL, tn), lambda i, j: (0, j))],
        out_specs=pl.BlockSpec((tb, tn), lambda i, j: (i, j)),
        out_shape=jax.ShapeDtypeStruct((n, ncol), F32),
        scratch_shapes=[pltpu.VMEM((tb, D_MODEL), BF16)],
        compiler_params=_cp(2), name="norm_mod_proj",
    )(x, g.reshape(1, D_MODEL), mod, mod, w)


def _outproj_kernel(ya_ref, yb_ref, w_ref, x_ref, g1_ref, o_ref):
    half = ya_ref.shape[1]
    y = _dot(ya_ref[...].astype(BF16), w_ref[0:half, :]) + _dot(yb_ref[...].astype(BF16), w_ref[half:, :])
    o_ref[...] = x_ref[...] + g1_ref[0] * y


def _outproj(ya, yb, w, x, mod, *, mod_base, rows_per_cond):
    n = x.shape[0]
    tb = 512
    return pl.pallas_call(
        _outproj_kernel,
        grid=(n // tb,),
        in_specs=[pl.BlockSpec((tb, ya.shape[1]), lambda i: (i, 0)), pl.BlockSpec((tb, yb.shape[1]), lambda i: (i, 0)),
                  _full(w.shape), pl.BlockSpec((tb, D_MODEL), lambda i: (i, 0)),
                  _mod_spec(2, tb, mod_base, rows_per_cond)],
        out_specs=pl.BlockSpec((tb, D_MODEL), lambda i: (i, 0)),
        out_shape=jax.ShapeDtypeStruct((n, D_MODEL), F32),
        compiler_params=_cp(1), name="out_proj_residual",
    )(ya, yb, w, x, mod)


ROUTE_TB = 512
_CAND_KEEP = (16, 8, 5, 4, 3, 2, 2, 2)


def _sort_pairs(n):
    pairs = []

    def merge(lo, hi, r):
        step = r * 2
        if step < hi - lo:
            merge(lo, hi, step)
            merge(lo + r, hi, step)
            pairs.extend((i, i + r) for i in range(lo + r, hi - r, step))
        else:
            pairs.append((lo, lo + r))

    def sort(lo, hi):
        if hi - lo >= 1:
            mid = lo + (hi - lo) // 2
            sort(lo, mid)
            sort(mid + 1, hi)
            merge(lo, hi, 1)

    sort(0, 15)
    return [(i, j) for i, j in pairs if j < n]


def _top_values(x, count):
    groups = [x[r:r + SUBLANES] for r in range(0, x.shape[0], SUBLANES)]
    n = len(groups)
    for i, j in _sort_pairs(n):
        groups[i], groups[j] = jnp.maximum(groups[i], groups[j]), jnp.minimum(groups[i], groups[j])
    sub = lax.broadcasted_iota(jnp.int32, groups[0].shape, 0)
    tops = []
    for t in range(count):
        head = groups[0]
        m = jnp.max(head, axis=0, keepdims=True)
        tops.append(m)
        if t == count - 1:
            break
        first = jnp.min(jnp.where(head == m, sub, SUBLANES), axis=0, keepdims=True)
        popped = sub == first
        depth = min(count - 1 - t, n)
        for r in range(depth):
            below = groups[r + 1] if r + 1 < n else NEG_INF
            groups[r] = jnp.where(popped, below, groups[r])
    return tops


def _route_kernel(x_ref, g_ref, sc_ref, sh_ref, wqt_ref, keys_ref,
                  hb_ref, th_ref, s2_ref, e2_ref, c1_ref, qt_s, top_s):
    tb = x_ref.shape[0]
    h = _rms(x_ref[...]) * g_ref[...] * (1.0 + sc_ref[0]) + sh_ref[0]
    hb = h.astype(BF16)
    hb_ref[...] = hb
    qt_s[...] = _dot_nt(wqt_ref[...], hb).astype(BF16)

    iota_8 = lax.broadcasted_iota(jnp.int32, (SUBLANES, tb), 0)

    def head(hd, _):
        scores = []
        for c in range(2):
            r0 = pl.multiple_of((hd * 2 + c) * PEER_NKEYS, PEER_NKEYS)
            s = _dot(keys_ref[hd * 2 + c], qt_s[pl.ds(r0, PEER_NKEYS), :])
            scores.append(s)
            for it, m in enumerate(_top_values(s, PEER_TOPK)):
                top_s[c, it:it + 1, :] = m
        a1 = top_s[0]
        a2 = top_s[1]
        pieces = [a1[0:1] + a2, a1[1:2] + a2[0:8]]
        for i in range(2, 8):
            pieces.append(jnp.where(iota_8 < _CAND_KEEP[i], a1[i:i + 1] + a2[0:8], NEG_INF))
        pieces.append(a1[8:16] + a2[0:1])
        cand = jnp.concatenate(pieces, axis=0)
        tau = _top_values(cand, PEER_TOPK)[-1]
        top = a1[0:1] + a2[0:1]
        z = jnp.sum(jnp.where(cand >= tau, jnp.exp(cand - top), 0.0), axis=0, keepdims=True)
        e2 = jnp.exp(scores[1] - a2[0:1])
        c1 = jnp.exp(scores[0] - a1[0:1]) / (2.0 * z)
        theta = jnp.full(scores[0].shape, jnp.inf, F32)
        for q in range(PEER_TOPK):
            a2q = a2[q:q + 1]
            phi = jnp.min(jnp.where(a1 + a2q >= tau, a1, jnp.inf), axis=0, keepdims=True)
            theta = jnp.where(scores[0] >= phi, a2q, theta)
        for t in range(tb // LANES):
            ts = slice(t * LANES, (t + 1) * LANES)
            th_ref[hd, t] = theta[:, ts]
            s2_ref[hd, t] = scores[1][:, ts]
            e2_ref[hd, t] = e2[:, ts]
            c1_ref[hd, t] = c1[:, ts]
        return 0

    lax.fori_loop(0, PEER_HEADS, head, 0)


def _route(x, g, mod, wqt, keys, *, mod_base, rows_per_cond):
    n = x.shape[0]
    tb = ROUTE_TB
    ntg = tb // LANES
    hk = (PEER_HEADS, n // LANES, PEER_NKEYS, LANES)
    rspec = pl.BlockSpec((PEER_HEADS, ntg, PEER_NKEYS, LANES), lambda i: (0, i, 0, 0))
    return pl.pallas_call(
        _route_kernel,
        grid=(n // tb,),
        in_specs=[pl.BlockSpec((tb, D_MODEL), lambda i: (i, 0)), _full((1, D_MODEL)),
                  _mod_spec(4, tb, mod_base, rows_per_cond), _mod_spec(3, tb, mod_base, rows_per_cond),
                  _full(wqt.shape), _full(keys.shape)],
        out_specs=[pl.BlockSpec((tb, D_MODEL), lambda i: (i, 0)), rspec, rspec, rspec, rspec],
        out_shape=[jax.ShapeDtypeStruct((n, D_MODEL), BF16)] + [jax.ShapeDtypeStruct(hk, F32)] * 4,
        scratch_shapes=[pltpu.VMEM((PEER_HEADS * 2 * PEER_NKEYS, tb), BF16), pltpu.VMEM((2, PEER_TOPK, tb), F32)],
        compiler_params=_cp(1), name="peer_route",
    )(x, g.reshape(1, D_MODEL), mod, mod, wqt, keys)


PEER_TB = 512
PEER_EC = 1024
PEER_SUB = PEER_EC // 2


def _experts_kernel(hb_ref, th_ref, s2_ref, e2_ref, c1_ref, u0_ref, ub_ref, un_ref, vp_ref, va_ref, vl_ref,
                    x_ref, g2_ref, fg_ref, o_ref, acc_s, st0_s, st1_s, a0_s, a1_s, *, final_norm):
    e = pl.program_id(1)
    tb = hb_ref.shape[0]
    ntg = tb // LANES
    rows_per_sub = PEER_SUB // PEER_NKEYS

    def scores(u_blk, st_s, h):
        res = _dot_nt(u_blk[...], hb_ref[h * 2 * LANES:(h + 1) * 2 * LANES, :])
        st_s[2 * h] = res[:, :LANES]
        st_s[2 * h + 1] = res[:, LANES:]

    def values(vt_blk, act_s, h):
        act = jnp.concatenate([act_s[2 * h], act_s[2 * h + 1]], axis=1)
        hs = slice(h * 2 * LANES, (h + 1) * 2 * LANES)
        for r in range(0, D_MODEL, PEER_SUB):
            acc_s[r:r + PEER_SUB, hs] += _dot(vt_blk[r:r + PEER_SUB, :], act)

    def gates(c, tg, st_s, act_s):
        sub = 32
        for k in range(rows_per_sub):
            row = c * rows_per_sub + k
            for j0 in range(0, PEER_NKEYS, sub):
                js = slice(j0, j0 + sub)
                ks = slice(k * PEER_NKEYS + j0, k * PEER_NKEYS + j0 + sub)
                w = None
                for hd in range(PEER_HEADS):
                    chosen = s2_ref[hd, tg, js, :] >= th_ref[hd, tg, row:row + 1, :]
                    term = jnp.where(chosen, e2_ref[hd, tg, js, :], 0.0) * c1_ref[hd, tg, row:row + 1, :]
                    w = term if w is None else w + term
                s = st_s[tg, ks, :]
                act = s * (1.0 + lax.erf(s * np.float32(2.0 ** -0.5)))
                act_s[tg, ks, :] = (act * w).astype(BF16)

    @pl.when(e == 0)
    def _():
        acc_s[...] = jnp.zeros_like(acc_s)
        a1_s[...] = jnp.zeros_like(a1_s)
        scores(u0_ref, st0_s, 0)
        scores(u0_ref, st0_s, 1)

    def phase(c, st_cur, act_cur, u_next, st_next, vt_prev, act_prev):
        for h in range(2):
            scores(u_next, st_next, h)
            gates(c, 2 * h, st_cur, act_cur)
            values(vt_prev, act_prev, h)
            gates(c, 2 * h + 1, st_cur, act_cur)

    phase(0, st0_s, a0_s, ub_ref, st1_s, vp_ref, a1_s)
    phase(1, st1_s, a1_s, un_ref, st0_s, va_ref, a0_s)

    @pl.when(e == pl.num_programs(1) - 1)
    def _():
        values(vl_ref, a1_s, 0)
        values(vl_ref, a1_s, 1)
        y = x_ref[...] + g2_ref[0] * acc_s[...].T
        if final_norm:
            y = _rms(y) * fg_ref[...]
        o_ref[...] = y


def _experts(hb, theta, s2, e2, c1, u, vt, x, mod, fg, *, mod_base, rows_per_cond, final_norm):
    n = x.shape[0]
    tb = PEER_TB
    ntg = tb // LANES
    ne = PEER_N // PEER_EC
    once = dict(pipeline_mode=pl.Buffered(1))
    rspec = pl.BlockSpec((PEER_HEADS, ntg, PEER_NKEYS, LANES), lambda i, e: (0, i, 0, 0), **once)
    rows8 = pl.BlockSpec((PEER_HEADS, ntg, SUBLANES, LANES), lambda i, e: (0, i, e, 0))
    uspec = lambda index: pl.BlockSpec((PEER_SUB, D_MODEL), index)
    vspec = lambda index: pl.BlockSpec((D_MODEL, PEER_SUB), index)
    return pl.pallas_call(
        functools.partial(_experts_kernel, final_norm=final_norm),
        grid=(n // tb, ne),
        in_specs=[pl.BlockSpec((tb, D_MODEL), lambda i, e: (i, 0)), rows8, rspec, rspec, rows8,
                  pl.BlockSpec((PEER_SUB, D_MODEL), lambda i, e: (0, 0), **once),
                  uspec(lambda i, e: (2 * e + 1, 0)),
                  uspec(lambda i, e: (jnp.minimum(2 * e + 2, 2 * ne - 2), 0)),
                  vspec(lambda i, e: (0, jnp.maximum(2 * e - 1, 0))),
                  vspec(lambda i, e: (0, 2 * e)),
                  pl.BlockSpec((D_MODEL, PEER_SUB), lambda i, e: (0, 2 * ne - 1), **once),
                  pl.BlockSpec((tb, D_MODEL), lambda i, e: (i, 0)),
                  _mod_spec(5, tb, mod_base, rows_per_cond), _full((1, D_MODEL))],
        out_specs=pl.BlockSpec((tb, D_MODEL), lambda i, e: (i, 0)),
        out_shape=jax.ShapeDtypeStruct((n, D_MODEL), F32),
        scratch_shapes=[pltpu.VMEM((D_MODEL, tb), F32), pltpu.VMEM((ntg, PEER_SUB, LANES), F32),
                        pltpu.VMEM((ntg, PEER_SUB, LANES), F32), pltpu.VMEM((ntg, PEER_SUB, LANES), BF16),
                        pltpu.VMEM((ntg, PEER_SUB, LANES), BF16)],
        compiler_params=_cp(2), name="peer_experts",
    )(hb, theta, s2, e2, c1, u, u, u, vt, vt, vt, x, mod, fg.reshape(1, D_MODEL))


def _peer(x, g, mod, wqt, keys, u, vt, fg, *, mod_base, rows_per_cond, final_norm):
    hb, theta, s2, e2, c1 = _route(x, g, mod, wqt, keys, mod_base=mod_base, rows_per_cond=rows_per_cond)
    return _experts(hb, theta, s2, e2, c1, u, vt, x, mod, fg, mod_base=mod_base, rows_per_cond=rows_per_cond,
                    final_norm=final_norm)


def _chunk_index(s, nc):
    return jnp.where(s < nc, s, 2 * nc - 1 - s)


def _scan_group(nseq, largest):
    return max(g for g in (2, 4, 8) if g <= largest and nseq % g == 0)


def _scan_specs(nc):
    def cur(col):
        return lambda b, s: (b, _chunk_index(s, nc), col)

    def prev(col):
        return lambda b, s: (b, jnp.maximum(_chunk_index(s, nc) * 8 - 1, 0), col)

    def nxt(col):
        return lambda b, s: (b, jnp.minimum(_chunk_index(s, nc) * 8 + 8, nc * 8 - 1), col)

    def out(col):
        return lambda b, s: (b, jnp.where(s < nc, nc - 1, 2 * nc - 1 - s), col)
    return cur, prev, nxt, out


def _conv_silu(prev, cur, nxt, w_ref, b_ref, first, last):
    win = jnp.concatenate([jnp.where(first, 0.0, prev), cur, jnp.where(last, 0.0, nxt)], axis=1)
    acc = b_ref[...] + win[:, 6:70, :] * w_ref[0:1, :]
    for k in range(1, CONV_W):
        acc = acc + win[:, 6 + k:70 + k, :] * w_ref[k:k + 1, :]
    return _silu(acc)


def _tri(bwd, G):
    r = lax.broadcasted_iota(jnp.int32, (G, CHUNK, CHUNK), 1)
    c = lax.broadcasted_iota(jnp.int32, (G, CHUNK, CHUNK), 2)
    lower = (c <= r).astype(F32)
    upper = (c >= r).astype(F32)
    return jnp.where(bwd, upper, lower), jnp.where(bwd, lower, upper)


def _bdot(a, b, contract, precision=None):
    return lax.dot_general(a, b, (((contract[0],), (contract[1],)), ((0,), (0,))), preferred_element_type=F32,
                           precision=precision)


def _ssd_kernel(z_ref, xp_ref, xc_ref, xn_ref, bp_ref, bc_ref, bn_ref, sm_ref, dtt_ref,
                cwx_ref, cbx_ref, cwb_ref, cbb_ref, arow_ref, acol_ref, brow_ref, bcol_ref, drow_ref, ng_ref,
                *rest, nc, zero_init):
    st0_ref = None if zero_init else rest[0]
    y_ref, st_ref, yf_s, h_s = rest[0 if zero_init else 1:]
    s = pl.program_id(1)
    bwd = s >= nc
    cidx = _chunk_index(s, nc)
    first = cidx == 0
    last = cidx == nc - 1
    nh = SSD_HEADS

    def initial(direction):
        return jnp.zeros(h_s.shape, F32) if zero_init else st0_ref[:, direction]

    @pl.when(s == 0)
    def _():
        h_s[...] = initial(0)

    @pl.when(s == nc)
    def _():
        h_s[...] = initial(1)

    G = z_ref.shape[0]
    tsel, tsel_t = _tri(bwd, G)
    mask = tsel > 0.0
    r0 = pl.multiple_of(cidx * CHUNK, CHUNK)
    hi = lax.Precision.HIGHEST

    xs = _conv_silu(xp_ref[...], xc_ref[...], xn_ref[...], cwx_ref, cbx_ref, first, last)
    bcv = _conv_silu(bp_ref[...], bc_ref[...], bn_ref[...], cwb_ref, cbb_ref, first, last)
    dt_all = _softplus(sm_ref[:, :, 64:96] + brow_ref[...])
    dtt_all = _softplus(dtt_ref[:, 0] + bcol_ref[...])
    a_all = dt_all * (-jnp.exp(arow_ref[...]))
    at_all = dtt_all * (-jnp.exp(acol_ref[...]))
    dt = jnp.where(bwd, dt_all[:, :, nh:2 * nh], dt_all[:, :, 0:nh])
    a = jnp.where(bwd, a_all[:, :, nh:2 * nh], a_all[:, :, 0:nh])
    at = jnp.where(bwd, at_all[:, nh:2 * nh, :], at_all[:, 0:nh, :])
    cum = _bdot(tsel, a, (2, 1), hi)
    cum_t = _bdot(at, tsel_t, (2, 1), hi)
    tot = jnp.where(bwd, cum[:, 0:1, :], cum[:, CHUNK - 1:CHUNK, :])

    ys = []
    for g in range(2):
        bg = bcv[:, :, g * SSD_STATE:(g + 1) * SSD_STATE]
        cg = bcv[:, :, 256 + g * SSD_STATE:256 + (g + 1) * SSD_STATE]
        cg16 = cg.astype(BF16)
        cb = _bdot(cg16, bg.astype(BF16), (2, 2))
        for hh in range(nh // 2):
            h = g * (nh // 2) + hh
            cq = cum[:, :, h:h + 1]
            decay = jnp.exp(jnp.where(mask, cq - cum_t[:, h:h + 1, :], NEG_INF))
            xd = xs[:, :, h * SSD_HEAD_DIM:(h + 1) * SSD_HEAD_DIM] * dt[:, :, h:h + 1]
            xd16 = xd.astype(BF16)
            hprev = h_s[:, h]
            y = (_bdot((cb * decay).astype(BF16), xd16, (2, 1))
                 + _bdot(cg16, hprev.astype(BF16), (2, 2)) * jnp.exp(cq))
            ys.append(y)
            th = tot[:, :, h:h + 1]
            bw = bg * jnp.exp(th - cq)
            h_s[:, h] = hprev * jnp.exp(th) + _bdot(xd16, bw.astype(BF16), (1, 1))
    y = jnp.concatenate(ys, axis=2)

    @pl.when(jnp.logical_not(bwd))
    def _():
        yf_s[:, pl.ds(r0, CHUNK), :] = y

    @pl.when(bwd)
    def _():
        yt = (yf_s[:, pl.ds(r0, CHUNK), :] + y + drow_ref[...] * xs) * _silu(z_ref[...])
        y_ref[...] = _rms(yt) * ng_ref[...]

    @pl.when(jnp.logical_or(s == nc - 1, s == 2 * nc - 1))
    def _():
        st_ref[:, 0] = h_s[...]


def _ssd(p0, dtt, cwx, cbx, cwb, cbb, a_log, dt_bias, d_skip, norm_g, st0, *, L):
    nseq, nc = p0.shape[0], L // CHUNK
    zero_init = st0 is None
    G = _scan_group(nseq, 8 if zero_init else 4)
    cur, prev, nxt, out = _scan_specs(nc)
    st_shape = (G, 2, SSD_HEADS, SSD_HEAD_DIM, SSD_STATE)
    states = [] if zero_init else [st0]
    row = lambda v: v.reshape(1, -1)
    col = lambda v: v.reshape(-1, 1)
    small = [cwx, row(cbx), cwb, row(cbb), row(a_log), col(a_log), row(dt_bias), col(dt_bias),
             row(jnp.repeat(d_skip, SSD_HEAD_DIM)), row(norm_g)]
    blk = lambda rows, width, index: pl.BlockSpec((G, rows, width), index)
    return pl.pallas_call(
        functools.partial(_ssd_kernel, nc=nc, zero_init=zero_init),
        grid=(nseq // G, 2 * nc),
        in_specs=[blk(CHUNK, 1024, cur(0)),
                  blk(8, 1024, prev(1)), blk(CHUNK, 1024, cur(1)), blk(8, 1024, nxt(1)),
                  blk(8, 512, prev(4)), blk(CHUNK, 512, cur(4)), blk(8, 512, nxt(4)),
                  blk(CHUNK, LANES, cur(28)),
                  pl.BlockSpec((G, 1, 32, CHUNK), lambda b, s: (b, _chunk_index(s, nc), 0, 0))]
        + [_full(v.shape) for v in small]
        + [pl.BlockSpec(st_shape, lambda b, s: (b, 0, 0, 0, 0)) for _ in states],
        out_specs=[blk(CHUNK, SSD_D_INNER, out(0)),
                   pl.BlockSpec((G, 1) + st_shape[2:], lambda b, s: (b, jnp.where(s < nc, 0, 1), 0, 0, 0))],
        out_shape=[jax.ShapeDtypeStruct((nseq, L, SSD_D_INNER), F32),
                   jax.ShapeDtypeStruct((nseq,) + st_shape[1:], F32)],
        scratch_shapes=[pltpu.VMEM((G, L, SSD_D_INNER), F32), pltpu.VMEM(st_shape[:1] + st_shape[2:], F32)],
        compiler_params=_cp(2), name="ssd_mixer",
    )(p0, p0, p0, p0, p0, p0, p0, p0, dtt, *small, *states)


def _ssd_bidir_kernel(*refs, nc, zero_init):
    f_in, b_in = refs[0:9], refs[9:18]
    (cwx_ref, cbx_ref, cwb_ref, cbb_ref, arow_ref, acol_ref, brow_ref, bcol_ref, drow_ref, ng_ref) = refs[18:28]
    rest = refs[28:]
    st0_ref = None if zero_init else rest[0]
    ya_ref, yb_ref, st_ref, ys_s, h_s = rest[0 if zero_init else 1:]
    j = pl.program_id(1)
    nh = SSD_HEADS
    G = f_in[0].shape[0]
    hi = lax.Precision.HIGHEST

    @pl.when(j == 0)
    def _():
        if zero_init:
            h_s[...] = jnp.zeros(h_s.shape, F32)
        else:
            h_s[0:G] = st0_ref[:, 0]
            h_s[G:] = st0_ref[:, 1]

    def conv_half(refs9, first, last):
        _, xp, xc, xn, bp, bc, bn, _, _ = refs9
        return (_conv_silu(xp[...], xc[...], xn[...], cwx_ref, cbx_ref, first, last),
                _conv_silu(bp[...], bc[...], bn[...], cwb_ref, cbb_ref, first, last))

    xs_f, bc_f = conv_half(f_in, j == 0, j == nc - 1)
    xs_b, bc_b = conv_half(b_in, j == nc - 1, j == 0)
    xs = jnp.concatenate([xs_f, xs_b], axis=0)
    bcv = jnp.concatenate([bc_f, bc_b], axis=0)

    def gate_cols(refs9, lo):
        dt_all = _softplus(refs9[7][:, :, 64:96] + brow_ref[...])
        dtt_all = _softplus(refs9[8][:, 0] + bcol_ref[...])
        a_all = dt_all * (-jnp.exp(arow_ref[...]))
        at_all = dtt_all * (-jnp.exp(acol_ref[...]))
        return dt_all[:, :, lo:lo + nh], a_all[:, :, lo:lo + nh], at_all[:, lo:lo + nh, :]

    dt_f, a_f, at_f = gate_cols(f_in, 0)
    dt_b, a_b, at_b = gate_cols(b_in, nh)
    dt = jnp.concatenate([dt_f, dt_b], axis=0)
    a = jnp.concatenate([a_f, a_b], axis=0)
    at = jnp.concatenate([at_f, at_b], axis=0)

    shape = (2 * G, CHUNK, CHUNK)
    ent = lax.broadcasted_iota(jnp.int32, shape, 0)
    r = lax.broadcasted_iota(jnp.int32, shape, 1)
    c = lax.broadcasted_iota(jnp.int32, shape, 2)
    fwd = ent < G
    lower = (c <= r).astype(F32)
    upper = (c >= r).astype(F32)
    tsel = jnp.where(fwd, lower, upper)
    tsel_t = jnp.where(fwd, upper, lower)
    mask = tsel > 0.0
    cum = _bdot(tsel, a, (2, 1), hi)
    cum_t = _bdot(at, tsel_t, (2, 1), hi)
    tot = jnp.concatenate([cum[0:G, CHUNK - 1:CHUNK, :], cum[G:, 0:1, :]], axis=0)

    ys = []
    for g in range(2):
        bg = bcv[:, :, g * SSD_STATE:(g + 1) * SSD_STATE]
        cg = bcv[:, :, 256 + g * SSD_STATE:256 + (g + 1) * SSD_STATE]
        cg16 = cg.astype(BF16)
        cb = _bdot(cg16, bg.astype(BF16), (2, 2))
        for hh in range(nh // 2):
            h = g * (nh // 2) + hh
            cq = cum[:, :, h:h + 1]
            decay = jnp.exp(jnp.where(mask, cq - cum_t[:, h:h + 1, :], NEG_INF))
            xd = xs[:, :, h * SSD_HEAD_DIM:(h + 1) * SSD_HEAD_DIM] * dt[:, :, h:h + 1]
            xd16 = xd.astype(BF16)
            hprev = h_s[:, h]
            y = (_bdot((cb * decay).astype(BF16), xd16, (2, 1))
                 + _bdot(cg16, hprev.astype(BF16), (2, 2)) * jnp.exp(cq))
            ys.append(y)
            th = tot[:, :, h:h + 1]
            bw = bg * jnp.exp(th - cq)
            h_s[:, h] = hprev * jnp.exp(th) + _bdot(xd16, bw.astype(BF16), (1, 1))
    y = jnp.concatenate(ys, axis=2)
    rf = pl.multiple_of(j * CHUNK, CHUNK)
    rb = pl.multiple_of((nc - 1 - j) * CHUNK, CHUNK)

    @pl.when(j < nc // 2)
    def _():
        ys_s[:, pl.ds(rf, CHUNK), :] = y[0:G]
        ys_s[:, pl.ds(rb, CHUNK), :] = y[G:]

    @pl.when(j >= nc // 2)
    def _():
        def finish(now, stored, x_half, z_ref):
            yt = (stored + now + drow_ref[...] * x_half) * _silu(z_ref[...])
            return _rms(yt) * ng_ref[...]
        ya_ref[...] = finish(y[0:G], ys_s[:, pl.ds(rf, CHUNK), :], xs_f, f_in[0])
        yb_ref[...] = finish(y[G:], ys_s[:, pl.ds(rb, CHUNK), :], xs_b, b_in[0])

    @pl.when(j == nc - 1)
    def _():
        st_ref[:, 0] = h_s[0:G]
        st_ref[:, 1] = h_s[G:]


def _ssd_bidir(p0, dtt, cwx, cbx, cwb, cbb, a_log, dt_bias, d_skip, norm_g, st0, *, L):
    nseq, nc = p0.shape[0], L // CHUNK
    zero_init = st0 is None
    G = _scan_group(nseq, 4)
    st_shape = (G, 2, SSD_HEADS, SSD_HEAD_DIM, SSD_STATE)
    states = [] if zero_init else [st0]
    row = lambda v: v.reshape(1, -1)
    col = lambda v: v.reshape(-1, 1)
    small = [cwx, row(cbx), cwb, row(cbb), row(a_log), col(a_log), row(dt_bias), col(dt_bias),
             row(jnp.repeat(d_skip, SSD_HEAD_DIM)), row(norm_g)]

    def specs(chunk):
        cur = lambda colb: (lambda b, j: (b, chunk(j), colb))
        prev = lambda colb: (lambda b, j: (b, jnp.maximum(chunk(j) * 8 - 1, 0), colb))
        nxt = lambda colb: (lambda b, j: (b, jnp.minimum(chunk(j) * 8 + 8, nc * 8 - 1), colb))
        blk = lambda rows, width, index: pl.BlockSpec((G, rows, width), index)
        return [blk(CHUNK, 1024, cur(0)),
                blk(8, 1024, prev(1)), blk(CHUNK, 1024, cur(1)), blk(8, 1024, nxt(1)),
                blk(8, 512, prev(4)), blk(CHUNK, 512, cur(4)), blk(8, 512, nxt(4)),
                blk(CHUNK, LANES, cur(28)),
                pl.BlockSpec((G, 1, 32, CHUNK), lambda b, j: (b, chunk(j), 0, 0))]

    half = nc // 2
    y_sds = jax.ShapeDtypeStruct((nseq, L // 2, SSD_D_INNER), F32)
    ya, yb, st = pl.pallas_call(
        functools.partial(_ssd_bidir_kernel, nc=nc, zero_init=zero_init),
        grid=(nseq // G, nc),
        in_specs=specs(lambda j: j) + specs(lambda j: nc - 1 - j) + [_full(v.shape) for v in small]
        + [pl.BlockSpec(st_shape, lambda b, j: (b, 0, 0, 0, 0)) for _ in states],
        out_specs=[pl.BlockSpec((G, CHUNK, SSD_D_INNER), lambda b, j: (b, jnp.maximum(j, half) - half, 0)),
                   pl.BlockSpec((G, CHUNK, SSD_D_INNER), lambda b, j: (b, jnp.minimum(nc - 1 - j, half - 1), 0)),
                   pl.BlockSpec(st_shape, lambda b, j: (b, 0, 0, 0, 0))],
        out_shape=[y_sds, y_sds, jax.ShapeDtypeStruct((nseq,) + st_shape[1:], F32)],
        scratch_shapes=[pltpu.VMEM((G, L, SSD_D_INNER), F32), pltpu.VMEM((2 * G,) + st_shape[2:], F32)],
        compiler_params=_cp(2), name="ssd_mixer",
    )(*([p0] * 8 + [dtt]) * 2, *small, *states)
    return jnp.concatenate([yb, ya], axis=1), st


def _rope_tables(L):
    t = jnp.arange(L)
    r = (t // GRID_W).astype(F32)
    c = (t % GRID_W).astype(F32)
    nf = 16
    inv = ROPE_BASE ** (-jnp.arange(nf, dtype=F32) / nf)
    ang = jnp.concatenate([r[:, None] * inv, c[:, None] * inv], axis=-1)
    ang = jnp.concatenate([ang, ang], axis=-1)
    return jnp.cos(ang), jnp.sin(ang)


def _rope128(x, cos, sin):
    lane = lax.broadcasted_iota(jnp.int32, x.shape, 1)
    rot = jnp.where(lane % 64 < 32, -pltpu.roll(x, 96, 1), pltpu.roll(x, 32, 1))
    return x * cos + rot * sin


def _pad_lanes(x):
    return jnp.concatenate([x, jnp.zeros_like(x)], axis=1)


MLA_QB = 256


def _mla_kernel(*refs, L, Lc, rope, emit):
    refs = list(refs)
    qa_ref, kva_ref, sm_ref, qg_ref, kvg_ref, wqn_ref, wqr_ref, wkk_ref, wkv_ref = refs[:9]
    pos = 9
    if Lc:
        cckv_ref, ckpe_ref = refs[pos:pos + 2]
        pos += 2
    if rope:
        cos_ref, sin_ref = refs[pos:pos + 2]
        pos += 2
    y_ref = refs[pos]
    pos += 1
    if emit:
        ckv_ref = refs[pos]
        pos += 1
    qn_s, qr_s, kn_s, v_s, kpe_s = refs[pos:]
    qb = pl.program_id(1)
    scale = np.float32((MLA_NOPE + MLA_ROPE) ** -0.5)

    @pl.when(qb == 0)
    def _():
        ckv = _rms(kva_ref[...]) * kvg_ref[...]
        if emit:
            ckv_ref[...] = ckv
        kpe = _pad_lanes(sm_ref[:, 0:MLA_ROPE])
        if rope:
            kpe = _rope128(kpe, cos_ref[...], sin_ref[...])
        if Lc:
            ckv = jnp.concatenate([cckv_ref[0], ckv], axis=0)
            kpe = jnp.concatenate([_pad_lanes(ckpe_ref[0]), kpe], axis=0)
        c16 = ckv.astype(BF16)
        kn_s[...] = _dot(c16, wkk_ref[...]).astype(BF16)
        v_s[...] = _dot(c16, wkv_ref[...]).astype(BF16)
        kpe_s[...] = kpe.astype(BF16)

    qn = (_rms(qa_ref[...]) * qg_ref[...]).astype(BF16)
    qn_s[...] = _dot(qn, wqn_ref[...]).astype(BF16)
    qr = _dot(qn, wqr_ref[...])
    if rope:
        q0 = pl.multiple_of(qb * MLA_QB, MLA_QB)
        cos = cos_ref[pl.ds(q0, MLA_QB), :]
        sin = sin_ref[pl.ds(q0, MLA_QB), :]
        for h in range(MLA_HEADS):
            qr_s[:, h * LANES:(h + 1) * LANES] = _rope128(qr[:, h * LANES:(h + 1) * LANES], cos, sin).astype(BF16)
    else:
        qr_s[...] = qr.astype(BF16)

    def head(h, _):
        c0 = h * LANES if isinstance(h, int) else pl.multiple_of(h * LANES, LANES)
        s = (_dot_nt(qn_s[:, pl.ds(c0, LANES)], kn_s[:, pl.ds(c0, LANES)])
             + _dot_nt(qr_s[:, pl.ds(c0, LANES)], kpe_s[...])) * scale
        p = jnp.exp(s - jnp.max(s, axis=-1, keepdims=True))
        p = p / jnp.sum(p, axis=-1, keepdims=True)
        y_ref[:, pl.ds(c0, LANES)] = _dot(p.astype(BF16), v_s[:, pl.ds(c0, LANES)])
        return 0

    if L + Lc <= 512:
        for h in range(MLA_HEADS):
            head(h, 0)
    else:
        lax.fori_loop(0, MLA_HEADS, head, 0)


def _mla(p0, qg, kvg, wqn, wqr, wkk, wkv, cache_ckv, cache_kpe, rope_tabs, *, L):
    n = p0.shape[0]
    nseq, nqb = n // L, L // MLA_QB
    Lc = 0 if cache_ckv is None else cache_ckv.shape[1]
    rope = rope_tabs is not None
    emit = cache_ckv is None
    Lk = L + Lc
    args = [p0, p0, p0, qg.reshape(1, -1), kvg.reshape(1, -1), wqn, wqr, wkk, wkv]
    specs = [pl.BlockSpec((MLA_QB, MLA_RANK), lambda b, q: (b * nqb + q, 5)),
             pl.BlockSpec((L, MLA_RANK), lambda b, q: (b, 6)),
             pl.BlockSpec((L, LANES), lambda b, q: (b, 28)),
             _full((1, MLA_RANK)), _full((1, MLA_RANK)), _full(wqn.shape), _full(wqr.shape), _full(wkk.shape),
             _full(wkv.shape)]
    if Lc:
        args += [cache_ckv, cache_kpe]
        specs += [pl.BlockSpec((1, Lc, MLA_RANK), lambda b, q: (b, 0, 0)),
                  pl.BlockSpec((1, Lc, MLA_ROPE), lambda b, q: (b, 0, 0))]
    if rope:
        args += list(rope_tabs)
        specs += [_full((L, LANES)), _full((L, LANES))]
    out_specs = [pl.BlockSpec((MLA_QB, 1024), lambda b, q: (b * nqb + q, 0))]
    out_shape = [jax.ShapeDtypeStruct((n, 1024), F32)]
    if emit:
        out_specs.append(pl.BlockSpec((L, MLA_RANK), lambda b, q: (b, 0)))
        out_shape.append(jax.ShapeDtypeStruct((n, MLA_RANK), F32))
    return pl.pallas_call(
        functools.partial(_mla_kernel, L=L, Lc=Lc, rope=rope, emit=emit),
        grid=(nseq, nqb), in_specs=specs, out_specs=out_specs, out_shape=out_shape,
        scratch_shapes=[pltpu.VMEM((MLA_QB, 1024), BF16), pltpu.VMEM((MLA_QB, 1024), BF16),
                        pltpu.VMEM((Lk, 1024), BF16), pltpu.VMEM((Lk, 1024), BF16), pltpu.VMEM((Lk, LANES), BF16)],
        compiler_params=_cp(2), name="mla_attention",
    )(*args)


def _gqa_ctx_kernel(q_ref, k_ref, v_ref, sink_ref, y_ref):
    q = q_ref[...]
    k = k_ref[...].astype(BF16)
    v = v_ref[...].astype(BF16)
    scale = np.float32(GQA_HEAD_DIM ** -0.5)
    group = GQA_HEADS // GQA_KV_HEADS
    outs = []
    for h in range(GQA_HEADS):
        kh = h // group
        ks = slice(kh * GQA_HEAD_DIM, (kh + 1) * GQA_HEAD_DIM)
        s = _dot_nt(q[:, h * GQA_HEAD_DIM:(h + 1) * GQA_HEAD_DIM].astype(BF16), k[:, ks]) * scale
        m = jnp.maximum(jnp.max(s, axis=-1, keepdims=True), sink_ref[h])
        p = jnp.exp(s - m)
        p = p / (jnp.sum(p, axis=-1, keepdims=True) + jnp.exp(sink_ref[h] - m))
        outs.append(_dot(p.astype(BF16), v[:, ks]))
    y_ref[...] = jnp.concatenate(outs, axis=1)


def _gqa_ctx(p1, sink, *, L):
    n = p1.shape[0]
    return pl.pallas_call(
        _gqa_ctx_kernel,
        grid=(n // L,),
        in_specs=[pl.BlockSpec((L, 1024), lambda b: (b, 2)), pl.BlockSpec((L, 256), lambda b: (b, 20)),
                  pl.BlockSpec((L, 256), lambda b: (b, 21)), pl.BlockSpec(memory_space=pltpu.SMEM)],
        out_specs=pl.BlockSpec((L, 1024), lambda b: (b, 0)),
        out_shape=jax.ShapeDtypeStruct((n, 1024), F32),
        compiler_params=_cp(1), name="gqa_context",
    )(p1, p1, p1, sink)


GQA_QB = 128
GQA_SPAN = GQA_QB + 2 * WINDOW


def _gqa_win_kernel(q_ref, k_ref, v_ref, ck_ref, cv_ref, cos_ref, sin_ref, sink_ref, y_ref, kp_s, vp_s, *, L):
    qb = pl.program_id(1)
    scale = np.float32(GQA_HEAD_DIM ** -0.5)
    group = GQA_HEADS // GQA_KV_HEADS

    @pl.when(qb == 0)
    def _():
        zeros = jnp.zeros((WINDOW, 256), BF16)
        kp_s[0:WINDOW, :] = zeros
        kp_s[WINDOW + L:, :] = zeros
        vp_s[0:WINDOW, :] = zeros
        vp_s[WINDOW + L:, :] = zeros
        for j in range(2):
            cs = slice(j * LANES, (j + 1) * LANES)
            kp_s[WINDOW:WINDOW + L, cs] = _rope128(k_ref[:, cs], cos_ref[...], sin_ref[...]).astype(BF16)
        vp_s[WINDOW:WINDOW + L, :] = v_ref[...].astype(BF16)

    q0 = pl.multiple_of(qb * GQA_QB, GQA_QB)
    cos = cos_ref[pl.ds(q0, GQA_QB), :]
    sin = sin_ref[pl.ds(q0, GQA_QB), :]
    kw = kp_s[pl.ds(q0, GQA_SPAN), :]
    vw = vp_s[pl.ds(q0, GQA_SPAN), :]
    kc = ck_ref[0].astype(BF16)
    vc = cv_ref[0].astype(BF16)
    qpos = q0 + lax.broadcasted_iota(jnp.int32, (GQA_QB, GQA_SPAN), 0)
    kpos = q0 - WINDOW + lax.broadcasted_iota(jnp.int32, (GQA_QB, GQA_SPAN), 1)
    bias = jnp.where(kpos < 0, NEG_INF, jnp.where(kpos >= L, NEG_INF, jnp.where(jnp.abs(qpos - kpos) <= WINDOW, 0.0, NEG_INF)))
    outs = []
    for j in range(GQA_HEADS * GQA_HEAD_DIM // LANES):
        qj = _rope128(q_ref[:, j * LANES:(j + 1) * LANES], cos, sin).astype(BF16)
        for half in range(2):
            h = 2 * j + half
            kh = h // group
            ks = slice(kh * GQA_HEAD_DIM, (kh + 1) * GQA_HEAD_DIM)
            qh = qj[:, half * GQA_HEAD_DIM:(half + 1) * GQA_HEAD_DIM]
            s_loc = _dot_nt(qh, kw[:, ks]) * scale + bias
            s_ctx = _dot_nt(qh, kc[:, ks]) * scale
            m = jnp.maximum(jnp.maximum(jnp.max(s_loc, axis=-1, keepdims=True), jnp.max(s_ctx, axis=-1, keepdims=True)),
                            sink_ref[h])
            p_loc = jnp.exp(s_loc - m)
            p_ctx = jnp.exp(s_ctx - m)
            inv = 1.0 / (jnp.sum(p_loc, axis=-1, keepdims=True) + jnp.sum(p_ctx, axis=-1, keepdims=True)
                         + jnp.exp(sink_ref[h] - m))
            outs.append(_dot((p_loc * inv).astype(BF16), vw[:, ks]) + _dot((p_ctx * inv).astype(BF16), vc[:, ks]))
    y_ref[...] = jnp.concatenate(outs, axis=1)


def _gqa_win(p1, cache_k, cache_v, cos, sin, sink, *, L):
    n = p1.shape[0]
    nseq, nqb = n // L, L // GQA_QB
    Lc = cache_k.shape[1]
    return pl.pallas_call(
        functools.partial(_gqa_win_kernel, L=L),
        grid=(nseq, nqb),
        in_specs=[pl.BlockSpec((GQA_QB, 1024), lambda b, q: (b * nqb + q, 2)),
                  pl.BlockSpec((L, 256), lambda b, q: (b, 20)), pl.BlockSpec((L, 256), lambda b, q: (b, 21)),
                  pl.BlockSpec((1, Lc, 256), lambda b, q: (b, 0, 0)), pl.BlockSpec((1, Lc, 256), lambda b, q: (b, 0, 0)),
                  _full((L, LANES)), _full((L, LANES)), pl.BlockSpec(memory_space=pltpu.SMEM)],
        out_specs=pl.BlockSpec((GQA_QB, 1024), lambda b, q: (b * nqb + q, 0)),
        out_shape=jax.ShapeDtypeStruct((n, 1024), F32),
        scratch_shapes=[pltpu.VMEM((L + 2 * WINDOW, 256), BF16), pltpu.VMEM((L + 2 * WINDOW, 256), BF16)],
        compiler_params=_cp(2), name="gqa_window",
    )(p1, p1, p1, cache_k, cache_v, cos, sin, sink)


def _mlstm_kernel(qp_ref, qc_ref, qn_ref, v_ref, o_ref, sm_ref, gt_ref, cw_ref, cb_ref, brow_ref, bcol_ref, ng_ref,
                  *rest, nc, zero_init):
    init_refs = () if zero_init else rest[:3]
    y_ref, cst_ref, nst_ref, mst_ref, hf_s, c_s, n_s, m_s = rest[len(init_refs):]
    s = pl.program_id(1)
    bwd = s >= nc
    cidx = _chunk_index(s, nc)
    nh = ML_HEADS
    hd = ML_HEAD_DIM

    def load_initial(direction):
        for k, dst in enumerate((c_s, n_s, m_s)):
            dst[...] = jnp.zeros(dst.shape, F32) if zero_init else init_refs[k][:, direction]

    @pl.when(s == 0)
    def _():
        load_initial(0)

    @pl.when(s == nc)
    def _():
        cst_ref[:, 0] = c_s[...]
        nst_ref[:, 0] = n_s[...]
        mst_ref[:, 0] = m_s[...]
        load_initial(1)

    G = qc_ref.shape[0]
    tsel, tsel_t = _tri(bwd, G)
    mask = tsel > 0.0
    kscale = np.float32(hd ** -0.5)
    r0 = pl.multiple_of(cidx * CHUNK, CHUNK)
    hi = lax.Precision.HIGHEST

    qk = _conv_silu(qp_ref[...], qc_ref[...], qn_ref[...], cw_ref, cb_ref, cidx == 0, cidx == nc - 1)
    gates = sm_ref[:, :, 0:4 * nh] + brow_ref[...]
    gates_t = gt_ref[:, 0] + bcol_ref[...]
    li = jnp.where(bwd, gates[:, :, nh:2 * nh], gates[:, :, 0:nh])
    li_t = jnp.where(bwd, gates_t[:, nh:2 * nh, :], gates_t[:, 0:nh, :])
    lf = -_softplus(-jnp.where(bwd, gates[:, :, 3 * nh:4 * nh], gates[:, :, 2 * nh:3 * nh]))
    lf_t = -_softplus(-jnp.where(bwd, gates_t[:, 3 * nh:4 * nh, :], gates_t[:, 2 * nh:3 * nh, :]))
    bc = _bdot(tsel, lf, (2, 1), hi)
    bc_t = _bdot(lf_t, tsel_t, (2, 1), hi)
    tot = jnp.where(bwd, bc[:, 0:1, :], bc[:, CHUNK - 1:CHUNK, :])

    hs = []
    for h in range(nh):
        cs = slice(h * hd, (h + 1) * hd)
        q = qk[:, :, cs]
        k = qk[:, :, ML_D + h * hd:ML_D + (h + 1) * hd] * kscale
        q16, k16 = q.astype(BF16), k.astype(BF16)
        v = v_ref[:, :, cs]
        bq = bc[:, :, h:h + 1]
        m_prev = m_s[:, h:h + 1, 0:1]
        log_d = jnp.where(mask, bq - bc_t[:, h:h + 1, :] + li_t[:, h:h + 1, :], NEG_INF)
        log_inter = bq + m_prev
        m_out = jnp.maximum(log_inter, jnp.max(log_d, axis=2, keepdims=True))
        sd = _bdot(q16, k16, (2, 2)) * jnp.exp(log_d - m_out)
        w_inter = jnp.exp(log_inter - m_out)
        c_prev = c_s[:, h]
        n_prev = n_s[:, h:h + 1, :]
        num = _bdot(sd.astype(BF16), v.astype(BF16), (2, 1)) + w_inter * _bdot(q16, c_prev.astype(BF16), (2, 2))
        den = jnp.sum(sd, axis=2, keepdims=True) + w_inter * jnp.sum(q * n_prev, axis=2, keepdims=True)
        hs.append(num / jnp.maximum(jnp.abs(den), jnp.exp(-m_out)))
        th = tot[:, :, h:h + 1]
        end_inter = th + m_prev
        end_intra = th - bq + li[:, :, h:h + 1]
        m_new = jnp.maximum(end_inter, jnp.max(end_intra, axis=1, keepdims=True))
        w_c = jnp.exp(end_inter - m_new)
        w_k = jnp.exp(end_intra - m_new)
        c_s[:, h] = w_c * c_prev + _bdot((v * w_k).astype(BF16), k16, (1, 1))
        n_s[:, h:h + 1, :] = w_c * n_prev + jnp.sum(k * w_k, axis=1, keepdims=True)
        m_s[:, h:h + 1, :] = jnp.broadcast_to(m_new, (G, 1, hd))
    hcat = jnp.concatenate(hs, axis=2)

    @pl.when(jnp.logical_not(bwd))
    def _():
        hf_s[:, pl.ds(r0, CHUNK), :] = hcat

    @pl.when(bwd)
    def _():
        tot_h = hf_s[:, pl.ds(r0, CHUNK), :] + hcat
        normed = jnp.concatenate([_rms(tot_h[:, :, h * hd:(h + 1) * hd]) for h in range(nh)], axis=2) * ng_ref[...]
        y_ref[...] = normed * jax.nn.sigmoid(o_ref[...])

    @pl.when(s == 2 * nc - 1)
    def _():
        cst_ref[:, 1] = c_s[...]
        nst_ref[:, 1] = n_s[...]
        mst_ref[:, 1] = m_s[...]


def _mlstm(p1, gt, conv_w, conv_b, ig_b, fg_b, norm_g, c0, n0, m0, *, L):
    nseq, nc = p1.shape[0], L // CHUNK
    zero_init = c0 is None
    G = _scan_group(nseq, 8 if zero_init else 4)
    cur, prev, nxt, out = _scan_specs(nc)
    cshape = (G, 2, ML_HEADS, ML_HEAD_DIM, ML_HEAD_DIM)
    nshape = (G, 2, ML_HEADS, ML_HEAD_DIM)
    states = [] if zero_init else [c0, n0, m0]
    gb = jnp.concatenate([ig_b.reshape(-1), fg_b.reshape(-1)])
    small = [conv_w, conv_b.reshape(1, -1), gb.reshape(1, -1), gb.reshape(-1, 1), norm_g.reshape(1, -1)]
    st = lambda shape: pl.BlockSpec(shape, lambda b, s: (b,) + (0,) * (len(shape) - 1))
    blk = lambda rows, width, index: pl.BlockSpec((G, rows, width), index)
    return pl.pallas_call(
        functools.partial(_mlstm_kernel, nc=nc, zero_init=zero_init),
        grid=(nseq // G, 2 * nc),
        in_specs=[blk(8, 2048, prev(0)), blk(CHUNK, 2048, cur(0)), blk(8, 2048, nxt(0)),
                  blk(CHUNK, 1024, cur(3)), blk(CHUNK, 1024, cur(4)), blk(CHUNK, LANES, cur(44)),
                  pl.BlockSpec((G, 1, 32, CHUNK), lambda b, s: (b, _chunk_index(s, nc), 0, 0))]
        + [_full(v.shape) for v in small] + ([] if zero_init else [st(cshape), st(nshape), st(nshape)]),
        out_specs=[blk(CHUNK, ML_D, out(0)), st(cshape), st(nshape), st(nshape)],
        out_shape=[jax.ShapeDtypeStruct((nseq, L, ML_D), F32), jax.ShapeDtypeStruct((nseq,) + cshape[1:], F32),
                   jax.ShapeDtypeStruct((nseq,) + nshape[1:], F32), jax.ShapeDtypeStruct((nseq,) + nshape[1:], F32)],
        scratch_shapes=[pltpu.VMEM((G, L, ML_D), F32), pltpu.VMEM(cshape[:1] + cshape[2:], F32),
                        pltpu.VMEM(nshape[:1] + nshape[2:], F32), pltpu.VMEM(nshape[:1] + nshape[2:], F32)],
        compiler_params=_cp(2), name="mlstm_mixer",
    )(p1, p1, p1, p1, p1, p1, gt, *small, *states)


def _prep_layer0(w_in, w_qb, w_kvb, w_out):
    pad = jnp.zeros((D_MODEL, L0_COLS - 3680), F32)
    w = jnp.concatenate([w_in[:, :2560], w_in[:, 2592:3680], w_in[:, 2560:2592], pad], axis=1).astype(BF16)
    q3 = w_qb.reshape(MLA_RANK, MLA_HEADS, MLA_NOPE + MLA_ROPE)
    wqn = q3[:, :, :MLA_NOPE].reshape(MLA_RANK, -1).astype(BF16)
    wqr = jnp.concatenate([q3[:, :, MLA_NOPE:], jnp.zeros((MLA_RANK, MLA_HEADS, LANES - MLA_ROPE), F32)], axis=-1)
    wqr = wqr.reshape(MLA_RANK, -1).astype(BF16)
    k3 = w_kvb.reshape(MLA_RANK, MLA_HEADS, MLA_NOPE + MLA_V)
    wkk = k3[:, :, :MLA_NOPE].reshape(MLA_RANK, -1).astype(BF16)
    wkv = k3[:, :, MLA_NOPE:].reshape(MLA_RANK, -1).astype(BF16)
    return w, wqn, wqr, wkk, wkv, w_out.astype(BF16)


def _prep_layer1(w_in, w_out):
    pad = jnp.zeros((D_MODEL, L1_COLS - 5664), F32)
    w = jnp.concatenate([w_in[:, 1536:3584], w_in[:, 0:1024], w_in[:, 3584:5632], w_in[:, 1024:1536],
                         w_in[:, 5632:5664], pad], axis=1).astype(BF16)
    return w, w_out.astype(BF16)


def _prep_peer(wq, keys, u, v):
    return (wq.T.astype(BF16), keys.reshape(PEER_HEADS * 2, PEER_NKEYS, -1).astype(BF16), u.astype(BF16),
            v.T.astype(BF16))


def _chunk_transposed(cols, L):
    return cols.reshape(-1, L // CHUNK, CHUNK, cols.shape[1]).transpose(0, 1, 3, 2)


def _trunk(x, L, mod0, mod1, w0, w1, peer0, peer1, p, ctx, rope_tabs, mod_base, rows_per_cond, fg):
    kw = dict(mod_base=mod_base, rows_per_cond=rows_per_cond)
    w_in0, wqn, wqr, wkk, wkv, w_out0 = w0
    w_in1, w_out1 = w1
    nseq = x.shape[0] // L

    p0 = _proj(x, p['l0_norm1_g'], mod0, w_in0, tn=1280, **kw)
    cw, cb = p['l0_ssd_conv_w'], p['l0_ssd_conv_b']
    ssd = _ssd if nseq >= 8 else _ssd_bidir
    y_ssd, ssd_state = ssd(p0.reshape(nseq, L, -1), _chunk_transposed(p0[:, 3648:3680], L), cw[:, :1024], cb[:1024],
                           cw[:, 1024:], cb[1024:], p['l0_ssd_A_log'], p['l0_ssd_dt_bias'], p['l0_ssd_D'],
                           p['l0_ssd_norm_g'], None if ctx is None else ctx[0], L=L)
    y_ssd = y_ssd.reshape(nseq * L, -1)
    mla_tabs = None
    if rope_tabs is not None:
        z64 = jnp.zeros_like(rope_tabs[0])
        mla_tabs = (jnp.concatenate([rope_tabs[0], z64], axis=1), jnp.concatenate([rope_tabs[1], z64], axis=1))
    mla_out = _mla(p0, p['l0_mla_q_norm_g'], p['l0_mla_kv_norm_g'], wqn, wqr, wkk, wkv,
                   None if ctx is None else ctx[1], None if ctx is None else ctx[2], mla_tabs, L=L)
    x = _outproj(y_ssd, mla_out[0], w_out0, x, mod0, **kw)
    x = _peer(x, p['l0_norm2_g'], mod0, *peer0, fg, final_norm=False, **kw)

    p1 = _proj(x, p['l1_norm1_g'], mod1, w_in1, tn=1152, **kw)
    if ctx is None:
        y_gqa = _gqa_ctx(p1, p['l1_gqa_sink'], L=L)
        c0 = n0 = m0 = None
    else:
        gqa_tabs = tuple(jnp.concatenate([t, t], axis=1) for t in rope_tabs)
        y_gqa = _gqa_win(p1, ctx[3].reshape(nseq, -1, 256), ctx[4].reshape(nseq, -1, 256), *gqa_tabs,
                         p['l1_gqa_sink'], L=L)
        c0, n0 = ctx[5], ctx[6]
        m0 = jnp.broadcast_to(ctx[7][..., None], ctx[7].shape + (ML_HEAD_DIM,))
    y_ml, mc, mn, mm = _mlstm(p1.reshape(nseq, L, -1), _chunk_transposed(p1[:, 5632:5664], L), p['l1_ml_conv_w'],
                              p['l1_ml_conv_b'], p['l1_ml_ig_b'], p['l1_ml_fg_b'], p['l1_ml_norm_g'], c0, n0, m0, L=L)
    y_ml = y_ml.reshape(nseq * L, -1)
    x = _outproj(y_gqa, y_ml, w_out1, x, mod1, **kw)
    y = _peer(x, p['l1_norm2_g'], mod1, *peer1, fg, final_norm=True, **kw)
    new = None
    if ctx is None:
        new = (ssd_state, mla_out[1].reshape(nseq, L, MLA_RANK), p0[:, 3584:3648].reshape(nseq, L, MLA_ROPE),
               p1[:, 5120:5376].reshape(nseq, L, GQA_KV_HEADS, GQA_HEAD_DIM),
               p1[:, 5376:5632].reshape(nseq, L, GQA_KV_HEADS, GQA_HEAD_DIM), mc, mn, mm[..., 0])
    return y, new


def kernel(x_prompt, x_sample, state_l0_ssd, cache_l0_mla_ckv, cache_l0_mla_kpe, cache_l1_gqa_k, cache_l1_gqa_v, state_l1_mlstm_C, state_l1_mlstm_n, state_l1_mlstm_m, c, c_ctx, final_norm_g, l0_ada_w, l0_ada_b, l0_norm1_g, l0_norm2_g, l0_w_in, l0_ssd_conv_w, l0_ssd_conv_b, l0_ssd_A_log, l0_ssd_dt_bias, l0_ssd_D, l0_ssd_norm_g, l0_mla_q_norm_g, l0_mla_w_qb, l0_mla_kv_norm_g, l0_mla_w_kvb, l0_w_out, l0_peer_wq, l0_peer_keys, l0_peer_u, l0_peer_v, l1_ada_w, l1_ada_b, l1_norm1_g, l1_norm2_g, l1_w_in, l1_gqa_sink, l1_ml_conv_w, l1_ml_conv_b, l1_ml_ig_b, l1_ml_fg_b, l1_ml_norm_g, l1_w_out, l1_peer_wq, l1_peer_keys, l1_peer_u, l1_peer_v):
    p = dict(l0_norm1_g=l0_norm1_g, l0_norm2_g=l0_norm2_g, l0_ssd_conv_w=l0_ssd_conv_w, l0_ssd_conv_b=l0_ssd_conv_b,
             l0_ssd_A_log=l0_ssd_A_log, l0_ssd_dt_bias=l0_ssd_dt_bias, l0_ssd_D=l0_ssd_D, l0_ssd_norm_g=l0_ssd_norm_g,
             l0_mla_q_norm_g=l0_mla_q_norm_g, l0_mla_kv_norm_g=l0_mla_kv_norm_g, l1_norm1_g=l1_norm1_g,
             l1_norm2_g=l1_norm2_g, l1_gqa_sink=l1_gqa_sink, l1_ml_conv_w=l1_ml_conv_w, l1_ml_conv_b=l1_ml_conv_b,
             l1_ml_ig_b=l1_ml_ig_b, l1_ml_fg_b=l1_ml_fg_b, l1_ml_norm_g=l1_ml_norm_g)
    nb, seq = x_prompt.shape[:2]
    db, dseq = x_sample.shape[:2]
    assert db <= 7 and seq % MLA_QB == 0 and dseq % MLA_QB == 0

    cond8 = jnp.zeros((8, D_MODEL), F32).at[0].set(c_ctx).at[1:1 + db].set(c)
    mod0 = _ada(cond8, l0_ada_w, l0_ada_b).reshape(8, 1, 6 * D_MODEL)
    mod1 = _ada(cond8, l1_ada_w, l1_ada_b).reshape(8, 1, 6 * D_MODEL)
    w0 = _prep_layer0(l0_w_in, l0_mla_w_qb, l0_mla_w_kvb, l0_w_out)
    w1 = _prep_layer1(l1_w_in, l1_w_out)
    peer0 = _prep_peer(l0_peer_wq, l0_peer_keys, l0_peer_u, l0_peer_v)
    peer1 = _prep_peer(l1_peer_wq, l1_peer_keys, l1_peer_u, l1_peer_v)

    y_prompt, new = _trunk(x_prompt.reshape(nb * seq, D_MODEL), seq, mod0, mod1, w0, w1, peer0, peer1, p, None, None,
                           0, nb * seq, final_norm_g)
    ctx = (state_l0_ssd, cache_l0_mla_ckv, cache_l0_mla_kpe, cache_l1_gqa_k, cache_l1_gqa_v, state_l1_mlstm_C,
           state_l1_mlstm_n, state_l1_mlstm_m)
    y_sample, _ = _trunk(x_sample.reshape(db * dseq, D_MODEL), dseq, mod0, mod1, w0, w1, peer0, peer1, p, ctx,
                         _rope_tables(dseq), 1, dseq, final_norm_g)
    return (y_prompt.reshape(nb, seq, D_MODEL), y_sample.reshape(db, dseq, D_MODEL)) + new
```
